```python
import math
import jax, jax.numpy as jnp
from jax import lax
import numpy as np

D_MODEL = 1024
BATCH = 8
SEQ = 4096
DEPTH = 1

HEAD_DIM = 64
N_HEADS_DIL = 8
N_HEADS_SB = 8
D_DIL = N_HEADS_DIL * HEAD_DIM
D_SB = N_HEADS_SB * HEAD_DIM
D_MIX = D_DIL + D_SB
D_IN = 3 * D_DIL + 3 * D_SB
D_FF = -(-(8 * D_MODEL) // (3 * 256)) * 256
DILATED_PAIRS = ((128, 1), (512, 4), (2048, 16))
BLOCK = 128
ROPE_THETA = 10000.0
EPS = 1e-6

kernel_name = "hymba_dilated_stickbreaking_block"


def _rmsnorm(x, w):
    xf = x.astype(jnp.float32)
    y = xf * lax.rsqrt(jnp.mean(xf * xf, axis=-1, keepdims=True) + EPS)
    wb = w.astype(jnp.float32).reshape((1,) * (x.ndim - 1) + (w.shape[-1],))
    return (y * wb).astype(x.dtype)


def _rope_tables(seq_len):
    pos = jnp.arange(seq_len, dtype=jnp.float32)
    inv_freq = ROPE_THETA ** (-jnp.arange(0, HEAD_DIM, 2, dtype=jnp.float32) / HEAD_DIM)
    ang = pos[:, None] * inv_freq[None, :]
    return jnp.cos(ang)[None, None], jnp.sin(ang)[None, None]


def _apply_rope(x, cos, sin):
    xf = x.astype(jnp.float32)
    half = HEAD_DIM // 2
    x1, x2 = xf[..., :half], xf[..., half:]
    out = jnp.concatenate([x1 * cos - x2 * sin, x2 * cos + x1 * sin], axis=-1)
    return out.astype(x.dtype)


def _dilated_branch(q, k, v, window, dilation):
    B, H, S, Dh = q.shape
    r = dilation
    n_back = window // dilation
    L = S // r
    nb = -(-L // BLOCK)
    Lp = nb * BLOCK

    def strided(t):
        t = t.reshape(B, H, L, r, Dh).transpose(0, 1, 3, 2, 4)
        t = jnp.pad(t, ((0, 0), (0, 0), (0, 0), (0, Lp - L), (0, 0)))
        return t.reshape(B, H, r, nb, BLOCK, Dh)

    qb, kb, vb = strided(q), strided(k), strided(v)

    def with_prev(t):
        prev = jnp.pad(t, ((0, 0), (0, 0), (0, 0), (1, 0), (0, 0), (0, 0)))[:, :, :, :-1]
        return jnp.concatenate([prev, t], axis=-2)

    kk, vv = with_prev(kb), with_prev(vb)
    s = jnp.einsum('bhrnqd,bhrnkd->bhrnqk', qb, kk,
                   preferred_element_type=jnp.float32) * (Dh ** -0.5)
    i = jnp.arange(BLOCK)[:, None]
    j = jnp.arange(2 * BLOCK)[None, :]
    dist = i + BLOCK - j
    key_idx = (jnp.arange(nb)[:, None, None] - 1) * BLOCK + j[None]
    valid = (dist >= 0)[None] & (dist <= n_back)[None] & (key_idx >= 0)
    s = jnp.where(valid[None, None, None], s, -jnp.inf)
    m = jnp.max(s, axis=-1, keepdims=True)
    p = jnp.exp(s - m)
    den = jnp.sum(p, axis=-1, keepdims=True)
    o = jnp.einsum('bhrnqk,bhrnkd->bhrnqd', p.astype(v.dtype), vv,
                   preferred_element_type=jnp.float32) / den
    lse = (m + jnp.log(den))[..., 0]

    o = o.reshape(B, H, r, Lp, Dh)[:, :, :, :L].transpose(0, 1, 3, 2, 4).reshape(B, H, S, Dh)
    lse = lse.reshape(B, H, r, Lp)[:, :, :, :L].transpose(0, 1, 3, 2).reshape(B, H, S)
    return o, lse


def _dilated_attention(q, k, v):
    outs, lses = [], []
    for window, dilation in DILATED_PAIRS:
        o, lse = _dilated_branch(q, k, v, window, dilation)
        outs.append(o)
        lses.append(lse)
    w = jax.nn.softmax(jnp.stack(lses, axis=0), axis=0)
    o = jnp.sum(w[..., None] * jnp.stack(outs, axis=0), axis=0)
    return o.astype(q.dtype)


def _stick_breaking(q, k, v):
    B, H, S, Dh = q.shape
    nb = S // BLOCK
    scale = Dh ** -0.5
    qblocks = q.reshape(B, H, nb, BLOCK, Dh).transpose(2, 0, 1, 3, 4)
    kpos = jnp.arange(S)

    def one_block(args):
        qblk, bidx = args
        z = jnp.einsum('bhqd,bhkd->bhqk', qblk, k,
                       preferred_element_type=jnp.float32) * scale
        qpos = bidx * BLOCK + jnp.arange(BLOCK)
        causal = (kpos[None, :] < qpos[:, None])[None, None]
        log_beta = jax.nn.log_sigmoid(z)
        log_keep = jnp.where(causal, jax.nn.log_sigmoid(-z), 0.0)
        suffix = lax.cumsum(log_keep, axis=3, reverse=True) - log_keep
        a = jnp.where(causal, jnp.exp(log_beta + suffix), 0.0)
        return jnp.einsum('bhqk,bhkd->bhqd', a.astype(v.dtype), v,
                          preferred_element_type=jnp.float32).astype(v.dtype)

    out = lax.map(one_block, (qblocks, jnp.arange(nb)))
    return out.transpose(1, 2, 0, 3, 4).reshape(B, H, S, Dh)


def _fwd_setup_inputs(seed: int = 0) -> dict:
    key = jax.random.key(seed)
    ks = jax.random.split(key, 12)
    f32 = jnp.float32

    def gain(k, n):
        return (1.0 + 0.02 * jax.random.normal(k, (DEPTH, n))).astype(f32)

    return {
        "x": jax.random.normal(ks[0], (BATCH, SEQ, D_MODEL), f32),
        "attn_norm_w": gain(ks[1], D_MODEL),
        "w_in": jax.random.normal(ks[2], (DEPTH, D_MODEL, D_IN), f32) * D_MODEL ** -0.5,
        "q_norm_w": gain(ks[3], HEAD_DIM),
        "k_norm_w": gain(ks[4], HEAD_DIM),
        "dil_out_norm_w": gain(ks[5], D_DIL),
        "sb_out_norm_w": gain(ks[6], D_SB),
        "w_out": jax.random.normal(ks[7], (DEPTH, D_MIX, D_MODEL), f32) * D_MIX ** -0.5,
        "ffn_norm_w": gain(ks[8], D_MODEL),
        "w_gate": jax.random.normal(ks[9], (DEPTH, D_MODEL, D_FF), f32) * D_MODEL ** -0.5,
        "w_up": jax.random.normal(ks[10], (DEPTH, D_MODEL, D_FF), f32) * D_MODEL ** -0.5,
        "w_down": jax.random.normal(ks[11], (DEPTH, D_FF, D_MODEL), f32) * D_FF ** -0.5,
    }


def _fwd_reference(x, attn_norm_w, w_in, q_norm_w, k_norm_w, dil_out_norm_w, sb_out_norm_w,
              w_out, ffn_norm_w, w_gate, w_up, w_down):
    B, S, _ = x.shape
    cos, sin = _rope_tables(S)

    def heads(t, n):
        return t.reshape(B, S, n, HEAD_DIM).transpose(0, 2, 1, 3)

    def merge(t):
        return t.transpose(0, 2, 1, 3).reshape(B, S, t.shape[1] * t.shape[3])

    for l in range(DEPTH):
        h = _rmsnorm(x, attn_norm_w[l])
        proj = jnp.einsum('bsd,de->bse', h, w_in[l])
        qa = proj[..., 0:D_DIL]
        ka = proj[..., D_DIL:2 * D_DIL]
        va = proj[..., 2 * D_DIL:3 * D_DIL]
        o0 = 3 * D_DIL
        qs = proj[..., o0:o0 + D_SB]
        ksb = proj[..., o0 + D_SB:o0 + 2 * D_SB]
        vs = proj[..., o0 + 2 * D_SB:o0 + 3 * D_SB]

        qa = _apply_rope(_rmsnorm(heads(qa, N_HEADS_DIL), q_norm_w[l]), cos, sin)
        ka = _apply_rope(_rmsnorm(heads(ka, N_HEADS_DIL), k_norm_w[l]), cos, sin)
        o_dil = _dilated_attention(qa, ka, heads(va, N_HEADS_DIL))

        o_sb = _stick_breaking(heads(qs, N_HEADS_SB), heads(ksb, N_HEADS_SB),
                               heads(vs, N_HEADS_SB))

        mixed = jnp.concatenate([_rmsnorm(merge(o_dil), dil_out_norm_w[l]),
                                 _rmsnorm(merge(o_sb), sb_out_norm_w[l])], axis=-1)
        x = x + jnp.einsum('bse,ed->bsd', mixed, w_out[l])

        h = _rmsnorm(x, ffn_norm_w[l])
        g = jnp.einsum('bsd,df->bsf', h, w_gate[l])
        u = jnp.einsum('bsd,df->bsf', h, w_up[l])
        x = x + jnp.einsum('bsf,fd->bsd', jax.nn.silu(g) * u, w_down[l])
    return x


import jax as _jax
import jax.numpy as _jnp

TWIN_FORMAT = 'train_step'
FWD_PARAMS = ['x', 'attn_norm_w', 'w_in', 'q_norm_w', 'k_norm_w', 'dil_out_norm_w', 'sb_out_norm_w', 'w_out', 'ffn_norm_w', 'w_gate', 'w_up', 'w_down']
TWIN_WEIGHTS = ['attn_norm_w', 'w_in', 'q_norm_w', 'k_norm_w', 'dil_out_norm_w', 'sb_out_norm_w', 'w_out', 'ffn_norm_w', 'w_gate', 'w_up', 'w_down']
TWIN_DIFF_INPUT = 'x'
TWIN_INPUTS = ['x', 'attn_norm_w', 'w_in', 'q_norm_w', 'k_norm_w', 'dil_out_norm_w', 'sb_out_norm_w', 'w_out', 'ffn_norm_w', 'w_gate', 'w_up', 'w_down', 'loss_target', 'm_attn_norm_w', 'm_w_in', 'm_q_norm_w', 'm_k_norm_w', 'm_dil_out_norm_w', 'm_sb_out_norm_w', 'm_w_out', 'm_ffn_norm_w', 'm_w_gate', 'm_w_up', 'm_w_down', 'v_attn_norm_w', 'v_w_in', 'v_q_norm_w', 'v_k_norm_w', 'v_dil_out_norm_w', 'v_sb_out_norm_w', 'v_w_out', 'v_ffn_norm_w', 'v_w_gate', 'v_w_up', 'v_w_down']
TWIN_OUTPUTS = ['loss', 'grad_x', 'grad_attn_norm_w', 'grad_w_in', 'grad_q_norm_w', 'grad_k_norm_w', 'grad_dil_out_norm_w', 'grad_sb_out_norm_w', 'grad_w_out', 'grad_ffn_norm_w', 'grad_w_gate', 'grad_w_up', 'grad_w_down', 'delta_attn_norm_w', 'delta_w_in', 'delta_q_norm_w', 'delta_k_norm_w', 'delta_dil_out_norm_w', 'delta_sb_out_norm_w', 'delta_w_out', 'delta_ffn_norm_w', 'delta_w_gate', 'delta_w_up', 'delta_w_down', 'new_m_attn_norm_w', 'new_m_w_in', 'new_m_q_norm_w', 'new_m_k_norm_w', 'new_m_dil_out_norm_w', 'new_m_sb_out_norm_w', 'new_m_w_out', 'new_m_ffn_norm_w', 'new_m_w_gate', 'new_m_w_up', 'new_m_w_down', 'new_v_attn_norm_w', 'new_v_w_in', 'new_v_q_norm_w', 'new_v_k_norm_w', 'new_v_dil_out_norm_w', 'new_v_sb_out_norm_w', 'new_v_w_out', 'new_v_ffn_norm_w', 'new_v_w_gate', 'new_v_w_up', 'new_v_w_down']
TWIN_LEAF_KINDS = {'loss': 'loss', 'grad_x': 'grad_x', 'grad_attn_norm_w': 'grad_w', 'grad_w_in': 'grad_w', 'grad_q_norm_w': 'grad_w', 'grad_k_norm_w': 'grad_w', 'grad_dil_out_norm_w': 'grad_w', 'grad_sb_out_norm_w': 'grad_w', 'grad_w_out': 'grad_w', 'grad_ffn_norm_w': 'grad_w', 'grad_w_gate': 'grad_w', 'grad_w_up': 'grad_w', 'grad_w_down': 'grad_w', 'delta_attn_norm_w': 'delta_w', 'delta_w_in': 'delta_w', 'delta_q_norm_w': 'delta_w', 'delta_k_norm_w': 'delta_w', 'delta_dil_out_norm_w': 'delta_w', 'delta_sb_out_norm_w': 'delta_w', 'delta_w_out': 'delta_w', 'delta_ffn_norm_w': 'delta_w', 'delta_w_gate': 'delta_w', 'delta_w_up': 'delta_w', 'delta_w_down': 'delta_w', 'new_m_attn_norm_w': 'new_m', 'new_m_w_in': 'new_m', 'new_m_q_norm_w': 'new_m', 'new_m_k_norm_w': 'new_m', 'new_m_dil_out_norm_w': 'new_m', 'new_m_sb_out_norm_w': 'new_m', 'new_m_w_out': 'new_m', 'new_m_ffn_norm_w': 'new_m', 'new_m_w_gate': 'new_m', 'new_m_w_up': 'new_m', 'new_m_w_down': 'new_m', 'new_v_attn_norm_w': 'new_v', 'new_v_w_in': 'new_v', 'new_v_q_norm_w': 'new_v', 'new_v_k_norm_w': 'new_v', 'new_v_dil_out_norm_w': 'new_v', 'new_v_sb_out_norm_w': 'new_v', 'new_v_w_out': 'new_v', 'new_v_ffn_norm_w': 'new_v', 'new_v_w_gate': 'new_v', 'new_v_w_up': 'new_v', 'new_v_w_down': 'new_v'}


def _forward(args):
    return _fwd_reference(*[args[k] for k in FWD_PARAMS])


def _output_shape():
    out = _jax.eval_shape(lambda: _forward(_fwd_setup_inputs(0)))
    return out.shape, out.dtype

N_MICROBATCH = 1
ADAM_LR = 0.001
ADAM_B1 = 0.9
ADAM_B2 = 0.999
ADAM_EPS = 1e-08
ADAM_WD = 0.01
ADAM_STEP = 10
PER_EXAMPLE_BATCH_AXIS = {'x': 0, 'loss_target': 0}
SHARED_INPUTS = []
_WEIGHT_DTYPES = {'attn_norm_w': _jnp.float32, 'w_in': _jnp.float32, 'q_norm_w': _jnp.float32, 'k_norm_w': _jnp.float32, 'dil_out_norm_w': _jnp.float32, 'sb_out_norm_w': _jnp.float32, 'w_out': _jnp.float32, 'ffn_norm_w': _jnp.float32, 'w_gate': _jnp.float32, 'w_up': _jnp.float32, 'w_down': _jnp.float32}
MOMENT_SCALE = {'attn_norm_w': 7.669080e-01, 'w_in': 4.352435e-01, 'q_norm_w': 1.280347e+00, 'k_norm_w': 1.294264e+00, 'dil_out_norm_w': 3.187637e+01, 'sb_out_norm_w': 3.190036e+01, 'w_out': 8.731832e-01, 'ffn_norm_w': 2.471549e+01, 'w_gate': 1.992660e-01, 'w_up': 2.026349e-01, 'w_down': 3.103842e-01}


def _to_microbatches(a, axis):
    t = _jnp.moveaxis(a, axis, 0)
    t = t.reshape((N_MICROBATCH, t.shape[0] // N_MICROBATCH) + t.shape[1:])
    return _jnp.moveaxis(t, 1, axis + 1)


def setup_inputs(seed: int = 0) -> dict:
    inp = _fwd_setup_inputs(seed)
    key = _jax.random.fold_in(_jax.random.key(seed), 7919)
    shape, _ = _output_shape()
    out = dict(inp)
    out["loss_target"] = _jax.random.normal(_jax.random.fold_in(key, 0), shape, _jnp.float32)
    for i, name in enumerate(TWIN_WEIGHTS):
        w = inp[name].astype(_jnp.float32)
        if MOMENT_SCALE is None:
            s = _jnp.sqrt(_jnp.mean(_jnp.square(w)) + 1e-30)
        else:
            s = MOMENT_SCALE[name]
        km, kv = _jax.random.split(_jax.random.fold_in(key, i + 1))
        out[name] = w
        out["m_" + name] = s * _jax.random.normal(km, w.shape, _jnp.float32)
        out["v_" + name] = (s * s) * _jax.random.uniform(kv, w.shape, _jnp.float32, 0.5, 1.5)
    if N_MICROBATCH > 1:
        for name, axis in PER_EXAMPLE_BATCH_AXIS.items():
            out[name] = _to_microbatches(out[name], axis)
    return {'x': out['x'], 'attn_norm_w': out['attn_norm_w'], 'w_in': out['w_in'], 'q_norm_w': out['q_norm_w'], 'k_norm_w': out['k_norm_w'], 'dil_out_norm_w': out['dil_out_norm_w'], 'sb_out_norm_w': out['sb_out_norm_w'], 'w_out': out['w_out'], 'ffn_norm_w': out['ffn_norm_w'], 'w_gate': out['w_gate'], 'w_up': out['w_up'], 'w_down': out['w_down'], 'loss_target': out['loss_target'], 'm_attn_norm_w': out['m_attn_norm_w'], 'm_w_in': out['m_w_in'], 'm_q_norm_w': out['m_q_norm_w'], 'm_k_norm_w': out['m_k_norm_w'], 'm_dil_out_norm_w': out['m_dil_out_norm_w'], 'm_sb_out_norm_w': out['m_sb_out_norm_w'], 'm_w_out': out['m_w_out'], 'm_ffn_norm_w': out['m_ffn_norm_w'], 'm_w_gate': out['m_w_gate'], 'm_w_up': out['m_w_up'], 'm_w_down': out['m_w_down'], 'v_attn_norm_w': out['v_attn_norm_w'], 'v_w_in': out['v_w_in'], 'v_q_norm_w': out['v_q_norm_w'], 'v_k_norm_w': out['v_k_norm_w'], 'v_dil_out_norm_w': out['v_dil_out_norm_w'], 'v_sb_out_norm_w': out['v_sb_out_norm_w'], 'v_w_out': out['v_w_out'], 'v_ffn_norm_w': out['v_ffn_norm_w'], 'v_w_gate': out['v_w_gate'], 'v_w_up': out['v_w_up'], 'v_w_down': out['v_w_down']}


def _loss(weights, diff, rest, loss_target):
    with _jax.named_scope("forward"):
        args = {**rest, TWIN_DIFF_INPUT: diff, **{k: w.astype(_WEIGHT_DTYPES[k]) for k, w in weights.items()}}
        y = _forward(args)
    with _jax.named_scope("loss_head"):
        err = _jnp.square(y.astype(_jnp.float32) - loss_target)
        return 0.5 * _jnp.sum(_jnp.mean(err, axis=-1)) if err.ndim else 0.5 * err


def _adamw(w, g, m, v):
    m = ADAM_B1 * m + (1.0 - ADAM_B1) * g
    v = ADAM_B2 * v + (1.0 - ADAM_B2) * _jnp.square(g)
    m_hat = m / (1.0 - ADAM_B1 ** ADAM_STEP)
    v_hat = v / (1.0 - ADAM_B2 ** ADAM_STEP)
    delta = -ADAM_LR * (m_hat / (_jnp.sqrt(v_hat) + ADAM_EPS) + ADAM_WD * w)
    return delta, m, v


def reference(x, attn_norm_w, w_in, q_norm_w, k_norm_w, dil_out_norm_w, sb_out_norm_w, w_out, ffn_norm_w, w_gate, w_up, w_down, loss_target, m_attn_norm_w, m_w_in, m_q_norm_w, m_k_norm_w, m_dil_out_norm_w, m_sb_out_norm_w, m_w_out, m_ffn_norm_w, m_w_gate, m_w_up, m_w_down, v_attn_norm_w, v_w_in, v_q_norm_w, v_k_norm_w, v_dil_out_norm_w, v_sb_out_norm_w, v_w_out, v_ffn_norm_w, v_w_gate, v_w_up, v_w_down):
    given = dict(x=x, attn_norm_w=attn_norm_w, w_in=w_in, q_norm_w=q_norm_w, k_norm_w=k_norm_w, dil_out_norm_w=dil_out_norm_w, sb_out_norm_w=sb_out_norm_w, w_out=w_out, ffn_norm_w=ffn_norm_w, w_gate=w_gate, w_up=w_up, w_down=w_down, loss_target=loss_target, m_attn_norm_w=m_attn_norm_w, m_w_in=m_w_in, m_q_norm_w=m_q_norm_w, m_k_norm_w=m_k_norm_w, m_dil_out_norm_w=m_dil_out_norm_w, m_sb_out_norm_w=m_sb_out_norm_w, m_w_out=m_w_out, m_ffn_norm_w=m_ffn_norm_w, m_w_gate=m_w_gate, m_w_up=m_w_up, m_w_down=m_w_down, v_attn_norm_w=v_attn_norm_w, v_w_in=v_w_in, v_q_norm_w=v_q_norm_w, v_k_norm_w=v_k_norm_w, v_dil_out_norm_w=v_dil_out_norm_w, v_sb_out_norm_w=v_sb_out_norm_w, v_w_out=v_w_out, v_ffn_norm_w=v_ffn_norm_w, v_w_gate=v_w_gate, v_w_up=v_w_up, v_w_down=v_w_down)
    weights = {n: given[n] for n in TWIN_WEIGHTS}
    shared = {n: given[n] for n in SHARED_INPUTS}
    per_example = {n: given[n] for n in ['x']}
    grad_fn = _jax.value_and_grad(_loss, argnums=(0, 1))

    def one_microbatch(ex, loss_target):
        ex = dict(ex)
        diff = ex.pop(TWIN_DIFF_INPUT)
        return grad_fn(weights, diff, {**shared, **ex}, loss_target)

    if N_MICROBATCH == 1:
        loss, (grad_w, grad_x) = one_microbatch(per_example, given["loss_target"])
    else:
        def body(carry, xs):
            loss_sum, grad_sum = carry
            l_k, (gw_k, gx_k) = one_microbatch(xs[0], xs[1])
            with _jax.named_scope("update"):
                return (loss_sum + l_k, _jax.tree.map(_jnp.add, grad_sum, gw_k)), gx_k

        init = (_jnp.zeros((), _jnp.float32), _jax.tree.map(_jnp.zeros_like, weights))
        (loss, grad_w), grad_x = _jax.lax.scan(body, init, (per_example, given["loss_target"]))
    with _jax.named_scope("update"):
        delta_w, new_m, new_v = {}, {}, {}
        for n in TWIN_WEIGHTS:
            delta_w[n], new_m[n], new_v[n] = _adamw(weights[n], grad_w[n], given["m_" + n], given["v_" + n])
    return (loss, grad_x, *[grad_w[n] for n in TWIN_WEIGHTS], *[delta_w[n] for n in TWIN_WEIGHTS],
            *[new_m[n] for n in TWIN_WEIGHTS], *[new_v[n] for n in TWIN_WEIGHTS])
```

```python
import functools

import jax
import jax.numpy as jnp
from jax import lax
from jax.experimental import pallas as pl
from jax.experimental.pallas import tpu as pltpu

F32 = jnp.float32
BF16 = jnp.bfloat16
MESH = pl.DeviceIdType.MESH

HEAD_DIM = 64
GROUP = 512
BLOCK = 128
LANES = 128
N_CHIPS = 4
N_DEV = 8
EPS = 1e-6
ROPE_THETA = 10000.0
DILATIONS = (1, 4, 16)
NEG = -1e30

ADAM_LR = 0.001
ADAM_B1 = 0.9
ADAM_B2 = 0.999
ADAM_EPS = 1e-08
ADAM_WD = 0.01
ADAM_STEP = 10


def _dot(a, b):
    return jnp.dot(a, b, preferred_element_type=F32)


def _dot_nt(a, b):
    return lax.dot_general(a, b, (((1,), (1,)), ((), ())), preferred_element_type=F32)


def _dot_tn(a, b):
    return lax.dot_general(a, b, (((0,), (0,)), ((), ())), preferred_element_type=F32)


def _split(v):
    hi = v.astype(BF16)
    lo = (v - hi.astype(F32)).astype(BF16)
    return hi, lo


def _segsum(v, g):
    hi, lo = _split(v)
    return _dot(hi, g) + _dot(lo, g)


def _rot_half(x):
    outs = []
    for c in range(x.shape[1] // LANES):
        xc = x[:, c * LANES:(c + 1) * LANES]
        lane = lax.broadcasted_iota(jnp.int32, xc.shape, 1)
        first = (lane % HEAD_DIM) < (HEAD_DIM // 2)
        outs.append(jnp.where(first, pltpu.roll(xc, LANES - 32, 1), pltpu.roll(xc, 32, 1)))
    return outs[0] if len(outs) == 1 else jnp.concatenate(outs, axis=1)


def _rms(x):
    return lax.rsqrt(jnp.mean(x * x, axis=-1, keepdims=True) + EPS)


def _rms_bwd(dy, x, w):
    rstd = _rms(x)
    xh = x * rstd
    dxh = dy * w
    dx = rstd * (dxh - xh * jnp.mean(dxh * xh, axis=-1, keepdims=True))
    return dx, dy * xh


def _sigmoid(x):
    return 1.0 / (1.0 + jnp.exp(-x))


def _full(shape):
    n = len(shape)
    return pl.BlockSpec(shape, lambda *_: (0,) * n)


def _in_proj_fwd(x, attn_w, w_in_g, qw, kw, cos_t, sin_t, seg_ones):
    S, D = x.shape
    tm = 512
    wcols = w_in_g.shape[2]

    def body(x_ref, aw_ref, w_ref, qw_ref, kw_ref, cos_ref, sin_ref, g_ref,
             h_ref, qa_ref, ka_ref, qh_ref, kh_ref, va_ref, qs_ref, ks_ref, vs_ref):
        xv = x_ref[...]
        h = (xv * _rms(xv) * aw_ref[...]).astype(BF16)
        h_ref[...] = h
        proj = jnp.concatenate([_dot(h, w_ref[j]) for j in range(N_CHIPS)], axis=1)
        qa = proj[:, 0 * GROUP:1 * GROUP]
        ka = proj[:, 1 * GROUP:2 * GROUP]
        qa_ref[...] = qa
        ka_ref[...] = ka
        va_ref[...] = proj[:, 2 * GROUP:3 * GROUP].astype(BF16)
        qs_ref[...] = proj[:, 3 * GROUP:4 * GROUP].astype(BF16)
        ks_ref[...] = proj[:, 4 * GROUP:5 * GROUP].astype(BF16)
        vs_ref[...] = proj[:, 5 * GROUP:6 * GROUP].astype(BF16)
        g = g_ref[...]
        cos = cos_ref[...]
        sin = sin_ref[...]
        for t, w_r, o_r in ((qa, qw_ref, qh_ref), (ka, kw_ref, kh_ref)):
            rstd = lax.rsqrt(_segsum(t * t, g) * (1.0 / HEAD_DIM) + EPS)
            tn = t * rstd * w_r[...]
            o_r[...] = (tn * cos + _rot_half(tn) * sin).astype(BF16)

    row = lambda i: (i, 0)
    tile = lambda n, dt: jax.ShapeDtypeStruct((S, n), dt)
    return pl.pallas_call(
        body, name="in_proj_fwd", grid=(S // tm,),
        in_specs=[pl.BlockSpec((tm, D), row), _full((1, D)), _full((N_CHIPS, D, wcols)),
                  _full((1, GROUP)), _full((1, GROUP)),
                  pl.BlockSpec((tm, GROUP), row), pl.BlockSpec((tm, GROUP), row),
                  _full((GROUP, GROUP))],
        out_specs=[pl.BlockSpec((tm, D), row)] + [pl.BlockSpec((tm, GROUP), row)] * 8,
        out_shape=[tile(D, BF16), tile(GROUP, F32), tile(GROUP, F32)] + [tile(GROUP, BF16)] * 6,
    )(x, attn_w, w_in_g, qw, kw, cos_t, sin_t, seg_ones)


def _dil_view(a, r):
    return a.reshape(a.shape[0] // r, r * a.shape[1])


def _dil_fwd(qh, kh, va, r):
    S = qh.shape[0]
    nb = S // r // BLOCK

    def body(q_ref, kc_ref, kp_ref, vc_ref, vp_ref, o_ref, l_ref):
        n = pl.program_id(1)
        rowi = lax.broadcasted_iota(jnp.int32, (BLOCK, BLOCK), 0)
        coli = lax.broadcasted_iota(jnp.int32, (BLOCK, BLOCK), 1)
        m_cur = coli <= rowi
        m_prev = jnp.logical_and(coli >= rowi, n > 0)
        for hp in range(GROUP // LANES):
            sl = slice(hp * LANES, (hp + 1) * LANES)
            q2 = q_ref[:, sl]
            kc, kp, vc, vp = kc_ref[:, sl], kp_ref[:, sl], vc_ref[:, sl], vp_ref[:, sl]
            o_acc = jnp.zeros((BLOCK, LANES), F32)
            l_acc = jnp.zeros((BLOCK, LANES), F32)
            for hh in range(2):
                hmask = (coli // HEAD_DIM) == hh
                qm = jnp.where(hmask, q2, jnp.zeros_like(q2))
                s_c = jnp.where(m_cur, _dot_nt(qm, kc) * 0.125, NEG)
                s_p = jnp.where(m_prev, _dot_nt(qm, kp) * 0.125, NEG)
                m = jnp.maximum(jnp.max(s_c, axis=1, keepdims=True), jnp.max(s_p, axis=1, keepdims=True))
                p_c = jnp.exp(s_c - m)
                p_p = jnp.exp(s_p - m)
                den = jnp.sum(p_c, axis=1, keepdims=True) + jnp.sum(p_p, axis=1, keepdims=True)
                o = (_dot(p_c.astype(BF16), vc) + _dot(p_p.astype(BF16), vp)) / den
                o_acc = jnp.where(hmask, o, o_acc)
                l_acc = jnp.where(hmask, m + jnp.log(den), l_acc)
            o_ref[:, sl] = o_acc
            l_ref[:, sl] = l_acc

    cur = pl.BlockSpec((BLOCK, GROUP), lambda c, n: (n, c))
    prev = pl.BlockSpec((BLOCK, GROUP), lambda c, n: (jnp.maximum(n - 1, 0), c))
    qv, kv, vv = _dil_view(qh, r), _dil_view(kh, r), _dil_view(va, r)
    o, l = pl.pallas_call(
        body, name="dil_fwd_r%d" % r, grid=(r, nb),
        in_specs=[cur, cur, prev, cur, prev], out_specs=[cur, cur],
        out_shape=[jax.ShapeDtypeStruct(qv.shape, F32)] * 2,
    )(qv, kv, kv, vv, vv)
    return o.reshape(S, GROUP), l.reshape(S, GROUP)


def _dil_bwd(qh, kh, va, do, lse, delta, r):
    S = qh.shape[0]
    nb = S // r // BLOCK

    def body(qc_ref, qn_ref, doc_ref, don_ref, lc_ref, ln_ref, dc_ref, dn_ref, k_ref, v_ref,
             dq_ref, dk_ref, dv_ref, carry):
        j = pl.program_id(1)
        rowi = lax.broadcasted_iota(jnp.int32, (BLOCK, BLOCK), 0)
        coli = lax.broadcasted_iota(jnp.int32, (BLOCK, BLOCK), 1)
        m_cur = coli <= rowi
        m_next = jnp.logical_and(coli >= rowi, j < nb - 1)

        @pl.when(j == 0)
        def _():
            carry[...] = jnp.zeros_like(carry)

        for hp in range(GROUP // LANES):
            sl = slice(hp * LANES, (hp + 1) * LANES)
            k2, v2 = k_ref[:, sl], v_ref[:, sl]
            dk_acc = jnp.zeros((BLOCK, LANES), F32)
            dv_acc = jnp.zeros((BLOCK, LANES), F32)
            dq_c = jnp.zeros((BLOCK, LANES), F32)
            dq_n = jnp.zeros((BLOCK, LANES), F32)
            for hh in range(2):
                hmask = (coli // HEAD_DIM) == hh
                col = hp * LANES + hh * HEAD_DIM
                parts = []
                for q_r, do_r, l_r, d_r, msk in ((qc_ref, doc_ref, lc_ref, dc_ref, m_cur),
                                                 (qn_ref, don_ref, ln_ref, dn_ref, m_next)):
                    q2 = q_r[:, sl]
                    do2 = do_r[:, sl]
                    qm = jnp.where(hmask, q2, jnp.zeros_like(q2))
                    dom = jnp.where(hmask, do2, jnp.zeros_like(do2))
                    s = _dot_nt(qm, k2) * 0.125
                    p = jnp.where(msk, jnp.exp(s - l_r[:, col:col + 1]), 0.0)
                    dp = _dot_nt(dom, v2)
                    ds = (p * (dp - d_r[:, col:col + 1]) * 0.125).astype(BF16)
                    dv_acc = dv_acc + _dot_tn(p.astype(BF16), dom)
                    dk_acc = dk_acc + _dot_tn(ds, qm)
                    parts.append(_dot(ds, k2))
                dq_c = jnp.where(hmask, parts[0], dq_c)
                dq_n = jnp.where(hmask, parts[1], dq_n)
            dq_ref[:, sl] = carry[:, sl] + dq_c
            carry[:, sl] = dq_n
            dk_ref[:, sl] = dk_acc
            dv_ref[:, sl] = dv_acc

    cur = pl.BlockSpec((BLOCK, GROUP), lambda c, n: (n, c))
    nxt = pl.BlockSpec((BLOCK, GROUP), lambda c, n: (jnp.minimum(n + 1, nb - 1), c))
    v = lambda a: _dil_view(a, r)
    outs = pl.pallas_call(
        body, name="dil_bwd_r%d" % r, grid=(r, nb),
        in_specs=[cur, nxt, cur, nxt, cur, nxt, cur, nxt, cur, cur], out_specs=[cur, cur, cur],
        out_shape=[jax.ShapeDtypeStruct(v(qh).shape, F32)] * 3,
        scratch_shapes=[pltpu.VMEM((BLOCK, GROUP), F32)],
    )(v(qh), v(qh), v(do), v(do), v(lse), v(lse), v(delta), v(delta), v(kh), v(va))
    return [o.reshape(S, GROUP) for o in outs]


def _sb_logits(qm, kj, valid):
    z = _dot_nt(qm, kj) * 0.125
    e = jnp.exp(-jnp.abs(z))
    sp = jnp.log(1.0 + e)
    lb = jnp.minimum(z, 0.0) - sp
    lk = -jnp.maximum(z, 0.0) - sp
    if valid is not None:
        lk = jnp.where(valid, lk, 0.0)
    return z, e, lb, lk


def _sb_fwd(qs, ks, vs, tri_excl):
    S = qs.shape[0]

    def body(q_ref, k_ref, v_ref, u_ref, o_ref, lt_ref):
        i = pl.program_id(1)
        rowi = lax.broadcasted_iota(jnp.int32, (BLOCK, BLOCK), 0)
        coli = lax.broadcasted_iota(jnp.int32, (BLOCK, BLOCK), 1)
        causal = coli < rowi
        q2 = q_ref[...]
        out = jnp.zeros((BLOCK, LANES), F32)
        ltot = jnp.zeros((BLOCK, LANES), F32)
        for hh in range(2):
            hmask = (coli // HEAD_DIM) == hh
            qm = jnp.where(hmask, q2, jnp.zeros_like(q2))

            def tile(j, run, acc, valid):
                off = pl.multiple_of(j * BLOCK, BLOCK)
                kj = k_ref[pl.ds(off, BLOCK), :]
                vj = v_ref[pl.ds(off, BLOCK), :]
                _, _, lb, lk = _sb_logits(qm, kj, valid)
                suffix = _dot(jnp.concatenate(_split(lk), axis=1), u_ref[...])
                a = jnp.exp(lb + suffix + run)
                if valid is not None:
                    a = jnp.where(valid, a, 0.0)
                acc = acc + _dot(a.astype(BF16), vj)
                return run + jnp.sum(lk, axis=1, keepdims=True), acc

            run, acc = tile(i, jnp.zeros((BLOCK, 1), F32), jnp.zeros((BLOCK, LANES), F32), causal)
            run, acc = lax.fori_loop(0, i, lambda t, c: tile(i - 1 - t, c[0], c[1], None), (run, acc))
            out = jnp.where(hmask, acc, out)
            ltot = jnp.where(hmask, run, ltot)
        o_ref[...] = out
        lt_ref[...] = ltot

    blk = pl.BlockSpec((BLOCK, LANES), lambda hp, i: (i, hp))
    col = pl.BlockSpec((S, LANES), lambda hp, i: (0, hp))
    return pl.pallas_call(
        body, name="sb_fwd", grid=(GROUP // LANES, S // BLOCK),
        in_specs=[blk, col, col, _full((2 * BLOCK, BLOCK))], out_specs=[blk, blk],
        out_shape=[jax.ShapeDtypeStruct((S, GROUP), F32)] * 2,
    )(qs, ks, vs, tri_excl)


def _sb_bwd(qs, ks, vs, do, ltot, tri_upto, tri_before):
    S = qs.shape[0]

    def body(q_ref, k_ref, v_ref, do_ref, lt_ref, w_ref, x_ref, dq_ref, dk_ref, dv_ref):
        i = pl.program_id(1)

        @pl.when(i == 0)
        def _():
            dk_ref[...] = jnp.zeros_like(dk_ref)
            dv_ref[...] = jnp.zeros_like(dv_ref)

        rowi = lax.broadcasted_iota(jnp.int32, (BLOCK, BLOCK), 0)
        coli = lax.broadcasted_iota(jnp.int32, (BLOCK, BLOCK), 1)
        causal = coli < rowi
        q2 = q_ref[...]
        do2 = do_ref[...]
        dq_out = jnp.zeros((BLOCK, LANES), F32)
        for hh in range(2):
            hmask = (coli // HEAD_DIM) == hh
            qm = jnp.where(hmask, q2, jnp.zeros_like(q2))
            dom = jnp.where(hmask, do2, jnp.zeros_like(do2))
            total = lt_ref[:, hh * HEAD_DIM:hh * HEAD_DIM + 1]

            def tile(j, keep, grad, dq, valid):
                off = pl.multiple_of(j * BLOCK, BLOCK)
                kj = k_ref[pl.ds(off, BLOCK), :]
                vj = v_ref[pl.ds(off, BLOCK), :]
                z, e, lb, lk = _sb_logits(qm, kj, valid)
                upto = _dot(jnp.concatenate(_split(lk), axis=1), w_ref[...])
                a = jnp.exp(lb + (total - (keep + upto)))
                if valid is not None:
                    a = jnp.where(valid, a, 0.0)
                de = a * _dot_nt(dom, vj)
                prefix = grad + _dot(jnp.concatenate(_split(de), axis=1), x_ref[...])
                sig = jnp.where(z >= 0.0, 1.0, e) / (1.0 + e)
                dz = (de * (1.0 - sig) - prefix * sig) * 0.125
                if valid is not None:
                    dz = jnp.where(valid, dz, 0.0)
                dzb = dz.astype(BF16)
                dk_ref[pl.ds(off, BLOCK), :] += _dot_tn(dzb, qm)
                dv_ref[pl.ds(off, BLOCK), :] += _dot_tn(a.astype(BF16), dom)
                return (keep + jnp.sum(lk, axis=1, keepdims=True),
                        grad + jnp.sum(de, axis=1, keepdims=True), dq + _dot(dzb, kj))

            zero = jnp.zeros((BLOCK, 1), F32)
            c = lax.fori_loop(0, i, lambda t, c: tile(t, c[0], c[1], c[2], None),
                              (zero, zero, jnp.zeros((BLOCK, LANES), F32)))
            c = tile(i, c[0], c[1], c[2], causal)
            dq_out = jnp.where(hmask, c[2], dq_out)
        dq_ref[...] = dq_out

    blk = pl.BlockSpec((BLOCK, LANES), lambda hp, i: (i, hp))
    col = pl.BlockSpec((S, LANES), lambda hp, i: (0, hp))
    tri = _full((2 * BLOCK, BLOCK))
    return pl.pallas_call(
        body, name="sb_bwd", grid=(GROUP // LANES, S // BLOCK),
        in_specs=[blk, col, col, blk, blk, tri, tri], out_specs=[blk, col, col],
        out_shape=[jax.ShapeDtypeStruct((S, GROUP), F32)] * 3,
    )(qs, ks, vs, do, ltot, tri_upto, tri_before)


def _out_proj_fwd(o_br, l_br, o_sb, x, w_dil, w_sbn, w_out_g):
    S, D = x.shape
    tm = 512

    def body(o0, o1, o2, l0, l1, l2, os_ref, x_ref, wd_ref, ws_ref, w_ref, od_ref, lse_ref, x1_ref):
        ls = [l0[...], l1[...], l2[...]]
        m = jnp.maximum(jnp.maximum(ls[0], ls[1]), ls[2])
        es = [jnp.exp(l - m) for l in ls]
        den = es[0] + es[1] + es[2]
        od = (es[0] * o0[...] + es[1] * o1[...] + es[2] * o2[...]) / den
        od_ref[...] = od
        lse_ref[...] = m + jnp.log(den)
        osb = os_ref[...]
        mixed = jnp.concatenate([(od * _rms(od) * wd_ref[...]).astype(BF16),
                                 (osb * _rms(osb) * ws_ref[...]).astype(BF16)], axis=1)
        x1_ref[...] = x_ref[...] + _dot(mixed, w_ref[...])

    row = lambda i: (i, 0)
    g = pl.BlockSpec((tm, GROUP), row)
    d = pl.BlockSpec((tm, D), row)
    return pl.pallas_call(
        body, name="out_proj_fwd", grid=(S // tm,),
        in_specs=[g] * 7 + [d, _full((1, GROUP)), _full((1, GROUP)), _full((2 * GROUP, D))],
        out_specs=[g, g, d],
        out_shape=[jax.ShapeDtypeStruct((S, GROUP), F32)] * 2 + [jax.ShapeDtypeStruct((S, D), F32)],
    )(*o_br, *l_br, o_sb, x, w_dil, w_sbn, w_out_g)


def _ffn_fwd(x1, target, ffn_w, wg_g, wu_g, wd_g):
    S, D = x1.shape
    F = wg_g.shape[2]
    tm = 512
    nt = S // tm

    def body(x_ref, t_ref, nw_ref, wg_ref, wu_ref, wd_ref, h_ref, g_ref, u_ref, dy_ref, loss_ref, h_s, acc):
        j = pl.program_id(1)

        @pl.when(j == 0)
        def _():
            xv = x_ref[...]
            h = (xv * _rms(xv) * nw_ref[...]).astype(BF16)
            h_s[...] = h
            h_ref[...] = h
            acc[...] = xv

        h = h_s[...]
        g = _dot(h, wg_ref[0])
        u = _dot(h, wu_ref[0])
        g_ref[0] = g.astype(BF16)
        u_ref[0] = u.astype(BF16)
        a = (g * _sigmoid(g) * u).astype(BF16)
        acc[...] += _dot(a, wd_ref[0])

        @pl.when(j == N_CHIPS - 1)
        def _():
            err = acc[...] - t_ref[...]
            dy_ref[...] = err * (1.0 / D)
            loss_ref[...] = jnp.full(loss_ref.shape, jnp.sum(err * err), F32)

    row = lambda t, j: (t, 0)
    shard = lambda t, j: (j, 0, 0)
    act = lambda t, j: (j, t, 0)
    return pl.pallas_call(
        body, name="ffn_fwd", grid=(nt, N_CHIPS),
        in_specs=[pl.BlockSpec((tm, D), row), pl.BlockSpec((tm, D), row), pl.BlockSpec((1, D), lambda t, j: (0, 0)),
                  pl.BlockSpec((1, D, F), shard), pl.BlockSpec((1, D, F), shard), pl.BlockSpec((1, F, D), shard)],
        out_specs=[pl.BlockSpec((tm, D), row), pl.BlockSpec((1, tm, F), act), pl.BlockSpec((1, tm, F), act),
                   pl.BlockSpec((tm, D), row), pl.BlockSpec((1, 8, LANES), lambda t, j: (t, 0, 0))],
        out_shape=[jax.ShapeDtypeStruct((S, D), BF16), jax.ShapeDtypeStruct((N_CHIPS, S, F), BF16),
                   jax.ShapeDtypeStruct((N_CHIPS, S, F), BF16), jax.ShapeDtypeStruct((S, D), F32),
                   jax.ShapeDtypeStruct((nt, 8, LANES), F32)],
        scratch_shapes=[pltpu.VMEM((tm, D), BF16), pltpu.VMEM((tm, D), F32)],
    )(x1, target, ffn_w, wg_g, wu_g, wd_g)


def _ffn_bwd(h2, dy, g, u, wg_g, wu_g, wd_g):
    S, D = dy.shape
    F = wg_g.shape[2]
    tm = 512

    def body(h_ref, dy_ref, g_ref, u_ref, wg_ref, wu_ref, wd_ref, dwg_ref, dwu_ref, dwd_ref, dh_ref):
        t = pl.program_id(1)

        @pl.when(t == 0)
        def _():
            dwg_ref[...] = jnp.zeros_like(dwg_ref)
            dwu_ref[...] = jnp.zeros_like(dwu_ref)
            dwd_ref[...] = jnp.zeros_like(dwd_ref)

        h = h_ref[...]
        dyb = dy_ref[...].astype(BF16)
        gv = g_ref[0].astype(F32)
        uv = u_ref[0].astype(F32)
        da = _dot_nt(dyb, wd_ref[0])
        sg = _sigmoid(gv)
        silu = gv * sg
        du = (da * silu).astype(BF16)
        dg = (da * uv * (sg * (1.0 + gv * (1.0 - sg)))).astype(BF16)
        dwd_ref[0] += _dot_tn((silu * uv).astype(BF16), dyb)
        dwg_ref[0] += _dot_tn(h, dg)
        dwu_ref[0] += _dot_tn(h, du)
        dh_ref[0] = _dot_nt(dg, wg_ref[0]) + _dot_nt(du, wu_ref[0])

    row = lambda j, t: (t, 0)
    shard = lambda j, t: (j, 0, 0)
    act = lambda j, t: (j, t, 0)
    return pl.pallas_call(
        body, name="ffn_bwd", grid=(N_CHIPS, S // tm),
        in_specs=[pl.BlockSpec((tm, D), row), pl.BlockSpec((tm, D), row),
                  pl.BlockSpec((1, tm, F), act), pl.BlockSpec((1, tm, F), act),
                  pl.BlockSpec((1, D, F), shard), pl.BlockSpec((1, D, F), shard), pl.BlockSpec((1, F, D), shard)],
        out_specs=[pl.BlockSpec((1, D, F), shard), pl.BlockSpec((1, D, F), shard), pl.BlockSpec((1, F, D), shard),
                   pl.BlockSpec((1, tm, D), act)],
        out_shape=[jax.ShapeDtypeStruct((N_CHIPS, D, F), F32), jax.ShapeDtypeStruct((N_CHIPS, D, F), F32),
                   jax.ShapeDtypeStruct((N_CHIPS, F, D), F32), jax.ShapeDtypeStruct((N_CHIPS, S, D), F32)],
    )(h2, dy, g, u, wg_g, wu_g, wd_g)


def _out_proj_bwd(dh2p, dy, x1, ffn_w, w_out_g, o_dil, o_sb, w_dil, w_sbn, seg_ones):
    S, D = dy.shape
    tm = 256

    def body(dh_ref, dy_ref, x1_ref, nw_ref, w_ref, od_ref, os_ref, wd_ref, ws_ref, g_ref,
             dx1_ref, dod_ref, dos_ref, dl_ref, dw_ref, dnw_ref, dwd_ref, dws_ref):
        i = pl.program_id(0)

        @pl.when(i == 0)
        def _():
            for r_ in (dw_ref, dnw_ref, dwd_ref, dws_ref):
                r_[...] = jnp.zeros_like(r_)

        dh2 = (dh_ref[0] + dh_ref[1]) + (dh_ref[2] + dh_ref[3])
        dxn, dwn = _rms_bwd(dh2, x1_ref[...], nw_ref[...])
        dnw_ref[...] += jnp.sum(dwn, axis=0, keepdims=True)
        dx1 = dy_ref[...] + dxn
        dx1_ref[...] = dx1
        dx1b = dx1.astype(BF16)
        dmix = _dot_nt(dx1b, w_ref[...])
        od = od_ref[...]
        osb = os_ref[...]
        mixed = jnp.concatenate([(od * _rms(od) * wd_ref[...]).astype(BF16),
                                 (osb * _rms(osb) * ws_ref[...]).astype(BF16)], axis=1)
        dw_ref[...] += _dot_tn(mixed, dx1b)
        for o, wr, dm, do_r, dwr in ((od, wd_ref, dmix[:, :GROUP], dod_ref, dwd_ref),
                                     (osb, ws_ref, dmix[:, GROUP:], dos_ref, dws_ref)):
            do, dwo = _rms_bwd(dm, o, wr[...])
            dwr[...] += jnp.sum(dwo, axis=0, keepdims=True)
            do_r[...] = do.astype(BF16)
            if do_r is dod_ref:
                dl_ref[...] = _segsum(do * o, g_ref[...])

    row = lambda i: (i, 0)
    gsp = pl.BlockSpec((tm, GROUP), row)
    dsp = pl.BlockSpec((tm, D), row)
    return pl.pallas_call(
        body, name="out_proj_bwd", grid=(S // tm,),
        in_specs=[pl.BlockSpec((N_CHIPS, tm, D), lambda i: (0, i, 0)), dsp, dsp, _full((1, D)), _full((2 * GROUP, D)),
                  gsp, gsp, _full((1, GROUP)), _full((1, GROUP)), _full((GROUP, GROUP))],
        out_specs=[dsp, gsp, gsp, gsp, _full((2 * GROUP, D)), _full((1, D)), _full((1, GROUP)), _full((1, GROUP))],
        out_shape=[jax.ShapeDtypeStruct((S, D), F32), jax.ShapeDtypeStruct((S, GROUP), BF16),
                   jax.ShapeDtypeStruct((S, GROUP), BF16), jax.ShapeDtypeStruct((S, GROUP), F32),
                   jax.ShapeDtypeStruct((2 * GROUP, D), F32),
                   jax.ShapeDtypeStruct((1, D), F32), jax.ShapeDtypeStruct((1, GROUP), F32),
                   jax.ShapeDtypeStruct((1, GROUP), F32)],
    )(dh2p, dy, x1, ffn_w, w_out_g, o_dil, o_sb, w_dil, w_sbn, seg_ones)


def _qk_bwd(dq_br, dk_br, dv_br, dqs, dks, dvs, qa, ka, qw, kw, cos_t, sin_t, seg_ones):
    S = qa.shape[0]
    tm = 256

    def body(q0, q1, q2, k0, k1, k2, v0, v1, v2, dqs_ref, dks_ref, dvs_ref, qa_ref, ka_ref, qw_ref, kw_ref,
             cos_ref, sin_ref, g_ref, dp_ref, dqw_ref, dkw_ref, accq, acck):
        i = pl.program_id(0)

        @pl.when(i == 0)
        def _():
            accq[...] = jnp.zeros_like(accq)
            acck[...] = jnp.zeros_like(acck)

        g = g_ref[...]
        cos = cos_ref[...]
        sin = sin_ref[...]
        for b, (refs, pre_ref, w_ref, acc) in enumerate((((q0, q1, q2), qa_ref, qw_ref, accq),
                                                        ((k0, k1, k2), ka_ref, kw_ref, acck))):
            dh = (refs[0][...] + refs[1][...]) + refs[2][...]
            dn = dh * cos + _rot_half(dh * sin)
            pre = pre_ref[...]
            rstd = lax.rsqrt(_segsum(pre * pre, g) * (1.0 / HEAD_DIM) + EPS)
            xh = pre * rstd
            acc[...] += jnp.sum(dn * xh, axis=0, keepdims=True)
            dxh = dn * w_ref[...]
            dpre = rstd * (dxh - xh * (_segsum(dxh * xh, g) * (1.0 / HEAD_DIM)))
            dp_ref[:, b * GROUP:(b + 1) * GROUP] = dpre.astype(BF16)
        dp_ref[:, 2 * GROUP:3 * GROUP] = ((v0[...] + v1[...]) + v2[...]).astype(BF16)
        dp_ref[:, 3 * GROUP:4 * GROUP] = dqs_ref[...].astype(BF16)
        dp_ref[:, 4 * GROUP:5 * GROUP] = dks_ref[...].astype(BF16)
        dp_ref[:, 5 * GROUP:6 * GROUP] = dvs_ref[...].astype(BF16)

        @pl.when(i == S // tm - 1)
        def _():
            for acc, o_ref in ((accq, dqw_ref), (acck, dkw_ref)):
                a = acc[...]
                pair = (a[:, 0:LANES] + a[:, LANES:2 * LANES]) + (a[:, 2 * LANES:3 * LANES] + a[:, 3 * LANES:4 * LANES])
                o_ref[...] = pair + pltpu.roll(pair, HEAD_DIM, 1)

    row = lambda i: (i, 0)
    gsp = pl.BlockSpec((tm, GROUP), row)
    return pl.pallas_call(
        body, name="qk_bwd", grid=(S // tm,),
        in_specs=[gsp] * 14 + [_full((1, GROUP)), _full((1, GROUP)), gsp, gsp, _full((GROUP, GROUP))],
        out_specs=[pl.BlockSpec((tm, 6 * GROUP), row), _full((1, LANES)), _full((1, LANES))],
        out_shape=[jax.ShapeDtypeStruct((S, 6 * GROUP), BF16), jax.ShapeDtypeStruct((1, LANES), F32),
                   jax.ShapeDtypeStruct((1, LANES), F32)],
        scratch_shapes=[pltpu.VMEM((1, GROUP), F32), pltpu.VMEM((1, GROUP), F32)],
    )(*dq_br, *dk_br, *dv_br, dqs, dks, dvs, qa, ka, qw, kw, cos_t, sin_t, seg_ones)


def _in_proj_bwd(h, dproj, w_in_g):
    S, D = h.shape
    wc = w_in_g.shape[2]
    tm = 512

    def body(h_ref, dp_ref, w_ref, dw_ref, dh_ref):
        t = pl.program_id(1)

        @pl.when(t == 0)
        def _():
            dw_ref[...] = jnp.zeros_like(dw_ref)

        dp = dp_ref[...]
        dw_ref[0] += _dot_tn(h_ref[...], dp)
        dh_ref[0] = _dot_nt(dp, w_ref[0])

    return pl.pallas_call(
        body, name="in_proj_bwd", grid=(N_CHIPS, S // tm),
        in_specs=[pl.BlockSpec((tm, D), lambda j, t: (t, 0)), pl.BlockSpec((tm, wc), lambda j, t: (t, j)),
                  pl.BlockSpec((1, D, wc), lambda j, t: (j, 0, 0))],
        out_specs=[pl.BlockSpec((1, D, wc), lambda j, t: (j, 0, 0)), pl.BlockSpec((1, tm, D), lambda j, t: (j, t, 0))],
        out_shape=[jax.ShapeDtypeStruct((N_CHIPS, D, wc), F32), jax.ShapeDtypeStruct((N_CHIPS, S, D), F32)],
    )(h, dproj, w_in_g)


def _in_norm_bwd(dhp, dx1, x, attn_w):
    S, D = x.shape
    tm = 512

    def body(dh_ref, dx1_ref, x_ref, w_ref, gx_ref, dw_ref):
        i = pl.program_id(0)

        @pl.when(i == 0)
        def _():
            dw_ref[...] = jnp.zeros_like(dw_ref)

        dh = (dh_ref[0] + dh_ref[1]) + (dh_ref[2] + dh_ref[3])
        dx, dw = _rms_bwd(dh, x_ref[...], w_ref[...])
        dw_ref[...] += jnp.sum(dw, axis=0, keepdims=True)
        gx_ref[...] = dx1_ref[...] + dx

    row = lambda i: (i, 0)
    dsp = pl.BlockSpec((tm, D), row)
    return pl.pallas_call(
        body, name="in_norm_bwd", grid=(S // tm,),
        in_specs=[pl.BlockSpec((N_CHIPS, tm, D), lambda i: (0, i, 0)), dsp, dsp, _full((1, D))],
        out_specs=[dsp, _full((1, D))],
        out_shape=[jax.ShapeDtypeStruct((S, D), F32), jax.ShapeDtypeStruct((1, D), F32)],
    )(dhp, dx1, x, attn_w)


def _constants(S):
    pos = jnp.arange(S, dtype=F32)
    inv_freq = ROPE_THETA ** (-jnp.arange(0, HEAD_DIM, 2, dtype=F32) / HEAD_DIM)
    ang = pos[:, None] * inv_freq[None, :]
    cos, sin = jnp.cos(ang), jnp.sin(ang)
    reps = GROUP // HEAD_DIM
    cos_t = jnp.tile(jnp.concatenate([cos, cos], axis=1), (1, reps))
    sin_t = jnp.tile(jnp.concatenate([-sin, sin], axis=1), (1, reps))
    idx = jnp.arange(GROUP)
    seg_ones = (idx[:, None] // HEAD_DIM == idx[None, :] // HEAD_DIM).astype(BF16)
    r = jnp.arange(BLOCK)
    tris = [jnp.concatenate([m.astype(BF16)] * 2, axis=0) for m in
            (r[:, None] > r[None, :],
             r[:, None] <= r[None, :],
             r[:, None] < r[None, :])]
    return cos_t, sin_t, seg_ones, tris


def _local_step(x, target, attn_w, qn_w, kn_w, dil_w, sbn_w, ffn_w, w_in_g, w_out_g, wg_g, wu_g, wd_g):
    S = x.shape[0]
    cos_t, sin_t, seg_ones, (tri_later, tri_upto, tri_before) = _constants(S)
    reps = GROUP // HEAD_DIM
    qw = jnp.tile(qn_w, (1, reps))
    kw = jnp.tile(kn_w, (1, reps))

    h, qa, ka, qh, kh, va, qs, ks, vs = _in_proj_fwd(x, attn_w, w_in_g, qw, kw, cos_t, sin_t, seg_ones)
    branches = [_dil_fwd(qh, kh, va, r) for r in DILATIONS]
    o_sb, ltot = _sb_fwd(qs, ks, vs, tri_later)
    o_dil, lse, x1 = _out_proj_fwd([b[0] for b in branches], [b[1] for b in branches], o_sb, x, dil_w, sbn_w, w_out_g)
    h2, g, u, dy, loss_parts = _ffn_fwd(x1, target, ffn_w, wg_g, wu_g, wd_g)

    dwg, dwu, dwd, dh2p = _ffn_bwd(h2, dy, g, u, wg_g, wu_g, wd_g)
    dx1, do_dil, do_sb, delta, dw_out, dffn_w, ddil_w, dsbn_w = _out_proj_bwd(
        dh2p, dy, x1, ffn_w, w_out_g, o_dil, o_sb, dil_w, sbn_w, seg_ones)
    dqs, dks, dvs = _sb_bwd(qs, ks, vs, do_sb, ltot, tri_upto, tri_before)
    dbr = [_dil_bwd(qh, kh, va, do_dil, lse, delta, r) for r in DILATIONS]
    dproj, dqw, dkw = _qk_bwd([b[0] for b in dbr], [b[1] for b in dbr], [b[2] for b in dbr], dqs, dks, dvs,
                              qa, ka, qw, kw, cos_t, sin_t, seg_ones)
    dw_in, dhp = _in_proj_bwd(h, dproj, w_in_g)
    grad_x, dattn_w = _in_norm_bwd(dhp, dx1, x, attn_w)
    small = dict(attn=dattn_w, q=dqw[:, :HEAD_DIM], k=dkw[:, :HEAD_DIM], dil=ddil_w, sb=dsbn_w, ffn=dffn_w)
    return loss_parts, grad_x, small, dw_in, dw_out, dwg, dwu, dwd


HBM = pl.BlockSpec(memory_space=pltpu.HBM)
VMEM = pl.BlockSpec(memory_space=pltpu.VMEM)
CHIP_FLIPS = ((1, 0), (0, 1), (1, 1))


def _place():
    return lax.axis_index("x"), lax.axis_index("y"), lax.axis_index("c")


def _flip(v, d):
    return 1 - v if d else v


def _half_rows(c, n):
    return pl.ds(pl.multiple_of(c * (n // 2), 16), n // 2)


def _gather_weights(shards):
    n = len(shards)

    def body(*refs):
        ins, outs = refs[:n], refs[n:2 * n]
        send, recv, local = refs[2 * n:]
        x, y, c = _place()
        p = 2 * x + y
        chips = [(_flip(x, dx), _flip(y, dy)) for dx, dy in CHIP_FLIPS]
        own = []
        for a in range(n):
            own.append(pltpu.make_async_copy(ins[a], outs[a].at[p], local.at[a]))
            own[-1].start()

        def copy(a, k, src, dst, to):
            return pltpu.make_async_remote_copy(src_ref=src, dst_ref=dst, send_sem=send.at[a * 6 + k],
                                                recv_sem=recv.at[a * 6 + k], device_id=to, device_id_type=MESH)

        sent = []
        for a in range(n):
            mine = _half_rows(c, ins[a].shape[0])
            for k, chip in enumerate(chips):
                sent.append(copy(a, k, ins[a].at[mine], outs[a].at[p, mine], (*chip, c)))
                sent[-1].start()
        for a in range(n):
            mine = _half_rows(c, ins[a].shape[0])
            for k, (qx, qy) in enumerate(chips):
                land = outs[a].at[2 * qx + qy, mine]
                copy(a, k, land, land, (qx, qy, c)).wait_recv()
                sent.append(copy(a, 3 + k, land, land, (x, y, 1 - c)))
                sent[-1].start()
        for a in range(n):
            other = _half_rows(1 - c, ins[a].shape[0])
            for k, (qx, qy) in enumerate(chips):
                land = outs[a].at[2 * qx + qy, other]
                copy(a, 3 + k, land, land, (x, y, 1 - c)).wait_recv()
        for cp in sent:
            cp.wait_send()
        for cp in own:
            cp.wait()

    return pl.pallas_call(
        body, name="gather_weights", in_specs=[HBM] * n, out_specs=[HBM] * n,
        out_shape=[jax.ShapeDtypeStruct((N_CHIPS,) + s.shape, s.dtype) for s in shards],
        scratch_shapes=[pltpu.SemaphoreType.DMA((6 * n,)), pltpu.SemaphoreType.DMA((6 * n,)),
                        pltpu.SemaphoreType.DMA((n,))],
    )(*shards)


def _pair_exchange(grads, small):
    n = len(grads)

    def body(*refs):
        gin, sm = refs[:n], refs[n]
        gout, sm_all = refs[n + 1:2 * n + 1], refs[2 * n + 1]
        send, recv = refs[2 * n + 2:]
        x, y, c = _place()
        me = 4 * x + 2 * y + c
        big = []
        for a in range(n):
            theirs = _half_rows(1 - c, gin[a].shape[1])
            big.append(pltpu.make_async_remote_copy(
                src_ref=gin[a].at[:, theirs, :], dst_ref=gout[a], send_sem=send.at[a], recv_sem=recv.at[a],
                device_id=(x, y, 1 - c), device_id_type=MESH))
            big[-1].start()
        sm_all[pl.ds(me, 1)] = sm[...][None]
        tiny = []
        for k in range(1, N_DEV):
            px, py, pc = _flip(x, k & 4), _flip(y, k & 2), _flip(c, k & 1)
            tiny.append((pltpu.make_async_remote_copy(
                src_ref=sm, dst_ref=sm_all.at[me], send_sem=send.at[n + k - 1], recv_sem=recv.at[n + k - 1],
                device_id=(px, py, pc), device_id_type=MESH), 4 * px + 2 * py + pc))
            tiny[-1][0].start()
        for k, (cp, peer) in enumerate(tiny):
            pltpu.make_async_remote_copy(src_ref=sm, dst_ref=sm_all.at[peer], send_sem=send.at[n + k],
                                         recv_sem=recv.at[n + k], device_id=(x, y, c),
                                         device_id_type=MESH).wait_recv()
            cp.wait_send()
        for cp in big:
            cp.wait()

    halves = [jax.ShapeDtypeStruct((g.shape[0], g.shape[1] // 2, g.shape[2]), g.dtype) for g in grads]
    return pl.pallas_call(
        body, name="pair_exchange", in_specs=[HBM] * n + [VMEM], out_specs=[HBM] * n + [VMEM],
        out_shape=halves + [jax.ShapeDtypeStruct((N_DEV,) + small.shape, small.dtype)],
        scratch_shapes=[pltpu.SemaphoreType.DMA((n + N_DEV - 1,)), pltpu.SemaphoreType.DMA((n + N_DEV - 1,))],
    )(*grads, small)


def _chip_exchange(parts):
    n = len(parts)

    def body(*refs):
        pin, pout = refs[:n], refs[n:2 * n]
        send, recv, local = refs[2 * n:]
        x, y, c = _place()
        p = 2 * x + y
        chips = [(_flip(x, dx), _flip(y, dy)) for dx, dy in CHIP_FLIPS]
        own, sent = [], []
        for a in range(n):
            own.append(pltpu.make_async_copy(pin[a].at[p], pout[a].at[p], local.at[a]))
            own[-1].start()
            for k, (qx, qy) in enumerate(chips):
                sent.append(pltpu.make_async_remote_copy(
                    src_ref=pin[a].at[2 * qx + qy], dst_ref=pout[a].at[p], send_sem=send.at[a * 3 + k],
                    recv_sem=recv.at[a * 3 + k], device_id=(qx, qy, c), device_id_type=MESH))
                sent[-1].start()
        for a in range(n):
            for k, (qx, qy) in enumerate(chips):
                land = pout[a].at[2 * qx + qy]
                pltpu.make_async_remote_copy(src_ref=land, dst_ref=land, send_sem=send.at[a * 3 + k],
                                             recv_sem=recv.at[a * 3 + k], device_id=(qx, qy, c),
                                             device_id_type=MESH).wait_recv()
        for cp in sent:
            cp.wait_send()
        for cp in own:
            cp.wait()

    return pl.pallas_call(
        body, name="chip_exchange", in_specs=[HBM] * n, out_specs=[HBM] * n,
        out_shape=[jax.ShapeDtypeStruct(s.shape, s.dtype) for s in parts],
        scratch_shapes=[pltpu.SemaphoreType.DMA((3 * n,)), pltpu.SemaphoreType.DMA((3 * n,)),
                        pltpu.SemaphoreType.DMA((n,))],
    )(*parts)


def _pair_gather(halves):
    n = len(halves)

    def body(*refs):
        hin, gout = refs[:n], refs[n:2 * n]
        send, recv, local = refs[2 * n:]
        x, y, c = _place()
        own, sent = [], []
        for a in range(n):
            mine = _half_rows(c, gout[a].shape[0])
            own.append(pltpu.make_async_copy(hin[a], gout[a].at[mine], local.at[a]))
            own[-1].start()
            sent.append(pltpu.make_async_remote_copy(
                src_ref=hin[a], dst_ref=gout[a].at[mine], send_sem=send.at[a], recv_sem=recv.at[a],
                device_id=(x, y, 1 - c), device_id_type=MESH))
            sent[-1].start()
        for a in range(n):
            land = gout[a].at[_half_rows(1 - c, gout[a].shape[0])]
            pltpu.make_async_remote_copy(src_ref=land, dst_ref=land, send_sem=send.at[a], recv_sem=recv.at[a],
                                         device_id=(x, y, 1 - c), device_id_type=MESH).wait_recv()
        for cp in sent:
            cp.wait_send()
        for cp in own:
            cp.wait()

    return pl.pallas_call(
        body, name="pair_gather", in_specs=[HBM] * n, out_specs=[HBM] * n,
        out_shape=[jax.ShapeDtypeStruct((2 * s.shape[0], s.shape[1]), s.dtype) for s in halves],
        scratch_shapes=[pltpu.SemaphoreType.DMA((n,)), pltpu.SemaphoreType.DMA((n,)), pltpu.SemaphoreType.DMA((n,))],
    )(*halves)


def _pair_sum(grad, recv, c, tag):
    _, R, C = grad.shape
    hr = R // 2

    def body(c_ref, a_ref, b_ref, o_ref):
        o_ref[...] = a_ref[...] + b_ref[...]

    return pl.pallas_call(
        body, name="pair_sum_" + tag,
        grid_spec=pltpu.PrefetchScalarGridSpec(
            num_scalar_prefetch=1, grid=(N_CHIPS,),
            in_specs=[pl.BlockSpec((1, hr, C), lambda s, cr: (s, cr[0], 0)),
                      pl.BlockSpec((1, hr, C), lambda s, cr: (s, 0, 0))],
            out_specs=pl.BlockSpec((1, hr, C), lambda s, cr: (s, 0, 0))),
        out_shape=jax.ShapeDtypeStruct((N_CHIPS, hr, C), grad.dtype),
    )(c, grad, recv)


def _chip_sum(parts, tag):
    _, rows, C = parts.shape
    tr = rows // 2

    def body(p_ref, o_ref):
        o_ref[...] = (p_ref[0] + p_ref[1]) + (p_ref[2] + p_ref[3])

    return pl.pallas_call(
        body, name="chip_sum_" + tag, grid=(rows // tr,),
        in_specs=[pl.BlockSpec((N_CHIPS, tr, C), lambda i: (0, i, 0))],
        out_specs=pl.BlockSpec((tr, C), lambda i: (i, 0)),
        out_shape=jax.ShapeDtypeStruct((rows, C), parts.dtype),
    )(parts)


def _adamw_math(w, g, m, v):
    m = ADAM_B1 * m + (1.0 - ADAM_B1) * g
    v = ADAM_B2 * v + (1.0 - ADAM_B2) * (g * g)
    m_hat = m / (1.0 - ADAM_B1 ** ADAM_STEP)
    v_hat = v / (1.0 - ADAM_B2 ** ADAM_STEP)
    delta = -ADAM_LR * (m_hat / (jnp.sqrt(v_hat) + ADAM_EPS) + ADAM_WD * w)
    return delta, m, v


def _adamw(w, g, m, v, tag):
    R, C = w.shape
    tr = R // 4

    def body(w_ref, g_ref, m_ref, v_ref, d_ref, nm_ref, nv_ref):
        d_ref[...], nm_ref[...], nv_ref[...] = _adamw_math(w_ref[...], g_ref[...], m_ref[...], v_ref[...])

    blk = pl.BlockSpec((tr, C), lambda i: (i, 0))
    return pl.pallas_call(
        body, name="adamw_" + tag, grid=(R // tr,), in_specs=[blk] * 4, out_specs=[blk] * 3,
        out_shape=[jax.ShapeDtypeStruct((R, C), F32)] * 3,
    )(w, g, m, v)


def _small_update(all_small, w, m, v):
    def body(a_ref, w_ref, m_ref, v_ref, g_ref, d_ref, nm_ref, nv_ref):
        g = ((a_ref[0] + a_ref[1]) + (a_ref[2] + a_ref[3])) + ((a_ref[4] + a_ref[5]) + (a_ref[6] + a_ref[7]))
        g_ref[...] = g
        d_ref[...], nm_ref[...], nv_ref[...] = _adamw_math(w_ref[...], g, m_ref[...], v_ref[...])

    return pl.pallas_call(
        body, name="small_update", out_shape=[jax.ShapeDtypeStruct(w.shape, F32)] * 4,
    )(all_small, w, m, v)


SMALL_ROWS = (("attn", 0, 0), ("ffn", 1, 0), ("dil", 2, 0), ("sb", 2, GROUP), ("q", 3, 0), ("k", 3, HEAD_DIM))


def _pack_small(vals, D):
    rows = [jnp.zeros((1, D), F32) for _ in range(8)]
    for name, r, off in SMALL_ROWS:
        rows[r] = lax.dynamic_update_slice(rows[r], vals[name].astype(F32), (0, off))
    return jnp.concatenate(rows, axis=0)


def _unpack_small(packed, vals):
    return {name: packed[r:r + 1, off:off + vals[name].shape[1]] for name, r, off in SMALL_ROWS}


def kernel(x, attn_norm_w, w_in, q_norm_w, k_norm_w, dil_out_norm_w, sb_out_norm_w, w_out, ffn_norm_w, w_gate, w_up, w_down, loss_target, m_attn_norm_w, m_w_in, m_q_norm_w, m_k_norm_w, m_dil_out_norm_w, m_sb_out_norm_w, m_w_out, m_ffn_norm_w, m_w_gate, m_w_up, m_w_down, v_attn_norm_w, v_w_in, v_q_norm_w, v_k_norm_w, v_dil_out_norm_w, v_sb_out_norm_w, v_w_out, v_ffn_norm_w, v_w_gate, v_w_up, v_w_down):
    D = x.shape[-1]
    big_names = ("w_in", "w_out", "w_gate", "w_up", "w_down")
    big_w = dict(w_in=w_in[0], w_out=w_out[0], w_gate=w_gate[0], w_up=w_up[0], w_down=w_down[0])
    big_m = dict(w_in=m_w_in[0], w_out=m_w_out[0], w_gate=m_w_gate[0], w_up=m_w_up[0], w_down=m_w_down[0])
    big_v = dict(w_in=v_w_in[0], w_out=v_w_out[0], w_gate=v_w_gate[0], w_up=v_w_up[0], w_down=v_w_down[0])
    small_w = dict(attn=attn_norm_w, q=q_norm_w, k=k_norm_w, dil=dil_out_norm_w, sb=sb_out_norm_w, ffn=ffn_norm_w)
    small_m = dict(attn=m_attn_norm_w, q=m_q_norm_w, k=m_k_norm_w, dil=m_dil_out_norm_w, sb=m_sb_out_norm_w,
                   ffn=m_ffn_norm_w)
    small_v = dict(attn=v_attn_norm_w, q=v_q_norm_w, k=v_k_norm_w, dil=v_dil_out_norm_w, sb=v_sb_out_norm_w,
                   ffn=v_ffn_norm_w)

    gathered = _gather_weights([big_w[n].astype(BF16) for n in big_names])
    w_in_g, w_out_g, wg_g, wu_g, wd_g = gathered
    w_out_full = w_out_g.reshape(-1, D)

    loss_parts, grad_x, small_g, dw_in, dw_out, dwg, dwu, dwd = _local_step(
        x[0], loss_target[0], attn_norm_w, q_norm_w, k_norm_w, dil_out_norm_w, sb_out_norm_w, ffn_norm_w,
        w_in_g, w_out_full, wg_g, wu_g, wd_g)
    loss = lax.psum(jnp.sum(loss_parts[:, 0, 0]) * (0.5 / D), ("x", "y", "c"))

    full = [dw_in, dw_out.reshape(N_CHIPS, -1, D), dwg, dwu, dwd]
    c = lax.axis_index("c").astype(jnp.int32).reshape(1)
    *from_pair, all_small = _pair_exchange(full, _pack_small(small_g, D))
    chip_parts = [_pair_sum(g, r, c, n) for g, r, n in zip(full, from_pair, big_names)]
    from_chips = _chip_exchange(chip_parts)
    halves = [_chip_sum(p, n) for p, n in zip(from_chips, big_names)]
    grads = dict(zip(big_names, _pair_gather(halves)))

    big_out = {n: _adamw(big_w[n], grads[n], big_m[n], big_v[n], n) for n in big_names}
    sg, sd, sm, sv = _small_update(all_small, _pack_small(small_w, D), _pack_small(small_m, D),
                                   _pack_small(small_v, D))
    small_out = [_unpack_small(t, small_w) for t in (sg, sd, sm, sv)]

    order = (("attn", None), (None, "w_in"), ("q", None), ("k", None), ("dil", None), ("sb", None),
             (None, "w_out"), ("ffn", None), (None, "w_gate"), (None, "w_up"), (None, "w_down"))
    outs = [loss, grad_x[None]]
    for kind in range(4):
        for s_name, b_name in order:
            if s_name is not None:
                outs.append(small_out[kind][s_name])
            elif kind == 0:
                outs.append(grads[b_name][None])
            else:
                outs.append(big_out[b_name][kind - 1][None])
    return tuple(outs)
```

```python
import functools

import jax
import jax.numpy as jnp
from jax import lax
from jax.experimental import pallas as pl
from jax.experimental.pallas import tpu as pltpu

F32 = jnp.float32
BF16 = jnp.bfloat16
MESH = pl.DeviceIdType.MESH

HEAD_DIM = 64
GROUP = 512
BLOCK = 128
LANES = 128
N_CHIPS = 4
N_DEV = 8
EPS = 1e-6
ROPE_THETA = 10000.0
DILATIONS = (1, 4, 16)
NEG = -1e30

ADAM_LR = 0.001
ADAM_B1 = 0.9
ADAM_B2 = 0.999
ADAM_EPS = 1e-08
ADAM_WD = 0.01
ADAM_STEP = 10


def _dot(a, b):
    return jnp.dot(a, b, preferred_element_type=F32)


def _dot_nt(a, b):
    return lax.dot_general(a, b, (((1,), (1,)), ((), ())), preferred_element_type=F32)


def _dot_tn(a, b):
    return lax.dot_general(a, b, (((0,), (0,)), ((), ())), preferred_element_type=F32)


def _split(v):
    hi = lax.bitcast_convert_type(lax.bitcast_convert_type(v, jnp.uint32) & jnp.uint32(0xFFFF0000), F32)
    return hi.astype(BF16), (v - hi).astype(BF16)


def _segsum(v, g):
    hi, lo = _split(v)
    return _dot(hi, g) + _dot(lo, g)


def _rot_half(x):
    outs = []
    for c in range(x.shape[1] // LANES):
        xc = x[:, c * LANES:(c + 1) * LANES]
        lane = lax.broadcasted_iota(jnp.int32, xc.shape, 1)
        first = (lane % HEAD_DIM) < (HEAD_DIM // 2)
        outs.append(jnp.where(first, pltpu.roll(xc, LANES - 32, 1), pltpu.roll(xc, 32, 1)))
    return outs[0] if len(outs) == 1 else jnp.concatenate(outs, axis=1)


def _rms(x):
    return lax.rsqrt(jnp.mean(x * x, axis=-1, keepdims=True) + EPS)


def _rms_bwd(dy, x, w):
    rstd = _rms(x)
    xh = x * rstd
    dxh = dy * w
    dx = rstd * (dxh - xh * jnp.mean(dxh * xh, axis=-1, keepdims=True))
    return dx, dy * xh


def _sigmoid(x):
    return 1.0 / (1.0 + jnp.exp(-x))


def _full(shape):
    n = len(shape)
    return pl.BlockSpec(shape, lambda *_: (0,) * n)


def _in_proj_fwd(x, attn_w, w_in_g, qw, kw, cos_t, sin_t, seg_ones):
    S, D = x.shape
    tm = 512
    wcols = w_in_g.shape[2]

    def body(x_ref, aw_ref, w_ref, qw_ref, kw_ref, cos_ref, sin_ref, g_ref,
             h_ref, qa_ref, ka_ref, qh_ref, kh_ref, va_ref, qs_ref, ks_ref, vs_ref):
        xv = x_ref[...]
        h = (xv * _rms(xv) * aw_ref[...]).astype(BF16)
        h_ref[...] = h
        proj = jnp.concatenate([_dot(h, w_ref[j]) for j in range(N_CHIPS)], axis=1)
        qa = proj[:, 0 * GROUP:1 * GROUP]
        ka = proj[:, 1 * GROUP:2 * GROUP]
        qa_ref[...] = qa
        ka_ref[...] = ka
        va_ref[...] = proj[:, 2 * GROUP:3 * GROUP].astype(BF16)
        qs_ref[...] = proj[:, 3 * GROUP:4 * GROUP].astype(BF16)
        ks_ref[...] = proj[:, 4 * GROUP:5 * GROUP].astype(BF16)
        vs_ref[...] = proj[:, 5 * GROUP:6 * GROUP].astype(BF16)
        g = g_ref[...]
        cos = cos_ref[...]
        sin = sin_ref[...]
        for t, w_r, o_r in ((qa, qw_ref, qh_ref), (ka, kw_ref, kh_ref)):
            rstd = lax.rsqrt(_segsum(t * t, g) * (1.0 / HEAD_DIM) + EPS)
            tn = t * rstd * w_r[...]
            o_r[...] = (tn * cos + _rot_half(tn) * sin).astype(BF16)

    row = lambda i: (i, 0)
    tile = lambda n, dt: jax.ShapeDtypeStruct((S, n), dt)
    return pl.pallas_call(
        body, name="in_proj_fwd", grid=(S // tm,),
        in_specs=[pl.BlockSpec((tm, D), row), _full((1, D)), _full((N_CHIPS, D, wcols)),
                  _full((1, GROUP)), _full((1, GROUP)),
                  pl.BlockSpec((tm, GROUP), row), pl.BlockSpec((tm, GROUP), row),
                  _full((GROUP, GROUP))],
        out_specs=[pl.BlockSpec((tm, D), row)] + [pl.BlockSpec((tm, GROUP), row)] * 8,
        out_shape=[tile(D, BF16), tile(GROUP, F32), tile(GROUP, F32)] + [tile(GROUP, BF16)] * 6,
    )(x, attn_w, w_in_g, qw, kw, cos_t, sin_t, seg_ones)


def _dil_view(a, r):
    return a.reshape(a.shape[0] // r, r * a.shape[1])


def _dil_fwd(qh, kh, va, r):
    S = qh.shape[0]
    nb = S // r // BLOCK

    def body(q_ref, kc_ref, kp_ref, vc_ref, vp_ref, o_ref, l_ref):
        n = pl.program_id(1)
        rowi = lax.broadcasted_iota(jnp.int32, (BLOCK, BLOCK), 0)
        coli = lax.broadcasted_iota(jnp.int32, (BLOCK, BLOCK), 1)
        m_cur = coli <= rowi
        m_prev = jnp.logical_and(coli >= rowi, n > 0)
        for hp in range(GROUP // LANES):
            sl = slice(hp * LANES, (hp + 1) * LANES)
            q2 = q_ref[:, sl]
            kc, kp, vc, vp = kc_ref[:, sl], kp_ref[:, sl], vc_ref[:, sl], vp_ref[:, sl]
            o_acc = jnp.zeros((BLOCK, LANES), F32)
            l_acc = jnp.zeros((BLOCK, LANES), F32)
            for hh in range(2):
                hmask = (coli // HEAD_DIM) == hh
                qm = jnp.where(hmask, q2, jnp.zeros_like(q2))
                s_c = jnp.where(m_cur, _dot_nt(qm, kc) * 0.125, NEG)
                s_p = jnp.where(m_prev, _dot_nt(qm, kp) * 0.125, NEG)
                m = jnp.maximum(jnp.max(s_c, axis=1, keepdims=True), jnp.max(s_p, axis=1, keepdims=True))
                p_c = jnp.exp(s_c - m)
                p_p = jnp.exp(s_p - m)
                den = jnp.sum(p_c, axis=1, keepdims=True) + jnp.sum(p_p, axis=1, keepdims=True)
                o = (_dot(p_c.astype(BF16), vc) + _dot(p_p.astype(BF16), vp)) / den
                o_acc = jnp.where(hmask, o, o_acc)
                l_acc = jnp.where(hmask, m + jnp.log(den), l_acc)
            o_ref[:, sl] = o_acc
            l_ref[:, sl] = l_acc

    cur = pl.BlockSpec((BLOCK, GROUP), lambda c, n: (n, c))
    prev = pl.BlockSpec((BLOCK, GROUP), lambda c, n: (jnp.maximum(n - 1, 0), c))
    qv, kv, vv = _dil_view(qh, r), _dil_view(kh, r), _dil_view(va, r)
    o, l = pl.pallas_call(
        body, name="dil_fwd_r%d" % r, grid=(r, nb),
        in_specs=[cur, cur, prev, cur, prev], out_specs=[cur, cur],
        out_shape=[jax.ShapeDtypeStruct(qv.shape, F32)] * 2,
    )(qv, kv, kv, vv, vv)
    return o.reshape(S, GROUP), l.reshape(S, GROUP)


def _dil_bwd(qh, kh, va, do, lse, delta, r):
    S = qh.shape[0]
    nb = S // r // BLOCK

    def body(qc_ref, qn_ref, doc_ref, don_ref, lc_ref, ln_ref, dc_ref, dn_ref, k_ref, v_ref,
             dq_ref, dk_ref, dv_ref, carry):
        j = pl.program_id(1)
        rowi = lax.broadcasted_iota(jnp.int32, (BLOCK, BLOCK), 0)
        coli = lax.broadcasted_iota(jnp.int32, (BLOCK, BLOCK), 1)
        m_cur = coli <= rowi
        m_next = jnp.logical_and(coli >= rowi, j < nb - 1)

        @pl.when(j == 0)
        def _():
            carry[...] = jnp.zeros_like(carry)

        for hp in range(GROUP // LANES):
            sl = slice(hp * LANES, (hp + 1) * LANES)
            k2, v2 = k_ref[:, sl], v_ref[:, sl]
            dk_acc = jnp.zeros((BLOCK, LANES), F32)
            dv_acc = jnp.zeros((BLOCK, LANES), F32)
            dq_c = jnp.zeros((BLOCK, LANES), F32)
            dq_n = jnp.zeros((BLOCK, LANES), F32)
            for hh in range(2):
                hmask = (coli // HEAD_DIM) == hh
                col = hp * LANES + hh * HEAD_DIM
                parts = []
                for q_r, do_r, l_r, d_r, msk in ((qc_ref, doc_ref, lc_ref, dc_ref, m_cur),
                                                 (qn_ref, don_ref, ln_ref, dn_ref, m_next)):
                    q2 = q_r[:, sl]
                    do2 = do_r[:, sl]
                    qm = jnp.where(hmask, q2, jnp.zeros_like(q2))
                    dom = jnp.where(hmask, do2, jnp.zeros_like(do2))
                    s = _dot_nt(qm, k2) * 0.125
                    p = jnp.where(msk, jnp.exp(s - l_r[:, col:col + 1]), 0.0)
                    dp = _dot_nt(dom, v2)
                    ds = (p * (dp - d_r[:, col:col + 1]) * 0.125).astype(BF16)
                    dv_acc = dv_acc + _dot_tn(p.astype(BF16), dom)
                    dk_acc = dk_acc + _dot_tn(ds, qm)
                    parts.append(_dot(ds, k2))
                dq_c = jnp.where(hmask, parts[0], dq_c)
                dq_n = jnp.where(hmask, parts[1], dq_n)
            dq_ref[:, sl] = carry[:, sl] + dq_c
            carry[:, sl] = dq_n
            dk_ref[:, sl] = dk_acc
            dv_ref[:, sl] = dv_acc

    cur = pl.BlockSpec((BLOCK, GROUP), lambda c, n: (n, c))
    nxt = pl.BlockSpec((BLOCK, GROUP), lambda c, n: (jnp.minimum(n + 1, nb - 1), c))
    v = lambda a: _dil_view(a, r)
    outs = pl.pallas_call(
        body, name="dil_bwd_r%d" % r, grid=(r, nb),
        in_specs=[cur, nxt, cur, nxt, cur, nxt, cur, nxt, cur, cur], out_specs=[cur, cur, cur],
        out_shape=[jax.ShapeDtypeStruct(v(qh).shape, F32)] * 3,
        scratch_shapes=[pltpu.VMEM((BLOCK, GROUP), F32)],
    )(v(qh), v(qh), v(do), v(do), v(lse), v(lse), v(delta), v(delta), v(kh), v(va))
    return [o.reshape(S, GROUP) for o in outs]


SB_TILES = 4


def _sb_logits(z, valid):
    e = jnp.exp(-jnp.abs(z))
    lb = jnp.minimum(z, 0.0) - jnp.log(1.0 + e)
    lk = lb - z
    if valid is not None:
        lk = jnp.where(valid, lk, 0.0)
    return e, lb, lk


def _by_head(t, first):
    zero = jnp.zeros_like(t)
    return jnp.concatenate([jnp.where(first, t, zero), jnp.where(first, zero, t)], axis=0)


def _sb_valid(i, j):
    rowi = lax.broadcasted_iota(jnp.int32, (BLOCK, BLOCK), 0)
    coli = lax.broadcasted_iota(jnp.int32, (BLOCK, BLOCK), 1)
    return (coli - rowi) < (i - j) * BLOCK


def _scaled(q):
    return (q.astype(F32) * (HEAD_DIM ** -0.5)).astype(BF16)


def _sb_fwd(qs, ks, vs, tri_later):
    S = qs.shape[0]

    def body(q_ref, k_ref, v_ref, u_ref, o_ref, lt_ref):
        i = pl.program_id(1)
        first = lax.broadcasted_iota(jnp.int32, (BLOCK, LANES), 1) < HEAD_DIM
        q2 = _scaled(q_ref[...])

        def tile(j, runs, acc, masked):
            off = pl.multiple_of(j * BLOCK, BLOCK)
            kcat = _by_head(k_ref[pl.ds(off, BLOCK), :], first)
            vcat = _by_head(v_ref[pl.ds(off, BLOCK), :], first)
            z2 = _dot_nt(q2, kcat)
            valid = _sb_valid(i, j) if masked else None
            a2, new_runs = [], []
            for h in range(2):
                _, lb, lk = _sb_logits(z2[:, h * BLOCK:(h + 1) * BLOCK], valid)
                c2 = _dot(jnp.concatenate(_split(lk), axis=1), u_ref[...])
                a = jnp.exp(lb + c2[:, :BLOCK] + runs[h])
                if masked:
                    a = jnp.where(valid, a, 0.0)
                a2.append(a.astype(BF16))
                new_runs.append(runs[h] + c2[:, BLOCK:])
            return tuple(new_runs), acc + _dot(jnp.concatenate(a2, axis=1), vcat)

        def chunk(ci, carry, masked):
            runs, acc = carry
            for t in reversed(range(SB_TILES)):
                runs, acc = tile(ci * SB_TILES + t, runs, acc, masked)
            return runs, acc

        zero = jnp.zeros((BLOCK, LANES), F32)
        nfull = i // SB_TILES
        carry = chunk(nfull, ((zero, zero), zero), True)
        (run0, run1), acc = lax.fori_loop(0, nfull, lambda t, c: chunk(nfull - 1 - t, c, False), carry)
        o_ref[...] = acc
        lt_ref[...] = jnp.where(first, run0, run1)

    blk = pl.BlockSpec((BLOCK, LANES), lambda hp, i: (i, hp))
    col = pl.BlockSpec((S, LANES), lambda hp, i: (0, hp))
    return pl.pallas_call(
        body, name="sb_fwd", grid=(GROUP // LANES, S // BLOCK),
        in_specs=[blk, col, col, _full((2 * BLOCK, 2 * BLOCK))], out_specs=[blk, blk],
        out_shape=[jax.ShapeDtypeStruct((S, GROUP), F32)] * 2,
    )(qs, ks, vs, tri_later)


def _sb_bwd(qs, ks, vs, do, ltot, tri_upto, tri_before):
    S = qs.shape[0]

    def body(q_ref, k_ref, v_ref, do_ref, lt_ref, w_ref, x_ref, dq_ref, dk_ref, dv_ref):
        i = pl.program_id(1)

        @pl.when(i == 0)
        def _():
            dk_ref[...] = jnp.zeros_like(dk_ref)
            dv_ref[...] = jnp.zeros_like(dv_ref)

        first = lax.broadcasted_iota(jnp.int32, (BLOCK, LANES), 1) < HEAD_DIM
        q2 = _scaled(q_ref[...])
        do2 = do_ref[...]
        lt = lt_ref[...]
        totals = [jnp.broadcast_to(lt[:, h * HEAD_DIM:h * HEAD_DIM + 1], (BLOCK, LANES)) for h in range(2)]

        def tile(j, keeps, grads, dq, masked):
            off = pl.multiple_of(j * BLOCK, BLOCK)
            kcat = _by_head(k_ref[pl.ds(off, BLOCK), :], first)
            vcat = _by_head(v_ref[pl.ds(off, BLOCK), :], first)
            z2 = _dot_nt(q2, kcat)
            da2 = _dot_nt(do2, vcat)
            valid = _sb_valid(i, j) if masked else None
            a2, dz2, new_keeps, new_grads = [], [], [], []
            for h in range(2):
                z = z2[:, h * BLOCK:(h + 1) * BLOCK]
                e, lb, lk = _sb_logits(z, valid)
                c2 = _dot(jnp.concatenate(_split(lk), axis=1), w_ref[...])
                a = jnp.exp(lb + (totals[h] - (keeps[h] + c2[:, :BLOCK])))
                if masked:
                    a = jnp.where(valid, a, 0.0)
                de = a * da2[:, h * BLOCK:(h + 1) * BLOCK]
                p2 = _dot(jnp.concatenate(_split(de), axis=1), x_ref[...])
                sig = jnp.where(z >= 0.0, 1.0, e) / (1.0 + e)
                dz = de * (1.0 - sig) - (grads[h] + p2[:, :BLOCK]) * sig
                if masked:
                    dz = jnp.where(valid, dz, 0.0)
                a2.append(a.astype(BF16))
                dz2.append(dz.astype(BF16))
                new_keeps.append(keeps[h] + c2[:, BLOCK:])
                new_grads.append(grads[h] + p2[:, BLOCK:])
            dzcat = jnp.concatenate(dz2, axis=1)
            dk2 = _dot_tn(dzcat, q2)
            dv2 = _dot_tn(jnp.concatenate(a2, axis=1), do2)
            dk_ref[pl.ds(off, BLOCK), :] += jnp.where(first, dk2[:BLOCK], dk2[BLOCK:])
            dv_ref[pl.ds(off, BLOCK), :] += jnp.where(first, dv2[:BLOCK], dv2[BLOCK:])
            return tuple(new_keeps), tuple(new_grads), dq + _dot(dzcat, kcat)

        def chunk(ci, carry, masked):
            for t in range(SB_TILES):
                carry = tile(ci * SB_TILES + t, *carry, masked)
            return carry

        zero = jnp.zeros((BLOCK, LANES), F32)
        nfull = i // SB_TILES
        carry = lax.fori_loop(0, nfull, lambda t, c: chunk(t, c, False), ((zero, zero), (zero, zero), zero))
        carry = chunk(nfull, carry, True)
        dq_ref[...] = carry[2] * (HEAD_DIM ** -0.5)

    blk = pl.BlockSpec((BLOCK, LANES), lambda hp, i: (i, hp))
    col = pl.BlockSpec((S, LANES), lambda hp, i: (0, hp))
    tri = _full((2 * BLOCK, 2 * BLOCK))
    return pl.pallas_call(
        body, name="sb_bwd", grid=(GROUP // LANES, S // BLOCK),
        in_specs=[blk, col, col, blk, blk, tri, tri], out_specs=[blk, col, col],
        out_shape=[jax.ShapeDtypeStruct((S, GROUP), F32)] * 3,
    )(qs, ks, vs, do, ltot, tri_upto, tri_before)


def _out_proj_fwd(o_br, l_br, o_sb, x, w_dil, w_sbn, w_out_g):
    S, D = x.shape
    tm = 512

    def body(o0, o1, o2, l0, l1, l2, os_ref, x_ref, wd_ref, ws_ref, w_ref, od_ref, lse_ref, x1_ref):
        ls = [l0[...], l1[...], l2[...]]
        m = jnp.maximum(jnp.maximum(ls[0], ls[1]), ls[2])
        es = [jnp.exp(l - m) for l in ls]
        den = es[0] + es[1] + es[2]
        od = (es[0] * o0[...] + es[1] * o1[...] + es[2] * o2[...]) / den
        od_ref[...] = od
        lse_ref[...] = m + jnp.log(den)
        osb = os_ref[...]
        mixed = jnp.concatenate([(od * _rms(od) * wd_ref[...]).astype(BF16),
                                 (osb * _rms(osb) * ws_ref[...]).astype(BF16)], axis=1)
        x1_ref[...] = x_ref[...] + _dot(mixed, w_ref[...])

    row = lambda i: (i, 0)
    g = pl.BlockSpec((tm, GROUP), row)
    d = pl.BlockSpec((tm, D), row)
    return pl.pallas_call(
        body, name="out_proj_fwd", grid=(S // tm,),
        in_specs=[g] * 7 + [d, _full((1, GROUP)), _full((1, GROUP)), _full((2 * GROUP, D))],
        out_specs=[g, g, d],
        out_shape=[jax.ShapeDtypeStruct((S, GROUP), F32)] * 2 + [jax.ShapeDtypeStruct((S, D), F32)],
    )(*o_br, *l_br, o_sb, x, w_dil, w_sbn, w_out_g)


def _ffn_fwd(x1, target, ffn_w, wg_g, wu_g, wd_g):
    S, D = x1.shape
    F = wg_g.shape[2]
    tm = 512
    nt = S // tm

    def body(x_ref, t_ref, nw_ref, wg_ref, wu_ref, wd_ref, h_ref, g_ref, u_ref, dy_ref, loss_ref, h_s, acc):
        j = pl.program_id(1)

        @pl.when(j == 0)
        def _():
            xv = x_ref[...]
            h = (xv * _rms(xv) * nw_ref[...]).astype(BF16)
            h_s[...] = h
            h_ref[...] = h
            acc[...] = xv

        h = h_s[...]
        g = _dot(h, wg_ref[0])
        u = _dot(h, wu_ref[0])
        g_ref[0] = g.astype(BF16)
        u_ref[0] = u.astype(BF16)
        a = (g * _sigmoid(g) * u).astype(BF16)
        acc[...] += _dot(a, wd_ref[0])

        @pl.when(j == N_CHIPS - 1)
        def _():
            err = acc[...] - t_ref[...]
            dy_ref[...] = err * (1.0 / D)
            loss_ref[...] = jnp.full(loss_ref.shape, jnp.sum(err * err), F32)

    row = lambda t, j: (t, 0)
    shard = lambda t, j: (j, 0, 0)
    act = lambda t, j: (j, t, 0)
    return pl.pallas_call(
        body, name="ffn_fwd", grid=(nt, N_CHIPS),
        in_specs=[pl.BlockSpec((tm, D), row), pl.BlockSpec((tm, D), row), pl.BlockSpec((1, D), lambda t, j: (0, 0)),
                  pl.BlockSpec((1, D, F), shard), pl.BlockSpec((1, D, F), shard), pl.BlockSpec((1, F, D), shard)],
        out_specs=[pl.BlockSpec((tm, D), row), pl.BlockSpec((1, tm, F), act), pl.BlockSpec((1, tm, F), act),
                   pl.BlockSpec((tm, D), row), pl.BlockSpec((1, 8, LANES), lambda t, j: (t, 0, 0))],
        out_shape=[jax.ShapeDtypeStruct((S, D), BF16), jax.ShapeDtypeStruct((N_CHIPS, S, F), BF16),
                   jax.ShapeDtypeStruct((N_CHIPS, S, F), BF16), jax.ShapeDtypeStruct((S, D), F32),
                   jax.ShapeDtypeStruct((nt, 8, LANES), F32)],
        scratch_shapes=[pltpu.VMEM((tm, D), BF16), pltpu.VMEM((tm, D), F32)],
    )(x1, target, ffn_w, wg_g, wu_g, wd_g)


def _ffn_bwd(h2, dy, g, u, wg_g, wu_g, wd_g):
    S, D = dy.shape
    F = wg_g.shape[2]
    tm = 512

    def body(h_ref, dy_ref, g_ref, u_ref, wg_ref, wu_ref, wd_ref, dwg_ref, dwu_ref, dwd_ref, dh_ref):
        t = pl.program_id(1)

        @pl.when(t == 0)
        def _():
            dwg_ref[...] = jnp.zeros_like(dwg_ref)
            dwu_ref[...] = jnp.zeros_like(dwu_ref)
            dwd_ref[...] = jnp.zeros_like(dwd_ref)

        h = h_ref[...]
        dyb = dy_ref[...].astype(BF16)
        gv = g_ref[0].astype(F32)
        uv = u_ref[0].astype(F32)
        da = _dot_nt(dyb, wd_ref[0])
        sg = _sigmoid(gv)
        silu = gv * sg
        du = (da * silu).astype(BF16)
        dg = (da * uv * (sg * (1.0 + gv * (1.0 - sg)))).astype(BF16)
        dwd_ref[0] += _dot_tn((silu * uv).astype(BF16), dyb)
        dwg_ref[0] += _dot_tn(h, dg)
        dwu_ref[0] += _dot_tn(h, du)
        dh_ref[0] = _dot_nt(dg, wg_ref[0]) + _dot_nt(du, wu_ref[0])

    row = lambda j, t: (t, 0)
    shard = lambda j, t: (j, 0, 0)
    act = lambda j, t: (j, t, 0)
    return pl.pallas_call(
        body, name="ffn_bwd", grid=(N_CHIPS, S // tm),
        in_specs=[pl.BlockSpec((tm, D), row), pl.BlockSpec((tm, D), row),
                  pl.BlockSpec((1, tm, F), act), pl.BlockSpec((1, tm, F), act),
                  pl.BlockSpec((1, D, F), shard), pl.BlockSpec((1, D, F), shard), pl.BlockSpec((1, F, D), shard)],
        out_specs=[pl.BlockSpec((1, D, F), shard), pl.BlockSpec((1, D, F), shard), pl.BlockSpec((1, F, D), shard),
                   pl.BlockSpec((1, tm, D), act)],
        out_shape=[jax.ShapeDtypeStruct((N_CHIPS, D, F), F32), jax.ShapeDtypeStruct((N_CHIPS, D, F), F32),
                   jax.ShapeDtypeStruct((N_CHIPS, F, D), F32), jax.ShapeDtypeStruct((N_CHIPS, S, D), F32)],
    )(h2, dy, g, u, wg_g, wu_g, wd_g)


def _out_proj_bwd(dh2p, dy, x1, ffn_w, w_out_g, o_dil, o_sb, w_dil, w_sbn, seg_ones):
    S, D = dy.shape
    tm = 256

    def body(dh_ref, dy_ref, x1_ref, nw_ref, w_ref, od_ref, os_ref, wd_ref, ws_ref, g_ref,
             dx1_ref, dod_ref, dos_ref, dl_ref, dw_ref, dnw_ref, dwd_ref, dws_ref):
        i = pl.program_id(0)

        @pl.when(i == 0)
        def _():
            for r_ in (dw_ref, dnw_ref, dwd_ref, dws_ref):
                r_[...] = jnp.zeros_like(r_)

        dh2 = (dh_ref[0] + dh_ref[1]) + (dh_ref[2] + dh_ref[3])
        dxn, dwn = _rms_bwd(dh2, x1_ref[...], nw_ref[...])
        dnw_ref[...] += jnp.sum(dwn, axis=0, keepdims=True)
        dx1 = dy_ref[...] + dxn
        dx1_ref[...] = dx1
        dx1b = dx1.astype(BF16)
        dmix = _dot_nt(dx1b, w_ref[...])
        od = od_ref[...]
        osb = os_ref[...]
        mixed = jnp.concatenate([(od * _rms(od) * wd_ref[...]).astype(BF16),
                                 (osb * _rms(osb) * ws_ref[...]).astype(BF16)], axis=1)
        dw_ref[...] += _dot_tn(mixed, dx1b)
        for o, wr, dm, do_r, dwr in ((od, wd_ref, dmix[:, :GROUP], dod_ref, dwd_ref),
                                     (osb, ws_ref, dmix[:, GROUP:], dos_ref, dws_ref)):
            do, dwo = _rms_bwd(dm, o, wr[...])
            dwr[...] += jnp.sum(dwo, axis=0, keepdims=True)
            do_r[...] = do.astype(BF16)
            if do_r is dod_ref:
                dl_ref[...] = _segsum(do * o, g_ref[...])

    row = lambda i: (i, 0)
    gsp = pl.BlockSpec((tm, GROUP), row)
    dsp = pl.BlockSpec((tm, D), row)
    return pl.pallas_call(
        body, name="out_proj_bwd", grid=(S // tm,),
        in_specs=[pl.BlockSpec((N_CHIPS, tm, D), lambda i: (0, i, 0)), dsp, dsp, _full((1, D)), _full((2 * GROUP, D)),
                  gsp, gsp, _full((1, GROUP)), _full((1, GROUP)), _full((GROUP, GROUP))],
        out_specs=[dsp, gsp, gsp, gsp, _full((2 * GROUP, D)), _full((1, D)), _full((1, GROUP)), _full((1, GROUP))],
        out_shape=[jax.ShapeDtypeStruct((S, D), F32), jax.ShapeDtypeStruct((S, GROUP), BF16),
                   jax.ShapeDtypeStruct((S, GROUP), BF16), jax.ShapeDtypeStruct((S, GROUP), F32),
                   jax.ShapeDtypeStruct((2 * GROUP, D), F32),
                   jax.ShapeDtypeStruct((1, D), F32), jax.ShapeDtypeStruct((1, GROUP), F32),
                   jax.ShapeDtypeStruct((1, GROUP), F32)],
    )(dh2p, dy, x1, ffn_w, w_out_g, o_dil, o_sb, w_dil, w_sbn, seg_ones)


def _qk_bwd(dq_br, dk_br, dv_br, dqs, dks, dvs, qa, ka, qw, kw, cos_t, sin_t, seg_ones):
    S = qa.shape[0]
    tm = 256

    def body(q0, q1, q2, k0, k1, k2, v0, v1, v2, dqs_ref, dks_ref, dvs_ref, qa_ref, ka_ref, qw_ref, kw_ref,
             cos_ref, sin_ref, g_ref, dp_ref, dqw_ref, dkw_ref, accq, acck):
        i = pl.program_id(0)

        @pl.when(i == 0)
        def _():
            accq[...] = jnp.zeros_like(accq)
            acck[...] = jnp.zeros_like(acck)

        g = g_ref[...]
        cos = cos_ref[...]
        sin = sin_ref[...]
        for b, (refs, pre_ref, w_ref, acc) in enumerate((((q0, q1, q2), qa_ref, qw_ref, accq),
                                                        ((k0, k1, k2), ka_ref, kw_ref, acck))):
            dh = (refs[0][...] + refs[1][...]) + refs[2][...]
            dn = dh * cos + _rot_half(dh * sin)
            pre = pre_ref[...]
            rstd = lax.rsqrt(_segsum(pre * pre, g) * (1.0 / HEAD_DIM) + EPS)
            xh = pre * rstd
            acc[...] += jnp.sum(dn * xh, axis=0, keepdims=True)
            dxh = dn * w_ref[...]
            dpre = rstd * (dxh - xh * (_segsum(dxh * xh, g) * (1.0 / HEAD_DIM)))
            dp_ref[:, b * GROUP:(b + 1) * GROUP] = dpre.astype(BF16)
        dp_ref[:, 2 * GROUP:3 * GROUP] = ((v0[...] + v1[...]) + v2[...]).astype(BF16)
        dp_ref[:, 3 * GROUP:4 * GROUP] = dqs_ref[...].astype(BF16)
        dp_ref[:, 4 * GROUP:5 * GROUP] = dks_ref[...].astype(BF16)
        dp_ref[:, 5 * GROUP:6 * GROUP] = dvs_ref[...].astype(BF16)

        @pl.when(i == S // tm - 1)
        def _():
            for acc, o_ref in ((accq, dqw_ref), (acck, dkw_ref)):
                a = acc[...]
                pair = (a[:, 0:LANES] + a[:, LANES:2 * LANES]) + (a[:, 2 * LANES:3 * LANES] + a[:, 3 * LANES:4 * LANES])
                o_ref[...] = pair + pltpu.roll(pair, HEAD_DIM, 1)

    row = lambda i: (i, 0)
    gsp = pl.BlockSpec((tm, GROUP), row)
    return pl.pallas_call(
        body, name="qk_bwd", grid=(S // tm,),
        in_specs=[gsp] * 14 + [_full((1, GROUP)), _full((1, GROUP)), gsp, gsp, _full((GROUP, GROUP))],
        out_specs=[pl.BlockSpec((tm, 6 * GROUP), row), _full((1, LANES)), _full((1, LANES))],
        out_shape=[jax.ShapeDtypeStruct((S, 6 * GROUP), BF16), jax.ShapeDtypeStruct((1, LANES), F32),
                   jax.ShapeDtypeStruct((1, LANES), F32)],
        scratch_shapes=[pltpu.VMEM((1, GROUP), F32), pltpu.VMEM((1, GROUP), F32)],
    )(*dq_br, *dk_br, *dv_br, dqs, dks, dvs, qa, ka, qw, kw, cos_t, sin_t, seg_ones)


def _in_proj_bwd(h, dproj, w_in_g):
    S, D = h.shape
    wc = w_in_g.shape[2]
    tm = 512

    def body(h_ref, dp_ref, w_ref, dw_ref, dh_ref):
        t = pl.program_id(1)

        @pl.when(t == 0)
        def _():
            dw_ref[...] = jnp.zeros_like(dw_ref)

        dp = dp_ref[...]
        dw_ref[0] += _dot_tn(h_ref[...], dp)
        dh_ref[0] = _dot_nt(dp, w_ref[0])

    return pl.pallas_call(
        body, name="in_proj_bwd", grid=(N_CHIPS, S // tm),
        in_specs=[pl.BlockSpec((tm, D), lambda j, t: (t, 0)), pl.BlockSpec((tm, wc), lambda j, t: (t, j)),
                  pl.BlockSpec((1, D, wc), lambda j, t: (j, 0, 0))],
        out_specs=[pl.BlockSpec((1, D, wc), lambda j, t: (j, 0, 0)), pl.BlockSpec((1, tm, D), lambda j, t: (j, t, 0))],
        out_shape=[jax.ShapeDtypeStruct((N_CHIPS, D, wc), F32), jax.ShapeDtypeStruct((N_CHIPS, S, D), F32)],
    )(h, dproj, w_in_g)


def _in_norm_bwd(dhp, dx1, x, attn_w):
    S, D = x.shape
    tm = 512

    def body(dh_ref, dx1_ref, x_ref, w_ref, gx_ref, dw_ref):
        i = pl.program_id(0)

        @pl.when(i == 0)
        def _():
            dw_ref[...] = jnp.zeros_like(dw_ref)

        dh = (dh_ref[0] + dh_ref[1]) + (dh_ref[2] + dh_ref[3])
        dx, dw = _rms_bwd(dh, x_ref[...], w_ref[...])
        dw_ref[...] += jnp.sum(dw, axis=0, keepdims=True)
        gx_ref[...] = dx1_ref[...] + dx

    row = lambda i: (i, 0)
    dsp = pl.BlockSpec((tm, D), row)
    return pl.pallas_call(
        body, name="in_norm_bwd", grid=(S // tm,),
        in_specs=[pl.BlockSpec((N_CHIPS, tm, D), lambda i: (0, i, 0)), dsp, dsp, _full((1, D))],
        out_specs=[dsp, _full((1, D))],
        out_shape=[jax.ShapeDtypeStruct((S, D), F32), jax.ShapeDtypeStruct((1, D), F32)],
    )(dhp, dx1, x, attn_w)


def _constants(S):
    pos = jnp.arange(S, dtype=F32)
    inv_freq = ROPE_THETA ** (-jnp.arange(0, HEAD_DIM, 2, dtype=F32) / HEAD_DIM)
    ang = pos[:, None] * inv_freq[None, :]
    cos, sin = jnp.cos(ang), jnp.sin(ang)
    reps = GROUP // HEAD_DIM
    cos_t = jnp.tile(jnp.concatenate([cos, cos], axis=1), (1, reps))
    sin_t = jnp.tile(jnp.concatenate([-sin, sin], axis=1), (1, reps))
    idx = jnp.arange(GROUP)
    seg_ones = (idx[:, None] // HEAD_DIM == idx[None, :] // HEAD_DIM).astype(BF16)
    r = jnp.arange(BLOCK)
    ones = jnp.ones((BLOCK, BLOCK), BF16)
    tris = [jnp.concatenate([jnp.concatenate([m.astype(BF16), ones], axis=1)] * 2, axis=0) for m in
            (r[:, None] > r[None, :],
             r[:, None] <= r[None, :],
             r[:, None] < r[None, :])]
    return cos_t, sin_t, seg_ones, tris


def _local_step(x, target, attn_w, qn_w, kn_w, dil_w, sbn_w, ffn_w, w_in_g, w_out_g, wg_g, wu_g, wd_g):
    S = x.shape[0]
    cos_t, sin_t, seg_ones, (tri_later, tri_upto, tri_before) = _constants(S)
    reps = GROUP // HEAD_DIM
    qw = jnp.tile(qn_w, (1, reps))
    kw = jnp.tile(kn_w, (1, reps))

    h, qa, ka, qh, kh, va, qs, ks, vs = _in_proj_fwd(x, attn_w, w_in_g, qw, kw, cos_t, sin_t, seg_ones)
    branches = [_dil_fwd(qh, kh, va, r) for r in DILATIONS]
    o_sb, ltot = _sb_fwd(qs, ks, vs, tri_later)
    o_dil, lse, x1 = _out_proj_fwd([b[0] for b in branches], [b[1] for b in branches], o_sb, x, dil_w, sbn_w, w_out_g)
    h2, g, u, dy, loss_parts = _ffn_fwd(x1, target, ffn_w, wg_g, wu_g, wd_g)

    dwg, dwu, dwd, dh2p = _ffn_bwd(h2, dy, g, u, wg_g, wu_g, wd_g)
    dx1, do_dil, do_sb, delta, dw_out, dffn_w, ddil_w, dsbn_w = _out_proj_bwd(
        dh2p, dy, x1, ffn_w, w_out_g, o_dil, o_sb, dil_w, sbn_w, seg_ones)
    dqs, dks, dvs = _sb_bwd(qs, ks, vs, do_sb, ltot, tri_upto, tri_before)
    dbr = [_dil_bwd(qh, kh, va, do_dil, lse, delta, r) for r in DILATIONS]
    dproj, dqw, dkw = _qk_bwd([b[0] for b in dbr], [b[1] for b in dbr], [b[2] for b in dbr], dqs, dks, dvs,
                              qa, ka, qw, kw, cos_t, sin_t, seg_ones)
    dw_in, dhp = _in_proj_bwd(h, dproj, w_in_g)
    grad_x, dattn_w = _in_norm_bwd(dhp, dx1, x, attn_w)
    small = dict(attn=dattn_w, q=dqw[:, :HEAD_DIM], k=dkw[:, :HEAD_DIM], dil=ddil_w, sb=dsbn_w, ffn=dffn_w)
    return loss_parts, grad_x, small, dw_in, dw_out, dwg, dwu, dwd


HBM = pl.BlockSpec(memory_space=pltpu.HBM)
VMEM = pl.BlockSpec(memory_space=pltpu.VMEM)
CHIP_FLIPS = ((1, 0), (0, 1), (1, 1))


def _place():
    return lax.axis_index("x"), lax.axis_index("y"), lax.axis_index("c")


def _flip(v, d):
    return 1 - v if d else v


def _half_rows(c, n):
    return pl.ds(pl.multiple_of(c * (n // 2), 16), n // 2)


def _gather_weights(shards):
    n = len(shards)

    def body(*refs):
        ins, outs = refs[:n], refs[n:2 * n]
        send, recv, local = refs[2 * n:]
        x, y, c = _place()
        p = 2 * x + y
        chips = [(_flip(x, dx), _flip(y, dy)) for dx, dy in CHIP_FLIPS]
        own = []
        for a in range(n):
            own.append(pltpu.make_async_copy(ins[a], outs[a].at[p], local.at[a]))
            own[-1].start()

        def copy(a, k, src, dst, to):
            return pltpu.make_async_remote_copy(src_ref=src, dst_ref=dst, send_sem=send.at[a * 6 + k],
                                                recv_sem=recv.at[a * 6 + k], device_id=to, device_id_type=MESH)

        sent = []
        for a in range(n):
            mine = _half_rows(c, ins[a].shape[0])
            for k, chip in enumerate(chips):
                sent.append(copy(a, k, ins[a].at[mine], outs[a].at[p, mine], (*chip, c)))
                sent[-1].start()
        for a in range(n):
            mine = _half_rows(c, ins[a].shape[0])
            for k, (qx, qy) in enumerate(chips):
                land = outs[a].at[2 * qx + qy, mine]
                copy(a, k, land, land, (qx, qy, c)).wait_recv()
                sent.append(copy(a, 3 + k, land, land, (x, y, 1 - c)))
                sent[-1].start()
        for a in range(n):
            other = _half_rows(1 - c, ins[a].shape[0])
            for k, (qx, qy) in enumerate(chips):
                land = outs[a].at[2 * qx + qy, other]
                copy(a, 3 + k, land, land, (x, y, 1 - c)).wait_recv()
        for cp in sent:
            cp.wait_send()
        for cp in own:
            cp.wait()

    return pl.pallas_call(
        body, name="gather_weights", in_specs=[HBM] * n, out_specs=[HBM] * n,
        out_shape=[jax.ShapeDtypeStruct((N_CHIPS,) + s.shape, s.dtype) for s in shards],
        scratch_shapes=[pltpu.SemaphoreType.DMA((6 * n,)), pltpu.SemaphoreType.DMA((6 * n,)),
                        pltpu.SemaphoreType.DMA((n,))],
    )(*shards)


def _pair_exchange(grads, small):
    n = len(grads)

    def body(*refs):
        gin, sm = refs[:n], refs[n]
        gout, sm_all = refs[n + 1:2 * n + 1], refs[2 * n + 1]
        send, recv = refs[2 * n + 2:]
        x, y, c = _place()
        me = 4 * x + 2 * y + c
        big = []
        for a in range(n):
            theirs = _half_rows(1 - c, gin[a].shape[1])
            big.append(pltpu.make_async_remote_copy(
                src_ref=gin[a].at[:, theirs, :], dst_ref=gout[a], send_sem=send.at[a], recv_sem=recv.at[a],
                device_id=(x, y, 1 - c), device_id_type=MESH))
            big[-1].start()
        sm_all[pl.ds(me, 1)] = sm[...][None]
        tiny = []
        for k in range(1, N_DEV):
            px, py, pc = _flip(x, k & 4), _flip(y, k & 2), _flip(c, k & 1)
            tiny.append((pltpu.make_async_remote_copy(
                src_ref=sm, dst_ref=sm_all.at[me], send_sem=send.at[n + k - 1], recv_sem=recv.at[n + k - 1],
                device_id=(px, py, pc), device_id_type=MESH), 4 * px + 2 * py + pc))
            tiny[-1][0].start()
        for k, (cp, peer) in enumerate(tiny):
            pltpu.make_async_remote_copy(src_ref=sm, dst_ref=sm_all.at[peer], send_sem=send.at[n + k],
                                         recv_sem=recv.at[n + k], device_id=(x, y, c),
                                         device_id_type=MESH).wait_recv()
            cp.wait_send()
        for cp in big:
            cp.wait()

    halves = [jax.ShapeDtypeStruct((g.shape[0], g.shape[1] // 2, g.shape[2]), g.dtype) for g in grads]
    return pl.pallas_call(
        body, name="pair_exchange", in_specs=[HBM] * n + [VMEM], out_specs=[HBM] * n + [VMEM],
        out_shape=halves + [jax.ShapeDtypeStruct((N_DEV,) + small.shape, small.dtype)],
        scratch_shapes=[pltpu.SemaphoreType.DMA((n + N_DEV - 1,)), pltpu.SemaphoreType.DMA((n + N_DEV - 1,))],
    )(*grads, small)


def _chip_exchange(parts):
    n = len(parts)

    def body(*refs):
        pin, pout = refs[:n], refs[n:2 * n]
        send, recv, local = refs[2 * n:]
        x, y, c = _place()
        p = 2 * x + y
        chips = [(_flip(x, dx), _flip(y, dy)) for dx, dy in CHIP_FLIPS]
        own, sent = [], []
        for a in range(n):
            own.append(pltpu.make_async_copy(pin[a].at[p], pout[a].at[p], local.at[a]))
            own[-1].start()
            for k, (qx, qy) in enumerate(chips):
                sent.append(pltpu.make_async_remote_copy(
                    src_ref=pin[a].at[2 * qx + qy], dst_ref=pout[a].at[p], send_sem=send.at[a * 3 + k],
                    recv_sem=recv.at[a * 3 + k], device_id=(qx, qy, c), device_id_type=MESH))
                sent[-1].start()
        for a in range(n):
            for k, (qx, qy) in enumerate(chips):
                land = pout[a].at[2 * qx + qy]
                pltpu.make_async_remote_copy(src_ref=land, dst_ref=land, send_sem=send.at[a * 3 + k],
                                             recv_sem=recv.at[a * 3 + k], device_id=(qx, qy, c),
                                             device_id_type=MESH).wait_recv()
        for cp in sent:
            cp.wait_send()
        for cp in own:
            cp.wait()

    return pl.pallas_call(
        body, name="chip_exchange", in_specs=[HBM] * n, out_specs=[HBM] * n,
        out_shape=[jax.ShapeDtypeStruct(s.shape, s.dtype) for s in parts],
        scratch_shapes=[pltpu.SemaphoreType.DMA((3 * n,)), pltpu.SemaphoreType.DMA((3 * n,)),
                        pltpu.SemaphoreType.DMA((n,))],
    )(*parts)


def _pair_gather(halves):
    n = len(halves)

    def body(*refs):
        hin, gout = refs[:n], refs[n:2 * n]
        send, recv, local = refs[2 * n:]
        x, y, c = _place()
        own, sent = [], []
        for a in range(n):
            mine = _half_rows(c, gout[a].shape[0])
            own.append(pltpu.make_async_copy(hin[a], gout[a].at[mine], local.at[a]))
            own[-1].start()
            sent.append(pltpu.make_async_remote_copy(
                src_ref=hin[a], dst_ref=gout[a].at[mine], send_sem=send.at[a], recv_sem=recv.at[a],
                device_id=(x, y, 1 - c), device_id_type=MESH))
            sent[-1].start()
        for a in range(n):
            land = gout[a].at[_half_rows(1 - c, gout[a].shape[0])]
            pltpu.make_async_remote_copy(src_ref=land, dst_ref=land, send_sem=send.at[a], recv_sem=recv.at[a],
                                         device_id=(x, y, 1 - c), device_id_type=MESH).wait_recv()
        for cp in sent:
            cp.wait_send()
        for cp in own:
            cp.wait()

    return pl.pallas_call(
        body, name="pair_gather", in_specs=[HBM] * n, out_specs=[HBM] * n,
        out_shape=[jax.ShapeDtypeStruct((2 * s.shape[0], s.shape[1]), s.dtype) for s in halves],
        scratch_shapes=[pltpu.SemaphoreType.DMA((n,)), pltpu.SemaphoreType.DMA((n,)), pltpu.SemaphoreType.DMA((n,))],
    )(*halves)


def _pair_sum(grad, recv, c, tag):
    _, R, C = grad.shape
    hr = R // 2

    def body(c_ref, a_ref, b_ref, o_ref):
        o_ref[...] = (a_ref[...] + b_ref[...]).astype(BF16)

    return pl.pallas_call(
        body, name="pair_sum_" + tag,
        grid_spec=pltpu.PrefetchScalarGridSpec(
            num_scalar_prefetch=1, grid=(N_CHIPS,),
            in_specs=[pl.BlockSpec((1, hr, C), lambda s, cr: (s, cr[0], 0)),
                      pl.BlockSpec((1, hr, C), lambda s, cr: (s, 0, 0))],
            out_specs=pl.BlockSpec((1, hr, C), lambda s, cr: (s, 0, 0))),
        out_shape=jax.ShapeDtypeStruct((N_CHIPS, hr, C), BF16),
    )(c, grad, recv)


def _chip_sum(parts, tag):
    _, rows, C = parts.shape
    tr = rows // 2

    def body(p_ref, o_ref):
        p = [p_ref[q].astype(F32) for q in range(N_CHIPS)]
        o_ref[...] = (p[0] + p[1]) + (p[2] + p[3])

    return pl.pallas_call(
        body, name="chip_sum_" + tag, grid=(rows // tr,),
        in_specs=[pl.BlockSpec((N_CHIPS, tr, C), lambda i: (0, i, 0))],
        out_specs=pl.BlockSpec((tr, C), lambda i: (i, 0)),
        out_shape=jax.ShapeDtypeStruct((rows, C), F32),
    )(parts)


def _adamw_math(w, g, m, v):
    m = ADAM_B1 * m + (1.0 - ADAM_B1) * g
    v = ADAM_B2 * v + (1.0 - ADAM_B2) * (g * g)
    m_hat = m / (1.0 - ADAM_B1 ** ADAM_STEP)
    v_hat = v / (1.0 - ADAM_B2 ** ADAM_STEP)
    delta = -ADAM_LR * (m_hat / (jnp.sqrt(v_hat) + ADAM_EPS) + ADAM_WD * w)
    return delta, m, v


def _adamw(w, g, m, v, tag):
    R, C = w.shape
    tr = R // 4

    def body(w_ref, g_ref, m_ref, v_ref, d_ref, nm_ref, nv_ref):
        d_ref[...], nm_ref[...], nv_ref[...] = _adamw_math(w_ref[...], g_ref[...], m_ref[...], v_ref[...])

    blk = pl.BlockSpec((tr, C), lambda i: (i, 0))
    return pl.pallas_call(
        body, name="adamw_" + tag, grid=(R // tr,), in_specs=[blk] * 4, out_specs=[blk] * 3,
        out_shape=[jax.ShapeDtypeStruct((R, C), F32)] * 3,
    )(w, g, m, v)


def _small_update(all_small, w, m, v):
    def body(a_ref, w_ref, m_ref, v_ref, g_ref, d_ref, nm_ref, nv_ref):
        g = ((a_ref[0] + a_ref[1]) + (a_ref[2] + a_ref[3])) + ((a_ref[4] + a_ref[5]) + (a_ref[6] + a_ref[7]))
        g_ref[...] = g
        d_ref[...], nm_ref[...], nv_ref[...] = _adamw_math(w_ref[...], g, m_ref[...], v_ref[...])

    return pl.pallas_call(
        body, name="small_update", out_shape=[jax.ShapeDtypeStruct(w.shape, F32)] * 4,
    )(all_small, w, m, v)


SMALL_ROWS = (("attn", 0, 0), ("ffn", 1, 0), ("dil", 2, 0), ("sb", 2, GROUP), ("q", 3, 0), ("k", 3, HEAD_DIM))


def _pack_small(vals, D):
    rows = [jnp.zeros((1, D), F32) for _ in range(8)]
    for name, r, off in SMALL_ROWS:
        rows[r] = lax.dynamic_update_slice(rows[r], vals[name].astype(F32), (0, off))
    return jnp.concatenate(rows, axis=0)


def _unpack_small(packed, vals):
    return {name: packed[r:r + 1, off:off + vals[name].shape[1]] for name, r, off in SMALL_ROWS}


def kernel(x, attn_norm_w, w_in, q_norm_w, k_norm_w, dil_out_norm_w, sb_out_norm_w, w_out, ffn_norm_w, w_gate, w_up, w_down, loss_target, m_attn_norm_w, m_w_in, m_q_norm_w, m_k_norm_w, m_dil_out_norm_w, m_sb_out_norm_w, m_w_out, m_ffn_norm_w, m_w_gate, m_w_up, m_w_down, v_attn_norm_w, v_w_in, v_q_norm_w, v_k_norm_w, v_dil_out_norm_w, v_sb_out_norm_w, v_w_out, v_ffn_norm_w, v_w_gate, v_w_up, v_w_down):
    D = x.shape[-1]
    big_names = ("w_in", "w_out", "w_gate", "w_up", "w_down")
    big_w = dict(w_in=w_in[0], w_out=w_out[0], w_gate=w_gate[0], w_up=w_up[0], w_down=w_down[0])
    big_m = dict(w_in=m_w_in[0], w_out=m_w_out[0], w_gate=m_w_gate[0], w_up=m_w_up[0], w_down=m_w_down[0])
    big_v = dict(w_in=v_w_in[0], w_out=v_w_out[0], w_gate=v_w_gate[0], w_up=v_w_up[0], w_down=v_w_down[0])
    small_w = dict(attn=attn_norm_w, q=q_norm_w, k=k_norm_w, dil=dil_out_norm_w, sb=sb_out_norm_w, ffn=ffn_norm_w)
    small_m = dict(attn=m_attn_norm_w, q=m_q_norm_w, k=m_k_norm_w, dil=m_dil_out_norm_w, sb=m_sb_out_norm_w,
                   ffn=m_ffn_norm_w)
    small_v = dict(attn=v_attn_norm_w, q=v_q_norm_w, k=v_k_norm_w, dil=v_dil_out_norm_w, sb=v_sb_out_norm_w,
                   ffn=v_ffn_norm_w)

    gathered = _gather_weights([big_w[n].astype(BF16) for n in big_names])
    w_in_g, w_out_g, wg_g, wu_g, wd_g = gathered
    w_out_full = w_out_g.reshape(-1, D)

    loss_parts, grad_x, small_g, dw_in, dw_out, dwg, dwu, dwd = _local_step(
        x[0], loss_target[0], attn_norm_w, q_norm_w, k_norm_w, dil_out_norm_w, sb_out_norm_w, ffn_norm_w,
        w_in_g, w_out_full, wg_g, wu_g, wd_g)
    loss = lax.psum(jnp.sum(loss_parts[:, 0, 0]) * (0.5 / D), ("x", "y", "c"))

    full = [dw_in, dw_out.reshape(N_CHIPS, -1, D), dwg, dwu, dwd]
    c = lax.axis_index("c").astype(jnp.int32).reshape(1)
    *from_pair, all_small = _pair_exchange(full, _pack_small(small_g, D))
    chip_parts = [_pair_sum(g, r, c, n) for g, r, n in zip(full, from_pair, big_names)]
    from_chips = _chip_exchange(chip_parts)
    halves = [_chip_sum(p, n) for p, n in zip(from_chips, big_names)]
    grads = dict(zip(big_names, _pair_gather(halves)))

    big_out = {n: _adamw(big_w[n], grads[n], big_m[n], big_v[n], n) for n in big_names}
    sg, sd, sm, sv = _small_update(all_small, _pack_small(small_w, D), _pack_small(small_m, D),
                                   _pack_small(small_v, D))
    small_out = [_unpack_small(t, small_w) for t in (sg, sd, sm, sv)]

    order = (("attn", None), (None, "w_in"), ("q", None), ("k", None), ("dil", None), ("sb", None),
             (None, "w_out"), ("ffn", None), (None, "w_gate"), (None, "w_up"), (None, "w_down"))
    outs = [loss, grad_x[None]]
    for kind in range(4):
        for s_name, b_name in order:
            if s_name is not None:
                outs.append(small_out[kind][s_name])
            elif kind == 0:
                outs.append(grads[b_name][None])
            else:
                outs.append(big_out[b_name][kind - 1][None])
    return tuple(outs)
```

```python
import functools

import jax
import jax.numpy as jnp
from jax import lax
from jax.experimental import pallas as pl
from jax.experimental.pallas import tpu as pltpu

F32 = jnp.float32
BF16 = jnp.bfloat16
MESH = pl.DeviceIdType.MESH

HEAD_DIM = 64
GROUP = 512
BLOCK = 128
LANES = 128
N_CHIPS = 4
N_DEV = 8
EPS = 1e-6
ROPE_THETA = 10000.0
DILATIONS = (1, 4, 16)
NEG = -1e30

ADAM_LR = 0.001
ADAM_B1 = 0.9
ADAM_B2 = 0.999
ADAM_EPS = 1e-08
ADAM_WD = 0.01
ADAM_STEP = 10


def _dot(a, b):
    return jnp.dot(a, b, preferred_element_type=F32)


def _dot_nt(a, b):
    return lax.dot_general(a, b, (((1,), (1,)), ((), ())), preferred_element_type=F32)


def _dot_tn(a, b):
    return lax.dot_general(a, b, (((0,), (0,)), ((), ())), preferred_element_type=F32)


def _split(v):
    hi = lax.bitcast_convert_type(lax.bitcast_convert_type(v, jnp.uint32) & jnp.uint32(0xFFFF0000), F32)
    return hi.astype(BF16), (v - hi).astype(BF16)


def _segsum(v, g):
    hi, lo = _split(v)
    return _dot(hi, g) + _dot(lo, g)


def _rot_half(x):
    outs = []
    for c in range(x.shape[1] // LANES):
        xc = x[:, c * LANES:(c + 1) * LANES]
        lane = lax.broadcasted_iota(jnp.int32, xc.shape, 1)
        first = (lane % HEAD_DIM) < (HEAD_DIM // 2)
        outs.append(jnp.where(first, pltpu.roll(xc, LANES - 32, 1), pltpu.roll(xc, 32, 1)))
    return outs[0] if len(outs) == 1 else jnp.concatenate(outs, axis=1)


def _rms(x):
    return lax.rsqrt(jnp.mean(x * x, axis=-1, keepdims=True) + EPS)


def _rms_bwd(dy, x, w):
    rstd = _rms(x)
    xh = x * rstd
    dxh = dy * w
    dx = rstd * (dxh - xh * jnp.mean(dxh * xh, axis=-1, keepdims=True))
    return dx, dy * xh


def _sigmoid(x):
    return 1.0 / (1.0 + jnp.exp(-x))


def _full(shape):
    n = len(shape)
    return pl.BlockSpec(shape, lambda *_: (0,) * n)


def _in_proj_fwd(x, attn_w, w_in_g, qw, kw, cos_t, sin_t, seg_ones):
    S, D = x.shape
    tm = 512
    wcols = w_in_g.shape[2]

    def body(x_ref, aw_ref, w_ref, qw_ref, kw_ref, cos_ref, sin_ref, g_ref,
             h_ref, qa_ref, ka_ref, qh_ref, kh_ref, va_ref, qs_ref, ks_ref, vs_ref):
        xv = x_ref[...]
        h = (xv * _rms(xv) * aw_ref[...]).astype(BF16)
        h_ref[...] = h
        proj = jnp.concatenate([_dot(h, w_ref[j]) for j in range(N_CHIPS)], axis=1)
        qa = proj[:, 0 * GROUP:1 * GROUP]
        ka = proj[:, 1 * GROUP:2 * GROUP]
        qa_ref[...] = qa
        ka_ref[...] = ka
        va_ref[...] = proj[:, 2 * GROUP:3 * GROUP].astype(BF16)
        qs_ref[...] = proj[:, 3 * GROUP:4 * GROUP].astype(BF16)
        ks_ref[...] = proj[:, 4 * GROUP:5 * GROUP].astype(BF16)
        vs_ref[...] = proj[:, 5 * GROUP:6 * GROUP].astype(BF16)
        g = g_ref[...]
        cos = cos_ref[...]
        sin = sin_ref[...]
        for t, w_r, o_r in ((qa, qw_ref, qh_ref), (ka, kw_ref, kh_ref)):
            rstd = lax.rsqrt(_segsum(t * t, g) * (1.0 / HEAD_DIM) + EPS)
            tn = t * rstd * w_r[...]
            o_r[...] = (tn * cos + _rot_half(tn) * sin).astype(BF16)

    row = lambda i: (i, 0)
    tile = lambda n, dt: jax.ShapeDtypeStruct((S, n), dt)
    return pl.pallas_call(
        body, name="in_proj_fwd", grid=(S // tm,),
        in_specs=[pl.BlockSpec((tm, D), row), _full((1, D)), _full((N_CHIPS, D, wcols)),
                  _full((1, GROUP)), _full((1, GROUP)),
                  pl.BlockSpec((tm, GROUP), row), pl.BlockSpec((tm, GROUP), row),
                  _full((GROUP, GROUP))],
        out_specs=[pl.BlockSpec((tm, D), row)] + [pl.BlockSpec((tm, GROUP), row)] * 8,
        out_shape=[tile(D, BF16), tile(GROUP, F32), tile(GROUP, F32)] + [tile(GROUP, BF16)] * 6,
    )(x, attn_w, w_in_g, qw, kw, cos_t, sin_t, seg_ones)


def _dil_view(a, r):
    return a.reshape(a.shape[0] // r, r * a.shape[1])


def _dil_fwd(qh, kh, va, r):
    S = qh.shape[0]
    nb = S // r // BLOCK

    def body(q_ref, kc_ref, kp_ref, vc_ref, vp_ref, o_ref, l_ref):
        n = pl.program_id(1)
        rowi = lax.broadcasted_iota(jnp.int32, (BLOCK, BLOCK), 0)
        coli = lax.broadcasted_iota(jnp.int32, (BLOCK, BLOCK), 1)
        m_cur = coli <= rowi
        m_prev = jnp.logical_and(coli >= rowi, n > 0)
        for hp in range(GROUP // LANES):
            sl = slice(hp * LANES, (hp + 1) * LANES)
            q2 = q_ref[:, sl]
            kc, kp, vc, vp = kc_ref[:, sl], kp_ref[:, sl], vc_ref[:, sl], vp_ref[:, sl]
            o_acc = jnp.zeros((BLOCK, LANES), F32)
            l_acc = jnp.zeros((BLOCK, LANES), F32)
            for hh in range(2):
                hmask = (coli // HEAD_DIM) == hh
                qm = jnp.where(hmask, q2, jnp.zeros_like(q2))
                s_c = jnp.where(m_cur, _dot_nt(qm, kc) * 0.125, NEG)
                s_p = jnp.where(m_prev, _dot_nt(qm, kp) * 0.125, NEG)
                m = jnp.maximum(jnp.max(s_c, axis=1, keepdims=True), jnp.max(s_p, axis=1, keepdims=True))
                p_c = jnp.exp(s_c - m)
                p_p = jnp.exp(s_p - m)
                den = jnp.sum(p_c, axis=1, keepdims=True) + jnp.sum(p_p, axis=1, keepdims=True)
                o = (_dot(p_c.astype(BF16), vc) + _dot(p_p.astype(BF16), vp)) / den
                o_acc = jnp.where(hmask, o, o_acc)
                l_acc = jnp.where(hmask, m + jnp.log(den), l_acc)
            o_ref[:, sl] = o_acc
            l_ref[:, sl] = l_acc

    cur = pl.BlockSpec((BLOCK, GROUP), lambda c, n: (n, c))
    prev = pl.BlockSpec((BLOCK, GROUP), lambda c, n: (jnp.maximum(n - 1, 0), c))
    qv, kv, vv = _dil_view(qh, r), _dil_view(kh, r), _dil_view(va, r)
    o, l = pl.pallas_call(
        body, name="dil_fwd_r%d" % r, grid=(r, nb),
        in_specs=[cur, cur, prev, cur, prev], out_specs=[cur, cur],
        out_shape=[jax.ShapeDtypeStruct(qv.shape, F32)] * 2,
    )(qv, kv, kv, vv, vv)
    return o.reshape(S, GROUP), l.reshape(S, GROUP)


def _dil_bwd(qh, kh, va, do, lse, delta, r):
    S = qh.shape[0]
    nb = S // r // BLOCK

    def body(qc_ref, qn_ref, doc_ref, don_ref, lc_ref, ln_ref, dc_ref, dn_ref, k_ref, v_ref,
             dq_ref, dk_ref, dv_ref, carry):
        j = pl.program_id(1)
        rowi = lax.broadcasted_iota(jnp.int32, (BLOCK, BLOCK), 0)
        coli = lax.broadcasted_iota(jnp.int32, (BLOCK, BLOCK), 1)
        m_cur = coli <= rowi
        m_next = jnp.logical_and(coli >= rowi, j < nb - 1)

        @pl.when(j == 0)
        def _():
            carry[...] = jnp.zeros_like(carry)

        for hp in range(GROUP // LANES):
            sl = slice(hp * LANES, (hp + 1) * LANES)
            k2, v2 = k_ref[:, sl], v_ref[:, sl]
            dk_acc = jnp.zeros((BLOCK, LANES), F32)
            dv_acc = jnp.zeros((BLOCK, LANES), F32)
            dq_c = jnp.zeros((BLOCK, LANES), F32)
            dq_n = jnp.zeros((BLOCK, LANES), F32)
            for hh in range(2):
                hmask = (coli // HEAD_DIM) == hh
                col = hp * LANES + hh * HEAD_DIM
                parts = []
                for q_r, do_r, l_r, d_r, msk in ((qc_ref, doc_ref, lc_ref, dc_ref, m_cur),
                                                 (qn_ref, don_ref, ln_ref, dn_ref, m_next)):
                    q2 = q_r[:, sl]
                    do2 = do_r[:, sl]
                    qm = jnp.where(hmask, q2, jnp.zeros_like(q2))
                    dom = jnp.where(hmask, do2, jnp.zeros_like(do2))
                    s = _dot_nt(qm, k2) * 0.125
                    p = jnp.where(msk, jnp.exp(s - l_r[:, col:col + 1]), 0.0)
                    dp = _dot_nt(dom, v2)
                    ds = (p * (dp - d_r[:, col:col + 1]) * 0.125).astype(BF16)
                    dv_acc = dv_acc + _dot_tn(p.astype(BF16), dom)
                    dk_acc = dk_acc + _dot_tn(ds, qm)
                    parts.append(_dot(ds, k2))
                dq_c = jnp.where(hmask, parts[0], dq_c)
                dq_n = jnp.where(hmask, parts[1], dq_n)
            dq_ref[:, sl] = carry[:, sl] + dq_c
            carry[:, sl] = dq_n
            dk_ref[:, sl] = dk_acc
            dv_ref[:, sl] = dv_acc

    cur = pl.BlockSpec((BLOCK, GROUP), lambda c, n: (n, c))
    nxt = pl.BlockSpec((BLOCK, GROUP), lambda c, n: (jnp.minimum(n + 1, nb - 1), c))
    v = lambda a: _dil_view(a, r)
    outs = pl.pallas_call(
        body, name="dil_bwd_r%d" % r, grid=(r, nb),
        in_specs=[cur, nxt, cur, nxt, cur, nxt, cur, nxt, cur, cur], out_specs=[cur, cur, cur],
        out_shape=[jax.ShapeDtypeStruct(v(qh).shape, F32)] * 3,
        scratch_shapes=[pltpu.VMEM((BLOCK, GROUP), F32)],
    )(v(qh), v(qh), v(do), v(do), v(lse), v(lse), v(delta), v(delta), v(kh), v(va))
    return [o.reshape(S, GROUP) for o in outs]


SB_TILES = 2
SB_PAIRS_FWD = 4
SB_PAIRS_BWD = 2


def _lanes(hp):
    return slice(hp * LANES, (hp + 1) * LANES)


def _sb_logits(z, valid):
    e = jnp.exp(-jnp.abs(z))
    lb = jnp.minimum(z, 0.0) - jnp.log(1.0 + e)
    lk = lb - z
    if valid is not None:
        lk = jnp.where(valid, lk, 0.0)
    return e, lb, lk


def _by_head(t, first):
    zero = jnp.zeros_like(t)
    return jnp.concatenate([jnp.where(first, t, zero), jnp.where(first, zero, t)], axis=0)


def _sb_valid(i, j):
    rowi = lax.broadcasted_iota(jnp.int32, (BLOCK, BLOCK), 0)
    coli = lax.broadcasted_iota(jnp.int32, (BLOCK, BLOCK), 1)
    return (coli - rowi) < (i - j) * BLOCK


def _scaled(q):
    return (q.astype(F32) * (HEAD_DIM ** -0.5)).astype(BF16)


def _sb_fwd(qs, ks, vs, tri_later):
    S = qs.shape[0]
    P = SB_PAIRS_FWD
    W = P * LANES

    def body(q_ref, k_ref, v_ref, u_ref, o_ref, lt_ref):
        i = pl.program_id(1)
        first = lax.broadcasted_iota(jnp.int32, (BLOCK, LANES), 1) < HEAD_DIM
        q2 = [_scaled(q_ref[:, _lanes(hp)]) for hp in range(P)]

        def chunk(ci, carry, masked):
            runs, accs = list(carry[0]), list(carry[1])
            units = [(t, hp) for t in reversed(range(SB_TILES)) for hp in range(P)]
            z2s, lbs, c2s = {}, {}, {}
            for t, hp in units:
                off = pl.multiple_of((ci * SB_TILES + t) * BLOCK, BLOCK)
                z2s[t, hp] = _dot_nt(q2[hp], _by_head(k_ref[pl.ds(off, BLOCK), _lanes(hp)], first))
            for t, hp in units:
                valid = _sb_valid(i, ci * SB_TILES + t) if masked else None
                for h in range(2):
                    _, lb, lk = _sb_logits(z2s[t, hp][:, h * BLOCK:(h + 1) * BLOCK], valid)
                    lbs[t, hp, h] = lb
                    c2s[t, hp, h] = _dot(jnp.concatenate(_split(lk), axis=1), u_ref[...])
            for t, hp in units:
                off = pl.multiple_of((ci * SB_TILES + t) * BLOCK, BLOCK)
                valid = _sb_valid(i, ci * SB_TILES + t) if masked else None
                a2 = []
                for h in range(2):
                    a = jnp.exp(lbs[t, hp, h] + c2s[t, hp, h][:, :BLOCK] + runs[2 * hp + h])
                    if masked:
                        a = jnp.where(valid, a, 0.0)
                    a2.append(a.astype(BF16))
                    runs[2 * hp + h] = runs[2 * hp + h] + c2s[t, hp, h][:, BLOCK:]
                vcat = _by_head(v_ref[pl.ds(off, BLOCK), _lanes(hp)], first)
                accs[hp] = accs[hp] + _dot(jnp.concatenate(a2, axis=1), vcat)
            return tuple(runs), tuple(accs)

        zero = jnp.zeros((BLOCK, LANES), F32)
        nfull = i // SB_TILES
        carry = chunk(nfull, ((zero,) * (2 * P), (zero,) * P), True)
        runs, accs = lax.fori_loop(0, nfull, lambda t, c: chunk(nfull - 1 - t, c, False), carry)
        for hp in range(P):
            o_ref[:, _lanes(hp)] = accs[hp]
            lt_ref[:, _lanes(hp)] = jnp.where(first, runs[2 * hp], runs[2 * hp + 1])

    blk = pl.BlockSpec((BLOCK, W), lambda hp, i: (i, hp))
    col = pl.BlockSpec((S, W), lambda hp, i: (0, hp))
    return pl.pallas_call(
        body, name="sb_fwd", grid=(GROUP // W, S // BLOCK),
        in_specs=[blk, col, col, _full((2 * BLOCK, 2 * BLOCK))], out_specs=[blk, blk],
        out_shape=[jax.ShapeDtypeStruct((S, GROUP), F32)] * 2,
    )(qs, ks, vs, tri_later)


def _sb_bwd(qs, ks, vs, do, ltot, tri_upto, tri_before):
    S = qs.shape[0]
    P = SB_PAIRS_BWD
    W = P * LANES

    def body(q_ref, k_ref, v_ref, do_ref, lt_ref, w_ref, x_ref, dq_ref, dk_ref, dv_ref):
        i = pl.program_id(1)

        @pl.when(i == 0)
        def _():
            dk_ref[...] = jnp.zeros_like(dk_ref)
            dv_ref[...] = jnp.zeros_like(dv_ref)

        first = lax.broadcasted_iota(jnp.int32, (BLOCK, LANES), 1) < HEAD_DIM
        q2 = [_scaled(q_ref[:, _lanes(hp)]) for hp in range(P)]
        do2 = [do_ref[:, _lanes(hp)] for hp in range(P)]
        totals = [jnp.broadcast_to(lt_ref[:, n * HEAD_DIM:n * HEAD_DIM + 1], (BLOCK, LANES)) for n in range(2 * P)]

        def chunk(ci, carry, masked):
            keeps, grads, dqs = list(carry[0]), list(carry[1]), list(carry[2])
            units = [(t, hp) for t in range(SB_TILES) for hp in range(P)]
            offs = [pl.multiple_of((ci * SB_TILES + t) * BLOCK, BLOCK) for t in range(SB_TILES)]
            valids = [_sb_valid(i, ci * SB_TILES + t) if masked else None for t in range(SB_TILES)]
            kcat, z2, da2, es, lbs, c2s, as_, des, p2s = {}, {}, {}, {}, {}, {}, {}, {}, {}
            for t, hp in units:
                kcat[t, hp] = _by_head(k_ref[pl.ds(offs[t], BLOCK), _lanes(hp)], first)
                z2[t, hp] = _dot_nt(q2[hp], kcat[t, hp])
                da2[t, hp] = _dot_nt(do2[hp], _by_head(v_ref[pl.ds(offs[t], BLOCK), _lanes(hp)], first))
            for t, hp in units:
                for h in range(2):
                    es[t, hp, h], lbs[t, hp, h], lk = _sb_logits(z2[t, hp][:, h * BLOCK:(h + 1) * BLOCK], valids[t])
                    c2s[t, hp, h] = _dot(jnp.concatenate(_split(lk), axis=1), w_ref[...])
            for t, hp in units:
                for h in range(2):
                    n = 2 * hp + h
                    a = jnp.exp(lbs[t, hp, h] + (totals[n] - (keeps[n] + c2s[t, hp, h][:, :BLOCK])))
                    if masked:
                        a = jnp.where(valids[t], a, 0.0)
                    keeps[n] = keeps[n] + c2s[t, hp, h][:, BLOCK:]
                    de = a * da2[t, hp][:, h * BLOCK:(h + 1) * BLOCK]
                    as_[t, hp, h], des[t, hp, h] = a.astype(BF16), de
                    p2s[t, hp, h] = _dot(jnp.concatenate(_split(de), axis=1), x_ref[...])
            for t, hp in units:
                dz2 = []
                for h in range(2):
                    n = 2 * hp + h
                    e = es[t, hp, h]
                    sig = jnp.where(z2[t, hp][:, h * BLOCK:(h + 1) * BLOCK] >= 0.0, 1.0, e) / (1.0 + e)
                    dz = des[t, hp, h] * (1.0 - sig) - (grads[n] + p2s[t, hp, h][:, :BLOCK]) * sig
                    if masked:
                        dz = jnp.where(valids[t], dz, 0.0)
                    grads[n] = grads[n] + p2s[t, hp, h][:, BLOCK:]
                    dz2.append(dz.astype(BF16))
                dzcat = jnp.concatenate(dz2, axis=1)
                dk2 = _dot_tn(dzcat, q2[hp])
                dv2 = _dot_tn(jnp.concatenate([as_[t, hp, 0], as_[t, hp, 1]], axis=1), do2[hp])
                dk_ref[pl.ds(offs[t], BLOCK), _lanes(hp)] += jnp.where(first, dk2[:BLOCK], dk2[BLOCK:])
                dv_ref[pl.ds(offs[t], BLOCK), _lanes(hp)] += jnp.where(first, dv2[:BLOCK], dv2[BLOCK:])
                dqs[hp] = dqs[hp] + _dot(dzcat, kcat[t, hp])
            return tuple(keeps), tuple(grads), tuple(dqs)

        zero = jnp.zeros((BLOCK, LANES), F32)
        nfull = i // SB_TILES
        carry = lax.fori_loop(0, nfull, lambda t, c: chunk(t, c, False),
                              ((zero,) * (2 * P), (zero,) * (2 * P), (zero,) * P))
        carry = chunk(nfull, carry, True)
        for hp in range(P):
            dq_ref[:, _lanes(hp)] = carry[2][hp] * (HEAD_DIM ** -0.5)

    blk = pl.BlockSpec((BLOCK, W), lambda hp, i: (i, hp))
    col = pl.BlockSpec((S, W), lambda hp, i: (0, hp))
    tri = _full((2 * BLOCK, 2 * BLOCK))
    return pl.pallas_call(
        body, name="sb_bwd", grid=(GROUP // W, S // BLOCK),
        in_specs=[blk, col, col, blk, blk, tri, tri], out_specs=[blk, col, col],
        out_shape=[jax.ShapeDtypeStruct((S, GROUP), F32)] * 3,
    )(qs, ks, vs, do, ltot, tri_upto, tri_before)


def _out_proj_fwd(o_br, l_br, o_sb, x, w_dil, w_sbn, w_out_g):
    S, D = x.shape
    tm = 512

    def body(o0, o1, o2, l0, l1, l2, os_ref, x_ref, wd_ref, ws_ref, w_ref, od_ref, lse_ref, x1_ref):
        ls = [l0[...], l1[...], l2[...]]
        m = jnp.maximum(jnp.maximum(ls[0], ls[1]), ls[2])
        es = [jnp.exp(l - m) for l in ls]
        den = es[0] + es[1] + es[2]
        od = (es[0] * o0[...] + es[1] * o1[...] + es[2] * o2[...]) / den
        od_ref[...] = od
        lse_ref[...] = m + jnp.log(den)
        osb = os_ref[...]
        mixed = jnp.concatenate([(od * _rms(od) * wd_ref[...]).astype(BF16),
                                 (osb * _rms(osb) * ws_ref[...]).astype(BF16)], axis=1)
        x1_ref[...] = x_ref[...] + _dot(mixed, w_ref[...])

    row = lambda i: (i, 0)
    g = pl.BlockSpec((tm, GROUP), row)
    d = pl.BlockSpec((tm, D), row)
    return pl.pallas_call(
        body, name="out_proj_fwd", grid=(S // tm,),
        in_specs=[g] * 7 + [d, _full((1, GROUP)), _full((1, GROUP)), _full((2 * GROUP, D))],
        out_specs=[g, g, d],
        out_shape=[jax.ShapeDtypeStruct((S, GROUP), F32)] * 2 + [jax.ShapeDtypeStruct((S, D), F32)],
    )(*o_br, *l_br, o_sb, x, w_dil, w_sbn, w_out_g)


def _ffn_fwd(x1, target, ffn_w, wg_g, wu_g, wd_g):
    S, D = x1.shape
    F = wg_g.shape[2]
    tm = 512
    nt = S // tm

    def body(x_ref, t_ref, nw_ref, wg_ref, wu_ref, wd_ref, h_ref, g_ref, u_ref, dy_ref, loss_ref, h_s, acc):
        j = pl.program_id(1)

        @pl.when(j == 0)
        def _():
            xv = x_ref[...]
            h = (xv * _rms(xv) * nw_ref[...]).astype(BF16)
            h_s[...] = h
            h_ref[...] = h
            acc[...] = xv

        h = h_s[...]
        g = _dot(h, wg_ref[0])
        u = _dot(h, wu_ref[0])
        g_ref[0] = g.astype(BF16)
        u_ref[0] = u.astype(BF16)
        a = (g * _sigmoid(g) * u).astype(BF16)
        acc[...] += _dot(a, wd_ref[0])

        @pl.when(j == N_CHIPS - 1)
        def _():
            err = acc[...] - t_ref[...]
            dy_ref[...] = err * (1.0 / D)
            loss_ref[...] = jnp.full(loss_ref.shape, jnp.sum(err * err), F32)

    row = lambda t, j: (t, 0)
    shard = lambda t, j: (j, 0, 0)
    act = lambda t, j: (j, t, 0)
    return pl.pallas_call(
        body, name="ffn_fwd", grid=(nt, N_CHIPS),
        in_specs=[pl.BlockSpec((tm, D), row), pl.BlockSpec((tm, D), row), pl.BlockSpec((1, D), lambda t, j: (0, 0)),
                  pl.BlockSpec((1, D, F), shard), pl.BlockSpec((1, D, F), shard), pl.BlockSpec((1, F, D), shard)],
        out_specs=[pl.BlockSpec((tm, D), row), pl.BlockSpec((1, tm, F), act), pl.BlockSpec((1, tm, F), act),
                   pl.BlockSpec((tm, D), row), pl.BlockSpec((1, 8, LANES), lambda t, j: (t, 0, 0))],
        out_shape=[jax.ShapeDtypeStruct((S, D), BF16), jax.ShapeDtypeStruct((N_CHIPS, S, F), BF16),
                   jax.ShapeDtypeStruct((N_CHIPS, S, F), BF16), jax.ShapeDtypeStruct((S, D), F32),
                   jax.ShapeDtypeStruct((nt, 8, LANES), F32)],
        scratch_shapes=[pltpu.VMEM((tm, D), BF16), pltpu.VMEM((tm, D), F32)],
    )(x1, target, ffn_w, wg_g, wu_g, wd_g)


def _ffn_bwd(h2, dy, g, u, wg_g, wu_g, wd_g):
    S, D = dy.shape
    F = wg_g.shape[2]
    tm = 512

    def body(h_ref, dy_ref, g_ref, u_ref, wg_ref, wu_ref, wd_ref, dwg_ref, dwu_ref, dwd_ref, dh_ref):
        t = pl.program_id(1)

        @pl.when(t == 0)
        def _():
            dwg_ref[...] = jnp.zeros_like(dwg_ref)
            dwu_ref[...] = jnp.zeros_like(dwu_ref)
            dwd_ref[...] = jnp.zeros_like(dwd_ref)

        h = h_ref[...]
        dyb = dy_ref[...].astype(BF16)
        gv = g_ref[0].astype(F32)
        uv = u_ref[0].astype(F32)
        da = _dot_nt(dyb, wd_ref[0])
        sg = _sigmoid(gv)
        silu = gv * sg
        du = (da * silu).astype(BF16)
        dg = (da * uv * (sg * (1.0 + gv * (1.0 - sg)))).astype(BF16)
        dwd_ref[0] += _dot_tn((silu * uv).astype(BF16), dyb)
        dwg_ref[0] += _dot_tn(h, dg)
        dwu_ref[0] += _dot_tn(h, du)
        dh_ref[0] = _dot_nt(dg, wg_ref[0]) + _dot_nt(du, wu_ref[0])

    row = lambda j, t: (t, 0)
    shard = lambda j, t: (j, 0, 0)
    act = lambda j, t: (j, t, 0)
    return pl.pallas_call(
        body, name="ffn_bwd", grid=(N_CHIPS, S // tm),
        in_specs=[pl.BlockSpec((tm, D), row), pl.BlockSpec((tm, D), row),
                  pl.BlockSpec((1, tm, F), act), pl.BlockSpec((1, tm, F), act),
                  pl.BlockSpec((1, D, F), shard), pl.BlockSpec((1, D, F), shard), pl.BlockSpec((1, F, D), shard)],
        out_specs=[pl.BlockSpec((1, D, F), shard), pl.BlockSpec((1, D, F), shard), pl.BlockSpec((1, F, D), shard),
                   pl.BlockSpec((1, tm, D), act)],
        out_shape=[jax.ShapeDtypeStruct((N_CHIPS, D, F), F32), jax.ShapeDtypeStruct((N_CHIPS, D, F), F32),
                   jax.ShapeDtypeStruct((N_CHIPS, F, D), F32), jax.ShapeDtypeStruct((N_CHIPS, S, D), F32)],
    )(h2, dy, g, u, wg_g, wu_g, wd_g)


def _out_proj_bwd(dh2p, dy, x1, ffn_w, w_out_g, o_dil, o_sb, w_dil, w_sbn, seg_ones):
    S, D = dy.shape
    tm = 256

    def body(dh_ref, dy_ref, x1_ref, nw_ref, w_ref, od_ref, os_ref, wd_ref, ws_ref, g_ref,
             dx1_ref, dod_ref, dos_ref, dl_ref, dw_ref, dnw_ref, dwd_ref, dws_ref):
        i = pl.program_id(0)

        @pl.when(i == 0)
        def _():
            for r_ in (dw_ref, dnw_ref, dwd_ref, dws_ref):
                r_[...] = jnp.zeros_like(r_)

        dh2 = (dh_ref[0] + dh_ref[1]) + (dh_ref[2] + dh_ref[3])
        dxn, dwn = _rms_bwd(dh2, x1_ref[...], nw_ref[...])
        dnw_ref[...] += jnp.sum(dwn, axis=0, keepdims=True)
        dx1 = dy_ref[...] + dxn
        dx1_ref[...] = dx1
        dx1b = dx1.astype(BF16)
        dmix = _dot_nt(dx1b, w_ref[...])
        od = od_ref[...]
        osb = os_ref[...]
        mixed = jnp.concatenate([(od * _rms(od) * wd_ref[...]).astype(BF16),
                                 (osb * _rms(osb) * ws_ref[...]).astype(BF16)], axis=1)
        dw_ref[...] += _dot_tn(mixed, dx1b)
        for o, wr, dm, do_r, dwr in ((od, wd_ref, dmix[:, :GROUP], dod_ref, dwd_ref),
                                     (osb, ws_ref, dmix[:, GROUP:], dos_ref, dws_ref)):
            do, dwo = _rms_bwd(dm, o, wr[...])
            dwr[...] += jnp.sum(dwo, axis=0, keepdims=True)
            do_r[...] = do.astype(BF16)
            if do_r is dod_ref:
                dl_ref[...] = _segsum(do * o, g_ref[...])

    row = lambda i: (i, 0)
    gsp = pl.BlockSpec((tm, GROUP), row)
    dsp = pl.BlockSpec((tm, D), row)
    return pl.pallas_call(
        body, name="out_proj_bwd", grid=(S // tm,),
        in_specs=[pl.BlockSpec((N_CHIPS, tm, D), lambda i: (0, i, 0)), dsp, dsp, _full((1, D)), _full((2 * GROUP, D)),
                  gsp, gsp, _full((1, GROUP)), _full((1, GROUP)), _full((GROUP, GROUP))],
        out_specs=[dsp, gsp, gsp, gsp, _full((2 * GROUP, D)), _full((1, D)), _full((1, GROUP)), _full((1, GROUP))],
        out_shape=[jax.ShapeDtypeStruct((S, D), F32), jax.ShapeDtypeStruct((S, GROUP), BF16),
                   jax.ShapeDtypeStruct((S, GROUP), BF16), jax.ShapeDtypeStruct((S, GROUP), F32),
                   jax.ShapeDtypeStruct((2 * GROUP, D), F32),
                   jax.ShapeDtypeStruct((1, D), F32), jax.ShapeDtypeStruct((1, GROUP), F32),
                   jax.ShapeDtypeStruct((1, GROUP), F32)],
    )(dh2p, dy, x1, ffn_w, w_out_g, o_dil, o_sb, w_dil, w_sbn, seg_ones)


def _qk_bwd(dq_br, dk_br, dv_br, dqs, dks, dvs, qa, ka, qw, kw, cos_t, sin_t, seg_ones):
    S = qa.shape[0]
    tm = 256

    def body(q0, q1, q2, k0, k1, k2, v0, v1, v2, dqs_ref, dks_ref, dvs_ref, qa_ref, ka_ref, qw_ref, kw_ref,
             cos_ref, sin_ref, g_ref, dp_ref, dqw_ref, dkw_ref, accq, acck):
        i = pl.program_id(0)

        @pl.when(i == 0)
        def _():
            accq[...] = jnp.zeros_like(accq)
            acck[...] = jnp.zeros_like(acck)

        g = g_ref[...]
        cos = cos_ref[...]
        sin = sin_ref[...]
        for b, (refs, pre_ref, w_ref, acc) in enumerate((((q0, q1, q2), qa_ref, qw_ref, accq),
                                                        ((k0, k1, k2), ka_ref, kw_ref, acck))):
            dh = (refs[0][...] + refs[1][...]) + refs[2][...]
            dn = dh * cos + _rot_half(dh * sin)
            pre = pre_ref[...]
            rstd = lax.rsqrt(_segsum(pre * pre, g) * (1.0 / HEAD_DIM) + EPS)
            xh = pre * rstd
            acc[...] += jnp.sum(dn * xh, axis=0, keepdims=True)
            dxh = dn * w_ref[...]
            dpre = rstd * (dxh - xh * (_segsum(dxh * xh, g) * (1.0 / HEAD_DIM)))
            dp_ref[:, b * GROUP:(b + 1) * GROUP] = dpre.astype(BF16)
        dp_ref[:, 2 * GROUP:3 * GROUP] = ((v0[...] + v1[...]) + v2[...]).astype(BF16)
        dp_ref[:, 3 * GROUP:4 * GROUP] = dqs_ref[...].astype(BF16)
        dp_ref[:, 4 * GROUP:5 * GROUP] = dks_ref[...].astype(BF16)
        dp_ref[:, 5 * GROUP:6 * GROUP] = dvs_ref[...].astype(BF16)

        @pl.when(i == S // tm - 1)
        def _():
            for acc, o_ref in ((accq, dqw_ref), (acck, dkw_ref)):
                a = acc[...]
                pair = (a[:, 0:LANES] + a[:, LANES:2 * LANES]) + (a[:, 2 * LANES:3 * LANES] + a[:, 3 * LANES:4 * LANES])
                o_ref[...] = pair + pltpu.roll(pair, HEAD_DIM, 1)

    row = lambda i: (i, 0)
    gsp = pl.BlockSpec((tm, GROUP), row)
    return pl.pallas_call(
        body, name="qk_bwd", grid=(S // tm,),
        in_specs=[gsp] * 14 + [_full((1, GROUP)), _full((1, GROUP)), gsp, gsp, _full((GROUP, GROUP))],
        out_specs=[pl.BlockSpec((tm, 6 * GROUP), row), _full((1, LANES)), _full((1, LANES))],
        out_shape=[jax.ShapeDtypeStruct((S, 6 * GROUP), BF16), jax.ShapeDtypeStruct((1, LANES), F32),
                   jax.ShapeDtypeStruct((1, LANES), F32)],
        scratch_shapes=[pltpu.VMEM((1, GROUP), F32), pltpu.VMEM((1, GROUP), F32)],
    )(*dq_br, *dk_br, *dv_br, dqs, dks, dvs, qa, ka, qw, kw, cos_t, sin_t, seg_ones)


def _in_proj_bwd(h, dproj, w_in_g):
    S, D = h.shape
    wc = w_in_g.shape[2]
    tm = 512

    def body(h_ref, dp_ref, w_ref, dw_ref, dh_ref):
        t = pl.program_id(1)

        @pl.when(t == 0)
        def _():
            dw_ref[...] = jnp.zeros_like(dw_ref)

        dp = dp_ref[...]
        dw_ref[0] += _dot_tn(h_ref[...], dp)
        dh_ref[0] = _dot_nt(dp, w_ref[0])

    return pl.pallas_call(
        body, name="in_proj_bwd", grid=(N_CHIPS, S // tm),
        in_specs=[pl.BlockSpec((tm, D), lambda j, t: (t, 0)), pl.BlockSpec((tm, wc), lambda j, t: (t, j)),
                  pl.BlockSpec((1, D, wc), lambda j, t: (j, 0, 0))],
        out_specs=[pl.BlockSpec((1, D, wc), lambda j, t: (j, 0, 0)), pl.BlockSpec((1, tm, D), lambda j, t: (j, t, 0))],
        out_shape=[jax.ShapeDtypeStruct((N_CHIPS, D, wc), F32), jax.ShapeDtypeStruct((N_CHIPS, S, D), F32)],
    )(h, dproj, w_in_g)


def _in_norm_bwd(dhp, dx1, x, attn_w):
    S, D = x.shape
    tm = 512

    def body(dh_ref, dx1_ref, x_ref, w_ref, gx_ref, dw_ref):
        i = pl.program_id(0)

        @pl.when(i == 0)
        def _():
            dw_ref[...] = jnp.zeros_like(dw_ref)

        dh = (dh_ref[0] + dh_ref[1]) + (dh_ref[2] + dh_ref[3])
        dx, dw = _rms_bwd(dh, x_ref[...], w_ref[...])
        dw_ref[...] += jnp.sum(dw, axis=0, keepdims=True)
        gx_ref[...] = dx1_ref[...] + dx

    row = lambda i: (i, 0)
    dsp = pl.BlockSpec((tm, D), row)
    return pl.pallas_call(
        body, name="in_norm_bwd", grid=(S // tm,),
        in_specs=[pl.BlockSpec((N_CHIPS, tm, D), lambda i: (0, i, 0)), dsp, dsp, _full((1, D))],
        out_specs=[dsp, _full((1, D))],
        out_shape=[jax.ShapeDtypeStruct((S, D), F32), jax.ShapeDtypeStruct((1, D), F32)],
    )(dhp, dx1, x, attn_w)


def _constants(S):
    pos = jnp.arange(S, dtype=F32)
    inv_freq = ROPE_THETA ** (-jnp.arange(0, HEAD_DIM, 2, dtype=F32) / HEAD_DIM)
    ang = pos[:, None] * inv_freq[None, :]
    cos, sin = jnp.cos(ang), jnp.sin(ang)
    reps = GROUP // HEAD_DIM
    cos_t = jnp.tile(jnp.concatenate([cos, cos], axis=1), (1, reps))
    sin_t = jnp.tile(jnp.concatenate([-sin, sin], axis=1), (1, reps))
    idx = jnp.arange(GROUP)
    seg_ones = (idx[:, None] // HEAD_DIM == idx[None, :] // HEAD_DIM).astype(BF16)
    r = jnp.arange(BLOCK)
    ones = jnp.ones((BLOCK, BLOCK), BF16)
    tris = [jnp.concatenate([jnp.concatenate([m.astype(BF16), ones], axis=1)] * 2, axis=0) for m in
            (r[:, None] > r[None, :],
             r[:, None] <= r[None, :],
             r[:, None] < r[None, :])]
    return cos_t, sin_t, seg_ones, tris


def _local_step(x, target, attn_w, qn_w, kn_w, dil_w, sbn_w, ffn_w, w_in_g, w_out_g, wg_g, wu_g, wd_g):
    S = x.shape[0]
    cos_t, sin_t, seg_ones, (tri_later, tri_upto, tri_before) = _constants(S)
    reps = GROUP // HEAD_DIM
    qw = jnp.tile(qn_w, (1, reps))
    kw = jnp.tile(kn_w, (1, reps))

    h, qa, ka, qh, kh, va, qs, ks, vs = _in_proj_fwd(x, attn_w, w_in_g, qw, kw, cos_t, sin_t, seg_ones)
    branches = [_dil_fwd(qh, kh, va, r) for r in DILATIONS]
    o_sb, ltot = _sb_fwd(qs, ks, vs, tri_later)
    o_dil, lse, x1 = _out_proj_fwd([b[0] for b in branches], [b[1] for b in branches], o_sb, x, dil_w, sbn_w, w_out_g)
    h2, g, u, dy, loss_parts = _ffn_fwd(x1, target, ffn_w, wg_g, wu_g, wd_g)

    dwg, dwu, dwd, dh2p = _ffn_bwd(h2, dy, g, u, wg_g, wu_g, wd_g)
    dx1, do_dil, do_sb, delta, dw_out, dffn_w, ddil_w, dsbn_w = _out_proj_bwd(
        dh2p, dy, x1, ffn_w, w_out_g, o_dil, o_sb, dil_w, sbn_w, seg_ones)
    dqs, dks, dvs = _sb_bwd(qs, ks, vs, do_sb, ltot, tri_upto, tri_before)
    dbr = [_dil_bwd(qh, kh, va, do_dil, lse, delta, r) for r in DILATIONS]
    dproj, dqw, dkw = _qk_bwd([b[0] for b in dbr], [b[1] for b in dbr], [b[2] for b in dbr], dqs, dks, dvs,
                              qa, ka, qw, kw, cos_t, sin_t, seg_ones)
    dw_in, dhp = _in_proj_bwd(h, dproj, w_in_g)
    grad_x, dattn_w = _in_norm_bwd(dhp, dx1, x, attn_w)
    small = dict(attn=dattn_w, q=dqw[:, :HEAD_DIM], k=dkw[:, :HEAD_DIM], dil=ddil_w, sb=dsbn_w, ffn=dffn_w)
    return loss_parts, grad_x, small, dw_in, dw_out, dwg, dwu, dwd


HBM = pl.BlockSpec(memory_space=pltpu.HBM)
VMEM = pl.BlockSpec(memory_space=pltpu.VMEM)
CHIP_FLIPS = ((1, 0), (0, 1), (1, 1))


def _place():
    return lax.axis_index("x"), lax.axis_index("y"), lax.axis_index("c")


def _flip(v, d):
    return 1 - v if d else v


def _half_rows(c, n):
    return pl.ds(pl.multiple_of(c * (n // 2), 16), n // 2)


def _gather_weights(shards):
    n = len(shards)

    def body(*refs):
        ins, outs = refs[:n], refs[n:2 * n]
        send, recv, local = refs[2 * n:]
        x, y, c = _place()
        p = 2 * x + y
        chips = [(_flip(x, dx), _flip(y, dy)) for dx, dy in CHIP_FLIPS]
        own = []
        for a in range(n):
            own.append(pltpu.make_async_copy(ins[a], outs[a].at[p], local.at[a]))
            own[-1].start()

        def copy(a, k, src, dst, to):
            return pltpu.make_async_remote_copy(src_ref=src, dst_ref=dst, send_sem=send.at[a * 6 + k],
                                                recv_sem=recv.at[a * 6 + k], device_id=to, device_id_type=MESH)

        sent = []
        for a in range(n):
            mine = _half_rows(c, ins[a].shape[0])
            for k, chip in enumerate(chips):
                sent.append(copy(a, k, ins[a].at[mine], outs[a].at[p, mine], (*chip, c)))
                sent[-1].start()
        for a in range(n):
            mine = _half_rows(c, ins[a].shape[0])
            for k, (qx, qy) in enumerate(chips):
                land = outs[a].at[2 * qx + qy, mine]
                copy(a, k, land, land, (qx, qy, c)).wait_recv()
                sent.append(copy(a, 3 + k, land, land, (x, y, 1 - c)))
                sent[-1].start()
        for a in range(n):
            other = _half_rows(1 - c, ins[a].shape[0])
            for k, (qx, qy) in enumerate(chips):
                land = outs[a].at[2 * qx + qy, other]
                copy(a, 3 + k, land, land, (x, y, 1 - c)).wait_recv()
        for cp in sent:
            cp.wait_send()
        for cp in own:
            cp.wait()

    return pl.pallas_call(
        body, name="gather_weights", in_specs=[HBM] * n, out_specs=[HBM] * n,
        out_shape=[jax.ShapeDtypeStruct((N_CHIPS,) + s.shape, s.dtype) for s in shards],
        scratch_shapes=[pltpu.SemaphoreType.DMA((6 * n,)), pltpu.SemaphoreType.DMA((6 * n,)),
                        pltpu.SemaphoreType.DMA((n,))],
    )(*shards)


def _pair_exchange(grads, small):
    n = len(grads)

    def body(*refs):
        gin, sm = refs[:n], refs[n]
        gout, sm_all = refs[n + 1:2 * n + 1], refs[2 * n + 1]
        send, recv = refs[2 * n + 2:]
        x, y, c = _place()
        me = 4 * x + 2 * y + c
        big = []
        for a in range(n):
            theirs = _half_rows(1 - c, gin[a].shape[1])
            big.append(pltpu.make_async_remote_copy(
                src_ref=gin[a].at[:, theirs, :], dst_ref=gout[a], send_sem=send.at[a], recv_sem=recv.at[a],
                device_id=(x, y, 1 - c), device_id_type=MESH))
            big[-1].start()
        sm_all[pl.ds(me, 1)] = sm[...][None]
        tiny = []
        for k in range(1, N_DEV):
            px, py, pc = _flip(x, k & 4), _flip(y, k & 2), _flip(c, k & 1)
            tiny.append((pltpu.make_async_remote_copy(
                src_ref=sm, dst_ref=sm_all.at[me], send_sem=send.at[n + k - 1], recv_sem=recv.at[n + k - 1],
                device_id=(px, py, pc), device_id_type=MESH), 4 * px + 2 * py + pc))
            tiny[-1][0].start()
        for k, (cp, peer) in enumerate(tiny):
            pltpu.make_async_remote_copy(src_ref=sm, dst_ref=sm_all.at[peer], send_sem=send.at[n + k],
                                         recv_sem=recv.at[n + k], device_id=(x, y, c),
                                         device_id_type=MESH).wait_recv()
            cp.wait_send()
        for cp in big:
            cp.wait()

    halves = [jax.ShapeDtypeStruct((g.shape[0], g.shape[1] // 2, g.shape[2]), g.dtype) for g in grads]
    return pl.pallas_call(
        body, name="pair_exchange", in_specs=[HBM] * n + [VMEM], out_specs=[HBM] * n + [VMEM],
        out_shape=halves + [jax.ShapeDtypeStruct((N_DEV,) + small.shape, small.dtype)],
        scratch_shapes=[pltpu.SemaphoreType.DMA((n + N_DEV - 1,)), pltpu.SemaphoreType.DMA((n + N_DEV - 1,))],
    )(*grads, small)


def _chip_exchange(parts):
    n = len(parts)

    def body(*refs):
        pin, pout = refs[:n], refs[n:2 * n]
        send, recv, local = refs[2 * n:]
        x, y, c = _place()
        p = 2 * x + y
        chips = [(_flip(x, dx), _flip(y, dy)) for dx, dy in CHIP_FLIPS]
        own, sent = [], []
        for a in range(n):
            own.append(pltpu.make_async_copy(pin[a].at[p], pout[a].at[p], local.at[a]))
            own[-1].start()
            for k, (qx, qy) in enumerate(chips):
                sent.append(pltpu.make_async_remote_copy(
                    src_ref=pin[a].at[2 * qx + qy], dst_ref=pout[a].at[p], send_sem=send.at[a * 3 + k],
                    recv_sem=recv.at[a * 3 + k], device_id=(qx, qy, c), device_id_type=MESH))
                sent[-1].start()
        for a in range(n):
            for k, (qx, qy) in enumerate(chips):
                land = pout[a].at[2 * qx + qy]
                pltpu.make_async_remote_copy(src_ref=land, dst_ref=land, send_sem=send.at[a * 3 + k],
                                             recv_sem=recv.at[a * 3 + k], device_id=(qx, qy, c),
                                             device_id_type=MESH).wait_recv()
        for cp in sent:
            cp.wait_send()
        for cp in own:
            cp.wait()

    return pl.pallas_call(
        body, name="chip_exchange", in_specs=[HBM] * n, out_specs=[HBM] * n,
        out_shape=[jax.ShapeDtypeStruct(s.shape, s.dtype) for s in parts],
        scratch_shapes=[pltpu.SemaphoreType.DMA((3 * n,)), pltpu.SemaphoreType.DMA((3 * n,)),
                        pltpu.SemaphoreType.DMA((n,))],
    )(*parts)


def _pair_swap(halves):
    n = len(halves)

    def body(*refs):
        hin, hout = refs[:n], refs[n:2 * n]
        send, recv = refs[2 * n:]
        x, y, c = _place()
        swaps = [pltpu.make_async_remote_copy(src_ref=hin[a], dst_ref=hout[a], send_sem=send.at[a],
                                              recv_sem=recv.at[a], device_id=(x, y, 1 - c), device_id_type=MESH)
                 for a in range(n)]
        for cp in swaps:
            cp.start()
        for cp in swaps:
            cp.wait()

    return pl.pallas_call(
        body, name="pair_swap", in_specs=[HBM] * n, out_specs=[HBM] * n,
        out_shape=[jax.ShapeDtypeStruct(s.shape, s.dtype) for s in halves],
        scratch_shapes=[pltpu.SemaphoreType.DMA((n,)), pltpu.SemaphoreType.DMA((n,))],
    )(*halves)


def _pair_sum(grad, recv, c, tag):
    _, R, C = grad.shape
    hr = R // 2

    def body(c_ref, a_ref, b_ref, o_ref):
        o_ref[...] = (a_ref[...] + b_ref[...]).astype(BF16)

    return pl.pallas_call(
        body, name="pair_sum_" + tag,
        grid_spec=pltpu.PrefetchScalarGridSpec(
            num_scalar_prefetch=1, grid=(N_CHIPS,),
            in_specs=[pl.BlockSpec((1, hr, C), lambda s, cr: (s, cr[0], 0)),
                      pl.BlockSpec((1, hr, C), lambda s, cr: (s, 0, 0))],
            out_specs=pl.BlockSpec((1, hr, C), lambda s, cr: (s, 0, 0))),
        out_shape=jax.ShapeDtypeStruct((N_CHIPS, hr, C), BF16),
    )(c, grad, recv)


def _chip_sum(parts, tag):
    _, rows, C = parts.shape
    tr = rows // 2

    def body(p_ref, o_ref):
        p = [p_ref[q].astype(F32) for q in range(N_CHIPS)]
        o_ref[...] = (p[0] + p[1]) + (p[2] + p[3])

    return pl.pallas_call(
        body, name="chip_sum_" + tag, grid=(rows // tr,),
        in_specs=[pl.BlockSpec((N_CHIPS, tr, C), lambda i: (0, i, 0))],
        out_specs=pl.BlockSpec((tr, C), lambda i: (i, 0)),
        out_shape=jax.ShapeDtypeStruct((rows, C), F32),
    )(parts)


def _adamw_math(w, g, m, v):
    m = ADAM_B1 * m + (1.0 - ADAM_B1) * g
    v = ADAM_B2 * v + (1.0 - ADAM_B2) * (g * g)
    m_hat = m / (1.0 - ADAM_B1 ** ADAM_STEP)
    v_hat = v / (1.0 - ADAM_B2 ** ADAM_STEP)
    delta = -ADAM_LR * (m_hat / (jnp.sqrt(v_hat) + ADAM_EPS) + ADAM_WD * w)
    return delta, m, v


def _adamw(w, g_mine, g_other, m, v, c, tag):
    R, C = w.shape
    tr = R // 4

    def body(c_ref, w_ref, gm_ref, go_ref, m_ref, v_ref, g_ref, d_ref, nm_ref, nv_ref):
        g = jnp.where(pl.program_id(0) == c_ref[0], gm_ref[...], go_ref[...])
        g_ref[...] = g
        d_ref[...], nm_ref[...], nv_ref[...] = _adamw_math(w_ref[...], g, m_ref[...], v_ref[...])

    blk = pl.BlockSpec((tr, C), lambda h, i, cr: (2 * h + i, 0))
    half = pl.BlockSpec((tr, C), lambda h, i, cr: (i, 0))
    return pl.pallas_call(
        body, name="adamw_" + tag,
        grid_spec=pltpu.PrefetchScalarGridSpec(
            num_scalar_prefetch=1, grid=(2, 2), in_specs=[blk, half, half, blk, blk], out_specs=[blk] * 4),
        out_shape=[jax.ShapeDtypeStruct((R, C), F32)] * 4,
    )(c, w, g_mine, g_other, m, v)


def _small_update(all_small, w, m, v):
    def body(a_ref, w_ref, m_ref, v_ref, g_ref, d_ref, nm_ref, nv_ref):
        g = ((a_ref[0] + a_ref[1]) + (a_ref[2] + a_ref[3])) + ((a_ref[4] + a_ref[5]) + (a_ref[6] + a_ref[7]))
        g_ref[...] = g
        d_ref[...], nm_ref[...], nv_ref[...] = _adamw_math(w_ref[...], g, m_ref[...], v_ref[...])

    return pl.pallas_call(
        body, name="small_update", out_shape=[jax.ShapeDtypeStruct(w.shape, F32)] * 4,
    )(all_small, w, m, v)


SMALL_ROWS = (("attn", 0, 0), ("ffn", 1, 0), ("dil", 2, 0), ("sb", 2, GROUP), ("q", 3, 0), ("k", 3, HEAD_DIM))


def _pack_small(vals, D):
    rows = [jnp.zeros((1, D), F32) for _ in range(8)]
    for name, r, off in SMALL_ROWS:
        rows[r] = lax.dynamic_update_slice(rows[r], vals[name].astype(F32), (0, off))
    return jnp.concatenate(rows, axis=0)


def _unpack_small(packed, vals):
    return {name: packed[r:r + 1, off:off + vals[name].shape[1]] for name, r, off in SMALL_ROWS}


def kernel(x, attn_norm_w, w_in, q_norm_w, k_norm_w, dil_out_norm_w, sb_out_norm_w, w_out, ffn_norm_w, w_gate, w_up, w_down, loss_target, m_attn_norm_w, m_w_in, m_q_norm_w, m_k_norm_w, m_dil_out_norm_w, m_sb_out_norm_w, m_w_out, m_ffn_norm_w, m_w_gate, m_w_up, m_w_down, v_attn_norm_w, v_w_in, v_q_norm_w, v_k_norm_w, v_dil_out_norm_w, v_sb_out_norm_w, v_w_out, v_ffn_norm_w, v_w_gate, v_w_up, v_w_down):
    D = x.shape[-1]
    big_names = ("w_in", "w_out", "w_gate", "w_up", "w_down")
    big_w = dict(w_in=w_in[0], w_out=w_out[0], w_gate=w_gate[0], w_up=w_up[0], w_down=w_down[0])
    big_m = dict(w_in=m_w_in[0], w_out=m_w_out[0], w_gate=m_w_gate[0], w_up=m_w_up[0], w_down=m_w_down[0])
    big_v = dict(w_in=v_w_in[0], w_out=v_w_out[0], w_gate=v_w_gate[0], w_up=v_w_up[0], w_down=v_w_down[0])
    small_w = dict(attn=attn_norm_w, q=q_norm_w, k=k_norm_w, dil=dil_out_norm_w, sb=sb_out_norm_w, ffn=ffn_norm_w)
    small_m = dict(attn=m_attn_norm_w, q=m_q_norm_w, k=m_k_norm_w, dil=m_dil_out_norm_w, sb=m_sb_out_norm_w,
                   ffn=m_ffn_norm_w)
    small_v = dict(attn=v_attn_norm_w, q=v_q_norm_w, k=v_k_norm_w, dil=v_dil_out_norm_w, sb=v_sb_out_norm_w,
                   ffn=v_ffn_norm_w)

    gathered = _gather_weights([big_w[n].astype(BF16) for n in big_names])
    w_in_g, w_out_g, wg_g, wu_g, wd_g = gathered
    w_out_full = w_out_g.reshape(-1, D)

    loss_parts, grad_x, small_g, dw_in, dw_out, dwg, dwu, dwd = _local_step(
        x[0], loss_target[0], attn_norm_w, q_norm_w, k_norm_w, dil_out_norm_w, sb_out_norm_w, ffn_norm_w,
        w_in_g, w_out_full, wg_g, wu_g, wd_g)
    loss = lax.psum(jnp.sum(loss_parts[:, 0, 0]) * (0.5 / D), ("x", "y", "c"))

    full = [dw_in, dw_out.reshape(N_CHIPS, -1, D), dwg, dwu, dwd]
    c = lax.axis_index("c").astype(jnp.int32).reshape(1)
    *from_pair, all_small = _pair_exchange(full, _pack_small(small_g, D))
    chip_parts = [_pair_sum(g, r, c, n) for g, r, n in zip(full, from_pair, big_names)]
    from_chips = _chip_exchange(chip_parts)
    halves = [_chip_sum(p, n) for p, n in zip(from_chips, big_names)]
    others = _pair_swap(halves)
    big_out = {n: _adamw(big_w[n], mine, other, big_m[n], big_v[n], c, n)
               for n, mine, other in zip(big_names, halves, others)}
    sg, sd, sm, sv = _small_update(all_small, _pack_small(small_w, D), _pack_small(small_m, D),
                                   _pack_small(small_v, D))
    small_out = [_unpack_small(t, small_w) for t in (sg, sd, sm, sv)]

    order = (("attn", None), (None, "w_in"), ("q", None), ("k", None), ("dil", None), ("sb", None),
             (None, "w_out"), ("ffn", None), (None, "w_gate"), (None, "w_up"), (None, "w_down"))
    outs = [loss, grad_x[None]]
    for kind in range(4):
        for s_name, b_name in order:
            if s_name is not None:
                outs.append(small_out[kind][s_name])
            else:
                outs.append(big_out[b_name][kind][None])
    return tuple(outs)
```

```python
import functools

import jax
import jax.numpy as jnp
from jax import lax
from jax.experimental import pallas as pl
from jax.experimental.pallas import tpu as pltpu

F32 = jnp.float32
BF16 = jnp.bfloat16
MESH = pl.DeviceIdType.MESH

HEAD_DIM = 64
GROUP = 512
BLOCK = 128
LANES = 128
N_CHIPS = 4
N_DEV = 8
EPS = 1e-6
ROPE_THETA = 10000.0
DILATIONS = (1, 4, 16)
NEG = -1e30

ADAM_LR = 0.001
ADAM_B1 = 0.9
ADAM_B2 = 0.999
ADAM_EPS = 1e-08
ADAM_WD = 0.01
ADAM_STEP = 10


def _dot(a, b):
    return jnp.dot(a, b, preferred_element_type=F32)


def _dot_nt(a, b):
    return lax.dot_general(a, b, (((1,), (1,)), ((), ())), preferred_element_type=F32)


def _dot_tn(a, b):
    return lax.dot_general(a, b, (((0,), (0,)), ((), ())), preferred_element_type=F32)


def _split(v):
    hi = lax.bitcast_convert_type(lax.bitcast_convert_type(v, jnp.uint32) & jnp.uint32(0xFFFF0000), F32)
    return hi.astype(BF16), (v - hi).astype(BF16)


def _segsum(v, g):
    hi, lo = _split(v)
    return _dot(hi, g) + _dot(lo, g)


def _rot_half(x):
    outs = []
    for c in range(x.shape[1] // LANES):
        xc = x[:, c * LANES:(c + 1) * LANES]
        lane = lax.broadcasted_iota(jnp.int32, xc.shape, 1)
        first = (lane % HEAD_DIM) < (HEAD_DIM // 2)
        outs.append(jnp.where(first, pltpu.roll(xc, LANES - 32, 1), pltpu.roll(xc, 32, 1)))
    return outs[0] if len(outs) == 1 else jnp.concatenate(outs, axis=1)


def _rms(x):
    return lax.rsqrt(jnp.mean(x * x, axis=-1, keepdims=True) + EPS)


def _rms_bwd(dy, x, w):
    rstd = _rms(x)
    xh = x * rstd
    dxh = dy * w
    dx = rstd * (dxh - xh * jnp.mean(dxh * xh, axis=-1, keepdims=True))
    return dx, dy * xh


def _sigmoid(x):
    return 1.0 / (1.0 + jnp.exp(-x))


def _full(shape):
    n = len(shape)
    return pl.BlockSpec(shape, lambda *_: (0,) * n)


def _strided_spec(tm, r):
    return pl.BlockSpec((r, tm // r, GROUP), lambda i: (0, i, 0))


def _strided_shape(S, r, dtype):
    return jax.ShapeDtypeStruct((r, S // r, GROUP), dtype)


def _to_strided(scr, val, outs):
    chunks = range(GROUP // LANES)
    for k in chunks:
        scr[k] = val[:, _lanes(k)]
    for r, o_ref in outs:
        if r == 1:
            o_ref[0] = val.astype(o_ref.dtype)
            continue
        n = val.shape[0] // r
        for c in range(r):
            rows = pl.ds(c, n, stride=r)
            o_ref[c] = jnp.concatenate([scr.at[k][rows, :] for k in chunks], axis=1).astype(o_ref.dtype)


def _from_strided(scr, ref):
    r, n, _ = ref.shape
    if r == 1:
        return ref[0]
    chunks = range(GROUP // LANES)
    for c in range(r):
        plane = ref[c]
        for k in chunks:
            scr.at[k][pl.ds(c, n, stride=r), :] = plane[:, _lanes(k)]
    return jnp.concatenate([scr[k] for k in chunks], axis=1)


def _strided_scratch(tm):
    return pltpu.VMEM((GROUP // LANES, tm, LANES), F32)


def _tile4(t):
    return jnp.concatenate([t] * (GROUP // LANES), axis=1)


def _in_proj_fwd(x, attn_w, w_in_g, qw, kw, cos_t, sin_t, seg_ones):
    S, D = x.shape
    tm = 512
    wcols = w_in_g.shape[2]
    nd = len(DILATIONS)

    def body(x_ref, aw_ref, w_ref, qw_ref, kw_ref, cos_ref, sin_ref, g_ref, h_ref, qa_ref, ka_ref, *rest):
        q_refs, k_refs, v_refs = rest[:nd], rest[nd:2 * nd], rest[2 * nd:3 * nd]
        qs_ref, ks_ref, vs_ref, scr = rest[3 * nd:]
        xv = x_ref[...]
        h = (xv * _rms(xv) * aw_ref[...]).astype(BF16)
        h_ref[...] = h
        proj = jnp.concatenate([_dot(h, w_ref[j]) for j in range(N_CHIPS)], axis=1)
        qa = proj[:, 0 * GROUP:1 * GROUP]
        ka = proj[:, 1 * GROUP:2 * GROUP]
        qa_ref[...] = qa
        ka_ref[...] = ka
        _to_strided(scr, proj[:, 2 * GROUP:3 * GROUP], list(zip(DILATIONS, v_refs)))
        qs_ref[...] = proj[:, 3 * GROUP:4 * GROUP].astype(BF16)
        ks_ref[...] = proj[:, 4 * GROUP:5 * GROUP].astype(BF16)
        vs_ref[...] = proj[:, 5 * GROUP:6 * GROUP].astype(BF16)
        g = g_ref[...]
        cos = _tile4(cos_ref[...])
        sin = _tile4(sin_ref[...])
        for t, w_r, o_rs in ((qa, qw_ref, q_refs), (ka, kw_ref, k_refs)):
            rstd = lax.rsqrt(_segsum(t * t, g) * (1.0 / HEAD_DIM) + EPS)
            tn = t * rstd * w_r[...]
            _to_strided(scr, tn * cos + _rot_half(tn) * sin, list(zip(DILATIONS, o_rs)))

    row = lambda i: (i, 0)
    tile = lambda n, dt: jax.ShapeDtypeStruct((S, n), dt)
    planes = [_strided_spec(tm, r) for r in DILATIONS]
    return pl.pallas_call(
        body, name="in_proj_fwd", grid=(S // tm,),
        in_specs=[pl.BlockSpec((tm, D), row), _full((1, D)), _full((N_CHIPS, D, wcols)),
                  _full((1, GROUP)), _full((1, GROUP)),
                  pl.BlockSpec((tm, LANES), row), pl.BlockSpec((tm, LANES), row),
                  _full((GROUP, GROUP))],
        out_specs=[pl.BlockSpec((tm, D), row)] + [pl.BlockSpec((tm, GROUP), row)] * 2 + planes * 3
                  + [pl.BlockSpec((tm, GROUP), row)] * 3,
        out_shape=[tile(D, BF16), tile(GROUP, F32), tile(GROUP, F32)]
                  + [_strided_shape(S, r, BF16) for r in DILATIONS] * 3 + [tile(GROUP, BF16)] * 3,
        scratch_shapes=[_strided_scratch(tm)],
    )(x, attn_w, w_in_g, qw, kw, cos_t, sin_t, seg_ones)


def _dil_fwd(q, k, v):
    r, L, _ = q.shape
    nb = L // BLOCK
    P = GROUP // LANES

    def body(q_ref, kc_ref, kp_ref, vc_ref, vp_ref, o_ref, l_ref):
        n = pl.program_id(1)
        rowi = lax.broadcasted_iota(jnp.int32, (BLOCK, BLOCK), 0)
        coli = lax.broadcasted_iota(jnp.int32, (BLOCK, BLOCK), 1)
        first = coli < HEAD_DIM
        masks = (coli <= rowi, jnp.logical_and(coli >= rowi, n > 0))
        s2 = {}
        for hp in range(P):
            q2 = _scaled(q_ref[:, _lanes(hp)])
            for b, k_ref in enumerate((kc_ref, kp_ref)):
                s2[hp, b] = _dot_nt(q2, _by_head(k_ref[:, _lanes(hp)], first))
        ps, inv, lse = {}, {}, {}
        for hp in range(P):
            for h in range(2):
                s = [jnp.where(masks[b], s2[hp, b][:, h * BLOCK:(h + 1) * BLOCK], NEG) for b in range(2)]
                m = jnp.maximum(jnp.max(s[0], axis=1, keepdims=True), jnp.max(s[1], axis=1, keepdims=True))
                p = [jnp.exp(s[b] - m) for b in range(2)]
                den = jnp.sum(p[0], axis=1, keepdims=True) + jnp.sum(p[1], axis=1, keepdims=True)
                ps[hp, h] = [p[b].astype(BF16) for b in range(2)]
                inv[hp, h] = 1.0 / den
                lse[hp, h] = m + jnp.log(den)
        for hp in range(P):
            o = jnp.zeros((BLOCK, LANES), F32)
            for b, v_ref in enumerate((vc_ref, vp_ref)):
                o = o + _dot(jnp.concatenate([ps[hp, 0][b], ps[hp, 1][b]], axis=1),
                             _by_head(v_ref[:, _lanes(hp)], first))
            o_ref[:, _lanes(hp)] = o * jnp.where(first, inv[hp, 0], inv[hp, 1])
            l_ref[:, _lanes(hp)] = jnp.where(first, lse[hp, 0], lse[hp, 1])

    cur = pl.BlockSpec((None, BLOCK, GROUP), lambda c, n: (c, n, 0))
    prev = pl.BlockSpec((None, BLOCK, GROUP), lambda c, n: (c, jnp.maximum(n - 1, 0), 0))
    return pl.pallas_call(
        body, name="dil_fwd_r%d" % r, grid=(r, nb),
        in_specs=[cur, cur, prev, cur, prev], out_specs=[cur, cur],
        out_shape=[jax.ShapeDtypeStruct(q.shape, F32)] * 2,
    )(q, k, k, v, v)


def _dil_bwd(q, k, v, do, lse, delta):
    r, L, _ = q.shape
    nb = L // BLOCK
    P = GROUP // LANES
    scale = HEAD_DIM ** -0.5

    def body(qc_ref, qn_ref, doc_ref, don_ref, lc_ref, ln_ref, dc_ref, dn_ref, k_ref, v_ref,
             dq_ref, dk_ref, dv_ref, carry):
        j = pl.program_id(1)
        rowi = lax.broadcasted_iota(jnp.int32, (BLOCK, BLOCK), 0)
        coli = lax.broadcasted_iota(jnp.int32, (BLOCK, BLOCK), 1)
        first = coli < HEAD_DIM
        sides = ((qc_ref, doc_ref, lc_ref, dc_ref, coli <= rowi),
                 (qn_ref, don_ref, ln_ref, dn_ref, jnp.logical_and(coli >= rowi, j < nb - 1)))

        @pl.when(j == 0)
        def _():
            carry[...] = jnp.zeros_like(carry)

        kcat, q2, do2, s2, dp2 = {}, {}, {}, {}, {}
        for hp in range(P):
            kcat[hp] = _by_head(k_ref[:, _lanes(hp)], first)
            vcat = _by_head(v_ref[:, _lanes(hp)], first)
            for x, (q_r, do_r, _, _, _) in enumerate(sides):
                q2[hp, x] = _scaled(q_r[:, _lanes(hp)])
                do2[hp, x] = do_r[:, _lanes(hp)]
                s2[hp, x] = _dot_nt(q2[hp, x], kcat[hp])
                dp2[hp, x] = _dot_nt(do2[hp, x], vcat)
        pcat, dscat = {}, {}
        for hp in range(P):
            for x, (_, _, l_r, d_r, msk) in enumerate(sides):
                ps, dss = [], []
                for h in range(2):
                    col = hp * LANES + h * HEAD_DIM
                    half = slice(h * BLOCK, (h + 1) * BLOCK)
                    p = jnp.where(msk, jnp.exp(s2[hp, x][:, half] - l_r[:, col:col + 1]), 0.0)
                    ps.append(p.astype(BF16))
                    dss.append((p * (dp2[hp, x][:, half] - d_r[:, col:col + 1])).astype(BF16))
                pcat[hp, x] = jnp.concatenate(ps, axis=1)
                dscat[hp, x] = jnp.concatenate(dss, axis=1)
        for hp in range(P):
            dv2 = _dot_tn(pcat[hp, 0], do2[hp, 0]) + _dot_tn(pcat[hp, 1], do2[hp, 1])
            dk2 = _dot_tn(dscat[hp, 0], q2[hp, 0]) + _dot_tn(dscat[hp, 1], q2[hp, 1])
            dv_ref[:, _lanes(hp)] = jnp.where(first, dv2[:BLOCK], dv2[BLOCK:])
            dk_ref[:, _lanes(hp)] = jnp.where(first, dk2[:BLOCK], dk2[BLOCK:])
            dq_ref[:, _lanes(hp)] = carry[:, _lanes(hp)] + _dot(dscat[hp, 0], kcat[hp]) * scale
            carry[:, _lanes(hp)] = _dot(dscat[hp, 1], kcat[hp]) * scale

    cur = pl.BlockSpec((None, BLOCK, GROUP), lambda c, n: (c, n, 0))
    nxt = pl.BlockSpec((None, BLOCK, GROUP), lambda c, n: (c, jnp.minimum(n + 1, nb - 1), 0))
    return pl.pallas_call(
        body, name="dil_bwd_r%d" % r, grid=(r, nb),
        in_specs=[cur, nxt, cur, nxt, cur, nxt, cur, nxt, cur, cur], out_specs=[cur, cur, cur],
        out_shape=[jax.ShapeDtypeStruct(q.shape, F32)] * 3,
        scratch_shapes=[pltpu.VMEM((BLOCK, GROUP), F32)],
    )(q, q, do, do, lse, lse, delta, delta, k, v)


SB_TILES = 2
SB_PAIRS_FWD = 4
SB_PAIRS_BWD = 2


def _lanes(hp):
    return slice(hp * LANES, (hp + 1) * LANES)


def _sb_logits(z, valid):
    e = jnp.exp(-jnp.abs(z))
    lb = jnp.minimum(z, 0.0) - jnp.log(1.0 + e)
    lk = lb - z
    if valid is not None:
        lk = jnp.where(valid, lk, 0.0)
    return e, lb, lk


def _by_head(t, first):
    zero = jnp.zeros_like(t)
    return jnp.concatenate([jnp.where(first, t, zero), jnp.where(first, zero, t)], axis=0)


def _sb_valid(i, j):
    rowi = lax.broadcasted_iota(jnp.int32, (BLOCK, BLOCK), 0)
    coli = lax.broadcasted_iota(jnp.int32, (BLOCK, BLOCK), 1)
    return (coli - rowi) < (i - j) * BLOCK


def _scaled(q):
    return (q.astype(F32) * (HEAD_DIM ** -0.5)).astype(BF16)


def _sb_fwd(qs, ks, vs, tri_later):
    S = qs.shape[0]
    P = SB_PAIRS_FWD
    W = P * LANES

    def body(q_ref, k_ref, v_ref, u_ref, o_ref, lt_ref):
        i = pl.program_id(1)
        first = lax.broadcasted_iota(jnp.int32, (BLOCK, LANES), 1) < HEAD_DIM
        q2 = [_scaled(q_ref[:, _lanes(hp)]) for hp in range(P)]

        def chunk(ci, carry, masked):
            runs, accs = list(carry[0]), list(carry[1])
            units = [(t, hp) for t in reversed(range(SB_TILES)) for hp in range(P)]
            z2s, lbs, c2s = {}, {}, {}
            for t, hp in units:
                off = pl.multiple_of((ci * SB_TILES + t) * BLOCK, BLOCK)
                z2s[t, hp] = _dot_nt(q2[hp], _by_head(k_ref[pl.ds(off, BLOCK), _lanes(hp)], first))
            for t, hp in units:
                valid = _sb_valid(i, ci * SB_TILES + t) if masked else None
                for h in range(2):
                    _, lb, lk = _sb_logits(z2s[t, hp][:, h * BLOCK:(h + 1) * BLOCK], valid)
                    lbs[t, hp, h] = lb
                    c2s[t, hp, h] = _dot(jnp.concatenate(_split(lk), axis=1), u_ref[...])
            for t, hp in units:
                off = pl.multiple_of((ci * SB_TILES + t) * BLOCK, BLOCK)
                valid = _sb_valid(i, ci * SB_TILES + t) if masked else None
                a2 = []
                for h in range(2):
                    a = jnp.exp(lbs[t, hp, h] + c2s[t, hp, h][:, :BLOCK] + runs[2 * hp + h])
                    if masked:
                        a = jnp.where(valid, a, 0.0)
                    a2.append(a.astype(BF16))
                    runs[2 * hp + h] = runs[2 * hp + h] + c2s[t, hp, h][:, BLOCK:]
                vcat = _by_head(v_ref[pl.ds(off, BLOCK), _lanes(hp)], first)
                accs[hp] = accs[hp] + _dot(jnp.concatenate(a2, axis=1), vcat)
            return tuple(runs), tuple(accs)

        zero = jnp.zeros((BLOCK, LANES), F32)
        nfull = i // SB_TILES
        carry = chunk(nfull, ((zero,) * (2 * P), (zero,) * P), True)
        runs, accs = lax.fori_loop(0, nfull, lambda t, c: chunk(nfull - 1 - t, c, False), carry)
        for hp in range(P):
            o_ref[:, _lanes(hp)] = accs[hp]
            lt_ref[:, _lanes(hp)] = jnp.where(first, runs[2 * hp], runs[2 * hp + 1])

    blk = pl.BlockSpec((BLOCK, W), lambda hp, i: (i, hp))
    col = pl.BlockSpec((S, W), lambda hp, i: (0, hp))
    return pl.pallas_call(
        body, name="sb_fwd", grid=(GROUP // W, S // BLOCK),
        in_specs=[blk, col, col, _full((2 * BLOCK, 2 * BLOCK))], out_specs=[blk, blk],
        out_shape=[jax.ShapeDtypeStruct((S, GROUP), F32)] * 2,
    )(qs, ks, vs, tri_later)


def _sb_bwd(qs, ks, vs, do, ltot, tri_upto, tri_before):
    S = qs.shape[0]
    P = SB_PAIRS_BWD
    W = P * LANES

    def body(q_ref, k_ref, v_ref, do_ref, lt_ref, w_ref, x_ref, dq_ref, dk_ref, dv_ref):
        i = pl.program_id(1)

        @pl.when(i == 0)
        def _():
            dk_ref[...] = jnp.zeros_like(dk_ref)
            dv_ref[...] = jnp.zeros_like(dv_ref)

        first = lax.broadcasted_iota(jnp.int32, (BLOCK, LANES), 1) < HEAD_DIM
        q2 = [_scaled(q_ref[:, _lanes(hp)]) for hp in range(P)]
        do2 = [do_ref[:, _lanes(hp)] for hp in range(P)]
        totals = [jnp.broadcast_to(lt_ref[:, n * HEAD_DIM:n * HEAD_DIM + 1], (BLOCK, LANES)) for n in range(2 * P)]

        def chunk(ci, carry, masked):
            keeps, grads, dqs = list(carry[0]), list(carry[1]), list(carry[2])
            units = [(t, hp) for t in range(SB_TILES) for hp in range(P)]
            offs = [pl.multiple_of((ci * SB_TILES + t) * BLOCK, BLOCK) for t in range(SB_TILES)]
            valids = [_sb_valid(i, ci * SB_TILES + t) if masked else None for t in range(SB_TILES)]
            kcat, z2, da2, es, lbs, c2s, as_, des, p2s = {}, {}, {}, {}, {}, {}, {}, {}, {}
            for t, hp in units:
                kcat[t, hp] = _by_head(k_ref[pl.ds(offs[t], BLOCK), _lanes(hp)], first)
                z2[t, hp] = _dot_nt(q2[hp], kcat[t, hp])
                da2[t, hp] = _dot_nt(do2[hp], _by_head(v_ref[pl.ds(offs[t], BLOCK), _lanes(hp)], first))
            for t, hp in units:
                for h in range(2):
                    es[t, hp, h], lbs[t, hp, h], lk = _sb_logits(z2[t, hp][:, h * BLOCK:(h + 1) * BLOCK], valids[t])
                    c2s[t, hp, h] = _dot(jnp.concatenate(_split(lk), axis=1), w_ref[...])
            for t, hp in units:
                for h in range(2):
                    n = 2 * hp + h
                    a = jnp.exp(lbs[t, hp, h] + (totals[n] - (keeps[n] + c2s[t, hp, h][:, :BLOCK])))
                    if masked:
                        a = jnp.where(valids[t], a, 0.0)
                    keeps[n] = keeps[n] + c2s[t, hp, h][:, BLOCK:]
                    de = a * da2[t, hp][:, h * BLOCK:(h + 1) * BLOCK]
                    as_[t, hp, h], des[t, hp, h] = a.astype(BF16), de
                    p2s[t, hp, h] = _dot(jnp.concatenate(_split(de), axis=1), x_ref[...])
            for t, hp in units:
                dz2 = []
                for h in range(2):
                    n = 2 * hp + h
                    e = es[t, hp, h]
                    sig = jnp.where(z2[t, hp][:, h * BLOCK:(h + 1) * BLOCK] >= 0.0, 1.0, e) / (1.0 + e)
                    dz = des[t, hp, h] * (1.0 - sig) - (grads[n] + p2s[t, hp, h][:, :BLOCK]) * sig
                    if masked:
                        dz = jnp.where(valids[t], dz, 0.0)
                    grads[n] = grads[n] + p2s[t, hp, h][:, BLOCK:]
                    dz2.append(dz.astype(BF16))
                dzcat = jnp.concatenate(dz2, axis=1)
                dk2 = _dot_tn(dzcat, q2[hp])
                dv2 = _dot_tn(jnp.concatenate([as_[t, hp, 0], as_[t, hp, 1]], axis=1), do2[hp])
                dk_ref[pl.ds(offs[t], BLOCK), _lanes(hp)] += jnp.where(first, dk2[:BLOCK], dk2[BLOCK:])
                dv_ref[pl.ds(offs[t], BLOCK), _lanes(hp)] += jnp.where(first, dv2[:BLOCK], dv2[BLOCK:])
                dqs[hp] = dqs[hp] + _dot(dzcat, kcat[t, hp])
            return tuple(keeps), tuple(grads), tuple(dqs)

        zero = jnp.zeros((BLOCK, LANES), F32)
        nfull = i // SB_TILES
        carry = lax.fori_loop(0, nfull, lambda t, c: chunk(t, c, False),
                              ((zero,) * (2 * P), (zero,) * (2 * P), (zero,) * P))
        carry = chunk(nfull, carry, True)
        for hp in range(P):
            dq_ref[:, _lanes(hp)] = carry[2][hp] * (HEAD_DIM ** -0.5)

    blk = pl.BlockSpec((BLOCK, W), lambda hp, i: (i, hp))
    col = pl.BlockSpec((S, W), lambda hp, i: (0, hp))
    tri = _full((2 * BLOCK, 2 * BLOCK))
    return pl.pallas_call(
        body, name="sb_bwd", grid=(GROUP // W, S // BLOCK),
        in_specs=[blk, col, col, blk, blk, tri, tri], out_specs=[blk, col, col],
        out_shape=[jax.ShapeDtypeStruct((S, GROUP), F32)] * 3,
    )(qs, ks, vs, do, ltot, tri_upto, tri_before)


def _out_proj_fwd(o_br, l_br, o_sb, x, w_dil, w_sbn, w_out_g):
    S, D = x.shape
    tm = 512

    def body(o0, o1, o2, l0, l1, l2, os_ref, x_ref, wd_ref, ws_ref, w_ref, od_ref, s0, s1, s2, x1_ref, scr):
        ls = [_from_strided(scr, l) for l in (l0, l1, l2)]
        os_ = [_from_strided(scr, o) for o in (o0, o1, o2)]
        m = jnp.maximum(jnp.maximum(ls[0], ls[1]), ls[2])
        es = [jnp.exp(l - m) for l in ls]
        den = es[0] + es[1] + es[2]
        od = (es[0] * os_[0] + es[1] * os_[1] + es[2] * os_[2]) / den
        od_ref[...] = od
        _to_strided(scr, m + jnp.log(den), list(zip(DILATIONS, (s0, s1, s2))))
        osb = os_ref[...]
        mixed = jnp.concatenate([(od * _rms(od) * wd_ref[...]).astype(BF16),
                                 (osb * _rms(osb) * ws_ref[...]).astype(BF16)], axis=1)
        x1_ref[...] = x_ref[...] + _dot(mixed, w_ref[...])

    row = lambda i: (i, 0)
    g = pl.BlockSpec((tm, GROUP), row)
    d = pl.BlockSpec((tm, D), row)
    planes = [_strided_spec(tm, r) for r in DILATIONS]
    return pl.pallas_call(
        body, name="out_proj_fwd", grid=(S // tm,),
        in_specs=planes * 2 + [g, d, _full((1, GROUP)), _full((1, GROUP)), _full((2 * GROUP, D))],
        out_specs=[g] + planes + [d],
        out_shape=[jax.ShapeDtypeStruct((S, GROUP), F32)] + [_strided_shape(S, r, F32) for r in DILATIONS]
                  + [jax.ShapeDtypeStruct((S, D), F32)],
        scratch_shapes=[_strided_scratch(tm)],
    )(*o_br, *l_br, o_sb, x, w_dil, w_sbn, w_out_g)


def _ffn_fwd(x1, target, ffn_w, wg_g, wu_g, wd_g):
    S, D = x1.shape
    F = wg_g.shape[2]
    tm = 512
    nt = S // tm

    def body(x_ref, t_ref, nw_ref, wg_ref, wu_ref, wd_ref, h_ref, g_ref, u_ref, dy_ref, loss_ref, h_s, acc):
        j = pl.program_id(1)

        @pl.when(j == 0)
        def _():
            xv = x_ref[...]
            h = (xv * _rms(xv) * nw_ref[...]).astype(BF16)
            h_s[...] = h
            h_ref[...] = h
            acc[...] = xv

        h = h_s[...]
        g = _dot(h, wg_ref[0])
        u = _dot(h, wu_ref[0])
        g_ref[0] = g.astype(BF16)
        u_ref[0] = u.astype(BF16)
        a = (g * _sigmoid(g) * u).astype(BF16)
        acc[...] += _dot(a, wd_ref[0])

        @pl.when(j == N_CHIPS - 1)
        def _():
            err = acc[...] - t_ref[...]
            dy_ref[...] = err * (1.0 / D)
            loss_ref[...] = jnp.full(loss_ref.shape, jnp.sum(err * err), F32)

    row = lambda t, j: (t, 0)
    shard = lambda t, j: (j, 0, 0)
    act = lambda t, j: (j, t, 0)
    return pl.pallas_call(
        body, name="ffn_fwd", grid=(nt, N_CHIPS),
        in_specs=[pl.BlockSpec((tm, D), row), pl.BlockSpec((tm, D), row), pl.BlockSpec((1, D), lambda t, j: (0, 0)),
                  pl.BlockSpec((1, D, F), shard), pl.BlockSpec((1, D, F), shard), pl.BlockSpec((1, F, D), shard)],
        out_specs=[pl.BlockSpec((tm, D), row), pl.BlockSpec((1, tm, F), act), pl.BlockSpec((1, tm, F), act),
                   pl.BlockSpec((tm, D), row), pl.BlockSpec((1, 8, LANES), lambda t, j: (t, 0, 0))],
        out_shape=[jax.ShapeDtypeStruct((S, D), BF16), jax.ShapeDtypeStruct((N_CHIPS, S, F), BF16),
                   jax.ShapeDtypeStruct((N_CHIPS, S, F), BF16), jax.ShapeDtypeStruct((S, D), F32),
                   jax.ShapeDtypeStruct((nt, 8, LANES), F32)],
        scratch_shapes=[pltpu.VMEM((tm, D), BF16), pltpu.VMEM((tm, D), F32)],
    )(x1, target, ffn_w, wg_g, wu_g, wd_g)


def _ffn_bwd(h2, dy, g, u, wg_g, wu_g, wd_g):
    S, D = dy.shape
    F = wg_g.shape[2]
    tm = 512

    def body(h_ref, dy_ref, g_ref, u_ref, wg_ref, wu_ref, wd_ref, dwg_ref, dwu_ref, dwd_ref, dh_ref):
        t = pl.program_id(1)

        @pl.when(t == 0)
        def _():
            dwg_ref[...] = jnp.zeros_like(dwg_ref)
            dwu_ref[...] = jnp.zeros_like(dwu_ref)
            dwd_ref[...] = jnp.zeros_like(dwd_ref)

        h = h_ref[...]
        dyb = dy_ref[...].astype(BF16)
        gv = g_ref[0].astype(F32)
        uv = u_ref[0].astype(F32)
        da = _dot_nt(dyb, wd_ref[0])
        sg = _sigmoid(gv)
        silu = gv * sg
        du = (da * silu).astype(BF16)
        dg = (da * uv * (sg * (1.0 + gv * (1.0 - sg)))).astype(BF16)
        dwd_ref[0] += _dot_tn((silu * uv).astype(BF16), dyb)
        dwg_ref[0] += _dot_tn(h, dg)
        dwu_ref[0] += _dot_tn(h, du)
        dh_ref[0] = _dot_nt(dg, wg_ref[0]) + _dot_nt(du, wu_ref[0])

    row = lambda j, t: (t, 0)
    shard = lambda j, t: (j, 0, 0)
    act = lambda j, t: (j, t, 0)
    return pl.pallas_call(
        body, name="ffn_bwd", grid=(N_CHIPS, S // tm),
        in_specs=[pl.BlockSpec((tm, D), row), pl.BlockSpec((tm, D), row),
                  pl.BlockSpec((1, tm, F), act), pl.BlockSpec((1, tm, F), act),
                  pl.BlockSpec((1, D, F), shard), pl.BlockSpec((1, D, F), shard), pl.BlockSpec((1, F, D), shard)],
        out_specs=[pl.BlockSpec((1, D, F), shard), pl.BlockSpec((1, D, F), shard), pl.BlockSpec((1, F, D), shard),
                   pl.BlockSpec((1, tm, D), act)],
        out_shape=[jax.ShapeDtypeStruct((N_CHIPS, D, F), F32), jax.ShapeDtypeStruct((N_CHIPS, D, F), F32),
                   jax.ShapeDtypeStruct((N_CHIPS, F, D), F32), jax.ShapeDtypeStruct((N_CHIPS, S, D), F32)],
    )(h2, dy, g, u, wg_g, wu_g, wd_g)


def _out_proj_bwd(dh2p, dy, x1, ffn_w, w_out_g, o_dil, o_sb, w_dil, w_sbn, seg_ones):
    S, D = dy.shape
    tm = 256

    def body(dh_ref, dy_ref, x1_ref, nw_ref, w_ref, od_ref, os_ref, wd_ref, ws_ref, g_ref,
             dx1_ref, dod0, dod1, dod2, dos_ref, dl0, dl1, dl2, dw_ref, dnw_ref, dwd_ref, dws_ref, scr):
        i = pl.program_id(0)

        @pl.when(i == 0)
        def _():
            for r_ in (dw_ref, dnw_ref, dwd_ref, dws_ref):
                r_[...] = jnp.zeros_like(r_)

        dh2 = (dh_ref[0] + dh_ref[1]) + (dh_ref[2] + dh_ref[3])
        dxn, dwn = _rms_bwd(dh2, x1_ref[...], nw_ref[...])
        dnw_ref[...] += jnp.sum(dwn, axis=0, keepdims=True)
        dx1 = dy_ref[...] + dxn
        dx1_ref[...] = dx1
        dx1b = dx1.astype(BF16)
        dmix = _dot_nt(dx1b, w_ref[...])
        od = od_ref[...]
        osb = os_ref[...]
        mixed = jnp.concatenate([(od * _rms(od) * wd_ref[...]).astype(BF16),
                                 (osb * _rms(osb) * ws_ref[...]).astype(BF16)], axis=1)
        dw_ref[...] += _dot_tn(mixed, dx1b)
        do, dwo = _rms_bwd(dmix[:, :GROUP], od, wd_ref[...])
        dwd_ref[...] += jnp.sum(dwo, axis=0, keepdims=True)
        _to_strided(scr, do, list(zip(DILATIONS, (dod0, dod1, dod2))))
        _to_strided(scr, _segsum(do * od, g_ref[...]), list(zip(DILATIONS, (dl0, dl1, dl2))))
        do, dwo = _rms_bwd(dmix[:, GROUP:], osb, ws_ref[...])
        dws_ref[...] += jnp.sum(dwo, axis=0, keepdims=True)
        dos_ref[...] = do.astype(BF16)

    row = lambda i: (i, 0)
    gsp = pl.BlockSpec((tm, GROUP), row)
    dsp = pl.BlockSpec((tm, D), row)
    planes = [_strided_spec(tm, r) for r in DILATIONS]
    return pl.pallas_call(
        body, name="out_proj_bwd", grid=(S // tm,),
        in_specs=[pl.BlockSpec((N_CHIPS, tm, D), lambda i: (0, i, 0)), dsp, dsp, _full((1, D)), _full((2 * GROUP, D)),
                  gsp, gsp, _full((1, GROUP)), _full((1, GROUP)), _full((GROUP, GROUP))],
        out_specs=[dsp] + planes + [gsp] + planes
                  + [_full((2 * GROUP, D)), _full((1, D)), _full((1, GROUP)), _full((1, GROUP))],
        out_shape=[jax.ShapeDtypeStruct((S, D), F32)] + [_strided_shape(S, r, BF16) for r in DILATIONS]
                  + [jax.ShapeDtypeStruct((S, GROUP), BF16)] + [_strided_shape(S, r, F32) for r in DILATIONS]
                  + [jax.ShapeDtypeStruct((2 * GROUP, D), F32),
                     jax.ShapeDtypeStruct((1, D), F32), jax.ShapeDtypeStruct((1, GROUP), F32),
                     jax.ShapeDtypeStruct((1, GROUP), F32)],
        scratch_shapes=[_strided_scratch(tm)],
    )(dh2p, dy, x1, ffn_w, w_out_g, o_dil, o_sb, w_dil, w_sbn, seg_ones)


def _qk_bwd(dq_br, dk_br, dv_br, dqs, dks, dvs, qa, ka, qw, kw, cos_t, sin_t, seg_ones):
    S = qa.shape[0]
    tm = 256

    def body(q0, q1, q2, k0, k1, k2, v0, v1, v2, dqs_ref, dks_ref, dvs_ref, qa_ref, ka_ref, qw_ref, kw_ref,
             cos_ref, sin_ref, g_ref, dp_ref, dqw_ref, dkw_ref, accq, acck, scr):
        i = pl.program_id(0)

        @pl.when(i == 0)
        def _():
            accq[...] = jnp.zeros_like(accq)
            acck[...] = jnp.zeros_like(acck)

        def branches(refs):
            return (_from_strided(scr, refs[0]) + _from_strided(scr, refs[1])) + _from_strided(scr, refs[2])

        g = g_ref[...]
        cos = _tile4(cos_ref[...])
        sin = _tile4(sin_ref[...])
        for b, (refs, pre_ref, w_ref, acc) in enumerate((((q0, q1, q2), qa_ref, qw_ref, accq),
                                                        ((k0, k1, k2), ka_ref, kw_ref, acck))):
            dh = branches(refs)
            dn = dh * cos + _rot_half(dh * sin)
            pre = pre_ref[...]
            rstd = lax.rsqrt(_segsum(pre * pre, g) * (1.0 / HEAD_DIM) + EPS)
            xh = pre * rstd
            acc[...] += jnp.sum(dn * xh, axis=0, keepdims=True)
            dxh = dn * w_ref[...]
            dpre = rstd * (dxh - xh * (_segsum(dxh * xh, g) * (1.0 / HEAD_DIM)))
            dp_ref[:, b * GROUP:(b + 1) * GROUP] = dpre.astype(BF16)
        dp_ref[:, 2 * GROUP:3 * GROUP] = branches((v0, v1, v2)).astype(BF16)
        dp_ref[:, 3 * GROUP:4 * GROUP] = dqs_ref[...].astype(BF16)
        dp_ref[:, 4 * GROUP:5 * GROUP] = dks_ref[...].astype(BF16)
        dp_ref[:, 5 * GROUP:6 * GROUP] = dvs_ref[...].astype(BF16)

        @pl.when(i == S // tm - 1)
        def _():
            for acc, o_ref in ((accq, dqw_ref), (acck, dkw_ref)):
                a = acc[...]
                pair = (a[:, 0:LANES] + a[:, LANES:2 * LANES]) + (a[:, 2 * LANES:3 * LANES] + a[:, 3 * LANES:4 * LANES])
                o_ref[...] = pair + pltpu.roll(pair, HEAD_DIM, 1)

    row = lambda i: (i, 0)
    gsp = pl.BlockSpec((tm, GROUP), row)
    tab = pl.BlockSpec((tm, LANES), row)
    planes = [_strided_spec(tm, r) for r in DILATIONS]
    return pl.pallas_call(
        body, name="qk_bwd", grid=(S // tm,),
        in_specs=planes * 3 + [gsp] * 5 + [_full((1, GROUP)), _full((1, GROUP)), tab, tab, _full((GROUP, GROUP))],
        out_specs=[pl.BlockSpec((tm, 6 * GROUP), row), _full((1, LANES)), _full((1, LANES))],
        out_shape=[jax.ShapeDtypeStruct((S, 6 * GROUP), BF16), jax.ShapeDtypeStruct((1, LANES), F32),
                   jax.ShapeDtypeStruct((1, LANES), F32)],
        scratch_shapes=[pltpu.VMEM((1, GROUP), F32), pltpu.VMEM((1, GROUP), F32), _strided_scratch(tm)],
    )(*dq_br, *dk_br, *dv_br, dqs, dks, dvs, qa, ka, qw, kw, cos_t, sin_t, seg_ones)


def _in_proj_bwd(h, dproj, w_in_g):
    S, D = h.shape
    wc = w_in_g.shape[2]
    tm = 512

    def body(h_ref, dp_ref, w_ref, dw_ref, dh_ref):
        t = pl.program_id(1)

        @pl.when(t == 0)
        def _():
            dw_ref[...] = jnp.zeros_like(dw_ref)

        dp = dp_ref[...]
        dw_ref[0] += _dot_tn(h_ref[...], dp)
        dh_ref[0] = _dot_nt(dp, w_ref[0])

    return pl.pallas_call(
        body, name="in_proj_bwd", grid=(N_CHIPS, S // tm),
        in_specs=[pl.BlockSpec((tm, D), lambda j, t: (t, 0)), pl.BlockSpec((tm, wc), lambda j, t: (t, j)),
                  pl.BlockSpec((1, D, wc), lambda j, t: (j, 0, 0))],
        out_specs=[pl.BlockSpec((1, D, wc), lambda j, t: (j, 0, 0)), pl.BlockSpec((1, tm, D), lambda j, t: (j, t, 0))],
        out_shape=[jax.ShapeDtypeStruct((N_CHIPS, D, wc), F32), jax.ShapeDtypeStruct((N_CHIPS, S, D), F32)],
    )(h, dproj, w_in_g)


def _in_norm_bwd(dhp, dx1, x, attn_w):
    S, D = x.shape
    tm = 512

    def body(dh_ref, dx1_ref, x_ref, w_ref, gx_ref, dw_ref):
        i = pl.program_id(0)

        @pl.when(i == 0)
        def _():
            dw_ref[...] = jnp.zeros_like(dw_ref)

        dh = (dh_ref[0] + dh_ref[1]) + (dh_ref[2] + dh_ref[3])
        dx, dw = _rms_bwd(dh, x_ref[...], w_ref[...])
        dw_ref[...] += jnp.sum(dw, axis=0, keepdims=True)
        gx_ref[...] = dx1_ref[...] + dx

    row = lambda i: (i, 0)
    dsp = pl.BlockSpec((tm, D), row)
    return pl.pallas_call(
        body, name="in_norm_bwd", grid=(S // tm,),
        in_specs=[pl.BlockSpec((N_CHIPS, tm, D), lambda i: (0, i, 0)), dsp, dsp, _full((1, D))],
        out_specs=[dsp, _full((1, D))],
        out_shape=[jax.ShapeDtypeStruct((S, D), F32), jax.ShapeDtypeStruct((1, D), F32)],
    )(dhp, dx1, x, attn_w)


def _constants(S):
    pos = jnp.arange(S, dtype=F32)
    inv_freq = ROPE_THETA ** (-jnp.arange(0, HEAD_DIM, 2, dtype=F32) / HEAD_DIM)
    ang = pos[:, None] * inv_freq[None, :]
    cos, sin = jnp.cos(ang), jnp.sin(ang)
    cos_t = jnp.concatenate([cos, cos] * 2, axis=1)
    sin_t = jnp.concatenate([-sin, sin] * 2, axis=1)
    idx = jnp.arange(GROUP)
    seg_ones = (idx[:, None] // HEAD_DIM == idx[None, :] // HEAD_DIM).astype(BF16)
    r = jnp.arange(BLOCK)
    ones = jnp.ones((BLOCK, BLOCK), BF16)
    tris = [jnp.concatenate([jnp.concatenate([m.astype(BF16), ones], axis=1)] * 2, axis=0) for m in
            (r[:, None] > r[None, :],
             r[:, None] <= r[None, :],
             r[:, None] < r[None, :])]
    return cos_t, sin_t, seg_ones, tris


def _local_step(x, target, attn_w, qn_w, kn_w, dil_w, sbn_w, ffn_w, w_in_g, w_out_g, wg_g, wu_g, wd_g):
    S = x.shape[0]
    cos_t, sin_t, seg_ones, (tri_later, tri_upto, tri_before) = _constants(S)
    reps = GROUP // HEAD_DIM
    qw = jnp.tile(qn_w, (1, reps))
    kw = jnp.tile(kn_w, (1, reps))

    nd = len(DILATIONS)
    h, qa, ka, *rest = _in_proj_fwd(x, attn_w, w_in_g, qw, kw, cos_t, sin_t, seg_ones)
    qh, kh, va, (qs, ks, vs) = rest[:nd], rest[nd:2 * nd], rest[2 * nd:3 * nd], rest[3 * nd:]
    branches = [_dil_fwd(qh[b], kh[b], va[b]) for b in range(nd)]
    o_sb, ltot = _sb_fwd(qs, ks, vs, tri_later)
    o_dil, *lse, x1 = _out_proj_fwd([b[0] for b in branches], [b[1] for b in branches], o_sb, x, dil_w, sbn_w, w_out_g)
    h2, g, u, dy, loss_parts = _ffn_fwd(x1, target, ffn_w, wg_g, wu_g, wd_g)

    dwg, dwu, dwd, dh2p = _ffn_bwd(h2, dy, g, u, wg_g, wu_g, wd_g)
    dx1, *mid, dw_out, dffn_w, ddil_w, dsbn_w = _out_proj_bwd(
        dh2p, dy, x1, ffn_w, w_out_g, o_dil, o_sb, dil_w, sbn_w, seg_ones)
    do_dil, do_sb, delta = mid[:nd], mid[nd], mid[nd + 1:]
    dqs, dks, dvs = _sb_bwd(qs, ks, vs, do_sb, ltot, tri_upto, tri_before)
    dbr = [_dil_bwd(qh[b], kh[b], va[b], do_dil[b], lse[b], delta[b]) for b in range(nd)]
    dproj, dqw, dkw = _qk_bwd([b[0] for b in dbr], [b[1] for b in dbr], [b[2] for b in dbr], dqs, dks, dvs,
                              qa, ka, qw, kw, cos_t, sin_t, seg_ones)
    dw_in, dhp = _in_proj_bwd(h, dproj, w_in_g)
    grad_x, dattn_w = _in_norm_bwd(dhp, dx1, x, attn_w)
    small = dict(attn=dattn_w, q=dqw[:, :HEAD_DIM], k=dkw[:, :HEAD_DIM], dil=ddil_w, sb=dsbn_w, ffn=dffn_w)
    return loss_parts, grad_x, small, dw_in, dw_out, dwg, dwu, dwd


HBM = pl.BlockSpec(memory_space=pltpu.HBM)
VMEM = pl.BlockSpec(memory_space=pltpu.VMEM)
CHIP_FLIPS = ((1, 0), (0, 1), (1, 1))


def _place():
    return lax.axis_index("x"), lax.axis_index("y"), lax.axis_index("c")


def _flip(v, d):
    return 1 - v if d else v


def _half_rows(c, n):
    return pl.ds(pl.multiple_of(c * (n // 2), 16), n // 2)


def _gather_weights(shards):
    n = len(shards)
    here = 2 * lax.axis_index("x") + lax.axis_index("y")
    slots = [lax.dynamic_update_slice(lax.empty((N_CHIPS,) + s.shape, s.dtype), s[None], (here, 0, 0)) for s in shards]

    def body(*refs):
        ins, outs = refs[:n], refs[n:2 * n]
        send, recv = refs[2 * n:]
        x, y, c = _place()
        p = 2 * x + y
        chips = [(_flip(x, dx), _flip(y, dy)) for dx, dy in CHIP_FLIPS]

        def copy(a, k, src, dst, to):
            return pltpu.make_async_remote_copy(src_ref=src, dst_ref=dst, send_sem=send.at[a * 6 + k],
                                                recv_sem=recv.at[a * 6 + k], device_id=to, device_id_type=MESH)

        sent = []
        for a in range(n):
            mine = _half_rows(c, ins[a].shape[1])
            for k, chip in enumerate(chips):
                sent.append(copy(a, k, ins[a].at[p, mine], outs[a].at[p, mine], (*chip, c)))
                sent[-1].start()
        for a in range(n):
            mine = _half_rows(c, ins[a].shape[1])
            for k, (qx, qy) in enumerate(chips):
                land = outs[a].at[2 * qx + qy, mine]
                copy(a, k, land, land, (qx, qy, c)).wait_recv()
                sent.append(copy(a, 3 + k, land, land, (x, y, 1 - c)))
                sent[-1].start()
        for a in range(n):
            other = _half_rows(1 - c, ins[a].shape[1])
            for k, (qx, qy) in enumerate(chips):
                land = outs[a].at[2 * qx + qy, other]
                copy(a, 3 + k, land, land, (x, y, 1 - c)).wait_recv()
        for cp in sent:
            cp.wait_send()

    return pl.pallas_call(
        body, name="gather_weights", in_specs=[HBM] * n, out_specs=[HBM] * n,
        out_shape=[jax.ShapeDtypeStruct(s.shape, s.dtype) for s in slots],
        input_output_aliases={a: a for a in range(n)},
        scratch_shapes=[pltpu.SemaphoreType.DMA((6 * n,)), pltpu.SemaphoreType.DMA((6 * n,))],
    )(*slots)


def _pair_exchange(grads, small):
    n = len(grads)

    def body(*refs):
        gin, sm = refs[:n], refs[n]
        gout, sm_all = refs[n + 1:2 * n + 1], refs[2 * n + 1]
        send, recv = refs[2 * n + 2:]
        x, y, c = _place()
        me = 4 * x + 2 * y + c
        big = []
        for a in range(n):
            theirs = _half_rows(1 - c, gin[a].shape[1])
            big.append(pltpu.make_async_remote_copy(
                src_ref=gin[a].at[:, theirs, :], dst_ref=gout[a], send_sem=send.at[a], recv_sem=recv.at[a],
                device_id=(x, y, 1 - c), device_id_type=MESH))
            big[-1].start()
        sm_all[pl.ds(me, 1)] = sm[...][None]
        tiny = []
        for k in range(1, N_DEV):
            px, py, pc = _flip(x, k & 4), _flip(y, k & 2), _flip(c, k & 1)
            tiny.append((pltpu.make_async_remote_copy(
                src_ref=sm, dst_ref=sm_all.at[me], send_sem=send.at[n + k - 1], recv_sem=recv.at[n + k - 1],
                device_id=(px, py, pc), device_id_type=MESH), 4 * px + 2 * py + pc))
            tiny[-1][0].start()
        for k, (cp, peer) in enumerate(tiny):
            pltpu.make_async_remote_copy(src_ref=sm, dst_ref=sm_all.at[peer], send_sem=send.at[n + k],
                                         recv_sem=recv.at[n + k], device_id=(x, y, c),
                                         device_id_type=MESH).wait_recv()
            cp.wait_send()
        for cp in big:
            cp.wait()

    halves = [jax.ShapeDtypeStruct((g.shape[0], g.shape[1] // 2, g.shape[2]), g.dtype) for g in grads]
    return pl.pallas_call(
        body, name="pair_exchange", in_specs=[HBM] * n + [VMEM], out_specs=[HBM] * n + [VMEM],
        out_shape=halves + [jax.ShapeDtypeStruct((N_DEV,) + small.shape, small.dtype)],
        scratch_shapes=[pltpu.SemaphoreType.DMA((n + N_DEV - 1,)), pltpu.SemaphoreType.DMA((n + N_DEV - 1,))],
    )(*grads, small)


def _chip_exchange(parts):
    n = len(parts)

    def body(*refs):
        pin, pout = refs[:n], refs[n:2 * n]
        send, recv = refs[2 * n:]
        x, y, c = _place()
        p = 2 * x + y
        chips = [(_flip(x, dx), _flip(y, dy)) for dx, dy in CHIP_FLIPS]
        sent = []
        for a in range(n):
            for k, (qx, qy) in enumerate(chips):
                sent.append(pltpu.make_async_remote_copy(
                    src_ref=pin[a].at[2 * qx + qy], dst_ref=pout[a].at[p], send_sem=send.at[a * 3 + k],
                    recv_sem=recv.at[a * 3 + k], device_id=(qx, qy, c), device_id_type=MESH))
                sent[-1].start()
        for a in range(n):
            for k, (qx, qy) in enumerate(chips):
                land = pout[a].at[2 * qx + qy]
                pltpu.make_async_remote_copy(src_ref=land, dst_ref=land, send_sem=send.at[a * 3 + k],
                                             recv_sem=recv.at[a * 3 + k], device_id=(qx, qy, c),
                                             device_id_type=MESH).wait_recv()
        for cp in sent:
            cp.wait_send()

    return pl.pallas_call(
        body, name="chip_exchange", in_specs=[HBM] * n, out_specs=[HBM] * n,
        out_shape=[jax.ShapeDtypeStruct(s.shape, s.dtype) for s in parts],
        scratch_shapes=[pltpu.SemaphoreType.DMA((3 * n,)), pltpu.SemaphoreType.DMA((3 * n,))],
    )(*parts)


def _pair_swap(halves):
    n = len(halves)

    def body(*refs):
        hin, hout = refs[:n], refs[n:2 * n]
        send, recv = refs[2 * n:]
        x, y, c = _place()
        swaps = [pltpu.make_async_remote_copy(src_ref=hin[a], dst_ref=hout[a], send_sem=send.at[a],
                                              recv_sem=recv.at[a], device_id=(x, y, 1 - c), device_id_type=MESH)
                 for a in range(n)]
        for cp in swaps:
            cp.start()
        for cp in swaps:
            cp.wait()

    return pl.pallas_call(
        body, name="pair_swap", in_specs=[HBM] * n, out_specs=[HBM] * n,
        out_shape=[jax.ShapeDtypeStruct(s.shape, s.dtype) for s in halves],
        scratch_shapes=[pltpu.SemaphoreType.DMA((n,)), pltpu.SemaphoreType.DMA((n,))],
    )(*halves)


def _pair_sum(grad, recv, c, tag):
    _, R, C = grad.shape
    hr = R // 2

    def body(c_ref, a_ref, b_ref, o_ref):
        o_ref[...] = (a_ref[...] + b_ref[...]).astype(BF16)

    return pl.pallas_call(
        body, name="pair_sum_" + tag,
        grid_spec=pltpu.PrefetchScalarGridSpec(
            num_scalar_prefetch=1, grid=(N_CHIPS,),
            in_specs=[pl.BlockSpec((1, hr, C), lambda s, cr: (s, cr[0], 0)),
                      pl.BlockSpec((1, hr, C), lambda s, cr: (s, 0, 0))],
            out_specs=pl.BlockSpec((1, hr, C), lambda s, cr: (s, 0, 0))),
        out_shape=jax.ShapeDtypeStruct((N_CHIPS, hr, C), BF16),
    )(c, grad, recv)


def _chip_sum(received, own, chip, tag):
    _, rows, C = received.shape
    tr = rows // 2

    def body(chip_ref, own_ref, r1_ref, r2_ref, r3_ref, o_ref):
        p = [r[0].astype(F32) for r in (own_ref, r1_ref, r2_ref, r3_ref)]
        o_ref[...] = (p[0] + p[1]) + (p[2] + p[3])

    def slot(k):
        return pl.BlockSpec((1, tr, C), lambda i, cr: (jnp.bitwise_xor(cr[0], k), i, 0))

    return pl.pallas_call(
        body, name="chip_sum_" + tag,
        grid_spec=pltpu.PrefetchScalarGridSpec(
            num_scalar_prefetch=1, grid=(rows // tr,), in_specs=[slot(0), slot(1), slot(2), slot(3)],
            out_specs=pl.BlockSpec((tr, C), lambda i, cr: (i, 0))),
        out_shape=jax.ShapeDtypeStruct((rows, C), F32),
    )(chip, own, received, received, received)


def _adamw_math(w, g, m, v):
    m = ADAM_B1 * m + (1.0 - ADAM_B1) * g
    v = ADAM_B2 * v + (1.0 - ADAM_B2) * (g * g)
    m_hat = m / (1.0 - ADAM_B1 ** ADAM_STEP)
    v_hat = v / (1.0 - ADAM_B2 ** ADAM_STEP)
    delta = -ADAM_LR * (m_hat / (jnp.sqrt(v_hat) + ADAM_EPS) + ADAM_WD * w)
    return delta, m, v


def _adamw(w, g_mine, g_other, m, v, c, tag):
    R, C = w.shape
    tr = R // 4

    def body(c_ref, w_ref, gm_ref, go_ref, m_ref, v_ref, g_ref, d_ref, nm_ref, nv_ref):
        g = jnp.where(pl.program_id(0) == c_ref[0], gm_ref[...], go_ref[...])
        g_ref[...] = g
        d_ref[...], nm_ref[...], nv_ref[...] = _adamw_math(w_ref[...], g, m_ref[...], v_ref[...])

    blk = pl.BlockSpec((tr, C), lambda h, i, cr: (2 * h + i, 0))
    half = pl.BlockSpec((tr, C), lambda h, i, cr: (i, 0))
    return pl.pallas_call(
        body, name="adamw_" + tag,
        grid_spec=pltpu.PrefetchScalarGridSpec(
            num_scalar_prefetch=1, grid=(2, 2), in_specs=[blk, half, half, blk, blk], out_specs=[blk] * 4),
        out_shape=[jax.ShapeDtypeStruct((R, C), F32)] * 4,
    )(c, w, g_mine, g_other, m, v)


def _small_update(all_small, w, m, v):
    def body(a_ref, w_ref, m_ref, v_ref, g_ref, d_ref, nm_ref, nv_ref):
        g = ((a_ref[0] + a_ref[1]) + (a_ref[2] + a_ref[3])) + ((a_ref[4] + a_ref[5]) + (a_ref[6] + a_ref[7]))
        g_ref[...] = g
        d_ref[...], nm_ref[...], nv_ref[...] = _adamw_math(w_ref[...], g, m_ref[...], v_ref[...])

    return pl.pallas_call(
        body, name="small_update", out_shape=[jax.ShapeDtypeStruct(w.shape, F32)] * 4,
    )(all_small, w, m, v)


SMALL_ROWS = (("attn", 0, 0), ("ffn", 1, 0), ("dil", 2, 0), ("sb", 2, GROUP), ("q", 3, 0), ("k", 3, HEAD_DIM))


def _pack_small(vals, D):
    rows = [jnp.zeros((1, D), F32) for _ in range(8)]
    for name, r, off in SMALL_ROWS:
        rows[r] = lax.dynamic_update_slice(rows[r], vals[name].astype(F32), (0, off))
    return jnp.concatenate(rows, axis=0)


def _unpack_small(packed, vals):
    return {name: packed[r:r + 1, off:off + vals[name].shape[1]] for name, r, off in SMALL_ROWS}


def kernel(x, attn_norm_w, w_in, q_norm_w, k_norm_w, dil_out_norm_w, sb_out_norm_w, w_out, ffn_norm_w, w_gate, w_up, w_down, loss_target, m_attn_norm_w, m_w_in, m_q_norm_w, m_k_norm_w, m_dil_out_norm_w, m_sb_out_norm_w, m_w_out, m_ffn_norm_w, m_w_gate, m_w_up, m_w_down, v_attn_norm_w, v_w_in, v_q_norm_w, v_k_norm_w, v_dil_out_norm_w, v_sb_out_norm_w, v_w_out, v_ffn_norm_w, v_w_gate, v_w_up, v_w_down):
    D = x.shape[-1]
    big_names = ("w_in", "w_out", "w_gate", "w_up", "w_down")
    big_w = dict(w_in=w_in[0], w_out=w_out[0], w_gate=w_gate[0], w_up=w_up[0], w_down=w_down[0])
    big_m = dict(w_in=m_w_in[0], w_out=m_w_out[0], w_gate=m_w_gate[0], w_up=m_w_up[0], w_down=m_w_down[0])
    big_v = dict(w_in=v_w_in[0], w_out=v_w_out[0], w_gate=v_w_gate[0], w_up=v_w_up[0], w_down=v_w_down[0])
    small_w = dict(attn=attn_norm_w, q=q_norm_w, k=k_norm_w, dil=dil_out_norm_w, sb=sb_out_norm_w, ffn=ffn_norm_w)
    small_m = dict(attn=m_attn_norm_w, q=m_q_norm_w, k=m_k_norm_w, dil=m_dil_out_norm_w, sb=m_sb_out_norm_w,
                   ffn=m_ffn_norm_w)
    small_v = dict(attn=v_attn_norm_w, q=v_q_norm_w, k=v_k_norm_w, dil=v_dil_out_norm_w, sb=v_sb_out_norm_w,
                   ffn=v_ffn_norm_w)

    gathered = _gather_weights([big_w[n].astype(BF16) for n in big_names])
    w_in_g, w_out_g, wg_g, wu_g, wd_g = gathered
    w_out_full = w_out_g.reshape(-1, D)

    loss_parts, grad_x, small_g, dw_in, dw_out, dwg, dwu, dwd = _local_step(
        x[0], loss_target[0], attn_norm_w, q_norm_w, k_norm_w, dil_out_norm_w, sb_out_norm_w, ffn_norm_w,
        w_in_g, w_out_full, wg_g, wu_g, wd_g)
    loss = lax.psum(jnp.sum(loss_parts[:, 0, 0]) * (0.5 / D), ("x", "y", "c"))

    full = [dw_in, dw_out.reshape(N_CHIPS, -1, D), dwg, dwu, dwd]
    c = lax.axis_index("c").astype(jnp.int32).reshape(1)
    *from_pair, all_small = _pair_exchange(full, _pack_small(small_g, D))
    chip_parts = [_pair_sum(g, r, c, n) for g, r, n in zip(full, from_pair, big_names)]
    from_chips = _chip_exchange(chip_parts)
    chip = (2 * lax.axis_index("x") + lax.axis_index("y")).astype(jnp.int32).reshape(1)
    halves = [_chip_sum(r, p, chip, n) for r, p, n in zip(from_chips, chip_parts, big_names)]
    others = _pair_swap(halves)
    big_out = {n: _adamw(big_w[n], mine, other, big_m[n], big_v[n], c, n)
               for n, mine, other in zip(big_names, halves, others)}
    sg, sd, sm, sv = _small_update(all_small, _pack_small(small_w, D), _pack_small(small_m, D),
                                   _pack_small(small_v, D))
    small_out = [_unpack_small(t, small_w) for t in (sg, sd, sm, sv)]

    order = (("attn", None), (None, "w_in"), ("q", None), ("k", None), ("dil", None), ("sb", None),
             (None, "w_out"), ("ffn", None), (None, "w_gate"), (None, "w_up"), (None, "w_down"))
    outs = [loss, grad_x[None]]
    for kind in range(4):
        for s_name, b_name in order:
            if s_name is not None:
                outs.append(small_out[kind][s_name])
            else:
                outs.append(big_out[b_name][kind][None])
    return tuple(outs)
```

```python
import functools

import jax
import jax.numpy as jnp
from jax import lax
from jax.experimental import pallas as pl
from jax.experimental.pallas import tpu as pltpu

F32 = jnp.float32
BF16 = jnp.bfloat16
MESH = pl.DeviceIdType.MESH

HEAD_DIM = 64
GROUP = 512
BLOCK = 128
LANES = 128
N_CHIPS = 4
N_DEV = 8
EPS = 1e-6
ROPE_THETA = 10000.0
DILATIONS = (1, 4, 16)
NEG = -1e30

ADAM_LR = 0.001
ADAM_B1 = 0.9
ADAM_B2 = 0.999
ADAM_EPS = 1e-08
ADAM_WD = 0.01
ADAM_STEP = 10


def _dot(a, b):
    return jnp.dot(a, b, preferred_element_type=F32)


def _dot_nt(a, b):
    return lax.dot_general(a, b, (((1,), (1,)), ((), ())), preferred_element_type=F32)


def _dot_tn(a, b):
    return lax.dot_general(a, b, (((0,), (0,)), ((), ())), preferred_element_type=F32)


def _split(v):
    hi = lax.bitcast_convert_type(lax.bitcast_convert_type(v, jnp.uint32) & jnp.uint32(0xFFFF0000), F32)
    return hi.astype(BF16), (v - hi).astype(BF16)


def _segsum(v, g):
    hi, lo = _split(v)
    return _dot(hi, g) + _dot(lo, g)


def _rot_half(x):
    outs = []
    for c in range(x.shape[1] // LANES):
        xc = x[:, c * LANES:(c + 1) * LANES]
        lane = lax.broadcasted_iota(jnp.int32, xc.shape, 1)
        first = (lane % HEAD_DIM) < (HEAD_DIM // 2)
        outs.append(jnp.where(first, pltpu.roll(xc, LANES - 32, 1), pltpu.roll(xc, 32, 1)))
    return outs[0] if len(outs) == 1 else jnp.concatenate(outs, axis=1)


def _rms(x):
    return lax.rsqrt(jnp.mean(x * x, axis=-1, keepdims=True) + EPS)


def _rms_bwd(dy, x, w):
    rstd = _rms(x)
    xh = x * rstd
    dxh = dy * w
    dx = rstd * (dxh - xh * jnp.mean(dxh * xh, axis=-1, keepdims=True))
    return dx, dy * xh


def _sigmoid(x):
    return 1.0 / (1.0 + jnp.exp(-x))


def _full(shape):
    n = len(shape)
    return pl.BlockSpec(shape, lambda *_: (0,) * n)


def _strided_spec(tm, r):
    return pl.BlockSpec((r, tm // r, GROUP), lambda i: (0, i, 0))


def _strided_shape(S, r, dtype):
    return jax.ShapeDtypeStruct((r, S // r, GROUP), dtype)


def _to_strided(scr, val, outs):
    chunks = range(GROUP // LANES)
    for k in chunks:
        scr[k] = val[:, _lanes(k)]
    for r, o_ref in outs:
        if r == 1:
            o_ref[0] = val.astype(o_ref.dtype)
            continue
        n = val.shape[0] // r
        for c in range(r):
            rows = pl.ds(c, n, stride=r)
            o_ref[c] = jnp.concatenate([scr.at[k][rows, :] for k in chunks], axis=1).astype(o_ref.dtype)


def _from_strided(scr, ref):
    r, n, _ = ref.shape
    if r == 1:
        return ref[0]
    chunks = range(GROUP // LANES)
    for c in range(r):
        plane = ref[c]
        for k in chunks:
            scr.at[k][pl.ds(c, n, stride=r), :] = plane[:, _lanes(k)]
    return jnp.concatenate([scr[k] for k in chunks], axis=1)


def _strided_scratch(tm):
    return pltpu.VMEM((GROUP // LANES, tm, LANES), F32)


def _tile4(t):
    return jnp.concatenate([t] * (GROUP // LANES), axis=1)


def _in_proj_fwd(x, attn_w, w_in_g, qw, kw, cos_t, sin_t, seg_ones):
    S, D = x.shape
    tm = 512
    wcols = w_in_g.shape[2]
    nd = len(DILATIONS)

    def body(x_ref, aw_ref, w_ref, qw_ref, kw_ref, cos_ref, sin_ref, g_ref, h_ref, qa_ref, ka_ref, *rest):
        q_refs, k_refs, v_refs = rest[:nd], rest[nd:2 * nd], rest[2 * nd:3 * nd]
        qs_ref, ks_ref, vs_ref, scr = rest[3 * nd:]
        xv = x_ref[...]
        h = (xv * _rms(xv) * aw_ref[...]).astype(BF16)
        h_ref[...] = h
        proj = jnp.concatenate([_dot(h, w_ref[j]) for j in range(N_CHIPS)], axis=1)
        qa = proj[:, 0 * GROUP:1 * GROUP]
        ka = proj[:, 1 * GROUP:2 * GROUP]
        qa_ref[...] = qa
        ka_ref[...] = ka
        _to_strided(scr, proj[:, 2 * GROUP:3 * GROUP], list(zip(DILATIONS, v_refs)))
        qs_ref[...] = proj[:, 3 * GROUP:4 * GROUP].astype(BF16)
        ks_ref[...] = proj[:, 4 * GROUP:5 * GROUP].astype(BF16)
        vs_ref[...] = proj[:, 5 * GROUP:6 * GROUP].astype(BF16)
        g = g_ref[...]
        cos = _tile4(cos_ref[...])
        sin = _tile4(sin_ref[...])
        for t, w_r, o_rs in ((qa, qw_ref, q_refs), (ka, kw_ref, k_refs)):
            rstd = lax.rsqrt(_segsum(t * t, g) * (1.0 / HEAD_DIM) + EPS)
            tn = t * rstd * w_r[...]
            _to_strided(scr, tn * cos + _rot_half(tn) * sin, list(zip(DILATIONS, o_rs)))

    row = lambda i: (i, 0)
    tile = lambda n, dt: jax.ShapeDtypeStruct((S, n), dt)
    planes = [_strided_spec(tm, r) for r in DILATIONS]
    return pl.pallas_call(
        body, name="in_proj_fwd", grid=(S // tm,),
        in_specs=[pl.BlockSpec((tm, D), row), _full((1, D)), _full((N_CHIPS, D, wcols)),
                  _full((1, GROUP)), _full((1, GROUP)),
                  pl.BlockSpec((tm, LANES), row), pl.BlockSpec((tm, LANES), row),
                  _full((GROUP, GROUP))],
        out_specs=[pl.BlockSpec((tm, D), row)] + [pl.BlockSpec((tm, GROUP), row)] * 2 + planes * 3
                  + [pl.BlockSpec((tm, GROUP), row)] * 3,
        out_shape=[tile(D, BF16), tile(GROUP, F32), tile(GROUP, F32)]
                  + [_strided_shape(S, r, BF16) for r in DILATIONS] * 3 + [tile(GROUP, BF16)] * 3,
        scratch_shapes=[_strided_scratch(tm)],
    )(x, attn_w, w_in_g, qw, kw, cos_t, sin_t, seg_ones)


def _dil_fwd(q, k, v):
    r, L, _ = q.shape
    nb = L // BLOCK
    P = GROUP // LANES

    def body(q_ref, kc_ref, kp_ref, vc_ref, vp_ref, o_ref, l_ref):
        n = pl.program_id(1)
        rowi = lax.broadcasted_iota(jnp.int32, (BLOCK, BLOCK), 0)
        coli = lax.broadcasted_iota(jnp.int32, (BLOCK, BLOCK), 1)
        first = coli < HEAD_DIM
        masks = (coli <= rowi, jnp.logical_and(coli >= rowi, n > 0))
        s2 = {}
        for hp in range(P):
            q2 = _scaled(q_ref[:, _lanes(hp)])
            for b, k_ref in enumerate((kc_ref, kp_ref)):
                s2[hp, b] = _dot_nt(q2, _by_head(k_ref[:, _lanes(hp)], first))
        ps, inv, lse = {}, {}, {}
        for hp in range(P):
            for h in range(2):
                s = [jnp.where(masks[b], s2[hp, b][:, h * BLOCK:(h + 1) * BLOCK], NEG) for b in range(2)]
                m = jnp.maximum(jnp.max(s[0], axis=1, keepdims=True), jnp.max(s[1], axis=1, keepdims=True))
                p = [jnp.exp(s[b] - m) for b in range(2)]
                den = jnp.sum(p[0], axis=1, keepdims=True) + jnp.sum(p[1], axis=1, keepdims=True)
                ps[hp, h] = [p[b].astype(BF16) for b in range(2)]
                inv[hp, h] = 1.0 / den
                lse[hp, h] = m + jnp.log(den)
        for hp in range(P):
            o = jnp.zeros((BLOCK, LANES), F32)
            for b, v_ref in enumerate((vc_ref, vp_ref)):
                o = o + _dot(jnp.concatenate([ps[hp, 0][b], ps[hp, 1][b]], axis=1),
                             _by_head(v_ref[:, _lanes(hp)], first))
            o_ref[:, _lanes(hp)] = o * jnp.where(first, inv[hp, 0], inv[hp, 1])
            l_ref[:, _lanes(hp)] = jnp.where(first, lse[hp, 0], lse[hp, 1])

    cur = pl.BlockSpec((None, BLOCK, GROUP), lambda c, n: (c, n, 0))
    prev = pl.BlockSpec((None, BLOCK, GROUP), lambda c, n: (c, jnp.maximum(n - 1, 0), 0))
    return pl.pallas_call(
        body, name="dil_fwd_r%d" % r, grid=(r, nb),
        in_specs=[cur, cur, prev, cur, prev], out_specs=[cur, cur],
        out_shape=[jax.ShapeDtypeStruct(q.shape, F32)] * 2,
    )(q, k, k, v, v)


def _dil_bwd(q, k, v, do, lse, delta):
    r, L, _ = q.shape
    nb = L // BLOCK
    P = GROUP // LANES
    scale = HEAD_DIM ** -0.5

    def body(qc_ref, qn_ref, doc_ref, don_ref, lc_ref, ln_ref, dc_ref, dn_ref, k_ref, v_ref,
             dq_ref, dk_ref, dv_ref, carry):
        j = pl.program_id(1)
        rowi = lax.broadcasted_iota(jnp.int32, (BLOCK, BLOCK), 0)
        coli = lax.broadcasted_iota(jnp.int32, (BLOCK, BLOCK), 1)
        first = coli < HEAD_DIM
        sides = ((qc_ref, doc_ref, lc_ref, dc_ref, coli <= rowi),
                 (qn_ref, don_ref, ln_ref, dn_ref, jnp.logical_and(coli >= rowi, j < nb - 1)))

        @pl.when(j == 0)
        def _():
            carry[...] = jnp.zeros_like(carry)

        kcat, q2, do2, s2, dp2 = {}, {}, {}, {}, {}
        for hp in range(P):
            kcat[hp] = _by_head(k_ref[:, _lanes(hp)], first)
            vcat = _by_head(v_ref[:, _lanes(hp)], first)
            for x, (q_r, do_r, _, _, _) in enumerate(sides):
                q2[hp, x] = _scaled(q_r[:, _lanes(hp)])
                do2[hp, x] = do_r[:, _lanes(hp)]
                s2[hp, x] = _dot_nt(q2[hp, x], kcat[hp])
                dp2[hp, x] = _dot_nt(do2[hp, x], vcat)
        pcat, dscat = {}, {}
        for hp in range(P):
            for x, (_, _, l_r, d_r, msk) in enumerate(sides):
                ps, dss = [], []
                for h in range(2):
                    col = hp * LANES + h * HEAD_DIM
                    half = slice(h * BLOCK, (h + 1) * BLOCK)
                    p = jnp.where(msk, jnp.exp(s2[hp, x][:, half] - l_r[:, col:col + 1]), 0.0)
                    ps.append(p.astype(BF16))
                    dss.append((p * (dp2[hp, x][:, half] - d_r[:, col:col + 1])).astype(BF16))
                pcat[hp, x] = jnp.concatenate(ps, axis=1)
                dscat[hp, x] = jnp.concatenate(dss, axis=1)
        for hp in range(P):
            dv2 = _dot_tn(pcat[hp, 0], do2[hp, 0]) + _dot_tn(pcat[hp, 1], do2[hp, 1])
            dk2 = _dot_tn(dscat[hp, 0], q2[hp, 0]) + _dot_tn(dscat[hp, 1], q2[hp, 1])
            dv_ref[:, _lanes(hp)] = jnp.where(first, dv2[:BLOCK], dv2[BLOCK:])
            dk_ref[:, _lanes(hp)] = jnp.where(first, dk2[:BLOCK], dk2[BLOCK:])
            dq_ref[:, _lanes(hp)] = carry[:, _lanes(hp)] + _dot(dscat[hp, 0], kcat[hp]) * scale
            carry[:, _lanes(hp)] = _dot(dscat[hp, 1], kcat[hp]) * scale

    cur = pl.BlockSpec((None, BLOCK, GROUP), lambda c, n: (c, n, 0))
    nxt = pl.BlockSpec((None, BLOCK, GROUP), lambda c, n: (c, jnp.minimum(n + 1, nb - 1), 0))
    return pl.pallas_call(
        body, name="dil_bwd_r%d" % r, grid=(r, nb),
        in_specs=[cur, nxt, cur, nxt, cur, nxt, cur, nxt, cur, cur], out_specs=[cur, cur, cur],
        out_shape=[jax.ShapeDtypeStruct(q.shape, F32)] * 3,
        scratch_shapes=[pltpu.VMEM((BLOCK, GROUP), F32)],
    )(q, q, do, do, lse, lse, delta, delta, k, v)


SB_TILES = 2
SB_PAIRS_FWD = 4
SB_PAIRS_BWD = 2
SB_DEAD = -110.0


def _lanes(hp):
    return slice(hp * LANES, (hp + 1) * LANES)


def _sb_logits(z, valid):
    e = jnp.exp(-jnp.abs(z))
    lb = jnp.minimum(z, 0.0) - jnp.log(1.0 + e)
    lk = lb - z
    if valid is not None:
        lk = jnp.where(valid, lk, 0.0)
    return e, lb, lk


def _by_head(t, first):
    zero = jnp.zeros_like(t)
    return jnp.concatenate([jnp.where(first, t, zero), jnp.where(first, zero, t)], axis=0)


def _sb_valid(i, j):
    rowi = lax.broadcasted_iota(jnp.int32, (BLOCK, BLOCK), 0)
    coli = lax.broadcasted_iota(jnp.int32, (BLOCK, BLOCK), 1)
    return (coli - rowi) < (i - j) * BLOCK


def _scaled(q):
    return (q.astype(F32) * (HEAD_DIM ** -0.5)).astype(BF16)


def _sb_fwd(qs, ks, vs, tri_later):
    S = qs.shape[0]
    P = SB_PAIRS_FWD
    W = P * LANES

    def body(q_ref, k_ref, v_ref, u_ref, o_ref, lt_ref, from_ref):
        i = pl.program_id(1)
        first = lax.broadcasted_iota(jnp.int32, (BLOCK, LANES), 1) < HEAD_DIM
        q2 = [_scaled(q_ref[:, _lanes(hp)]) for hp in range(P)]

        def chunk(ci, carry, masked):
            runs, accs = list(carry[0]), list(carry[1])
            units = [(t, hp) for t in reversed(range(SB_TILES)) for hp in range(P)]
            z2s, lbs, c2s = {}, {}, {}
            for t, hp in units:
                off = pl.multiple_of((ci * SB_TILES + t) * BLOCK, BLOCK)
                z2s[t, hp] = _dot_nt(q2[hp], _by_head(k_ref[pl.ds(off, BLOCK), _lanes(hp)], first))
            for t, hp in units:
                valid = _sb_valid(i, ci * SB_TILES + t) if masked else None
                for h in range(2):
                    _, lb, lk = _sb_logits(z2s[t, hp][:, h * BLOCK:(h + 1) * BLOCK], valid)
                    lbs[t, hp, h] = lb
                    c2s[t, hp, h] = _dot(jnp.concatenate(_split(lk), axis=1), u_ref[...])
            for t, hp in units:
                off = pl.multiple_of((ci * SB_TILES + t) * BLOCK, BLOCK)
                valid = _sb_valid(i, ci * SB_TILES + t) if masked else None
                a2 = []
                for h in range(2):
                    a = jnp.exp(lbs[t, hp, h] + c2s[t, hp, h][:, :BLOCK] + runs[2 * hp + h])
                    if masked:
                        a = jnp.where(valid, a, 0.0)
                    a2.append(a.astype(BF16))
                    runs[2 * hp + h] = runs[2 * hp + h] + c2s[t, hp, h][:, BLOCK:]
                vcat = _by_head(v_ref[pl.ds(off, BLOCK), _lanes(hp)], first)
                accs[hp] = accs[hp] + _dot(jnp.concatenate(a2, axis=1), vcat)
            return tuple(runs), tuple(accs)

        def alive(runs):
            top = functools.reduce(jnp.maximum, runs)
            return (jnp.max(top) > SB_DEAD).astype(jnp.int32)

        def step(c):
            t, _, runs, accs = c
            runs, accs = chunk(nfull - 1 - t, (runs, accs), False)
            return t + 1, alive(runs), runs, accs

        zero = jnp.zeros((BLOCK, LANES), F32)
        nfull = i // SB_TILES
        runs, accs = chunk(nfull, ((zero,) * (2 * P), (zero,) * P), True)
        done, _, runs, accs = lax.while_loop(lambda c: jnp.logical_and(c[0] < nfull, c[1] > 0), step,
                                             (jnp.int32(0), alive(runs), runs, accs))
        for hp in range(P):
            o_ref[:, _lanes(hp)] = accs[hp]
            lt_ref[:, _lanes(hp)] = jnp.where(first, runs[2 * hp], runs[2 * hp + 1])
        from_ref[...] = jnp.full(from_ref.shape, nfull - done, jnp.int32)

    assert W == GROUP
    blk = pl.BlockSpec((BLOCK, W), lambda hp, i: (i, hp))
    col = pl.BlockSpec((S, W), lambda hp, i: (0, hp))
    return pl.pallas_call(
        body, name="sb_fwd", grid=(GROUP // W, S // BLOCK),
        in_specs=[blk, col, col, _full((2 * BLOCK, 2 * BLOCK))],
        out_specs=[blk, blk, pl.BlockSpec((1, 8, LANES), lambda hp, i: (i, 0, 0))],
        out_shape=[jax.ShapeDtypeStruct((S, GROUP), F32)] * 2 + [jax.ShapeDtypeStruct((S // BLOCK, 8, LANES), jnp.int32)],
    )(qs, ks, vs, tri_later)


def _sb_bwd(first_chunk, qs, ks, vs, do, ltot, tri_upto, tri_before):
    S = qs.shape[0]
    P = SB_PAIRS_BWD
    W = P * LANES

    def body(from_ref, q_ref, k_ref, v_ref, do_ref, lt_ref, w_ref, x_ref, dq_ref, dk_ref, dv_ref):
        i = pl.program_id(1)

        @pl.when(i == 0)
        def _():
            dk_ref[...] = jnp.zeros_like(dk_ref)
            dv_ref[...] = jnp.zeros_like(dv_ref)

        first = lax.broadcasted_iota(jnp.int32, (BLOCK, LANES), 1) < HEAD_DIM
        q2 = [_scaled(q_ref[:, _lanes(hp)]) for hp in range(P)]
        do2 = [do_ref[:, _lanes(hp)] for hp in range(P)]
        totals = [jnp.broadcast_to(lt_ref[:, n * HEAD_DIM:n * HEAD_DIM + 1], (BLOCK, LANES)) for n in range(2 * P)]

        def chunk(ci, carry, masked):
            keeps, grads, dqs = list(carry[0]), list(carry[1]), list(carry[2])
            units = [(t, hp) for t in range(SB_TILES) for hp in range(P)]
            offs = [pl.multiple_of((ci * SB_TILES + t) * BLOCK, BLOCK) for t in range(SB_TILES)]
            valids = [_sb_valid(i, ci * SB_TILES + t) if masked else None for t in range(SB_TILES)]
            kcat, z2, da2, es, lbs, c2s, as_, des, p2s = {}, {}, {}, {}, {}, {}, {}, {}, {}
            for t, hp in units:
                kcat[t, hp] = _by_head(k_ref[pl.ds(offs[t], BLOCK), _lanes(hp)], first)
                z2[t, hp] = _dot_nt(q2[hp], kcat[t, hp])
                da2[t, hp] = _dot_nt(do2[hp], _by_head(v_ref[pl.ds(offs[t], BLOCK), _lanes(hp)], first))
            for t, hp in units:
                for h in range(2):
                    es[t, hp, h], lbs[t, hp, h], lk = _sb_logits(z2[t, hp][:, h * BLOCK:(h + 1) * BLOCK], valids[t])
                    c2s[t, hp, h] = _dot(jnp.concatenate(_split(lk), axis=1), w_ref[...])
            for t, hp in units:
                for h in range(2):
                    n = 2 * hp + h
                    a = jnp.exp(lbs[t, hp, h] + (totals[n] - (keeps[n] + c2s[t, hp, h][:, :BLOCK])))
                    if masked:
                        a = jnp.where(valids[t], a, 0.0)
                    keeps[n] = keeps[n] + c2s[t, hp, h][:, BLOCK:]
                    de = a * da2[t, hp][:, h * BLOCK:(h + 1) * BLOCK]
                    as_[t, hp, h], des[t, hp, h] = a.astype(BF16), de
                    p2s[t, hp, h] = _dot(jnp.concatenate(_split(de), axis=1), x_ref[...])
            for t, hp in units:
                dz2 = []
                for h in range(2):
                    n = 2 * hp + h
                    e = es[t, hp, h]
                    sig = jnp.where(z2[t, hp][:, h * BLOCK:(h + 1) * BLOCK] >= 0.0, 1.0, e) / (1.0 + e)
                    dz = des[t, hp, h] * (1.0 - sig) - (grads[n] + p2s[t, hp, h][:, :BLOCK]) * sig
                    if masked:
                        dz = jnp.where(valids[t], dz, 0.0)
                    grads[n] = grads[n] + p2s[t, hp, h][:, BLOCK:]
                    dz2.append(dz.astype(BF16))
                dzcat = jnp.concatenate(dz2, axis=1)
                dk2 = _dot_tn(dzcat, q2[hp])
                dv2 = _dot_tn(jnp.concatenate([as_[t, hp, 0], as_[t, hp, 1]], axis=1), do2[hp])
                dk_ref[pl.ds(offs[t], BLOCK), _lanes(hp)] += jnp.where(first, dk2[:BLOCK], dk2[BLOCK:])
                dv_ref[pl.ds(offs[t], BLOCK), _lanes(hp)] += jnp.where(first, dv2[:BLOCK], dv2[BLOCK:])
                dqs[hp] = dqs[hp] + _dot(dzcat, kcat[t, hp])
            return tuple(keeps), tuple(grads), tuple(dqs)

        zero = jnp.zeros((BLOCK, LANES), F32)
        nfull = i // SB_TILES
        carry = lax.fori_loop(from_ref[i], nfull, lambda t, c: chunk(t, c, False),
                              ((zero,) * (2 * P), (zero,) * (2 * P), (zero,) * P))
        carry = chunk(nfull, carry, True)
        for hp in range(P):
            dq_ref[:, _lanes(hp)] = carry[2][hp] * (HEAD_DIM ** -0.5)

    blk = pl.BlockSpec((BLOCK, W), lambda hp, i, fr: (i, hp))
    col = pl.BlockSpec((S, W), lambda hp, i, fr: (0, hp))
    tri = pl.BlockSpec((2 * BLOCK, 2 * BLOCK), lambda hp, i, fr: (0, 0))
    return pl.pallas_call(
        body, name="sb_bwd",
        grid_spec=pltpu.PrefetchScalarGridSpec(
            num_scalar_prefetch=1, grid=(GROUP // W, S // BLOCK),
            in_specs=[blk, col, col, blk, blk, tri, tri], out_specs=[blk, col, col]),
        out_shape=[jax.ShapeDtypeStruct((S, GROUP), F32)] * 3,
    )(first_chunk, qs, ks, vs, do, ltot, tri_upto, tri_before)


def _out_proj_fwd(o_br, l_br, o_sb, x, w_dil, w_sbn, w_out_g):
    S, D = x.shape
    tm = 512

    def body(o0, o1, o2, l0, l1, l2, os_ref, x_ref, wd_ref, ws_ref, w_ref, od_ref, s0, s1, s2, x1_ref, scr):
        ls = [_from_strided(scr, l) for l in (l0, l1, l2)]
        os_ = [_from_strided(scr, o) for o in (o0, o1, o2)]
        m = jnp.maximum(jnp.maximum(ls[0], ls[1]), ls[2])
        es = [jnp.exp(l - m) for l in ls]
        den = es[0] + es[1] + es[2]
        od = (es[0] * os_[0] + es[1] * os_[1] + es[2] * os_[2]) / den
        od_ref[...] = od
        _to_strided(scr, m + jnp.log(den), list(zip(DILATIONS, (s0, s1, s2))))
        osb = os_ref[...]
        mixed = jnp.concatenate([(od * _rms(od) * wd_ref[...]).astype(BF16),
                                 (osb * _rms(osb) * ws_ref[...]).astype(BF16)], axis=1)
        x1_ref[...] = x_ref[...] + _dot(mixed, w_ref[...])

    row = lambda i: (i, 0)
    g = pl.BlockSpec((tm, GROUP), row)
    d = pl.BlockSpec((tm, D), row)
    planes = [_strided_spec(tm, r) for r in DILATIONS]
    return pl.pallas_call(
        body, name="out_proj_fwd", grid=(S // tm,),
        in_specs=planes * 2 + [g, d, _full((1, GROUP)), _full((1, GROUP)), _full((2 * GROUP, D))],
        out_specs=[g] + planes + [d],
        out_shape=[jax.ShapeDtypeStruct((S, GROUP), F32)] + [_strided_shape(S, r, F32) for r in DILATIONS]
                  + [jax.ShapeDtypeStruct((S, D), F32)],
        scratch_shapes=[_strided_scratch(tm)],
    )(*o_br, *l_br, o_sb, x, w_dil, w_sbn, w_out_g)


def _ffn_fwd(x1, target, ffn_w, wg_g, wu_g, wd_g):
    S, D = x1.shape
    F = wg_g.shape[2]
    tm = 512
    nt = S // tm

    def body(x_ref, t_ref, nw_ref, wg_ref, wu_ref, wd_ref, h_ref, g_ref, u_ref, dy_ref, loss_ref, h_s, acc):
        j = pl.program_id(1)

        @pl.when(j == 0)
        def _():
            xv = x_ref[...]
            h = (xv * _rms(xv) * nw_ref[...]).astype(BF16)
            h_s[...] = h
            h_ref[...] = h
            acc[...] = xv

        h = h_s[...]
        g = _dot(h, wg_ref[0])
        u = _dot(h, wu_ref[0])
        g_ref[0] = g.astype(BF16)
        u_ref[0] = u.astype(BF16)
        a = (g * _sigmoid(g) * u).astype(BF16)
        acc[...] += _dot(a, wd_ref[0])

        @pl.when(j == N_CHIPS - 1)
        def _():
            err = acc[...] - t_ref[...]
            dy_ref[...] = err * (1.0 / D)
            loss_ref[...] = jnp.full(loss_ref.shape, jnp.sum(err * err), F32)

    row = lambda t, j: (t, 0)
    shard = lambda t, j: (j, 0, 0)
    act = lambda t, j: (j, t, 0)
    return pl.pallas_call(
        body, name="ffn_fwd", grid=(nt, N_CHIPS),
        in_specs=[pl.BlockSpec((tm, D), row), pl.BlockSpec((tm, D), row), pl.BlockSpec((1, D), lambda t, j: (0, 0)),
                  pl.BlockSpec((1, D, F), shard), pl.BlockSpec((1, D, F), shard), pl.BlockSpec((1, F, D), shard)],
        out_specs=[pl.BlockSpec((tm, D), row), pl.BlockSpec((1, tm, F), act), pl.BlockSpec((1, tm, F), act),
                   pl.BlockSpec((tm, D), row), pl.BlockSpec((1, 8, LANES), lambda t, j: (t, 0, 0))],
        out_shape=[jax.ShapeDtypeStruct((S, D), BF16), jax.ShapeDtypeStruct((N_CHIPS, S, F), BF16),
                   jax.ShapeDtypeStruct((N_CHIPS, S, F), BF16), jax.ShapeDtypeStruct((S, D), F32),
                   jax.ShapeDtypeStruct((nt, 8, LANES), F32)],
        scratch_shapes=[pltpu.VMEM((tm, D), BF16), pltpu.VMEM((tm, D), F32)],
    )(x1, target, ffn_w, wg_g, wu_g, wd_g)


def _ffn_bwd(h2, dy, g, u, wg_g, wu_g, wd_g):
    S, D = dy.shape
    F = wg_g.shape[2]
    tm = 512

    def body(h_ref, dy_ref, g_ref, u_ref, wg_ref, wu_ref, wd_ref, dwg_ref, dwu_ref, dwd_ref, dh_ref):
        t = pl.program_id(1)

        @pl.when(t == 0)
        def _():
            dwg_ref[...] = jnp.zeros_like(dwg_ref)
            dwu_ref[...] = jnp.zeros_like(dwu_ref)
            dwd_ref[...] = jnp.zeros_like(dwd_ref)

        h = h_ref[...]
        dyb = dy_ref[...].astype(BF16)
        gv = g_ref[0].astype(F32)
        uv = u_ref[0].astype(F32)
        da = _dot_nt(dyb, wd_ref[0])
        sg = _sigmoid(gv)
        silu = gv * sg
        du = (da * silu).astype(BF16)
        dg = (da * uv * (sg * (1.0 + gv * (1.0 - sg)))).astype(BF16)
        dwd_ref[0] += _dot_tn((silu * uv).astype(BF16), dyb)
        dwg_ref[0] += _dot_tn(h, dg)
        dwu_ref[0] += _dot_tn(h, du)
        dh_ref[0] = _dot_nt(dg, wg_ref[0]) + _dot_nt(du, wu_ref[0])

    row = lambda j, t: (t, 0)
    shard = lambda j, t: (j, 0, 0)
    act = lambda j, t: (j, t, 0)
    return pl.pallas_call(
        body, name="ffn_bwd", grid=(N_CHIPS, S // tm),
        in_specs=[pl.BlockSpec((tm, D), row), pl.BlockSpec((tm, D), row),
                  pl.BlockSpec((1, tm, F), act), pl.BlockSpec((1, tm, F), act),
                  pl.BlockSpec((1, D, F), shard), pl.BlockSpec((1, D, F), shard), pl.BlockSpec((1, F, D), shard)],
        out_specs=[pl.BlockSpec((1, D, F), shard), pl.BlockSpec((1, D, F), shard), pl.BlockSpec((1, F, D), shard),
                   pl.BlockSpec((1, tm, D), act)],
        out_shape=[jax.ShapeDtypeStruct((N_CHIPS, D, F), F32), jax.ShapeDtypeStruct((N_CHIPS, D, F), F32),
                   jax.ShapeDtypeStruct((N_CHIPS, F, D), F32), jax.ShapeDtypeStruct((N_CHIPS, S, D), F32)],
    )(h2, dy, g, u, wg_g, wu_g, wd_g)


def _out_proj_bwd(dh2p, dy, x1, ffn_w, w_out_g, o_dil, o_sb, w_dil, w_sbn, seg_ones):
    S, D = dy.shape
    tm = 256

    def body(dh_ref, dy_ref, x1_ref, nw_ref, w_ref, od_ref, os_ref, wd_ref, ws_ref, g_ref,
             dx1_ref, dod0, dod1, dod2, dos_ref, dl0, dl1, dl2, dw_ref, dnw_ref, dwd_ref, dws_ref, scr):
        i = pl.program_id(0)

        @pl.when(i == 0)
        def _():
            for r_ in (dw_ref, dnw_ref, dwd_ref, dws_ref):
                r_[...] = jnp.zeros_like(r_)

        dh2 = (dh_ref[0] + dh_ref[1]) + (dh_ref[2] + dh_ref[3])
        dxn, dwn = _rms_bwd(dh2, x1_ref[...], nw_ref[...])
        dnw_ref[...] += jnp.sum(dwn, axis=0, keepdims=True)
        dx1 = dy_ref[...] + dxn
        dx1_ref[...] = dx1
        dx1b = dx1.astype(BF16)
        dmix = _dot_nt(dx1b, w_ref[...])
        od = od_ref[...]
        osb = os_ref[...]
        mixed = jnp.concatenate([(od * _rms(od) * wd_ref[...]).astype(BF16),
                                 (osb * _rms(osb) * ws_ref[...]).astype(BF16)], axis=1)
        dw_ref[...] += _dot_tn(mixed, dx1b)
        do, dwo = _rms_bwd(dmix[:, :GROUP], od, wd_ref[...])
        dwd_ref[...] += jnp.sum(dwo, axis=0, keepdims=True)
        _to_strided(scr, do, list(zip(DILATIONS, (dod0, dod1, dod2))))
        _to_strided(scr, _segsum(do * od, g_ref[...]), list(zip(DILATIONS, (dl0, dl1, dl2))))
        do, dwo = _rms_bwd(dmix[:, GROUP:], osb, ws_ref[...])
        dws_ref[...] += jnp.sum(dwo, axis=0, keepdims=True)
        dos_ref[...] = do.astype(BF16)

    row = lambda i: (i, 0)
    gsp = pl.BlockSpec((tm, GROUP), row)
    dsp = pl.BlockSpec((tm, D), row)
    planes = [_strided_spec(tm, r) for r in DILATIONS]
    return pl.pallas_call(
        body, name="out_proj_bwd", grid=(S // tm,),
        in_specs=[pl.BlockSpec((N_CHIPS, tm, D), lambda i: (0, i, 0)), dsp, dsp, _full((1, D)), _full((2 * GROUP, D)),
                  gsp, gsp, _full((1, GROUP)), _full((1, GROUP)), _full((GROUP, GROUP))],
        out_specs=[dsp] + planes + [gsp] + planes
                  + [_full((2 * GROUP, D)), _full((1, D)), _full((1, GROUP)), _full((1, GROUP))],
        out_shape=[jax.ShapeDtypeStruct((S, D), F32)] + [_strided_shape(S, r, BF16) for r in DILATIONS]
                  + [jax.ShapeDtypeStruct((S, GROUP), BF16)] + [_strided_shape(S, r, F32) for r in DILATIONS]
                  + [jax.ShapeDtypeStruct((2 * GROUP, D), F32),
                     jax.ShapeDtypeStruct((1, D), F32), jax.ShapeDtypeStruct((1, GROUP), F32),
                     jax.ShapeDtypeStruct((1, GROUP), F32)],
        scratch_shapes=[_strided_scratch(tm)],
    )(dh2p, dy, x1, ffn_w, w_out_g, o_dil, o_sb, w_dil, w_sbn, seg_ones)


def _qk_bwd(dq_br, dk_br, dv_br, dqs, dks, dvs, qa, ka, qw, kw, cos_t, sin_t, seg_ones):
    S = qa.shape[0]
    tm = 256

    def body(q0, q1, q2, k0, k1, k2, v0, v1, v2, dqs_ref, dks_ref, dvs_ref, qa_ref, ka_ref, qw_ref, kw_ref,
             cos_ref, sin_ref, g_ref, dp_ref, dqw_ref, dkw_ref, accq, acck, scr):
        i = pl.program_id(0)

        @pl.when(i == 0)
        def _():
            accq[...] = jnp.zeros_like(accq)
            acck[...] = jnp.zeros_like(acck)

        def branches(refs):
            return (_from_strided(scr, refs[0]) + _from_strided(scr, refs[1])) + _from_strided(scr, refs[2])

        g = g_ref[...]
        cos = _tile4(cos_ref[...])
        sin = _tile4(sin_ref[...])
        for b, (refs, pre_ref, w_ref, acc) in enumerate((((q0, q1, q2), qa_ref, qw_ref, accq),
                                                        ((k0, k1, k2), ka_ref, kw_ref, acck))):
            dh = branches(refs)
            dn = dh * cos + _rot_half(dh * sin)
            pre = pre_ref[...]
            rstd = lax.rsqrt(_segsum(pre * pre, g) * (1.0 / HEAD_DIM) + EPS)
            xh = pre * rstd
            acc[...] += jnp.sum(dn * xh, axis=0, keepdims=True)
            dxh = dn * w_ref[...]
            dpre = rstd * (dxh - xh * (_segsum(dxh * xh, g) * (1.0 / HEAD_DIM)))
            dp_ref[:, b * GROUP:(b + 1) * GROUP] = dpre.astype(BF16)
        dp_ref[:, 2 * GROUP:3 * GROUP] = branches((v0, v1, v2)).astype(BF16)
        dp_ref[:, 3 * GROUP:4 * GROUP] = dqs_ref[...].astype(BF16)
        dp_ref[:, 4 * GROUP:5 * GROUP] = dks_ref[...].astype(BF16)
        dp_ref[:, 5 * GROUP:6 * GROUP] = dvs_ref[...].astype(BF16)

        @pl.when(i == S // tm - 1)
        def _():
            for acc, o_ref in ((accq, dqw_ref), (acck, dkw_ref)):
                a = acc[...]
                pair = (a[:, 0:LANES] + a[:, LANES:2 * LANES]) + (a[:, 2 * LANES:3 * LANES] + a[:, 3 * LANES:4 * LANES])
                o_ref[...] = pair + pltpu.roll(pair, HEAD_DIM, 1)

    row = lambda i: (i, 0)
    gsp = pl.BlockSpec((tm, GROUP), row)
    tab = pl.BlockSpec((tm, LANES), row)
    planes = [_strided_spec(tm, r) for r in DILATIONS]
    return pl.pallas_call(
        body, name="qk_bwd", grid=(S // tm,),
        in_specs=planes * 3 + [gsp] * 5 + [_full((1, GROUP)), _full((1, GROUP)), tab, tab, _full((GROUP, GROUP))],
        out_specs=[pl.BlockSpec((tm, 6 * GROUP), row), _full((1, LANES)), _full((1, LANES))],
        out_shape=[jax.ShapeDtypeStruct((S, 6 * GROUP), BF16), jax.ShapeDtypeStruct((1, LANES), F32),
                   jax.ShapeDtypeStruct((1, LANES), F32)],
        scratch_shapes=[pltpu.VMEM((1, GROUP), F32), pltpu.VMEM((1, GROUP), F32), _strided_scratch(tm)],
    )(*dq_br, *dk_br, *dv_br, dqs, dks, dvs, qa, ka, qw, kw, cos_t, sin_t, seg_ones)


def _in_proj_bwd(h, dproj, w_in_g):
    S, D = h.shape
    wc = w_in_g.shape[2]
    tm = 512

    def body(h_ref, dp_ref, w_ref, dw_ref, dh_ref):
        t = pl.program_id(1)

        @pl.when(t == 0)
        def _():
            dw_ref[...] = jnp.zeros_like(dw_ref)

        dp = dp_ref[...]
        dw_ref[0] += _dot_tn(h_ref[...], dp)
        dh_ref[0] = _dot_nt(dp, w_ref[0])

    return pl.pallas_call(
        body, name="in_proj_bwd", grid=(N_CHIPS, S // tm),
        in_specs=[pl.BlockSpec((tm, D), lambda j, t: (t, 0)), pl.BlockSpec((tm, wc), lambda j, t: (t, j)),
                  pl.BlockSpec((1, D, wc), lambda j, t: (j, 0, 0))],
        out_specs=[pl.BlockSpec((1, D, wc), lambda j, t: (j, 0, 0)), pl.BlockSpec((1, tm, D), lambda j, t: (j, t, 0))],
        out_shape=[jax.ShapeDtypeStruct((N_CHIPS, D, wc), F32), jax.ShapeDtypeStruct((N_CHIPS, S, D), F32)],
    )(h, dproj, w_in_g)


def _in_norm_bwd(dhp, dx1, x, attn_w):
    S, D = x.shape
    tm = 512

    def body(dh_ref, dx1_ref, x_ref, w_ref, gx_ref, dw_ref):
        i = pl.program_id(0)

        @pl.when(i == 0)
        def _():
            dw_ref[...] = jnp.zeros_like(dw_ref)

        dh = (dh_ref[0] + dh_ref[1]) + (dh_ref[2] + dh_ref[3])
        dx, dw = _rms_bwd(dh, x_ref[...], w_ref[...])
        dw_ref[...] += jnp.sum(dw, axis=0, keepdims=True)
        gx_ref[...] = dx1_ref[...] + dx

    row = lambda i: (i, 0)
    dsp = pl.BlockSpec((tm, D), row)
    return pl.pallas_call(
        body, name="in_norm_bwd", grid=(S // tm,),
        in_specs=[pl.BlockSpec((N_CHIPS, tm, D), lambda i: (0, i, 0)), dsp, dsp, _full((1, D))],
        out_specs=[dsp, _full((1, D))],
        out_shape=[jax.ShapeDtypeStruct((S, D), F32), jax.ShapeDtypeStruct((1, D), F32)],
    )(dhp, dx1, x, attn_w)


def _constants(S):
    pos = jnp.arange(S, dtype=F32)
    inv_freq = ROPE_THETA ** (-jnp.arange(0, HEAD_DIM, 2, dtype=F32) / HEAD_DIM)
    ang = pos[:, None] * inv_freq[None, :]
    cos, sin = jnp.cos(ang), jnp.sin(ang)
    cos_t = jnp.concatenate([cos, cos] * 2, axis=1)
    sin_t = jnp.concatenate([-sin, sin] * 2, axis=1)
    idx = jnp.arange(GROUP)
    seg_ones = (idx[:, None] // HEAD_DIM == idx[None, :] // HEAD_DIM).astype(BF16)
    r = jnp.arange(BLOCK)
    ones = jnp.ones((BLOCK, BLOCK), BF16)
    tris = [jnp.concatenate([jnp.concatenate([m.astype(BF16), ones], axis=1)] * 2, axis=0) for m in
            (r[:, None] > r[None, :],
             r[:, None] <= r[None, :],
             r[:, None] < r[None, :])]
    return cos_t, sin_t, seg_ones, tris


def _local_step(x, target, attn_w, qn_w, kn_w, dil_w, sbn_w, ffn_w, w_in_g, w_out_g, wg_g, wu_g, wd_g):
    S = x.shape[0]
    cos_t, sin_t, seg_ones, (tri_later, tri_upto, tri_before) = _constants(S)
    reps = GROUP // HEAD_DIM
    qw = jnp.tile(qn_w, (1, reps))
    kw = jnp.tile(kn_w, (1, reps))

    nd = len(DILATIONS)
    h, qa, ka, *rest = _in_proj_fwd(x, attn_w, w_in_g, qw, kw, cos_t, sin_t, seg_ones)
    qh, kh, va, (qs, ks, vs) = rest[:nd], rest[nd:2 * nd], rest[2 * nd:3 * nd], rest[3 * nd:]
    branches = [_dil_fwd(qh[b], kh[b], va[b]) for b in range(nd)]
    o_sb, ltot, walked = _sb_fwd(qs, ks, vs, tri_later)
    o_dil, *lse, x1 = _out_proj_fwd([b[0] for b in branches], [b[1] for b in branches], o_sb, x, dil_w, sbn_w, w_out_g)
    h2, g, u, dy, loss_parts = _ffn_fwd(x1, target, ffn_w, wg_g, wu_g, wd_g)

    dwg, dwu, dwd, dh2p = _ffn_bwd(h2, dy, g, u, wg_g, wu_g, wd_g)
    dx1, *mid, dw_out, dffn_w, ddil_w, dsbn_w = _out_proj_bwd(
        dh2p, dy, x1, ffn_w, w_out_g, o_dil, o_sb, dil_w, sbn_w, seg_ones)
    do_dil, do_sb, delta = mid[:nd], mid[nd], mid[nd + 1:]
    dqs, dks, dvs = _sb_bwd(walked[:, 0, 0], qs, ks, vs, do_sb, ltot, tri_upto, tri_before)
    dbr = [_dil_bwd(qh[b], kh[b], va[b], do_dil[b], lse[b], delta[b]) for b in range(nd)]
    dproj, dqw, dkw = _qk_bwd([b[0] for b in dbr], [b[1] for b in dbr], [b[2] for b in dbr], dqs, dks, dvs,
                              qa, ka, qw, kw, cos_t, sin_t, seg_ones)
    dw_in, dhp = _in_proj_bwd(h, dproj, w_in_g)
    grad_x, dattn_w = _in_norm_bwd(dhp, dx1, x, attn_w)
    small = dict(attn=dattn_w, q=dqw[:, :HEAD_DIM], k=dkw[:, :HEAD_DIM], dil=ddil_w, sb=dsbn_w, ffn=dffn_w)
    return loss_parts, grad_x, small, dw_in, dw_out, dwg, dwu, dwd


HBM = pl.BlockSpec(memory_space=pltpu.HBM)
VMEM = pl.BlockSpec(memory_space=pltpu.VMEM)
CHIP_FLIPS = ((1, 0), (0, 1), (1, 1))


def _place():
    return lax.axis_index("x"), lax.axis_index("y"), lax.axis_index("c")


def _flip(v, d):
    return 1 - v if d else v


def _half_rows(c, n):
    return pl.ds(pl.multiple_of(c * (n // 2), 16), n // 2)


def _gather_weights(shards):
    n = len(shards)
    here = 2 * lax.axis_index("x") + lax.axis_index("y")
    slots = [lax.dynamic_update_slice(lax.empty((N_CHIPS,) + s.shape, s.dtype), s[None], (here, 0, 0)) for s in shards]

    def body(*refs):
        ins, outs = refs[:n], refs[n:2 * n]
        send, recv = refs[2 * n:]
        x, y, c = _place()
        p = 2 * x + y
        chips = [(_flip(x, dx), _flip(y, dy)) for dx, dy in CHIP_FLIPS]

        def copy(a, k, src, dst, to):
            return pltpu.make_async_remote_copy(src_ref=src, dst_ref=dst, send_sem=send.at[a * 6 + k],
                                                recv_sem=recv.at[a * 6 + k], device_id=to, device_id_type=MESH)

        sent = []
        for a in range(n):
            mine = _half_rows(c, ins[a].shape[1])
            for k, chip in enumerate(chips):
                sent.append(copy(a, k, ins[a].at[p, mine], outs[a].at[p, mine], (*chip, c)))
                sent[-1].start()
        for a in range(n):
            mine = _half_rows(c, ins[a].shape[1])
            for k, (qx, qy) in enumerate(chips):
                land = outs[a].at[2 * qx + qy, mine]
                copy(a, k, land, land, (qx, qy, c)).wait_recv()
                sent.append(copy(a, 3 + k, land, land, (x, y, 1 - c)))
                sent[-1].start()
        for a in range(n):
            other = _half_rows(1 - c, ins[a].shape[1])
            for k, (qx, qy) in enumerate(chips):
                land = outs[a].at[2 * qx + qy, other]
                copy(a, 3 + k, land, land, (x, y, 1 - c)).wait_recv()
        for cp in sent:
            cp.wait_send()

    return pl.pallas_call(
        body, name="gather_weights", in_specs=[HBM] * n, out_specs=[HBM] * n,
        out_shape=[jax.ShapeDtypeStruct(s.shape, s.dtype) for s in slots],
        input_output_aliases={a: a for a in range(n)},
        scratch_shapes=[pltpu.SemaphoreType.DMA((6 * n,)), pltpu.SemaphoreType.DMA((6 * n,))],
    )(*slots)


def _pair_exchange(grads, small):
    n = len(grads)

    def body(*refs):
        gin, sm = refs[:n], refs[n]
        gout, sm_all = refs[n + 1:2 * n + 1], refs[2 * n + 1]
        send, recv = refs[2 * n + 2:]
        x, y, c = _place()
        me = 4 * x + 2 * y + c
        big = []
        for a in range(n):
            theirs = _half_rows(1 - c, gin[a].shape[1])
            big.append(pltpu.make_async_remote_copy(
                src_ref=gin[a].at[:, theirs, :], dst_ref=gout[a], send_sem=send.at[a], recv_sem=recv.at[a],
                device_id=(x, y, 1 - c), device_id_type=MESH))
            big[-1].start()
        sm_all[pl.ds(me, 1)] = sm[...][None]
        tiny = []
        for k in range(1, N_DEV):
            px, py, pc = _flip(x, k & 4), _flip(y, k & 2), _flip(c, k & 1)
            tiny.append((pltpu.make_async_remote_copy(
                src_ref=sm, dst_ref=sm_all.at[me], send_sem=send.at[n + k - 1], recv_sem=recv.at[n + k - 1],
                device_id=(px, py, pc), device_id_type=MESH), 4 * px + 2 * py + pc))
            tiny[-1][0].start()
        for k, (cp, peer) in enumerate(tiny):
            pltpu.make_async_remote_copy(src_ref=sm, dst_ref=sm_all.at[peer], send_sem=send.at[n + k],
                                         recv_sem=recv.at[n + k], device_id=(x, y, c),
                                         device_id_type=MESH).wait_recv()
            cp.wait_send()
        for cp in big:
            cp.wait()

    halves = [jax.ShapeDtypeStruct((g.shape[0], g.shape[1] // 2, g.shape[2]), g.dtype) for g in grads]
    return pl.pallas_call(
        body, name="pair_exchange", in_specs=[HBM] * n + [VMEM], out_specs=[HBM] * n + [VMEM],
        out_shape=halves + [jax.ShapeDtypeStruct((N_DEV,) + small.shape, small.dtype)],
        scratch_shapes=[pltpu.SemaphoreType.DMA((n + N_DEV - 1,)), pltpu.SemaphoreType.DMA((n + N_DEV - 1,))],
    )(*grads, small)


def _chip_exchange(parts):
    n = len(parts)

    def body(*refs):
        pin, pout = refs[:n], refs[n:2 * n]
        send, recv = refs[2 * n:]
        x, y, c = _place()
        p = 2 * x + y
        chips = [(_flip(x, dx), _flip(y, dy)) for dx, dy in CHIP_FLIPS]
        sent = []
        for a in range(n):
            for k, (qx, qy) in enumerate(chips):
                sent.append(pltpu.make_async_remote_copy(
                    src_ref=pin[a].at[2 * qx + qy], dst_ref=pout[a].at[p], send_sem=send.at[a * 3 + k],
                    recv_sem=recv.at[a * 3 + k], device_id=(qx, qy, c), device_id_type=MESH))
                sent[-1].start()
        for a in range(n):
            for k, (qx, qy) in enumerate(chips):
                land = pout[a].at[2 * qx + qy]
                pltpu.make_async_remote_copy(src_ref=land, dst_ref=land, send_sem=send.at[a * 3 + k],
                                             recv_sem=recv.at[a * 3 + k], device_id=(qx, qy, c),
                                             device_id_type=MESH).wait_recv()
        for cp in sent:
            cp.wait_send()

    return pl.pallas_call(
        body, name="chip_exchange", in_specs=[HBM] * n, out_specs=[HBM] * n,
        out_shape=[jax.ShapeDtypeStruct(s.shape, s.dtype) for s in parts],
        scratch_shapes=[pltpu.SemaphoreType.DMA((3 * n,)), pltpu.SemaphoreType.DMA((3 * n,))],
    )(*parts)


def _pair_swap(halves):
    n = len(halves)

    def body(*refs):
        hin, hout = refs[:n], refs[n:2 * n]
        send, recv = refs[2 * n:]
        x, y, c = _place()
        swaps = [pltpu.make_async_remote_copy(src_ref=hin[a], dst_ref=hout[a], send_sem=send.at[a],
                                              recv_sem=recv.at[a], device_id=(x, y, 1 - c), device_id_type=MESH)
                 for a in range(n)]
        for cp in swaps:
            cp.start()
        for cp in swaps:
            cp.wait()

    return pl.pallas_call(
        body, name="pair_swap", in_specs=[HBM] * n, out_specs=[HBM] * n,
        out_shape=[jax.ShapeDtypeStruct(s.shape, s.dtype) for s in halves],
        scratch_shapes=[pltpu.SemaphoreType.DMA((n,)), pltpu.SemaphoreType.DMA((n,))],
    )(*halves)


def _pair_sum(grad, recv, c, tag):
    _, R, C = grad.shape
    hr = R // 2

    def body(c_ref, a_ref, b_ref, o_ref):
        o_ref[...] = (a_ref[...] + b_ref[...]).astype(BF16)

    return pl.pallas_call(
        body, name="pair_sum_" + tag,
        grid_spec=pltpu.PrefetchScalarGridSpec(
            num_scalar_prefetch=1, grid=(N_CHIPS,),
            in_specs=[pl.BlockSpec((1, hr, C), lambda s, cr: (s, cr[0], 0)),
                      pl.BlockSpec((1, hr, C), lambda s, cr: (s, 0, 0))],
            out_specs=pl.BlockSpec((1, hr, C), lambda s, cr: (s, 0, 0))),
        out_shape=jax.ShapeDtypeStruct((N_CHIPS, hr, C), BF16),
    )(c, grad, recv)


def _chip_sum(received, own, chip, tag):
    _, rows, C = received.shape
    tr = rows // 2

    def body(chip_ref, own_ref, r1_ref, r2_ref, r3_ref, o_ref):
        p = [r[0].astype(F32) for r in (own_ref, r1_ref, r2_ref, r3_ref)]
        o_ref[...] = (p[0] + p[1]) + (p[2] + p[3])

    def slot(k):
        return pl.BlockSpec((1, tr, C), lambda i, cr: (jnp.bitwise_xor(cr[0], k), i, 0))

    return pl.pallas_call(
        body, name="chip_sum_" + tag,
        grid_spec=pltpu.PrefetchScalarGridSpec(
            num_scalar_prefetch=1, grid=(rows // tr,), in_specs=[slot(0), slot(1), slot(2), slot(3)],
            out_specs=pl.BlockSpec((tr, C), lambda i, cr: (i, 0))),
        out_shape=jax.ShapeDtypeStruct((rows, C), F32),
    )(chip, own, received, received, received)


def _adamw_math(w, g, m, v):
    m = ADAM_B1 * m + (1.0 - ADAM_B1) * g
    v = ADAM_B2 * v + (1.0 - ADAM_B2) * (g * g)
    m_hat = m / (1.0 - ADAM_B1 ** ADAM_STEP)
    v_hat = v / (1.0 - ADAM_B2 ** ADAM_STEP)
    delta = -ADAM_LR * (m_hat / (jnp.sqrt(v_hat) + ADAM_EPS) + ADAM_WD * w)
    return delta, m, v


def _adamw(w, g_mine, g_other, m, v, c, tag):
    R, C = w.shape
    tr = R // 4

    def body(c_ref, w_ref, gm_ref, go_ref, m_ref, v_ref, g_ref, d_ref, nm_ref, nv_ref):
        g = jnp.where(pl.program_id(0) == c_ref[0], gm_ref[...], go_ref[...])
        g_ref[...] = g
        d_ref[...], nm_ref[...], nv_ref[...] = _adamw_math(w_ref[...], g, m_ref[...], v_ref[...])

    blk = pl.BlockSpec((tr, C), lambda h, i, cr: (2 * h + i, 0))
    half = pl.BlockSpec((tr, C), lambda h, i, cr: (i, 0))
    return pl.pallas_call(
        body, name="adamw_" + tag,
        grid_spec=pltpu.PrefetchScalarGridSpec(
            num_scalar_prefetch=1, grid=(2, 2), in_specs=[blk, half, half, blk, blk], out_specs=[blk] * 4),
        out_shape=[jax.ShapeDtypeStruct((R, C), F32)] * 4,
    )(c, w, g_mine, g_other, m, v)


def _small_update(all_small, w, m, v):
    def body(a_ref, w_ref, m_ref, v_ref, g_ref, d_ref, nm_ref, nv_ref):
        g = ((a_ref[0] + a_ref[1]) + (a_ref[2] + a_ref[3])) + ((a_ref[4] + a_ref[5]) + (a_ref[6] + a_ref[7]))
        g_ref[...] = g
        d_ref[...], nm_ref[...], nv_ref[...] = _adamw_math(w_ref[...], g, m_ref[...], v_ref[...])

    return pl.pallas_call(
        body, name="small_update", out_shape=[jax.ShapeDtypeStruct(w.shape, F32)] * 4,
    )(all_small, w, m, v)


SMALL_ROWS = (("attn", 0, 0), ("ffn", 1, 0), ("dil", 2, 0), ("sb", 2, GROUP), ("q", 3, 0), ("k", 3, HEAD_DIM))


def _pack_small(vals, D):
    rows = [jnp.zeros((1, D), F32) for _ in range(8)]
    for name, r, off in SMALL_ROWS:
        rows[r] = lax.dynamic_update_slice(rows[r], vals[name].astype(F32), (0, off))
    return jnp.concatenate(rows, axis=0)


def _unpack_small(packed, vals):
    return {name: packed[r:r + 1, off:off + vals[name].shape[1]] for name, r, off in SMALL_ROWS}


def kernel(x, attn_norm_w, w_in, q_norm_w, k_norm_w, dil_out_norm_w, sb_out_norm_w, w_out, ffn_norm_w, w_gate, w_up, w_down, loss_target, m_attn_norm_w, m_w_in, m_q_norm_w, m_k_norm_w, m_dil_out_norm_w, m_sb_out_norm_w, m_w_out, m_ffn_norm_w, m_w_gate, m_w_up, m_w_down, v_attn_norm_w, v_w_in, v_q_norm_w, v_k_norm_w, v_dil_out_norm_w, v_sb_out_norm_w, v_w_out, v_ffn_norm_w, v_w_gate, v_w_up, v_w_down):
    D = x.shape[-1]
    big_names = ("w_in", "w_out", "w_gate", "w_up", "w_down")
    big_w = dict(w_in=w_in[0], w_out=w_out[0], w_gate=w_gate[0], w_up=w_up[0], w_down=w_down[0])
    big_m = dict(w_in=m_w_in[0], w_out=m_w_out[0], w_gate=m_w_gate[0], w_up=m_w_up[0], w_down=m_w_down[0])
    big_v = dict(w_in=v_w_in[0], w_out=v_w_out[0], w_gate=v_w_gate[0], w_up=v_w_up[0], w_down=v_w_down[0])
    small_w = dict(attn=attn_norm_w, q=q_norm_w, k=k_norm_w, dil=dil_out_norm_w, sb=sb_out_norm_w, ffn=ffn_norm_w)
    small_m = dict(attn=m_attn_norm_w, q=m_q_norm_w, k=m_k_norm_w, dil=m_dil_out_norm_w, sb=m_sb_out_norm_w,
                   ffn=m_ffn_norm_w)
    small_v = dict(attn=v_attn_norm_w, q=v_q_norm_w, k=v_k_norm_w, dil=v_dil_out_norm_w, sb=v_sb_out_norm_w,
                   ffn=v_ffn_norm_w)

    gathered = _gather_weights([big_w[n].astype(BF16) for n in big_names])
    w_in_g, w_out_g, wg_g, wu_g, wd_g = gathered
    w_out_full = w_out_g.reshape(-1, D)

    loss_parts, grad_x, small_g, dw_in, dw_out, dwg, dwu, dwd = _local_step(
        x[0], loss_target[0], attn_norm_w, q_norm_w, k_norm_w, dil_out_norm_w, sb_out_norm_w, ffn_norm_w,
        w_in_g, w_out_full, wg_g, wu_g, wd_g)
    loss = lax.psum(jnp.sum(loss_parts[:, 0, 0]) * (0.5 / D), ("x", "y", "c"))

    full = [dw_in, dw_out.reshape(N_CHIPS, -1, D), dwg, dwu, dwd]
    c = lax.axis_index("c").astype(jnp.int32).reshape(1)
    *from_pair, all_small = _pair_exchange(full, _pack_small(small_g, D))
    chip_parts = [_pair_sum(g, r, c, n) for g, r, n in zip(full, from_pair, big_names)]
    from_chips = _chip_exchange(chip_parts)
    chip = (2 * lax.axis_index("x") + lax.axis_index("y")).astype(jnp.int32).reshape(1)
    halves = [_chip_sum(r, p, chip, n) for r, p, n in zip(from_chips, chip_parts, big_names)]
    others = _pair_swap(halves)
    big_out = {n: _adamw(big_w[n], mine, other, big_m[n], big_v[n], c, n)
               for n, mine, other in zip(big_names, halves, others)}
    sg, sd, sm, sv = _small_update(all_small, _pack_small(small_w, D), _pack_small(small_m, D),
                                   _pack_small(small_v, D))
    small_out = [_unpack_small(t, small_w) for t in (sg, sd, sm, sv)]

    order = (("attn", None), (None, "w_in"), ("q", None), ("k", None), ("dil", None), ("sb", None),
             (None, "w_out"), ("ffn", None), (None, "w_gate"), (None, "w_up"), (None, "w_down"))
    outs = [loss, grad_x[None]]
    for kind in range(4):
        for s_name, b_name in order:
            if s_name is not None:
                outs.append(small_out[kind][s_name])
            else:
                outs.append(big_out[b_name][kind][None])
    return tuple(outs)
```

```python
import functools

import jax
import jax.numpy as jnp
from jax import lax
from jax.experimental import pallas as pl
from jax.experimental.pallas import tpu as pltpu

F32 = jnp.float32
BF16 = jnp.bfloat16
MESH = pl.DeviceIdType.MESH

HEAD_DIM = 64
GROUP = 512
BLOCK = 128
LANES = 128
N_CHIPS = 4
N_DEV = 8
EPS = 1e-6
ROPE_THETA = 10000.0
DILATIONS = (1, 4, 16)
NEG = -1e30

ADAM_LR = 0.001
ADAM_B1 = 0.9
ADAM_B2 = 0.999
ADAM_EPS = 1e-08
ADAM_WD = 0.01
ADAM_STEP = 10


def _dot(a, b):
    return jnp.dot(a, b, preferred_element_type=F32)


def _dot_nt(a, b):
    return lax.dot_general(a, b, (((1,), (1,)), ((), ())), preferred_element_type=F32)


def _dot_tn(a, b):
    return lax.dot_general(a, b, (((0,), (0,)), ((), ())), preferred_element_type=F32)


def _split(v):
    hi = lax.bitcast_convert_type(lax.bitcast_convert_type(v, jnp.uint32) & jnp.uint32(0xFFFF0000), F32)
    return hi.astype(BF16), (v - hi).astype(BF16)


def _segsum(v, g):
    hi, lo = _split(v)
    return _dot(hi, g) + _dot(lo, g)


def _rot_half(x):
    outs = []
    for c in range(x.shape[1] // LANES):
        xc = x[:, c * LANES:(c + 1) * LANES]
        lane = lax.broadcasted_iota(jnp.int32, xc.shape, 1)
        first = (lane % HEAD_DIM) < (HEAD_DIM // 2)
        outs.append(jnp.where(first, pltpu.roll(xc, LANES - 32, 1), pltpu.roll(xc, 32, 1)))
    return outs[0] if len(outs) == 1 else jnp.concatenate(outs, axis=1)


def _rms(x):
    return lax.rsqrt(jnp.mean(x * x, axis=-1, keepdims=True) + EPS)


def _rms_bwd(dy, x, w):
    rstd = _rms(x)
    xh = x * rstd
    dxh = dy * w
    dx = rstd * (dxh - xh * jnp.mean(dxh * xh, axis=-1, keepdims=True))
    return dx, dy * xh


def _sigmoid(x):
    return 1.0 / (1.0 + jnp.exp(-x))


def _full(shape):
    n = len(shape)
    return pl.BlockSpec(shape, lambda *_: (0,) * n)


def _strided_spec(tm, r):
    return pl.BlockSpec((r, tm // r, GROUP), lambda i: (0, i, 0))


def _strided_shape(S, r, dtype):
    return jax.ShapeDtypeStruct((r, S // r, GROUP), dtype)


def _to_strided(scr, val, outs):
    chunks = range(GROUP // LANES)
    for k in chunks:
        scr[k] = val[:, _lanes(k)]
    for r, o_ref in outs:
        if r == 1:
            o_ref[0] = val.astype(o_ref.dtype)
            continue
        n = val.shape[0] // r
        for c in range(r):
            rows = pl.ds(c, n, stride=r)
            o_ref[c] = jnp.concatenate([scr.at[k][rows, :] for k in chunks], axis=1).astype(o_ref.dtype)


def _from_strided(scr, ref):
    r, n, _ = ref.shape
    if r == 1:
        return ref[0]
    chunks = range(GROUP // LANES)
    for c in range(r):
        plane = ref[c]
        for k in chunks:
            scr.at[k][pl.ds(c, n, stride=r), :] = plane[:, _lanes(k)]
    return jnp.concatenate([scr[k] for k in chunks], axis=1)


def _strided_scratch(tm):
    return pltpu.VMEM((GROUP // LANES, tm, LANES), F32)


def _tile4(t):
    return jnp.concatenate([t] * (GROUP // LANES), axis=1)


def _in_proj_fwd(x, attn_w, w_in_g, qw, kw, cos_t, sin_t, seg_ones):
    S, D = x.shape
    tm = 512
    wcols = w_in_g.shape[2]
    nd = len(DILATIONS)

    def body(x_ref, aw_ref, w_ref, qw_ref, kw_ref, cos_ref, sin_ref, g_ref, h_ref, qa_ref, ka_ref, *rest):
        q_refs, k_refs, v_refs = rest[:nd], rest[nd:2 * nd], rest[2 * nd:3 * nd]
        qs_ref, ks_ref, vs_ref, scr = rest[3 * nd:]
        xv = x_ref[...]
        h = (xv * _rms(xv) * aw_ref[...]).astype(BF16)
        h_ref[...] = h
        proj = jnp.concatenate([_dot(h, w_ref[j]) for j in range(N_CHIPS)], axis=1)
        qa = proj[:, 0 * GROUP:1 * GROUP]
        ka = proj[:, 1 * GROUP:2 * GROUP]
        qa_ref[...] = qa
        ka_ref[...] = ka
        _to_strided(scr, proj[:, 2 * GROUP:3 * GROUP], list(zip(DILATIONS, v_refs)))
        qs_ref[...] = proj[:, 3 * GROUP:4 * GROUP].astype(BF16)
        ks_ref[...] = proj[:, 4 * GROUP:5 * GROUP].astype(BF16)
        vs_ref[...] = proj[:, 5 * GROUP:6 * GROUP].astype(BF16)
        g = g_ref[...]
        cos = _tile4(cos_ref[...])
        sin = _tile4(sin_ref[...])
        for t, w_r, o_rs in ((qa, qw_ref, q_refs), (ka, kw_ref, k_refs)):
            rstd = lax.rsqrt(_segsum(t * t, g) * (1.0 / HEAD_DIM) + EPS)
            tn = t * rstd * w_r[...]
            _to_strided(scr, tn * cos + _rot_half(tn) * sin, list(zip(DILATIONS, o_rs)))

    row = lambda i: (i, 0)
    tile = lambda n, dt: jax.ShapeDtypeStruct((S, n), dt)
    planes = [_strided_spec(tm, r) for r in DILATIONS]
    return pl.pallas_call(
        body, name="in_proj_fwd", grid=(S // tm,),
        in_specs=[pl.BlockSpec((tm, D), row), _full((1, D)), _full((N_CHIPS, D, wcols)),
                  _full((1, GROUP)), _full((1, GROUP)),
                  pl.BlockSpec((tm, LANES), row), pl.BlockSpec((tm, LANES), row),
                  _full((GROUP, GROUP))],
        out_specs=[pl.BlockSpec((tm, D), row)] + [pl.BlockSpec((tm, GROUP), row)] * 2 + planes * 3
                  + [pl.BlockSpec((tm, GROUP), row)] * 3,
        out_shape=[tile(D, BF16), tile(GROUP, F32), tile(GROUP, F32)]
                  + [_strided_shape(S, r, BF16) for r in DILATIONS] * 3 + [tile(GROUP, BF16)] * 3,
        scratch_shapes=[_strided_scratch(tm)],
    )(x, attn_w, w_in_g, qw, kw, cos_t, sin_t, seg_ones)


def _dil_fwd(q, k, v, slots):
    r, L, _ = q.shape
    nb = L // BLOCK
    P = GROUP // LANES
    steps = r * nb

    def body(q_ref, kc_ref, kp_ref, vc_ref, vp_ref, slot_in, o_ref, l_ref, slot_out, send, recv):
        n = pl.program_id(1)
        step = pl.program_id(0) * nb + n
        start, forward, finish = _gather_plan(slot_in, slot_out, send, recv)
        pl.when(step == 0)(start)
        pl.when(step == (2 * steps) // 3)(forward)
        rowi = lax.broadcasted_iota(jnp.int32, (BLOCK, BLOCK), 0)
        coli = lax.broadcasted_iota(jnp.int32, (BLOCK, BLOCK), 1)
        first = coli < HEAD_DIM
        masks = (coli <= rowi, jnp.logical_and(coli >= rowi, n > 0))
        s2 = {}
        for hp in range(P):
            q2 = _scaled(q_ref[:, _lanes(hp)])
            for b, k_ref in enumerate((kc_ref, kp_ref)):
                s2[hp, b] = _dot_nt(q2, _by_head(k_ref[:, _lanes(hp)], first))
        ps, inv, lse = {}, {}, {}
        for hp in range(P):
            for h in range(2):
                s = [jnp.where(masks[b], s2[hp, b][:, h * BLOCK:(h + 1) * BLOCK], NEG) for b in range(2)]
                m = jnp.maximum(jnp.max(s[0], axis=1, keepdims=True), jnp.max(s[1], axis=1, keepdims=True))
                p = [jnp.exp(s[b] - m) for b in range(2)]
                den = jnp.sum(p[0], axis=1, keepdims=True) + jnp.sum(p[1], axis=1, keepdims=True)
                ps[hp, h] = [p[b].astype(BF16) for b in range(2)]
                inv[hp, h] = 1.0 / den
                lse[hp, h] = m + jnp.log(den)
        for hp in range(P):
            o = jnp.zeros((BLOCK, LANES), F32)
            for b, v_ref in enumerate((vc_ref, vp_ref)):
                o = o + _dot(jnp.concatenate([ps[hp, 0][b], ps[hp, 1][b]], axis=1),
                             _by_head(v_ref[:, _lanes(hp)], first))
            o_ref[:, _lanes(hp)] = o * jnp.where(first, inv[hp, 0], inv[hp, 1])
            l_ref[:, _lanes(hp)] = jnp.where(first, lse[hp, 0], lse[hp, 1])
        pl.when(step == steps - 1)(finish)

    cur = pl.BlockSpec((None, BLOCK, GROUP), lambda c, n: (c, n, 0))
    prev = pl.BlockSpec((None, BLOCK, GROUP), lambda c, n: (c, jnp.maximum(n - 1, 0), 0))
    return pl.pallas_call(
        body, name="dil_fwd_r%d" % r, grid=(r, nb),
        in_specs=[cur, cur, prev, cur, prev, HBM], out_specs=[cur, cur, HBM],
        out_shape=[jax.ShapeDtypeStruct(q.shape, F32)] * 2 + [jax.ShapeDtypeStruct(slots.shape, slots.dtype)],
        input_output_aliases={5: 2},
        scratch_shapes=[pltpu.SemaphoreType.DMA((6,)), pltpu.SemaphoreType.DMA((6,))],
    )(q, k, k, v, v, slots)


def _dil_bwd(q, k, v, do, lse, delta, part):
    r, L, _ = q.shape
    nb = L // BLOCK
    P = GROUP // LANES
    scale = HEAD_DIM ** -0.5

    def body(qc_ref, qn_ref, doc_ref, don_ref, lc_ref, ln_ref, dc_ref, dn_ref, k_ref, v_ref, part_in,
             dq_ref, dk_ref, dv_ref, part_out, carry, send, recv):
        j = pl.program_id(1)
        step = pl.program_id(0) * nb + j
        start, finish = _chip_send_plan(part_in, part_out, send, recv)
        pl.when(step == 0)(start)
        rowi = lax.broadcasted_iota(jnp.int32, (BLOCK, BLOCK), 0)
        coli = lax.broadcasted_iota(jnp.int32, (BLOCK, BLOCK), 1)
        first = coli < HEAD_DIM
        sides = ((qc_ref, doc_ref, lc_ref, dc_ref, coli <= rowi),
                 (qn_ref, don_ref, ln_ref, dn_ref, jnp.logical_and(coli >= rowi, j < nb - 1)))

        @pl.when(j == 0)
        def _():
            carry[...] = jnp.zeros_like(carry)

        kcat, q2, do2, s2, dp2 = {}, {}, {}, {}, {}
        for hp in range(P):
            kcat[hp] = _by_head(k_ref[:, _lanes(hp)], first)
            vcat = _by_head(v_ref[:, _lanes(hp)], first)
            for x, (q_r, do_r, _, _, _) in enumerate(sides):
                q2[hp, x] = _scaled(q_r[:, _lanes(hp)])
                do2[hp, x] = do_r[:, _lanes(hp)]
                s2[hp, x] = _dot_nt(q2[hp, x], kcat[hp])
                dp2[hp, x] = _dot_nt(do2[hp, x], vcat)
        pcat, dscat = {}, {}
        for hp in range(P):
            for x, (_, _, l_r, d_r, msk) in enumerate(sides):
                ps, dss = [], []
                for h in range(2):
                    col = hp * LANES + h * HEAD_DIM
                    half = slice(h * BLOCK, (h + 1) * BLOCK)
                    p = jnp.where(msk, jnp.exp(s2[hp, x][:, half] - l_r[:, col:col + 1]), 0.0)
                    ps.append(p.astype(BF16))
                    dss.append((p * (dp2[hp, x][:, half] - d_r[:, col:col + 1])).astype(BF16))
                pcat[hp, x] = jnp.concatenate(ps, axis=1)
                dscat[hp, x] = jnp.concatenate(dss, axis=1)
        for hp in range(P):
            dv2 = _dot_tn(pcat[hp, 0], do2[hp, 0]) + _dot_tn(pcat[hp, 1], do2[hp, 1])
            dk2 = _dot_tn(dscat[hp, 0], q2[hp, 0]) + _dot_tn(dscat[hp, 1], q2[hp, 1])
            dv_ref[:, _lanes(hp)] = jnp.where(first, dv2[:BLOCK], dv2[BLOCK:])
            dk_ref[:, _lanes(hp)] = jnp.where(first, dk2[:BLOCK], dk2[BLOCK:])
            dq_ref[:, _lanes(hp)] = carry[:, _lanes(hp)] + _dot(dscat[hp, 0], kcat[hp]) * scale
            carry[:, _lanes(hp)] = _dot(dscat[hp, 1], kcat[hp]) * scale
        pl.when(step == r * nb - 1)(finish)

    cur = pl.BlockSpec((None, BLOCK, GROUP), lambda c, n: (c, n, 0))
    nxt = pl.BlockSpec((None, BLOCK, GROUP), lambda c, n: (c, jnp.minimum(n + 1, nb - 1), 0))
    return pl.pallas_call(
        body, name="dil_bwd_r%d" % r, grid=(r, nb),
        in_specs=[cur, nxt, cur, nxt, cur, nxt, cur, nxt, cur, cur, HBM], out_specs=[cur, cur, cur, HBM],
        out_shape=[jax.ShapeDtypeStruct(q.shape, F32)] * 3 + [jax.ShapeDtypeStruct(part.shape, part.dtype)],
        scratch_shapes=[pltpu.VMEM((BLOCK, GROUP), F32), pltpu.SemaphoreType.DMA((3,)), pltpu.SemaphoreType.DMA((3,))],
    )(q, q, do, do, lse, lse, delta, delta, k, v, part)


SB_TILES = 2
SB_PAIRS_FWD = 4
SB_PAIRS_BWD = 2
SB_DEAD = -110.0


def _lanes(hp):
    return slice(hp * LANES, (hp + 1) * LANES)


def _sb_logits(z, valid):
    e = jnp.exp(-jnp.abs(z))
    lb = jnp.minimum(z, 0.0) - jnp.log(1.0 + e)
    lk = lb - z
    if valid is not None:
        lk = jnp.where(valid, lk, 0.0)
    return e, lb, lk


def _by_head(t, first):
    zero = jnp.zeros_like(t)
    return jnp.concatenate([jnp.where(first, t, zero), jnp.where(first, zero, t)], axis=0)


def _sb_valid(i, j):
    rowi = lax.broadcasted_iota(jnp.int32, (BLOCK, BLOCK), 0)
    coli = lax.broadcasted_iota(jnp.int32, (BLOCK, BLOCK), 1)
    return (coli - rowi) < (i - j) * BLOCK


def _scaled(q):
    return (q.astype(F32) * (HEAD_DIM ** -0.5)).astype(BF16)


def _sb_fwd(qs, ks, vs, tri_later):
    S = qs.shape[0]
    P = SB_PAIRS_FWD
    W = P * LANES

    def body(q_ref, k_ref, v_ref, u_ref, o_ref, lt_ref, from_ref):
        i = pl.program_id(1)
        first = lax.broadcasted_iota(jnp.int32, (BLOCK, LANES), 1) < HEAD_DIM
        q2 = [_scaled(q_ref[:, _lanes(hp)]) for hp in range(P)]

        def chunk(ci, carry, masked):
            runs, accs = list(carry[0]), list(carry[1])
            units = [(t, hp) for t in reversed(range(SB_TILES)) for hp in range(P)]
            z2s, lbs, c2s = {}, {}, {}
            for t, hp in units:
                off = pl.multiple_of((ci * SB_TILES + t) * BLOCK, BLOCK)
                z2s[t, hp] = _dot_nt(q2[hp], _by_head(k_ref[pl.ds(off, BLOCK), _lanes(hp)], first))
            for t, hp in units:
                valid = _sb_valid(i, ci * SB_TILES + t) if masked else None
                for h in range(2):
                    _, lb, lk = _sb_logits(z2s[t, hp][:, h * BLOCK:(h + 1) * BLOCK], valid)
                    lbs[t, hp, h] = lb
                    c2s[t, hp, h] = _dot(jnp.concatenate(_split(lk), axis=1), u_ref[...])
            for t, hp in units:
                off = pl.multiple_of((ci * SB_TILES + t) * BLOCK, BLOCK)
                valid = _sb_valid(i, ci * SB_TILES + t) if masked else None
                a2 = []
                for h in range(2):
                    a = jnp.exp(lbs[t, hp, h] + c2s[t, hp, h][:, :BLOCK] + runs[2 * hp + h])
                    if masked:
                        a = jnp.where(valid, a, 0.0)
                    a2.append(a.astype(BF16))
                    runs[2 * hp + h] = runs[2 * hp + h] + c2s[t, hp, h][:, BLOCK:]
                vcat = _by_head(v_ref[pl.ds(off, BLOCK), _lanes(hp)], first)
                accs[hp] = accs[hp] + _dot(jnp.concatenate(a2, axis=1), vcat)
            return tuple(runs), tuple(accs)

        def alive(runs):
            top = functools.reduce(jnp.maximum, runs)
            return (jnp.max(top) > SB_DEAD).astype(jnp.int32)

        def step(c):
            t, _, runs, accs = c
            runs, accs = chunk(nfull - 1 - t, (runs, accs), False)
            return t + 1, alive(runs), runs, accs

        zero = jnp.zeros((BLOCK, LANES), F32)
        nfull = i // SB_TILES
        runs, accs = chunk(nfull, ((zero,) * (2 * P), (zero,) * P), True)
        done, _, runs, accs = lax.while_loop(lambda c: jnp.logical_and(c[0] < nfull, c[1] > 0), step,
                                             (jnp.int32(0), alive(runs), runs, accs))
        for hp in range(P):
            o_ref[:, _lanes(hp)] = accs[hp]
            lt_ref[:, _lanes(hp)] = jnp.where(first, runs[2 * hp], runs[2 * hp + 1])
        from_ref[...] = jnp.full(from_ref.shape, nfull - done, jnp.int32)

    assert W == GROUP
    blk = pl.BlockSpec((BLOCK, W), lambda hp, i: (i, hp))
    col = pl.BlockSpec((S, W), lambda hp, i: (0, hp))
    return pl.pallas_call(
        body, name="sb_fwd", grid=(GROUP // W, S // BLOCK),
        in_specs=[blk, col, col, _full((2 * BLOCK, 2 * BLOCK))],
        out_specs=[blk, blk, pl.BlockSpec((1, 8, LANES), lambda hp, i: (i, 0, 0))],
        out_shape=[jax.ShapeDtypeStruct((S, GROUP), F32)] * 2 + [jax.ShapeDtypeStruct((S // BLOCK, 8, LANES), jnp.int32)],
    )(qs, ks, vs, tri_later)


def _sb_bwd(first_chunk, qs, ks, vs, do, ltot, tri_upto, tri_before):
    S = qs.shape[0]
    P = SB_PAIRS_BWD
    W = P * LANES

    def body(from_ref, q_ref, k_ref, v_ref, do_ref, lt_ref, w_ref, x_ref, dq_ref, dk_ref, dv_ref):
        i = pl.program_id(1)

        @pl.when(i == 0)
        def _():
            dk_ref[...] = jnp.zeros_like(dk_ref)
            dv_ref[...] = jnp.zeros_like(dv_ref)

        first = lax.broadcasted_iota(jnp.int32, (BLOCK, LANES), 1) < HEAD_DIM
        q2 = [_scaled(q_ref[:, _lanes(hp)]) for hp in range(P)]
        do2 = [do_ref[:, _lanes(hp)] for hp in range(P)]
        totals = [jnp.broadcast_to(lt_ref[:, n * HEAD_DIM:n * HEAD_DIM + 1], (BLOCK, LANES)) for n in range(2 * P)]

        def chunk(ci, carry, masked):
            keeps, grads, dqs = list(carry[0]), list(carry[1]), list(carry[2])
            units = [(t, hp) for t in range(SB_TILES) for hp in range(P)]
            offs = [pl.multiple_of((ci * SB_TILES + t) * BLOCK, BLOCK) for t in range(SB_TILES)]
            valids = [_sb_valid(i, ci * SB_TILES + t) if masked else None for t in range(SB_TILES)]
            kcat, z2, da2, es, lbs, c2s, as_, des, p2s = {}, {}, {}, {}, {}, {}, {}, {}, {}
            for t, hp in units:
                kcat[t, hp] = _by_head(k_ref[pl.ds(offs[t], BLOCK), _lanes(hp)], first)
                z2[t, hp] = _dot_nt(q2[hp], kcat[t, hp])
                da2[t, hp] = _dot_nt(do2[hp], _by_head(v_ref[pl.ds(offs[t], BLOCK), _lanes(hp)], first))
            for t, hp in units:
                for h in range(2):
                    es[t, hp, h], lbs[t, hp, h], lk = _sb_logits(z2[t, hp][:, h * BLOCK:(h + 1) * BLOCK], valids[t])
                    c2s[t, hp, h] = _dot(jnp.concatenate(_split(lk), axis=1), w_ref[...])
            for t, hp in units:
                for h in range(2):
                    n = 2 * hp + h
                    a = jnp.exp(lbs[t, hp, h] + (totals[n] - (keeps[n] + c2s[t, hp, h][:, :BLOCK])))
                    if masked:
                        a = jnp.where(valids[t], a, 0.0)
                    keeps[n] = keeps[n] + c2s[t, hp, h][:, BLOCK:]
                    de = a * da2[t, hp][:, h * BLOCK:(h + 1) * BLOCK]
                    as_[t, hp, h], des[t, hp, h] = a.astype(BF16), de
                    p2s[t, hp, h] = _dot(jnp.concatenate(_split(de), axis=1), x_ref[...])
            for t, hp in units:
                dz2 = []
                for h in range(2):
                    n = 2 * hp + h
                    e = es[t, hp, h]
                    sig = jnp.where(z2[t, hp][:, h * BLOCK:(h + 1) * BLOCK] >= 0.0, 1.0, e) / (1.0 + e)
                    dz = des[t, hp, h] * (1.0 - sig) - (grads[n] + p2s[t, hp, h][:, :BLOCK]) * sig
                    if masked:
                        dz = jnp.where(valids[t], dz, 0.0)
                    grads[n] = grads[n] + p2s[t, hp, h][:, BLOCK:]
                    dz2.append(dz.astype(BF16))
                dzcat = jnp.concatenate(dz2, axis=1)
                dk2 = _dot_tn(dzcat, q2[hp])
                dv2 = _dot_tn(jnp.concatenate([as_[t, hp, 0], as_[t, hp, 1]], axis=1), do2[hp])
                dk_ref[pl.ds(offs[t], BLOCK), _lanes(hp)] += jnp.where(first, dk2[:BLOCK], dk2[BLOCK:])
                dv_ref[pl.ds(offs[t], BLOCK), _lanes(hp)] += jnp.where(first, dv2[:BLOCK], dv2[BLOCK:])
                dqs[hp] = dqs[hp] + _dot(dzcat, kcat[t, hp])
            return tuple(keeps), tuple(grads), tuple(dqs)

        zero = jnp.zeros((BLOCK, LANES), F32)
        nfull = i // SB_TILES
        carry = lax.fori_loop(from_ref[i], nfull, lambda t, c: chunk(t, c, False),
                              ((zero,) * (2 * P), (zero,) * (2 * P), (zero,) * P))
        carry = chunk(nfull, carry, True)
        for hp in range(P):
            dq_ref[:, _lanes(hp)] = carry[2][hp] * (HEAD_DIM ** -0.5)

    blk = pl.BlockSpec((BLOCK, W), lambda hp, i, fr: (i, hp))
    col = pl.BlockSpec((S, W), lambda hp, i, fr: (0, hp))
    tri = pl.BlockSpec((2 * BLOCK, 2 * BLOCK), lambda hp, i, fr: (0, 0))
    return pl.pallas_call(
        body, name="sb_bwd",
        grid_spec=pltpu.PrefetchScalarGridSpec(
            num_scalar_prefetch=1, grid=(GROUP // W, S // BLOCK),
            in_specs=[blk, col, col, blk, blk, tri, tri], out_specs=[blk, col, col]),
        out_shape=[jax.ShapeDtypeStruct((S, GROUP), F32)] * 3,
    )(first_chunk, qs, ks, vs, do, ltot, tri_upto, tri_before)


def _out_proj_fwd(o_br, l_br, o_sb, x, w_dil, w_sbn, w_out_g):
    S, D = x.shape
    tm = 512

    def body(o0, o1, o2, l0, l1, l2, os_ref, x_ref, wd_ref, ws_ref, w_ref, od_ref, s0, s1, s2, x1_ref, scr):
        ls = [_from_strided(scr, l) for l in (l0, l1, l2)]
        os_ = [_from_strided(scr, o) for o in (o0, o1, o2)]
        m = jnp.maximum(jnp.maximum(ls[0], ls[1]), ls[2])
        es = [jnp.exp(l - m) for l in ls]
        den = es[0] + es[1] + es[2]
        od = (es[0] * os_[0] + es[1] * os_[1] + es[2] * os_[2]) / den
        od_ref[...] = od
        _to_strided(scr, m + jnp.log(den), list(zip(DILATIONS, (s0, s1, s2))))
        osb = os_ref[...]
        mixed = jnp.concatenate([(od * _rms(od) * wd_ref[...]).astype(BF16),
                                 (osb * _rms(osb) * ws_ref[...]).astype(BF16)], axis=1)
        x1_ref[...] = x_ref[...] + _dot(mixed, w_ref[...])

    row = lambda i: (i, 0)
    g = pl.BlockSpec((tm, GROUP), row)
    d = pl.BlockSpec((tm, D), row)
    planes = [_strided_spec(tm, r) for r in DILATIONS]
    return pl.pallas_call(
        body, name="out_proj_fwd", grid=(S // tm,),
        in_specs=planes * 2 + [g, d, _full((1, GROUP)), _full((1, GROUP)), _full((2 * GROUP, D))],
        out_specs=[g] + planes + [d],
        out_shape=[jax.ShapeDtypeStruct((S, GROUP), F32)] + [_strided_shape(S, r, F32) for r in DILATIONS]
                  + [jax.ShapeDtypeStruct((S, D), F32)],
        scratch_shapes=[_strided_scratch(tm)],
    )(*o_br, *l_br, o_sb, x, w_dil, w_sbn, w_out_g)


def _ffn_fwd(x1, target, ffn_w, wg_g, wu_g, wd_g):
    S, D = x1.shape
    F = wg_g.shape[2]
    tm = 512
    nt = S // tm

    def body(x_ref, t_ref, nw_ref, wg_ref, wu_ref, wd_ref, h_ref, g_ref, u_ref, dy_ref, loss_ref, h_s, acc):
        j = pl.program_id(1)

        @pl.when(j == 0)
        def _():
            xv = x_ref[...]
            h = (xv * _rms(xv) * nw_ref[...]).astype(BF16)
            h_s[...] = h
            h_ref[...] = h
            acc[...] = xv

        h = h_s[...]
        g = _dot(h, wg_ref[0])
        u = _dot(h, wu_ref[0])
        g_ref[0] = g.astype(BF16)
        u_ref[0] = u.astype(BF16)
        a = (g * _sigmoid(g) * u).astype(BF16)
        acc[...] += _dot(a, wd_ref[0])

        @pl.when(j == N_CHIPS - 1)
        def _():
            err = acc[...] - t_ref[...]
            dy_ref[...] = err * (1.0 / D)
            loss_ref[...] = jnp.full(loss_ref.shape, jnp.sum(err * err), F32)

    row = lambda t, j: (t, 0)
    shard = lambda t, j: (j, 0, 0)
    act = lambda t, j: (j, t, 0)
    return pl.pallas_call(
        body, name="ffn_fwd", grid=(nt, N_CHIPS),
        in_specs=[pl.BlockSpec((tm, D), row), pl.BlockSpec((tm, D), row), pl.BlockSpec((1, D), lambda t, j: (0, 0)),
                  pl.BlockSpec((1, D, F), shard), pl.BlockSpec((1, D, F), shard), pl.BlockSpec((1, F, D), shard)],
        out_specs=[pl.BlockSpec((tm, D), row), pl.BlockSpec((1, tm, F), act), pl.BlockSpec((1, tm, F), act),
                   pl.BlockSpec((tm, D), row), pl.BlockSpec((1, 8, LANES), lambda t, j: (t, 0, 0))],
        out_shape=[jax.ShapeDtypeStruct((S, D), BF16), jax.ShapeDtypeStruct((N_CHIPS, S, F), BF16),
                   jax.ShapeDtypeStruct((N_CHIPS, S, F), BF16), jax.ShapeDtypeStruct((S, D), F32),
                   jax.ShapeDtypeStruct((nt, 8, LANES), F32)],
        scratch_shapes=[pltpu.VMEM((tm, D), BF16), pltpu.VMEM((tm, D), F32)],
    )(x1, target, ffn_w, wg_g, wu_g, wd_g)


def _ffn_bwd(h2, dy, g, u, wg_g, wu_g, wd_g):
    S, D = dy.shape
    F = wg_g.shape[2]
    tm = 512

    def body(h_ref, dy_ref, g_ref, u_ref, wg_ref, wu_ref, wd_ref, dwg_ref, dwu_ref, dwd_ref, dh_ref):
        t = pl.program_id(1)

        @pl.when(t == 0)
        def _():
            dwg_ref[...] = jnp.zeros_like(dwg_ref)
            dwu_ref[...] = jnp.zeros_like(dwu_ref)
            dwd_ref[...] = jnp.zeros_like(dwd_ref)

        h = h_ref[...]
        dyb = dy_ref[...].astype(BF16)
        gv = g_ref[0].astype(F32)
        uv = u_ref[0].astype(F32)
        da = _dot_nt(dyb, wd_ref[0])
        sg = _sigmoid(gv)
        silu = gv * sg
        du = (da * silu).astype(BF16)
        dg = (da * uv * (sg * (1.0 + gv * (1.0 - sg)))).astype(BF16)
        dwd_ref[0] += _dot_tn((silu * uv).astype(BF16), dyb)
        dwg_ref[0] += _dot_tn(h, dg)
        dwu_ref[0] += _dot_tn(h, du)
        dh_ref[0] = _dot_nt(dg, wg_ref[0]) + _dot_nt(du, wu_ref[0])

    row = lambda j, t: (t, 0)
    shard = lambda j, t: (j, 0, 0)
    act = lambda j, t: (j, t, 0)
    return pl.pallas_call(
        body, name="ffn_bwd", grid=(N_CHIPS, S // tm),
        in_specs=[pl.BlockSpec((tm, D), row), pl.BlockSpec((tm, D), row),
                  pl.BlockSpec((1, tm, F), act), pl.BlockSpec((1, tm, F), act),
                  pl.BlockSpec((1, D, F), shard), pl.BlockSpec((1, D, F), shard), pl.BlockSpec((1, F, D), shard)],
        out_specs=[pl.BlockSpec((1, D, F), shard), pl.BlockSpec((1, D, F), shard), pl.BlockSpec((1, F, D), shard),
                   pl.BlockSpec((1, tm, D), act)],
        out_shape=[jax.ShapeDtypeStruct((N_CHIPS, D, F), F32), jax.ShapeDtypeStruct((N_CHIPS, D, F), F32),
                   jax.ShapeDtypeStruct((N_CHIPS, F, D), F32), jax.ShapeDtypeStruct((N_CHIPS, S, D), F32)],
    )(h2, dy, g, u, wg_g, wu_g, wd_g)


def _out_proj_bwd(dh2p, dy, x1, ffn_w, w_out_g, o_dil, o_sb, w_dil, w_sbn, seg_ones, ffn_grads):
    S, D = dy.shape
    tm = 256
    ng = len(ffn_grads)

    def body(dh_ref, dy_ref, x1_ref, nw_ref, w_ref, od_ref, os_ref, wd_ref, ws_ref, g_ref, *rest):
        gin, rest = rest[:ng], rest[ng:]
        dx1_ref, dod0, dod1, dod2, dos_ref, dl0, dl1, dl2, dw_ref, dnw_ref, dwd_ref, dws_ref = rest[:12]
        gout, (scr, send, recv) = rest[12:12 + ng], rest[12 + ng:]
        i = pl.program_id(0)
        plans = [_pair_send_plan(gin[a], gout[a], send.at[a], recv.at[a]) for a in range(ng)]

        @pl.when(i == 0)
        def _():
            for start, _ in plans:
                start()

        @pl.when(i == 0)
        def _():
            for r_ in (dw_ref, dnw_ref, dwd_ref, dws_ref):
                r_[...] = jnp.zeros_like(r_)

        dh2 = (dh_ref[0] + dh_ref[1]) + (dh_ref[2] + dh_ref[3])
        dxn, dwn = _rms_bwd(dh2, x1_ref[...], nw_ref[...])
        dnw_ref[...] += jnp.sum(dwn, axis=0, keepdims=True)
        dx1 = dy_ref[...] + dxn
        dx1_ref[...] = dx1
        dx1b = dx1.astype(BF16)
        dmix = _dot_nt(dx1b, w_ref[...])
        od = od_ref[...]
        osb = os_ref[...]
        mixed = jnp.concatenate([(od * _rms(od) * wd_ref[...]).astype(BF16),
                                 (osb * _rms(osb) * ws_ref[...]).astype(BF16)], axis=1)
        dw_ref[...] += _dot_tn(mixed, dx1b)
        do, dwo = _rms_bwd(dmix[:, :GROUP], od, wd_ref[...])
        dwd_ref[...] += jnp.sum(dwo, axis=0, keepdims=True)
        _to_strided(scr, do, list(zip(DILATIONS, (dod0, dod1, dod2))))
        _to_strided(scr, _segsum(do * od, g_ref[...]), list(zip(DILATIONS, (dl0, dl1, dl2))))
        do, dwo = _rms_bwd(dmix[:, GROUP:], osb, ws_ref[...])
        dws_ref[...] += jnp.sum(dwo, axis=0, keepdims=True)
        dos_ref[...] = do.astype(BF16)

        @pl.when(i == S // tm - 1)
        def _():
            for _, finish in plans:
                finish()

    row = lambda i: (i, 0)
    gsp = pl.BlockSpec((tm, GROUP), row)
    dsp = pl.BlockSpec((tm, D), row)
    planes = [_strided_spec(tm, r) for r in DILATIONS]
    halves = [jax.ShapeDtypeStruct((g.shape[0], g.shape[1] // 2, g.shape[2]), g.dtype) for g in ffn_grads]
    return pl.pallas_call(
        body, name="out_proj_bwd", grid=(S // tm,),
        in_specs=[pl.BlockSpec((N_CHIPS, tm, D), lambda i: (0, i, 0)), dsp, dsp, _full((1, D)), _full((2 * GROUP, D)),
                  gsp, gsp, _full((1, GROUP)), _full((1, GROUP)), _full((GROUP, GROUP))] + [HBM] * ng,
        out_specs=[dsp] + planes + [gsp] + planes
                  + [_full((2 * GROUP, D)), _full((1, D)), _full((1, GROUP)), _full((1, GROUP))] + [HBM] * ng,
        out_shape=[jax.ShapeDtypeStruct((S, D), F32)] + [_strided_shape(S, r, BF16) for r in DILATIONS]
                  + [jax.ShapeDtypeStruct((S, GROUP), BF16)] + [_strided_shape(S, r, F32) for r in DILATIONS]
                  + [jax.ShapeDtypeStruct((2 * GROUP, D), F32),
                     jax.ShapeDtypeStruct((1, D), F32), jax.ShapeDtypeStruct((1, GROUP), F32),
                     jax.ShapeDtypeStruct((1, GROUP), F32)] + halves,
        scratch_shapes=[_strided_scratch(tm), pltpu.SemaphoreType.DMA((ng,)), pltpu.SemaphoreType.DMA((ng,))],
    )(dh2p, dy, x1, ffn_w, w_out_g, o_dil, o_sb, w_dil, w_sbn, seg_ones, *ffn_grads)


def _qk_bwd(dq_br, dk_br, dv_br, dqs, dks, dvs, qa, ka, qw, kw, cos_t, sin_t, seg_ones):
    S = qa.shape[0]
    tm = 256

    def body(q0, q1, q2, k0, k1, k2, v0, v1, v2, dqs_ref, dks_ref, dvs_ref, qa_ref, ka_ref, qw_ref, kw_ref,
             cos_ref, sin_ref, g_ref, dp_ref, dqw_ref, dkw_ref, accq, acck, scr):
        i = pl.program_id(0)

        @pl.when(i == 0)
        def _():
            accq[...] = jnp.zeros_like(accq)
            acck[...] = jnp.zeros_like(acck)

        def branches(refs):
            return (_from_strided(scr, refs[0]) + _from_strided(scr, refs[1])) + _from_strided(scr, refs[2])

        g = g_ref[...]
        cos = _tile4(cos_ref[...])
        sin = _tile4(sin_ref[...])
        for b, (refs, pre_ref, w_ref, acc) in enumerate((((q0, q1, q2), qa_ref, qw_ref, accq),
                                                        ((k0, k1, k2), ka_ref, kw_ref, acck))):
            dh = branches(refs)
            dn = dh * cos + _rot_half(dh * sin)
            pre = pre_ref[...]
            rstd = lax.rsqrt(_segsum(pre * pre, g) * (1.0 / HEAD_DIM) + EPS)
            xh = pre * rstd
            acc[...] += jnp.sum(dn * xh, axis=0, keepdims=True)
            dxh = dn * w_ref[...]
            dpre = rstd * (dxh - xh * (_segsum(dxh * xh, g) * (1.0 / HEAD_DIM)))
            dp_ref[:, b * GROUP:(b + 1) * GROUP] = dpre.astype(BF16)
        dp_ref[:, 2 * GROUP:3 * GROUP] = branches((v0, v1, v2)).astype(BF16)
        dp_ref[:, 3 * GROUP:4 * GROUP] = dqs_ref[...].astype(BF16)
        dp_ref[:, 4 * GROUP:5 * GROUP] = dks_ref[...].astype(BF16)
        dp_ref[:, 5 * GROUP:6 * GROUP] = dvs_ref[...].astype(BF16)

        @pl.when(i == S // tm - 1)
        def _():
            for acc, o_ref in ((accq, dqw_ref), (acck, dkw_ref)):
                a = acc[...]
                pair = (a[:, 0:LANES] + a[:, LANES:2 * LANES]) + (a[:, 2 * LANES:3 * LANES] + a[:, 3 * LANES:4 * LANES])
                o_ref[...] = pair + pltpu.roll(pair, HEAD_DIM, 1)

    row = lambda i: (i, 0)
    gsp = pl.BlockSpec((tm, GROUP), row)
    tab = pl.BlockSpec((tm, LANES), row)
    planes = [_strided_spec(tm, r) for r in DILATIONS]
    return pl.pallas_call(
        body, name="qk_bwd", grid=(S // tm,),
        in_specs=planes * 3 + [gsp] * 5 + [_full((1, GROUP)), _full((1, GROUP)), tab, tab, _full((GROUP, GROUP))],
        out_specs=[pl.BlockSpec((tm, 6 * GROUP), row), _full((1, LANES)), _full((1, LANES))],
        out_shape=[jax.ShapeDtypeStruct((S, 6 * GROUP), BF16), jax.ShapeDtypeStruct((1, LANES), F32),
                   jax.ShapeDtypeStruct((1, LANES), F32)],
        scratch_shapes=[pltpu.VMEM((1, GROUP), F32), pltpu.VMEM((1, GROUP), F32), _strided_scratch(tm)],
    )(*dq_br, *dk_br, *dv_br, dqs, dks, dvs, qa, ka, qw, kw, cos_t, sin_t, seg_ones)


def _in_proj_bwd(h, dproj, w_in_g):
    S, D = h.shape
    wc = w_in_g.shape[2]
    tm = 512

    def body(h_ref, dp_ref, w_ref, dw_ref, dh_ref):
        t = pl.program_id(1)

        @pl.when(t == 0)
        def _():
            dw_ref[...] = jnp.zeros_like(dw_ref)

        dp = dp_ref[...]
        dw_ref[0] += _dot_tn(h_ref[...], dp)
        dh_ref[0] = _dot_nt(dp, w_ref[0])

    return pl.pallas_call(
        body, name="in_proj_bwd", grid=(N_CHIPS, S // tm),
        in_specs=[pl.BlockSpec((tm, D), lambda j, t: (t, 0)), pl.BlockSpec((tm, wc), lambda j, t: (t, j)),
                  pl.BlockSpec((1, D, wc), lambda j, t: (j, 0, 0))],
        out_specs=[pl.BlockSpec((1, D, wc), lambda j, t: (j, 0, 0)), pl.BlockSpec((1, tm, D), lambda j, t: (j, t, 0))],
        out_shape=[jax.ShapeDtypeStruct((N_CHIPS, D, wc), F32), jax.ShapeDtypeStruct((N_CHIPS, S, D), F32)],
    )(h, dproj, w_in_g)


def _in_norm_bwd(dhp, dx1, x, attn_w):
    S, D = x.shape
    tm = 512

    def body(dh_ref, dx1_ref, x_ref, w_ref, gx_ref, dw_ref):
        i = pl.program_id(0)

        @pl.when(i == 0)
        def _():
            dw_ref[...] = jnp.zeros_like(dw_ref)

        dh = (dh_ref[0] + dh_ref[1]) + (dh_ref[2] + dh_ref[3])
        dx, dw = _rms_bwd(dh, x_ref[...], w_ref[...])
        dw_ref[...] += jnp.sum(dw, axis=0, keepdims=True)
        gx_ref[...] = dx1_ref[...] + dx

    row = lambda i: (i, 0)
    dsp = pl.BlockSpec((tm, D), row)
    return pl.pallas_call(
        body, name="in_norm_bwd", grid=(S // tm,),
        in_specs=[pl.BlockSpec((N_CHIPS, tm, D), lambda i: (0, i, 0)), dsp, dsp, _full((1, D))],
        out_specs=[dsp, _full((1, D))],
        out_shape=[jax.ShapeDtypeStruct((S, D), F32), jax.ShapeDtypeStruct((1, D), F32)],
    )(dhp, dx1, x, attn_w)


def _constants(S):
    pos = jnp.arange(S, dtype=F32)
    inv_freq = ROPE_THETA ** (-jnp.arange(0, HEAD_DIM, 2, dtype=F32) / HEAD_DIM)
    ang = pos[:, None] * inv_freq[None, :]
    cos, sin = jnp.cos(ang), jnp.sin(ang)
    cos_t = jnp.concatenate([cos, cos] * 2, axis=1)
    sin_t = jnp.concatenate([-sin, sin] * 2, axis=1)
    idx = jnp.arange(GROUP)
    seg_ones = (idx[:, None] // HEAD_DIM == idx[None, :] // HEAD_DIM).astype(BF16)
    r = jnp.arange(BLOCK)
    ones = jnp.ones((BLOCK, BLOCK), BF16)
    tris = [jnp.concatenate([jnp.concatenate([m.astype(BF16), ones], axis=1)] * 2, axis=0) for m in
            (r[:, None] > r[None, :],
             r[:, None] <= r[None, :],
             r[:, None] < r[None, :])]
    return cos_t, sin_t, seg_ones, tris


FFN_NAMES = ("w_gate", "w_up", "w_down")


def _device_step(x, target, attn_w, qn_w, kn_w, dil_w, sbn_w, ffn_w, w_in_g, w_out_g, ffn_slots, core, chip):
    S = x.shape[0]
    cos_t, sin_t, seg_ones, (tri_later, tri_upto, tri_before) = _constants(S)
    reps = GROUP // HEAD_DIM
    qw = jnp.tile(qn_w, (1, reps))
    kw = jnp.tile(kn_w, (1, reps))

    nd = len(DILATIONS)
    h, qa, ka, *rest = _in_proj_fwd(x, attn_w, w_in_g, qw, kw, cos_t, sin_t, seg_ones)
    qh, kh, va, (qs, ks, vs) = rest[:nd], rest[nd:2 * nd], rest[2 * nd:3 * nd], rest[3 * nd:]
    branches = [_dil_fwd(qh[b], kh[b], va[b], ffn_slots[b]) for b in range(nd)]
    wg_g, wu_g, wd_g = [b[2] for b in branches]
    o_sb, ltot, walked = _sb_fwd(qs, ks, vs, tri_later)
    o_dil, *lse, x1 = _out_proj_fwd([b[0] for b in branches], [b[1] for b in branches], o_sb, x, dil_w, sbn_w, w_out_g)
    h2, g, u, dy, loss_parts = _ffn_fwd(x1, target, ffn_w, wg_g, wu_g, wd_g)

    *ffn_grads, dh2p = _ffn_bwd(h2, dy, g, u, wg_g, wu_g, wd_g)
    dx1, *mid, dw_out, dffn_w, ddil_w, dsbn_w, p0, p1, p2 = _out_proj_bwd(
        dh2p, dy, x1, ffn_w, w_out_g, o_dil, o_sb, dil_w, sbn_w, seg_ones, ffn_grads)
    do_dil, do_sb, delta = mid[:nd], mid[nd], mid[nd + 1:]
    parts = [_pair_sum(gr, fr, core, n) for gr, fr, n in zip(ffn_grads, (p0, p1, p2), FFN_NAMES)]
    dqs, dks, dvs = _sb_bwd(walked[:, 0, 0], qs, ks, vs, do_sb, ltot, tri_upto, tri_before)
    dbr = [_dil_bwd(qh[b], kh[b], va[b], do_dil[b], lse[b], delta[b], parts[b]) for b in range(nd)]
    ffn_halves = [_chip_sum(dbr[b][3], parts[b], chip, FFN_NAMES[b]) for b in range(nd)]
    dproj, dqw, dkw = _qk_bwd([b[0] for b in dbr], [b[1] for b in dbr], [b[2] for b in dbr], dqs, dks, dvs,
                              qa, ka, qw, kw, cos_t, sin_t, seg_ones)
    dw_in, dhp = _in_proj_bwd(h, dproj, w_in_g)
    grad_x, dattn_w = _in_norm_bwd(dhp, dx1, x, attn_w)
    small = dict(attn=dattn_w, q=dqw[:, :HEAD_DIM], k=dkw[:, :HEAD_DIM], dil=ddil_w, sb=dsbn_w, ffn=dffn_w)
    return loss_parts, grad_x, small, dw_in, dw_out, ffn_halves


HBM = pl.BlockSpec(memory_space=pltpu.HBM)
VMEM = pl.BlockSpec(memory_space=pltpu.VMEM)
CHIP_FLIPS = ((1, 0), (0, 1), (1, 1))


def _place():
    return lax.axis_index("x"), lax.axis_index("y"), lax.axis_index("c")


def _flip(v, d):
    return 1 - v if d else v


def _half_rows(c, n):
    return pl.ds(pl.multiple_of(c * (n // 2), 16), n // 2)


def _gather_plan(slot_in, slot_out, send, recv):
    x, y, c = _place()
    p = 2 * x + y
    chips = [(_flip(x, dx), _flip(y, dy)) for dx, dy in CHIP_FLIPS]
    mine, other = _half_rows(c, slot_in.shape[1]), _half_rows(1 - c, slot_in.shape[1])

    def copy(k, src, dst, to):
        return pltpu.make_async_remote_copy(src_ref=src, dst_ref=dst, send_sem=send.at[k], recv_sem=recv.at[k],
                                            device_id=to, device_id_type=MESH)

    def first(k):
        return copy(k, slot_in.at[p, mine], slot_out.at[p, mine], (*chips[k], c))

    def passed(k, rows):
        land = slot_out.at[2 * chips[k][0] + chips[k][1], rows]
        return copy(3 + k, land, land, (x, y, 1 - c))

    def start():
        for k in range(3):
            first(k).start()

    def forward():
        for k in range(3):
            land = slot_out.at[2 * chips[k][0] + chips[k][1], mine]
            copy(k, land, land, (*chips[k], c)).wait_recv()
            passed(k, mine).start()

    def finish():
        for k in range(3):
            passed(k, other).wait_recv()
        for k in range(3):
            first(k).wait_send()
            passed(k, mine).wait_send()

    return start, forward, finish


def _chip_send_plan(part_in, recv_out, send, recv):
    x, y, c = _place()
    p = 2 * x + y
    chips = [(_flip(x, dx), _flip(y, dy)) for dx, dy in CHIP_FLIPS]

    def copy(k):
        q = 2 * chips[k][0] + chips[k][1]
        return pltpu.make_async_remote_copy(src_ref=part_in.at[q], dst_ref=recv_out.at[p], send_sem=send.at[k],
                                            recv_sem=recv.at[k], device_id=(*chips[k], c), device_id_type=MESH)

    def start():
        for k in range(3):
            copy(k).start()

    def finish():
        for k in range(3):
            land = recv_out.at[2 * chips[k][0] + chips[k][1]]
            pltpu.make_async_remote_copy(src_ref=land, dst_ref=land, send_sem=send.at[k], recv_sem=recv.at[k],
                                         device_id=(*chips[k], c), device_id_type=MESH).wait_recv()
        for k in range(3):
            copy(k).wait_send()

    return start, finish


def _pair_send_plan(grad_in, recv_out, send, recv):
    x, y, c = _place()

    def copy():
        theirs = _half_rows(1 - c, grad_in.shape[1])
        return pltpu.make_async_remote_copy(src_ref=grad_in.at[:, theirs, :], dst_ref=recv_out, send_sem=send,
                                            recv_sem=recv, device_id=(x, y, 1 - c), device_id_type=MESH)

    return (lambda: copy().start()), (lambda: copy().wait())


def _own_slots(shard):
    here = 2 * lax.axis_index("x") + lax.axis_index("y")
    return lax.dynamic_update_slice(lax.empty((N_CHIPS,) + shard.shape, shard.dtype), shard[None], (here, 0, 0))


def _gather_weights(shards):
    n = len(shards)

    def body(*refs):
        ins, outs = refs[:n], refs[n:2 * n]
        send, recv = refs[2 * n:]
        plans = [_gather_plan(ins[a], outs[a], send.at[pl.ds(6 * a, 6)], recv.at[pl.ds(6 * a, 6)]) for a in range(n)]
        for stage in range(3):
            for plan in plans:
                plan[stage]()

    slots = [_own_slots(s) for s in shards]
    return pl.pallas_call(
        body, name="gather_weights", in_specs=[HBM] * n, out_specs=[HBM] * n,
        out_shape=[jax.ShapeDtypeStruct(s.shape, s.dtype) for s in slots],
        input_output_aliases={a: a for a in range(n)},
        scratch_shapes=[pltpu.SemaphoreType.DMA((6 * n,)), pltpu.SemaphoreType.DMA((6 * n,))],
    )(*slots)


def _pair_exchange(grads, small):
    n = len(grads)

    def body(*refs):
        gin, sm = refs[:n], refs[n]
        gout, sm_all = refs[n + 1:2 * n + 1], refs[2 * n + 1]
        send, recv = refs[2 * n + 2:]
        x, y, c = _place()
        me = 4 * x + 2 * y + c
        big = [_pair_send_plan(gin[a], gout[a], send.at[a], recv.at[a]) for a in range(n)]
        for start, _ in big:
            start()
        sm_all[pl.ds(me, 1)] = sm[...][None]
        tiny = []
        for k in range(1, N_DEV):
            px, py, pc = _flip(x, k & 4), _flip(y, k & 2), _flip(c, k & 1)
            tiny.append((pltpu.make_async_remote_copy(
                src_ref=sm, dst_ref=sm_all.at[me], send_sem=send.at[n + k - 1], recv_sem=recv.at[n + k - 1],
                device_id=(px, py, pc), device_id_type=MESH), 4 * px + 2 * py + pc))
            tiny[-1][0].start()
        for k, (cp, peer) in enumerate(tiny):
            pltpu.make_async_remote_copy(src_ref=sm, dst_ref=sm_all.at[peer], send_sem=send.at[n + k],
                                         recv_sem=recv.at[n + k], device_id=(x, y, c),
                                         device_id_type=MESH).wait_recv()
            cp.wait_send()
        for _, finish in big:
            finish()

    halves = [jax.ShapeDtypeStruct((g.shape[0], g.shape[1] // 2, g.shape[2]), g.dtype) for g in grads]
    return pl.pallas_call(
        body, name="pair_exchange", in_specs=[HBM] * n + [VMEM], out_specs=[HBM] * n + [VMEM],
        out_shape=halves + [jax.ShapeDtypeStruct((N_DEV,) + small.shape, small.dtype)],
        scratch_shapes=[pltpu.SemaphoreType.DMA((n + N_DEV - 1,)), pltpu.SemaphoreType.DMA((n + N_DEV - 1,))],
    )(*grads, small)


def _chip_exchange(parts):
    n = len(parts)

    def body(*refs):
        pin, pout = refs[:n], refs[n:2 * n]
        send, recv = refs[2 * n:]
        plans = [_chip_send_plan(pin[a], pout[a], send.at[pl.ds(3 * a, 3)], recv.at[pl.ds(3 * a, 3)]) for a in range(n)]
        for stage in range(2):
            for plan in plans:
                plan[stage]()

    return pl.pallas_call(
        body, name="chip_exchange", in_specs=[HBM] * n, out_specs=[HBM] * n,
        out_shape=[jax.ShapeDtypeStruct(s.shape, s.dtype) for s in parts],
        scratch_shapes=[pltpu.SemaphoreType.DMA((3 * n,)), pltpu.SemaphoreType.DMA((3 * n,))],
    )(*parts)


def _pair_swap(halves):
    n = len(halves)

    def body(*refs):
        hin, hout = refs[:n], refs[n:2 * n]
        send, recv = refs[2 * n:]
        x, y, c = _place()
        swaps = [pltpu.make_async_remote_copy(src_ref=hin[a], dst_ref=hout[a], send_sem=send.at[a],
                                              recv_sem=recv.at[a], device_id=(x, y, 1 - c), device_id_type=MESH)
                 for a in range(n)]
        for cp in swaps:
            cp.start()
        for cp in swaps:
            cp.wait()

    return pl.pallas_call(
        body, name="pair_swap", in_specs=[HBM] * n, out_specs=[HBM] * n,
        out_shape=[jax.ShapeDtypeStruct(s.shape, s.dtype) for s in halves],
        scratch_shapes=[pltpu.SemaphoreType.DMA((n,)), pltpu.SemaphoreType.DMA((n,))],
    )(*halves)


def _pair_sum(grad, recv, c, tag):
    _, R, C = grad.shape
    hr = R // 2

    def body(c_ref, a_ref, b_ref, o_ref):
        o_ref[...] = (a_ref[...] + b_ref[...]).astype(BF16)

    return pl.pallas_call(
        body, name="pair_sum_" + tag,
        grid_spec=pltpu.PrefetchScalarGridSpec(
            num_scalar_prefetch=1, grid=(N_CHIPS,),
            in_specs=[pl.BlockSpec((1, hr, C), lambda s, cr: (s, cr[0], 0)),
                      pl.BlockSpec((1, hr, C), lambda s, cr: (s, 0, 0))],
            out_specs=pl.BlockSpec((1, hr, C), lambda s, cr: (s, 0, 0))),
        out_shape=jax.ShapeDtypeStruct((N_CHIPS, hr, C), BF16),
    )(c, grad, recv)


def _chip_sum(received, own, chip, tag):
    _, rows, C = received.shape
    tr = rows // 2

    def body(chip_ref, own_ref, r1_ref, r2_ref, r3_ref, o_ref):
        p = [r[0].astype(F32) for r in (own_ref, r1_ref, r2_ref, r3_ref)]
        o_ref[...] = (p[0] + p[1]) + (p[2] + p[3])

    def slot(k):
        return pl.BlockSpec((1, tr, C), lambda i, cr: (jnp.bitwise_xor(cr[0], k), i, 0))

    return pl.pallas_call(
        body, name="chip_sum_" + tag,
        grid_spec=pltpu.PrefetchScalarGridSpec(
            num_scalar_prefetch=1, grid=(rows // tr,), in_specs=[slot(0), slot(1), slot(2), slot(3)],
            out_specs=pl.BlockSpec((tr, C), lambda i, cr: (i, 0))),
        out_shape=jax.ShapeDtypeStruct((rows, C), F32),
    )(chip, own, received, received, received)


def _adamw_math(w, g, m, v):
    m = ADAM_B1 * m + (1.0 - ADAM_B1) * g
    v = ADAM_B2 * v + (1.0 - ADAM_B2) * (g * g)
    m_hat = m / (1.0 - ADAM_B1 ** ADAM_STEP)
    v_hat = v / (1.0 - ADAM_B2 ** ADAM_STEP)
    delta = -ADAM_LR * (m_hat / (jnp.sqrt(v_hat) + ADAM_EPS) + ADAM_WD * w)
    return delta, m, v


def _adamw(w, g_mine, g_other, m, v, c, tag):
    R, C = w.shape
    tr = R // 4

    def body(c_ref, w_ref, gm_ref, go_ref, m_ref, v_ref, g_ref, d_ref, nm_ref, nv_ref):
        g = jnp.where(pl.program_id(0) == c_ref[0], gm_ref[...], go_ref[...])
        g_ref[...] = g
        d_ref[...], nm_ref[...], nv_ref[...] = _adamw_math(w_ref[...], g, m_ref[...], v_ref[...])

    blk = pl.BlockSpec((tr, C), lambda h, i, cr: (2 * h + i, 0))
    half = pl.BlockSpec((tr, C), lambda h, i, cr: (i, 0))
    return pl.pallas_call(
        body, name="adamw_" + tag,
        grid_spec=pltpu.PrefetchScalarGridSpec(
            num_scalar_prefetch=1, grid=(2, 2), in_specs=[blk, half, half, blk, blk], out_specs=[blk] * 4),
        out_shape=[jax.ShapeDtypeStruct((R, C), F32)] * 4,
    )(c, w, g_mine, g_other, m, v)


def _small_update(all_small, w, m, v):
    def body(a_ref, w_ref, m_ref, v_ref, g_ref, d_ref, nm_ref, nv_ref):
        g = ((a_ref[0] + a_ref[1]) + (a_ref[2] + a_ref[3])) + ((a_ref[4] + a_ref[5]) + (a_ref[6] + a_ref[7]))
        g_ref[...] = g
        d_ref[...], nm_ref[...], nv_ref[...] = _adamw_math(w_ref[...], g, m_ref[...], v_ref[...])

    return pl.pallas_call(
        body, name="small_update", out_shape=[jax.ShapeDtypeStruct(w.shape, F32)] * 4,
    )(all_small, w, m, v)


SMALL_ROWS = (("attn", 0, 0), ("ffn", 1, 0), ("dil", 2, 0), ("sb", 2, GROUP), ("q", 3, 0), ("k", 3, HEAD_DIM))


def _pack_small(vals, D):
    rows = [jnp.zeros((1, D), F32) for _ in range(8)]
    for name, r, off in SMALL_ROWS:
        rows[r] = lax.dynamic_update_slice(rows[r], vals[name].astype(F32), (0, off))
    return jnp.concatenate(rows, axis=0)


def _unpack_small(packed, vals):
    return {name: packed[r:r + 1, off:off + vals[name].shape[1]] for name, r, off in SMALL_ROWS}


def kernel(x, attn_norm_w, w_in, q_norm_w, k_norm_w, dil_out_norm_w, sb_out_norm_w, w_out, ffn_norm_w, w_gate, w_up, w_down, loss_target, m_attn_norm_w, m_w_in, m_q_norm_w, m_k_norm_w, m_dil_out_norm_w, m_sb_out_norm_w, m_w_out, m_ffn_norm_w, m_w_gate, m_w_up, m_w_down, v_attn_norm_w, v_w_in, v_q_norm_w, v_k_norm_w, v_dil_out_norm_w, v_sb_out_norm_w, v_w_out, v_ffn_norm_w, v_w_gate, v_w_up, v_w_down):
    D = x.shape[-1]
    big_names = ("w_in", "w_out", "w_gate", "w_up", "w_down")
    big_w = dict(w_in=w_in[0], w_out=w_out[0], w_gate=w_gate[0], w_up=w_up[0], w_down=w_down[0])
    big_m = dict(w_in=m_w_in[0], w_out=m_w_out[0], w_gate=m_w_gate[0], w_up=m_w_up[0], w_down=m_w_down[0])
    big_v = dict(w_in=v_w_in[0], w_out=v_w_out[0], w_gate=v_w_gate[0], w_up=v_w_up[0], w_down=v_w_down[0])
    small_w = dict(attn=attn_norm_w, q=q_norm_w, k=k_norm_w, dil=dil_out_norm_w, sb=sb_out_norm_w, ffn=ffn_norm_w)
    small_m = dict(attn=m_attn_norm_w, q=m_q_norm_w, k=m_k_norm_w, dil=m_dil_out_norm_w, sb=m_sb_out_norm_w,
                   ffn=m_ffn_norm_w)
    small_v = dict(attn=v_attn_norm_w, q=v_q_norm_w, k=v_k_norm_w, dil=v_dil_out_norm_w, sb=v_sb_out_norm_w,
                   ffn=v_ffn_norm_w)

    c = lax.axis_index("c").astype(jnp.int32).reshape(1)
    chip = (2 * lax.axis_index("x") + lax.axis_index("y")).astype(jnp.int32).reshape(1)
    w_in_g, w_out_g = _gather_weights([big_w[n].astype(BF16) for n in big_names[:2]])
    ffn_slots = [_own_slots(big_w[n].astype(BF16)) for n in FFN_NAMES]

    loss_parts, grad_x, small_g, dw_in, dw_out, ffn_halves = _device_step(
        x[0], loss_target[0], attn_norm_w, q_norm_w, k_norm_w, dil_out_norm_w, sb_out_norm_w, ffn_norm_w,
        w_in_g, w_out_g.reshape(-1, D), ffn_slots, c, chip)
    loss = lax.psum(jnp.sum(loss_parts[:, 0, 0]) * (0.5 / D), ("x", "y", "c"))

    late = [dw_in, dw_out.reshape(N_CHIPS, -1, D)]
    *from_pair, all_small = _pair_exchange(late, _pack_small(small_g, D))
    chip_parts = [_pair_sum(g, r, c, n) for g, r, n in zip(late, from_pair, big_names)]
    from_chips = _chip_exchange(chip_parts)
    halves = [_chip_sum(r, p, chip, n) for r, p, n in zip(from_chips, chip_parts, big_names)] + ffn_halves
    others = _pair_swap(halves)
    big_out = {n: _adamw(big_w[n], mine, other, big_m[n], big_v[n], c, n)
               for n, mine, other in zip(big_names, halves, others)}
    sg, sd, sm, sv = _small_update(all_small, _pack_small(small_w, D), _pack_small(small_m, D),
                                   _pack_small(small_v, D))
    small_out = [_unpack_small(t, small_w) for t in (sg, sd, sm, sv)]

    order = (("attn", None), (None, "w_in"), ("q", None), ("k", None), ("dil", None), ("sb", None),
             (None, "w_out"), ("ffn", None), (None, "w_gate"), (None, "w_up"), (None, "w_down"))
    outs = [loss, grad_x[None]]
    for kind in range(4):
        for s_name, b_name in order:
            if s_name is not None:
                outs.append(small_out[kind][s_name])
            else:
                outs.append(big_out[b_name][kind][None])
    return tuple(outs)
```

```python
import functools

import jax
import jax.numpy as jnp
from jax import lax
from jax.experimental import pallas as pl
from jax.experimental.pallas import tpu as pltpu

F32 = jnp.float32
BF16 = jnp.bfloat16
MESH = pl.DeviceIdType.MESH

HEAD_DIM = 64
GROUP = 512
BLOCK = 128
LANES = 128
N_CHIPS = 4
N_DEV = 8
EPS = 1e-6
ROPE_THETA = 10000.0
DILATIONS = (1, 4, 16)
NEG = -1e30

ADAM_LR = 0.001
ADAM_B1 = 0.9
ADAM_B2 = 0.999
ADAM_EPS = 1e-08
ADAM_WD = 0.01
ADAM_STEP = 10


def _dot(a, b):
    return jnp.dot(a, b, preferred_element_type=F32)


def _dot_nt(a, b):
    return lax.dot_general(a, b, (((1,), (1,)), ((), ())), preferred_element_type=F32)


def _dot_tn(a, b):
    return lax.dot_general(a, b, (((0,), (0,)), ((), ())), preferred_element_type=F32)


def _split(v):
    hi = lax.bitcast_convert_type(lax.bitcast_convert_type(v, jnp.uint32) & jnp.uint32(0xFFFF0000), F32)
    return hi.astype(BF16), (v - hi).astype(BF16)


def _segsum(v, g):
    hi, lo = _split(v)
    return _dot(hi, g) + _dot(lo, g)


def _rot_half(x):
    outs = []
    for c in range(x.shape[1] // LANES):
        xc = x[:, c * LANES:(c + 1) * LANES]
        lane = lax.broadcasted_iota(jnp.int32, xc.shape, 1)
        first = (lane % HEAD_DIM) < (HEAD_DIM // 2)
        outs.append(jnp.where(first, pltpu.roll(xc, LANES - 32, 1), pltpu.roll(xc, 32, 1)))
    return outs[0] if len(outs) == 1 else jnp.concatenate(outs, axis=1)


def _rms(x):
    return lax.rsqrt(jnp.mean(x * x, axis=-1, keepdims=True) + EPS)


def _rms_bwd(dy, x, w):
    rstd = _rms(x)
    xh = x * rstd
    dxh = dy * w
    dx = rstd * (dxh - xh * jnp.mean(dxh * xh, axis=-1, keepdims=True))
    return dx, dy * xh


def _sigmoid(x):
    return 1.0 / (1.0 + jnp.exp(-x))


def _sum4(ref):
    p = [ref[j].astype(F32) for j in range(N_CHIPS)]
    return (p[0] + p[1]) + (p[2] + p[3])


def _full(shape):
    n = len(shape)
    return pl.BlockSpec(shape, lambda *_: (0,) * n)


def _strided_spec(tm, r):
    return pl.BlockSpec((r, tm // r, GROUP), lambda i: (0, i, 0))


def _strided_shape(S, r, dtype):
    return jax.ShapeDtypeStruct((r, S // r, GROUP), dtype)


def _to_strided(scr, val, outs):
    chunks = range(GROUP // LANES)
    for k in chunks:
        scr[k] = val[:, _lanes(k)]
    for r, o_ref in outs:
        if r == 1:
            o_ref[0] = val.astype(o_ref.dtype)
            continue
        n = val.shape[0] // r
        for c in range(r):
            rows = pl.ds(c, n, stride=r)
            o_ref[c] = jnp.concatenate([scr.at[k][rows, :] for k in chunks], axis=1).astype(o_ref.dtype)


def _from_strided(scr, ref):
    r, n, _ = ref.shape
    if r == 1:
        return ref[0].astype(F32)
    chunks = range(GROUP // LANES)
    for c in range(r):
        plane = ref[c].astype(F32)
        for k in chunks:
            scr.at[k][pl.ds(c, n, stride=r), :] = plane[:, _lanes(k)]
    return jnp.concatenate([scr[k] for k in chunks], axis=1)


def _strided_scratch(tm):
    return pltpu.VMEM((GROUP // LANES, tm, LANES), F32)


def _tile4(t):
    return jnp.concatenate([t] * (GROUP // LANES), axis=1)


def _in_proj_fwd(x, attn_w, w_in_g, qw, kw, cos_t, sin_t, seg_ones):
    S, D = x.shape
    tm = 512
    wcols = w_in_g.shape[2]
    nd = len(DILATIONS)

    def body(x_ref, aw_ref, w_ref, qw_ref, kw_ref, cos_ref, sin_ref, g_ref, h_ref, qa_ref, ka_ref, *rest):
        q_refs, k_refs, v_refs = rest[:nd], rest[nd:2 * nd], rest[2 * nd:3 * nd]
        qs_ref, ks_ref, vs_ref, scr = rest[3 * nd:]
        xv = x_ref[...]
        h = (xv * _rms(xv) * aw_ref[...]).astype(BF16)
        h_ref[...] = h
        proj = jnp.concatenate([_dot(h, w_ref[j]) for j in range(N_CHIPS)], axis=1)
        qa = proj[:, 0 * GROUP:1 * GROUP]
        ka = proj[:, 1 * GROUP:2 * GROUP]
        qa_ref[...] = qa
        ka_ref[...] = ka
        _to_strided(scr, proj[:, 2 * GROUP:3 * GROUP], list(zip(DILATIONS, v_refs)))
        qs_ref[...] = proj[:, 3 * GROUP:4 * GROUP].astype(BF16)
        ks_ref[...] = proj[:, 4 * GROUP:5 * GROUP].astype(BF16)
        vs_ref[...] = proj[:, 5 * GROUP:6 * GROUP].astype(BF16)
        g = g_ref[...]
        cos = _tile4(cos_ref[...])
        sin = _tile4(sin_ref[...])
        for t, w_r, o_rs in ((qa, qw_ref, q_refs), (ka, kw_ref, k_refs)):
            rstd = lax.rsqrt(_segsum(t * t, g) * (1.0 / HEAD_DIM) + EPS)
            tn = t * rstd * w_r[...]
            _to_strided(scr, tn * cos + _rot_half(tn) * sin, list(zip(DILATIONS, o_rs)))

    row = lambda i: (i, 0)
    tile = lambda n, dt: jax.ShapeDtypeStruct((S, n), dt)
    planes = [_strided_spec(tm, r) for r in DILATIONS]
    return pl.pallas_call(
        body, name="in_proj_fwd", grid=(S // tm,),
        in_specs=[pl.BlockSpec((tm, D), row), _full((1, D)), _full((N_CHIPS, D, wcols)),
                  _full((1, GROUP)), _full((1, GROUP)),
                  pl.BlockSpec((tm, LANES), row), pl.BlockSpec((tm, LANES), row),
                  _full((GROUP, GROUP))],
        out_specs=[pl.BlockSpec((tm, D), row)] + [pl.BlockSpec((tm, GROUP), row)] * 2 + planes * 3
                  + [pl.BlockSpec((tm, GROUP), row)] * 3,
        out_shape=[tile(D, BF16), tile(GROUP, F32), tile(GROUP, F32)]
                  + [_strided_shape(S, r, BF16) for r in DILATIONS] * 3 + [tile(GROUP, BF16)] * 3,
        scratch_shapes=[_strided_scratch(tm)],
    )(x, attn_w, w_in_g, qw, kw, cos_t, sin_t, seg_ones)


def _dil_fwd(q, k, v, slots):
    r, L, _ = q.shape
    nb = L // BLOCK
    P = GROUP // LANES
    steps = r * nb

    def body(q_ref, kc_ref, kp_ref, vc_ref, vp_ref, slot_in, o_ref, l_ref, slot_out, send, recv):
        n = pl.program_id(1)
        step = pl.program_id(0) * nb + n
        start, forward, finish = _gather_plan(slot_in, slot_out, send, recv)
        pl.when(step == 0)(start)
        pl.when(step == (2 * steps) // 3)(forward)
        rowi = lax.broadcasted_iota(jnp.int32, (BLOCK, BLOCK), 0)
        coli = lax.broadcasted_iota(jnp.int32, (BLOCK, BLOCK), 1)
        first = coli < HEAD_DIM
        masks = (coli <= rowi, jnp.logical_and(coli >= rowi, n > 0))
        s2 = {}
        for hp in range(P):
            q2 = _scaled(q_ref[:, _lanes(hp)])
            for b, k_ref in enumerate((kc_ref, kp_ref)):
                s2[hp, b] = _dot_nt(q2, _by_head(k_ref[:, _lanes(hp)], first))
        ps, inv, lse = {}, {}, {}
        for hp in range(P):
            for h in range(2):
                s = [jnp.where(masks[b], s2[hp, b][:, h * BLOCK:(h + 1) * BLOCK], NEG) for b in range(2)]
                m = jnp.maximum(jnp.max(s[0], axis=1, keepdims=True), jnp.max(s[1], axis=1, keepdims=True))
                p = [jnp.exp(s[b] - m) for b in range(2)]
                den = jnp.sum(p[0], axis=1, keepdims=True) + jnp.sum(p[1], axis=1, keepdims=True)
                ps[hp, h] = [p[b].astype(BF16) for b in range(2)]
                inv[hp, h] = 1.0 / den
                lse[hp, h] = m + jnp.log(den)
        for hp in range(P):
            o = jnp.zeros((BLOCK, LANES), F32)
            for b, v_ref in enumerate((vc_ref, vp_ref)):
                o = o + _dot(jnp.concatenate([ps[hp, 0][b], ps[hp, 1][b]], axis=1),
                             _by_head(v_ref[:, _lanes(hp)], first))
            o_ref[:, _lanes(hp)] = o * jnp.where(first, inv[hp, 0], inv[hp, 1])
            l_ref[:, _lanes(hp)] = jnp.where(first, lse[hp, 0], lse[hp, 1])
        pl.when(step == steps - 1)(finish)

    cur = pl.BlockSpec((None, BLOCK, GROUP), lambda c, n: (c, n, 0))
    prev = pl.BlockSpec((None, BLOCK, GROUP), lambda c, n: (c, jnp.maximum(n - 1, 0), 0))
    return pl.pallas_call(
        body, name="dil_fwd_r%d" % r, grid=(r, nb),
        in_specs=[cur, cur, prev, cur, prev, HBM], out_specs=[cur, cur, HBM],
        out_shape=[jax.ShapeDtypeStruct(q.shape, F32)] * 2 + [jax.ShapeDtypeStruct(slots.shape, slots.dtype)],
        input_output_aliases={5: 2},
        scratch_shapes=[pltpu.SemaphoreType.DMA((6,)), pltpu.SemaphoreType.DMA((6,))],
    )(q, k, k, v, v, slots)


def _dil_bwd(q, k, v, do, lse, delta, part):
    r, L, _ = q.shape
    nb = L // BLOCK
    P = GROUP // LANES
    scale = HEAD_DIM ** -0.5

    def body(qc_ref, qn_ref, doc_ref, don_ref, lc_ref, ln_ref, dc_ref, dn_ref, k_ref, v_ref, part_in,
             dq_ref, dk_ref, dv_ref, part_out, carry, send, recv):
        j = pl.program_id(1)
        step = pl.program_id(0) * nb + j
        start, finish = _chip_send_plan(part_in, part_out, send, recv)
        pl.when(step == 0)(start)
        rowi = lax.broadcasted_iota(jnp.int32, (BLOCK, BLOCK), 0)
        coli = lax.broadcasted_iota(jnp.int32, (BLOCK, BLOCK), 1)
        first = coli < HEAD_DIM
        sides = ((qc_ref, doc_ref, lc_ref, dc_ref, coli <= rowi),
                 (qn_ref, don_ref, ln_ref, dn_ref, jnp.logical_and(coli >= rowi, j < nb - 1)))

        @pl.when(j == 0)
        def _():
            carry[...] = jnp.zeros_like(carry)

        kcat, q2, do2, s2, dp2 = {}, {}, {}, {}, {}
        for hp in range(P):
            kcat[hp] = _by_head(k_ref[:, _lanes(hp)], first)
            vcat = _by_head(v_ref[:, _lanes(hp)], first)
            for x, (q_r, do_r, _, _, _) in enumerate(sides):
                q2[hp, x] = _scaled(q_r[:, _lanes(hp)])
                do2[hp, x] = do_r[:, _lanes(hp)]
                s2[hp, x] = _dot_nt(q2[hp, x], kcat[hp])
                dp2[hp, x] = _dot_nt(do2[hp, x], vcat)
        pcat, dscat = {}, {}
        for hp in range(P):
            for x, (_, _, l_r, d_r, msk) in enumerate(sides):
                ps, dss = [], []
                for h in range(2):
                    col = hp * LANES + h * HEAD_DIM
                    half = slice(h * BLOCK, (h + 1) * BLOCK)
                    p = jnp.where(msk, jnp.exp(s2[hp, x][:, half] - l_r[:, col:col + 1]), 0.0)
                    ps.append(p.astype(BF16))
                    dss.append((p * (dp2[hp, x][:, half] - d_r[:, col:col + 1])).astype(BF16))
                pcat[hp, x] = jnp.concatenate(ps, axis=1)
                dscat[hp, x] = jnp.concatenate(dss, axis=1)
        for hp in range(P):
            dv2 = _dot_tn(pcat[hp, 0], do2[hp, 0]) + _dot_tn(pcat[hp, 1], do2[hp, 1])
            dk2 = _dot_tn(dscat[hp, 0], q2[hp, 0]) + _dot_tn(dscat[hp, 1], q2[hp, 1])
            dv_ref[:, _lanes(hp)] = jnp.where(first, dv2[:BLOCK], dv2[BLOCK:]).astype(BF16)
            dk_ref[:, _lanes(hp)] = jnp.where(first, dk2[:BLOCK], dk2[BLOCK:]).astype(BF16)
            dq_ref[:, _lanes(hp)] = (carry[:, _lanes(hp)] + _dot(dscat[hp, 0], kcat[hp]) * scale).astype(BF16)
            carry[:, _lanes(hp)] = _dot(dscat[hp, 1], kcat[hp]) * scale
        pl.when(step == r * nb - 1)(finish)

    cur = pl.BlockSpec((None, BLOCK, GROUP), lambda c, n: (c, n, 0))
    nxt = pl.BlockSpec((None, BLOCK, GROUP), lambda c, n: (c, jnp.minimum(n + 1, nb - 1), 0))
    return pl.pallas_call(
        body, name="dil_bwd_r%d" % r, grid=(r, nb),
        in_specs=[cur, nxt, cur, nxt, cur, nxt, cur, nxt, cur, cur, HBM], out_specs=[cur, cur, cur, HBM],
        out_shape=[jax.ShapeDtypeStruct(q.shape, BF16)] * 3 + [jax.ShapeDtypeStruct(part.shape, part.dtype)],
        scratch_shapes=[pltpu.VMEM((BLOCK, GROUP), F32), pltpu.SemaphoreType.DMA((3,)), pltpu.SemaphoreType.DMA((3,))],
    )(q, q, do, do, lse, lse, delta, delta, k, v, part)


SB_TILES = 2
SB_PAIRS_FWD = 4
SB_PAIRS_BWD = 2
SB_DEAD = -110.0


def _lanes(hp):
    return slice(hp * LANES, (hp + 1) * LANES)


def _sb_logits(z, valid):
    e = jnp.exp(-jnp.abs(z))
    lb = jnp.minimum(z, 0.0) - jnp.log(1.0 + e)
    lk = lb - z
    if valid is not None:
        lk = jnp.where(valid, lk, 0.0)
    return e, lb, lk


def _by_head(t, first):
    zero = jnp.zeros_like(t)
    return jnp.concatenate([jnp.where(first, t, zero), jnp.where(first, zero, t)], axis=0)


def _sb_valid(i, j):
    rowi = lax.broadcasted_iota(jnp.int32, (BLOCK, BLOCK), 0)
    coli = lax.broadcasted_iota(jnp.int32, (BLOCK, BLOCK), 1)
    return (coli - rowi) < (i - j) * BLOCK


def _scaled(q):
    return (q.astype(F32) * (HEAD_DIM ** -0.5)).astype(BF16)


def _sb_fwd(qs, ks, vs, tri_later):
    S = qs.shape[0]
    P = SB_PAIRS_FWD
    W = P * LANES

    def body(q_ref, k_ref, v_ref, u_ref, o_ref, lt_ref, from_ref):
        i = pl.program_id(1)
        first = lax.broadcasted_iota(jnp.int32, (BLOCK, LANES), 1) < HEAD_DIM
        q2 = [_scaled(q_ref[:, _lanes(hp)]) for hp in range(P)]

        def chunk(ci, carry, masked):
            runs, accs = list(carry[0]), list(carry[1])
            units = [(t, hp) for t in reversed(range(SB_TILES)) for hp in range(P)]
            z2s, lbs, c2s = {}, {}, {}
            for t, hp in units:
                off = pl.multiple_of((ci * SB_TILES + t) * BLOCK, BLOCK)
                z2s[t, hp] = _dot_nt(q2[hp], _by_head(k_ref[pl.ds(off, BLOCK), _lanes(hp)], first))
            for t, hp in units:
                valid = _sb_valid(i, ci * SB_TILES + t) if masked else None
                for h in range(2):
                    _, lb, lk = _sb_logits(z2s[t, hp][:, h * BLOCK:(h + 1) * BLOCK], valid)
                    lbs[t, hp, h] = lb
                    c2s[t, hp, h] = _dot(jnp.concatenate(_split(lk), axis=1), u_ref[...])
            for t, hp in units:
                off = pl.multiple_of((ci * SB_TILES + t) * BLOCK, BLOCK)
                valid = _sb_valid(i, ci * SB_TILES + t) if masked else None
                a2 = []
                for h in range(2):
                    a = jnp.exp(lbs[t, hp, h] + c2s[t, hp, h][:, :BLOCK] + runs[2 * hp + h])
                    if masked:
                        a = jnp.where(valid, a, 0.0)
                    a2.append(a.astype(BF16))
                    runs[2 * hp + h] = runs[2 * hp + h] + c2s[t, hp, h][:, BLOCK:]
                vcat = _by_head(v_ref[pl.ds(off, BLOCK), _lanes(hp)], first)
                accs[hp] = accs[hp] + _dot(jnp.concatenate(a2, axis=1), vcat)
            return tuple(runs), tuple(accs)

        def alive(runs):
            top = functools.reduce(jnp.maximum, runs)
            return (jnp.max(top) > SB_DEAD).astype(jnp.int32)

        def step(c):
            t, _, runs, accs = c
            runs, accs = chunk(nfull - 1 - t, (runs, accs), False)
            return t + 1, alive(runs), runs, accs

        zero = jnp.zeros((BLOCK, LANES), F32)
        nfull = i // SB_TILES
        runs, accs = chunk(nfull, ((zero,) * (2 * P), (zero,) * P), True)
        done, _, runs, accs = lax.while_loop(lambda c: jnp.logical_and(c[0] < nfull, c[1] > 0), step,
                                             (jnp.int32(0), alive(runs), runs, accs))
        for hp in range(P):
            o_ref[:, _lanes(hp)] = accs[hp]
            lt_ref[:, _lanes(hp)] = jnp.where(first, runs[2 * hp], runs[2 * hp + 1])
        from_ref[...] = jnp.full(from_ref.shape, nfull - done, jnp.int32)

    assert W == GROUP
    blk = pl.BlockSpec((BLOCK, W), lambda hp, i: (i, hp))
    col = pl.BlockSpec((S, W), lambda hp, i: (0, hp))
    return pl.pallas_call(
        body, name="sb_fwd", grid=(GROUP // W, S // BLOCK),
        in_specs=[blk, col, col, _full((2 * BLOCK, 2 * BLOCK))],
        out_specs=[blk, blk, pl.BlockSpec((1, 8, LANES), lambda hp, i: (i, 0, 0))],
        out_shape=[jax.ShapeDtypeStruct((S, GROUP), F32)] * 2 + [jax.ShapeDtypeStruct((S // BLOCK, 8, LANES), jnp.int32)],
    )(qs, ks, vs, tri_later)


def _sb_bwd(first_chunk, qs, ks, vs, do, ltot, tri_upto, tri_before):
    S = qs.shape[0]
    P = SB_PAIRS_BWD
    W = P * LANES

    def body(from_ref, q_ref, k_ref, v_ref, do_ref, lt_ref, w_ref, x_ref, dq_ref, dk_ref, dv_ref):
        i = pl.program_id(1)

        @pl.when(i == 0)
        def _():
            dk_ref[...] = jnp.zeros_like(dk_ref)
            dv_ref[...] = jnp.zeros_like(dv_ref)

        first = lax.broadcasted_iota(jnp.int32, (BLOCK, LANES), 1) < HEAD_DIM
        q2 = [_scaled(q_ref[:, _lanes(hp)]) for hp in range(P)]
        do2 = [do_ref[:, _lanes(hp)] for hp in range(P)]
        totals = [jnp.broadcast_to(lt_ref[:, n * HEAD_DIM:n * HEAD_DIM + 1], (BLOCK, LANES)) for n in range(2 * P)]

        def chunk(ci, carry, masked):
            keeps, grads, dqs = list(carry[0]), list(carry[1]), list(carry[2])
            units = [(t, hp) for t in range(SB_TILES) for hp in range(P)]
            offs = [pl.multiple_of((ci * SB_TILES + t) * BLOCK, BLOCK) for t in range(SB_TILES)]
            valids = [_sb_valid(i, ci * SB_TILES + t) if masked else None for t in range(SB_TILES)]
            kcat, z2, da2, es, lbs, c2s, as_, des, p2s = {}, {}, {}, {}, {}, {}, {}, {}, {}
            for t, hp in units:
                kcat[t, hp] = _by_head(k_ref[pl.ds(offs[t], BLOCK), _lanes(hp)], first)
                z2[t, hp] = _dot_nt(q2[hp], kcat[t, hp])
                da2[t, hp] = _dot_nt(do2[hp], _by_head(v_ref[pl.ds(offs[t], BLOCK), _lanes(hp)], first))
            for t, hp in units:
                for h in range(2):
                    es[t, hp, h], lbs[t, hp, h], lk = _sb_logits(z2[t, hp][:, h * BLOCK:(h + 1) * BLOCK], valids[t])
                    c2s[t, hp, h] = _dot(jnp.concatenate(_split(lk), axis=1), w_ref[...])
            for t, hp in units:
                for h in range(2):
                    n = 2 * hp + h
                    a = jnp.exp(lbs[t, hp, h] + (totals[n] - (keeps[n] + c2s[t, hp, h][:, :BLOCK])))
                    if masked:
                        a = jnp.where(valids[t], a, 0.0)
                    keeps[n] = keeps[n] + c2s[t, hp, h][:, BLOCK:]
                    de = a * da2[t, hp][:, h * BLOCK:(h + 1) * BLOCK]
                    as_[t, hp, h], des[t, hp, h] = a.astype(BF16), de
                    p2s[t, hp, h] = _dot(jnp.concatenate(_split(de), axis=1), x_ref[...])
            for t, hp in units:
                dz2 = []
                for h in range(2):
                    n = 2 * hp + h
                    e = es[t, hp, h]
                    sig = jnp.where(z2[t, hp][:, h * BLOCK:(h + 1) * BLOCK] >= 0.0, 1.0, e) / (1.0 + e)
                    dz = des[t, hp, h] * (1.0 - sig) - (grads[n] + p2s[t, hp, h][:, :BLOCK]) * sig
                    if masked:
                        dz = jnp.where(valids[t], dz, 0.0)
                    grads[n] = grads[n] + p2s[t, hp, h][:, BLOCK:]
                    dz2.append(dz.astype(BF16))
                dzcat = jnp.concatenate(dz2, axis=1)
                dk2 = _dot_tn(dzcat, q2[hp])
                dv2 = _dot_tn(jnp.concatenate([as_[t, hp, 0], as_[t, hp, 1]], axis=1), do2[hp])
                dk_ref[pl.ds(offs[t], BLOCK), _lanes(hp)] += jnp.where(first, dk2[:BLOCK], dk2[BLOCK:])
                dv_ref[pl.ds(offs[t], BLOCK), _lanes(hp)] += jnp.where(first, dv2[:BLOCK], dv2[BLOCK:])
                dqs[hp] = dqs[hp] + _dot(dzcat, kcat[t, hp])
            return tuple(keeps), tuple(grads), tuple(dqs)

        zero = jnp.zeros((BLOCK, LANES), F32)
        nfull = i // SB_TILES
        carry = lax.fori_loop(from_ref[i], nfull, lambda t, c: chunk(t, c, False),
                              ((zero,) * (2 * P), (zero,) * (2 * P), (zero,) * P))
        carry = chunk(nfull, carry, True)
        for hp in range(P):
            dq_ref[:, _lanes(hp)] = carry[2][hp] * (HEAD_DIM ** -0.5)

    blk = pl.BlockSpec((BLOCK, W), lambda hp, i, fr: (i, hp))
    col = pl.BlockSpec((S, W), lambda hp, i, fr: (0, hp))
    tri = pl.BlockSpec((2 * BLOCK, 2 * BLOCK), lambda hp, i, fr: (0, 0))
    return pl.pallas_call(
        body, name="sb_bwd",
        grid_spec=pltpu.PrefetchScalarGridSpec(
            num_scalar_prefetch=1, grid=(GROUP // W, S // BLOCK),
            in_specs=[blk, col, col, blk, blk, tri, tri], out_specs=[blk, col, col]),
        out_shape=[jax.ShapeDtypeStruct((S, GROUP), F32)] * 3,
    )(first_chunk, qs, ks, vs, do, ltot, tri_upto, tri_before)


def _out_proj_fwd(o_br, l_br, o_sb, x, w_dil, w_sbn, w_out_g):
    S, D = x.shape
    tm = 512

    def body(o0, o1, o2, l0, l1, l2, os_ref, x_ref, wd_ref, ws_ref, w_ref, od_ref, s0, s1, s2, x1_ref, scr):
        ls = [_from_strided(scr, l) for l in (l0, l1, l2)]
        os_ = [_from_strided(scr, o) for o in (o0, o1, o2)]
        m = jnp.maximum(jnp.maximum(ls[0], ls[1]), ls[2])
        es = [jnp.exp(l - m) for l in ls]
        den = es[0] + es[1] + es[2]
        od = (es[0] * os_[0] + es[1] * os_[1] + es[2] * os_[2]) / den
        od_ref[...] = od
        _to_strided(scr, m + jnp.log(den), list(zip(DILATIONS, (s0, s1, s2))))
        osb = os_ref[...]
        mixed = jnp.concatenate([(od * _rms(od) * wd_ref[...]).astype(BF16),
                                 (osb * _rms(osb) * ws_ref[...]).astype(BF16)], axis=1)
        x1_ref[...] = x_ref[...] + _dot(mixed, w_ref[...])

    row = lambda i: (i, 0)
    g = pl.BlockSpec((tm, GROUP), row)
    d = pl.BlockSpec((tm, D), row)
    planes = [_strided_spec(tm, r) for r in DILATIONS]
    return pl.pallas_call(
        body, name="out_proj_fwd", grid=(S // tm,),
        in_specs=planes * 2 + [g, d, _full((1, GROUP)), _full((1, GROUP)), _full((2 * GROUP, D))],
        out_specs=[g] + planes + [d],
        out_shape=[jax.ShapeDtypeStruct((S, GROUP), F32)] + [_strided_shape(S, r, F32) for r in DILATIONS]
                  + [jax.ShapeDtypeStruct((S, D), F32)],
        scratch_shapes=[_strided_scratch(tm)],
    )(*o_br, *l_br, o_sb, x, w_dil, w_sbn, w_out_g)


def _ffn_fwd(x1, target, ffn_w, wg_g, wu_g, wd_g):
    S, D = x1.shape
    F = wg_g.shape[1]
    tm = 512
    nt = S // tm

    def body(x_ref, t_ref, nw_ref, wg_ref, wu_ref, wd_ref, h_ref, g_ref, u_ref, dy_ref, loss_ref, h_s, acc):
        j = pl.program_id(1)

        @pl.when(j == 0)
        def _():
            xv = x_ref[...]
            h = (xv * _rms(xv) * nw_ref[...]).astype(BF16)
            h_s[...] = h
            h_ref[...] = h
            acc[...] = xv

        h = h_s[...]
        g = _dot_nt(h, wg_ref[0])
        u = _dot_nt(h, wu_ref[0])
        g_ref[0] = g.astype(BF16)
        u_ref[0] = u.astype(BF16)
        a = (g * _sigmoid(g) * u).astype(BF16)
        acc[...] += _dot(a, wd_ref[0])

        @pl.when(j == N_CHIPS - 1)
        def _():
            err = acc[...] - t_ref[...]
            dy_ref[...] = err * (1.0 / D)
            loss_ref[...] = jnp.full(loss_ref.shape, jnp.sum(err * err), F32)

    row = lambda t, j: (t, 0)
    shard = lambda t, j: (j, 0, 0)
    act = lambda t, j: (j, t, 0)
    return pl.pallas_call(
        body, name="ffn_fwd", grid=(nt, N_CHIPS),
        in_specs=[pl.BlockSpec((tm, D), row), pl.BlockSpec((tm, D), row), pl.BlockSpec((1, D), lambda t, j: (0, 0))]
                 + [pl.BlockSpec((1, F, D), shard)] * 3,
        out_specs=[pl.BlockSpec((tm, D), row), pl.BlockSpec((1, tm, F), act), pl.BlockSpec((1, tm, F), act),
                   pl.BlockSpec((tm, D), row), pl.BlockSpec((1, 8, LANES), lambda t, j: (t, 0, 0))],
        out_shape=[jax.ShapeDtypeStruct((S, D), BF16), jax.ShapeDtypeStruct((N_CHIPS, S, F), BF16),
                   jax.ShapeDtypeStruct((N_CHIPS, S, F), BF16), jax.ShapeDtypeStruct((S, D), F32),
                   jax.ShapeDtypeStruct((nt, 8, LANES), F32)],
        scratch_shapes=[pltpu.VMEM((tm, D), BF16), pltpu.VMEM((tm, D), F32)],
    )(x1, target, ffn_w, wg_g, wu_g, wd_g)


def _ffn_bwd(h2, dy, g, u, wg_g, wu_g, wd_g):
    S, D = dy.shape
    F = wg_g.shape[1]
    tm = 512

    def body(h_ref, dy_ref, g_ref, u_ref, wg_ref, wu_ref, wd_ref, dwg_ref, dwu_ref, dwd_ref, dh_ref):
        t = pl.program_id(1)

        @pl.when(t == 0)
        def _():
            dwg_ref[...] = jnp.zeros_like(dwg_ref)
            dwu_ref[...] = jnp.zeros_like(dwu_ref)
            dwd_ref[...] = jnp.zeros_like(dwd_ref)

        h = h_ref[...]
        dyb = dy_ref[...].astype(BF16)
        gv = g_ref[0].astype(F32)
        uv = u_ref[0].astype(F32)
        da = _dot_nt(dyb, wd_ref[0])
        sg = _sigmoid(gv)
        silu = gv * sg
        du = (da * silu).astype(BF16)
        dg = (da * uv * (sg * (1.0 + gv * (1.0 - sg)))).astype(BF16)
        dwd_ref[0] += _dot_tn((silu * uv).astype(BF16), dyb)
        dwg_ref[0] += _dot_tn(dg, h)
        dwu_ref[0] += _dot_tn(du, h)
        dh_ref[0] = (_dot(dg, wg_ref[0]) + _dot(du, wu_ref[0])).astype(BF16)

    row = lambda j, t: (t, 0)
    shard = lambda j, t: (j, 0, 0)
    act = lambda j, t: (j, t, 0)
    return pl.pallas_call(
        body, name="ffn_bwd", grid=(N_CHIPS, S // tm),
        in_specs=[pl.BlockSpec((tm, D), row), pl.BlockSpec((tm, D), row),
                  pl.BlockSpec((1, tm, F), act), pl.BlockSpec((1, tm, F), act)] + [pl.BlockSpec((1, F, D), shard)] * 3,
        out_specs=[pl.BlockSpec((1, F, D), shard)] * 3 + [pl.BlockSpec((1, tm, D), act)],
        out_shape=[jax.ShapeDtypeStruct((N_CHIPS, F, D), F32)] * 3 + [jax.ShapeDtypeStruct((N_CHIPS, S, D), BF16)],
    )(h2, dy, g, u, wg_g, wu_g, wd_g)


def _out_proj_bwd(dh2p, dy, x1, ffn_w, w_out_g, o_dil, o_sb, w_dil, w_sbn, seg_ones, ffn_grads):
    S, D = dy.shape
    tm = 256
    ng = len(ffn_grads)

    def body(dh_ref, dy_ref, x1_ref, nw_ref, w_ref, od_ref, os_ref, wd_ref, ws_ref, g_ref, *rest):
        gin, rest = rest[:ng], rest[ng:]
        dx1_ref, dod0, dod1, dod2, dos_ref, dl0, dl1, dl2, dw_ref, dnw_ref, dwd_ref, dws_ref = rest[:12]
        gout, (scr, send, recv) = rest[12:12 + ng], rest[12 + ng:]
        i = pl.program_id(0)
        plans = [_pair_send_plan(gin[a], gout[a], send.at[a], recv.at[a]) for a in range(ng)]

        @pl.when(i == 0)
        def _():
            for start, _ in plans:
                start()

        @pl.when(i == 0)
        def _():
            for r_ in (dw_ref, dnw_ref, dwd_ref, dws_ref):
                r_[...] = jnp.zeros_like(r_)

        dh2 = _sum4(dh_ref)
        dxn, dwn = _rms_bwd(dh2, x1_ref[...], nw_ref[...])
        dnw_ref[...] += jnp.sum(dwn, axis=0, keepdims=True)
        dx1 = dy_ref[...] + dxn
        dx1_ref[...] = dx1
        dx1b = dx1.astype(BF16)
        dmix = _dot_nt(dx1b, w_ref[...])
        od = od_ref[...]
        osb = os_ref[...]
        mixed = jnp.concatenate([(od * _rms(od) * wd_ref[...]).astype(BF16),
                                 (osb * _rms(osb) * ws_ref[...]).astype(BF16)], axis=1)
        dw_ref[...] += _dot_tn(mixed, dx1b)
        do, dwo = _rms_bwd(dmix[:, :GROUP], od, wd_ref[...])
        dwd_ref[...] += jnp.sum(dwo, axis=0, keepdims=True)
        _to_strided(scr, do, list(zip(DILATIONS, (dod0, dod1, dod2))))
        _to_strided(scr, _segsum(do * od, g_ref[...]), list(zip(DILATIONS, (dl0, dl1, dl2))))
        do, dwo = _rms_bwd(dmix[:, GROUP:], osb, ws_ref[...])
        dws_ref[...] += jnp.sum(dwo, axis=0, keepdims=True)
        dos_ref[...] = do.astype(BF16)

        @pl.when(i == S // tm - 1)
        def _():
            for _, finish in plans:
                finish()

    row = lambda i: (i, 0)
    gsp = pl.BlockSpec((tm, GROUP), row)
    dsp = pl.BlockSpec((tm, D), row)
    planes = [_strided_spec(tm, r) for r in DILATIONS]
    halves = [jax.ShapeDtypeStruct((g.shape[0], g.shape[1] // 2, g.shape[2]), g.dtype) for g in ffn_grads]
    return pl.pallas_call(
        body, name="out_proj_bwd", grid=(S // tm,),
        in_specs=[pl.BlockSpec((N_CHIPS, tm, D), lambda i: (0, i, 0)), dsp, dsp, _full((1, D)), _full((2 * GROUP, D)),
                  gsp, gsp, _full((1, GROUP)), _full((1, GROUP)), _full((GROUP, GROUP))] + [HBM] * ng,
        out_specs=[dsp] + planes + [gsp] + planes
                  + [_full((2 * GROUP, D)), _full((1, D)), _full((1, GROUP)), _full((1, GROUP))] + [HBM] * ng,
        out_shape=[jax.ShapeDtypeStruct((S, D), F32)] + [_strided_shape(S, r, BF16) for r in DILATIONS]
                  + [jax.ShapeDtypeStruct((S, GROUP), BF16)] + [_strided_shape(S, r, F32) for r in DILATIONS]
                  + [jax.ShapeDtypeStruct((2 * GROUP, D), F32),
                     jax.ShapeDtypeStruct((1, D), F32), jax.ShapeDtypeStruct((1, GROUP), F32),
                     jax.ShapeDtypeStruct((1, GROUP), F32)] + halves,
        scratch_shapes=[_strided_scratch(tm), pltpu.SemaphoreType.DMA((ng,)), pltpu.SemaphoreType.DMA((ng,))],
    )(dh2p, dy, x1, ffn_w, w_out_g, o_dil, o_sb, w_dil, w_sbn, seg_ones, *ffn_grads)


def _qk_bwd(dq_br, dk_br, dv_br, dqs, dks, dvs, qa, ka, qw, kw, cos_t, sin_t, seg_ones):
    S = qa.shape[0]
    tm = 256

    def body(q0, q1, q2, k0, k1, k2, v0, v1, v2, dqs_ref, dks_ref, dvs_ref, qa_ref, ka_ref, qw_ref, kw_ref,
             cos_ref, sin_ref, g_ref, dp_ref, dqw_ref, dkw_ref, accq, acck, scr):
        i = pl.program_id(0)

        @pl.when(i == 0)
        def _():
            accq[...] = jnp.zeros_like(accq)
            acck[...] = jnp.zeros_like(acck)

        def branches(refs):
            return (_from_strided(scr, refs[0]) + _from_strided(scr, refs[1])) + _from_strided(scr, refs[2])

        g = g_ref[...]
        cos = _tile4(cos_ref[...])
        sin = _tile4(sin_ref[...])
        for b, (refs, pre_ref, w_ref, acc) in enumerate((((q0, q1, q2), qa_ref, qw_ref, accq),
                                                        ((k0, k1, k2), ka_ref, kw_ref, acck))):
            dh = branches(refs)
            dn = dh * cos + _rot_half(dh * sin)
            pre = pre_ref[...]
            rstd = lax.rsqrt(_segsum(pre * pre, g) * (1.0 / HEAD_DIM) + EPS)
            xh = pre * rstd
            acc[...] += jnp.sum(dn * xh, axis=0, keepdims=True)
            dxh = dn * w_ref[...]
            dpre = rstd * (dxh - xh * (_segsum(dxh * xh, g) * (1.0 / HEAD_DIM)))
            dp_ref[:, b * GROUP:(b + 1) * GROUP] = dpre.astype(BF16)
        dp_ref[:, 2 * GROUP:3 * GROUP] = branches((v0, v1, v2)).astype(BF16)
        dp_ref[:, 3 * GROUP:4 * GROUP] = dqs_ref[...].astype(BF16)
        dp_ref[:, 4 * GROUP:5 * GROUP] = dks_ref[...].astype(BF16)
        dp_ref[:, 5 * GROUP:6 * GROUP] = dvs_ref[...].astype(BF16)

        @pl.when(i == S // tm - 1)
        def _():
            for acc, o_ref in ((accq, dqw_ref), (acck, dkw_ref)):
                a = acc[...]
                pair = (a[:, 0:LANES] + a[:, LANES:2 * LANES]) + (a[:, 2 * LANES:3 * LANES] + a[:, 3 * LANES:4 * LANES])
                o_ref[...] = pair + pltpu.roll(pair, HEAD_DIM, 1)

    row = lambda i: (i, 0)
    gsp = pl.BlockSpec((tm, GROUP), row)
    tab = pl.BlockSpec((tm, LANES), row)
    planes = [_strided_spec(tm, r) for r in DILATIONS]
    return pl.pallas_call(
        body, name="qk_bwd", grid=(S // tm,),
        in_specs=planes * 3 + [gsp] * 5 + [_full((1, GROUP)), _full((1, GROUP)), tab, tab, _full((GROUP, GROUP))],
        out_specs=[pl.BlockSpec((tm, 6 * GROUP), row), _full((1, LANES)), _full((1, LANES))],
        out_shape=[jax.ShapeDtypeStruct((S, 6 * GROUP), BF16), jax.ShapeDtypeStruct((1, LANES), F32),
                   jax.ShapeDtypeStruct((1, LANES), F32)],
        scratch_shapes=[pltpu.VMEM((1, GROUP), F32), pltpu.VMEM((1, GROUP), F32), _strided_scratch(tm)],
    )(*dq_br, *dk_br, *dv_br, dqs, dks, dvs, qa, ka, qw, kw, cos_t, sin_t, seg_ones)


def _in_proj_bwd(h, dproj, w_in_g):
    S, D = h.shape
    wc = w_in_g.shape[2]
    tm = 512

    def body(h_ref, dp_ref, w_ref, dw_ref, dh_ref):
        t = pl.program_id(1)

        @pl.when(t == 0)
        def _():
            dw_ref[...] = jnp.zeros_like(dw_ref)

        dp = dp_ref[...]
        dw_ref[0] += _dot_tn(h_ref[...], dp)
        dh_ref[0] = _dot_nt(dp, w_ref[0]).astype(BF16)

    return pl.pallas_call(
        body, name="in_proj_bwd", grid=(N_CHIPS, S // tm),
        in_specs=[pl.BlockSpec((tm, D), lambda j, t: (t, 0)), pl.BlockSpec((tm, wc), lambda j, t: (t, j)),
                  pl.BlockSpec((1, D, wc), lambda j, t: (j, 0, 0))],
        out_specs=[pl.BlockSpec((1, D, wc), lambda j, t: (j, 0, 0)), pl.BlockSpec((1, tm, D), lambda j, t: (j, t, 0))],
        out_shape=[jax.ShapeDtypeStruct((N_CHIPS, D, wc), F32), jax.ShapeDtypeStruct((N_CHIPS, S, D), BF16)],
    )(h, dproj, w_in_g)


def _in_norm_bwd(dhp, dx1, x, attn_w):
    S, D = x.shape
    tm = 512

    def body(dh_ref, dx1_ref, x_ref, w_ref, gx_ref, dw_ref):
        i = pl.program_id(0)

        @pl.when(i == 0)
        def _():
            dw_ref[...] = jnp.zeros_like(dw_ref)

        dh = _sum4(dh_ref)
        dx, dw = _rms_bwd(dh, x_ref[...], w_ref[...])
        dw_ref[...] += jnp.sum(dw, axis=0, keepdims=True)
        gx_ref[...] = dx1_ref[...] + dx

    row = lambda i: (i, 0)
    dsp = pl.BlockSpec((tm, D), row)
    return pl.pallas_call(
        body, name="in_norm_bwd", grid=(S // tm,),
        in_specs=[pl.BlockSpec((N_CHIPS, tm, D), lambda i: (0, i, 0)), dsp, dsp, _full((1, D))],
        out_specs=[dsp, _full((1, D))],
        out_shape=[jax.ShapeDtypeStruct((S, D), F32), jax.ShapeDtypeStruct((1, D), F32)],
    )(dhp, dx1, x, attn_w)


def _constants(S):
    pos = jnp.arange(S, dtype=F32)
    inv_freq = ROPE_THETA ** (-jnp.arange(0, HEAD_DIM, 2, dtype=F32) / HEAD_DIM)
    ang = pos[:, None] * inv_freq[None, :]
    cos, sin = jnp.cos(ang), jnp.sin(ang)
    cos_t = jnp.concatenate([cos, cos] * 2, axis=1)
    sin_t = jnp.concatenate([-sin, sin] * 2, axis=1)
    idx = jnp.arange(GROUP)
    seg_ones = (idx[:, None] // HEAD_DIM == idx[None, :] // HEAD_DIM).astype(BF16)
    r = jnp.arange(BLOCK)
    ones = jnp.ones((BLOCK, BLOCK), BF16)
    tris = [jnp.concatenate([jnp.concatenate([m.astype(BF16), ones], axis=1)] * 2, axis=0) for m in
            (r[:, None] > r[None, :],
             r[:, None] <= r[None, :],
             r[:, None] < r[None, :])]
    return cos_t, sin_t, seg_ones, tris


FFN_NAMES = ("w_gate", "w_up", "w_down")


def _device_step(x, target, attn_w, qn_w, kn_w, dil_w, sbn_w, ffn_w, w_in_g, w_out_g, ffn_slots, core, chip):
    S = x.shape[0]
    cos_t, sin_t, seg_ones, (tri_later, tri_upto, tri_before) = _constants(S)
    reps = GROUP // HEAD_DIM
    qw = jnp.tile(qn_w, (1, reps))
    kw = jnp.tile(kn_w, (1, reps))

    nd = len(DILATIONS)
    h, qa, ka, *rest = _in_proj_fwd(x, attn_w, w_in_g, qw, kw, cos_t, sin_t, seg_ones)
    qh, kh, va, (qs, ks, vs) = rest[:nd], rest[nd:2 * nd], rest[2 * nd:3 * nd], rest[3 * nd:]
    branches = [_dil_fwd(qh[b], kh[b], va[b], ffn_slots[b]) for b in range(nd)]
    wg_g, wu_g, wd_g = [b[2] for b in branches]
    o_sb, ltot, walked = _sb_fwd(qs, ks, vs, tri_later)
    o_dil, *lse, x1 = _out_proj_fwd([b[0] for b in branches], [b[1] for b in branches], o_sb, x, dil_w, sbn_w, w_out_g)
    h2, g, u, dy, loss_parts = _ffn_fwd(x1, target, ffn_w, wg_g, wu_g, wd_g)

    *ffn_grads, dh2p = _ffn_bwd(h2, dy, g, u, wg_g, wu_g, wd_g)
    dx1, *mid, dw_out, dffn_w, ddil_w, dsbn_w, p0, p1, p2 = _out_proj_bwd(
        dh2p, dy, x1, ffn_w, w_out_g, o_dil, o_sb, dil_w, sbn_w, seg_ones, ffn_grads)
    do_dil, do_sb, delta = mid[:nd], mid[nd], mid[nd + 1:]
    parts = [_pair_sum(gr, fr, core, n) for gr, fr, n in zip(ffn_grads, (p0, p1, p2), FFN_NAMES)]
    dqs, dks, dvs = _sb_bwd(walked[:, 0, 0], qs, ks, vs, do_sb, ltot, tri_upto, tri_before)
    dbr = [_dil_bwd(qh[b], kh[b], va[b], do_dil[b], lse[b], delta[b], parts[b]) for b in range(nd)]
    ffn_halves = [_chip_sum(dbr[b][3], parts[b], chip, FFN_NAMES[b]) for b in range(nd)]
    dproj, dqw, dkw = _qk_bwd([b[0] for b in dbr], [b[1] for b in dbr], [b[2] for b in dbr], dqs, dks, dvs,
                              qa, ka, qw, kw, cos_t, sin_t, seg_ones)
    dw_in, dhp = _in_proj_bwd(h, dproj, w_in_g)
    grad_x, dattn_w = _in_norm_bwd(dhp, dx1, x, attn_w)
    small = dict(attn=dattn_w, q=dqw[:, :HEAD_DIM], k=dkw[:, :HEAD_DIM], dil=ddil_w, sb=dsbn_w, ffn=dffn_w)
    return loss_parts, grad_x, small, dw_in, dw_out, ffn_halves


HBM = pl.BlockSpec(memory_space=pltpu.HBM)
VMEM = pl.BlockSpec(memory_space=pltpu.VMEM)
CHIP_FLIPS = ((1, 0), (0, 1), (1, 1))


def _place():
    return lax.axis_index("x"), lax.axis_index("y"), lax.axis_index("c")


def _flip(v, d):
    return 1 - v if d else v


def _half_rows(c, n):
    return pl.ds(pl.multiple_of(c * (n // 2), 16), n // 2)


def _gather_plan(slot_in, slot_out, send, recv):
    x, y, c = _place()
    p = 2 * x + y
    chips = [(_flip(x, dx), _flip(y, dy)) for dx, dy in CHIP_FLIPS]
    mine, other = _half_rows(c, slot_in.shape[1]), _half_rows(1 - c, slot_in.shape[1])

    def copy(k, src, dst, to):
        return pltpu.make_async_remote_copy(src_ref=src, dst_ref=dst, send_sem=send.at[k], recv_sem=recv.at[k],
                                            device_id=to, device_id_type=MESH)

    def first(k):
        return copy(k, slot_in.at[p, mine], slot_out.at[p, mine], (*chips[k], c))

    def passed(k, rows):
        land = slot_out.at[2 * chips[k][0] + chips[k][1], rows]
        return copy(3 + k, land, land, (x, y, 1 - c))

    def start():
        for k in range(3):
            first(k).start()

    def forward():
        for k in range(3):
            land = slot_out.at[2 * chips[k][0] + chips[k][1], mine]
            copy(k, land, land, (*chips[k], c)).wait_recv()
            passed(k, mine).start()

    def finish():
        for k in range(3):
            passed(k, other).wait_recv()
        for k in range(3):
            first(k).wait_send()
            passed(k, mine).wait_send()

    return start, forward, finish


def _chip_send_plan(part_in, recv_out, send, recv):
    x, y, c = _place()
    p = 2 * x + y
    chips = [(_flip(x, dx), _flip(y, dy)) for dx, dy in CHIP_FLIPS]

    def copy(k):
        q = 2 * chips[k][0] + chips[k][1]
        return pltpu.make_async_remote_copy(src_ref=part_in.at[q], dst_ref=recv_out.at[p], send_sem=send.at[k],
                                            recv_sem=recv.at[k], device_id=(*chips[k], c), device_id_type=MESH)

    def start():
        for k in range(3):
            copy(k).start()

    def finish():
        for k in range(3):
            land = recv_out.at[2 * chips[k][0] + chips[k][1]]
            pltpu.make_async_remote_copy(src_ref=land, dst_ref=land, send_sem=send.at[k], recv_sem=recv.at[k],
                                         device_id=(*chips[k], c), device_id_type=MESH).wait_recv()
        for k in range(3):
            copy(k).wait_send()

    return start, finish


def _pair_send_plan(grad_in, recv_out, send, recv):
    x, y, c = _place()

    def copy():
        theirs = _half_rows(1 - c, grad_in.shape[1])
        return pltpu.make_async_remote_copy(src_ref=grad_in.at[:, theirs, :], dst_ref=recv_out, send_sem=send,
                                            recv_sem=recv, device_id=(x, y, 1 - c), device_id_type=MESH)

    return (lambda: copy().start()), (lambda: copy().wait())


def _own_slots(shard):
    here = 2 * lax.axis_index("x") + lax.axis_index("y")
    return lax.dynamic_update_slice(lax.empty((N_CHIPS,) + shard.shape, shard.dtype), shard[None], (here, 0, 0))


def _gather_weights(shards):
    n = len(shards)

    def body(*refs):
        ins, outs = refs[:n], refs[n:2 * n]
        send, recv = refs[2 * n:]
        plans = [_gather_plan(ins[a], outs[a], send.at[pl.ds(6 * a, 6)], recv.at[pl.ds(6 * a, 6)]) for a in range(n)]
        for stage in range(3):
            for plan in plans:
                plan[stage]()

    slots = [_own_slots(s) for s in shards]
    return pl.pallas_call(
        body, name="gather_weights", in_specs=[HBM] * n, out_specs=[HBM] * n,
        out_shape=[jax.ShapeDtypeStruct(s.shape, s.dtype) for s in slots],
        input_output_aliases={a: a for a in range(n)},
        scratch_shapes=[pltpu.SemaphoreType.DMA((6 * n,)), pltpu.SemaphoreType.DMA((6 * n,))],
    )(*slots)


def _pair_exchange(grads, small):
    n = len(grads)

    def body(*refs):
        gin, sm = refs[:n], refs[n]
        gout, sm_all = refs[n + 1:2 * n + 1], refs[2 * n + 1]
        send, recv = refs[2 * n + 2:]
        x, y, c = _place()
        me = 4 * x + 2 * y + c
        big = [_pair_send_plan(gin[a], gout[a], send.at[a], recv.at[a]) for a in range(n)]
        for start, _ in big:
            start()
        sm_all[pl.ds(me, 1)] = sm[...][None]
        tiny = []
        for k in range(1, N_DEV):
            px, py, pc = _flip(x, k & 4), _flip(y, k & 2), _flip(c, k & 1)
            tiny.append((pltpu.make_async_remote_copy(
                src_ref=sm, dst_ref=sm_all.at[me], send_sem=send.at[n + k - 1], recv_sem=recv.at[n + k - 1],
                device_id=(px, py, pc), device_id_type=MESH), 4 * px + 2 * py + pc))
            tiny[-1][0].start()
        for k, (cp, peer) in enumerate(tiny):
            pltpu.make_async_remote_copy(src_ref=sm, dst_ref=sm_all.at[peer], send_sem=send.at[n + k],
                                         recv_sem=recv.at[n + k], device_id=(x, y, c),
                                         device_id_type=MESH).wait_recv()
            cp.wait_send()
        for _, finish in big:
            finish()

    halves = [jax.ShapeDtypeStruct((g.shape[0], g.shape[1] // 2, g.shape[2]), g.dtype) for g in grads]
    return pl.pallas_call(
        body, name="pair_exchange", in_specs=[HBM] * n + [VMEM], out_specs=[HBM] * n + [VMEM],
        out_shape=halves + [jax.ShapeDtypeStruct((N_DEV,) + small.shape, small.dtype)],
        scratch_shapes=[pltpu.SemaphoreType.DMA((n + N_DEV - 1,)), pltpu.SemaphoreType.DMA((n + N_DEV - 1,))],
    )(*grads, small)


def _chip_exchange(parts):
    n = len(parts)

    def body(*refs):
        pin, pout = refs[:n], refs[n:2 * n]
        send, recv = refs[2 * n:]
        plans = [_chip_send_plan(pin[a], pout[a], send.at[pl.ds(3 * a, 3)], recv.at[pl.ds(3 * a, 3)]) for a in range(n)]
        for stage in range(2):
            for plan in plans:
                plan[stage]()

    return pl.pallas_call(
        body, name="chip_exchange", in_specs=[HBM] * n, out_specs=[HBM] * n,
        out_shape=[jax.ShapeDtypeStruct(s.shape, s.dtype) for s in parts],
        scratch_shapes=[pltpu.SemaphoreType.DMA((3 * n,)), pltpu.SemaphoreType.DMA((3 * n,))],
    )(*parts)


def _pair_swap(halves):
    n = len(halves)

    def body(*refs):
        hin, hout = refs[:n], refs[n:2 * n]
        send, recv = refs[2 * n:]
        x, y, c = _place()
        swaps = [pltpu.make_async_remote_copy(src_ref=hin[a], dst_ref=hout[a], send_sem=send.at[a],
                                              recv_sem=recv.at[a], device_id=(x, y, 1 - c), device_id_type=MESH)
                 for a in range(n)]
        for cp in swaps:
            cp.start()
        for cp in swaps:
            cp.wait()

    return pl.pallas_call(
        body, name="pair_swap", in_specs=[HBM] * n, out_specs=[HBM] * n,
        out_shape=[jax.ShapeDtypeStruct(s.shape, s.dtype) for s in halves],
        scratch_shapes=[pltpu.SemaphoreType.DMA((n,)), pltpu.SemaphoreType.DMA((n,))],
    )(*halves)


def _pair_sum(grad, recv, c, tag):
    _, R, C = grad.shape
    hr = R // 2

    def body(c_ref, a_ref, b_ref, o_ref):
        o_ref[...] = (a_ref[...] + b_ref[...]).astype(BF16)

    return pl.pallas_call(
        body, name="pair_sum_" + tag,
        grid_spec=pltpu.PrefetchScalarGridSpec(
            num_scalar_prefetch=1, grid=(N_CHIPS,),
            in_specs=[pl.BlockSpec((1, hr, C), lambda s, cr: (s, cr[0], 0)),
                      pl.BlockSpec((1, hr, C), lambda s, cr: (s, 0, 0))],
            out_specs=pl.BlockSpec((1, hr, C), lambda s, cr: (s, 0, 0))),
        out_shape=jax.ShapeDtypeStruct((N_CHIPS, hr, C), BF16),
    )(c, grad, recv)


def _chip_sum(received, own, chip, tag):
    _, rows, C = received.shape
    tr = rows // 2

    def body(chip_ref, own_ref, r1_ref, r2_ref, r3_ref, o_ref):
        p = [r[0].astype(F32) for r in (own_ref, r1_ref, r2_ref, r3_ref)]
        o_ref[...] = (p[0] + p[1]) + (p[2] + p[3])

    def slot(k):
        return pl.BlockSpec((1, tr, C), lambda i, cr: (jnp.bitwise_xor(cr[0], k), i, 0))

    return pl.pallas_call(
        body, name="chip_sum_" + tag,
        grid_spec=pltpu.PrefetchScalarGridSpec(
            num_scalar_prefetch=1, grid=(rows // tr,), in_specs=[slot(0), slot(1), slot(2), slot(3)],
            out_specs=pl.BlockSpec((tr, C), lambda i, cr: (i, 0))),
        out_shape=jax.ShapeDtypeStruct((rows, C), F32),
    )(chip, own, received, received, received)


def _adamw_math(w, g, m, v):
    m = ADAM_B1 * m + (1.0 - ADAM_B1) * g
    v = ADAM_B2 * v + (1.0 - ADAM_B2) * (g * g)
    m_hat = m / (1.0 - ADAM_B1 ** ADAM_STEP)
    v_hat = v / (1.0 - ADAM_B2 ** ADAM_STEP)
    delta = -ADAM_LR * (m_hat / (jnp.sqrt(v_hat) + ADAM_EPS) + ADAM_WD * w)
    return delta, m, v


def _adamw(w, g_mine, g_other, m, v, c, tag):
    R, C = w.shape
    tr = R // 4

    def body(c_ref, w_ref, gm_ref, go_ref, m_ref, v_ref, g_ref, d_ref, nm_ref, nv_ref):
        g = jnp.where(pl.program_id(0) == c_ref[0], gm_ref[...], go_ref[...])
        g_ref[...] = g
        d_ref[...], nm_ref[...], nv_ref[...] = _adamw_math(w_ref[...], g, m_ref[...], v_ref[...])

    blk = pl.BlockSpec((tr, C), lambda h, i, cr: (2 * h + i, 0))
    half = pl.BlockSpec((tr, C), lambda h, i, cr: (i, 0))
    return pl.pallas_call(
        body, name="adamw_" + tag,
        grid_spec=pltpu.PrefetchScalarGridSpec(
            num_scalar_prefetch=1, grid=(2, 2), in_specs=[blk, half, half, blk, blk], out_specs=[blk] * 4),
        out_shape=[jax.ShapeDtypeStruct((R, C), F32)] * 4,
    )(c, w, g_mine, g_other, m, v)


def _small_update(all_small, w, m, v):
    def body(a_ref, w_ref, m_ref, v_ref, g_ref, d_ref, nm_ref, nv_ref):
        g = ((a_ref[0] + a_ref[1]) + (a_ref[2] + a_ref[3])) + ((a_ref[4] + a_ref[5]) + (a_ref[6] + a_ref[7]))
        g_ref[...] = g
        d_ref[...], nm_ref[...], nv_ref[...] = _adamw_math(w_ref[...], g, m_ref[...], v_ref[...])

    return pl.pallas_call(
        body, name="small_update", out_shape=[jax.ShapeDtypeStruct(w.shape, F32)] * 4,
    )(all_small, w, m, v)


SMALL_ROWS = (("attn", 0, 0), ("ffn", 1, 0), ("dil", 2, 0), ("sb", 2, GROUP), ("q", 3, 0), ("k", 3, HEAD_DIM))


def _pack_small(vals, D):
    rows = [jnp.zeros((1, D), F32) for _ in range(8)]
    for name, r, off in SMALL_ROWS:
        rows[r] = lax.dynamic_update_slice(rows[r], vals[name].astype(F32), (0, off))
    return jnp.concatenate(rows, axis=0)


def _unpack_small(packed, vals):
    return {name: packed[r:r + 1, off:off + vals[name].shape[1]] for name, r, off in SMALL_ROWS}


def kernel(x, attn_norm_w, w_in, q_norm_w, k_norm_w, dil_out_norm_w, sb_out_norm_w, w_out, ffn_norm_w, w_gate, w_up, w_down, loss_target, m_attn_norm_w, m_w_in, m_q_norm_w, m_k_norm_w, m_dil_out_norm_w, m_sb_out_norm_w, m_w_out, m_ffn_norm_w, m_w_gate, m_w_up, m_w_down, v_attn_norm_w, v_w_in, v_q_norm_w, v_k_norm_w, v_dil_out_norm_w, v_sb_out_norm_w, v_w_out, v_ffn_norm_w, v_w_gate, v_w_up, v_w_down):
    D = x.shape[-1]
    big_names = ("w_in", "w_out", "w_gate", "w_up", "w_down")
    flipped = ("w_gate", "w_up")
    tr = lambda a: jnp.swapaxes(a[0], 0, 1)
    big_w = dict(w_in=w_in[0], w_out=w_out[0], w_gate=tr(w_gate), w_up=tr(w_up), w_down=w_down[0])
    big_m = dict(w_in=m_w_in[0], w_out=m_w_out[0], w_gate=tr(m_w_gate), w_up=tr(m_w_up), w_down=m_w_down[0])
    big_v = dict(w_in=v_w_in[0], w_out=v_w_out[0], w_gate=tr(v_w_gate), w_up=tr(v_w_up), w_down=v_w_down[0])
    small_w = dict(attn=attn_norm_w, q=q_norm_w, k=k_norm_w, dil=dil_out_norm_w, sb=sb_out_norm_w, ffn=ffn_norm_w)
    small_m = dict(attn=m_attn_norm_w, q=m_q_norm_w, k=m_k_norm_w, dil=m_dil_out_norm_w, sb=m_sb_out_norm_w,
                   ffn=m_ffn_norm_w)
    small_v = dict(attn=v_attn_norm_w, q=v_q_norm_w, k=v_k_norm_w, dil=v_dil_out_norm_w, sb=v_sb_out_norm_w,
                   ffn=v_ffn_norm_w)

    c = lax.axis_index("c").astype(jnp.int32).reshape(1)
    chip = (2 * lax.axis_index("x") + lax.axis_index("y")).astype(jnp.int32).reshape(1)
    w_in_g, w_out_g = _gather_weights([big_w[n].astype(BF16) for n in big_names[:2]])
    ffn_slots = [_own_slots(big_w[n].astype(BF16)) for n in FFN_NAMES]

    loss_parts, grad_x, small_g, dw_in, dw_out, ffn_halves = _device_step(
        x[0], loss_target[0], attn_norm_w, q_norm_w, k_norm_w, dil_out_norm_w, sb_out_norm_w, ffn_norm_w,
        w_in_g, w_out_g.reshape(-1, D), ffn_slots, c, chip)
    loss = lax.psum(jnp.sum(loss_parts[:, 0, 0]) * (0.5 / D), ("x", "y", "c"))

    late = [dw_in, dw_out.reshape(N_CHIPS, -1, D)]
    *from_pair, all_small = _pair_exchange(late, _pack_small(small_g, D))
    chip_parts = [_pair_sum(g, r, c, n) for g, r, n in zip(late, from_pair, big_names)]
    from_chips = _chip_exchange(chip_parts)
    halves = [_chip_sum(r, p, chip, n) for r, p, n in zip(from_chips, chip_parts, big_names)] + ffn_halves
    others = _pair_swap(halves)
    big_out = {n: _adamw(big_w[n], mine, other, big_m[n], big_v[n], c, n)
               for n, mine, other in zip(big_names, halves, others)}
    sg, sd, sm, sv = _small_update(all_small, _pack_small(small_w, D), _pack_small(small_m, D),
                                   _pack_small(small_v, D))
    small_out = [_unpack_small(t, small_w) for t in (sg, sd, sm, sv)]

    order = (("attn", None), (None, "w_in"), ("q", None), ("k", None), ("dil", None), ("sb", None),
             (None, "w_out"), ("ffn", None), (None, "w_gate"), (None, "w_up"), (None, "w_down"))
    outs = [loss, grad_x[None]]
    for kind in range(4):
        for s_name, b_name in order:
            if s_name is not None:
                outs.append(small_out[kind][s_name])
            else:
                res = big_out[b_name][kind]
                outs.append((jnp.swapaxes(res, 0, 1) if b_name in flipped else res)[None])
    return tuple(outs)
```

```python
import functools

import jax
import jax.numpy as jnp
from jax import lax
from jax.experimental import pallas as pl
from jax.experimental.pallas import tpu as pltpu

F32 = jnp.float32
BF16 = jnp.bfloat16
MESH = pl.DeviceIdType.MESH

HEAD_DIM = 64
GROUP = 512
BLOCK = 128
LANES = 128
N_CHIPS = 4
N_DEV = 8
EPS = 1e-6
ROPE_THETA = 10000.0
DILATIONS = (1, 4, 16)
NEG = -1e30

ADAM_LR = 0.001
ADAM_B1 = 0.9
ADAM_B2 = 0.999
ADAM_EPS = 1e-08
ADAM_WD = 0.01
ADAM_STEP = 10


def _dot(a, b):
    return jnp.dot(a, b, preferred_element_type=F32)


def _dot_nt(a, b):
    return lax.dot_general(a, b, (((1,), (1,)), ((), ())), preferred_element_type=F32)


def _dot_tn(a, b):
    return lax.dot_general(a, b, (((0,), (0,)), ((), ())), preferred_element_type=F32)


def _split(v):
    hi = lax.bitcast_convert_type(lax.bitcast_convert_type(v, jnp.uint32) & jnp.uint32(0xFFFF0000), F32)
    return hi.astype(BF16), (v - hi).astype(BF16)


def _segsum(v, g):
    hi, lo = _split(v)
    return _dot(hi, g) + _dot(lo, g)


def _rot_half(x):
    outs = []
    for c in range(x.shape[1] // LANES):
        xc = x[:, c * LANES:(c + 1) * LANES]
        lane = lax.broadcasted_iota(jnp.int32, xc.shape, 1)
        first = (lane % HEAD_DIM) < (HEAD_DIM // 2)
        outs.append(jnp.where(first, pltpu.roll(xc, LANES - 32, 1), pltpu.roll(xc, 32, 1)))
    return outs[0] if len(outs) == 1 else jnp.concatenate(outs, axis=1)


def _rms(x):
    return lax.rsqrt(jnp.mean(x * x, axis=-1, keepdims=True) + EPS)


def _rms_bwd(dy, x, w):
    rstd = _rms(x)
    xh = x * rstd
    dxh = dy * w
    dx = rstd * (dxh - xh * jnp.mean(dxh * xh, axis=-1, keepdims=True))
    return dx, dy * xh


def _sigmoid(x):
    return 1.0 / (1.0 + jnp.exp(-x))


def _sum4(ref):
    p = [ref[j].astype(F32) for j in range(N_CHIPS)]
    return (p[0] + p[1]) + (p[2] + p[3])


def _full(shape):
    n = len(shape)
    return pl.BlockSpec(shape, lambda *_: (0,) * n)


def _strided_spec(tm, r):
    return pl.BlockSpec((r, tm // r, GROUP), lambda i: (0, i, 0))


def _strided_shape(S, r, dtype):
    return jax.ShapeDtypeStruct((r, S // r, GROUP), dtype)


def _to_strided(scr, val, outs):
    chunks = range(GROUP // LANES)
    for k in chunks:
        scr[k] = val[:, _lanes(k)]
    for r, o_ref in outs:
        if r == 1:
            o_ref[0] = val.astype(o_ref.dtype)
            continue
        n = val.shape[0] // r
        for c in range(r):
            rows = pl.ds(c, n, stride=r)
            o_ref[c] = jnp.concatenate([scr.at[k][rows, :] for k in chunks], axis=1).astype(o_ref.dtype)


def _from_strided(scr, ref):
    r, n, _ = ref.shape
    if r == 1:
        return ref[0].astype(F32)
    chunks = range(GROUP // LANES)
    for c in range(r):
        plane = ref[c].astype(F32)
        for k in chunks:
            scr.at[k][pl.ds(c, n, stride=r), :] = plane[:, _lanes(k)]
    return jnp.concatenate([scr[k] for k in chunks], axis=1)


def _strided_scratch(tm):
    return pltpu.VMEM((GROUP // LANES, tm, LANES), F32)


def _tile4(t):
    return jnp.concatenate([t] * (GROUP // LANES), axis=1)


def _in_proj_fwd(x, attn_w, w_in_g, qw, kw, cos_t, sin_t, seg_ones):
    S, D = x.shape
    tm = 512
    wcols = w_in_g.shape[2]
    nd = len(DILATIONS)

    def body(x_ref, aw_ref, w_ref, qw_ref, kw_ref, cos_ref, sin_ref, g_ref, h_ref, qa_ref, ka_ref, *rest):
        q_refs, k_refs, v_refs = rest[:nd], rest[nd:2 * nd], rest[2 * nd:3 * nd]
        qs_ref, ks_ref, vs_ref, scr = rest[3 * nd:]
        xv = x_ref[...]
        h = (xv * _rms(xv) * aw_ref[...]).astype(BF16)
        h_ref[...] = h
        proj = jnp.concatenate([_dot(h, w_ref[j]) for j in range(N_CHIPS)], axis=1)
        qa = proj[:, 0 * GROUP:1 * GROUP]
        ka = proj[:, 1 * GROUP:2 * GROUP]
        qa_ref[...] = qa
        ka_ref[...] = ka
        _to_strided(scr, proj[:, 2 * GROUP:3 * GROUP], list(zip(DILATIONS, v_refs)))
        qs_ref[...] = proj[:, 3 * GROUP:4 * GROUP].astype(BF16)
        ks_ref[...] = proj[:, 4 * GROUP:5 * GROUP].astype(BF16)
        vs_ref[...] = proj[:, 5 * GROUP:6 * GROUP].astype(BF16)
        g = g_ref[...]
        cos = _tile4(cos_ref[...])
        sin = _tile4(sin_ref[...])
        for t, w_r, o_rs in ((qa, qw_ref, q_refs), (ka, kw_ref, k_refs)):
            rstd = lax.rsqrt(_segsum(t * t, g) * (1.0 / HEAD_DIM) + EPS)
            tn = t * rstd * w_r[...]
            _to_strided(scr, tn * cos + _rot_half(tn) * sin, list(zip(DILATIONS, o_rs)))

    row = lambda i: (i, 0)
    tile = lambda n, dt: jax.ShapeDtypeStruct((S, n), dt)
    planes = [_strided_spec(tm, r) for r in DILATIONS]
    return pl.pallas_call(
        body, name="in_proj_fwd", grid=(S // tm,),
        in_specs=[pl.BlockSpec((tm, D), row), _full((1, D)), _full((N_CHIPS, D, wcols)),
                  _full((1, GROUP)), _full((1, GROUP)),
                  pl.BlockSpec((tm, LANES), row), pl.BlockSpec((tm, LANES), row),
                  _full((GROUP, GROUP))],
        out_specs=[pl.BlockSpec((tm, D), row)] + [pl.BlockSpec((tm, GROUP), row)] * 2 + planes * 3
                  + [pl.BlockSpec((tm, GROUP), row)] * 3,
        out_shape=[tile(D, BF16), tile(GROUP, F32), tile(GROUP, F32)]
                  + [_strided_shape(S, r, BF16) for r in DILATIONS] * 3 + [tile(GROUP, BF16)] * 3,
        scratch_shapes=[_strided_scratch(tm)],
    )(x, attn_w, w_in_g, qw, kw, cos_t, sin_t, seg_ones)


def _dil_fwd(q, k, v, slots):
    r, L, _ = q.shape
    nb = L // BLOCK
    P = GROUP // LANES
    ns = len(slots)

    def body(q_ref, kc_ref, kp_ref, vc_ref, vp_ref, *rest):
        o_ref, l_ref = rest[ns:ns + 2]
        n = pl.program_id(1)
        finish = (_hosted_gathers(rest[:ns], rest[ns + 2:2 * ns + 2], *rest[2 * ns + 2:],
                                  pl.program_id(0) * nb + n, r * nb) if ns else None)
        rowi = lax.broadcasted_iota(jnp.int32, (BLOCK, BLOCK), 0)
        coli = lax.broadcasted_iota(jnp.int32, (BLOCK, BLOCK), 1)
        first = coli < HEAD_DIM
        masks = (coli <= rowi, jnp.logical_and(coli >= rowi, n > 0))
        s2 = {}
        for hp in range(P):
            q2 = _scaled(q_ref[:, _lanes(hp)])
            for b, k_ref in enumerate((kc_ref, kp_ref)):
                s2[hp, b] = _dot_nt(q2, _by_head(k_ref[:, _lanes(hp)], first))
        ps, inv, lse = {}, {}, {}
        for hp in range(P):
            for h in range(2):
                s = [jnp.where(masks[b], s2[hp, b][:, h * BLOCK:(h + 1) * BLOCK], NEG) for b in range(2)]
                m = jnp.maximum(jnp.max(s[0], axis=1, keepdims=True), jnp.max(s[1], axis=1, keepdims=True))
                p = [jnp.exp(s[b] - m) for b in range(2)]
                den = jnp.sum(p[0], axis=1, keepdims=True) + jnp.sum(p[1], axis=1, keepdims=True)
                ps[hp, h] = [p[b].astype(BF16) for b in range(2)]
                inv[hp, h] = 1.0 / den
                lse[hp, h] = m + jnp.log(den)
        for hp in range(P):
            o = jnp.zeros((BLOCK, LANES), F32)
            for b, v_ref in enumerate((vc_ref, vp_ref)):
                o = o + _dot(jnp.concatenate([ps[hp, 0][b], ps[hp, 1][b]], axis=1),
                             _by_head(v_ref[:, _lanes(hp)], first))
            o_ref[:, _lanes(hp)] = o * jnp.where(first, inv[hp, 0], inv[hp, 1])
            l_ref[:, _lanes(hp)] = jnp.where(first, lse[hp, 0], lse[hp, 1])
        if finish is not None:
            finish()

    cur = pl.BlockSpec((None, BLOCK, GROUP), lambda c, n: (c, n, 0))
    prev = pl.BlockSpec((None, BLOCK, GROUP), lambda c, n: (c, jnp.maximum(n - 1, 0), 0))
    h_in, h_out, h_shape, h_sems = _hosted_specs(slots)
    return pl.pallas_call(
        body, name="dil_fwd_r%d" % r, grid=(r, nb),
        in_specs=[cur, cur, prev, cur, prev] + h_in, out_specs=[cur, cur] + h_out,
        out_shape=[jax.ShapeDtypeStruct(q.shape, F32)] * 2 + h_shape,
        input_output_aliases={5 + a: 2 + a for a in range(ns)},
        scratch_shapes=h_sems,
    )(q, k, k, v, v, *slots)


def _dil_bwd(q, k, v, do, lse, delta, part):
    r, L, _ = q.shape
    nb = L // BLOCK
    P = GROUP // LANES
    scale = HEAD_DIM ** -0.5

    def body(qc_ref, qn_ref, doc_ref, don_ref, lc_ref, ln_ref, dc_ref, dn_ref, k_ref, v_ref, part_in,
             dq_ref, dk_ref, dv_ref, part_out, carry, send, recv):
        j = pl.program_id(1)
        step = pl.program_id(0) * nb + j
        start, finish = _chip_send_plan(part_in, part_out, send, recv)
        pl.when(step == 0)(start)
        rowi = lax.broadcasted_iota(jnp.int32, (BLOCK, BLOCK), 0)
        coli = lax.broadcasted_iota(jnp.int32, (BLOCK, BLOCK), 1)
        first = coli < HEAD_DIM
        sides = ((qc_ref, doc_ref, lc_ref, dc_ref, coli <= rowi),
                 (qn_ref, don_ref, ln_ref, dn_ref, jnp.logical_and(coli >= rowi, j < nb - 1)))

        @pl.when(j == 0)
        def _():
            carry[...] = jnp.zeros_like(carry)

        kcat, q2, do2, s2, dp2 = {}, {}, {}, {}, {}
        for hp in range(P):
            kcat[hp] = _by_head(k_ref[:, _lanes(hp)], first)
            vcat = _by_head(v_ref[:, _lanes(hp)], first)
            for x, (q_r, do_r, _, _, _) in enumerate(sides):
                q2[hp, x] = _scaled(q_r[:, _lanes(hp)])
                do2[hp, x] = do_r[:, _lanes(hp)]
                s2[hp, x] = _dot_nt(q2[hp, x], kcat[hp])
                dp2[hp, x] = _dot_nt(do2[hp, x], vcat)
        pcat, dscat = {}, {}
        for hp in range(P):
            for x, (_, _, l_r, d_r, msk) in enumerate(sides):
                ps, dss = [], []
                for h in range(2):
                    col = hp * LANES + h * HEAD_DIM
                    half = slice(h * BLOCK, (h + 1) * BLOCK)
                    p = jnp.where(msk, jnp.exp(s2[hp, x][:, half] - l_r[:, col:col + 1]), 0.0)
                    ps.append(p.astype(BF16))
                    dss.append((p * (dp2[hp, x][:, half] - d_r[:, col:col + 1])).astype(BF16))
                pcat[hp, x] = jnp.concatenate(ps, axis=1)
                dscat[hp, x] = jnp.concatenate(dss, axis=1)
        for hp in range(P):
            dv2 = _dot_tn(pcat[hp, 0], do2[hp, 0]) + _dot_tn(pcat[hp, 1], do2[hp, 1])
            dk2 = _dot_tn(dscat[hp, 0], q2[hp, 0]) + _dot_tn(dscat[hp, 1], q2[hp, 1])
            dv_ref[:, _lanes(hp)] = jnp.where(first, dv2[:BLOCK], dv2[BLOCK:]).astype(BF16)
            dk_ref[:, _lanes(hp)] = jnp.where(first, dk2[:BLOCK], dk2[BLOCK:]).astype(BF16)
            dq_ref[:, _lanes(hp)] = (carry[:, _lanes(hp)] + _dot(dscat[hp, 0], kcat[hp]) * scale).astype(BF16)
            carry[:, _lanes(hp)] = _dot(dscat[hp, 1], kcat[hp]) * scale
        pl.when(step == r * nb - 1)(finish)

    cur = pl.BlockSpec((None, BLOCK, GROUP), lambda c, n: (c, n, 0))
    nxt = pl.BlockSpec((None, BLOCK, GROUP), lambda c, n: (c, jnp.minimum(n + 1, nb - 1), 0))
    return pl.pallas_call(
        body, name="dil_bwd_r%d" % r, grid=(r, nb),
        in_specs=[cur, nxt, cur, nxt, cur, nxt, cur, nxt, cur, cur, HBM], out_specs=[cur, cur, cur, HBM],
        out_shape=[jax.ShapeDtypeStruct(q.shape, BF16)] * 3 + [jax.ShapeDtypeStruct(part.shape, part.dtype)],
        scratch_shapes=[pltpu.VMEM((BLOCK, GROUP), F32), pltpu.SemaphoreType.DMA((3,)), pltpu.SemaphoreType.DMA((3,))],
    )(q, q, do, do, lse, lse, delta, delta, k, v, part)


SB_TILES = 2
SB_PAIRS_FWD = 4
SB_PAIRS_BWD = 2
SB_DEAD = -110.0


def _lanes(hp):
    return slice(hp * LANES, (hp + 1) * LANES)


def _sb_logits(z, valid):
    e = jnp.exp(-jnp.abs(z))
    lb = jnp.minimum(z, 0.0) - jnp.log(1.0 + e)
    lk = lb - z
    if valid is not None:
        lk = jnp.where(valid, lk, 0.0)
    return e, lb, lk


def _by_head(t, first):
    zero = jnp.zeros_like(t)
    return jnp.concatenate([jnp.where(first, t, zero), jnp.where(first, zero, t)], axis=0)


def _sb_valid(i, j):
    rowi = lax.broadcasted_iota(jnp.int32, (BLOCK, BLOCK), 0)
    coli = lax.broadcasted_iota(jnp.int32, (BLOCK, BLOCK), 1)
    return (coli - rowi) < (i - j) * BLOCK


def _scaled(q):
    return (q.astype(F32) * (HEAD_DIM ** -0.5)).astype(BF16)


def _hosted_gathers(refs_in, refs_out, send, recv, step, steps):
    plans = [_gather_plan(refs_in[a], refs_out[a], send.at[pl.ds(6 * a, 6)], recv.at[pl.ds(6 * a, 6)])
             for a in range(len(refs_in))]
    for stage, at in ((0, 0), (1, (2 * steps) // 3)):
        @pl.when(step == at)
        def _():
            for plan in plans:
                plan[stage]()

    def finish():
        @pl.when(step == steps - 1)
        def _():
            for plan in plans:
                plan[2]()

    return finish


def _hosted_specs(slots):
    n = len(slots)
    sems = [pltpu.SemaphoreType.DMA((6 * n,))] * 2 if n else []
    return [HBM] * n, [HBM] * n, [jax.ShapeDtypeStruct(s.shape, s.dtype) for s in slots], sems


def _sb_fwd(qs, ks, vs, tri_later, slots):
    S = qs.shape[0]
    P = SB_PAIRS_FWD
    W = P * LANES
    ns = len(slots)

    def body(q_ref, k_ref, v_ref, u_ref, *rest):
        o_ref, lt_ref, from_ref = rest[ns:ns + 3]
        i = pl.program_id(1)
        finish = _hosted_gathers(rest[:ns], rest[ns + 3:2 * ns + 3], *rest[2 * ns + 3:], i, S // BLOCK) if ns else None
        first = lax.broadcasted_iota(jnp.int32, (BLOCK, LANES), 1) < HEAD_DIM
        q2 = [_scaled(q_ref[:, _lanes(hp)]) for hp in range(P)]

        def chunk(ci, carry, masked):
            runs, accs = list(carry[0]), list(carry[1])
            units = [(t, hp) for t in reversed(range(SB_TILES)) for hp in range(P)]
            z2s, lbs, c2s = {}, {}, {}
            for t, hp in units:
                off = pl.multiple_of((ci * SB_TILES + t) * BLOCK, BLOCK)
                z2s[t, hp] = _dot_nt(q2[hp], _by_head(k_ref[pl.ds(off, BLOCK), _lanes(hp)], first))
            for t, hp in units:
                valid = _sb_valid(i, ci * SB_TILES + t) if masked else None
                for h in range(2):
                    _, lb, lk = _sb_logits(z2s[t, hp][:, h * BLOCK:(h + 1) * BLOCK], valid)
                    lbs[t, hp, h] = lb
                    c2s[t, hp, h] = _dot(jnp.concatenate(_split(lk), axis=1), u_ref[...])
            for t, hp in units:
                off = pl.multiple_of((ci * SB_TILES + t) * BLOCK, BLOCK)
                valid = _sb_valid(i, ci * SB_TILES + t) if masked else None
                a2 = []
                for h in range(2):
                    a = jnp.exp(lbs[t, hp, h] + c2s[t, hp, h][:, :BLOCK] + runs[2 * hp + h])
                    if masked:
                        a = jnp.where(valid, a, 0.0)
                    a2.append(a.astype(BF16))
                    runs[2 * hp + h] = runs[2 * hp + h] + c2s[t, hp, h][:, BLOCK:]
                vcat = _by_head(v_ref[pl.ds(off, BLOCK), _lanes(hp)], first)
                accs[hp] = accs[hp] + _dot(jnp.concatenate(a2, axis=1), vcat)
            return tuple(runs), tuple(accs)

        def alive(runs):
            top = functools.reduce(jnp.maximum, runs)
            return (jnp.max(top) > SB_DEAD).astype(jnp.int32)

        def step(c):
            t, _, runs, accs = c
            runs, accs = chunk(nfull - 1 - t, (runs, accs), False)
            return t + 1, alive(runs), runs, accs

        zero = jnp.zeros((BLOCK, LANES), F32)
        nfull = i // SB_TILES
        runs, accs = chunk(nfull, ((zero,) * (2 * P), (zero,) * P), True)
        done, _, runs, accs = lax.while_loop(lambda c: jnp.logical_and(c[0] < nfull, c[1] > 0), step,
                                             (jnp.int32(0), alive(runs), runs, accs))
        for hp in range(P):
            o_ref[:, _lanes(hp)] = accs[hp]
            lt_ref[:, _lanes(hp)] = jnp.where(first, runs[2 * hp], runs[2 * hp + 1])
        from_ref[...] = jnp.full(from_ref.shape, nfull - done, jnp.int32)
        if finish is not None:
            finish()

    assert W == GROUP
    blk = pl.BlockSpec((BLOCK, W), lambda hp, i: (i, hp))
    col = pl.BlockSpec((S, W), lambda hp, i: (0, hp))
    h_in, h_out, h_shape, h_sems = _hosted_specs(slots)
    return pl.pallas_call(
        body, name="sb_fwd", grid=(GROUP // W, S // BLOCK),
        in_specs=[blk, col, col, _full((2 * BLOCK, 2 * BLOCK))] + h_in,
        out_specs=[blk, blk, pl.BlockSpec((1, 8, LANES), lambda hp, i: (i, 0, 0))] + h_out,
        out_shape=[jax.ShapeDtypeStruct((S, GROUP), F32)] * 2
                  + [jax.ShapeDtypeStruct((S // BLOCK, 8, LANES), jnp.int32)] + h_shape,
        input_output_aliases={4 + a: 3 + a for a in range(ns)},
        scratch_shapes=h_sems,
    )(qs, ks, vs, tri_later, *slots)


def _sb_bwd(first_chunk, qs, ks, vs, do, ltot, tri_upto, tri_before):
    S = qs.shape[0]
    P = SB_PAIRS_BWD
    W = P * LANES

    def body(from_ref, q_ref, k_ref, v_ref, do_ref, lt_ref, w_ref, x_ref, dq_ref, dk_ref, dv_ref):
        i = pl.program_id(1)

        @pl.when(i == 0)
        def _():
            dk_ref[...] = jnp.zeros_like(dk_ref)
            dv_ref[...] = jnp.zeros_like(dv_ref)

        first = lax.broadcasted_iota(jnp.int32, (BLOCK, LANES), 1) < HEAD_DIM
        q2 = [_scaled(q_ref[:, _lanes(hp)]) for hp in range(P)]
        do2 = [do_ref[:, _lanes(hp)] for hp in range(P)]
        totals = [jnp.broadcast_to(lt_ref[:, n * HEAD_DIM:n * HEAD_DIM + 1], (BLOCK, LANES)) for n in range(2 * P)]

        def chunk(ci, carry, masked):
            keeps, grads, dqs = list(carry[0]), list(carry[1]), list(carry[2])
            units = [(t, hp) for t in range(SB_TILES) for hp in range(P)]
            offs = [pl.multiple_of((ci * SB_TILES + t) * BLOCK, BLOCK) for t in range(SB_TILES)]
            valids = [_sb_valid(i, ci * SB_TILES + t) if masked else None for t in range(SB_TILES)]
            kcat, z2, da2, es, lbs, c2s, as_, des, p2s = {}, {}, {}, {}, {}, {}, {}, {}, {}
            for t, hp in units:
                kcat[t, hp] = _by_head(k_ref[pl.ds(offs[t], BLOCK), _lanes(hp)], first)
                z2[t, hp] = _dot_nt(q2[hp], kcat[t, hp])
                da2[t, hp] = _dot_nt(do2[hp], _by_head(v_ref[pl.ds(offs[t], BLOCK), _lanes(hp)], first))
            for t, hp in units:
                for h in range(2):
                    es[t, hp, h], lbs[t, hp, h], lk = _sb_logits(z2[t, hp][:, h * BLOCK:(h + 1) * BLOCK], valids[t])
                    c2s[t, hp, h] = _dot(jnp.concatenate(_split(lk), axis=1), w_ref[...])
            for t, hp in units:
                for h in range(2):
                    n = 2 * hp + h
                    a = jnp.exp(lbs[t, hp, h] + (totals[n] - (keeps[n] + c2s[t, hp, h][:, :BLOCK])))
                    if masked:
                        a = jnp.where(valids[t], a, 0.0)
                    keeps[n] = keeps[n] + c2s[t, hp, h][:, BLOCK:]
                    de = a * da2[t, hp][:, h * BLOCK:(h + 1) * BLOCK]
                    as_[t, hp, h], des[t, hp, h] = a.astype(BF16), de
                    p2s[t, hp, h] = _dot(jnp.concatenate(_split(de), axis=1), x_ref[...])
            for t, hp in units:
                dz2 = []
                for h in range(2):
                    n = 2 * hp + h
                    e = es[t, hp, h]
                    sig = jnp.where(z2[t, hp][:, h * BLOCK:(h + 1) * BLOCK] >= 0.0, 1.0, e) / (1.0 + e)
                    dz = des[t, hp, h] * (1.0 - sig) - (grads[n] + p2s[t, hp, h][:, :BLOCK]) * sig
                    if masked:
                        dz = jnp.where(valids[t], dz, 0.0)
                    grads[n] = grads[n] + p2s[t, hp, h][:, BLOCK:]
                    dz2.append(dz.astype(BF16))
                dzcat = jnp.concatenate(dz2, axis=1)
                dk2 = _dot_tn(dzcat, q2[hp])
                dv2 = _dot_tn(jnp.concatenate([as_[t, hp, 0], as_[t, hp, 1]], axis=1), do2[hp])
                dk_ref[pl.ds(offs[t], BLOCK), _lanes(hp)] += jnp.where(first, dk2[:BLOCK], dk2[BLOCK:])
                dv_ref[pl.ds(offs[t], BLOCK), _lanes(hp)] += jnp.where(first, dv2[:BLOCK], dv2[BLOCK:])
                dqs[hp] = dqs[hp] + _dot(dzcat, kcat[t, hp])
            return tuple(keeps), tuple(grads), tuple(dqs)

        zero = jnp.zeros((BLOCK, LANES), F32)
        nfull = i // SB_TILES
        carry = lax.fori_loop(from_ref[i], nfull, lambda t, c: chunk(t, c, False),
                              ((zero,) * (2 * P), (zero,) * (2 * P), (zero,) * P))
        carry = chunk(nfull, carry, True)
        for hp in range(P):
            dq_ref[:, _lanes(hp)] = carry[2][hp] * (HEAD_DIM ** -0.5)

    blk = pl.BlockSpec((BLOCK, W), lambda hp, i, fr: (i, hp))
    col = pl.BlockSpec((S, W), lambda hp, i, fr: (0, hp))
    tri = pl.BlockSpec((2 * BLOCK, 2 * BLOCK), lambda hp, i, fr: (0, 0))
    return pl.pallas_call(
        body, name="sb_bwd",
        grid_spec=pltpu.PrefetchScalarGridSpec(
            num_scalar_prefetch=1, grid=(GROUP // W, S // BLOCK),
            in_specs=[blk, col, col, blk, blk, tri, tri], out_specs=[blk, col, col]),
        out_shape=[jax.ShapeDtypeStruct((S, GROUP), F32)] * 3,
    )(first_chunk, qs, ks, vs, do, ltot, tri_upto, tri_before)


def _out_proj_fwd(o_br, l_br, o_sb, x, w_dil, w_sbn, w_out_g):
    S, D = x.shape
    tm = 512

    def body(o0, o1, o2, l0, l1, l2, os_ref, x_ref, wd_ref, ws_ref, w_ref, od_ref, s0, s1, s2, x1_ref, scr):
        ls = [_from_strided(scr, l) for l in (l0, l1, l2)]
        os_ = [_from_strided(scr, o) for o in (o0, o1, o2)]
        m = jnp.maximum(jnp.maximum(ls[0], ls[1]), ls[2])
        es = [jnp.exp(l - m) for l in ls]
        den = es[0] + es[1] + es[2]
        od = (es[0] * os_[0] + es[1] * os_[1] + es[2] * os_[2]) / den
        od_ref[...] = od
        _to_strided(scr, m + jnp.log(den), list(zip(DILATIONS, (s0, s1, s2))))
        osb = os_ref[...]
        mixed = jnp.concatenate([(od * _rms(od) * wd_ref[...]).astype(BF16),
                                 (osb * _rms(osb) * ws_ref[...]).astype(BF16)], axis=1)
        x1_ref[...] = x_ref[...] + _dot(mixed, w_ref[...])

    row = lambda i: (i, 0)
    g = pl.BlockSpec((tm, GROUP), row)
    d = pl.BlockSpec((tm, D), row)
    planes = [_strided_spec(tm, r) for r in DILATIONS]
    return pl.pallas_call(
        body, name="out_proj_fwd", grid=(S // tm,),
        in_specs=planes * 2 + [g, d, _full((1, GROUP)), _full((1, GROUP)), _full((2 * GROUP, D))],
        out_specs=[g] + planes + [d],
        out_shape=[jax.ShapeDtypeStruct((S, GROUP), F32)] + [_strided_shape(S, r, F32) for r in DILATIONS]
                  + [jax.ShapeDtypeStruct((S, D), F32)],
        scratch_shapes=[_strided_scratch(tm)],
    )(*o_br, *l_br, o_sb, x, w_dil, w_sbn, w_out_g)


def _ffn_fwd(x1, target, ffn_w, wg_g, wu_g, wd_g):
    S, D = x1.shape
    F = wg_g.shape[1]
    tm = 512
    nt = S // tm

    def body(x_ref, t_ref, nw_ref, wg_ref, wu_ref, wd_ref, h_ref, g_ref, u_ref, dy_ref, loss_ref, h_s, acc):
        j = pl.program_id(1)

        @pl.when(j == 0)
        def _():
            xv = x_ref[...]
            h = (xv * _rms(xv) * nw_ref[...]).astype(BF16)
            h_s[...] = h
            h_ref[...] = h
            acc[...] = xv

        h = h_s[...]
        g = _dot_nt(h, wg_ref[0])
        u = _dot_nt(h, wu_ref[0])
        g_ref[0] = g.astype(BF16)
        u_ref[0] = u.astype(BF16)
        a = (g * _sigmoid(g) * u).astype(BF16)
        acc[...] += _dot(a, wd_ref[0])

        @pl.when(j == N_CHIPS - 1)
        def _():
            err = acc[...] - t_ref[...]
            dy_ref[...] = err * (1.0 / D)
            loss_ref[...] = jnp.full(loss_ref.shape, jnp.sum(err * err), F32)

    row = lambda t, j: (t, 0)
    shard = lambda t, j: (j, 0, 0)
    act = lambda t, j: (j, t, 0)
    return pl.pallas_call(
        body, name="ffn_fwd", grid=(nt, N_CHIPS),
        in_specs=[pl.BlockSpec((tm, D), row), pl.BlockSpec((tm, D), row), pl.BlockSpec((1, D), lambda t, j: (0, 0))]
                 + [pl.BlockSpec((1, F, D), shard)] * 3,
        out_specs=[pl.BlockSpec((tm, D), row), pl.BlockSpec((1, tm, F), act), pl.BlockSpec((1, tm, F), act),
                   pl.BlockSpec((tm, D), row), pl.BlockSpec((1, 8, LANES), lambda t, j: (t, 0, 0))],
        out_shape=[jax.ShapeDtypeStruct((S, D), BF16), jax.ShapeDtypeStruct((N_CHIPS, S, F), BF16),
                   jax.ShapeDtypeStruct((N_CHIPS, S, F), BF16), jax.ShapeDtypeStruct((S, D), F32),
                   jax.ShapeDtypeStruct((nt, 8, LANES), F32)],
        scratch_shapes=[pltpu.VMEM((tm, D), BF16), pltpu.VMEM((tm, D), F32)],
    )(x1, target, ffn_w, wg_g, wu_g, wd_g)


def _ffn_bwd(h2, dy, g, u, wg_g, wu_g, wd_g):
    S, D = dy.shape
    F = wg_g.shape[1]
    tm = 512

    def body(h_ref, dy_ref, g_ref, u_ref, wg_ref, wu_ref, wd_ref, dwg_ref, dwu_ref, dwd_ref, dh_ref):
        t = pl.program_id(1)

        @pl.when(t == 0)
        def _():
            dwg_ref[...] = jnp.zeros_like(dwg_ref)
            dwu_ref[...] = jnp.zeros_like(dwu_ref)
            dwd_ref[...] = jnp.zeros_like(dwd_ref)

        h = h_ref[...]
        dyb = dy_ref[...].astype(BF16)
        gv = g_ref[0].astype(F32)
        uv = u_ref[0].astype(F32)
        da = _dot_nt(dyb, wd_ref[0])
        sg = _sigmoid(gv)
        silu = gv * sg
        du = (da * silu).astype(BF16)
        dg = (da * uv * (sg * (1.0 + gv * (1.0 - sg)))).astype(BF16)
        dwd_ref[0] += _dot_tn((silu * uv).astype(BF16), dyb)
        dwg_ref[0] += _dot_tn(dg, h)
        dwu_ref[0] += _dot_tn(du, h)
        dh_ref[0] = (_dot(dg, wg_ref[0]) + _dot(du, wu_ref[0])).astype(BF16)

    row = lambda j, t: (t, 0)
    shard = lambda j, t: (j, 0, 0)
    act = lambda j, t: (j, t, 0)
    return pl.pallas_call(
        body, name="ffn_bwd", grid=(N_CHIPS, S // tm),
        in_specs=[pl.BlockSpec((tm, D), row), pl.BlockSpec((tm, D), row),
                  pl.BlockSpec((1, tm, F), act), pl.BlockSpec((1, tm, F), act)] + [pl.BlockSpec((1, F, D), shard)] * 3,
        out_specs=[pl.BlockSpec((1, F, D), shard)] * 3 + [pl.BlockSpec((1, tm, D), act)],
        out_shape=[jax.ShapeDtypeStruct((N_CHIPS, F, D), F32)] * 3 + [jax.ShapeDtypeStruct((N_CHIPS, S, D), BF16)],
    )(h2, dy, g, u, wg_g, wu_g, wd_g)


def _out_proj_bwd(dh2p, dy, x1, ffn_w, w_out_g, o_dil, o_sb, w_dil, w_sbn, seg_ones, ffn_grads):
    S, D = dy.shape
    tm = 256
    ng = len(ffn_grads)

    def body(dh_ref, dy_ref, x1_ref, nw_ref, w_ref, od_ref, os_ref, wd_ref, ws_ref, g_ref, *rest):
        gin, rest = rest[:ng], rest[ng:]
        dx1_ref, dod0, dod1, dod2, dos_ref, dl0, dl1, dl2, dw_ref, dnw_ref, dwd_ref, dws_ref = rest[:12]
        gout, (scr, send, recv) = rest[12:12 + ng], rest[12 + ng:]
        i = pl.program_id(0)
        plans = [_pair_send_plan(gin[a], gout[a], send.at[a], recv.at[a]) for a in range(ng)]

        @pl.when(i == 0)
        def _():
            for start, _ in plans:
                start()

        @pl.when(i == 0)
        def _():
            for r_ in (dw_ref, dnw_ref, dwd_ref, dws_ref):
                r_[...] = jnp.zeros_like(r_)

        dh2 = _sum4(dh_ref)
        dxn, dwn = _rms_bwd(dh2, x1_ref[...], nw_ref[...])
        dnw_ref[...] += jnp.sum(dwn, axis=0, keepdims=True)
        dx1 = dy_ref[...] + dxn
        dx1_ref[...] = dx1
        dx1b = dx1.astype(BF16)
        dmix = _dot_nt(dx1b, w_ref[...])
        od = od_ref[...]
        osb = os_ref[...]
        mixed = jnp.concatenate([(od * _rms(od) * wd_ref[...]).astype(BF16),
                                 (osb * _rms(osb) * ws_ref[...]).astype(BF16)], axis=1)
        dw_ref[...] += _dot_tn(mixed, dx1b)
        do, dwo = _rms_bwd(dmix[:, :GROUP], od, wd_ref[...])
        dwd_ref[...] += jnp.sum(dwo, axis=0, keepdims=True)
        _to_strided(scr, do, list(zip(DILATIONS, (dod0, dod1, dod2))))
        _to_strided(scr, _segsum(do * od, g_ref[...]), list(zip(DILATIONS, (dl0, dl1, dl2))))
        do, dwo = _rms_bwd(dmix[:, GROUP:], osb, ws_ref[...])
        dws_ref[...] += jnp.sum(dwo, axis=0, keepdims=True)
        dos_ref[...] = do.astype(BF16)

        @pl.when(i == S // tm - 1)
        def _():
            for _, finish in plans:
                finish()

    row = lambda i: (i, 0)
    gsp = pl.BlockSpec((tm, GROUP), row)
    dsp = pl.BlockSpec((tm, D), row)
    planes = [_strided_spec(tm, r) for r in DILATIONS]
    halves = [jax.ShapeDtypeStruct((g.shape[0], g.shape[1] // 2, g.shape[2]), g.dtype) for g in ffn_grads]
    return pl.pallas_call(
        body, name="out_proj_bwd", grid=(S // tm,),
        in_specs=[pl.BlockSpec((N_CHIPS, tm, D), lambda i: (0, i, 0)), dsp, dsp, _full((1, D)), _full((2 * GROUP, D)),
                  gsp, gsp, _full((1, GROUP)), _full((1, GROUP)), _full((GROUP, GROUP))] + [HBM] * ng,
        out_specs=[dsp] + planes + [gsp] + planes
                  + [_full((2 * GROUP, D)), _full((1, D)), _full((1, GROUP)), _full((1, GROUP))] + [HBM] * ng,
        out_shape=[jax.ShapeDtypeStruct((S, D), F32)] + [_strided_shape(S, r, BF16) for r in DILATIONS]
                  + [jax.ShapeDtypeStruct((S, GROUP), BF16)] + [_strided_shape(S, r, F32) for r in DILATIONS]
                  + [jax.ShapeDtypeStruct((2 * GROUP, D), F32),
                     jax.ShapeDtypeStruct((1, D), F32), jax.ShapeDtypeStruct((1, GROUP), F32),
                     jax.ShapeDtypeStruct((1, GROUP), F32)] + halves,
        scratch_shapes=[_strided_scratch(tm), pltpu.SemaphoreType.DMA((ng,)), pltpu.SemaphoreType.DMA((ng,))],
    )(dh2p, dy, x1, ffn_w, w_out_g, o_dil, o_sb, w_dil, w_sbn, seg_ones, *ffn_grads)


def _qk_bwd(dq_br, dk_br, dv_br, dqs, dks, dvs, qa, ka, qw, kw, cos_t, sin_t, seg_ones):
    S = qa.shape[0]
    tm = 256

    def body(q0, q1, q2, k0, k1, k2, v0, v1, v2, dqs_ref, dks_ref, dvs_ref, qa_ref, ka_ref, qw_ref, kw_ref,
             cos_ref, sin_ref, g_ref, dp_ref, dqw_ref, dkw_ref, accq, acck, scr):
        i = pl.program_id(0)

        @pl.when(i == 0)
        def _():
            accq[...] = jnp.zeros_like(accq)
            acck[...] = jnp.zeros_like(acck)

        def branches(refs):
            return (_from_strided(scr, refs[0]) + _from_strided(scr, refs[1])) + _from_strided(scr, refs[2])

        g = g_ref[...]
        cos = _tile4(cos_ref[...])
        sin = _tile4(sin_ref[...])
        for b, (refs, pre_ref, w_ref, acc) in enumerate((((q0, q1, q2), qa_ref, qw_ref, accq),
                                                        ((k0, k1, k2), ka_ref, kw_ref, acck))):
            dh = branches(refs)
            dn = dh * cos + _rot_half(dh * sin)
            pre = pre_ref[...]
            rstd = lax.rsqrt(_segsum(pre * pre, g) * (1.0 / HEAD_DIM) + EPS)
            xh = pre * rstd
            acc[...] += jnp.sum(dn * xh, axis=0, keepdims=True)
            dxh = dn * w_ref[...]
            dpre = rstd * (dxh - xh * (_segsum(dxh * xh, g) * (1.0 / HEAD_DIM)))
            dp_ref[:, b * GROUP:(b + 1) * GROUP] = dpre.astype(BF16)
        dp_ref[:, 2 * GROUP:3 * GROUP] = branches((v0, v1, v2)).astype(BF16)
        dp_ref[:, 3 * GROUP:4 * GROUP] = dqs_ref[...].astype(BF16)
        dp_ref[:, 4 * GROUP:5 * GROUP] = dks_ref[...].astype(BF16)
        dp_ref[:, 5 * GROUP:6 * GROUP] = dvs_ref[...].astype(BF16)

        @pl.when(i == S // tm - 1)
        def _():
            for acc, o_ref in ((accq, dqw_ref), (acck, dkw_ref)):
                a = acc[...]
                pair = (a[:, 0:LANES] + a[:, LANES:2 * LANES]) + (a[:, 2 * LANES:3 * LANES] + a[:, 3 * LANES:4 * LANES])
                o_ref[...] = pair + pltpu.roll(pair, HEAD_DIM, 1)

    row = lambda i: (i, 0)
    gsp = pl.BlockSpec((tm, GROUP), row)
    tab = pl.BlockSpec((tm, LANES), row)
    planes = [_strided_spec(tm, r) for r in DILATIONS]
    return pl.pallas_call(
        body, name="qk_bwd", grid=(S // tm,),
        in_specs=planes * 3 + [gsp] * 5 + [_full((1, GROUP)), _full((1, GROUP)), tab, tab, _full((GROUP, GROUP))],
        out_specs=[pl.BlockSpec((tm, 6 * GROUP), row), _full((1, LANES)), _full((1, LANES))],
        out_shape=[jax.ShapeDtypeStruct((S, 6 * GROUP), BF16), jax.ShapeDtypeStruct((1, LANES), F32),
                   jax.ShapeDtypeStruct((1, LANES), F32)],
        scratch_shapes=[pltpu.VMEM((1, GROUP), F32), pltpu.VMEM((1, GROUP), F32), _strided_scratch(tm)],
    )(*dq_br, *dk_br, *dv_br, dqs, dks, dvs, qa, ka, qw, kw, cos_t, sin_t, seg_ones)


def _in_proj_bwd(h, dproj, w_in_g):
    S, D = h.shape
    wc = w_in_g.shape[2]
    tm = 512

    def body(h_ref, dp_ref, w_ref, dw_ref, dh_ref):
        t = pl.program_id(1)

        @pl.when(t == 0)
        def _():
            dw_ref[...] = jnp.zeros_like(dw_ref)

        dp = dp_ref[...]
        dw_ref[0] += _dot_tn(h_ref[...], dp)
        dh_ref[0] = _dot_nt(dp, w_ref[0]).astype(BF16)

    return pl.pallas_call(
        body, name="in_proj_bwd", grid=(N_CHIPS, S // tm),
        in_specs=[pl.BlockSpec((tm, D), lambda j, t: (t, 0)), pl.BlockSpec((tm, wc), lambda j, t: (t, j)),
                  pl.BlockSpec((1, D, wc), lambda j, t: (j, 0, 0))],
        out_specs=[pl.BlockSpec((1, D, wc), lambda j, t: (j, 0, 0)), pl.BlockSpec((1, tm, D), lambda j, t: (j, t, 0))],
        out_shape=[jax.ShapeDtypeStruct((N_CHIPS, D, wc), F32), jax.ShapeDtypeStruct((N_CHIPS, S, D), BF16)],
    )(h, dproj, w_in_g)


def _in_norm_bwd(dhp, dx1, x, attn_w):
    S, D = x.shape
    tm = 512

    def body(dh_ref, dx1_ref, x_ref, w_ref, gx_ref, dw_ref):
        i = pl.program_id(0)

        @pl.when(i == 0)
        def _():
            dw_ref[...] = jnp.zeros_like(dw_ref)

        dh = _sum4(dh_ref)
        dx, dw = _rms_bwd(dh, x_ref[...], w_ref[...])
        dw_ref[...] += jnp.sum(dw, axis=0, keepdims=True)
        gx_ref[...] = dx1_ref[...] + dx

    row = lambda i: (i, 0)
    dsp = pl.BlockSpec((tm, D), row)
    return pl.pallas_call(
        body, name="in_norm_bwd", grid=(S // tm,),
        in_specs=[pl.BlockSpec((N_CHIPS, tm, D), lambda i: (0, i, 0)), dsp, dsp, _full((1, D))],
        out_specs=[dsp, _full((1, D))],
        out_shape=[jax.ShapeDtypeStruct((S, D), F32), jax.ShapeDtypeStruct((1, D), F32)],
    )(dhp, dx1, x, attn_w)


def _constants(S):
    pos = jnp.arange(S, dtype=F32)
    inv_freq = ROPE_THETA ** (-jnp.arange(0, HEAD_DIM, 2, dtype=F32) / HEAD_DIM)
    ang = pos[:, None] * inv_freq[None, :]
    cos, sin = jnp.cos(ang), jnp.sin(ang)
    cos_t = jnp.concatenate([cos, cos] * 2, axis=1)
    sin_t = jnp.concatenate([-sin, sin] * 2, axis=1)
    idx = jnp.arange(GROUP)
    seg_ones = (idx[:, None] // HEAD_DIM == idx[None, :] // HEAD_DIM).astype(BF16)
    r = jnp.arange(BLOCK)
    ones = jnp.ones((BLOCK, BLOCK), BF16)
    tris = [jnp.concatenate([jnp.concatenate([m.astype(BF16), ones], axis=1)] * 2, axis=0) for m in
            (r[:, None] > r[None, :],
             r[:, None] <= r[None, :],
             r[:, None] < r[None, :])]
    return cos_t, sin_t, seg_ones, tris


FFN_NAMES = ("w_gate", "w_up", "w_down")


def _device_step(x, target, attn_w, qn_w, kn_w, dil_w, sbn_w, ffn_w, w_in_g, w_out_slots, ffn_slots, core, chip):
    S = x.shape[0]
    cos_t, sin_t, seg_ones, (tri_later, tri_upto, tri_before) = _constants(S)
    reps = GROUP // HEAD_DIM
    qw = jnp.tile(qn_w, (1, reps))
    kw = jnp.tile(kn_w, (1, reps))

    nd = len(DILATIONS)
    h, qa, ka, *rest = _in_proj_fwd(x, attn_w, w_in_g, qw, kw, cos_t, sin_t, seg_ones)
    qh, kh, va, (qs, ks, vs) = rest[:nd], rest[nd:2 * nd], rest[2 * nd:3 * nd], rest[3 * nd:]
    hosted = ([ffn_slots[0]], [w_out_slots], [])
    branches = [_dil_fwd(qh[b], kh[b], va[b], hosted[b]) for b in range(nd)]
    wg_g, w_out_g = branches[0][2], branches[1][2].reshape(-1, x.shape[1])
    o_sb, ltot, walked, wu_g, wd_g = _sb_fwd(qs, ks, vs, tri_later, ffn_slots[1:])
    o_dil, *lse, x1 = _out_proj_fwd([b[0] for b in branches], [b[1] for b in branches], o_sb, x, dil_w, sbn_w, w_out_g)
    h2, g, u, dy, loss_parts = _ffn_fwd(x1, target, ffn_w, wg_g, wu_g, wd_g)

    *ffn_grads, dh2p = _ffn_bwd(h2, dy, g, u, wg_g, wu_g, wd_g)
    dx1, *mid, dw_out, dffn_w, ddil_w, dsbn_w, p0, p1, p2 = _out_proj_bwd(
        dh2p, dy, x1, ffn_w, w_out_g, o_dil, o_sb, dil_w, sbn_w, seg_ones, ffn_grads)
    do_dil, do_sb, delta = mid[:nd], mid[nd], mid[nd + 1:]
    parts = [_pair_sum(gr, fr, core, n) for gr, fr, n in zip(ffn_grads, (p0, p1, p2), FFN_NAMES)]
    dqs, dks, dvs = _sb_bwd(walked[:, 0, 0], qs, ks, vs, do_sb, ltot, tri_upto, tri_before)
    dbr = [_dil_bwd(qh[b], kh[b], va[b], do_dil[b], lse[b], delta[b], parts[b]) for b in range(nd)]
    ffn_halves = [_chip_sum(dbr[b][3], parts[b], chip, FFN_NAMES[b]) for b in range(nd)]
    dproj, dqw, dkw = _qk_bwd([b[0] for b in dbr], [b[1] for b in dbr], [b[2] for b in dbr], dqs, dks, dvs,
                              qa, ka, qw, kw, cos_t, sin_t, seg_ones)
    dw_in, dhp = _in_proj_bwd(h, dproj, w_in_g)
    grad_x, dattn_w = _in_norm_bwd(dhp, dx1, x, attn_w)
    small = dict(attn=dattn_w, q=dqw[:, :HEAD_DIM], k=dkw[:, :HEAD_DIM], dil=ddil_w, sb=dsbn_w, ffn=dffn_w)
    return loss_parts, grad_x, small, dw_in, dw_out, ffn_halves


HBM = pl.BlockSpec(memory_space=pltpu.HBM)
VMEM = pl.BlockSpec(memory_space=pltpu.VMEM)
CHIP_FLIPS = ((1, 0), (0, 1), (1, 1))


def _place():
    return lax.axis_index("x"), lax.axis_index("y"), lax.axis_index("c")


def _flip(v, d):
    return 1 - v if d else v


def _half_rows(c, n):
    return pl.ds(pl.multiple_of(c * (n // 2), 16), n // 2)


def _gather_plan(slot_in, slot_out, send, recv):
    x, y, c = _place()
    p = 2 * x + y
    chips = [(_flip(x, dx), _flip(y, dy)) for dx, dy in CHIP_FLIPS]
    mine, other = _half_rows(c, slot_in.shape[1]), _half_rows(1 - c, slot_in.shape[1])

    def copy(k, src, dst, to):
        return pltpu.make_async_remote_copy(src_ref=src, dst_ref=dst, send_sem=send.at[k], recv_sem=recv.at[k],
                                            device_id=to, device_id_type=MESH)

    def first(k):
        return copy(k, slot_in.at[p, mine], slot_out.at[p, mine], (*chips[k], c))

    def passed(k, rows):
        land = slot_out.at[2 * chips[k][0] + chips[k][1], rows]
        return copy(3 + k, land, land, (x, y, 1 - c))

    def start():
        for k in range(3):
            first(k).start()

    def forward():
        for k in range(3):
            land = slot_out.at[2 * chips[k][0] + chips[k][1], mine]
            copy(k, land, land, (*chips[k], c)).wait_recv()
            passed(k, mine).start()

    def finish():
        for k in range(3):
            passed(k, other).wait_recv()
        for k in range(3):
            first(k).wait_send()
            passed(k, mine).wait_send()

    return start, forward, finish


def _chip_send_plan(part_in, recv_out, send, recv):
    x, y, c = _place()
    p = 2 * x + y
    chips = [(_flip(x, dx), _flip(y, dy)) for dx, dy in CHIP_FLIPS]

    def copy(k):
        q = 2 * chips[k][0] + chips[k][1]
        return pltpu.make_async_remote_copy(src_ref=part_in.at[q], dst_ref=recv_out.at[p], send_sem=send.at[k],
                                            recv_sem=recv.at[k], device_id=(*chips[k], c), device_id_type=MESH)

    def start():
        for k in range(3):
            copy(k).start()

    def finish():
        for k in range(3):
            land = recv_out.at[2 * chips[k][0] + chips[k][1]]
            pltpu.make_async_remote_copy(src_ref=land, dst_ref=land, send_sem=send.at[k], recv_sem=recv.at[k],
                                         device_id=(*chips[k], c), device_id_type=MESH).wait_recv()
        for k in range(3):
            copy(k).wait_send()

    return start, finish


def _pair_send_plan(grad_in, recv_out, send, recv):
    x, y, c = _place()

    def copy():
        theirs = _half_rows(1 - c, grad_in.shape[1])
        return pltpu.make_async_remote_copy(src_ref=grad_in.at[:, theirs, :], dst_ref=recv_out, send_sem=send,
                                            recv_sem=recv, device_id=(x, y, 1 - c), device_id_type=MESH)

    return (lambda: copy().start()), (lambda: copy().wait())


def _own_slots(shard):
    here = 2 * lax.axis_index("x") + lax.axis_index("y")
    return lax.dynamic_update_slice(lax.empty((N_CHIPS,) + shard.shape, shard.dtype), shard[None], (here, 0, 0))


def _gather_weights(shards):
    n = len(shards)

    def body(*refs):
        ins, outs = refs[:n], refs[n:2 * n]
        send, recv = refs[2 * n:]
        plans = [_gather_plan(ins[a], outs[a], send.at[pl.ds(6 * a, 6)], recv.at[pl.ds(6 * a, 6)]) for a in range(n)]
        for stage in range(3):
            for plan in plans:
                plan[stage]()

    slots = [_own_slots(s) for s in shards]
    return pl.pallas_call(
        body, name="gather_weights", in_specs=[HBM] * n, out_specs=[HBM] * n,
        out_shape=[jax.ShapeDtypeStruct(s.shape, s.dtype) for s in slots],
        input_output_aliases={a: a for a in range(n)},
        scratch_shapes=[pltpu.SemaphoreType.DMA((6 * n,)), pltpu.SemaphoreType.DMA((6 * n,))],
    )(*slots)


def _pair_exchange(grads, small):
    n = len(grads)

    def body(*refs):
        gin, sm = refs[:n], refs[n]
        gout, sm_all = refs[n + 1:2 * n + 1], refs[2 * n + 1]
        send, recv = refs[2 * n + 2:]
        x, y, c = _place()
        me = 4 * x + 2 * y + c
        big = [_pair_send_plan(gin[a], gout[a], send.at[a], recv.at[a]) for a in range(n)]
        for start, _ in big:
            start()
        sm_all[pl.ds(me, 1)] = sm[...][None]
        tiny = []
        for k in range(1, N_DEV):
            px, py, pc = _flip(x, k & 4), _flip(y, k & 2), _flip(c, k & 1)
            tiny.append((pltpu.make_async_remote_copy(
                src_ref=sm, dst_ref=sm_all.at[me], send_sem=send.at[n + k - 1], recv_sem=recv.at[n + k - 1],
                device_id=(px, py, pc), device_id_type=MESH), 4 * px + 2 * py + pc))
            tiny[-1][0].start()
        for k, (cp, peer) in enumerate(tiny):
            pltpu.make_async_remote_copy(src_ref=sm, dst_ref=sm_all.at[peer], send_sem=send.at[n + k],
                                         recv_sem=recv.at[n + k], device_id=(x, y, c),
                                         device_id_type=MESH).wait_recv()
            cp.wait_send()
        for _, finish in big:
            finish()

    halves = [jax.ShapeDtypeStruct((g.shape[0], g.shape[1] // 2, g.shape[2]), g.dtype) for g in grads]
    return pl.pallas_call(
        body, name="pair_exchange", in_specs=[HBM] * n + [VMEM], out_specs=[HBM] * n + [VMEM],
        out_shape=halves + [jax.ShapeDtypeStruct((N_DEV,) + small.shape, small.dtype)],
        scratch_shapes=[pltpu.SemaphoreType.DMA((n + N_DEV - 1,)), pltpu.SemaphoreType.DMA((n + N_DEV - 1,))],
    )(*grads, small)


def _chip_exchange(parts):
    n = len(parts)

    def body(*refs):
        pin, pout = refs[:n], refs[n:2 * n]
        send, recv = refs[2 * n:]
        plans = [_chip_send_plan(pin[a], pout[a], send.at[pl.ds(3 * a, 3)], recv.at[pl.ds(3 * a, 3)]) for a in range(n)]
        for stage in range(2):
            for plan in plans:
                plan[stage]()

    return pl.pallas_call(
        body, name="chip_exchange", in_specs=[HBM] * n, out_specs=[HBM] * n,
        out_shape=[jax.ShapeDtypeStruct(s.shape, s.dtype) for s in parts],
        scratch_shapes=[pltpu.SemaphoreType.DMA((3 * n,)), pltpu.SemaphoreType.DMA((3 * n,))],
    )(*parts)


def _pair_swap(halves):
    n = len(halves)

    def body(*refs):
        hin, hout = refs[:n], refs[n:2 * n]
        send, recv = refs[2 * n:]
        x, y, c = _place()
        swaps = [pltpu.make_async_remote_copy(src_ref=hin[a], dst_ref=hout[a], send_sem=send.at[a],
                                              recv_sem=recv.at[a], device_id=(x, y, 1 - c), device_id_type=MESH)
                 for a in range(n)]
        for cp in swaps:
            cp.start()
        for cp in swaps:
            cp.wait()

    return pl.pallas_call(
        body, name="pair_swap", in_specs=[HBM] * n, out_specs=[HBM] * n,
        out_shape=[jax.ShapeDtypeStruct(s.shape, s.dtype) for s in halves],
        scratch_shapes=[pltpu.SemaphoreType.DMA((n,)), pltpu.SemaphoreType.DMA((n,))],
    )(*halves)


def _pair_sum(grad, recv, c, tag):
    _, R, C = grad.shape
    hr = R // 2

    def body(c_ref, a_ref, b_ref, o_ref):
        o_ref[...] = (a_ref[...] + b_ref[...]).astype(BF16)

    return pl.pallas_call(
        body, name="pair_sum_" + tag,
        grid_spec=pltpu.PrefetchScalarGridSpec(
            num_scalar_prefetch=1, grid=(N_CHIPS,),
            in_specs=[pl.BlockSpec((1, hr, C), lambda s, cr: (s, cr[0], 0)),
                      pl.BlockSpec((1, hr, C), lambda s, cr: (s, 0, 0))],
            out_specs=pl.BlockSpec((1, hr, C), lambda s, cr: (s, 0, 0))),
        out_shape=jax.ShapeDtypeStruct((N_CHIPS, hr, C), BF16),
    )(c, grad, recv)


def _chip_sum(received, own, chip, tag):
    _, rows, C = received.shape
    tr = rows // 2

    def body(chip_ref, own_ref, r1_ref, r2_ref, r3_ref, o_ref):
        p = [r[0].astype(F32) for r in (own_ref, r1_ref, r2_ref, r3_ref)]
        o_ref[...] = (p[0] + p[1]) + (p[2] + p[3])

    def slot(k):
        return pl.BlockSpec((1, tr, C), lambda i, cr: (jnp.bitwise_xor(cr[0], k), i, 0))

    return pl.pallas_call(
        body, name="chip_sum_" + tag,
        grid_spec=pltpu.PrefetchScalarGridSpec(
            num_scalar_prefetch=1, grid=(rows // tr,), in_specs=[slot(0), slot(1), slot(2), slot(3)],
            out_specs=pl.BlockSpec((tr, C), lambda i, cr: (i, 0))),
        out_shape=jax.ShapeDtypeStruct((rows, C), F32),
    )(chip, own, received, received, received)


def _adamw_math(w, g, m, v):
    m = ADAM_B1 * m + (1.0 - ADAM_B1) * g
    v = ADAM_B2 * v + (1.0 - ADAM_B2) * (g * g)
    m_hat = m / (1.0 - ADAM_B1 ** ADAM_STEP)
    v_hat = v / (1.0 - ADAM_B2 ** ADAM_STEP)
    delta = -ADAM_LR * (m_hat / (jnp.sqrt(v_hat) + ADAM_EPS) + ADAM_WD * w)
    return delta, m, v


def _adamw(w, g_mine, g_other, m, v, c, tag):
    R, C = w.shape
    tr = R // 4

    def body(c_ref, w_ref, gm_ref, go_ref, m_ref, v_ref, g_ref, d_ref, nm_ref, nv_ref):
        g = jnp.where(pl.program_id(0) == c_ref[0], gm_ref[...], go_ref[...])
        g_ref[...] = g
        d_ref[...], nm_ref[...], nv_ref[...] = _adamw_math(w_ref[...], g, m_ref[...], v_ref[...])

    blk = pl.BlockSpec((tr, C), lambda h, i, cr: (2 * h + i, 0))
    half = pl.BlockSpec((tr, C), lambda h, i, cr: (i, 0))
    return pl.pallas_call(
        body, name="adamw_" + tag,
        grid_spec=pltpu.PrefetchScalarGridSpec(
            num_scalar_prefetch=1, grid=(2, 2), in_specs=[blk, half, half, blk, blk], out_specs=[blk] * 4),
        out_shape=[jax.ShapeDtypeStruct((R, C), F32)] * 4,
    )(c, w, g_mine, g_other, m, v)


def _small_update(all_small, w, m, v):
    def body(a_ref, w_ref, m_ref, v_ref, g_ref, d_ref, nm_ref, nv_ref):
        g = ((a_ref[0] + a_ref[1]) + (a_ref[2] + a_ref[3])) + ((a_ref[4] + a_ref[5]) + (a_ref[6] + a_ref[7]))
        g_ref[...] = g
        d_ref[...], nm_ref[...], nv_ref[...] = _adamw_math(w_ref[...], g, m_ref[...], v_ref[...])

    return pl.pallas_call(
        body, name="small_update", out_shape=[jax.ShapeDtypeStruct(w.shape, F32)] * 4,
    )(all_small, w, m, v)


SMALL_ROWS = (("attn", 0, 0), ("ffn", 1, 0), ("dil", 2, 0), ("sb", 2, GROUP), ("q", 3, 0), ("k", 3, HEAD_DIM),
              ("loss", 4, 0))


def _pack_small(vals, D):
    rows = [jnp.zeros((1, D), F32) for _ in range(8)]
    for name, r, off in SMALL_ROWS:
        if name in vals:
            rows[r] = lax.dynamic_update_slice(rows[r], vals[name].astype(F32), (0, off))
    return jnp.concatenate(rows, axis=0)


def _unpack_small(packed, vals):
    return {name: packed[r:r + 1, off:off + vals[name].shape[1]] for name, r, off in SMALL_ROWS if name in vals}


def kernel(x, attn_norm_w, w_in, q_norm_w, k_norm_w, dil_out_norm_w, sb_out_norm_w, w_out, ffn_norm_w, w_gate, w_up, w_down, loss_target, m_attn_norm_w, m_w_in, m_q_norm_w, m_k_norm_w, m_dil_out_norm_w, m_sb_out_norm_w, m_w_out, m_ffn_norm_w, m_w_gate, m_w_up, m_w_down, v_attn_norm_w, v_w_in, v_q_norm_w, v_k_norm_w, v_dil_out_norm_w, v_sb_out_norm_w, v_w_out, v_ffn_norm_w, v_w_gate, v_w_up, v_w_down):
    D = x.shape[-1]
    big_names = ("w_in", "w_out", "w_gate", "w_up", "w_down")
    flipped = ("w_gate", "w_up")
    tr = lambda a: jnp.swapaxes(a[0], 0, 1)
    big_w = dict(w_in=w_in[0], w_out=w_out[0], w_gate=tr(w_gate), w_up=tr(w_up), w_down=w_down[0])
    big_m = dict(w_in=m_w_in[0], w_out=m_w_out[0], w_gate=tr(m_w_gate), w_up=tr(m_w_up), w_down=m_w_down[0])
    big_v = dict(w_in=v_w_in[0], w_out=v_w_out[0], w_gate=tr(v_w_gate), w_up=tr(v_w_up), w_down=v_w_down[0])
    small_w = dict(attn=attn_norm_w, q=q_norm_w, k=k_norm_w, dil=dil_out_norm_w, sb=sb_out_norm_w, ffn=ffn_norm_w)
    small_m = dict(attn=m_attn_norm_w, q=m_q_norm_w, k=m_k_norm_w, dil=m_dil_out_norm_w, sb=m_sb_out_norm_w,
                   ffn=m_ffn_norm_w)
    small_v = dict(attn=v_attn_norm_w, q=v_q_norm_w, k=v_k_norm_w, dil=v_dil_out_norm_w, sb=v_sb_out_norm_w,
                   ffn=v_ffn_norm_w)

    c = lax.axis_index("c").astype(jnp.int32).reshape(1)
    chip = (2 * lax.axis_index("x") + lax.axis_index("y")).astype(jnp.int32).reshape(1)
    (w_in_g,) = _gather_weights([big_w["w_in"].astype(BF16)])
    w_out_slots = _own_slots(big_w["w_out"].astype(BF16))
    ffn_slots = [_own_slots(big_w[n].astype(BF16)) for n in FFN_NAMES]

    loss_parts, grad_x, small_g, dw_in, dw_out, ffn_halves = _device_step(
        x[0], loss_target[0], attn_norm_w, q_norm_w, k_norm_w, dil_out_norm_w, sb_out_norm_w, ffn_norm_w,
        w_in_g, w_out_slots, ffn_slots, c, chip)
    small_g["loss"] = (jnp.sum(loss_parts[:, 0, 0]) * (0.5 / D)).reshape(1, 1)

    late = [dw_in, dw_out.reshape(N_CHIPS, -1, D)]
    *from_pair, all_small = _pair_exchange(late, _pack_small(small_g, D))
    chip_parts = [_pair_sum(g, r, c, n) for g, r, n in zip(late, from_pair, big_names)]
    from_chips = _chip_exchange(chip_parts)
    halves = [_chip_sum(r, p, chip, n) for r, p, n in zip(from_chips, chip_parts, big_names)] + ffn_halves
    others = _pair_swap(halves)
    big_out = {n: _adamw(big_w[n], mine, other, big_m[n], big_v[n], c, n)
               for n, mine, other in zip(big_names, halves, others)}
    sg, sd, sm, sv = _small_update(all_small, _pack_small(small_w, D), _pack_small(small_m, D),
                                   _pack_small(small_v, D))
    small_out = [_unpack_small(t, small_w) for t in (sg, sd, sm, sv)]

    order = (("attn", None), (None, "w_in"), ("q", None), ("k", None), ("dil", None), ("sb", None),
             (None, "w_out"), ("ffn", None), (None, "w_gate"), (None, "w_up"), (None, "w_down"))
    outs = [sg[4, 0], grad_x[None]]
    for kind in range(4):
        for s_name, b_name in order:
            if s_name is not None:
                outs.append(small_out[kind][s_name])
            else:
                res = big_out[b_name][kind]
                outs.append((jnp.swapaxes(res, 0, 1) if b_name in flipped else res)[None])
    return tuple(outs)
```

```python
import functools

import jax
import jax.numpy as jnp
from jax import lax
from jax.experimental import pallas as pl
from jax.experimental.pallas import tpu as pltpu

F32 = jnp.float32
BF16 = jnp.bfloat16
MESH = pl.DeviceIdType.MESH

HEAD_DIM = 64
GROUP = 512
BLOCK = 128
LANES = 128
N_CHIPS = 4
N_DEV = 8
EPS = 1e-6
ROPE_THETA = 10000.0
DILATIONS = (1, 4, 16)
NEG = -1e30

ADAM_LR = 0.001
ADAM_B1 = 0.9
ADAM_B2 = 0.999
ADAM_EPS = 1e-08
ADAM_WD = 0.01
ADAM_STEP = 10


def _dot(a, b):
    return jnp.dot(a, b, preferred_element_type=F32)


def _dot_nt(a, b):
    return lax.dot_general(a, b, (((1,), (1,)), ((), ())), preferred_element_type=F32)


def _dot_tn(a, b):
    return lax.dot_general(a, b, (((0,), (0,)), ((), ())), preferred_element_type=F32)


def _split(v):
    hi = lax.bitcast_convert_type(lax.bitcast_convert_type(v, jnp.uint32) & jnp.uint32(0xFFFF0000), F32)
    return hi.astype(BF16), (v - hi).astype(BF16)


def _segsum(v, g):
    hi, lo = _split(v)
    return _dot(hi, g) + _dot(lo, g)


def _rot_half(x):
    outs = []
    for c in range(x.shape[1] // LANES):
        xc = x[:, c * LANES:(c + 1) * LANES]
        lane = lax.broadcasted_iota(jnp.int32, xc.shape, 1)
        first = (lane % HEAD_DIM) < (HEAD_DIM // 2)
        outs.append(jnp.where(first, pltpu.roll(xc, LANES - 32, 1), pltpu.roll(xc, 32, 1)))
    return outs[0] if len(outs) == 1 else jnp.concatenate(outs, axis=1)


def _rms(x):
    return lax.rsqrt(jnp.mean(x * x, axis=-1, keepdims=True) + EPS)


def _rms_bwd(dy, x, w):
    rstd = _rms(x)
    xh = x * rstd
    dxh = dy * w
    dx = rstd * (dxh - xh * jnp.mean(dxh * xh, axis=-1, keepdims=True))
    return dx, dy * xh


def _sigmoid(x):
    return 1.0 / (1.0 + jnp.exp(-x))


def _sum4(ref):
    p = [ref[j].astype(F32) for j in range(N_CHIPS)]
    return (p[0] + p[1]) + (p[2] + p[3])


def _full(shape):
    n = len(shape)
    return pl.BlockSpec(shape, lambda *_: (0,) * n)


def _strided_spec(tm, r):
    return pl.BlockSpec((r, tm // r, GROUP), lambda i: (0, i, 0))


def _strided_shape(S, r, dtype):
    return jax.ShapeDtypeStruct((r, S // r, GROUP), dtype)


def _to_strided(scr, val, outs):
    chunks = range(GROUP // LANES)
    for k in chunks:
        scr[k] = val[:, _lanes(k)]
    for r, o_ref in outs:
        if r == 1:
            o_ref[0] = val.astype(o_ref.dtype)
            continue
        n = val.shape[0] // r
        for c in range(r):
            rows = pl.ds(c, n, stride=r)
            o_ref[c] = jnp.concatenate([scr.at[k][rows, :] for k in chunks], axis=1).astype(o_ref.dtype)


def _from_strided(scr, ref):
    r, n, _ = ref.shape
    if r == 1:
        return ref[0].astype(F32)
    chunks = range(GROUP // LANES)
    for c in range(r):
        plane = ref[c].astype(F32)
        for k in chunks:
            scr.at[k][pl.ds(c, n, stride=r), :] = plane[:, _lanes(k)]
    return jnp.concatenate([scr[k] for k in chunks], axis=1)


def _strided_scratch(tm):
    return pltpu.VMEM((GROUP // LANES, tm, LANES), F32)


def _tile4(t):
    return jnp.concatenate([t] * (GROUP // LANES), axis=1)


def _in_proj_fwd(x, attn_w, w_in_g, qw, kw, cos_t, sin_t, seg_ones):
    S, D = x.shape
    tm = 512
    wcols = w_in_g.shape[2]
    nd = len(DILATIONS)

    def body(x_ref, aw_ref, w_ref, qw_ref, kw_ref, cos_ref, sin_ref, g_ref, h_ref, qa_ref, ka_ref, *rest):
        q_refs, k_refs, v_refs = rest[:nd], rest[nd:2 * nd], rest[2 * nd:3 * nd]
        qs_ref, ks_ref, vs_ref, scr = rest[3 * nd:]
        xv = x_ref[...]
        h = (xv * _rms(xv) * aw_ref[...]).astype(BF16)
        h_ref[...] = h
        proj = jnp.concatenate([_dot(h, w_ref[j]) for j in range(N_CHIPS)], axis=1)
        qa = proj[:, 0 * GROUP:1 * GROUP]
        ka = proj[:, 1 * GROUP:2 * GROUP]
        qa_ref[...] = qa
        ka_ref[...] = ka
        _to_strided(scr, proj[:, 2 * GROUP:3 * GROUP], list(zip(DILATIONS, v_refs)))
        qs_ref[...] = proj[:, 3 * GROUP:4 * GROUP].astype(BF16)
        ks_ref[...] = proj[:, 4 * GROUP:5 * GROUP].astype(BF16)
        vs_ref[...] = proj[:, 5 * GROUP:6 * GROUP].astype(BF16)
        g = g_ref[...]
        cos = _tile4(cos_ref[...])
        sin = _tile4(sin_ref[...])
        for t, w_r, o_rs in ((qa, qw_ref, q_refs), (ka, kw_ref, k_refs)):
            rstd = lax.rsqrt(_segsum(t * t, g) * (1.0 / HEAD_DIM) + EPS)
            tn = t * rstd * w_r[...]
            _to_strided(scr, tn * cos + _rot_half(tn) * sin, list(zip(DILATIONS, o_rs)))

    row = lambda i: (i, 0)
    tile = lambda n, dt: jax.ShapeDtypeStruct((S, n), dt)
    planes = [_strided_spec(tm, r) for r in DILATIONS]
    return pl.pallas_call(
        body, name="in_proj_fwd", grid=(S // tm,),
        in_specs=[pl.BlockSpec((tm, D), row), _full((1, D)), _full((N_CHIPS, D, wcols)),
                  _full((1, GROUP)), _full((1, GROUP)),
                  pl.BlockSpec((tm, LANES), row), pl.BlockSpec((tm, LANES), row),
                  _full((GROUP, GROUP))],
        out_specs=[pl.BlockSpec((tm, D), row)] + [pl.BlockSpec((tm, GROUP), row)] * 2 + planes * 3
                  + [pl.BlockSpec((tm, GROUP), row)] * 3,
        out_shape=[tile(D, BF16), tile(GROUP, F32), tile(GROUP, F32)]
                  + [_strided_shape(S, r, BF16) for r in DILATIONS] * 3 + [tile(GROUP, BF16)] * 3,
        scratch_shapes=[_strided_scratch(tm)],
    )(x, attn_w, w_in_g, qw, kw, cos_t, sin_t, seg_ones)


DIL_PLANES = 1


def _dil_fwd(q, k, v, slots):
    r, L, _ = q.shape
    nb = L // BLOCK
    P = GROUP // LANES
    PL = min(DIL_PLANES, r)
    ns = len(slots)
    units = [(pp, hp) for pp in range(PL) for hp in range(P)]

    def body(q_ref, kc_ref, kp_ref, vc_ref, vp_ref, *rest):
        o_ref, l_ref = rest[ns:ns + 2]
        n = pl.program_id(1)
        finish = (_hosted_gathers(rest[:ns], rest[ns + 2:2 * ns + 2], *rest[2 * ns + 2:],
                                  pl.program_id(0) * nb + n, (r // PL) * nb) if ns else None)
        rowi = lax.broadcasted_iota(jnp.int32, (BLOCK, BLOCK), 0)
        coli = lax.broadcasted_iota(jnp.int32, (BLOCK, BLOCK), 1)
        first = coli < HEAD_DIM
        masks = (coli <= rowi, jnp.logical_and(coli >= rowi, n > 0))
        s2 = {}
        for pp, hp in units:
            q2 = _scaled(q_ref[pp, :, _lanes(hp)])
            for b, k_ref in enumerate((kc_ref, kp_ref)):
                s2[pp, hp, b] = _dot_nt(q2, _by_head(k_ref[pp, :, _lanes(hp)], first))
        ps, inv, lse = {}, {}, {}
        for pp, hp in units:
            for h in range(2):
                s = [jnp.where(masks[b], s2[pp, hp, b][:, h * BLOCK:(h + 1) * BLOCK], NEG) for b in range(2)]
                m = jnp.maximum(jnp.max(s[0], axis=1, keepdims=True), jnp.max(s[1], axis=1, keepdims=True))
                p = [jnp.exp(s[b] - m) for b in range(2)]
                den = jnp.sum(p[0], axis=1, keepdims=True) + jnp.sum(p[1], axis=1, keepdims=True)
                ps[pp, hp, h] = [p[b].astype(BF16) for b in range(2)]
                inv[pp, hp, h] = 1.0 / den
                lse[pp, hp, h] = m + jnp.log(den)
        for pp, hp in units:
            o = jnp.zeros((BLOCK, LANES), F32)
            for b, v_ref in enumerate((vc_ref, vp_ref)):
                o = o + _dot(jnp.concatenate([ps[pp, hp, 0][b], ps[pp, hp, 1][b]], axis=1),
                             _by_head(v_ref[pp, :, _lanes(hp)], first))
            o_ref[pp, :, _lanes(hp)] = o * jnp.where(first, inv[pp, hp, 0], inv[pp, hp, 1])
            l_ref[pp, :, _lanes(hp)] = jnp.where(first, lse[pp, hp, 0], lse[pp, hp, 1])
        if finish is not None:
            finish()

    cur = pl.BlockSpec((PL, BLOCK, GROUP), lambda c, n: (c, n, 0))
    prev = pl.BlockSpec((PL, BLOCK, GROUP), lambda c, n: (c, jnp.maximum(n - 1, 0), 0))
    h_in, h_out, h_shape, h_sems = _hosted_specs(slots)
    return pl.pallas_call(
        body, name="dil_fwd_r%d" % r, grid=(r // PL, nb),
        in_specs=[cur, cur, prev, cur, prev] + h_in, out_specs=[cur, cur] + h_out,
        out_shape=[jax.ShapeDtypeStruct(q.shape, F32)] * 2 + h_shape,
        input_output_aliases={5 + a: 2 + a for a in range(ns)},
        scratch_shapes=h_sems,
    )(q, k, k, v, v, *slots)


def _dil_bwd(q, k, v, do, lse, delta, part):
    r, L, _ = q.shape
    nb = L // BLOCK
    P = GROUP // LANES
    PL = min(DIL_PLANES, r)
    scale = HEAD_DIM ** -0.5
    units = [(pp, hp) for pp in range(PL) for hp in range(P)]

    def body(qc_ref, qn_ref, doc_ref, don_ref, lc_ref, ln_ref, dc_ref, dn_ref, k_ref, v_ref, part_in,
             dq_ref, dk_ref, dv_ref, part_out, carry, send, recv):
        j = pl.program_id(1)
        step = pl.program_id(0) * nb + j
        start, finish = _chip_send_plan(part_in, part_out, send, recv)
        pl.when(step == 0)(start)
        rowi = lax.broadcasted_iota(jnp.int32, (BLOCK, BLOCK), 0)
        coli = lax.broadcasted_iota(jnp.int32, (BLOCK, BLOCK), 1)
        first = coli < HEAD_DIM
        sides = ((qc_ref, doc_ref, lc_ref, dc_ref, coli <= rowi),
                 (qn_ref, don_ref, ln_ref, dn_ref, jnp.logical_and(coli >= rowi, j < nb - 1)))

        @pl.when(j == 0)
        def _():
            carry[...] = jnp.zeros_like(carry)

        kcat, q2, do2, s2, dp2 = {}, {}, {}, {}, {}
        for pp, hp in units:
            kcat[pp, hp] = _by_head(k_ref[pp, :, _lanes(hp)], first)
            vcat = _by_head(v_ref[pp, :, _lanes(hp)], first)
            for x, (q_r, do_r, _, _, _) in enumerate(sides):
                q2[pp, hp, x] = _scaled(q_r[pp, :, _lanes(hp)])
                do2[pp, hp, x] = do_r[pp, :, _lanes(hp)]
                s2[pp, hp, x] = _dot_nt(q2[pp, hp, x], kcat[pp, hp])
                dp2[pp, hp, x] = _dot_nt(do2[pp, hp, x], vcat)
        pcat, dscat = {}, {}
        for pp, hp in units:
            for x, (_, _, l_r, d_r, msk) in enumerate(sides):
                ps, dss = [], []
                for h in range(2):
                    col = hp * LANES + h * HEAD_DIM
                    half = slice(h * BLOCK, (h + 1) * BLOCK)
                    p = jnp.where(msk, jnp.exp(s2[pp, hp, x][:, half] - l_r[pp, :, col:col + 1]), 0.0)
                    ps.append(p.astype(BF16))
                    dss.append((p * (dp2[pp, hp, x][:, half] - d_r[pp, :, col:col + 1])).astype(BF16))
                pcat[pp, hp, x] = jnp.concatenate(ps, axis=1)
                dscat[pp, hp, x] = jnp.concatenate(dss, axis=1)
        for pp, hp in units:
            dv2 = _dot_tn(pcat[pp, hp, 0], do2[pp, hp, 0]) + _dot_tn(pcat[pp, hp, 1], do2[pp, hp, 1])
            dk2 = _dot_tn(dscat[pp, hp, 0], q2[pp, hp, 0]) + _dot_tn(dscat[pp, hp, 1], q2[pp, hp, 1])
            dv_ref[pp, :, _lanes(hp)] = jnp.where(first, dv2[:BLOCK], dv2[BLOCK:]).astype(BF16)
            dk_ref[pp, :, _lanes(hp)] = jnp.where(first, dk2[:BLOCK], dk2[BLOCK:]).astype(BF16)
            dq_ref[pp, :, _lanes(hp)] = (carry[pp, :, _lanes(hp)]
                                         + _dot(dscat[pp, hp, 0], kcat[pp, hp]) * scale).astype(BF16)
            carry[pp, :, _lanes(hp)] = _dot(dscat[pp, hp, 1], kcat[pp, hp]) * scale
        pl.when(step == (r // PL) * nb - 1)(finish)

    cur = pl.BlockSpec((PL, BLOCK, GROUP), lambda c, n: (c, n, 0))
    nxt = pl.BlockSpec((PL, BLOCK, GROUP), lambda c, n: (c, jnp.minimum(n + 1, nb - 1), 0))
    return pl.pallas_call(
        body, name="dil_bwd_r%d" % r, grid=(r // PL, nb),
        in_specs=[cur, nxt, cur, nxt, cur, nxt, cur, nxt, cur, cur, HBM], out_specs=[cur, cur, cur, HBM],
        out_shape=[jax.ShapeDtypeStruct(q.shape, BF16)] * 3 + [jax.ShapeDtypeStruct(part.shape, part.dtype)],
        scratch_shapes=[pltpu.VMEM((PL, BLOCK, GROUP), F32), pltpu.SemaphoreType.DMA((3,)),
                        pltpu.SemaphoreType.DMA((3,))],
    )(q, q, do, do, lse, lse, delta, delta, k, v, part)


SB_TILES = 2
SB_PAIRS_FWD = 4
SB_PAIRS_BWD = 2
SB_DEAD = -110.0


def _lanes(hp):
    return slice(hp * LANES, (hp + 1) * LANES)


def _sb_logits(z, valid):
    e = jnp.exp(-jnp.abs(z))
    lb = jnp.minimum(z, 0.0) - jnp.log(1.0 + e)
    lk = lb - z
    if valid is not None:
        lk = jnp.where(valid, lk, 0.0)
    return e, lb, lk


def _by_head(t, first):
    zero = jnp.zeros_like(t)
    return jnp.concatenate([jnp.where(first, t, zero), jnp.where(first, zero, t)], axis=0)


def _sb_valid(i, j):
    rowi = lax.broadcasted_iota(jnp.int32, (BLOCK, BLOCK), 0)
    coli = lax.broadcasted_iota(jnp.int32, (BLOCK, BLOCK), 1)
    return (coli - rowi) < (i - j) * BLOCK


def _scaled(q):
    return (q.astype(F32) * (HEAD_DIM ** -0.5)).astype(BF16)


def _hosted_gathers(refs_in, refs_out, send, recv, step, steps):
    plans = [_gather_plan(refs_in[a], refs_out[a], send.at[pl.ds(6 * a, 6)], recv.at[pl.ds(6 * a, 6)])
             for a in range(len(refs_in))]
    for stage, at in ((0, 0), (1, (2 * steps) // 3)):
        @pl.when(step == at)
        def _():
            for plan in plans:
                plan[stage]()

    def finish():
        @pl.when(step == steps - 1)
        def _():
            for plan in plans:
                plan[2]()

    return finish


def _hosted_specs(slots):
    n = len(slots)
    sems = [pltpu.SemaphoreType.DMA((6 * n,))] * 2 if n else []
    return [HBM] * n, [HBM] * n, [jax.ShapeDtypeStruct(s.shape, s.dtype) for s in slots], sems


def _sb_fwd(qs, ks, vs, tri_later, slots):
    S = qs.shape[0]
    P = SB_PAIRS_FWD
    W = P * LANES
    ns = len(slots)

    def body(q_ref, k_ref, v_ref, u_ref, *rest):
        o_ref, lt_ref, from_ref = rest[ns:ns + 3]
        i = pl.program_id(1)
        finish = _hosted_gathers(rest[:ns], rest[ns + 3:2 * ns + 3], *rest[2 * ns + 3:], i, S // BLOCK) if ns else None
        first = lax.broadcasted_iota(jnp.int32, (BLOCK, LANES), 1) < HEAD_DIM
        q2 = [_scaled(q_ref[:, _lanes(hp)]) for hp in range(P)]

        def chunk(ci, carry, masked):
            runs, accs = list(carry[0]), list(carry[1])
            units = [(t, hp) for t in reversed(range(SB_TILES)) for hp in range(P)]
            z2s, lbs, c2s = {}, {}, {}
            for t, hp in units:
                off = pl.multiple_of((ci * SB_TILES + t) * BLOCK, BLOCK)
                z2s[t, hp] = _dot_nt(q2[hp], _by_head(k_ref[pl.ds(off, BLOCK), _lanes(hp)], first))
            for t, hp in units:
                valid = _sb_valid(i, ci * SB_TILES + t) if masked else None
                for h in range(2):
                    _, lb, lk = _sb_logits(z2s[t, hp][:, h * BLOCK:(h + 1) * BLOCK], valid)
                    lbs[t, hp, h] = lb
                    c2s[t, hp, h] = _dot(jnp.concatenate(_split(lk), axis=1), u_ref[...])
            for t, hp in units:
                off = pl.multiple_of((ci * SB_TILES + t) * BLOCK, BLOCK)
                valid = _sb_valid(i, ci * SB_TILES + t) if masked else None
                a2 = []
                for h in range(2):
                    a = jnp.exp(lbs[t, hp, h] + c2s[t, hp, h][:, :BLOCK] + runs[2 * hp + h])
                    if masked:
                        a = jnp.where(valid, a, 0.0)
                    a2.append(a.astype(BF16))
                    runs[2 * hp + h] = runs[2 * hp + h] + c2s[t, hp, h][:, BLOCK:]
                vcat = _by_head(v_ref[pl.ds(off, BLOCK), _lanes(hp)], first)
                accs[hp] = accs[hp] + _dot(jnp.concatenate(a2, axis=1), vcat)
            return tuple(runs), tuple(accs)

        def alive(runs):
            top = functools.reduce(jnp.maximum, runs)
            return (jnp.max(top) > SB_DEAD).astype(jnp.int32)

        def step(c):
            t, _, runs, accs = c
            runs, accs = chunk(nfull - 1 - t, (runs, accs), False)
            return t + 1, alive(runs), runs, accs

        zero = jnp.zeros((BLOCK, LANES), F32)
        nfull = i // SB_TILES
        runs, accs = chunk(nfull, ((zero,) * (2 * P), (zero,) * P), True)
        done, _, runs, accs = lax.while_loop(lambda c: jnp.logical_and(c[0] < nfull, c[1] > 0), step,
                                             (jnp.int32(0), alive(runs), runs, accs))
        for hp in range(P):
            o_ref[:, _lanes(hp)] = accs[hp]
            lt_ref[:, _lanes(hp)] = jnp.where(first, runs[2 * hp], runs[2 * hp + 1])
        from_ref[...] = jnp.full(from_ref.shape, nfull - done, jnp.int32)
        if finish is not None:
            finish()

    assert W == GROUP
    blk = pl.BlockSpec((BLOCK, W), lambda hp, i: (i, hp))
    col = pl.BlockSpec((S, W), lambda hp, i: (0, hp))
    h_in, h_out, h_shape, h_sems = _hosted_specs(slots)
    return pl.pallas_call(
        body, name="sb_fwd", grid=(GROUP // W, S // BLOCK),
        in_specs=[blk, col, col, _full((2 * BLOCK, 2 * BLOCK))] + h_in,
        out_specs=[blk, blk, pl.BlockSpec((1, 8, LANES), lambda hp, i: (i, 0, 0))] + h_out,
        out_shape=[jax.ShapeDtypeStruct((S, GROUP), F32)] * 2
                  + [jax.ShapeDtypeStruct((S // BLOCK, 8, LANES), jnp.int32)] + h_shape,
        input_output_aliases={4 + a: 3 + a for a in range(ns)},
        scratch_shapes=h_sems,
    )(qs, ks, vs, tri_later, *slots)


def _sb_bwd(first_chunk, qs, ks, vs, do, ltot, tri_upto, tri_before):
    S = qs.shape[0]
    P = SB_PAIRS_BWD
    W = P * LANES

    def body(from_ref, q_ref, k_ref, v_ref, do_ref, lt_ref, w_ref, x_ref, dq_ref, dk_ref, dv_ref):
        i = pl.program_id(1)

        @pl.when(i == 0)
        def _():
            dk_ref[...] = jnp.zeros_like(dk_ref)
            dv_ref[...] = jnp.zeros_like(dv_ref)

        first = lax.broadcasted_iota(jnp.int32, (BLOCK, LANES), 1) < HEAD_DIM
        q2 = [_scaled(q_ref[:, _lanes(hp)]) for hp in range(P)]
        do2 = [do_ref[:, _lanes(hp)] for hp in range(P)]
        totals = [jnp.broadcast_to(lt_ref[:, n * HEAD_DIM:n * HEAD_DIM + 1], (BLOCK, LANES)) for n in range(2 * P)]

        def chunk(ci, carry, masked):
            keeps, grads, dqs = list(carry[0]), list(carry[1]), list(carry[2])
            units = [(t, hp) for t in range(SB_TILES) for hp in range(P)]
            offs = [pl.multiple_of((ci * SB_TILES + t) * BLOCK, BLOCK) for t in range(SB_TILES)]
            valids = [_sb_valid(i, ci * SB_TILES + t) if masked else None for t in range(SB_TILES)]
            kcat, z2, da2, es, lbs, c2s, as_, des, p2s = {}, {}, {}, {}, {}, {}, {}, {}, {}
            for t, hp in units:
                kcat[t, hp] = _by_head(k_ref[pl.ds(offs[t], BLOCK), _lanes(hp)], first)
                z2[t, hp] = _dot_nt(q2[hp], kcat[t, hp])
                da2[t, hp] = _dot_nt(do2[hp], _by_head(v_ref[pl.ds(offs[t], BLOCK), _lanes(hp)], first))
            for t, hp in units:
                for h in range(2):
                    es[t, hp, h], lbs[t, hp, h], lk = _sb_logits(z2[t, hp][:, h * BLOCK:(h + 1) * BLOCK], valids[t])
                    c2s[t, hp, h] = _dot(jnp.concatenate(_split(lk), axis=1), w_ref[...])
            for t, hp in units:
                for h in range(2):
                    n = 2 * hp + h
                    a = jnp.exp(lbs[t, hp, h] + (totals[n] - (keeps[n] + c2s[t, hp, h][:, :BLOCK])))
                    if masked:
                        a = jnp.where(valids[t], a, 0.0)
                    keeps[n] = keeps[n] + c2s[t, hp, h][:, BLOCK:]
                    de = a * da2[t, hp][:, h * BLOCK:(h + 1) * BLOCK]
                    as_[t, hp, h], des[t, hp, h] = a.astype(BF16), de
                    p2s[t, hp, h] = _dot(jnp.concatenate(_split(de), axis=1), x_ref[...])
            for t, hp in units:
                dz2 = []
                for h in range(2):
                    n = 2 * hp + h
                    e = es[t, hp, h]
                    sig = jnp.where(z2[t, hp][:, h * BLOCK:(h + 1) * BLOCK] >= 0.0, 1.0, e) / (1.0 + e)
                    dz = des[t, hp, h] * (1.0 - sig) - (grads[n] + p2s[t, hp, h][:, :BLOCK]) * sig
                    if masked:
                        dz = jnp.where(valids[t], dz, 0.0)
                    grads[n] = grads[n] + p2s[t, hp, h][:, BLOCK:]
                    dz2.append(dz.astype(BF16))
                dzcat = jnp.concatenate(dz2, axis=1)
                dk2 = _dot_tn(dzcat, q2[hp])
                dv2 = _dot_tn(jnp.concatenate([as_[t, hp, 0], as_[t, hp, 1]], axis=1), do2[hp])
                dk_ref[pl.ds(offs[t], BLOCK), _lanes(hp)] += jnp.where(first, dk2[:BLOCK], dk2[BLOCK:])
                dv_ref[pl.ds(offs[t], BLOCK), _lanes(hp)] += jnp.where(first, dv2[:BLOCK], dv2[BLOCK:])
                dqs[hp] = dqs[hp] + _dot(dzcat, kcat[t, hp])
            return tuple(keeps), tuple(grads), tuple(dqs)

        zero = jnp.zeros((BLOCK, LANES), F32)
        nfull = i // SB_TILES
        carry = lax.fori_loop(from_ref[i], nfull, lambda t, c: chunk(t, c, False),
                              ((zero,) * (2 * P), (zero,) * (2 * P), (zero,) * P))
        carry = chunk(nfull, carry, True)
        for hp in range(P):
            dq_ref[:, _lanes(hp)] = carry[2][hp] * (HEAD_DIM ** -0.5)

    blk = pl.BlockSpec((BLOCK, W), lambda hp, i, fr: (i, hp))
    col = pl.BlockSpec((S, W), lambda hp, i, fr: (0, hp))
    tri = pl.BlockSpec((2 * BLOCK, 2 * BLOCK), lambda hp, i, fr: (0, 0))
    return pl.pallas_call(
        body, name="sb_bwd",
        grid_spec=pltpu.PrefetchScalarGridSpec(
            num_scalar_prefetch=1, grid=(GROUP // W, S // BLOCK),
            in_specs=[blk, col, col, blk, blk, tri, tri], out_specs=[blk, col, col]),
        out_shape=[jax.ShapeDtypeStruct((S, GROUP), F32)] * 3,
    )(first_chunk, qs, ks, vs, do, ltot, tri_upto, tri_before)


def _out_proj_fwd(o_br, l_br, o_sb, x, w_dil, w_sbn, w_out_g):
    S, D = x.shape
    tm = 512

    def body(o0, o1, o2, l0, l1, l2, os_ref, x_ref, wd_ref, ws_ref, w_ref, od_ref, s0, s1, s2, x1_ref, scr):
        ls = [_from_strided(scr, l) for l in (l0, l1, l2)]
        os_ = [_from_strided(scr, o) for o in (o0, o1, o2)]
        m = jnp.maximum(jnp.maximum(ls[0], ls[1]), ls[2])
        es = [jnp.exp(l - m) for l in ls]
        den = es[0] + es[1] + es[2]
        od = (es[0] * os_[0] + es[1] * os_[1] + es[2] * os_[2]) / den
        od_ref[...] = od
        _to_strided(scr, m + jnp.log(den), list(zip(DILATIONS, (s0, s1, s2))))
        osb = os_ref[...]
        mixed = jnp.concatenate([(od * _rms(od) * wd_ref[...]).astype(BF16),
                                 (osb * _rms(osb) * ws_ref[...]).astype(BF16)], axis=1)
        x1_ref[...] = x_ref[...] + _dot(mixed, w_ref[...])

    row = lambda i: (i, 0)
    g = pl.BlockSpec((tm, GROUP), row)
    d = pl.BlockSpec((tm, D), row)
    planes = [_strided_spec(tm, r) for r in DILATIONS]
    return pl.pallas_call(
        body, name="out_proj_fwd", grid=(S // tm,),
        in_specs=planes * 2 + [g, d, _full((1, GROUP)), _full((1, GROUP)), _full((2 * GROUP, D))],
        out_specs=[g] + planes + [d],
        out_shape=[jax.ShapeDtypeStruct((S, GROUP), F32)] + [_strided_shape(S, r, F32) for r in DILATIONS]
                  + [jax.ShapeDtypeStruct((S, D), F32)],
        scratch_shapes=[_strided_scratch(tm)],
    )(*o_br, *l_br, o_sb, x, w_dil, w_sbn, w_out_g)


def _ffn_fwd(x1, target, ffn_w, wg_g, wu_g, wd_g):
    S, D = x1.shape
    F = wg_g.shape[1]
    tm = 512
    nt = S // tm

    def body(x_ref, t_ref, nw_ref, wg_ref, wu_ref, wd_ref, h_ref, g_ref, u_ref, dy_ref, loss_ref, h_s, acc):
        j = pl.program_id(1)

        @pl.when(j == 0)
        def _():
            xv = x_ref[...]
            h = (xv * _rms(xv) * nw_ref[...]).astype(BF16)
            h_s[...] = h
            h_ref[...] = h
            acc[...] = xv

        h = h_s[...]
        g = _dot_nt(h, wg_ref[0])
        u = _dot_nt(h, wu_ref[0])
        g_ref[0] = g.astype(BF16)
        u_ref[0] = u.astype(BF16)
        a = (g * _sigmoid(g) * u).astype(BF16)
        acc[...] += _dot(a, wd_ref[0])

        @pl.when(j == N_CHIPS - 1)
        def _():
            err = acc[...] - t_ref[...]
            dy_ref[...] = err * (1.0 / D)
            loss_ref[...] = jnp.full(loss_ref.shape, jnp.sum(err * err), F32)

    row = lambda t, j: (t, 0)
    shard = lambda t, j: (j, 0, 0)
    act = lambda t, j: (j, t, 0)
    return pl.pallas_call(
        body, name="ffn_fwd", grid=(nt, N_CHIPS),
        in_specs=[pl.BlockSpec((tm, D), row), pl.BlockSpec((tm, D), row), pl.BlockSpec((1, D), lambda t, j: (0, 0))]
                 + [pl.BlockSpec((1, F, D), shard)] * 3,
        out_specs=[pl.BlockSpec((tm, D), row), pl.BlockSpec((1, tm, F), act), pl.BlockSpec((1, tm, F), act),
                   pl.BlockSpec((tm, D), row), pl.BlockSpec((1, 8, LANES), lambda t, j: (t, 0, 0))],
        out_shape=[jax.ShapeDtypeStruct((S, D), BF16), jax.ShapeDtypeStruct((N_CHIPS, S, F), BF16),
                   jax.ShapeDtypeStruct((N_CHIPS, S, F), BF16), jax.ShapeDtypeStruct((S, D), F32),
                   jax.ShapeDtypeStruct((nt, 8, LANES), F32)],
        scratch_shapes=[pltpu.VMEM((tm, D), BF16), pltpu.VMEM((tm, D), F32)],
    )(x1, target, ffn_w, wg_g, wu_g, wd_g)


def _ffn_bwd(h2, dy, g, u, wg_g, wu_g, wd_g):
    S, D = dy.shape
    F = wg_g.shape[1]
    tm = 512

    def body(h_ref, dy_ref, g_ref, u_ref, wg_ref, wu_ref, wd_ref, dwg_ref, dwu_ref, dwd_ref, dh_ref):
        t = pl.program_id(1)

        @pl.when(t == 0)
        def _():
            dwg_ref[...] = jnp.zeros_like(dwg_ref)
            dwu_ref[...] = jnp.zeros_like(dwu_ref)
            dwd_ref[...] = jnp.zeros_like(dwd_ref)

        h = h_ref[...]
        dyb = dy_ref[...].astype(BF16)
        gv = g_ref[0].astype(F32)
        uv = u_ref[0].astype(F32)
        da = _dot_nt(dyb, wd_ref[0])
        sg = _sigmoid(gv)
        silu = gv * sg
        du = (da * silu).astype(BF16)
        dg = (da * uv * (sg * (1.0 + gv * (1.0 - sg)))).astype(BF16)
        dwd_ref[0] += _dot_tn((silu * uv).astype(BF16), dyb)
        dwg_ref[0] += _dot_tn(dg, h)
        dwu_ref[0] += _dot_tn(du, h)
        dh_ref[0] = (_dot(dg, wg_ref[0]) + _dot(du, wu_ref[0])).astype(BF16)

    row = lambda j, t: (t, 0)
    shard = lambda j, t: (j, 0, 0)
    act = lambda j, t: (j, t, 0)
    return pl.pallas_call(
        body, name="ffn_bwd", grid=(N_CHIPS, S // tm),
        in_specs=[pl.BlockSpec((tm, D), row), pl.BlockSpec((tm, D), row),
                  pl.BlockSpec((1, tm, F), act), pl.BlockSpec((1, tm, F), act)] + [pl.BlockSpec((1, F, D), shard)] * 3,
        out_specs=[pl.BlockSpec((1, F, D), shard)] * 3 + [pl.BlockSpec((1, tm, D), act)],
        out_shape=[jax.ShapeDtypeStruct((N_CHIPS, F, D), F32)] * 3 + [jax.ShapeDtypeStruct((N_CHIPS, S, D), BF16)],
    )(h2, dy, g, u, wg_g, wu_g, wd_g)


def _out_proj_bwd(dh2p, dy, x1, ffn_w, w_out_g, o_dil, o_sb, w_dil, w_sbn, seg_ones, ffn_grads):
    S, D = dy.shape
    tm = 256
    ng = len(ffn_grads)

    def body(dh_ref, dy_ref, x1_ref, nw_ref, w_ref, od_ref, os_ref, wd_ref, ws_ref, g_ref, *rest):
        gin, rest = rest[:ng], rest[ng:]
        dx1_ref, dod0, dod1, dod2, dos_ref, dl0, dl1, dl2, dw_ref, dnw_ref, dwd_ref, dws_ref = rest[:12]
        gout, (scr, send, recv) = rest[12:12 + ng], rest[12 + ng:]
        i = pl.program_id(0)
        plans = [_pair_send_plan(gin[a], gout[a], send.at[a], recv.at[a]) for a in range(ng)]

        @pl.when(i == 0)
        def _():
            for start, _ in plans:
                start()

        @pl.when(i == 0)
        def _():
            for r_ in (dw_ref, dnw_ref, dwd_ref, dws_ref):
                r_[...] = jnp.zeros_like(r_)

        dh2 = _sum4(dh_ref)
        dxn, dwn = _rms_bwd(dh2, x1_ref[...], nw_ref[...])
        dnw_ref[...] += jnp.sum(dwn, axis=0, keepdims=True)
        dx1 = dy_ref[...] + dxn
        dx1_ref[...] = dx1
        dx1b = dx1.astype(BF16)
        dmix = _dot_nt(dx1b, w_ref[...])
        od = od_ref[...]
        osb = os_ref[...]
        mixed = jnp.concatenate([(od * _rms(od) * wd_ref[...]).astype(BF16),
                                 (osb * _rms(osb) * ws_ref[...]).astype(BF16)], axis=1)
        dw_ref[...] += _dot_tn(mixed, dx1b)
        do, dwo = _rms_bwd(dmix[:, :GROUP], od, wd_ref[...])
        dwd_ref[...] += jnp.sum(dwo, axis=0, keepdims=True)
        _to_strided(scr, do, list(zip(DILATIONS, (dod0, dod1, dod2))))
        _to_strided(scr, _segsum(do * od, g_ref[...]), list(zip(DILATIONS, (dl0, dl1, dl2))))
        do, dwo = _rms_bwd(dmix[:, GROUP:], osb, ws_ref[...])
        dws_ref[...] += jnp.sum(dwo, axis=0, keepdims=True)
        dos_ref[...] = do.astype(BF16)

        @pl.when(i == S // tm - 1)
        def _():
            for _, finish in plans:
                finish()

    row = lambda i: (i, 0)
    gsp = pl.BlockSpec((tm, GROUP), row)
    dsp = pl.BlockSpec((tm, D), row)
    planes = [_strided_spec(tm, r) for r in DILATIONS]
    halves = [jax.ShapeDtypeStruct((g.shape[0], g.shape[1] // 2, g.shape[2]), g.dtype) for g in ffn_grads]
    return pl.pallas_call(
        body, name="out_proj_bwd", grid=(S // tm,),
        in_specs=[pl.BlockSpec((N_CHIPS, tm, D), lambda i: (0, i, 0)), dsp, dsp, _full((1, D)), _full((2 * GROUP, D)),
                  gsp, gsp, _full((1, GROUP)), _full((1, GROUP)), _full((GROUP, GROUP))] + [HBM] * ng,
        out_specs=[dsp] + planes + [gsp] + planes
                  + [_full((2 * GROUP, D)), _full((1, D)), _full((1, GROUP)), _full((1, GROUP))] + [HBM] * ng,
        out_shape=[jax.ShapeDtypeStruct((S, D), F32)] + [_strided_shape(S, r, BF16) for r in DILATIONS]
                  + [jax.ShapeDtypeStruct((S, GROUP), BF16)] + [_strided_shape(S, r, F32) for r in DILATIONS]
                  + [jax.ShapeDtypeStruct((2 * GROUP, D), F32),
                     jax.ShapeDtypeStruct((1, D), F32), jax.ShapeDtypeStruct((1, GROUP), F32),
                     jax.ShapeDtypeStruct((1, GROUP), F32)] + halves,
        scratch_shapes=[_strided_scratch(tm), pltpu.SemaphoreType.DMA((ng,)), pltpu.SemaphoreType.DMA((ng,))],
    )(dh2p, dy, x1, ffn_w, w_out_g, o_dil, o_sb, w_dil, w_sbn, seg_ones, *ffn_grads)


def _qk_bwd(dq_br, dk_br, dv_br, dqs, dks, dvs, qa, ka, qw, kw, cos_t, sin_t, seg_ones):
    S = qa.shape[0]
    tm = 256

    def body(q0, q1, q2, k0, k1, k2, v0, v1, v2, dqs_ref, dks_ref, dvs_ref, qa_ref, ka_ref, qw_ref, kw_ref,
             cos_ref, sin_ref, g_ref, dp_ref, dqw_ref, dkw_ref, accq, acck, scr):
        i = pl.program_id(0)

        @pl.when(i == 0)
        def _():
            accq[...] = jnp.zeros_like(accq)
            acck[...] = jnp.zeros_like(acck)

        def branches(refs):
            return (_from_strided(scr, refs[0]) + _from_strided(scr, refs[1])) + _from_strided(scr, refs[2])

        g = g_ref[...]
        cos = _tile4(cos_ref[...])
        sin = _tile4(sin_ref[...])
        for b, (refs, pre_ref, w_ref, acc) in enumerate((((q0, q1, q2), qa_ref, qw_ref, accq),
                                                        ((k0, k1, k2), ka_ref, kw_ref, acck))):
            dh = branches(refs)
            dn = dh * cos + _rot_half(dh * sin)
            pre = pre_ref[...]
            rstd = lax.rsqrt(_segsum(pre * pre, g) * (1.0 / HEAD_DIM) + EPS)
            xh = pre * rstd
            acc[...] += jnp.sum(dn * xh, axis=0, keepdims=True)
            dxh = dn * w_ref[...]
            dpre = rstd * (dxh - xh * (_segsum(dxh * xh, g) * (1.0 / HEAD_DIM)))
            dp_ref[:, b * GROUP:(b + 1) * GROUP] = dpre.astype(BF16)
        dp_ref[:, 2 * GROUP:3 * GROUP] = branches((v0, v1, v2)).astype(BF16)
        dp_ref[:, 3 * GROUP:4 * GROUP] = dqs_ref[...].astype(BF16)
        dp_ref[:, 4 * GROUP:5 * GROUP] = dks_ref[...].astype(BF16)
        dp_ref[:, 5 * GROUP:6 * GROUP] = dvs_ref[...].astype(BF16)

        @pl.when(i == S // tm - 1)
        def _():
            for acc, o_ref in ((accq, dqw_ref), (acck, dkw_ref)):
                a = acc[...]
                pair = (a[:, 0:LANES] + a[:, LANES:2 * LANES]) + (a[:, 2 * LANES:3 * LANES] + a[:, 3 * LANES:4 * LANES])
                o_ref[...] = pair + pltpu.roll(pair, HEAD_DIM, 1)

    row = lambda i: (i, 0)
    gsp = pl.BlockSpec((tm, GROUP), row)
    tab = pl.BlockSpec((tm, LANES), row)
    planes = [_strided_spec(tm, r) for r in DILATIONS]
    return pl.pallas_call(
        body, name="qk_bwd", grid=(S // tm,),
        in_specs=planes * 3 + [gsp] * 5 + [_full((1, GROUP)), _full((1, GROUP)), tab, tab, _full((GROUP, GROUP))],
        out_specs=[pl.BlockSpec((tm, 6 * GROUP), row), _full((1, LANES)), _full((1, LANES))],
        out_shape=[jax.ShapeDtypeStruct((S, 6 * GROUP), BF16), jax.ShapeDtypeStruct((1, LANES), F32),
                   jax.ShapeDtypeStruct((1, LANES), F32)],
        scratch_shapes=[pltpu.VMEM((1, GROUP), F32), pltpu.VMEM((1, GROUP), F32), _strided_scratch(tm)],
    )(*dq_br, *dk_br, *dv_br, dqs, dks, dvs, qa, ka, qw, kw, cos_t, sin_t, seg_ones)


def _attn_in_bwd(dq_br, dk_br, dv_br, dqs, dks, dvs, qa, ka, qw, kw, cos_t, sin_t, seg_ones, h, w_in_g, x, dx1, attn_w):
    S, D = x.shape
    wc = w_in_g.shape[2]
    tm = 256

    def body(q0, q1, q2, k0, k1, k2, v0, v1, v2, dqs_ref, dks_ref, dvs_ref, qa_ref, ka_ref, qw_ref, kw_ref,
             cos_ref, sin_ref, g_ref, h_ref, w_ref, x_ref, dx1_ref, aw_ref,
             gx_ref, dw_ref, daw_ref, dqw_ref, dkw_ref, accq, acck, scr):
        i = pl.program_id(0)

        @pl.when(i == 0)
        def _():
            accq[...] = jnp.zeros_like(accq)
            acck[...] = jnp.zeros_like(acck)
            dw_ref[...] = jnp.zeros_like(dw_ref)
            daw_ref[...] = jnp.zeros_like(daw_ref)

        def branches(refs):
            return (_from_strided(scr, refs[0]) + _from_strided(scr, refs[1])) + _from_strided(scr, refs[2])

        g = g_ref[...]
        cos = _tile4(cos_ref[...])
        sin = _tile4(sin_ref[...])
        pieces = []
        for refs, pre_ref, w_r, acc in (((q0, q1, q2), qa_ref, qw_ref, accq), ((k0, k1, k2), ka_ref, kw_ref, acck)):
            dh = branches(refs)
            dn = dh * cos + _rot_half(dh * sin)
            pre = pre_ref[...]
            rstd = lax.rsqrt(_segsum(pre * pre, g) * (1.0 / HEAD_DIM) + EPS)
            xh = pre * rstd
            acc[...] += jnp.sum(dn * xh, axis=0, keepdims=True)
            dxh = dn * w_r[...]
            pieces.append((rstd * (dxh - xh * (_segsum(dxh * xh, g) * (1.0 / HEAD_DIM)))).astype(BF16))
        pieces += [branches((v0, v1, v2)).astype(BF16), dqs_ref[...].astype(BF16), dks_ref[...].astype(BF16),
                   dvs_ref[...].astype(BF16)]
        dproj = jnp.concatenate(pieces, axis=1)
        hv = h_ref[...]
        dh = jnp.zeros((tm, D), F32)
        for j in range(N_CHIPS):
            dp = dproj[:, j * wc:(j + 1) * wc]
            dw_ref[j] += _dot_tn(hv, dp)
            dh = dh + _dot_nt(dp, w_ref[j])
        dx, dw = _rms_bwd(dh, x_ref[...], aw_ref[...])
        daw_ref[...] += jnp.sum(dw, axis=0, keepdims=True)
        gx_ref[...] = dx1_ref[...] + dx

        @pl.when(i == S // tm - 1)
        def _():
            for acc, o_ref in ((accq, dqw_ref), (acck, dkw_ref)):
                a = acc[...]
                pair = (a[:, 0:LANES] + a[:, LANES:2 * LANES]) + (a[:, 2 * LANES:3 * LANES] + a[:, 3 * LANES:4 * LANES])
                o_ref[...] = pair + pltpu.roll(pair, HEAD_DIM, 1)

    row = lambda i: (i, 0)
    gsp = pl.BlockSpec((tm, GROUP), row)
    dsp = pl.BlockSpec((tm, D), row)
    tab = pl.BlockSpec((tm, LANES), row)
    planes = [_strided_spec(tm, r) for r in DILATIONS]
    return pl.pallas_call(
        body, name="attn_in_bwd", grid=(S // tm,),
        in_specs=planes * 3 + [gsp] * 5 + [_full((1, GROUP)), _full((1, GROUP)), tab, tab, _full((GROUP, GROUP)),
                                          dsp, _full((N_CHIPS, D, wc)), dsp, dsp, _full((1, D))],
        out_specs=[dsp, _full((N_CHIPS, D, wc)), _full((1, D)), _full((1, LANES)), _full((1, LANES))],
        out_shape=[jax.ShapeDtypeStruct((S, D), F32), jax.ShapeDtypeStruct((N_CHIPS, D, wc), F32),
                   jax.ShapeDtypeStruct((1, D), F32), jax.ShapeDtypeStruct((1, LANES), F32),
                   jax.ShapeDtypeStruct((1, LANES), F32)],
        scratch_shapes=[pltpu.VMEM((1, GROUP), F32), pltpu.VMEM((1, GROUP), F32), _strided_scratch(tm)],
    )(*dq_br, *dk_br, *dv_br, dqs, dks, dvs, qa, ka, qw, kw, cos_t, sin_t, seg_ones, h, w_in_g, x, dx1, attn_w)


def _in_proj_bwd(h, dproj, w_in_g):
    S, D = h.shape
    wc = w_in_g.shape[2]
    tm = 512

    def body(h_ref, dp_ref, w_ref, dw_ref, dh_ref):
        t = pl.program_id(1)

        @pl.when(t == 0)
        def _():
            dw_ref[...] = jnp.zeros_like(dw_ref)

        dp = dp_ref[...]
        dw_ref[0] += _dot_tn(h_ref[...], dp)
        dh_ref[0] = _dot_nt(dp, w_ref[0]).astype(BF16)

    return pl.pallas_call(
        body, name="in_proj_bwd", grid=(N_CHIPS, S // tm),
        in_specs=[pl.BlockSpec((tm, D), lambda j, t: (t, 0)), pl.BlockSpec((tm, wc), lambda j, t: (t, j)),
                  pl.BlockSpec((1, D, wc), lambda j, t: (j, 0, 0))],
        out_specs=[pl.BlockSpec((1, D, wc), lambda j, t: (j, 0, 0)), pl.BlockSpec((1, tm, D), lambda j, t: (j, t, 0))],
        out_shape=[jax.ShapeDtypeStruct((N_CHIPS, D, wc), F32), jax.ShapeDtypeStruct((N_CHIPS, S, D), BF16)],
    )(h, dproj, w_in_g)


def _in_norm_bwd(dhp, dx1, x, attn_w):
    S, D = x.shape
    tm = 512

    def body(dh_ref, dx1_ref, x_ref, w_ref, gx_ref, dw_ref):
        i = pl.program_id(0)

        @pl.when(i == 0)
        def _():
            dw_ref[...] = jnp.zeros_like(dw_ref)

        dh = _sum4(dh_ref)
        dx, dw = _rms_bwd(dh, x_ref[...], w_ref[...])
        dw_ref[...] += jnp.sum(dw, axis=0, keepdims=True)
        gx_ref[...] = dx1_ref[...] + dx

    row = lambda i: (i, 0)
    dsp = pl.BlockSpec((tm, D), row)
    return pl.pallas_call(
        body, name="in_norm_bwd", grid=(S // tm,),
        in_specs=[pl.BlockSpec((N_CHIPS, tm, D), lambda i: (0, i, 0)), dsp, dsp, _full((1, D))],
        out_specs=[dsp, _full((1, D))],
        out_shape=[jax.ShapeDtypeStruct((S, D), F32), jax.ShapeDtypeStruct((1, D), F32)],
    )(dhp, dx1, x, attn_w)


def _constants(S):
    pos = jnp.arange(S, dtype=F32)
    inv_freq = ROPE_THETA ** (-jnp.arange(0, HEAD_DIM, 2, dtype=F32) / HEAD_DIM)
    ang = pos[:, None] * inv_freq[None, :]
    cos, sin = jnp.cos(ang), jnp.sin(ang)
    cos_t = jnp.concatenate([cos, cos] * 2, axis=1)
    sin_t = jnp.concatenate([-sin, sin] * 2, axis=1)
    idx = jnp.arange(GROUP)
    seg_ones = (idx[:, None] // HEAD_DIM == idx[None, :] // HEAD_DIM).astype(BF16)
    r = jnp.arange(BLOCK)
    ones = jnp.ones((BLOCK, BLOCK), BF16)
    tris = [jnp.concatenate([jnp.concatenate([m.astype(BF16), ones], axis=1)] * 2, axis=0) for m in
            (r[:, None] > r[None, :],
             r[:, None] <= r[None, :],
             r[:, None] < r[None, :])]
    return cos_t, sin_t, seg_ones, tris


FFN_NAMES = ("w_gate", "w_up", "w_down")


def _device_step(x, target, attn_w, qn_w, kn_w, dil_w, sbn_w, ffn_w, w_in_g, w_out_slots, ffn_slots, core, chip):
    S = x.shape[0]
    cos_t, sin_t, seg_ones, (tri_later, tri_upto, tri_before) = _constants(S)
    reps = GROUP // HEAD_DIM
    qw = jnp.tile(qn_w, (1, reps))
    kw = jnp.tile(kn_w, (1, reps))

    nd = len(DILATIONS)
    h, qa, ka, *rest = _in_proj_fwd(x, attn_w, w_in_g, qw, kw, cos_t, sin_t, seg_ones)
    qh, kh, va, (qs, ks, vs) = rest[:nd], rest[nd:2 * nd], rest[2 * nd:3 * nd], rest[3 * nd:]
    hosted = ([ffn_slots[0]], [w_out_slots], [])
    branches = [_dil_fwd(qh[b], kh[b], va[b], hosted[b]) for b in range(nd)]
    wg_g, w_out_g = branches[0][2], branches[1][2].reshape(-1, x.shape[1])
    o_sb, ltot, walked, wu_g, wd_g = _sb_fwd(qs, ks, vs, tri_later, ffn_slots[1:])
    o_dil, *lse, x1 = _out_proj_fwd([b[0] for b in branches], [b[1] for b in branches], o_sb, x, dil_w, sbn_w, w_out_g)
    h2, g, u, dy, loss_parts = _ffn_fwd(x1, target, ffn_w, wg_g, wu_g, wd_g)

    *ffn_grads, dh2p = _ffn_bwd(h2, dy, g, u, wg_g, wu_g, wd_g)
    dx1, *mid, dw_out, dffn_w, ddil_w, dsbn_w, p0, p1, p2 = _out_proj_bwd(
        dh2p, dy, x1, ffn_w, w_out_g, o_dil, o_sb, dil_w, sbn_w, seg_ones, ffn_grads)
    do_dil, do_sb, delta = mid[:nd], mid[nd], mid[nd + 1:]
    parts = [_pair_sum(gr, fr, core, n) for gr, fr, n in zip(ffn_grads, (p0, p1, p2), FFN_NAMES)]
    dqs, dks, dvs = _sb_bwd(walked[:, 0, 0], qs, ks, vs, do_sb, ltot, tri_upto, tri_before)
    dbr = [_dil_bwd(qh[b], kh[b], va[b], do_dil[b], lse[b], delta[b], parts[b]) for b in range(nd)]
    ffn_halves = [_chip_sum(dbr[b][3], parts[b], chip, FFN_NAMES[b]) for b in range(nd)]
    grad_x, dw_in, dattn_w, dqw, dkw = _attn_in_bwd(
        [b[0] for b in dbr], [b[1] for b in dbr], [b[2] for b in dbr], dqs, dks, dvs,
        qa, ka, qw, kw, cos_t, sin_t, seg_ones, h, w_in_g, x, dx1, attn_w)
    small = dict(attn=dattn_w, q=dqw[:, :HEAD_DIM], k=dkw[:, :HEAD_DIM], dil=ddil_w, sb=dsbn_w, ffn=dffn_w)
    return loss_parts, grad_x, small, dw_in, dw_out, ffn_halves


HBM = pl.BlockSpec(memory_space=pltpu.HBM)
VMEM = pl.BlockSpec(memory_space=pltpu.VMEM)
CHIP_FLIPS = ((1, 0), (0, 1), (1, 1))


def _place():
    return lax.axis_index("x"), lax.axis_index("y"), lax.axis_index("c")


def _flip(v, d):
    return 1 - v if d else v


def _half_rows(c, n):
    return pl.ds(pl.multiple_of(c * (n // 2), 16), n // 2)


def _gather_plan(slot_in, slot_out, send, recv):
    x, y, c = _place()
    p = 2 * x + y
    chips = [(_flip(x, dx), _flip(y, dy)) for dx, dy in CHIP_FLIPS]
    mine, other = _half_rows(c, slot_in.shape[1]), _half_rows(1 - c, slot_in.shape[1])

    def copy(k, src, dst, to):
        return pltpu.make_async_remote_copy(src_ref=src, dst_ref=dst, send_sem=send.at[k], recv_sem=recv.at[k],
                                            device_id=to, device_id_type=MESH)

    def first(k):
        return copy(k, slot_in.at[p, mine], slot_out.at[p, mine], (*chips[k], c))

    def passed(k, rows):
        land = slot_out.at[2 * chips[k][0] + chips[k][1], rows]
        return copy(3 + k, land, land, (x, y, 1 - c))

    def start():
        for k in range(3):
            first(k).start()

    def forward():
        for k in range(3):
            land = slot_out.at[2 * chips[k][0] + chips[k][1], mine]
            copy(k, land, land, (*chips[k], c)).wait_recv()
            passed(k, mine).start()

    def finish():
        for k in range(3):
            passed(k, other).wait_recv()
        for k in range(3):
            first(k).wait_send()
            passed(k, mine).wait_send()

    return start, forward, finish


def _chip_send_plan(part_in, recv_out, send, recv):
    x, y, c = _place()
    p = 2 * x + y
    chips = [(_flip(x, dx), _flip(y, dy)) for dx, dy in CHIP_FLIPS]

    def copy(k):
        q = 2 * chips[k][0] + chips[k][1]
        return pltpu.make_async_remote_copy(src_ref=part_in.at[q], dst_ref=recv_out.at[p], send_sem=send.at[k],
                                            recv_sem=recv.at[k], device_id=(*chips[k], c), device_id_type=MESH)

    def start():
        for k in range(3):
            copy(k).start()

    def finish():
        for k in range(3):
            land = recv_out.at[2 * chips[k][0] + chips[k][1]]
            pltpu.make_async_remote_copy(src_ref=land, dst_ref=land, send_sem=send.at[k], recv_sem=recv.at[k],
                                         device_id=(*chips[k], c), device_id_type=MESH).wait_recv()
        for k in range(3):
            copy(k).wait_send()

    return start, finish


def _pair_send_plan(grad_in, recv_out, send, recv):
    x, y, c = _place()

    def copy():
        theirs = _half_rows(1 - c, grad_in.shape[1])
        return pltpu.make_async_remote_copy(src_ref=grad_in.at[:, theirs, :], dst_ref=recv_out, send_sem=send,
                                            recv_sem=recv, device_id=(x, y, 1 - c), device_id_type=MESH)

    return (lambda: copy().start()), (lambda: copy().wait())


def _own_slots(shard):
    here = 2 * lax.axis_index("x") + lax.axis_index("y")
    return lax.dynamic_update_slice(lax.empty((N_CHIPS,) + shard.shape, shard.dtype), shard[None], (here, 0, 0))


def _gather_weights(shards):
    n = len(shards)

    def body(*refs):
        ins, outs = refs[:n], refs[n:2 * n]
        send, recv = refs[2 * n:]
        plans = [_gather_plan(ins[a], outs[a], send.at[pl.ds(6 * a, 6)], recv.at[pl.ds(6 * a, 6)]) for a in range(n)]
        for stage in range(3):
            for plan in plans:
                plan[stage]()

    slots = [_own_slots(s) for s in shards]
    return pl.pallas_call(
        body, name="gather_weights", in_specs=[HBM] * n, out_specs=[HBM] * n,
        out_shape=[jax.ShapeDtypeStruct(s.shape, s.dtype) for s in slots],
        input_output_aliases={a: a for a in range(n)},
        scratch_shapes=[pltpu.SemaphoreType.DMA((6 * n,)), pltpu.SemaphoreType.DMA((6 * n,))],
    )(*slots)


def _pair_exchange(grads, small):
    n = len(grads)

    def body(*refs):
        gin, sm = refs[:n], refs[n]
        gout, sm_all = refs[n + 1:2 * n + 1], refs[2 * n + 1]
        send, recv = refs[2 * n + 2:]
        x, y, c = _place()
        me = 4 * x + 2 * y + c
        big = [_pair_send_plan(gin[a], gout[a], send.at[a], recv.at[a]) for a in range(n)]
        for start, _ in big:
            start()
        sm_all[pl.ds(me, 1)] = sm[...][None]
        tiny = []
        for k in range(1, N_DEV):
            px, py, pc = _flip(x, k & 4), _flip(y, k & 2), _flip(c, k & 1)
            tiny.append((pltpu.make_async_remote_copy(
                src_ref=sm, dst_ref=sm_all.at[me], send_sem=send.at[n + k - 1], recv_sem=recv.at[n + k - 1],
                device_id=(px, py, pc), device_id_type=MESH), 4 * px + 2 * py + pc))
            tiny[-1][0].start()
        for k, (cp, peer) in enumerate(tiny):
            pltpu.make_async_remote_copy(src_ref=sm, dst_ref=sm_all.at[peer], send_sem=send.at[n + k],
                                         recv_sem=recv.at[n + k], device_id=(x, y, c),
                                         device_id_type=MESH).wait_recv()
            cp.wait_send()
        for _, finish in big:
            finish()

    halves = [jax.ShapeDtypeStruct((g.shape[0], g.shape[1] // 2, g.shape[2]), g.dtype) for g in grads]
    return pl.pallas_call(
        body, name="pair_exchange", in_specs=[HBM] * n + [VMEM], out_specs=[HBM] * n + [VMEM],
        out_shape=halves + [jax.ShapeDtypeStruct((N_DEV,) + small.shape, small.dtype)],
        scratch_shapes=[pltpu.SemaphoreType.DMA((n + N_DEV - 1,)), pltpu.SemaphoreType.DMA((n + N_DEV - 1,))],
    )(*grads, small)


def _chip_exchange(parts):
    n = len(parts)

    def body(*refs):
        pin, pout = refs[:n], refs[n:2 * n]
        send, recv = refs[2 * n:]
        plans = [_chip_send_plan(pin[a], pout[a], send.at[pl.ds(3 * a, 3)], recv.at[pl.ds(3 * a, 3)]) for a in range(n)]
        for stage in range(2):
            for plan in plans:
                plan[stage]()

    return pl.pallas_call(
        body, name="chip_exchange", in_specs=[HBM] * n, out_specs=[HBM] * n,
        out_shape=[jax.ShapeDtypeStruct(s.shape, s.dtype) for s in parts],
        scratch_shapes=[pltpu.SemaphoreType.DMA((3 * n,)), pltpu.SemaphoreType.DMA((3 * n,))],
    )(*parts)


def _pair_swap(halves):
    n = len(halves)

    def body(*refs):
        hin, hout = refs[:n], refs[n:2 * n]
        send, recv = refs[2 * n:]
        x, y, c = _place()
        swaps = [pltpu.make_async_remote_copy(src_ref=hin[a], dst_ref=hout[a], send_sem=send.at[a],
                                              recv_sem=recv.at[a], device_id=(x, y, 1 - c), device_id_type=MESH)
                 for a in range(n)]
        for cp in swaps:
            cp.start()
        for cp in swaps:
            cp.wait()

    return pl.pallas_call(
        body, name="pair_swap", in_specs=[HBM] * n, out_specs=[HBM] * n,
        out_shape=[jax.ShapeDtypeStruct(s.shape, s.dtype) for s in halves],
        scratch_shapes=[pltpu.SemaphoreType.DMA((n,)), pltpu.SemaphoreType.DMA((n,))],
    )(*halves)


def _pair_sum(grad, recv, c, tag):
    _, R, C = grad.shape
    hr = R // 2

    def body(c_ref, a_ref, b_ref, o_ref):
        o_ref[...] = (a_ref[...] + b_ref[...]).astype(BF16)

    return pl.pallas_call(
        body, name="pair_sum_" + tag,
        grid_spec=pltpu.PrefetchScalarGridSpec(
            num_scalar_prefetch=1, grid=(N_CHIPS,),
            in_specs=[pl.BlockSpec((1, hr, C), lambda s, cr: (s, cr[0], 0)),
                      pl.BlockSpec((1, hr, C), lambda s, cr: (s, 0, 0))],
            out_specs=pl.BlockSpec((1, hr, C), lambda s, cr: (s, 0, 0))),
        out_shape=jax.ShapeDtypeStruct((N_CHIPS, hr, C), BF16),
    )(c, grad, recv)


def _chip_sum(received, own, chip, tag):
    _, rows, C = received.shape
    tr = rows // 2

    def body(chip_ref, own_ref, r1_ref, r2_ref, r3_ref, o_ref):
        p = [r[0].astype(F32) for r in (own_ref, r1_ref, r2_ref, r3_ref)]
        o_ref[...] = (p[0] + p[1]) + (p[2] + p[3])

    def slot(k):
        return pl.BlockSpec((1, tr, C), lambda i, cr: (jnp.bitwise_xor(cr[0], k), i, 0))

    return pl.pallas_call(
        body, name="chip_sum_" + tag,
        grid_spec=pltpu.PrefetchScalarGridSpec(
            num_scalar_prefetch=1, grid=(rows // tr,), in_specs=[slot(0), slot(1), slot(2), slot(3)],
            out_specs=pl.BlockSpec((tr, C), lambda i, cr: (i, 0))),
        out_shape=jax.ShapeDtypeStruct((rows, C), F32),
    )(chip, own, received, received, received)


def _adamw_math(w, g, m, v):
    m = ADAM_B1 * m + (1.0 - ADAM_B1) * g
    v = ADAM_B2 * v + (1.0 - ADAM_B2) * (g * g)
    m_hat = m / (1.0 - ADAM_B1 ** ADAM_STEP)
    v_hat = v / (1.0 - ADAM_B2 ** ADAM_STEP)
    delta = -ADAM_LR * (m_hat / (jnp.sqrt(v_hat) + ADAM_EPS) + ADAM_WD * w)
    return delta, m, v


def _adamw(w, g_mine, g_other, m, v, c, tag):
    R, C = w.shape
    tr = R // 4

    def body(c_ref, w_ref, gm_ref, go_ref, m_ref, v_ref, g_ref, d_ref, nm_ref, nv_ref):
        g = jnp.where(pl.program_id(0) == c_ref[0], gm_ref[...], go_ref[...])
        g_ref[...] = g
        d_ref[...], nm_ref[...], nv_ref[...] = _adamw_math(w_ref[...], g, m_ref[...], v_ref[...])

    blk = pl.BlockSpec((tr, C), lambda h, i, cr: (2 * h + i, 0))
    half = pl.BlockSpec((tr, C), lambda h, i, cr: (i, 0))
    return pl.pallas_call(
        body, name="adamw_" + tag,
        grid_spec=pltpu.PrefetchScalarGridSpec(
            num_scalar_prefetch=1, grid=(2, 2), in_specs=[blk, half, half, blk, blk], out_specs=[blk] * 4),
        out_shape=[jax.ShapeDtypeStruct((R, C), F32)] * 4,
    )(c, w, g_mine, g_other, m, v)


def _small_update(all_small, w, m, v):
    def body(a_ref, w_ref, m_ref, v_ref, g_ref, d_ref, nm_ref, nv_ref):
        g = ((a_ref[0] + a_ref[1]) + (a_ref[2] + a_ref[3])) + ((a_ref[4] + a_ref[5]) + (a_ref[6] + a_ref[7]))
        g_ref[...] = g
        d_ref[...], nm_ref[...], nv_ref[...] = _adamw_math(w_ref[...], g, m_ref[...], v_ref[...])

    return pl.pallas_call(
        body, name="small_update", out_shape=[jax.ShapeDtypeStruct(w.shape, F32)] * 4,
    )(all_small, w, m, v)


SMALL_ROWS = (("attn", 0, 0), ("ffn", 1, 0), ("dil", 2, 0), ("sb", 2, GROUP), ("q", 3, 0), ("k", 3, HEAD_DIM),
              ("loss", 4, 0))


def _pack_small(vals, D):
    rows = [jnp.zeros((1, D), F32) for _ in range(8)]
    for name, r, off in SMALL_ROWS:
        if name in vals:
            rows[r] = lax.dynamic_update_slice(rows[r], vals[name].astype(F32), (0, off))
    return jnp.concatenate(rows, axis=0)


def _unpack_small(packed, vals):
    return {name: packed[r:r + 1, off:off + vals[name].shape[1]] for name, r, off in SMALL_ROWS if name in vals}


def kernel(x, attn_norm_w, w_in, q_norm_w, k_norm_w, dil_out_norm_w, sb_out_norm_w, w_out, ffn_norm_w, w_gate, w_up, w_down, loss_target, m_attn_norm_w, m_w_in, m_q_norm_w, m_k_norm_w, m_dil_out_norm_w, m_sb_out_norm_w, m_w_out, m_ffn_norm_w, m_w_gate, m_w_up, m_w_down, v_attn_norm_w, v_w_in, v_q_norm_w, v_k_norm_w, v_dil_out_norm_w, v_sb_out_norm_w, v_w_out, v_ffn_norm_w, v_w_gate, v_w_up, v_w_down):
    D = x.shape[-1]
    big_names = ("w_in", "w_out", "w_gate", "w_up", "w_down")
    flipped = ("w_gate", "w_up")
    tr = lambda a: jnp.swapaxes(a[0], 0, 1)
    big_w = dict(w_in=w_in[0], w_out=w_out[0], w_gate=tr(w_gate), w_up=tr(w_up), w_down=w_down[0])
    big_m = dict(w_in=m_w_in[0], w_out=m_w_out[0], w_gate=tr(m_w_gate), w_up=tr(m_w_up), w_down=m_w_down[0])
    big_v = dict(w_in=v_w_in[0], w_out=v_w_out[0], w_gate=tr(v_w_gate), w_up=tr(v_w_up), w_down=v_w_down[0])
    small_w = dict(attn=attn_norm_w, q=q_norm_w, k=k_norm_w, dil=dil_out_norm_w, sb=sb_out_norm_w, ffn=ffn_norm_w)
    small_m = dict(attn=m_attn_norm_w, q=m_q_norm_w, k=m_k_norm_w, dil=m_dil_out_norm_w, sb=m_sb_out_norm_w,
                   ffn=m_ffn_norm_w)
    small_v = dict(attn=v_attn_norm_w, q=v_q_norm_w, k=v_k_norm_w, dil=v_dil_out_norm_w, sb=v_sb_out_norm_w,
                   ffn=v_ffn_norm_w)

    c = lax.axis_index("c").astype(jnp.int32).reshape(1)
    chip = (2 * lax.axis_index("x") + lax.axis_index("y")).astype(jnp.int32).reshape(1)
    (w_in_g,) = _gather_weights([big_w["w_in"].astype(BF16)])
    w_out_slots = _own_slots(big_w["w_out"].astype(BF16))
    ffn_slots = [_own_slots(big_w[n].astype(BF16)) for n in FFN_NAMES]

    loss_parts, grad_x, small_g, dw_in, dw_out, ffn_halves = _device_step(
        x[0], loss_target[0], attn_norm_w, q_norm_w, k_norm_w, dil_out_norm_w, sb_out_norm_w, ffn_norm_w,
        w_in_g, w_out_slots, ffn_slots, c, chip)
    small_g["loss"] = (jnp.sum(loss_parts[:, 0, 0]) * (0.5 / D)).reshape(1, 1)

    late = [dw_in, dw_out.reshape(N_CHIPS, -1, D)]
    *from_pair, all_small = _pair_exchange(late, _pack_small(small_g, D))
    chip_parts = [_pair_sum(g, r, c, n) for g, r, n in zip(late, from_pair, big_names)]
    from_chips = _chip_exchange(chip_parts)
    halves = [_chip_sum(r, p, chip, n) for r, p, n in zip(from_chips, chip_parts, big_names)] + ffn_halves
    others = _pair_swap(halves)
    big_out = {n: _adamw(big_w[n], mine, other, big_m[n], big_v[n], c, n)
               for n, mine, other in zip(big_names, halves, others)}
    sg, sd, sm, sv = _small_update(all_small, _pack_small(small_w, D), _pack_small(small_m, D),
                                   _pack_small(small_v, D))
    small_out = [_unpack_small(t, small_w) for t in (sg, sd, sm, sv)]

    order = (("attn", None), (None, "w_in"), ("q", None), ("k", None), ("dil", None), ("sb", None),
             (None, "w_out"), ("ffn", None), (None, "w_gate"), (None, "w_up"), (None, "w_down"))
    outs = [sg[4, 0], grad_x[None]]
    for kind in range(4):
        for s_name, b_name in order:
            if s_name is not None:
                outs.append(small_out[kind][s_name])
            else:
                res = big_out[b_name][kind]
                outs.append((jnp.swapaxes(res, 0, 1) if b_name in flipped else res)[None])
    return tuple(outs)
```

```python
import functools

import jax
import jax.numpy as jnp
from jax import lax
from jax.experimental import pallas as pl
from jax.experimental.pallas import tpu as pltpu

F32 = jnp.float32
BF16 = jnp.bfloat16
MESH = pl.DeviceIdType.MESH

HEAD_DIM = 64
GROUP = 512
BLOCK = 128
LANES = 128
N_CHIPS = 4
N_DEV = 8
EPS = 1e-6
ROPE_THETA = 10000.0
DILATIONS = (1, 4, 16)
NEG = -1e30

ADAM_LR = 0.001
ADAM_B1 = 0.9
ADAM_B2 = 0.999
ADAM_EPS = 1e-08
ADAM_WD = 0.01
ADAM_STEP = 10


def _dot(a, b):
    return jnp.dot(a, b, preferred_element_type=F32)


def _dot_nt(a, b):
    return lax.dot_general(a, b, (((1,), (1,)), ((), ())), preferred_element_type=F32)


def _dot_tn(a, b):
    return lax.dot_general(a, b, (((0,), (0,)), ((), ())), preferred_element_type=F32)


def _split(v):
    hi = lax.bitcast_convert_type(lax.bitcast_convert_type(v, jnp.uint32) & jnp.uint32(0xFFFF0000), F32)
    return hi.astype(BF16), (v - hi).astype(BF16)


def _segsum(v, g):
    hi, lo = _split(v)
    return _dot(hi, g) + _dot(lo, g)


def _rot_half(x):
    outs = []
    for c in range(x.shape[1] // LANES):
        xc = x[:, c * LANES:(c + 1) * LANES]
        lane = lax.broadcasted_iota(jnp.int32, xc.shape, 1)
        first = (lane % HEAD_DIM) < (HEAD_DIM // 2)
        outs.append(jnp.where(first, pltpu.roll(xc, LANES - 32, 1), pltpu.roll(xc, 32, 1)))
    return outs[0] if len(outs) == 1 else jnp.concatenate(outs, axis=1)


def _rms(x):
    return lax.rsqrt(jnp.mean(x * x, axis=-1, keepdims=True) + EPS)


def _rms_bwd(dy, x, w):
    rstd = _rms(x)
    xh = x * rstd
    dxh = dy * w
    dx = rstd * (dxh - xh * jnp.mean(dxh * xh, axis=-1, keepdims=True))
    return dx, dy * xh


def _sigmoid(x):
    return 1.0 / (1.0 + jnp.exp(-x))


def _sum4(ref):
    p = [ref[j].astype(F32) for j in range(N_CHIPS)]
    return (p[0] + p[1]) + (p[2] + p[3])


def _full(shape):
    n = len(shape)
    return pl.BlockSpec(shape, lambda *_: (0,) * n)


def _strided_spec(tm, r):
    return pl.BlockSpec((r, tm // r, GROUP), lambda i: (0, i, 0))


def _strided_shape(S, r, dtype):
    return jax.ShapeDtypeStruct((r, S // r, GROUP), dtype)


def _to_strided(scr, val, outs):
    chunks = range(GROUP // LANES)
    for k in chunks:
        scr[k] = val[:, _lanes(k)]
    for r, o_ref in outs:
        if r == 1:
            o_ref[0] = val.astype(o_ref.dtype)
            continue
        n = val.shape[0] // r
        for c in range(r):
            rows = pl.ds(c, n, stride=r)
            o_ref[c] = jnp.concatenate([scr.at[k][rows, :] for k in chunks], axis=1).astype(o_ref.dtype)


def _from_strided(scr, ref):
    r, n, _ = ref.shape
    if r == 1:
        return ref[0].astype(F32)
    chunks = range(GROUP // LANES)
    for c in range(r):
        plane = ref[c].astype(F32)
        for k in chunks:
            scr.at[k][pl.ds(c, n, stride=r), :] = plane[:, _lanes(k)]
    return jnp.concatenate([scr[k] for k in chunks], axis=1)


def _strided_scratch(tm):
    return pltpu.VMEM((GROUP // LANES, tm, LANES), F32)


def _tile4(t):
    return jnp.concatenate([t] * (GROUP // LANES), axis=1)


def _in_proj_fwd(x, attn_w, w_in_g, qw, kw, cos_t, sin_t, seg_ones):
    S, D = x.shape
    tm = 512
    wcols = w_in_g.shape[2]
    nd = len(DILATIONS)

    def body(x_ref, aw_ref, w_ref, qw_ref, kw_ref, cos_ref, sin_ref, g_ref, h_ref, qa_ref, ka_ref, *rest):
        q_refs, k_refs, v_refs = rest[:nd], rest[nd:2 * nd], rest[2 * nd:3 * nd]
        qs_ref, ks_ref, vs_ref, scr = rest[3 * nd:]
        xv = x_ref[...]
        h = (xv * _rms(xv) * aw_ref[...]).astype(BF16)
        h_ref[...] = h
        proj = jnp.concatenate([_dot(h, w_ref[j]) for j in range(N_CHIPS)], axis=1)
        qa = proj[:, 0 * GROUP:1 * GROUP]
        ka = proj[:, 1 * GROUP:2 * GROUP]
        qa_ref[...] = qa
        ka_ref[...] = ka
        _to_strided(scr, proj[:, 2 * GROUP:3 * GROUP], list(zip(DILATIONS, v_refs)))
        qs_ref[...] = proj[:, 3 * GROUP:4 * GROUP].astype(BF16)
        ks_ref[...] = proj[:, 4 * GROUP:5 * GROUP].astype(BF16)
        vs_ref[...] = proj[:, 5 * GROUP:6 * GROUP].astype(BF16)
        g = g_ref[...]
        cos = _tile4(cos_ref[...])
        sin = _tile4(sin_ref[...])
        for t, w_r, o_rs in ((qa, qw_ref, q_refs), (ka, kw_ref, k_refs)):
            rstd = lax.rsqrt(_segsum(t * t, g) * (1.0 / HEAD_DIM) + EPS)
            tn = t * rstd * w_r[...]
            _to_strided(scr, tn * cos + _rot_half(tn) * sin, list(zip(DILATIONS, o_rs)))

    row = lambda i: (i, 0)
    tile = lambda n, dt: jax.ShapeDtypeStruct((S, n), dt)
    planes = [_strided_spec(tm, r) for r in DILATIONS]
    return pl.pallas_call(
        body, name="in_proj_fwd", grid=(S // tm,),
        in_specs=[pl.BlockSpec((tm, D), row), _full((1, D)), _full((N_CHIPS, D, wcols)),
                  _full((1, GROUP)), _full((1, GROUP)),
                  pl.BlockSpec((tm, LANES), row), pl.BlockSpec((tm, LANES), row),
                  _full((GROUP, GROUP))],
        out_specs=[pl.BlockSpec((tm, D), row)] + [pl.BlockSpec((tm, GROUP), row)] * 2 + planes * 3
                  + [pl.BlockSpec((tm, GROUP), row)] * 3,
        out_shape=[tile(D, BF16), tile(GROUP, F32), tile(GROUP, F32)]
                  + [_strided_shape(S, r, BF16) for r in DILATIONS] * 3 + [tile(GROUP, BF16)] * 3,
        scratch_shapes=[_strided_scratch(tm)],
    )(x, attn_w, w_in_g, qw, kw, cos_t, sin_t, seg_ones)


DIL_PLANES = 1


def _dil_fwd(q, k, v, slots):
    r, L, _ = q.shape
    nb = L // BLOCK
    P = GROUP // LANES
    PL = min(DIL_PLANES, r)
    ns = len(slots)
    units = [(pp, hp) for pp in range(PL) for hp in range(P)]

    def body(q_ref, kc_ref, kp_ref, vc_ref, vp_ref, *rest):
        o_ref, l_ref = rest[ns:ns + 2]
        n = pl.program_id(1)
        finish = (_hosted_gathers(rest[:ns], rest[ns + 2:2 * ns + 2], *rest[2 * ns + 2:],
                                  pl.program_id(0) * nb + n, (r // PL) * nb) if ns else None)
        rowi = lax.broadcasted_iota(jnp.int32, (BLOCK, BLOCK), 0)
        coli = lax.broadcasted_iota(jnp.int32, (BLOCK, BLOCK), 1)
        first = coli < HEAD_DIM
        masks = (coli <= rowi, jnp.logical_and(coli >= rowi, n > 0))
        s2 = {}
        for pp, hp in units:
            q2 = _scaled(q_ref[pp, :, _lanes(hp)])
            for b, k_ref in enumerate((kc_ref, kp_ref)):
                s2[pp, hp, b] = _dot_nt(q2, _by_head(k_ref[pp, :, _lanes(hp)], first))
        ps, inv, lse = {}, {}, {}
        for pp, hp in units:
            for h in range(2):
                s = [jnp.where(masks[b], s2[pp, hp, b][:, h * BLOCK:(h + 1) * BLOCK], NEG) for b in range(2)]
                m = jnp.maximum(jnp.max(s[0], axis=1, keepdims=True), jnp.max(s[1], axis=1, keepdims=True))
                p = [jnp.exp(s[b] - m) for b in range(2)]
                den = jnp.sum(p[0], axis=1, keepdims=True) + jnp.sum(p[1], axis=1, keepdims=True)
                ps[pp, hp, h] = [p[b].astype(BF16) for b in range(2)]
                inv[pp, hp, h] = 1.0 / den
                lse[pp, hp, h] = m + jnp.log(den)
        for pp, hp in units:
            o = jnp.zeros((BLOCK, LANES), F32)
            for b, v_ref in enumerate((vc_ref, vp_ref)):
                o = o + _dot(jnp.concatenate([ps[pp, hp, 0][b], ps[pp, hp, 1][b]], axis=1),
                             _by_head(v_ref[pp, :, _lanes(hp)], first))
            o_ref[pp, :, _lanes(hp)] = o * jnp.where(first, inv[pp, hp, 0], inv[pp, hp, 1])
            l_ref[pp, :, _lanes(hp)] = jnp.where(first, lse[pp, hp, 0], lse[pp, hp, 1])
        if finish is not None:
            finish()

    cur = pl.BlockSpec((PL, BLOCK, GROUP), lambda c, n: (c, n, 0))
    prev = pl.BlockSpec((PL, BLOCK, GROUP), lambda c, n: (c, jnp.maximum(n - 1, 0), 0))
    h_in, h_out, h_shape, h_sems = _hosted_specs(slots)
    return pl.pallas_call(
        body, name="dil_fwd_r%d" % r, grid=(r // PL, nb),
        in_specs=[cur, cur, prev, cur, prev] + h_in, out_specs=[cur, cur] + h_out,
        out_shape=[jax.ShapeDtypeStruct(q.shape, F32)] * 2 + h_shape,
        input_output_aliases={5 + a: 2 + a for a in range(ns)},
        scratch_shapes=h_sems,
    )(q, k, k, v, v, *slots)


def _dil_bwd(q, k, v, do, lse, delta, part):
    r, L, _ = q.shape
    nb = L // BLOCK
    P = GROUP // LANES
    PL = min(DIL_PLANES, r)
    scale = HEAD_DIM ** -0.5
    units = [(pp, hp) for pp in range(PL) for hp in range(P)]

    def body(qc_ref, qn_ref, doc_ref, don_ref, lc_ref, ln_ref, dc_ref, dn_ref, k_ref, v_ref, part_in,
             dq_ref, dk_ref, dv_ref, part_out, carry, send, recv):
        j = pl.program_id(1)
        step = pl.program_id(0) * nb + j
        start, finish = _chip_send_plan(part_in, part_out, send, recv)
        pl.when(step == 0)(start)
        rowi = lax.broadcasted_iota(jnp.int32, (BLOCK, BLOCK), 0)
        coli = lax.broadcasted_iota(jnp.int32, (BLOCK, BLOCK), 1)
        first = coli < HEAD_DIM
        sides = ((qc_ref, doc_ref, lc_ref, dc_ref, coli <= rowi),
                 (qn_ref, don_ref, ln_ref, dn_ref, jnp.logical_and(coli >= rowi, j < nb - 1)))

        @pl.when(j == 0)
        def _():
            carry[...] = jnp.zeros_like(carry)

        kcat, q2, do2, s2, dp2 = {}, {}, {}, {}, {}
        for pp, hp in units:
            kcat[pp, hp] = _by_head(k_ref[pp, :, _lanes(hp)], first)
            vcat = _by_head(v_ref[pp, :, _lanes(hp)], first)
            for x, (q_r, do_r, _, _, _) in enumerate(sides):
                q2[pp, hp, x] = _scaled(q_r[pp, :, _lanes(hp)])
                do2[pp, hp, x] = do_r[pp, :, _lanes(hp)]
                s2[pp, hp, x] = _dot_nt(q2[pp, hp, x], kcat[pp, hp])
                dp2[pp, hp, x] = _dot_nt(do2[pp, hp, x], vcat)
        pcat, dscat = {}, {}
        for pp, hp in units:
            for x, (_, _, l_r, d_r, msk) in enumerate(sides):
                ps, dss = [], []
                for h in range(2):
                    col = hp * LANES + h * HEAD_DIM
                    half = slice(h * BLOCK, (h + 1) * BLOCK)
                    p = jnp.where(msk, jnp.exp(s2[pp, hp, x][:, half] - l_r[pp, :, col:col + 1]), 0.0)
                    ps.append(p.astype(BF16))
                    dss.append((p * (dp2[pp, hp, x][:, half] - d_r[pp, :, col:col + 1])).astype(BF16))
                pcat[pp, hp, x] = jnp.concatenate(ps, axis=1)
                dscat[pp, hp, x] = jnp.concatenate(dss, axis=1)
        for pp, hp in units:
            dv2 = _dot_tn(pcat[pp, hp, 0], do2[pp, hp, 0]) + _dot_tn(pcat[pp, hp, 1], do2[pp, hp, 1])
            dk2 = _dot_tn(dscat[pp, hp, 0], q2[pp, hp, 0]) + _dot_tn(dscat[pp, hp, 1], q2[pp, hp, 1])
            dv_ref[pp, :, _lanes(hp)] = jnp.where(first, dv2[:BLOCK], dv2[BLOCK:]).astype(BF16)
            dk_ref[pp, :, _lanes(hp)] = jnp.where(first, dk2[:BLOCK], dk2[BLOCK:]).astype(BF16)
            dq_ref[pp, :, _lanes(hp)] = (carry[pp, :, _lanes(hp)]
                                         + _dot(dscat[pp, hp, 0], kcat[pp, hp]) * scale).astype(BF16)
            carry[pp, :, _lanes(hp)] = _dot(dscat[pp, hp, 1], kcat[pp, hp]) * scale
        pl.when(step == (r // PL) * nb - 1)(finish)

    cur = pl.BlockSpec((PL, BLOCK, GROUP), lambda c, n: (c, n, 0))
    nxt = pl.BlockSpec((PL, BLOCK, GROUP), lambda c, n: (c, jnp.minimum(n + 1, nb - 1), 0))
    return pl.pallas_call(
        body, name="dil_bwd_r%d" % r, grid=(r // PL, nb),
        in_specs=[cur, nxt, cur, nxt, cur, nxt, cur, nxt, cur, cur, HBM], out_specs=[cur, cur, cur, HBM],
        out_shape=[jax.ShapeDtypeStruct(q.shape, BF16)] * 3 + [jax.ShapeDtypeStruct(part.shape, part.dtype)],
        scratch_shapes=[pltpu.VMEM((PL, BLOCK, GROUP), F32), pltpu.SemaphoreType.DMA((3,)),
                        pltpu.SemaphoreType.DMA((3,))],
    )(q, q, do, do, lse, lse, delta, delta, k, v, part)


SB_TILES = 2
SB_PAIRS_FWD = 4
SB_PAIRS_BWD = 2
SB_DEAD = -110.0


def _lanes(hp):
    return slice(hp * LANES, (hp + 1) * LANES)


def _sb_logits(z, valid):
    e = jnp.exp(-jnp.abs(z))
    lb = jnp.minimum(z, 0.0) - jnp.log(1.0 + e)
    lk = lb - z
    if valid is not None:
        lk = jnp.where(valid, lk, 0.0)
    return e, lb, lk


def _by_head(t, first):
    zero = jnp.zeros_like(t)
    return jnp.concatenate([jnp.where(first, t, zero), jnp.where(first, zero, t)], axis=0)


def _sb_valid(i, j):
    rowi = lax.broadcasted_iota(jnp.int32, (BLOCK, BLOCK), 0)
    coli = lax.broadcasted_iota(jnp.int32, (BLOCK, BLOCK), 1)
    return (coli - rowi) < (i - j) * BLOCK


def _scaled(q):
    return (q.astype(F32) * (HEAD_DIM ** -0.5)).astype(BF16)


def _hosted_gathers(refs_in, refs_out, send, recv, step, steps):
    plans = [_gather_plan(refs_in[a], refs_out[a], send.at[pl.ds(6 * a, 6)], recv.at[pl.ds(6 * a, 6)])
             for a in range(len(refs_in))]
    for stage, at in ((0, 0), (1, (2 * steps) // 3)):
        @pl.when(step == at)
        def _():
            for plan in plans:
                plan[stage]()

    def finish():
        @pl.when(step == steps - 1)
        def _():
            for plan in plans:
                plan[2]()

    return finish


def _hosted_specs(slots):
    n = len(slots)
    sems = [pltpu.SemaphoreType.DMA((6 * n,))] * 2 if n else []
    return [HBM] * n, [HBM] * n, [jax.ShapeDtypeStruct(s.shape, s.dtype) for s in slots], sems


def _sb_fwd(qs, ks, vs, tri_later, slots):
    S = qs.shape[0]
    P = SB_PAIRS_FWD
    W = P * LANES
    ns = len(slots)

    def body(q_ref, k_ref, v_ref, u_ref, *rest):
        o_ref, lt_ref, from_ref = rest[ns:ns + 3]
        i = pl.program_id(1)
        finish = _hosted_gathers(rest[:ns], rest[ns + 3:2 * ns + 3], *rest[2 * ns + 3:], i, S // BLOCK) if ns else None
        first = lax.broadcasted_iota(jnp.int32, (BLOCK, LANES), 1) < HEAD_DIM
        q2 = [_scaled(q_ref[:, _lanes(hp)]) for hp in range(P)]

        def walk(tiles, carry):
            runs, accs = list(carry[0]), list(carry[1])
            units = [(t, hp) for t in range(len(tiles)) for hp in range(P)]
            offs = [pl.multiple_of(j * BLOCK, BLOCK) for j, _ in tiles]
            valids = [_sb_valid(i, j) if diag else None for j, diag in tiles]
            z2s, lbs, c2s = {}, {}, {}
            for t, hp in units:
                z2s[t, hp] = _dot_nt(q2[hp], _by_head(k_ref[pl.ds(offs[t], BLOCK), _lanes(hp)], first))
            for t, hp in units:
                for h in range(2):
                    _, lb, lk = _sb_logits(z2s[t, hp][:, h * BLOCK:(h + 1) * BLOCK], valids[t])
                    lbs[t, hp, h] = lb
                    c2s[t, hp, h] = _dot(jnp.concatenate(_split(lk), axis=1), u_ref[...])
            for t, hp in units:
                a2 = []
                for h in range(2):
                    a = jnp.exp(lbs[t, hp, h] + c2s[t, hp, h][:, :BLOCK] + runs[2 * hp + h])
                    if valids[t] is not None:
                        a = jnp.where(valids[t], a, 0.0)
                    a2.append(a.astype(BF16))
                    runs[2 * hp + h] = runs[2 * hp + h] + c2s[t, hp, h][:, BLOCK:]
                vcat = _by_head(v_ref[pl.ds(offs[t], BLOCK), _lanes(hp)], first)
                accs[hp] = accs[hp] + _dot(jnp.concatenate(a2, axis=1), vcat)
            return tuple(runs), tuple(accs)

        def chunk(ci, carry):
            return walk([(ci * SB_TILES + t, False) for t in reversed(range(SB_TILES))], carry)

        def alive(runs):
            top = functools.reduce(jnp.maximum, runs)
            return (jnp.max(top) > SB_DEAD).astype(jnp.int32)

        def step(c):
            t, _, runs, accs = c
            runs, accs = chunk(nfull - 1 - t, (runs, accs))
            return t + 1, alive(runs), runs, accs

        zero = jnp.zeros((BLOCK, LANES), F32)
        nfull = i // SB_TILES
        ragged = [functools.partial(walk, [(i, True)] + [(i - 1 - m, False) for m in range(extra)])
                  for extra in range(SB_TILES)]
        runs, accs = lax.switch(i % SB_TILES, ragged, ((zero,) * (2 * P), (zero,) * P))
        done, _, runs, accs = lax.while_loop(lambda c: jnp.logical_and(c[0] < nfull, c[1] > 0), step,
                                             (jnp.int32(0), alive(runs), runs, accs))
        for hp in range(P):
            o_ref[:, _lanes(hp)] = accs[hp]
            lt_ref[:, _lanes(hp)] = jnp.where(first, runs[2 * hp], runs[2 * hp + 1])
        from_ref[...] = jnp.full(from_ref.shape, nfull - done, jnp.int32)
        if finish is not None:
            finish()

    assert W == GROUP
    blk = pl.BlockSpec((BLOCK, W), lambda hp, i: (i, hp))
    col = pl.BlockSpec((S, W), lambda hp, i: (0, hp))
    h_in, h_out, h_shape, h_sems = _hosted_specs(slots)
    return pl.pallas_call(
        body, name="sb_fwd", grid=(GROUP // W, S // BLOCK),
        in_specs=[blk, col, col, _full((2 * BLOCK, 2 * BLOCK))] + h_in,
        out_specs=[blk, blk, pl.BlockSpec((1, 8, LANES), lambda hp, i: (i, 0, 0))] + h_out,
        out_shape=[jax.ShapeDtypeStruct((S, GROUP), F32)] * 2
                  + [jax.ShapeDtypeStruct((S // BLOCK, 8, LANES), jnp.int32)] + h_shape,
        input_output_aliases={4 + a: 3 + a for a in range(ns)},
        scratch_shapes=h_sems,
    )(qs, ks, vs, tri_later, *slots)


def _sb_bwd(first_chunk, qs, ks, vs, do, ltot, tri_upto, tri_before):
    S = qs.shape[0]
    P = SB_PAIRS_BWD
    W = P * LANES

    def body(from_ref, q_ref, k_ref, v_ref, do_ref, lt_ref, w_ref, x_ref, dq_ref, dk_ref, dv_ref):
        i = pl.program_id(1)

        @pl.when(i == 0)
        def _():
            dk_ref[...] = jnp.zeros_like(dk_ref)
            dv_ref[...] = jnp.zeros_like(dv_ref)

        first = lax.broadcasted_iota(jnp.int32, (BLOCK, LANES), 1) < HEAD_DIM
        q2 = [_scaled(q_ref[:, _lanes(hp)]) for hp in range(P)]
        do2 = [do_ref[:, _lanes(hp)] for hp in range(P)]
        totals = [jnp.broadcast_to(lt_ref[:, n * HEAD_DIM:n * HEAD_DIM + 1], (BLOCK, LANES)) for n in range(2 * P)]

        def walk(tiles, carry):
            keeps, grads, dqs = list(carry[0]), list(carry[1]), list(carry[2])
            units = [(t, hp) for t in range(len(tiles)) for hp in range(P)]
            offs = [pl.multiple_of(j * BLOCK, BLOCK) for j, _ in tiles]
            valids = [_sb_valid(i, j) if diag else None for j, diag in tiles]
            kcat, z2, da2, es, lbs, c2s, as_, des, p2s = {}, {}, {}, {}, {}, {}, {}, {}, {}
            for t, hp in units:
                kcat[t, hp] = _by_head(k_ref[pl.ds(offs[t], BLOCK), _lanes(hp)], first)
                z2[t, hp] = _dot_nt(q2[hp], kcat[t, hp])
                da2[t, hp] = _dot_nt(do2[hp], _by_head(v_ref[pl.ds(offs[t], BLOCK), _lanes(hp)], first))
            for t, hp in units:
                for h in range(2):
                    es[t, hp, h], lbs[t, hp, h], lk = _sb_logits(z2[t, hp][:, h * BLOCK:(h + 1) * BLOCK], valids[t])
                    c2s[t, hp, h] = _dot(jnp.concatenate(_split(lk), axis=1), w_ref[...])
            for t, hp in units:
                for h in range(2):
                    n = 2 * hp + h
                    a = jnp.exp(lbs[t, hp, h] + (totals[n] - (keeps[n] + c2s[t, hp, h][:, :BLOCK])))
                    if valids[t] is not None:
                        a = jnp.where(valids[t], a, 0.0)
                    keeps[n] = keeps[n] + c2s[t, hp, h][:, BLOCK:]
                    de = a * da2[t, hp][:, h * BLOCK:(h + 1) * BLOCK]
                    as_[t, hp, h], des[t, hp, h] = a.astype(BF16), de
                    p2s[t, hp, h] = _dot(jnp.concatenate(_split(de), axis=1), x_ref[...])
            for t, hp in units:
                dz2 = []
                for h in range(2):
                    n = 2 * hp + h
                    e = es[t, hp, h]
                    sig = jnp.where(z2[t, hp][:, h * BLOCK:(h + 1) * BLOCK] >= 0.0, 1.0, e) / (1.0 + e)
                    dz = des[t, hp, h] * (1.0 - sig) - (grads[n] + p2s[t, hp, h][:, :BLOCK]) * sig
                    if valids[t] is not None:
                        dz = jnp.where(valids[t], dz, 0.0)
                    grads[n] = grads[n] + p2s[t, hp, h][:, BLOCK:]
                    dz2.append(dz.astype(BF16))
                dzcat = jnp.concatenate(dz2, axis=1)
                dk2 = _dot_tn(dzcat, q2[hp])
                dv2 = _dot_tn(jnp.concatenate([as_[t, hp, 0], as_[t, hp, 1]], axis=1), do2[hp])
                dk_ref[pl.ds(offs[t], BLOCK), _lanes(hp)] += jnp.where(first, dk2[:BLOCK], dk2[BLOCK:])
                dv_ref[pl.ds(offs[t], BLOCK), _lanes(hp)] += jnp.where(first, dv2[:BLOCK], dv2[BLOCK:])
                dqs[hp] = dqs[hp] + _dot(dzcat, kcat[t, hp])
            return tuple(keeps), tuple(grads), tuple(dqs)

        zero = jnp.zeros((BLOCK, LANES), F32)
        nfull = i // SB_TILES
        carry = lax.fori_loop(
            from_ref[i], nfull, lambda ci, c: walk([(ci * SB_TILES + t, False) for t in range(SB_TILES)], c),
            ((zero,) * (2 * P), (zero,) * (2 * P), (zero,) * P))
        ragged = [functools.partial(walk, [(i - m, False) for m in range(extra, 0, -1)] + [(i, True)])
                  for extra in range(SB_TILES)]
        carry = lax.switch(i % SB_TILES, ragged, carry)
        for hp in range(P):
            dq_ref[:, _lanes(hp)] = carry[2][hp] * (HEAD_DIM ** -0.5)

    blk = pl.BlockSpec((BLOCK, W), lambda hp, i, fr: (i, hp))
    col = pl.BlockSpec((S, W), lambda hp, i, fr: (0, hp))
    tri = pl.BlockSpec((2 * BLOCK, 2 * BLOCK), lambda hp, i, fr: (0, 0))
    return pl.pallas_call(
        body, name="sb_bwd",
        grid_spec=pltpu.PrefetchScalarGridSpec(
            num_scalar_prefetch=1, grid=(GROUP // W, S // BLOCK),
            in_specs=[blk, col, col, blk, blk, tri, tri], out_specs=[blk, col, col]),
        out_shape=[jax.ShapeDtypeStruct((S, GROUP), F32)] * 3,
    )(first_chunk, qs, ks, vs, do, ltot, tri_upto, tri_before)


def _out_proj_fwd(o_br, l_br, o_sb, x, w_dil, w_sbn, w_out_g):
    S, D = x.shape
    tm = 512

    def body(o0, o1, o2, l0, l1, l2, os_ref, x_ref, wd_ref, ws_ref, w_ref, od_ref, s0, s1, s2, x1_ref, scr):
        ls = [_from_strided(scr, l) for l in (l0, l1, l2)]
        os_ = [_from_strided(scr, o) for o in (o0, o1, o2)]
        m = jnp.maximum(jnp.maximum(ls[0], ls[1]), ls[2])
        es = [jnp.exp(l - m) for l in ls]
        den = es[0] + es[1] + es[2]
        od = (es[0] * os_[0] + es[1] * os_[1] + es[2] * os_[2]) / den
        od_ref[...] = od
        _to_strided(scr, m + jnp.log(den), list(zip(DILATIONS, (s0, s1, s2))))
        osb = os_ref[...]
        mixed = jnp.concatenate([(od * _rms(od) * wd_ref[...]).astype(BF16),
                                 (osb * _rms(osb) * ws_ref[...]).astype(BF16)], axis=1)
        x1_ref[...] = x_ref[...] + _dot(mixed, w_ref[...])

    row = lambda i: (i, 0)
    g = pl.BlockSpec((tm, GROUP), row)
    d = pl.BlockSpec((tm, D), row)
    planes = [_strided_spec(tm, r) for r in DILATIONS]
    return pl.pallas_call(
        body, name="out_proj_fwd", grid=(S // tm,),
        in_specs=planes * 2 + [g, d, _full((1, GROUP)), _full((1, GROUP)), _full((2 * GROUP, D))],
        out_specs=[g] + planes + [d],
        out_shape=[jax.ShapeDtypeStruct((S, GROUP), F32)] + [_strided_shape(S, r, F32) for r in DILATIONS]
                  + [jax.ShapeDtypeStruct((S, D), F32)],
        scratch_shapes=[_strided_scratch(tm)],
    )(*o_br, *l_br, o_sb, x, w_dil, w_sbn, w_out_g)


def _ffn_fwd(x1, target, ffn_w, wg_g, wu_g, wd_g):
    S, D = x1.shape
    F = wg_g.shape[1]
    tm = 512
    nt = S // tm

    def body(x_ref, t_ref, nw_ref, wg_ref, wu_ref, wd_ref, h_ref, g_ref, u_ref, dy_ref, loss_ref, h_s, acc):
        j = pl.program_id(1)

        @pl.when(j == 0)
        def _():
            xv = x_ref[...]
            h = (xv * _rms(xv) * nw_ref[...]).astype(BF16)
            h_s[...] = h
            h_ref[...] = h
            acc[...] = xv

        h = h_s[...]
        g = _dot_nt(h, wg_ref[0])
        u = _dot_nt(h, wu_ref[0])
        g_ref[0] = g.astype(BF16)
        u_ref[0] = u.astype(BF16)
        a = (g * _sigmoid(g) * u).astype(BF16)
        acc[...] += _dot(a, wd_ref[0])

        @pl.when(j == N_CHIPS - 1)
        def _():
            err = acc[...] - t_ref[...]
            dy_ref[...] = err * (1.0 / D)
            loss_ref[...] = jnp.full(loss_ref.shape, jnp.sum(err * err), F32)

    row = lambda t, j: (t, 0)
    shard = lambda t, j: (j, 0, 0)
    act = lambda t, j: (j, t, 0)
    return pl.pallas_call(
        body, name="ffn_fwd", grid=(nt, N_CHIPS),
        in_specs=[pl.BlockSpec((tm, D), row), pl.BlockSpec((tm, D), row), pl.BlockSpec((1, D), lambda t, j: (0, 0))]
                 + [pl.BlockSpec((1, F, D), shard)] * 3,
        out_specs=[pl.BlockSpec((tm, D), row), pl.BlockSpec((1, tm, F), act), pl.BlockSpec((1, tm, F), act),
                   pl.BlockSpec((tm, D), row), pl.BlockSpec((1, 8, LANES), lambda t, j: (t, 0, 0))],
        out_shape=[jax.ShapeDtypeStruct((S, D), BF16), jax.ShapeDtypeStruct((N_CHIPS, S, F), BF16),
                   jax.ShapeDtypeStruct((N_CHIPS, S, F), BF16), jax.ShapeDtypeStruct((S, D), F32),
                   jax.ShapeDtypeStruct((nt, 8, LANES), F32)],
        scratch_shapes=[pltpu.VMEM((tm, D), BF16), pltpu.VMEM((tm, D), F32)],
    )(x1, target, ffn_w, wg_g, wu_g, wd_g)


def _ffn_bwd(h2, dy, g, u, wg_g, wu_g, wd_g):
    S, D = dy.shape
    F = wg_g.shape[1]
    tm = 512

    def body(h_ref, dy_ref, g_ref, u_ref, wg_ref, wu_ref, wd_ref, dwg_ref, dwu_ref, dwd_ref, dh_ref):
        t = pl.program_id(1)

        @pl.when(t == 0)
        def _():
            dwg_ref[...] = jnp.zeros_like(dwg_ref)
            dwu_ref[...] = jnp.zeros_like(dwu_ref)
            dwd_ref[...] = jnp.zeros_like(dwd_ref)

        h = h_ref[...]
        dyb = dy_ref[...].astype(BF16)
        gv = g_ref[0].astype(F32)
        uv = u_ref[0].astype(F32)
        da = _dot_nt(dyb, wd_ref[0])
        sg = _sigmoid(gv)
        silu = gv * sg
        du = (da * silu).astype(BF16)
        dg = (da * uv * (sg * (1.0 + gv * (1.0 - sg)))).astype(BF16)
        dwd_ref[0] += _dot_tn((silu * uv).astype(BF16), dyb)
        dwg_ref[0] += _dot_tn(dg, h)
        dwu_ref[0] += _dot_tn(du, h)
        dh_ref[0] = (_dot(dg, wg_ref[0]) + _dot(du, wu_ref[0])).astype(BF16)

    row = lambda j, t: (t, 0)
    shard = lambda j, t: (j, 0, 0)
    act = lambda j, t: (j, t, 0)
    return pl.pallas_call(
        body, name="ffn_bwd", grid=(N_CHIPS, S // tm),
        in_specs=[pl.BlockSpec((tm, D), row), pl.BlockSpec((tm, D), row),
                  pl.BlockSpec((1, tm, F), act), pl.BlockSpec((1, tm, F), act)] + [pl.BlockSpec((1, F, D), shard)] * 3,
        out_specs=[pl.BlockSpec((1, F, D), shard)] * 3 + [pl.BlockSpec((1, tm, D), act)],
        out_shape=[jax.ShapeDtypeStruct((N_CHIPS, F, D), F32)] * 3 + [jax.ShapeDtypeStruct((N_CHIPS, S, D), BF16)],
    )(h2, dy, g, u, wg_g, wu_g, wd_g)


def _out_proj_bwd(dh2p, dy, x1, ffn_w, w_out_g, o_dil, o_sb, w_dil, w_sbn, seg_ones, ffn_grads):
    S, D = dy.shape
    tm = 256
    ng = len(ffn_grads)

    def body(dh_ref, dy_ref, x1_ref, nw_ref, w_ref, od_ref, os_ref, wd_ref, ws_ref, g_ref, *rest):
        gin, rest = rest[:ng], rest[ng:]
        dx1_ref, dod0, dod1, dod2, dos_ref, dl0, dl1, dl2, dw_ref, dnw_ref, dwd_ref, dws_ref = rest[:12]
        gout, (scr, send, recv) = rest[12:12 + ng], rest[12 + ng:]
        i = pl.program_id(0)
        plans = [_pair_send_plan(gin[a], gout[a], send.at[a], recv.at[a]) for a in range(ng)]

        @pl.when(i == 0)
        def _():
            for start, _ in plans:
                start()

        @pl.when(i == 0)
        def _():
            for r_ in (dw_ref, dnw_ref, dwd_ref, dws_ref):
                r_[...] = jnp.zeros_like(r_)

        dh2 = _sum4(dh_ref)
        dxn, dwn = _rms_bwd(dh2, x1_ref[...], nw_ref[...])
        dnw_ref[...] += jnp.sum(dwn, axis=0, keepdims=True)
        dx1 = dy_ref[...] + dxn
        dx1_ref[...] = dx1
        dx1b = dx1.astype(BF16)
        dmix = _dot_nt(dx1b, w_ref[...])
        od = od_ref[...]
        osb = os_ref[...]
        mixed = jnp.concatenate([(od * _rms(od) * wd_ref[...]).astype(BF16),
                                 (osb * _rms(osb) * ws_ref[...]).astype(BF16)], axis=1)
        dw_ref[...] += _dot_tn(mixed, dx1b)
        do, dwo = _rms_bwd(dmix[:, :GROUP], od, wd_ref[...])
        dwd_ref[...] += jnp.sum(dwo, axis=0, keepdims=True)
        _to_strided(scr, do, list(zip(DILATIONS, (dod0, dod1, dod2))))
        _to_strided(scr, _segsum(do * od, g_ref[...]), list(zip(DILATIONS, (dl0, dl1, dl2))))
        do, dwo = _rms_bwd(dmix[:, GROUP:], osb, ws_ref[...])
        dws_ref[...] += jnp.sum(dwo, axis=0, keepdims=True)
        dos_ref[...] = do.astype(BF16)

        @pl.when(i == S // tm - 1)
        def _():
            for _, finish in plans:
                finish()

    row = lambda i: (i, 0)
    gsp = pl.BlockSpec((tm, GROUP), row)
    dsp = pl.BlockSpec((tm, D), row)
    planes = [_strided_spec(tm, r) for r in DILATIONS]
    halves = [jax.ShapeDtypeStruct((g.shape[0], g.shape[1] // 2, g.shape[2]), g.dtype) for g in ffn_grads]
    return pl.pallas_call(
        body, name="out_proj_bwd", grid=(S // tm,),
        in_specs=[pl.BlockSpec((N_CHIPS, tm, D), lambda i: (0, i, 0)), dsp, dsp, _full((1, D)), _full((2 * GROUP, D)),
                  gsp, gsp, _full((1, GROUP)), _full((1, GROUP)), _full((GROUP, GROUP))] + [HBM] * ng,
        out_specs=[dsp] + planes + [gsp] + planes
                  + [_full((2 * GROUP, D)), _full((1, D)), _full((1, GROUP)), _full((1, GROUP))] + [HBM] * ng,
        out_shape=[jax.ShapeDtypeStruct((S, D), F32)] + [_strided_shape(S, r, BF16) for r in DILATIONS]
                  + [jax.ShapeDtypeStruct((S, GROUP), BF16)] + [_strided_shape(S, r, F32) for r in DILATIONS]
                  + [jax.ShapeDtypeStruct((2 * GROUP, D), F32),
                     jax.ShapeDtypeStruct((1, D), F32), jax.ShapeDtypeStruct((1, GROUP), F32),
                     jax.ShapeDtypeStruct((1, GROUP), F32)] + halves,
        scratch_shapes=[_strided_scratch(tm), pltpu.SemaphoreType.DMA((ng,)), pltpu.SemaphoreType.DMA((ng,))],
    )(dh2p, dy, x1, ffn_w, w_out_g, o_dil, o_sb, w_dil, w_sbn, seg_ones, *ffn_grads)


def _qk_bwd(dq_br, dk_br, dv_br, dqs, dks, dvs, qa, ka, qw, kw, cos_t, sin_t, seg_ones):
    S = qa.shape[0]
    tm = 256

    def body(q0, q1, q2, k0, k1, k2, v0, v1, v2, dqs_ref, dks_ref, dvs_ref, qa_ref, ka_ref, qw_ref, kw_ref,
             cos_ref, sin_ref, g_ref, dp_ref, dqw_ref, dkw_ref, accq, acck, scr):
        i = pl.program_id(0)

        @pl.when(i == 0)
        def _():
            accq[...] = jnp.zeros_like(accq)
            acck[...] = jnp.zeros_like(acck)

        def branches(refs):
            return (_from_strided(scr, refs[0]) + _from_strided(scr, refs[1])) + _from_strided(scr, refs[2])

        g = g_ref[...]
        cos = _tile4(cos_ref[...])
        sin = _tile4(sin_ref[...])
        for b, (refs, pre_ref, w_ref, acc) in enumerate((((q0, q1, q2), qa_ref, qw_ref, accq),
                                                        ((k0, k1, k2), ka_ref, kw_ref, acck))):
            dh = branches(refs)
            dn = dh * cos + _rot_half(dh * sin)
            pre = pre_ref[...]
            rstd = lax.rsqrt(_segsum(pre * pre, g) * (1.0 / HEAD_DIM) + EPS)
            xh = pre * rstd
            acc[...] += jnp.sum(dn * xh, axis=0, keepdims=True)
            dxh = dn * w_ref[...]
            dpre = rstd * (dxh - xh * (_segsum(dxh * xh, g) * (1.0 / HEAD_DIM)))
            dp_ref[:, b * GROUP:(b + 1) * GROUP] = dpre.astype(BF16)
        dp_ref[:, 2 * GROUP:3 * GROUP] = branches((v0, v1, v2)).astype(BF16)
        dp_ref[:, 3 * GROUP:4 * GROUP] = dqs_ref[...].astype(BF16)
        dp_ref[:, 4 * GROUP:5 * GROUP] = dks_ref[...].astype(BF16)
        dp_ref[:, 5 * GROUP:6 * GROUP] = dvs_ref[...].astype(BF16)

        @pl.when(i == S // tm - 1)
        def _():
            for acc, o_ref in ((accq, dqw_ref), (acck, dkw_ref)):
                a = acc[...]
                pair = (a[:, 0:LANES] + a[:, LANES:2 * LANES]) + (a[:, 2 * LANES:3 * LANES] + a[:, 3 * LANES:4 * LANES])
                o_ref[...] = pair + pltpu.roll(pair, HEAD_DIM, 1)

    row = lambda i: (i, 0)
    gsp = pl.BlockSpec((tm, GROUP), row)
    tab = pl.BlockSpec((tm, LANES), row)
    planes = [_strided_spec(tm, r) for r in DILATIONS]
    return pl.pallas_call(
        body, name="qk_bwd", grid=(S // tm,),
        in_specs=planes * 3 + [gsp] * 5 + [_full((1, GROUP)), _full((1, GROUP)), tab, tab, _full((GROUP, GROUP))],
        out_specs=[pl.BlockSpec((tm, 6 * GROUP), row), _full((1, LANES)), _full((1, LANES))],
        out_shape=[jax.ShapeDtypeStruct((S, 6 * GROUP), BF16), jax.ShapeDtypeStruct((1, LANES), F32),
                   jax.ShapeDtypeStruct((1, LANES), F32)],
        scratch_shapes=[pltpu.VMEM((1, GROUP), F32), pltpu.VMEM((1, GROUP), F32), _strided_scratch(tm)],
    )(*dq_br, *dk_br, *dv_br, dqs, dks, dvs, qa, ka, qw, kw, cos_t, sin_t, seg_ones)


def _attn_in_bwd(dq_br, dk_br, dv_br, dqs, dks, dvs, qa, ka, qw, kw, cos_t, sin_t, seg_ones, h, w_in_g, x, dx1, attn_w):
    S, D = x.shape
    wc = w_in_g.shape[2]
    tm = 256

    def body(q0, q1, q2, k0, k1, k2, v0, v1, v2, dqs_ref, dks_ref, dvs_ref, qa_ref, ka_ref, qw_ref, kw_ref,
             cos_ref, sin_ref, g_ref, h_ref, w_ref, x_ref, dx1_ref, aw_ref,
             gx_ref, dw_ref, daw_ref, dqw_ref, dkw_ref, accq, acck, scr):
        i = pl.program_id(0)

        @pl.when(i == 0)
        def _():
            accq[...] = jnp.zeros_like(accq)
            acck[...] = jnp.zeros_like(acck)
            dw_ref[...] = jnp.zeros_like(dw_ref)
            daw_ref[...] = jnp.zeros_like(daw_ref)

        def branches(refs):
            return (_from_strided(scr, refs[0]) + _from_strided(scr, refs[1])) + _from_strided(scr, refs[2])

        g = g_ref[...]
        cos = _tile4(cos_ref[...])
        sin = _tile4(sin_ref[...])
        pieces = []
        for refs, pre_ref, w_r, acc in (((q0, q1, q2), qa_ref, qw_ref, accq), ((k0, k1, k2), ka_ref, kw_ref, acck)):
            dh = branches(refs)
            dn = dh * cos + _rot_half(dh * sin)
            pre = pre_ref[...]
            rstd = lax.rsqrt(_segsum(pre * pre, g) * (1.0 / HEAD_DIM) + EPS)
            xh = pre * rstd
            acc[...] += jnp.sum(dn * xh, axis=0, keepdims=True)
            dxh = dn * w_r[...]
            pieces.append((rstd * (dxh - xh * (_segsum(dxh * xh, g) * (1.0 / HEAD_DIM)))).astype(BF16))
        pieces += [branches((v0, v1, v2)).astype(BF16), dqs_ref[...].astype(BF16), dks_ref[...].astype(BF16),
                   dvs_ref[...].astype(BF16)]
        dproj = jnp.concatenate(pieces, axis=1)
        hv = h_ref[...]
        dh = jnp.zeros((tm, D), F32)
        for j in range(N_CHIPS):
            dp = dproj[:, j * wc:(j + 1) * wc]
            dw_ref[j] += _dot_tn(hv, dp)
            dh = dh + _dot_nt(dp, w_ref[j])
        dx, dw = _rms_bwd(dh, x_ref[...], aw_ref[...])
        daw_ref[...] += jnp.sum(dw, axis=0, keepdims=True)
        gx_ref[...] = dx1_ref[...] + dx

        @pl.when(i == S // tm - 1)
        def _():
            for acc, o_ref in ((accq, dqw_ref), (acck, dkw_ref)):
                a = acc[...]
                pair = (a[:, 0:LANES] + a[:, LANES:2 * LANES]) + (a[:, 2 * LANES:3 * LANES] + a[:, 3 * LANES:4 * LANES])
                o_ref[...] = pair + pltpu.roll(pair, HEAD_DIM, 1)

    row = lambda i: (i, 0)
    gsp = pl.BlockSpec((tm, GROUP), row)
    dsp = pl.BlockSpec((tm, D), row)
    tab = pl.BlockSpec((tm, LANES), row)
    planes = [_strided_spec(tm, r) for r in DILATIONS]
    return pl.pallas_call(
        body, name="attn_in_bwd", grid=(S // tm,),
        in_specs=planes * 3 + [gsp] * 5 + [_full((1, GROUP)), _full((1, GROUP)), tab, tab, _full((GROUP, GROUP)),
                                          dsp, _full((N_CHIPS, D, wc)), dsp, dsp, _full((1, D))],
        out_specs=[dsp, _full((N_CHIPS, D, wc)), _full((1, D)), _full((1, LANES)), _full((1, LANES))],
        out_shape=[jax.ShapeDtypeStruct((S, D), F32), jax.ShapeDtypeStruct((N_CHIPS, D, wc), F32),
                   jax.ShapeDtypeStruct((1, D), F32), jax.ShapeDtypeStruct((1, LANES), F32),
                   jax.ShapeDtypeStruct((1, LANES), F32)],
        scratch_shapes=[pltpu.VMEM((1, GROUP), F32), pltpu.VMEM((1, GROUP), F32), _strided_scratch(tm)],
    )(*dq_br, *dk_br, *dv_br, dqs, dks, dvs, qa, ka, qw, kw, cos_t, sin_t, seg_ones, h, w_in_g, x, dx1, attn_w)


def _in_proj_bwd(h, dproj, w_in_g):
    S, D = h.shape
    wc = w_in_g.shape[2]
    tm = 512

    def body(h_ref, dp_ref, w_ref, dw_ref, dh_ref):
        t = pl.program_id(1)

        @pl.when(t == 0)
        def _():
            dw_ref[...] = jnp.zeros_like(dw_ref)

        dp = dp_ref[...]
        dw_ref[0] += _dot_tn(h_ref[...], dp)
        dh_ref[0] = _dot_nt(dp, w_ref[0]).astype(BF16)

    return pl.pallas_call(
        body, name="in_proj_bwd", grid=(N_CHIPS, S // tm),
        in_specs=[pl.BlockSpec((tm, D), lambda j, t: (t, 0)), pl.BlockSpec((tm, wc), lambda j, t: (t, j)),
                  pl.BlockSpec((1, D, wc), lambda j, t: (j, 0, 0))],
        out_specs=[pl.BlockSpec((1, D, wc), lambda j, t: (j, 0, 0)), pl.BlockSpec((1, tm, D), lambda j, t: (j, t, 0))],
        out_shape=[jax.ShapeDtypeStruct((N_CHIPS, D, wc), F32), jax.ShapeDtypeStruct((N_CHIPS, S, D), BF16)],
    )(h, dproj, w_in_g)


def _in_norm_bwd(dhp, dx1, x, attn_w):
    S, D = x.shape
    tm = 512

    def body(dh_ref, dx1_ref, x_ref, w_ref, gx_ref, dw_ref):
        i = pl.program_id(0)

        @pl.when(i == 0)
        def _():
            dw_ref[...] = jnp.zeros_like(dw_ref)

        dh = _sum4(dh_ref)
        dx, dw = _rms_bwd(dh, x_ref[...], w_ref[...])
        dw_ref[...] += jnp.sum(dw, axis=0, keepdims=True)
        gx_ref[...] = dx1_ref[...] + dx

    row = lambda i: (i, 0)
    dsp = pl.BlockSpec((tm, D), row)
    return pl.pallas_call(
        body, name="in_norm_bwd", grid=(S // tm,),
        in_specs=[pl.BlockSpec((N_CHIPS, tm, D), lambda i: (0, i, 0)), dsp, dsp, _full((1, D))],
        out_specs=[dsp, _full((1, D))],
        out_shape=[jax.ShapeDtypeStruct((S, D), F32), jax.ShapeDtypeStruct((1, D), F32)],
    )(dhp, dx1, x, attn_w)


def _constants(S):
    pos = jnp.arange(S, dtype=F32)
    inv_freq = ROPE_THETA ** (-jnp.arange(0, HEAD_DIM, 2, dtype=F32) / HEAD_DIM)
    ang = pos[:, None] * inv_freq[None, :]
    cos, sin = jnp.cos(ang), jnp.sin(ang)
    cos_t = jnp.concatenate([cos, cos] * 2, axis=1)
    sin_t = jnp.concatenate([-sin, sin] * 2, axis=1)
    idx = jnp.arange(GROUP)
    seg_ones = (idx[:, None] // HEAD_DIM == idx[None, :] // HEAD_DIM).astype(BF16)
    r = jnp.arange(BLOCK)
    ones = jnp.ones((BLOCK, BLOCK), BF16)
    tris = [jnp.concatenate([jnp.concatenate([m.astype(BF16), ones], axis=1)] * 2, axis=0) for m in
            (r[:, None] > r[None, :],
             r[:, None] <= r[None, :],
             r[:, None] < r[None, :])]
    return cos_t, sin_t, seg_ones, tris


FFN_NAMES = ("w_gate", "w_up", "w_down")


def _device_step(x, target, attn_w, qn_w, kn_w, dil_w, sbn_w, ffn_w, w_in_g, w_out_slots, ffn_slots, core, chip):
    S = x.shape[0]
    cos_t, sin_t, seg_ones, (tri_later, tri_upto, tri_before) = _constants(S)
    reps = GROUP // HEAD_DIM
    qw = jnp.tile(qn_w, (1, reps))
    kw = jnp.tile(kn_w, (1, reps))

    nd = len(DILATIONS)
    h, qa, ka, *rest = _in_proj_fwd(x, attn_w, w_in_g, qw, kw, cos_t, sin_t, seg_ones)
    qh, kh, va, (qs, ks, vs) = rest[:nd], rest[nd:2 * nd], rest[2 * nd:3 * nd], rest[3 * nd:]
    hosted = ([ffn_slots[0]], [w_out_slots], [])
    branches = [_dil_fwd(qh[b], kh[b], va[b], hosted[b]) for b in range(nd)]
    wg_g, w_out_g = branches[0][2], branches[1][2].reshape(-1, x.shape[1])
    o_sb, ltot, walked, wu_g, wd_g = _sb_fwd(qs, ks, vs, tri_later, ffn_slots[1:])
    o_dil, *lse, x1 = _out_proj_fwd([b[0] for b in branches], [b[1] for b in branches], o_sb, x, dil_w, sbn_w, w_out_g)
    h2, g, u, dy, loss_parts = _ffn_fwd(x1, target, ffn_w, wg_g, wu_g, wd_g)

    *ffn_grads, dh2p = _ffn_bwd(h2, dy, g, u, wg_g, wu_g, wd_g)
    dx1, *mid, dw_out, dffn_w, ddil_w, dsbn_w, p0, p1, p2 = _out_proj_bwd(
        dh2p, dy, x1, ffn_w, w_out_g, o_dil, o_sb, dil_w, sbn_w, seg_ones, ffn_grads)
    do_dil, do_sb, delta = mid[:nd], mid[nd], mid[nd + 1:]
    parts = [_pair_sum(gr, fr, core, n) for gr, fr, n in zip(ffn_grads, (p0, p1, p2), FFN_NAMES)]
    dqs, dks, dvs = _sb_bwd(walked[:, 0, 0], qs, ks, vs, do_sb, ltot, tri_upto, tri_before)
    dbr = [_dil_bwd(qh[b], kh[b], va[b], do_dil[b], lse[b], delta[b], parts[b]) for b in range(nd)]
    ffn_halves = [_chip_sum(dbr[b][3], parts[b], chip, FFN_NAMES[b]) for b in range(nd)]
    grad_x, dw_in, dattn_w, dqw, dkw = _attn_in_bwd(
        [b[0] for b in dbr], [b[1] for b in dbr], [b[2] for b in dbr], dqs, dks, dvs,
        qa, ka, qw, kw, cos_t, sin_t, seg_ones, h, w_in_g, x, dx1, attn_w)
    small = dict(attn=dattn_w, q=dqw[:, :HEAD_DIM], k=dkw[:, :HEAD_DIM], dil=ddil_w, sb=dsbn_w, ffn=dffn_w)
    return loss_parts, grad_x, small, dw_in, dw_out, ffn_halves


HBM = pl.BlockSpec(memory_space=pltpu.HBM)
VMEM = pl.BlockSpec(memory_space=pltpu.VMEM)
CHIP_FLIPS = ((1, 0), (0, 1), (1, 1))


def _place():
    return lax.axis_index("x"), lax.axis_index("y"), lax.axis_index("c")


def _flip(v, d):
    return 1 - v if d else v


def _half_rows(c, n):
    return pl.ds(pl.multiple_of(c * (n // 2), 16), n // 2)


def _gather_plan(slot_in, slot_out, send, recv):
    x, y, c = _place()
    p = 2 * x + y
    chips = [(_flip(x, dx), _flip(y, dy)) for dx, dy in CHIP_FLIPS]
    mine, other = _half_rows(c, slot_in.shape[1]), _half_rows(1 - c, slot_in.shape[1])

    def copy(k, src, dst, to):
        return pltpu.make_async_remote_copy(src_ref=src, dst_ref=dst, send_sem=send.at[k], recv_sem=recv.at[k],
                                            device_id=to, device_id_type=MESH)

    def first(k):
        return copy(k, slot_in.at[p, mine], slot_out.at[p, mine], (*chips[k], c))

    def passed(k, rows):
        land = slot_out.at[2 * chips[k][0] + chips[k][1], rows]
        return copy(3 + k, land, land, (x, y, 1 - c))

    def start():
        for k in range(3):
            first(k).start()

    def forward():
        for k in range(3):
            land = slot_out.at[2 * chips[k][0] + chips[k][1], mine]
            copy(k, land, land, (*chips[k], c)).wait_recv()
            passed(k, mine).start()

    def finish():
        for k in range(3):
            passed(k, other).wait_recv()
        for k in range(3):
            first(k).wait_send()
            passed(k, mine).wait_send()

    return start, forward, finish


def _chip_send_plan(part_in, recv_out, send, recv):
    x, y, c = _place()
    p = 2 * x + y
    chips = [(_flip(x, dx), _flip(y, dy)) for dx, dy in CHIP_FLIPS]

    def copy(k):
        q = 2 * chips[k][0] + chips[k][1]
        return pltpu.make_async_remote_copy(src_ref=part_in.at[q], dst_ref=recv_out.at[p], send_sem=send.at[k],
                                            recv_sem=recv.at[k], device_id=(*chips[k], c), device_id_type=MESH)

    def start():
        for k in range(3):
            copy(k).start()

    def finish():
        for k in range(3):
            land = recv_out.at[2 * chips[k][0] + chips[k][1]]
            pltpu.make_async_remote_copy(src_ref=land, dst_ref=land, send_sem=send.at[k], recv_sem=recv.at[k],
                                         device_id=(*chips[k], c), device_id_type=MESH).wait_recv()
        for k in range(3):
            copy(k).wait_send()

    return start, finish


def _pair_send_plan(grad_in, recv_out, send, recv):
    x, y, c = _place()

    def copy():
        theirs = _half_rows(1 - c, grad_in.shape[1])
        return pltpu.make_async_remote_copy(src_ref=grad_in.at[:, theirs, :], dst_ref=recv_out, send_sem=send,
                                            recv_sem=recv, device_id=(x, y, 1 - c), device_id_type=MESH)

    return (lambda: copy().start()), (lambda: copy().wait())


def _own_slots(shard):
    here = 2 * lax.axis_index("x") + lax.axis_index("y")
    return lax.dynamic_update_slice(lax.empty((N_CHIPS,) + shard.shape, shard.dtype), shard[None], (here, 0, 0))


def _gather_weights(shards):
    n = len(shards)

    def body(*refs):
        ins, outs = refs[:n], refs[n:2 * n]
        send, recv = refs[2 * n:]
        plans = [_gather_plan(ins[a], outs[a], send.at[pl.ds(6 * a, 6)], recv.at[pl.ds(6 * a, 6)]) for a in range(n)]
        for stage in range(3):
            for plan in plans:
                plan[stage]()

    slots = [_own_slots(s) for s in shards]
    return pl.pallas_call(
        body, name="gather_weights", in_specs=[HBM] * n, out_specs=[HBM] * n,
        out_shape=[jax.ShapeDtypeStruct(s.shape, s.dtype) for s in slots],
        input_output_aliases={a: a for a in range(n)},
        scratch_shapes=[pltpu.SemaphoreType.DMA((6 * n,)), pltpu.SemaphoreType.DMA((6 * n,))],
    )(*slots)


def _pair_exchange(grads, small):
    n = len(grads)

    def body(*refs):
        gin, sm = refs[:n], refs[n]
        gout, sm_all = refs[n + 1:2 * n + 1], refs[2 * n + 1]
        send, recv = refs[2 * n + 2:]
        x, y, c = _place()
        me = 4 * x + 2 * y + c
        big = [_pair_send_plan(gin[a], gout[a], send.at[a], recv.at[a]) for a in range(n)]
        for start, _ in big:
            start()
        sm_all[pl.ds(me, 1)] = sm[...][None]
        tiny = []
        for k in range(1, N_DEV):
            px, py, pc = _flip(x, k & 4), _flip(y, k & 2), _flip(c, k & 1)
            tiny.append((pltpu.make_async_remote_copy(
                src_ref=sm, dst_ref=sm_all.at[me], send_sem=send.at[n + k - 1], recv_sem=recv.at[n + k - 1],
                device_id=(px, py, pc), device_id_type=MESH), 4 * px + 2 * py + pc))
            tiny[-1][0].start()
        for k, (cp, peer) in enumerate(tiny):
            pltpu.make_async_remote_copy(src_ref=sm, dst_ref=sm_all.at[peer], send_sem=send.at[n + k],
                                         recv_sem=recv.at[n + k], device_id=(x, y, c),
                                         device_id_type=MESH).wait_recv()
            cp.wait_send()
        for _, finish in big:
            finish()

    halves = [jax.ShapeDtypeStruct((g.shape[0], g.shape[1] // 2, g.shape[2]), g.dtype) for g in grads]
    return pl.pallas_call(
        body, name="pair_exchange", in_specs=[HBM] * n + [VMEM], out_specs=[HBM] * n + [VMEM],
        out_shape=halves + [jax.ShapeDtypeStruct((N_DEV,) + small.shape, small.dtype)],
        scratch_shapes=[pltpu.SemaphoreType.DMA((n + N_DEV - 1,)), pltpu.SemaphoreType.DMA((n + N_DEV - 1,))],
    )(*grads, small)


def _chip_exchange(parts):
    n = len(parts)

    def body(*refs):
        pin, pout = refs[:n], refs[n:2 * n]
        send, recv = refs[2 * n:]
        plans = [_chip_send_plan(pin[a], pout[a], send.at[pl.ds(3 * a, 3)], recv.at[pl.ds(3 * a, 3)]) for a in range(n)]
        for stage in range(2):
            for plan in plans:
                plan[stage]()

    return pl.pallas_call(
        body, name="chip_exchange", in_specs=[HBM] * n, out_specs=[HBM] * n,
        out_shape=[jax.ShapeDtypeStruct(s.shape, s.dtype) for s in parts],
        scratch_shapes=[pltpu.SemaphoreType.DMA((3 * n,)), pltpu.SemaphoreType.DMA((3 * n,))],
    )(*parts)


def _pair_swap(halves):
    n = len(halves)

    def body(*refs):
        hin, hout = refs[:n], refs[n:2 * n]
        send, recv = refs[2 * n:]
        x, y, c = _place()
        swaps = [pltpu.make_async_remote_copy(src_ref=hin[a], dst_ref=hout[a], send_sem=send.at[a],
                                              recv_sem=recv.at[a], device_id=(x, y, 1 - c), device_id_type=MESH)
                 for a in range(n)]
        for cp in swaps:
            cp.start()
        for cp in swaps:
            cp.wait()

    return pl.pallas_call(
        body, name="pair_swap", in_specs=[HBM] * n, out_specs=[HBM] * n,
        out_shape=[jax.ShapeDtypeStruct(s.shape, s.dtype) for s in halves],
        scratch_shapes=[pltpu.SemaphoreType.DMA((n,)), pltpu.SemaphoreType.DMA((n,))],
    )(*halves)


def _pair_sum(grad, recv, c, tag):
    _, R, C = grad.shape
    hr = R // 2

    def body(c_ref, a_ref, b_ref, o_ref):
        o_ref[...] = (a_ref[...] + b_ref[...]).astype(BF16)

    return pl.pallas_call(
        body, name="pair_sum_" + tag,
        grid_spec=pltpu.PrefetchScalarGridSpec(
            num_scalar_prefetch=1, grid=(N_CHIPS,),
            in_specs=[pl.BlockSpec((1, hr, C), lambda s, cr: (s, cr[0], 0)),
                      pl.BlockSpec((1, hr, C), lambda s, cr: (s, 0, 0))],
            out_specs=pl.BlockSpec((1, hr, C), lambda s, cr: (s, 0, 0))),
        out_shape=jax.ShapeDtypeStruct((N_CHIPS, hr, C), BF16),
    )(c, grad, recv)


def _chip_sum(received, own, chip, tag):
    _, rows, C = received.shape
    tr = rows // 2

    def body(chip_ref, own_ref, r1_ref, r2_ref, r3_ref, o_ref):
        p = [r[0].astype(F32) for r in (own_ref, r1_ref, r2_ref, r3_ref)]
        o_ref[...] = (p[0] + p[1]) + (p[2] + p[3])

    def slot(k):
        return pl.BlockSpec((1, tr, C), lambda i, cr: (jnp.bitwise_xor(cr[0], k), i, 0))

    return pl.pallas_call(
        body, name="chip_sum_" + tag,
        grid_spec=pltpu.PrefetchScalarGridSpec(
            num_scalar_prefetch=1, grid=(rows // tr,), in_specs=[slot(0), slot(1), slot(2), slot(3)],
            out_specs=pl.BlockSpec((tr, C), lambda i, cr: (i, 0))),
        out_shape=jax.ShapeDtypeStruct((rows, C), F32),
    )(chip, own, received, received, received)


def _adamw_math(w, g, m, v):
    m = ADAM_B1 * m + (1.0 - ADAM_B1) * g
    v = ADAM_B2 * v + (1.0 - ADAM_B2) * (g * g)
    m_hat = m / (1.0 - ADAM_B1 ** ADAM_STEP)
    v_hat = v / (1.0 - ADAM_B2 ** ADAM_STEP)
    delta = -ADAM_LR * (m_hat / (jnp.sqrt(v_hat) + ADAM_EPS) + ADAM_WD * w)
    return delta, m, v


def _adamw(w, g_mine, g_other, m, v, c, tag):
    R, C = w.shape
    tr = R // 4

    def body(c_ref, w_ref, gm_ref, go_ref, m_ref, v_ref, g_ref, d_ref, nm_ref, nv_ref):
        g = jnp.where(pl.program_id(0) == c_ref[0], gm_ref[...], go_ref[...])
        g_ref[...] = g
        d_ref[...], nm_ref[...], nv_ref[...] = _adamw_math(w_ref[...], g, m_ref[...], v_ref[...])

    blk = pl.BlockSpec((tr, C), lambda h, i, cr: (2 * h + i, 0))
    half = pl.BlockSpec((tr, C), lambda h, i, cr: (i, 0))
    return pl.pallas_call(
        body, name="adamw_" + tag,
        grid_spec=pltpu.PrefetchScalarGridSpec(
            num_scalar_prefetch=1, grid=(2, 2), in_specs=[blk, half, half, blk, blk], out_specs=[blk] * 4),
        out_shape=[jax.ShapeDtypeStruct((R, C), F32)] * 4,
    )(c, w, g_mine, g_other, m, v)


def _small_update(all_small, w, m, v):
    def body(a_ref, w_ref, m_ref, v_ref, g_ref, d_ref, nm_ref, nv_ref):
        g = ((a_ref[0] + a_ref[1]) + (a_ref[2] + a_ref[3])) + ((a_ref[4] + a_ref[5]) + (a_ref[6] + a_ref[7]))
        g_ref[...] = g
        d_ref[...], nm_ref[...], nv_ref[...] = _adamw_math(w_ref[...], g, m_ref[...], v_ref[...])

    return pl.pallas_call(
        body, name="small_update", out_shape=[jax.ShapeDtypeStruct(w.shape, F32)] * 4,
    )(all_small, w, m, v)


SMALL_ROWS = (("attn", 0, 0), ("ffn", 1, 0), ("dil", 2, 0), ("sb", 2, GROUP), ("q", 3, 0), ("k", 3, HEAD_DIM),
              ("loss", 4, 0))


def _pack_small(vals, D):
    rows = [jnp.zeros((1, D), F32) for _ in range(8)]
    for name, r, off in SMALL_ROWS:
        if name in vals:
            rows[r] = lax.dynamic_update_slice(rows[r], vals[name].astype(F32), (0, off))
    return jnp.concatenate(rows, axis=0)


def _unpack_small(packed, vals):
    return {name: packed[r:r + 1, off:off + vals[name].shape[1]] for name, r, off in SMALL_ROWS if name in vals}


def kernel(x, attn_norm_w, w_in, q_norm_w, k_norm_w, dil_out_norm_w, sb_out_norm_w, w_out, ffn_norm_w, w_gate, w_up, w_down, loss_target, m_attn_norm_w, m_w_in, m_q_norm_w, m_k_norm_w, m_dil_out_norm_w, m_sb_out_norm_w, m_w_out, m_ffn_norm_w, m_w_gate, m_w_up, m_w_down, v_attn_norm_w, v_w_in, v_q_norm_w, v_k_norm_w, v_dil_out_norm_w, v_sb_out_norm_w, v_w_out, v_ffn_norm_w, v_w_gate, v_w_up, v_w_down):
    D = x.shape[-1]
    big_names = ("w_in", "w_out", "w_gate", "w_up", "w_down")
    flipped = ("w_gate", "w_up")
    tr = lambda a: jnp.swapaxes(a[0], 0, 1)
    big_w = dict(w_in=w_in[0], w_out=w_out[0], w_gate=tr(w_gate), w_up=tr(w_up), w_down=w_down[0])
    big_m = dict(w_in=m_w_in[0], w_out=m_w_out[0], w_gate=tr(m_w_gate), w_up=tr(m_w_up), w_down=m_w_down[0])
    big_v = dict(w_in=v_w_in[0], w_out=v_w_out[0], w_gate=tr(v_w_gate), w_up=tr(v_w_up), w_down=v_w_down[0])
    small_w = dict(attn=attn_norm_w, q=q_norm_w, k=k_norm_w, dil=dil_out_norm_w, sb=sb_out_norm_w, ffn=ffn_norm_w)
    small_m = dict(attn=m_attn_norm_w, q=m_q_norm_w, k=m_k_norm_w, dil=m_dil_out_norm_w, sb=m_sb_out_norm_w,
                   ffn=m_ffn_norm_w)
    small_v = dict(attn=v_attn_norm_w, q=v_q_norm_w, k=v_k_norm_w, dil=v_dil_out_norm_w, sb=v_sb_out_norm_w,
                   ffn=v_ffn_norm_w)

    c = lax.axis_index("c").astype(jnp.int32).reshape(1)
    chip = (2 * lax.axis_index("x") + lax.axis_index("y")).astype(jnp.int32).reshape(1)
    (w_in_g,) = _gather_weights([big_w["w_in"].astype(BF16)])
    w_out_slots = _own_slots(big_w["w_out"].astype(BF16))
    ffn_slots = [_own_slots(big_w[n].astype(BF16)) for n in FFN_NAMES]

    loss_parts, grad_x, small_g, dw_in, dw_out, ffn_halves = _device_step(
        x[0], loss_target[0], attn_norm_w, q_norm_w, k_norm_w, dil_out_norm_w, sb_out_norm_w, ffn_norm_w,
        w_in_g, w_out_slots, ffn_slots, c, chip)
    small_g["loss"] = (jnp.sum(loss_parts[:, 0, 0]) * (0.5 / D)).reshape(1, 1)

    late = [dw_in, dw_out.reshape(N_CHIPS, -1, D)]
    *from_pair, all_small = _pair_exchange(late, _pack_small(small_g, D))
    chip_parts = [_pair_sum(g, r, c, n) for g, r, n in zip(late, from_pair, big_names)]
    from_chips = _chip_exchange(chip_parts)
    halves = [_chip_sum(r, p, chip, n) for r, p, n in zip(from_chips, chip_parts, big_names)] + ffn_halves
    others = _pair_swap(halves)
    big_out = {n: _adamw(big_w[n], mine, other, big_m[n], big_v[n], c, n)
               for n, mine, other in zip(big_names, halves, others)}
    sg, sd, sm, sv = _small_update(all_small, _pack_small(small_w, D), _pack_small(small_m, D),
                                   _pack_small(small_v, D))
    small_out = [_unpack_small(t, small_w) for t in (sg, sd, sm, sv)]

    order = (("attn", None), (None, "w_in"), ("q", None), ("k", None), ("dil", None), ("sb", None),
             (None, "w_out"), ("ffn", None), (None, "w_gate"), (None, "w_up"), (None, "w_down"))
    outs = [sg[4, 0], grad_x[None]]
    for kind in range(4):
        for s_name, b_name in order:
            if s_name is not None:
                outs.append(small_out[kind][s_name])
            else:
                res = big_out[b_name][kind]
                outs.append((jnp.swapaxes(res, 0, 1) if b_name in flipped else res)[None])
    return tuple(outs)
```

```python
import functools

import jax
import jax.numpy as jnp
from jax import lax
from jax.experimental import pallas as pl
from jax.experimental.pallas import tpu as pltpu

F32 = jnp.float32
BF16 = jnp.bfloat16
MESH = pl.DeviceIdType.MESH

HEAD_DIM = 64
GROUP = 512
BLOCK = 128
LANES = 128
N_CHIPS = 4
N_DEV = 8
EPS = 1e-6
ROPE_THETA = 10000.0
DILATIONS = (1, 4, 16)
NEG = -1e30

ADAM_LR = 0.001
ADAM_B1 = 0.9
ADAM_B2 = 0.999
ADAM_EPS = 1e-08
ADAM_WD = 0.01
ADAM_STEP = 10


def _dot(a, b):
    return jnp.dot(a, b, preferred_element_type=F32)


def _dot_nt(a, b):
    return lax.dot_general(a, b, (((1,), (1,)), ((), ())), preferred_element_type=F32)


def _dot_tn(a, b):
    return lax.dot_general(a, b, (((0,), (0,)), ((), ())), preferred_element_type=F32)


def _split(v):
    hi = lax.bitcast_convert_type(lax.bitcast_convert_type(v, jnp.uint32) & jnp.uint32(0xFFFF0000), F32)
    return hi.astype(BF16), (v - hi).astype(BF16)


def _segsum(v, g):
    hi, lo = _split(v)
    return _dot(hi, g) + _dot(lo, g)


def _rot_half(x):
    outs = []
    for c in range(x.shape[1] // LANES):
        xc = x[:, c * LANES:(c + 1) * LANES]
        lane = lax.broadcasted_iota(jnp.int32, xc.shape, 1)
        first = (lane % HEAD_DIM) < (HEAD_DIM // 2)
        outs.append(jnp.where(first, pltpu.roll(xc, LANES - 32, 1), pltpu.roll(xc, 32, 1)))
    return outs[0] if len(outs) == 1 else jnp.concatenate(outs, axis=1)


def _rms(x):
    return lax.rsqrt(jnp.mean(x * x, axis=-1, keepdims=True) + EPS)


def _rms_bwd(dy, x, w):
    rstd = _rms(x)
    xh = x * rstd
    dxh = dy * w
    dx = rstd * (dxh - xh * jnp.mean(dxh * xh, axis=-1, keepdims=True))
    return dx, dy * xh


def _sigmoid(x):
    return 1.0 / (1.0 + jnp.exp(-x))


def _sum4(ref):
    p = [ref[j].astype(F32) for j in range(N_CHIPS)]
    return (p[0] + p[1]) + (p[2] + p[3])


def _full(shape):
    n = len(shape)
    return pl.BlockSpec(shape, lambda *_: (0,) * n)


def _strided_spec(tm, r):
    return pl.BlockSpec((r, tm // r, GROUP), lambda i: (0, i, 0))


def _strided_shape(S, r, dtype):
    return jax.ShapeDtypeStruct((r, S // r, GROUP), dtype)


def _to_strided(scr, val, outs):
    chunks = range(GROUP // LANES)
    for k in chunks:
        scr[k] = val[:, _lanes(k)]
    for r, o_ref in outs:
        if r == 1:
            o_ref[0] = val.astype(o_ref.dtype)
            continue
        n = val.shape[0] // r
        for c in range(r):
            rows = pl.ds(c, n, stride=r)
            o_ref[c] = jnp.concatenate([scr.at[k][rows, :] for k in chunks], axis=1).astype(o_ref.dtype)


def _from_strided(scr, ref):
    r, n, _ = ref.shape
    if r == 1:
        return ref[0].astype(F32)
    chunks = range(GROUP // LANES)
    for c in range(r):
        plane = ref[c].astype(F32)
        for k in chunks:
            scr.at[k][pl.ds(c, n, stride=r), :] = plane[:, _lanes(k)]
    return jnp.concatenate([scr[k] for k in chunks], axis=1)


def _strided_scratch(tm):
    return pltpu.VMEM((GROUP // LANES, tm, LANES), F32)


def _tile4(t):
    return jnp.concatenate([t] * (GROUP // LANES), axis=1)


def _in_proj_fwd(x, attn_w, w_in_g, qw, kw, cos_t, sin_t, seg_ones, slots):
    S, D = x.shape
    tm = 512
    wcols = w_in_g.shape[2]
    nd = len(DILATIONS)
    ns = len(slots)

    def body(x_ref, aw_ref, w_ref, qw_ref, kw_ref, cos_ref, sin_ref, g_ref, *rest):
        slot_in, (h_ref, qa_ref, ka_ref), rest = rest[:ns], rest[ns:ns + 3], rest[ns + 3:]
        q_refs, k_refs, v_refs = rest[:nd], rest[nd:2 * nd], rest[2 * nd:3 * nd]
        qs_ref, ks_ref, vs_ref = rest[3 * nd:3 * nd + 3]
        slot_out, scr, sems = rest[3 * nd + 3:3 * nd + 3 + ns], rest[3 * nd + 3 + ns], rest[3 * nd + 4 + ns:]
        finish = _hosted_gathers(slot_in, slot_out, *sems, pl.program_id(0), S // tm) if ns else None
        xv = x_ref[...]
        h = (xv * _rms(xv) * aw_ref[...]).astype(BF16)
        h_ref[...] = h
        proj = jnp.concatenate([_dot(h, w_ref[j]) for j in range(N_CHIPS)], axis=1)
        qa = proj[:, 0 * GROUP:1 * GROUP]
        ka = proj[:, 1 * GROUP:2 * GROUP]
        qa_ref[...] = qa
        ka_ref[...] = ka
        _to_strided(scr, proj[:, 2 * GROUP:3 * GROUP], list(zip(DILATIONS, v_refs)))
        qs_ref[...] = proj[:, 3 * GROUP:4 * GROUP].astype(BF16)
        ks_ref[...] = proj[:, 4 * GROUP:5 * GROUP].astype(BF16)
        vs_ref[...] = proj[:, 5 * GROUP:6 * GROUP].astype(BF16)
        g = g_ref[...]
        cos = _tile4(cos_ref[...])
        sin = _tile4(sin_ref[...])
        for t, w_r, o_rs in ((qa, qw_ref, q_refs), (ka, kw_ref, k_refs)):
            rstd = lax.rsqrt(_segsum(t * t, g) * (1.0 / HEAD_DIM) + EPS)
            tn = t * rstd * w_r[...]
            _to_strided(scr, tn * cos + _rot_half(tn) * sin, list(zip(DILATIONS, o_rs)))
        if finish is not None:
            finish()

    row = lambda i: (i, 0)
    tile = lambda n, dt: jax.ShapeDtypeStruct((S, n), dt)
    planes = [_strided_spec(tm, r) for r in DILATIONS]
    h_in, h_out, h_shape, h_sems = _hosted_specs(slots)
    n_out = 6 + 3 * nd
    return pl.pallas_call(
        body, name="in_proj_fwd", grid=(S // tm,),
        in_specs=[pl.BlockSpec((tm, D), row), _full((1, D)), _full((N_CHIPS, D, wcols)),
                  _full((1, GROUP)), _full((1, GROUP)),
                  pl.BlockSpec((tm, LANES), row), pl.BlockSpec((tm, LANES), row),
                  _full((GROUP, GROUP))] + h_in,
        out_specs=[pl.BlockSpec((tm, D), row)] + [pl.BlockSpec((tm, GROUP), row)] * 2 + planes * 3
                  + [pl.BlockSpec((tm, GROUP), row)] * 3 + h_out,
        out_shape=[tile(D, BF16), tile(GROUP, F32), tile(GROUP, F32)]
                  + [_strided_shape(S, r, BF16) for r in DILATIONS] * 3 + [tile(GROUP, BF16)] * 3 + h_shape,
        input_output_aliases={8 + a: n_out + a for a in range(ns)},
        scratch_shapes=[_strided_scratch(tm)] + h_sems,
    )(x, attn_w, w_in_g, qw, kw, cos_t, sin_t, seg_ones, *slots)


DIL_PLANES = 1


def _dil_fwd(q, k, v, slots):
    r, L, _ = q.shape
    nb = L // BLOCK
    P = GROUP // LANES
    PL = min(DIL_PLANES, r)
    ns = len(slots)
    units = [(pp, hp) for pp in range(PL) for hp in range(P)]

    def body(q_ref, kc_ref, kp_ref, vc_ref, vp_ref, *rest):
        o_ref, l_ref = rest[ns:ns + 2]
        n = pl.program_id(1)
        finish = (_hosted_gathers(rest[:ns], rest[ns + 2:2 * ns + 2], *rest[2 * ns + 2:],
                                  pl.program_id(0) * nb + n, (r // PL) * nb) if ns else None)
        rowi = lax.broadcasted_iota(jnp.int32, (BLOCK, BLOCK), 0)
        coli = lax.broadcasted_iota(jnp.int32, (BLOCK, BLOCK), 1)
        first = coli < HEAD_DIM
        masks = (coli <= rowi, jnp.logical_and(coli >= rowi, n > 0))
        s2 = {}
        for pp, hp in units:
            q2 = _scaled(q_ref[pp, :, _lanes(hp)])
            for b, k_ref in enumerate((kc_ref, kp_ref)):
                s2[pp, hp, b] = _dot_nt(q2, _by_head(k_ref[pp, :, _lanes(hp)], first))
        ps, inv, lse = {}, {}, {}
        for pp, hp in units:
            for h in range(2):
                s = [jnp.where(masks[b], s2[pp, hp, b][:, h * BLOCK:(h + 1) * BLOCK], NEG) for b in range(2)]
                m = jnp.maximum(jnp.max(s[0], axis=1, keepdims=True), jnp.max(s[1], axis=1, keepdims=True))
                p = [jnp.exp(s[b] - m) for b in range(2)]
                den = jnp.sum(p[0], axis=1, keepdims=True) + jnp.sum(p[1], axis=1, keepdims=True)
                ps[pp, hp, h] = [p[b].astype(BF16) for b in range(2)]
                inv[pp, hp, h] = 1.0 / den
                lse[pp, hp, h] = m + jnp.log(den)
        for pp, hp in units:
            o = jnp.zeros((BLOCK, LANES), F32)
            for b, v_ref in enumerate((vc_ref, vp_ref)):
                o = o + _dot(jnp.concatenate([ps[pp, hp, 0][b], ps[pp, hp, 1][b]], axis=1),
                             _by_head(v_ref[pp, :, _lanes(hp)], first))
            o_ref[pp, :, _lanes(hp)] = o * jnp.where(first, inv[pp, hp, 0], inv[pp, hp, 1])
            l_ref[pp, :, _lanes(hp)] = jnp.where(first, lse[pp, hp, 0], lse[pp, hp, 1])
        if finish is not None:
            finish()

    cur = pl.BlockSpec((PL, BLOCK, GROUP), lambda c, n: (c, n, 0))
    prev = pl.BlockSpec((PL, BLOCK, GROUP), lambda c, n: (c, jnp.maximum(n - 1, 0), 0))
    h_in, h_out, h_shape, h_sems = _hosted_specs(slots)
    return pl.pallas_call(
        body, name="dil_fwd_r%d" % r, grid=(r // PL, nb),
        in_specs=[cur, cur, prev, cur, prev] + h_in, out_specs=[cur, cur] + h_out,
        out_shape=[jax.ShapeDtypeStruct(q.shape, F32)] * 2 + h_shape,
        input_output_aliases={5 + a: 2 + a for a in range(ns)},
        scratch_shapes=h_sems,
    )(q, k, k, v, v, *slots)


def _dil_bwd(q, k, v, do, lse, delta, part):
    r, L, _ = q.shape
    nb = L // BLOCK
    P = GROUP // LANES
    PL = min(DIL_PLANES, r)
    scale = HEAD_DIM ** -0.5
    units = [(pp, hp) for pp in range(PL) for hp in range(P)]

    def body(qc_ref, qn_ref, doc_ref, don_ref, lc_ref, ln_ref, dc_ref, dn_ref, k_ref, v_ref, part_in,
             dq_ref, dk_ref, dv_ref, part_out, carry, send, recv):
        j = pl.program_id(1)
        step = pl.program_id(0) * nb + j
        start, finish = _chip_send_plan(part_in, part_out, send, recv)
        pl.when(step == 0)(start)
        rowi = lax.broadcasted_iota(jnp.int32, (BLOCK, BLOCK), 0)
        coli = lax.broadcasted_iota(jnp.int32, (BLOCK, BLOCK), 1)
        first = coli < HEAD_DIM
        sides = ((qc_ref, doc_ref, lc_ref, dc_ref, coli <= rowi),
                 (qn_ref, don_ref, ln_ref, dn_ref, jnp.logical_and(coli >= rowi, j < nb - 1)))

        @pl.when(j == 0)
        def _():
            carry[...] = jnp.zeros_like(carry)

        kcat, q2, do2, s2, dp2 = {}, {}, {}, {}, {}
        for pp, hp in units:
            kcat[pp, hp] = _by_head(k_ref[pp, :, _lanes(hp)], first)
            vcat = _by_head(v_ref[pp, :, _lanes(hp)], first)
            for x, (q_r, do_r, _, _, _) in enumerate(sides):
                q2[pp, hp, x] = _scaled(q_r[pp, :, _lanes(hp)])
                do2[pp, hp, x] = do_r[pp, :, _lanes(hp)]
                s2[pp, hp, x] = _dot_nt(q2[pp, hp, x], kcat[pp, hp])
                dp2[pp, hp, x] = _dot_nt(do2[pp, hp, x], vcat)
        pcat, dscat = {}, {}
        for pp, hp in units:
            for x, (_, _, l_r, d_r, msk) in enumerate(sides):
                ps, dss = [], []
                for h in range(2):
                    col = hp * LANES + h * HEAD_DIM
                    half = slice(h * BLOCK, (h + 1) * BLOCK)
                    p = jnp.where(msk, jnp.exp(s2[pp, hp, x][:, half] - l_r[pp, :, col:col + 1]), 0.0)
                    ps.append(p.astype(BF16))
                    dss.append((p * (dp2[pp, hp, x][:, half] - d_r[pp, :, col:col + 1])).astype(BF16))
                pcat[pp, hp, x] = jnp.concatenate(ps, axis=1)
                dscat[pp, hp, x] = jnp.concatenate(dss, axis=1)
        for pp, hp in units:
            dv2 = _dot_tn(pcat[pp, hp, 0], do2[pp, hp, 0]) + _dot_tn(pcat[pp, hp, 1], do2[pp, hp, 1])
            dk2 = _dot_tn(dscat[pp, hp, 0], q2[pp, hp, 0]) + _dot_tn(dscat[pp, hp, 1], q2[pp, hp, 1])
            dv_ref[pp, :, _lanes(hp)] = jnp.where(first, dv2[:BLOCK], dv2[BLOCK:]).astype(BF16)
            dk_ref[pp, :, _lanes(hp)] = jnp.where(first, dk2[:BLOCK], dk2[BLOCK:]).astype(BF16)
            dq_ref[pp, :, _lanes(hp)] = (carry[pp, :, _lanes(hp)]
                                         + _dot(dscat[pp, hp, 0], kcat[pp, hp]) * scale).astype(BF16)
            carry[pp, :, _lanes(hp)] = _dot(dscat[pp, hp, 1], kcat[pp, hp]) * scale
        pl.when(step == (r // PL) * nb - 1)(finish)

    cur = pl.BlockSpec((PL, BLOCK, GROUP), lambda c, n: (c, n, 0))
    nxt = pl.BlockSpec((PL, BLOCK, GROUP), lambda c, n: (c, jnp.minimum(n + 1, nb - 1), 0))
    return pl.pallas_call(
        body, name="dil_bwd_r%d" % r, grid=(r // PL, nb),
        in_specs=[cur, nxt, cur, nxt, cur, nxt, cur, nxt, cur, cur, HBM], out_specs=[cur, cur, cur, HBM],
        out_shape=[jax.ShapeDtypeStruct(q.shape, BF16)] * 3 + [jax.ShapeDtypeStruct(part.shape, part.dtype)],
        scratch_shapes=[pltpu.VMEM((PL, BLOCK, GROUP), F32), pltpu.SemaphoreType.DMA((3,)),
                        pltpu.SemaphoreType.DMA((3,))],
    )(q, q, do, do, lse, lse, delta, delta, k, v, part)


SB_TILES = 2
SB_PAIRS_FWD = 4
SB_PAIRS_BWD = 2
SB_DEAD = -110.0


def _lanes(hp):
    return slice(hp * LANES, (hp + 1) * LANES)


def _sb_logits(z, valid):
    e = jnp.exp(-jnp.abs(z))
    lb = jnp.minimum(z, 0.0) - jnp.log(1.0 + e)
    lk = lb - z
    if valid is not None:
        lk = jnp.where(valid, lk, 0.0)
    return e, lb, lk


def _by_head(t, first):
    zero = jnp.zeros_like(t)
    return jnp.concatenate([jnp.where(first, t, zero), jnp.where(first, zero, t)], axis=0)


def _sb_valid(i, j):
    rowi = lax.broadcasted_iota(jnp.int32, (BLOCK, BLOCK), 0)
    coli = lax.broadcasted_iota(jnp.int32, (BLOCK, BLOCK), 1)
    return (coli - rowi) < (i - j) * BLOCK


def _scaled(q):
    return (q.astype(F32) * (HEAD_DIM ** -0.5)).astype(BF16)


def _hosted_gathers(refs_in, refs_out, send, recv, step, steps):
    plans = [_gather_plan(refs_in[a], refs_out[a], send.at[pl.ds(6 * a, 6)], recv.at[pl.ds(6 * a, 6)])
             for a in range(len(refs_in))]
    for stage, at in ((0, 0), (1, (2 * steps) // 3)):
        @pl.when(step == at)
        def _():
            for plan in plans:
                plan[stage]()

    def finish():
        @pl.when(step == steps - 1)
        def _():
            for plan in plans:
                plan[2]()

    return finish


def _hosted_specs(slots):
    n = len(slots)
    sems = [pltpu.SemaphoreType.DMA((6 * n,))] * 2 if n else []
    return [HBM] * n, [HBM] * n, [jax.ShapeDtypeStruct(s.shape, s.dtype) for s in slots], sems


def _sb_fwd(qs, ks, vs, tri_later, slots):
    S = qs.shape[0]
    P = SB_PAIRS_FWD
    W = P * LANES
    ns = len(slots)

    def body(q_ref, k_ref, v_ref, u_ref, *rest):
        o_ref, lt_ref, from_ref = rest[ns:ns + 3]
        i = pl.program_id(1)
        finish = _hosted_gathers(rest[:ns], rest[ns + 3:2 * ns + 3], *rest[2 * ns + 3:], i, S // BLOCK) if ns else None
        first = lax.broadcasted_iota(jnp.int32, (BLOCK, LANES), 1) < HEAD_DIM
        q2 = [_scaled(q_ref[:, _lanes(hp)]) for hp in range(P)]

        def walk(tiles, carry):
            runs, accs = list(carry[0]), list(carry[1])
            units = [(t, hp) for t in range(len(tiles)) for hp in range(P)]
            offs = [pl.multiple_of(j * BLOCK, BLOCK) for j, _ in tiles]
            valids = [_sb_valid(i, j) if diag else None for j, diag in tiles]
            z2s, lbs, c2s = {}, {}, {}
            for t, hp in units:
                z2s[t, hp] = _dot_nt(q2[hp], _by_head(k_ref[pl.ds(offs[t], BLOCK), _lanes(hp)], first))
            for t, hp in units:
                for h in range(2):
                    _, lb, lk = _sb_logits(z2s[t, hp][:, h * BLOCK:(h + 1) * BLOCK], valids[t])
                    lbs[t, hp, h] = lb
                    c2s[t, hp, h] = _dot(jnp.concatenate(_split(lk), axis=1), u_ref[...])
            for t, hp in units:
                a2 = []
                for h in range(2):
                    a = jnp.exp(lbs[t, hp, h] + c2s[t, hp, h][:, :BLOCK] + runs[2 * hp + h])
                    if valids[t] is not None:
                        a = jnp.where(valids[t], a, 0.0)
                    a2.append(a.astype(BF16))
                    runs[2 * hp + h] = runs[2 * hp + h] + c2s[t, hp, h][:, BLOCK:]
                vcat = _by_head(v_ref[pl.ds(offs[t], BLOCK), _lanes(hp)], first)
                accs[hp] = accs[hp] + _dot(jnp.concatenate(a2, axis=1), vcat)
            return tuple(runs), tuple(accs)

        def chunk(ci, carry):
            return walk([(ci * SB_TILES + t, False) for t in reversed(range(SB_TILES))], carry)

        def alive(runs):
            top = functools.reduce(jnp.maximum, runs)
            return (jnp.max(top) > SB_DEAD).astype(jnp.int32)

        def step(c):
            t, _, runs, accs = c
            runs, accs = chunk(nfull - 1 - t, (runs, accs))
            return t + 1, alive(runs), runs, accs

        zero = jnp.zeros((BLOCK, LANES), F32)
        nfull = i // SB_TILES
        ragged = [functools.partial(walk, [(i, True)] + [(i - 1 - m, False) for m in range(extra)])
                  for extra in range(SB_TILES)]
        runs, accs = lax.switch(i % SB_TILES, ragged, ((zero,) * (2 * P), (zero,) * P))
        done, _, runs, accs = lax.while_loop(lambda c: jnp.logical_and(c[0] < nfull, c[1] > 0), step,
                                             (jnp.int32(0), alive(runs), runs, accs))
        for hp in range(P):
            o_ref[:, _lanes(hp)] = accs[hp]
            lt_ref[:, _lanes(hp)] = jnp.where(first, runs[2 * hp], runs[2 * hp + 1])
        from_ref[...] = jnp.full(from_ref.shape, nfull - done, jnp.int32)
        if finish is not None:
            finish()

    assert W == GROUP
    blk = pl.BlockSpec((BLOCK, W), lambda hp, i: (i, hp))
    col = pl.BlockSpec((S, W), lambda hp, i: (0, hp))
    h_in, h_out, h_shape, h_sems = _hosted_specs(slots)
    return pl.pallas_call(
        body, name="sb_fwd", grid=(GROUP // W, S // BLOCK),
        in_specs=[blk, col, col, _full((2 * BLOCK, 2 * BLOCK))] + h_in,
        out_specs=[blk, blk, pl.BlockSpec((1, 8, LANES), lambda hp, i: (i, 0, 0))] + h_out,
        out_shape=[jax.ShapeDtypeStruct((S, GROUP), F32)] * 2
                  + [jax.ShapeDtypeStruct((S // BLOCK, 8, LANES), jnp.int32)] + h_shape,
        input_output_aliases={4 + a: 3 + a for a in range(ns)},
        scratch_shapes=h_sems,
    )(qs, ks, vs, tri_later, *slots)


def _sb_bwd(first_chunk, qs, ks, vs, do, ltot, tri_upto, tri_before):
    S = qs.shape[0]
    P = SB_PAIRS_BWD
    W = P * LANES

    def body(from_ref, q_ref, k_ref, v_ref, do_ref, lt_ref, w_ref, x_ref, dq_ref, dk_ref, dv_ref):
        i = pl.program_id(1)

        @pl.when(i == 0)
        def _():
            dk_ref[...] = jnp.zeros_like(dk_ref)
            dv_ref[...] = jnp.zeros_like(dv_ref)

        first = lax.broadcasted_iota(jnp.int32, (BLOCK, LANES), 1) < HEAD_DIM
        q2 = [_scaled(q_ref[:, _lanes(hp)]) for hp in range(P)]
        do2 = [do_ref[:, _lanes(hp)] for hp in range(P)]
        totals = [jnp.broadcast_to(lt_ref[:, n * HEAD_DIM:n * HEAD_DIM + 1], (BLOCK, LANES)) for n in range(2 * P)]

        def walk(tiles, carry):
            keeps, grads, dqs = list(carry[0]), list(carry[1]), list(carry[2])
            units = [(t, hp) for t in range(len(tiles)) for hp in range(P)]
            offs = [pl.multiple_of(j * BLOCK, BLOCK) for j, _ in tiles]
            valids = [_sb_valid(i, j) if diag else None for j, diag in tiles]
            kcat, z2, da2, es, lbs, c2s, as_, des, p2s = {}, {}, {}, {}, {}, {}, {}, {}, {}
            for t, hp in units:
                kcat[t, hp] = _by_head(k_ref[pl.ds(offs[t], BLOCK), _lanes(hp)], first)
                z2[t, hp] = _dot_nt(q2[hp], kcat[t, hp])
                da2[t, hp] = _dot_nt(do2[hp], _by_head(v_ref[pl.ds(offs[t], BLOCK), _lanes(hp)], first))
            for t, hp in units:
                for h in range(2):
                    es[t, hp, h], lbs[t, hp, h], lk = _sb_logits(z2[t, hp][:, h * BLOCK:(h + 1) * BLOCK], valids[t])
                    c2s[t, hp, h] = _dot(jnp.concatenate(_split(lk), axis=1), w_ref[...])
            for t, hp in units:
                for h in range(2):
                    n = 2 * hp + h
                    a = jnp.exp(lbs[t, hp, h] + (totals[n] - (keeps[n] + c2s[t, hp, h][:, :BLOCK])))
                    if valids[t] is not None:
                        a = jnp.where(valids[t], a, 0.0)
                    keeps[n] = keeps[n] + c2s[t, hp, h][:, BLOCK:]
                    de = a * da2[t, hp][:, h * BLOCK:(h + 1) * BLOCK]
                    as_[t, hp, h], des[t, hp, h] = a.astype(BF16), de
                    p2s[t, hp, h] = _dot(jnp.concatenate(_split(de), axis=1), x_ref[...])
            for t, hp in units:
                dz2 = []
                for h in range(2):
                    n = 2 * hp + h
                    e = es[t, hp, h]
                    sig = jnp.where(z2[t, hp][:, h * BLOCK:(h + 1) * BLOCK] >= 0.0, 1.0, e) / (1.0 + e)
                    dz = des[t, hp, h] * (1.0 - sig) - (grads[n] + p2s[t, hp, h][:, :BLOCK]) * sig
                    if valids[t] is not None:
                        dz = jnp.where(valids[t], dz, 0.0)
                    grads[n] = grads[n] + p2s[t, hp, h][:, BLOCK:]
                    dz2.append(dz.astype(BF16))
                dzcat = jnp.concatenate(dz2, axis=1)
                dk2 = _dot_tn(dzcat, q2[hp])
                dv2 = _dot_tn(jnp.concatenate([as_[t, hp, 0], as_[t, hp, 1]], axis=1), do2[hp])
                dk_ref[pl.ds(offs[t], BLOCK), _lanes(hp)] += jnp.where(first, dk2[:BLOCK], dk2[BLOCK:])
                dv_ref[pl.ds(offs[t], BLOCK), _lanes(hp)] += jnp.where(first, dv2[:BLOCK], dv2[BLOCK:])
                dqs[hp] = dqs[hp] + _dot(dzcat, kcat[t, hp])
            return tuple(keeps), tuple(grads), tuple(dqs)

        zero = jnp.zeros((BLOCK, LANES), F32)
        nfull = i // SB_TILES
        carry = lax.fori_loop(
            from_ref[i], nfull, lambda ci, c: walk([(ci * SB_TILES + t, False) for t in range(SB_TILES)], c),
            ((zero,) * (2 * P), (zero,) * (2 * P), (zero,) * P))
        ragged = [functools.partial(walk, [(i - m, False) for m in range(extra, 0, -1)] + [(i, True)])
                  for extra in range(SB_TILES)]
        carry = lax.switch(i % SB_TILES, ragged, carry)
        for hp in range(P):
            dq_ref[:, _lanes(hp)] = carry[2][hp] * (HEAD_DIM ** -0.5)

    blk = pl.BlockSpec((BLOCK, W), lambda hp, i, fr: (i, hp))
    col = pl.BlockSpec((S, W), lambda hp, i, fr: (0, hp))
    tri = pl.BlockSpec((2 * BLOCK, 2 * BLOCK), lambda hp, i, fr: (0, 0))
    return pl.pallas_call(
        body, name="sb_bwd",
        grid_spec=pltpu.PrefetchScalarGridSpec(
            num_scalar_prefetch=1, grid=(GROUP // W, S // BLOCK),
            in_specs=[blk, col, col, blk, blk, tri, tri], out_specs=[blk, col, col]),
        out_shape=[jax.ShapeDtypeStruct((S, GROUP), F32)] * 3,
    )(first_chunk, qs, ks, vs, do, ltot, tri_upto, tri_before)


def _out_proj_fwd(o_br, l_br, o_sb, x, w_dil, w_sbn, w_out_g):
    S, D = x.shape
    tm = 512

    def body(o0, o1, o2, l0, l1, l2, os_ref, x_ref, wd_ref, ws_ref, w_ref, od_ref, s0, s1, s2, x1_ref, scr):
        ls = [_from_strided(scr, l) for l in (l0, l1, l2)]
        os_ = [_from_strided(scr, o) for o in (o0, o1, o2)]
        m = jnp.maximum(jnp.maximum(ls[0], ls[1]), ls[2])
        es = [jnp.exp(l - m) for l in ls]
        den = es[0] + es[1] + es[2]
        od = (es[0] * os_[0] + es[1] * os_[1] + es[2] * os_[2]) / den
        od_ref[...] = od
        _to_strided(scr, m + jnp.log(den), list(zip(DILATIONS, (s0, s1, s2))))
        osb = os_ref[...]
        mixed = jnp.concatenate([(od * _rms(od) * wd_ref[...]).astype(BF16),
                                 (osb * _rms(osb) * ws_ref[...]).astype(BF16)], axis=1)
        x1_ref[...] = x_ref[...] + _dot(mixed, w_ref[...])

    row = lambda i: (i, 0)
    g = pl.BlockSpec((tm, GROUP), row)
    d = pl.BlockSpec((tm, D), row)
    planes = [_strided_spec(tm, r) for r in DILATIONS]
    return pl.pallas_call(
        body, name="out_proj_fwd", grid=(S // tm,),
        in_specs=planes * 2 + [g, d, _full((1, GROUP)), _full((1, GROUP)), _full((2 * GROUP, D))],
        out_specs=[g] + planes + [d],
        out_shape=[jax.ShapeDtypeStruct((S, GROUP), F32)] + [_strided_shape(S, r, F32) for r in DILATIONS]
                  + [jax.ShapeDtypeStruct((S, D), F32)],
        scratch_shapes=[_strided_scratch(tm)],
    )(*o_br, *l_br, o_sb, x, w_dil, w_sbn, w_out_g)


def _ffn_fwd(x1, target, ffn_w, wg_g, wu_g, wd_g):
    S, D = x1.shape
    F = wg_g.shape[1]
    tm = 512
    nt = S // tm

    def body(x_ref, t_ref, nw_ref, wg_ref, wu_ref, wd_ref, h_ref, g_ref, u_ref, dy_ref, loss_ref, h_s, acc):
        j = pl.program_id(1)

        @pl.when(j == 0)
        def _():
            xv = x_ref[...]
            h = (xv * _rms(xv) * nw_ref[...]).astype(BF16)
            h_s[...] = h
            h_ref[...] = h
            acc[...] = xv

        h = h_s[...]
        g = _dot_nt(h, wg_ref[0])
        u = _dot_nt(h, wu_ref[0])
        g_ref[0] = g.astype(BF16)
        u_ref[0] = u.astype(BF16)
        a = (g * _sigmoid(g) * u).astype(BF16)
        acc[...] += _dot(a, wd_ref[0])

        @pl.when(j == N_CHIPS - 1)
        def _():
            err = acc[...] - t_ref[...]
            dy_ref[...] = err * (1.0 / D)
            loss_ref[...] = jnp.full(loss_ref.shape, jnp.sum(err * err), F32)

    row = lambda t, j: (t, 0)
    shard = lambda t, j: (j, 0, 0)
    act = lambda t, j: (j, t, 0)
    return pl.pallas_call(
        body, name="ffn_fwd", grid=(nt, N_CHIPS),
        in_specs=[pl.BlockSpec((tm, D), row), pl.BlockSpec((tm, D), row), pl.BlockSpec((1, D), lambda t, j: (0, 0))]
                 + [pl.BlockSpec((1, F, D), shard)] * 3,
        out_specs=[pl.BlockSpec((tm, D), row), pl.BlockSpec((1, tm, F), act), pl.BlockSpec((1, tm, F), act),
                   pl.BlockSpec((tm, D), row), pl.BlockSpec((1, 8, LANES), lambda t, j: (t, 0, 0))],
        out_shape=[jax.ShapeDtypeStruct((S, D), BF16), jax.ShapeDtypeStruct((N_CHIPS, S, F), BF16),
                   jax.ShapeDtypeStruct((N_CHIPS, S, F), BF16), jax.ShapeDtypeStruct((S, D), F32),
                   jax.ShapeDtypeStruct((nt, 8, LANES), F32)],
        scratch_shapes=[pltpu.VMEM((tm, D), BF16), pltpu.VMEM((tm, D), F32)],
    )(x1, target, ffn_w, wg_g, wu_g, wd_g)


def _ffn_bwd(h2, dy, g, u, wg_g, wu_g, wd_g):
    S, D = dy.shape
    F = wg_g.shape[1]
    tm = 512

    def body(h_ref, dy_ref, g_ref, u_ref, wg_ref, wu_ref, wd_ref, dwg_ref, dwu_ref, dwd_ref, dh_ref):
        t = pl.program_id(1)

        @pl.when(t == 0)
        def _():
            dwg_ref[...] = jnp.zeros_like(dwg_ref)
            dwu_ref[...] = jnp.zeros_like(dwu_ref)
            dwd_ref[...] = jnp.zeros_like(dwd_ref)

        h = h_ref[...]
        dyb = dy_ref[...].astype(BF16)
        gv = g_ref[0].astype(F32)
        uv = u_ref[0].astype(F32)
        da = _dot_nt(dyb, wd_ref[0])
        sg = _sigmoid(gv)
        silu = gv * sg
        du = (da * silu).astype(BF16)
        dg = (da * uv * (sg * (1.0 + gv * (1.0 - sg)))).astype(BF16)
        dwd_ref[0] += _dot_tn((silu * uv).astype(BF16), dyb)
        dwg_ref[0] += _dot_tn(dg, h)
        dwu_ref[0] += _dot_tn(du, h)
        dh_ref[0] = (_dot(dg, wg_ref[0]) + _dot(du, wu_ref[0])).astype(BF16)

    row = lambda j, t: (t, 0)
    shard = lambda j, t: (j, 0, 0)
    act = lambda j, t: (j, t, 0)
    return pl.pallas_call(
        body, name="ffn_bwd", grid=(N_CHIPS, S // tm),
        in_specs=[pl.BlockSpec((tm, D), row), pl.BlockSpec((tm, D), row),
                  pl.BlockSpec((1, tm, F), act), pl.BlockSpec((1, tm, F), act)] + [pl.BlockSpec((1, F, D), shard)] * 3,
        out_specs=[pl.BlockSpec((1, F, D), shard)] * 3 + [pl.BlockSpec((1, tm, D), act)],
        out_shape=[jax.ShapeDtypeStruct((N_CHIPS, F, D), F32)] * 3 + [jax.ShapeDtypeStruct((N_CHIPS, S, D), BF16)],
    )(h2, dy, g, u, wg_g, wu_g, wd_g)


def _out_proj_bwd(dh2p, dy, x1, ffn_w, w_out_g, o_dil, o_sb, w_dil, w_sbn, seg_ones, ffn_grads):
    S, D = dy.shape
    tm = 256
    ng = len(ffn_grads)

    def body(dh_ref, dy_ref, x1_ref, nw_ref, w_ref, od_ref, os_ref, wd_ref, ws_ref, g_ref, *rest):
        gin, rest = rest[:ng], rest[ng:]
        dx1_ref, dod0, dod1, dod2, dos_ref, dl0, dl1, dl2, dw_ref, dnw_ref, dwd_ref, dws_ref = rest[:12]
        gout, (scr, send, recv) = rest[12:12 + ng], rest[12 + ng:]
        i = pl.program_id(0)
        plans = [_pair_send_plan(gin[a], gout[a], send.at[a], recv.at[a]) for a in range(ng)]

        @pl.when(i == 0)
        def _():
            for start, _ in plans:
                start()

        @pl.when(i == 0)
        def _():
            for r_ in (dw_ref, dnw_ref, dwd_ref, dws_ref):
                r_[...] = jnp.zeros_like(r_)

        dh2 = _sum4(dh_ref)
        dxn, dwn = _rms_bwd(dh2, x1_ref[...], nw_ref[...])
        dnw_ref[...] += jnp.sum(dwn, axis=0, keepdims=True)
        dx1 = dy_ref[...] + dxn
        dx1_ref[...] = dx1
        dx1b = dx1.astype(BF16)
        dmix = _dot_nt(dx1b, w_ref[...])
        od = od_ref[...]
        osb = os_ref[...]
        mixed = jnp.concatenate([(od * _rms(od) * wd_ref[...]).astype(BF16),
                                 (osb * _rms(osb) * ws_ref[...]).astype(BF16)], axis=1)
        dw_ref[...] += _dot_tn(mixed, dx1b)
        do, dwo = _rms_bwd(dmix[:, :GROUP], od, wd_ref[...])
        dwd_ref[...] += jnp.sum(dwo, axis=0, keepdims=True)
        _to_strided(scr, do, list(zip(DILATIONS, (dod0, dod1, dod2))))
        _to_strided(scr, _segsum(do * od, g_ref[...]), list(zip(DILATIONS, (dl0, dl1, dl2))))
        do, dwo = _rms_bwd(dmix[:, GROUP:], osb, ws_ref[...])
        dws_ref[...] += jnp.sum(dwo, axis=0, keepdims=True)
        dos_ref[...] = do.astype(BF16)

        @pl.when(i == S // tm - 1)
        def _():
            for _, finish in plans:
                finish()

    row = lambda i: (i, 0)
    gsp = pl.BlockSpec((tm, GROUP), row)
    dsp = pl.BlockSpec((tm, D), row)
    planes = [_strided_spec(tm, r) for r in DILATIONS]
    halves = [jax.ShapeDtypeStruct((g.shape[0], g.shape[1] // 2, g.shape[2]), g.dtype) for g in ffn_grads]
    return pl.pallas_call(
        body, name="out_proj_bwd", grid=(S // tm,),
        in_specs=[pl.BlockSpec((N_CHIPS, tm, D), lambda i: (0, i, 0)), dsp, dsp, _full((1, D)), _full((2 * GROUP, D)),
                  gsp, gsp, _full((1, GROUP)), _full((1, GROUP)), _full((GROUP, GROUP))] + [HBM] * ng,
        out_specs=[dsp] + planes + [gsp] + planes
                  + [_full((2 * GROUP, D)), _full((1, D)), _full((1, GROUP)), _full((1, GROUP))] + [HBM] * ng,
        out_shape=[jax.ShapeDtypeStruct((S, D), F32)] + [_strided_shape(S, r, BF16) for r in DILATIONS]
                  + [jax.ShapeDtypeStruct((S, GROUP), BF16)] + [_strided_shape(S, r, F32) for r in DILATIONS]
                  + [jax.ShapeDtypeStruct((2 * GROUP, D), F32),
                     jax.ShapeDtypeStruct((1, D), F32), jax.ShapeDtypeStruct((1, GROUP), F32),
                     jax.ShapeDtypeStruct((1, GROUP), F32)] + halves,
        scratch_shapes=[_strided_scratch(tm), pltpu.SemaphoreType.DMA((ng,)), pltpu.SemaphoreType.DMA((ng,))],
    )(dh2p, dy, x1, ffn_w, w_out_g, o_dil, o_sb, w_dil, w_sbn, seg_ones, *ffn_grads)


def _qk_bwd(dq_br, dk_br, dv_br, dqs, dks, dvs, qa, ka, qw, kw, cos_t, sin_t, seg_ones):
    S = qa.shape[0]
    tm = 256

    def body(q0, q1, q2, k0, k1, k2, v0, v1, v2, dqs_ref, dks_ref, dvs_ref, qa_ref, ka_ref, qw_ref, kw_ref,
             cos_ref, sin_ref, g_ref, dp_ref, dqw_ref, dkw_ref, accq, acck, scr):
        i = pl.program_id(0)

        @pl.when(i == 0)
        def _():
            accq[...] = jnp.zeros_like(accq)
            acck[...] = jnp.zeros_like(acck)

        def branches(refs):
            return (_from_strided(scr, refs[0]) + _from_strided(scr, refs[1])) + _from_strided(scr, refs[2])

        g = g_ref[...]
        cos = _tile4(cos_ref[...])
        sin = _tile4(sin_ref[...])
        for b, (refs, pre_ref, w_ref, acc) in enumerate((((q0, q1, q2), qa_ref, qw_ref, accq),
                                                        ((k0, k1, k2), ka_ref, kw_ref, acck))):
            dh = branches(refs)
            dn = dh * cos + _rot_half(dh * sin)
            pre = pre_ref[...]
            rstd = lax.rsqrt(_segsum(pre * pre, g) * (1.0 / HEAD_DIM) + EPS)
            xh = pre * rstd
            acc[...] += jnp.sum(dn * xh, axis=0, keepdims=True)
            dxh = dn * w_ref[...]
            dpre = rstd * (dxh - xh * (_segsum(dxh * xh, g) * (1.0 / HEAD_DIM)))
            dp_ref[:, b * GROUP:(b + 1) * GROUP] = dpre.astype(BF16)
        dp_ref[:, 2 * GROUP:3 * GROUP] = branches((v0, v1, v2)).astype(BF16)
        dp_ref[:, 3 * GROUP:4 * GROUP] = dqs_ref[...].astype(BF16)
        dp_ref[:, 4 * GROUP:5 * GROUP] = dks_ref[...].astype(BF16)
        dp_ref[:, 5 * GROUP:6 * GROUP] = dvs_ref[...].astype(BF16)

        @pl.when(i == S // tm - 1)
        def _():
            for acc, o_ref in ((accq, dqw_ref), (acck, dkw_ref)):
                a = acc[...]
                pair = (a[:, 0:LANES] + a[:, LANES:2 * LANES]) + (a[:, 2 * LANES:3 * LANES] + a[:, 3 * LANES:4 * LANES])
                o_ref[...] = pair + pltpu.roll(pair, HEAD_DIM, 1)

    row = lambda i: (i, 0)
    gsp = pl.BlockSpec((tm, GROUP), row)
    tab = pl.BlockSpec((tm, LANES), row)
    planes = [_strided_spec(tm, r) for r in DILATIONS]
    return pl.pallas_call(
        body, name="qk_bwd", grid=(S // tm,),
        in_specs=planes * 3 + [gsp] * 5 + [_full((1, GROUP)), _full((1, GROUP)), tab, tab, _full((GROUP, GROUP))],
        out_specs=[pl.BlockSpec((tm, 6 * GROUP), row), _full((1, LANES)), _full((1, LANES))],
        out_shape=[jax.ShapeDtypeStruct((S, 6 * GROUP), BF16), jax.ShapeDtypeStruct((1, LANES), F32),
                   jax.ShapeDtypeStruct((1, LANES), F32)],
        scratch_shapes=[pltpu.VMEM((1, GROUP), F32), pltpu.VMEM((1, GROUP), F32), _strided_scratch(tm)],
    )(*dq_br, *dk_br, *dv_br, dqs, dks, dvs, qa, ka, qw, kw, cos_t, sin_t, seg_ones)


def _attn_in_bwd(dq_br, dk_br, dv_br, dqs, dks, dvs, qa, ka, qw, kw, cos_t, sin_t, seg_ones, h, w_in_g, x, dx1, attn_w):
    S, D = x.shape
    wc = w_in_g.shape[2]
    tm = 256

    def body(q0, q1, q2, k0, k1, k2, v0, v1, v2, dqs_ref, dks_ref, dvs_ref, qa_ref, ka_ref, qw_ref, kw_ref,
             cos_ref, sin_ref, g_ref, h_ref, w_ref, x_ref, dx1_ref, aw_ref,
             gx_ref, dw_ref, daw_ref, dqw_ref, dkw_ref, accq, acck, scr):
        i = pl.program_id(0)

        @pl.when(i == 0)
        def _():
            accq[...] = jnp.zeros_like(accq)
            acck[...] = jnp.zeros_like(acck)
            dw_ref[...] = jnp.zeros_like(dw_ref)
            daw_ref[...] = jnp.zeros_like(daw_ref)

        def branches(refs):
            return (_from_strided(scr, refs[0]) + _from_strided(scr, refs[1])) + _from_strided(scr, refs[2])

        g = g_ref[...]
        cos = _tile4(cos_ref[...])
        sin = _tile4(sin_ref[...])
        pieces = []
        for refs, pre_ref, w_r, acc in (((q0, q1, q2), qa_ref, qw_ref, accq), ((k0, k1, k2), ka_ref, kw_ref, acck)):
            dh = branches(refs)
            dn = dh * cos + _rot_half(dh * sin)
            pre = pre_ref[...]
            rstd = lax.rsqrt(_segsum(pre * pre, g) * (1.0 / HEAD_DIM) + EPS)
            xh = pre * rstd
            acc[...] += jnp.sum(dn * xh, axis=0, keepdims=True)
            dxh = dn * w_r[...]
            pieces.append((rstd * (dxh - xh * (_segsum(dxh * xh, g) * (1.0 / HEAD_DIM)))).astype(BF16))
        pieces += [branches((v0, v1, v2)).astype(BF16), dqs_ref[...].astype(BF16), dks_ref[...].astype(BF16),
                   dvs_ref[...].astype(BF16)]
        dproj = jnp.concatenate(pieces, axis=1)
        hv = h_ref[...]
        dh = jnp.zeros((tm, D), F32)
        for j in range(N_CHIPS):
            dp = dproj[:, j * wc:(j + 1) * wc]
            dw_ref[j] += _dot_tn(hv, dp)
            dh = dh + _dot_nt(dp, w_ref[j])
        dx, dw = _rms_bwd(dh, x_ref[...], aw_ref[...])
        daw_ref[...] += jnp.sum(dw, axis=0, keepdims=True)
        gx_ref[...] = dx1_ref[...] + dx

        @pl.when(i == S // tm - 1)
        def _():
            for acc, o_ref in ((accq, dqw_ref), (acck, dkw_ref)):
                a = acc[...]
                pair = (a[:, 0:LANES] + a[:, LANES:2 * LANES]) + (a[:, 2 * LANES:3 * LANES] + a[:, 3 * LANES:4 * LANES])
                o_ref[...] = pair + pltpu.roll(pair, HEAD_DIM, 1)

    row = lambda i: (i, 0)
    gsp = pl.BlockSpec((tm, GROUP), row)
    dsp = pl.BlockSpec((tm, D), row)
    tab = pl.BlockSpec((tm, LANES), row)
    planes = [_strided_spec(tm, r) for r in DILATIONS]
    return pl.pallas_call(
        body, name="attn_in_bwd", grid=(S // tm,),
        in_specs=planes * 3 + [gsp] * 5 + [_full((1, GROUP)), _full((1, GROUP)), tab, tab, _full((GROUP, GROUP)),
                                          dsp, _full((N_CHIPS, D, wc)), dsp, dsp, _full((1, D))],
        out_specs=[dsp, _full((N_CHIPS, D, wc)), _full((1, D)), _full((1, LANES)), _full((1, LANES))],
        out_shape=[jax.ShapeDtypeStruct((S, D), F32), jax.ShapeDtypeStruct((N_CHIPS, D, wc), F32),
                   jax.ShapeDtypeStruct((1, D), F32), jax.ShapeDtypeStruct((1, LANES), F32),
                   jax.ShapeDtypeStruct((1, LANES), F32)],
        scratch_shapes=[pltpu.VMEM((1, GROUP), F32), pltpu.VMEM((1, GROUP), F32), _strided_scratch(tm)],
    )(*dq_br, *dk_br, *dv_br, dqs, dks, dvs, qa, ka, qw, kw, cos_t, sin_t, seg_ones, h, w_in_g, x, dx1, attn_w)


def _in_proj_bwd(h, dproj, w_in_g):
    S, D = h.shape
    wc = w_in_g.shape[2]
    tm = 512

    def body(h_ref, dp_ref, w_ref, dw_ref, dh_ref):
        t = pl.program_id(1)

        @pl.when(t == 0)
        def _():
            dw_ref[...] = jnp.zeros_like(dw_ref)

        dp = dp_ref[...]
        dw_ref[0] += _dot_tn(h_ref[...], dp)
        dh_ref[0] = _dot_nt(dp, w_ref[0]).astype(BF16)

    return pl.pallas_call(
        body, name="in_proj_bwd", grid=(N_CHIPS, S // tm),
        in_specs=[pl.BlockSpec((tm, D), lambda j, t: (t, 0)), pl.BlockSpec((tm, wc), lambda j, t: (t, j)),
                  pl.BlockSpec((1, D, wc), lambda j, t: (j, 0, 0))],
        out_specs=[pl.BlockSpec((1, D, wc), lambda j, t: (j, 0, 0)), pl.BlockSpec((1, tm, D), lambda j, t: (j, t, 0))],
        out_shape=[jax.ShapeDtypeStruct((N_CHIPS, D, wc), F32), jax.ShapeDtypeStruct((N_CHIPS, S, D), BF16)],
    )(h, dproj, w_in_g)


def _in_norm_bwd(dhp, dx1, x, attn_w):
    S, D = x.shape
    tm = 512

    def body(dh_ref, dx1_ref, x_ref, w_ref, gx_ref, dw_ref):
        i = pl.program_id(0)

        @pl.when(i == 0)
        def _():
            dw_ref[...] = jnp.zeros_like(dw_ref)

        dh = _sum4(dh_ref)
        dx, dw = _rms_bwd(dh, x_ref[...], w_ref[...])
        dw_ref[...] += jnp.sum(dw, axis=0, keepdims=True)
        gx_ref[...] = dx1_ref[...] + dx

    row = lambda i: (i, 0)
    dsp = pl.BlockSpec((tm, D), row)
    return pl.pallas_call(
        body, name="in_norm_bwd", grid=(S // tm,),
        in_specs=[pl.BlockSpec((N_CHIPS, tm, D), lambda i: (0, i, 0)), dsp, dsp, _full((1, D))],
        out_specs=[dsp, _full((1, D))],
        out_shape=[jax.ShapeDtypeStruct((S, D), F32), jax.ShapeDtypeStruct((1, D), F32)],
    )(dhp, dx1, x, attn_w)


def _constants(S):
    pos = jnp.arange(S, dtype=F32)
    inv_freq = ROPE_THETA ** (-jnp.arange(0, HEAD_DIM, 2, dtype=F32) / HEAD_DIM)
    ang = pos[:, None] * inv_freq[None, :]
    cos, sin = jnp.cos(ang), jnp.sin(ang)
    cos_t = jnp.concatenate([cos, cos] * 2, axis=1)
    sin_t = jnp.concatenate([-sin, sin] * 2, axis=1)
    idx = jnp.arange(GROUP)
    seg_ones = (idx[:, None] // HEAD_DIM == idx[None, :] // HEAD_DIM).astype(BF16)
    r = jnp.arange(BLOCK)
    ones = jnp.ones((BLOCK, BLOCK), BF16)
    tris = [jnp.concatenate([jnp.concatenate([m.astype(BF16), ones], axis=1)] * 2, axis=0) for m in
            (r[:, None] > r[None, :],
             r[:, None] <= r[None, :],
             r[:, None] < r[None, :])]
    return cos_t, sin_t, seg_ones, tris


FFN_NAMES = ("w_gate", "w_up", "w_down")


def _device_step(x, target, attn_w, qn_w, kn_w, dil_w, sbn_w, ffn_w, w_in_g, w_out_slots, ffn_slots, core, chip):
    S = x.shape[0]
    cos_t, sin_t, seg_ones, (tri_later, tri_upto, tri_before) = _constants(S)
    reps = GROUP // HEAD_DIM
    qw = jnp.tile(qn_w, (1, reps))
    kw = jnp.tile(kn_w, (1, reps))

    nd = len(DILATIONS)
    h, qa, ka, *rest, wg_g = _in_proj_fwd(x, attn_w, w_in_g, qw, kw, cos_t, sin_t, seg_ones, ffn_slots[:1])
    qh, kh, va, (qs, ks, vs) = rest[:nd], rest[nd:2 * nd], rest[2 * nd:3 * nd], rest[3 * nd:]
    hosted = ([], [w_out_slots], [ffn_slots[2]])
    branches = [_dil_fwd(qh[b], kh[b], va[b], hosted[b]) for b in range(nd)]
    w_out_g, wd_g = branches[1][2].reshape(-1, x.shape[1]), branches[2][2]
    o_sb, ltot, walked, wu_g = _sb_fwd(qs, ks, vs, tri_later, ffn_slots[1:2])
    o_dil, *lse, x1 = _out_proj_fwd([b[0] for b in branches], [b[1] for b in branches], o_sb, x, dil_w, sbn_w, w_out_g)
    h2, g, u, dy, loss_parts = _ffn_fwd(x1, target, ffn_w, wg_g, wu_g, wd_g)

    *ffn_grads, dh2p = _ffn_bwd(h2, dy, g, u, wg_g, wu_g, wd_g)
    dx1, *mid, dw_out, dffn_w, ddil_w, dsbn_w, p0, p1, p2 = _out_proj_bwd(
        dh2p, dy, x1, ffn_w, w_out_g, o_dil, o_sb, dil_w, sbn_w, seg_ones, ffn_grads)
    do_dil, do_sb, delta = mid[:nd], mid[nd], mid[nd + 1:]
    parts = [_pair_sum(gr, fr, core, n) for gr, fr, n in zip(ffn_grads, (p0, p1, p2), FFN_NAMES)]
    dqs, dks, dvs = _sb_bwd(walked[:, 0, 0], qs, ks, vs, do_sb, ltot, tri_upto, tri_before)
    dbr = [_dil_bwd(qh[b], kh[b], va[b], do_dil[b], lse[b], delta[b], parts[b]) for b in range(nd)]
    ffn_halves = [_chip_sum(dbr[b][3], parts[b], chip, FFN_NAMES[b]) for b in range(nd)]
    grad_x, dw_in, dattn_w, dqw, dkw = _attn_in_bwd(
        [b[0] for b in dbr], [b[1] for b in dbr], [b[2] for b in dbr], dqs, dks, dvs,
        qa, ka, qw, kw, cos_t, sin_t, seg_ones, h, w_in_g, x, dx1, attn_w)
    small = dict(attn=dattn_w, q=dqw[:, :HEAD_DIM], k=dkw[:, :HEAD_DIM], dil=ddil_w, sb=dsbn_w, ffn=dffn_w)
    return loss_parts, grad_x, small, dw_in, dw_out, ffn_halves


HBM = pl.BlockSpec(memory_space=pltpu.HBM)
VMEM = pl.BlockSpec(memory_space=pltpu.VMEM)
CHIP_FLIPS = ((1, 0), (0, 1), (1, 1))


def _place():
    return lax.axis_index("x"), lax.axis_index("y"), lax.axis_index("c")


def _flip(v, d):
    return 1 - v if d else v


def _half_rows(c, n):
    return pl.ds(pl.multiple_of(c * (n // 2), 16), n // 2)


def _gather_plan(slot_in, slot_out, send, recv):
    x, y, c = _place()
    p = 2 * x + y
    chips = [(_flip(x, dx), _flip(y, dy)) for dx, dy in CHIP_FLIPS]
    mine, other = _half_rows(c, slot_in.shape[1]), _half_rows(1 - c, slot_in.shape[1])

    def copy(k, src, dst, to):
        return pltpu.make_async_remote_copy(src_ref=src, dst_ref=dst, send_sem=send.at[k], recv_sem=recv.at[k],
                                            device_id=to, device_id_type=MESH)

    def first(k):
        return copy(k, slot_in.at[p, mine], slot_out.at[p, mine], (*chips[k], c))

    def passed(k, rows):
        land = slot_out.at[2 * chips[k][0] + chips[k][1], rows]
        return copy(3 + k, land, land, (x, y, 1 - c))

    def start():
        for k in range(3):
            first(k).start()

    def forward():
        for k in range(3):
            land = slot_out.at[2 * chips[k][0] + chips[k][1], mine]
            copy(k, land, land, (*chips[k], c)).wait_recv()
            passed(k, mine).start()

    def finish():
        for k in range(3):
            passed(k, other).wait_recv()
        for k in range(3):
            first(k).wait_send()
            passed(k, mine).wait_send()

    return start, forward, finish


def _chip_send_plan(part_in, recv_out, send, recv):
    x, y, c = _place()
    p = 2 * x + y
    chips = [(_flip(x, dx), _flip(y, dy)) for dx, dy in CHIP_FLIPS]

    def copy(k):
        q = 2 * chips[k][0] + chips[k][1]
        return pltpu.make_async_remote_copy(src_ref=part_in.at[q], dst_ref=recv_out.at[p], send_sem=send.at[k],
                                            recv_sem=recv.at[k], device_id=(*chips[k], c), device_id_type=MESH)

    def start():
        for k in range(3):
            copy(k).start()

    def finish():
        for k in range(3):
            land = recv_out.at[2 * chips[k][0] + chips[k][1]]
            pltpu.make_async_remote_copy(src_ref=land, dst_ref=land, send_sem=send.at[k], recv_sem=recv.at[k],
                                         device_id=(*chips[k], c), device_id_type=MESH).wait_recv()
        for k in range(3):
            copy(k).wait_send()

    return start, finish


def _pair_send_plan(grad_in, recv_out, send, recv):
    x, y, c = _place()

    def copy():
        theirs = _half_rows(1 - c, grad_in.shape[1])
        return pltpu.make_async_remote_copy(src_ref=grad_in.at[:, theirs, :], dst_ref=recv_out, send_sem=send,
                                            recv_sem=recv, device_id=(x, y, 1 - c), device_id_type=MESH)

    return (lambda: copy().start()), (lambda: copy().wait())


def _own_slots(shard):
    here = 2 * lax.axis_index("x") + lax.axis_index("y")
    return lax.dynamic_update_slice(lax.empty((N_CHIPS,) + shard.shape, shard.dtype), shard[None], (here, 0, 0))


def _gather_weights(shards):
    n = len(shards)

    def body(*refs):
        ins, outs = refs[:n], refs[n:2 * n]
        send, recv = refs[2 * n:]
        plans = [_gather_plan(ins[a], outs[a], send.at[pl.ds(6 * a, 6)], recv.at[pl.ds(6 * a, 6)]) for a in range(n)]
        for stage in range(3):
            for plan in plans:
                plan[stage]()

    slots = [_own_slots(s) for s in shards]
    return pl.pallas_call(
        body, name="gather_weights", in_specs=[HBM] * n, out_specs=[HBM] * n,
        out_shape=[jax.ShapeDtypeStruct(s.shape, s.dtype) for s in slots],
        input_output_aliases={a: a for a in range(n)},
        scratch_shapes=[pltpu.SemaphoreType.DMA((6 * n,)), pltpu.SemaphoreType.DMA((6 * n,))],
    )(*slots)


def _pair_exchange(grads, small):
    n = len(grads)

    def body(*refs):
        gin, sm = refs[:n], refs[n]
        gout, sm_all = refs[n + 1:2 * n + 1], refs[2 * n + 1]
        send, recv = refs[2 * n + 2:]
        x, y, c = _place()
        me = 4 * x + 2 * y + c
        big = [_pair_send_plan(gin[a], gout[a], send.at[a], recv.at[a]) for a in range(n)]
        for start, _ in big:
            start()
        sm_all[pl.ds(me, 1)] = sm[...][None]
        tiny = []
        for k in range(1, N_DEV):
            px, py, pc = _flip(x, k & 4), _flip(y, k & 2), _flip(c, k & 1)
            tiny.append((pltpu.make_async_remote_copy(
                src_ref=sm, dst_ref=sm_all.at[me], send_sem=send.at[n + k - 1], recv_sem=recv.at[n + k - 1],
                device_id=(px, py, pc), device_id_type=MESH), 4 * px + 2 * py + pc))
            tiny[-1][0].start()
        for k, (cp, peer) in enumerate(tiny):
            pltpu.make_async_remote_copy(src_ref=sm, dst_ref=sm_all.at[peer], send_sem=send.at[n + k],
                                         recv_sem=recv.at[n + k], device_id=(x, y, c),
                                         device_id_type=MESH).wait_recv()
            cp.wait_send()
        for _, finish in big:
            finish()

    halves = [jax.ShapeDtypeStruct((g.shape[0], g.shape[1] // 2, g.shape[2]), g.dtype) for g in grads]
    return pl.pallas_call(
        body, name="pair_exchange", in_specs=[HBM] * n + [VMEM], out_specs=[HBM] * n + [VMEM],
        out_shape=halves + [jax.ShapeDtypeStruct((N_DEV,) + small.shape, small.dtype)],
        scratch_shapes=[pltpu.SemaphoreType.DMA((n + N_DEV - 1,)), pltpu.SemaphoreType.DMA((n + N_DEV - 1,))],
    )(*grads, small)


def _chip_exchange(parts):
    n = len(parts)

    def body(*refs):
        pin, pout = refs[:n], refs[n:2 * n]
        send, recv = refs[2 * n:]
        plans = [_chip_send_plan(pin[a], pout[a], send.at[pl.ds(3 * a, 3)], recv.at[pl.ds(3 * a, 3)]) for a in range(n)]
        for stage in range(2):
            for plan in plans:
                plan[stage]()

    return pl.pallas_call(
        body, name="chip_exchange", in_specs=[HBM] * n, out_specs=[HBM] * n,
        out_shape=[jax.ShapeDtypeStruct(s.shape, s.dtype) for s in parts],
        scratch_shapes=[pltpu.SemaphoreType.DMA((3 * n,)), pltpu.SemaphoreType.DMA((3 * n,))],
    )(*parts)


def _pair_swap(halves):
    n = len(halves)

    def body(*refs):
        hin, hout = refs[:n], refs[n:2 * n]
        send, recv = refs[2 * n:]
        x, y, c = _place()
        swaps = [pltpu.make_async_remote_copy(src_ref=hin[a], dst_ref=hout[a], send_sem=send.at[a],
                                              recv_sem=recv.at[a], device_id=(x, y, 1 - c), device_id_type=MESH)
                 for a in range(n)]
        for cp in swaps:
            cp.start()
        for cp in swaps:
            cp.wait()

    return pl.pallas_call(
        body, name="pair_swap", in_specs=[HBM] * n, out_specs=[HBM] * n,
        out_shape=[jax.ShapeDtypeStruct(s.shape, s.dtype) for s in halves],
        scratch_shapes=[pltpu.SemaphoreType.DMA((n,)), pltpu.SemaphoreType.DMA((n,))],
    )(*halves)


def _pair_sum(grad, recv, c, tag):
    _, R, C = grad.shape
    hr = R // 2

    def body(c_ref, a_ref, b_ref, o_ref):
        o_ref[...] = (a_ref[...] + b_ref[...]).astype(BF16)

    return pl.pallas_call(
        body, name="pair_sum_" + tag,
        grid_spec=pltpu.PrefetchScalarGridSpec(
            num_scalar_prefetch=1, grid=(N_CHIPS,),
            in_specs=[pl.BlockSpec((1, hr, C), lambda s, cr: (s, cr[0], 0)),
                      pl.BlockSpec((1, hr, C), lambda s, cr: (s, 0, 0))],
            out_specs=pl.BlockSpec((1, hr, C), lambda s, cr: (s, 0, 0))),
        out_shape=jax.ShapeDtypeStruct((N_CHIPS, hr, C), BF16),
    )(c, grad, recv)


def _chip_sum(received, own, chip, tag):
    _, rows, C = received.shape
    tr = rows // 2

    def body(chip_ref, own_ref, r1_ref, r2_ref, r3_ref, o_ref):
        p = [r[0].astype(F32) for r in (own_ref, r1_ref, r2_ref, r3_ref)]
        o_ref[...] = (p[0] + p[1]) + (p[2] + p[3])

    def slot(k):
        return pl.BlockSpec((1, tr, C), lambda i, cr: (jnp.bitwise_xor(cr[0], k), i, 0))

    return pl.pallas_call(
        body, name="chip_sum_" + tag,
        grid_spec=pltpu.PrefetchScalarGridSpec(
            num_scalar_prefetch=1, grid=(rows // tr,), in_specs=[slot(0), slot(1), slot(2), slot(3)],
            out_specs=pl.BlockSpec((tr, C), lambda i, cr: (i, 0))),
        out_shape=jax.ShapeDtypeStruct((rows, C), F32),
    )(chip, own, received, received, received)


def _adamw_math(w, g, m, v):
    m = ADAM_B1 * m + (1.0 - ADAM_B1) * g
    v = ADAM_B2 * v + (1.0 - ADAM_B2) * (g * g)
    m_hat = m / (1.0 - ADAM_B1 ** ADAM_STEP)
    v_hat = v / (1.0 - ADAM_B2 ** ADAM_STEP)
    delta = -ADAM_LR * (m_hat / (jnp.sqrt(v_hat) + ADAM_EPS) + ADAM_WD * w)
    return delta, m, v


def _adamw(w, g_mine, g_other, m, v, c, tag):
    R, C = w.shape
    tr = R // 4

    def body(c_ref, w_ref, gm_ref, go_ref, m_ref, v_ref, g_ref, d_ref, nm_ref, nv_ref):
        g = jnp.where(pl.program_id(0) == c_ref[0], gm_ref[...], go_ref[...])
        g_ref[...] = g
        d_ref[...], nm_ref[...], nv_ref[...] = _adamw_math(w_ref[...], g, m_ref[...], v_ref[...])

    blk = pl.BlockSpec((tr, C), lambda h, i, cr: (2 * h + i, 0))
    half = pl.BlockSpec((tr, C), lambda h, i, cr: (i, 0))
    return pl.pallas_call(
        body, name="adamw_" + tag,
        grid_spec=pltpu.PrefetchScalarGridSpec(
            num_scalar_prefetch=1, grid=(2, 2), in_specs=[blk, half, half, blk, blk], out_specs=[blk] * 4),
        out_shape=[jax.ShapeDtypeStruct((R, C), F32)] * 4,
    )(c, w, g_mine, g_other, m, v)


def _small_update(all_small, w, m, v):
    def body(a_ref, w_ref, m_ref, v_ref, g_ref, d_ref, nm_ref, nv_ref):
        g = ((a_ref[0] + a_ref[1]) + (a_ref[2] + a_ref[3])) + ((a_ref[4] + a_ref[5]) + (a_ref[6] + a_ref[7]))
        g_ref[...] = g
        d_ref[...], nm_ref[...], nv_ref[...] = _adamw_math(w_ref[...], g, m_ref[...], v_ref[...])

    return pl.pallas_call(
        body, name="small_update", out_shape=[jax.ShapeDtypeStruct(w.shape, F32)] * 4,
    )(all_small, w, m, v)


SMALL_ROWS = (("attn", 0, 0), ("ffn", 1, 0), ("dil", 2, 0), ("sb", 2, GROUP), ("q", 3, 0), ("k", 3, HEAD_DIM),
              ("loss", 4, 0))


def _pack_small(vals, D):
    rows = [jnp.zeros((1, D), F32) for _ in range(8)]
    for name, r, off in SMALL_ROWS:
        if name in vals:
            rows[r] = lax.dynamic_update_slice(rows[r], vals[name].astype(F32), (0, off))
    return jnp.concatenate(rows, axis=0)


def _unpack_small(packed, vals):
    return {name: packed[r:r + 1, off:off + vals[name].shape[1]] for name, r, off in SMALL_ROWS if name in vals}


def kernel(x, attn_norm_w, w_in, q_norm_w, k_norm_w, dil_out_norm_w, sb_out_norm_w, w_out, ffn_norm_w, w_gate, w_up, w_down, loss_target, m_attn_norm_w, m_w_in, m_q_norm_w, m_k_norm_w, m_dil_out_norm_w, m_sb_out_norm_w, m_w_out, m_ffn_norm_w, m_w_gate, m_w_up, m_w_down, v_attn_norm_w, v_w_in, v_q_norm_w, v_k_norm_w, v_dil_out_norm_w, v_sb_out_norm_w, v_w_out, v_ffn_norm_w, v_w_gate, v_w_up, v_w_down):
    D = x.shape[-1]
    big_names = ("w_in", "w_out", "w_gate", "w_up", "w_down")
    flipped = ("w_gate", "w_up")
    tr = lambda a: jnp.swapaxes(a[0], 0, 1)
    big_w = dict(w_in=w_in[0], w_out=w_out[0], w_gate=tr(w_gate), w_up=tr(w_up), w_down=w_down[0])
    big_m = dict(w_in=m_w_in[0], w_out=m_w_out[0], w_gate=tr(m_w_gate), w_up=tr(m_w_up), w_down=m_w_down[0])
    big_v = dict(w_in=v_w_in[0], w_out=v_w_out[0], w_gate=tr(v_w_gate), w_up=tr(v_w_up), w_down=v_w_down[0])
    small_w = dict(attn=attn_norm_w, q=q_norm_w, k=k_norm_w, dil=dil_out_norm_w, sb=sb_out_norm_w, ffn=ffn_norm_w)
    small_m = dict(attn=m_attn_norm_w, q=m_q_norm_w, k=m_k_norm_w, dil=m_dil_out_norm_w, sb=m_sb_out_norm_w,
                   ffn=m_ffn_norm_w)
    small_v = dict(attn=v_attn_norm_w, q=v_q_norm_w, k=v_k_norm_w, dil=v_dil_out_norm_w, sb=v_sb_out_norm_w,
                   ffn=v_ffn_norm_w)

    c = lax.axis_index("c").astype(jnp.int32).reshape(1)
    chip = (2 * lax.axis_index("x") + lax.axis_index("y")).astype(jnp.int32).reshape(1)
    (w_in_g,) = _gather_weights([big_w["w_in"].astype(BF16)])
    w_out_slots = _own_slots(big_w["w_out"].astype(BF16))
    ffn_slots = [_own_slots(big_w[n].astype(BF16)) for n in FFN_NAMES]

    loss_parts, grad_x, small_g, dw_in, dw_out, ffn_halves = _device_step(
        x[0], loss_target[0], attn_norm_w, q_norm_w, k_norm_w, dil_out_norm_w, sb_out_norm_w, ffn_norm_w,
        w_in_g, w_out_slots, ffn_slots, c, chip)
    small_g["loss"] = (jnp.sum(loss_parts[:, 0, 0]) * (0.5 / D)).reshape(1, 1)

    late = [dw_in, dw_out.reshape(N_CHIPS, -1, D)]
    *from_pair, all_small = _pair_exchange(late, _pack_small(small_g, D))
    chip_parts = [_pair_sum(g, r, c, n) for g, r, n in zip(late, from_pair, big_names)]
    from_chips = _chip_exchange(chip_parts)
    halves = [_chip_sum(r, p, chip, n) for r, p, n in zip(from_chips, chip_parts, big_names)] + ffn_halves
    others = _pair_swap(halves)
    big_out = {n: _adamw(big_w[n], mine, other, big_m[n], big_v[n], c, n)
               for n, mine, other in zip(big_names, halves, others)}
    sg, sd, sm, sv = _small_update(all_small, _pack_small(small_w, D), _pack_small(small_m, D),
                                   _pack_small(small_v, D))
    small_out = [_unpack_small(t, small_w) for t in (sg, sd, sm, sv)]

    order = (("attn", None), (None, "w_in"), ("q", None), ("k", None), ("dil", None), ("sb", None),
             (None, "w_out"), ("ffn", None), (None, "w_gate"), (None, "w_up"), (None, "w_down"))
    outs = [sg[4, 0], grad_x[None]]
    for kind in range(4):
        for s_name, b_name in order:
            if s_name is not None:
                outs.append(small_out[kind][s_name])
            else:
                res = big_out[b_name][kind]
                outs.append((jnp.swapaxes(res, 0, 1) if b_name in flipped else res)[None])
    return tuple(outs)
```

```python
import functools

import jax
import jax.numpy as jnp
from jax import lax
from jax.experimental import pallas as pl
from jax.experimental.pallas import tpu as pltpu

F32 = jnp.float32
BF16 = jnp.bfloat16
MESH = pl.DeviceIdType.MESH

HEAD_DIM = 64
GROUP = 512
BLOCK = 128
LANES = 128
N_CHIPS = 4
N_DEV = 8
EPS = 1e-6
ROPE_THETA = 10000.0
DILATIONS = (1, 4, 16)
NEG = -1e30

ADAM_LR = 0.001
ADAM_B1 = 0.9
ADAM_B2 = 0.999
ADAM_EPS = 1e-08
ADAM_WD = 0.01
ADAM_STEP = 10


def _dot(a, b):
    return jnp.dot(a, b, preferred_element_type=F32)


def _dot_nt(a, b):
    return lax.dot_general(a, b, (((1,), (1,)), ((), ())), preferred_element_type=F32)


def _dot_tn(a, b):
    return lax.dot_general(a, b, (((0,), (0,)), ((), ())), preferred_element_type=F32)


def _split(v):
    hi = lax.bitcast_convert_type(lax.bitcast_convert_type(v, jnp.uint32) & jnp.uint32(0xFFFF0000), F32)
    return hi.astype(BF16), (v - hi).astype(BF16)


def _segsum(v, g):
    hi, lo = _split(v)
    w = g.shape[1]
    return jnp.concatenate([_dot(jnp.concatenate([hi[:, c:c + w], lo[:, c:c + w]], axis=1), g)
                            for c in range(0, v.shape[1], w)], axis=1)


def _rot_half(x):
    outs = []
    for c in range(x.shape[1] // LANES):
        xc = x[:, c * LANES:(c + 1) * LANES]
        lane = lax.broadcasted_iota(jnp.int32, xc.shape, 1)
        first = (lane % HEAD_DIM) < (HEAD_DIM // 2)
        outs.append(jnp.where(first, pltpu.roll(xc, LANES - 32, 1), pltpu.roll(xc, 32, 1)))
    return outs[0] if len(outs) == 1 else jnp.concatenate(outs, axis=1)


def _rms(x):
    return lax.rsqrt(jnp.mean(x * x, axis=-1, keepdims=True) + EPS)


def _rms_bwd(dy, x, w):
    rstd = _rms(x)
    xh = x * rstd
    dxh = dy * w
    dx = rstd * (dxh - xh * jnp.mean(dxh * xh, axis=-1, keepdims=True))
    return dx, dy * xh


def _sigmoid(x):
    return 1.0 / (1.0 + jnp.exp(-x))


def _sum4(ref):
    p = [ref[j].astype(F32) for j in range(N_CHIPS)]
    return (p[0] + p[1]) + (p[2] + p[3])


def _full(shape):
    n = len(shape)
    return pl.BlockSpec(shape, lambda *_: (0,) * n)


def _strided_spec(tm, r):
    return pl.BlockSpec((r, tm // r, GROUP), lambda i: (0, i, 0))


def _strided_shape(S, r, dtype):
    return jax.ShapeDtypeStruct((r, S // r, GROUP), dtype)


def _to_strided(scr, val, outs):
    chunks = range(GROUP // LANES)
    for k in chunks:
        scr[k] = val[:, _lanes(k)]
    for r, o_ref in outs:
        if r == 1:
            o_ref[0] = val.astype(o_ref.dtype)
            continue
        n = val.shape[0] // r
        for c in range(r):
            rows = pl.ds(c, n, stride=r)
            o_ref[c] = jnp.concatenate([scr.at[k][rows, :] for k in chunks], axis=1).astype(o_ref.dtype)


def _from_strided(scr, ref):
    r, n, _ = ref.shape
    if r == 1:
        return ref[0].astype(F32)
    chunks = range(GROUP // LANES)
    for c in range(r):
        plane = ref[c].astype(F32)
        for k in chunks:
            scr.at[k][pl.ds(c, n, stride=r), :] = plane[:, _lanes(k)]
    return jnp.concatenate([scr[k] for k in chunks], axis=1)


def _strided_scratch(tm):
    return pltpu.VMEM((GROUP // LANES, tm, LANES), F32)


def _tile4(t):
    return jnp.concatenate([t] * (GROUP // LANES), axis=1)


def _in_proj_fwd(x, attn_w, w_in_g, qw, kw, cos_t, sin_t, seg_ones, slots):
    S, D = x.shape
    tm = 512
    wcols = w_in_g.shape[2]
    nd = len(DILATIONS)
    ns = len(slots)

    def body(x_ref, aw_ref, w_ref, qw_ref, kw_ref, cos_ref, sin_ref, g_ref, *rest):
        slot_in, (h_ref, qa_ref, ka_ref), rest = rest[:ns], rest[ns:ns + 3], rest[ns + 3:]
        q_refs, k_refs, v_refs = rest[:nd], rest[nd:2 * nd], rest[2 * nd:3 * nd]
        qs_ref, ks_ref, vs_ref = rest[3 * nd:3 * nd + 3]
        slot_out, scr, sems = rest[3 * nd + 3:3 * nd + 3 + ns], rest[3 * nd + 3 + ns], rest[3 * nd + 4 + ns:]
        finish = _hosted_gathers(slot_in, slot_out, *sems, pl.program_id(0), S // tm) if ns else None
        xv = x_ref[...]
        h = (xv * _rms(xv) * aw_ref[...]).astype(BF16)
        h_ref[...] = h
        proj = jnp.concatenate([_dot(h, w_ref[j]) for j in range(N_CHIPS)], axis=1)
        qa = proj[:, 0 * GROUP:1 * GROUP]
        ka = proj[:, 1 * GROUP:2 * GROUP]
        qa_ref[...] = qa
        ka_ref[...] = ka
        _to_strided(scr, proj[:, 2 * GROUP:3 * GROUP], list(zip(DILATIONS, v_refs)))
        qs_ref[...] = proj[:, 3 * GROUP:4 * GROUP].astype(BF16)
        ks_ref[...] = proj[:, 4 * GROUP:5 * GROUP].astype(BF16)
        vs_ref[...] = proj[:, 5 * GROUP:6 * GROUP].astype(BF16)
        g = g_ref[...]
        cos = _tile4(cos_ref[...])
        sin = _tile4(sin_ref[...])
        for t, w_r, o_rs in ((qa, qw_ref, q_refs), (ka, kw_ref, k_refs)):
            rstd = lax.rsqrt(_segsum(t * t, g) * (1.0 / HEAD_DIM) + EPS)
            tn = t * rstd * w_r[...]
            _to_strided(scr, tn * cos + _rot_half(tn) * sin, list(zip(DILATIONS, o_rs)))
        if finish is not None:
            finish()

    row = lambda i: (i, 0)
    tile = lambda n, dt: jax.ShapeDtypeStruct((S, n), dt)
    planes = [_strided_spec(tm, r) for r in DILATIONS]
    h_in, h_out, h_shape, h_sems = _hosted_specs(slots)
    n_out = 6 + 3 * nd
    return pl.pallas_call(
        body, name="in_proj_fwd", grid=(S // tm,),
        in_specs=[pl.BlockSpec((tm, D), row), _full((1, D)), _full((N_CHIPS, D, wcols)),
                  _full((1, GROUP)), _full((1, GROUP)),
                  pl.BlockSpec((tm, LANES), row), pl.BlockSpec((tm, LANES), row),
                  _full((GROUP, GROUP // 2))] + h_in,
        out_specs=[pl.BlockSpec((tm, D), row)] + [pl.BlockSpec((tm, GROUP), row)] * 2 + planes * 3
                  + [pl.BlockSpec((tm, GROUP), row)] * 3 + h_out,
        out_shape=[tile(D, BF16), tile(GROUP, F32), tile(GROUP, F32)]
                  + [_strided_shape(S, r, BF16) for r in DILATIONS] * 3 + [tile(GROUP, BF16)] * 3 + h_shape,
        input_output_aliases={8 + a: n_out + a for a in range(ns)},
        scratch_shapes=[_strided_scratch(tm)] + h_sems,
    )(x, attn_w, w_in_g, qw, kw, cos_t, sin_t, seg_ones, *slots)


DIL_PLANES = 1


def _dil_fwd(q, k, v, slots):
    r, L, _ = q.shape
    nb = L // BLOCK
    P = GROUP // LANES
    PL = min(DIL_PLANES, r)
    ns = len(slots)
    units = [(pp, hp) for pp in range(PL) for hp in range(P)]

    def body(q_ref, kc_ref, kp_ref, vc_ref, vp_ref, *rest):
        o_ref, l_ref = rest[ns:ns + 2]
        n = pl.program_id(1)
        finish = (_hosted_gathers(rest[:ns], rest[ns + 2:2 * ns + 2], *rest[2 * ns + 2:],
                                  pl.program_id(0) * nb + n, (r // PL) * nb) if ns else None)
        rowi = lax.broadcasted_iota(jnp.int32, (BLOCK, BLOCK), 0)
        coli = lax.broadcasted_iota(jnp.int32, (BLOCK, BLOCK), 1)
        first = coli < HEAD_DIM
        masks = (coli <= rowi, jnp.logical_and(coli >= rowi, n > 0))
        s2 = {}
        for pp, hp in units:
            q2 = _scaled(q_ref[pp, :, _lanes(hp)])
            for b, k_ref in enumerate((kc_ref, kp_ref)):
                s2[pp, hp, b] = _dot_nt(q2, _by_head(k_ref[pp, :, _lanes(hp)], first))
        ps, inv, lse = {}, {}, {}
        for pp, hp in units:
            for h in range(2):
                s = [jnp.where(masks[b], s2[pp, hp, b][:, h * BLOCK:(h + 1) * BLOCK], NEG) for b in range(2)]
                m = jnp.maximum(jnp.max(s[0], axis=1, keepdims=True), jnp.max(s[1], axis=1, keepdims=True))
                p = [jnp.exp(s[b] - m) for b in range(2)]
                den = jnp.sum(p[0], axis=1, keepdims=True) + jnp.sum(p[1], axis=1, keepdims=True)
                ps[pp, hp, h] = [p[b].astype(BF16) for b in range(2)]
                inv[pp, hp, h] = 1.0 / den
                lse[pp, hp, h] = m + jnp.log(den)
        for pp, hp in units:
            o = jnp.zeros((BLOCK, LANES), F32)
            for b, v_ref in enumerate((vc_ref, vp_ref)):
                o = o + _dot(jnp.concatenate([ps[pp, hp, 0][b], ps[pp, hp, 1][b]], axis=1),
                             _by_head(v_ref[pp, :, _lanes(hp)], first))
            o_ref[pp, :, _lanes(hp)] = o * jnp.where(first, inv[pp, hp, 0], inv[pp, hp, 1])
            l_ref[pp, :, _lanes(hp)] = jnp.where(first, lse[pp, hp, 0], lse[pp, hp, 1])
        if finish is not None:
            finish()

    cur = pl.BlockSpec((PL, BLOCK, GROUP), lambda c, n: (c, n, 0))
    prev = pl.BlockSpec((PL, BLOCK, GROUP), lambda c, n: (c, jnp.maximum(n - 1, 0), 0))
    h_in, h_out, h_shape, h_sems = _hosted_specs(slots)
    return pl.pallas_call(
        body, name="dil_fwd_r%d" % r, grid=(r // PL, nb),
        in_specs=[cur, cur, prev, cur, prev] + h_in, out_specs=[cur, cur] + h_out,
        out_shape=[jax.ShapeDtypeStruct(q.shape, F32)] * 2 + h_shape,
        input_output_aliases={5 + a: 2 + a for a in range(ns)},
        scratch_shapes=h_sems,
    )(q, k, k, v, v, *slots)


def _dil_bwd(q, k, v, do, lse, delta, part):
    r, L, _ = q.shape
    nb = L // BLOCK
    P = GROUP // LANES
    PL = min(DIL_PLANES, r)
    scale = HEAD_DIM ** -0.5
    units = [(pp, hp) for pp in range(PL) for hp in range(P)]

    def body(qc_ref, qn_ref, doc_ref, don_ref, lc_ref, ln_ref, dc_ref, dn_ref, k_ref, v_ref, part_in,
             dq_ref, dk_ref, dv_ref, part_out, carry, send, recv):
        j = pl.program_id(1)
        step = pl.program_id(0) * nb + j
        start, finish = _chip_send_plan(part_in, part_out, send, recv)
        pl.when(step == 0)(start)
        rowi = lax.broadcasted_iota(jnp.int32, (BLOCK, BLOCK), 0)
        coli = lax.broadcasted_iota(jnp.int32, (BLOCK, BLOCK), 1)
        first = coli < HEAD_DIM
        sides = ((qc_ref, doc_ref, lc_ref, dc_ref, coli <= rowi),
                 (qn_ref, don_ref, ln_ref, dn_ref, jnp.logical_and(coli >= rowi, j < nb - 1)))

        @pl.when(j == 0)
        def _():
            carry[...] = jnp.zeros_like(carry)

        kcat, q2, do2, s2, dp2 = {}, {}, {}, {}, {}
        for pp, hp in units:
            kcat[pp, hp] = _by_head(k_ref[pp, :, _lanes(hp)], first)
            vcat = _by_head(v_ref[pp, :, _lanes(hp)], first)
            for x, (q_r, do_r, _, _, _) in enumerate(sides):
                q2[pp, hp, x] = _scaled(q_r[pp, :, _lanes(hp)])
                do2[pp, hp, x] = do_r[pp, :, _lanes(hp)]
                s2[pp, hp, x] = _dot_nt(q2[pp, hp, x], kcat[pp, hp])
                dp2[pp, hp, x] = _dot_nt(do2[pp, hp, x], vcat)
        pcat, dscat = {}, {}
        for pp, hp in units:
            for x, (_, _, l_r, d_r, msk) in enumerate(sides):
                ps, dss = [], []
                for h in range(2):
                    col = hp * LANES + h * HEAD_DIM
                    half = slice(h * BLOCK, (h + 1) * BLOCK)
                    p = jnp.where(msk, jnp.exp(s2[pp, hp, x][:, half] - l_r[pp, :, col:col + 1]), 0.0)
                    ps.append(p.astype(BF16))
                    dss.append((p * (dp2[pp, hp, x][:, half] - d_r[pp, :, col:col + 1])).astype(BF16))
                pcat[pp, hp, x] = jnp.concatenate(ps, axis=1)
                dscat[pp, hp, x] = jnp.concatenate(dss, axis=1)
        for pp, hp in units:
            dv2 = _dot_tn(pcat[pp, hp, 0], do2[pp, hp, 0]) + _dot_tn(pcat[pp, hp, 1], do2[pp, hp, 1])
            dk2 = _dot_tn(dscat[pp, hp, 0], q2[pp, hp, 0]) + _dot_tn(dscat[pp, hp, 1], q2[pp, hp, 1])
            dv_ref[pp, :, _lanes(hp)] = jnp.where(first, dv2[:BLOCK], dv2[BLOCK:]).astype(BF16)
            dk_ref[pp, :, _lanes(hp)] = jnp.where(first, dk2[:BLOCK], dk2[BLOCK:]).astype(BF16)
            dq_ref[pp, :, _lanes(hp)] = (carry[pp, :, _lanes(hp)]
                                         + _dot(dscat[pp, hp, 0], kcat[pp, hp]) * scale).astype(BF16)
            carry[pp, :, _lanes(hp)] = _dot(dscat[pp, hp, 1], kcat[pp, hp]) * scale
        pl.when(step == (r // PL) * nb - 1)(finish)

    cur = pl.BlockSpec((PL, BLOCK, GROUP), lambda c, n: (c, n, 0))
    nxt = pl.BlockSpec((PL, BLOCK, GROUP), lambda c, n: (c, jnp.minimum(n + 1, nb - 1), 0))
    return pl.pallas_call(
        body, name="dil_bwd_r%d" % r, grid=(r // PL, nb),
        in_specs=[cur, nxt, cur, nxt, cur, nxt, cur, nxt, cur, cur, HBM], out_specs=[cur, cur, cur, HBM],
        out_shape=[jax.ShapeDtypeStruct(q.shape, BF16)] * 3 + [jax.ShapeDtypeStruct(part.shape, part.dtype)],
        scratch_shapes=[pltpu.VMEM((PL, BLOCK, GROUP), F32), pltpu.SemaphoreType.DMA((3,)),
                        pltpu.SemaphoreType.DMA((3,))],
    )(q, q, do, do, lse, lse, delta, delta, k, v, part)


SB_TILES = 2
SB_PAIRS_FWD = 4
SB_PAIRS_BWD = 2
SB_DEAD = -110.0


def _lanes(hp):
    return slice(hp * LANES, (hp + 1) * LANES)


def _sb_logits(z, valid):
    e = jnp.exp(-jnp.abs(z))
    lb = jnp.minimum(z, 0.0) - jnp.log(1.0 + e)
    lk = lb - z
    if valid is not None:
        lk = jnp.where(valid, lk, 0.0)
    return e, lb, lk


def _by_head(t, first):
    zero = jnp.zeros_like(t)
    return jnp.concatenate([jnp.where(first, t, zero), jnp.where(first, zero, t)], axis=0)


def _sb_valid(i, j):
    rowi = lax.broadcasted_iota(jnp.int32, (BLOCK, BLOCK), 0)
    coli = lax.broadcasted_iota(jnp.int32, (BLOCK, BLOCK), 1)
    return (coli - rowi) < (i - j) * BLOCK


def _scaled(q):
    return (q.astype(F32) * (HEAD_DIM ** -0.5)).astype(BF16)


def _hosted_gathers(refs_in, refs_out, send, recv, step, steps):
    plans = [_gather_plan(refs_in[a], refs_out[a], send.at[pl.ds(6 * a, 6)], recv.at[pl.ds(6 * a, 6)])
             for a in range(len(refs_in))]
    for stage, at in ((0, 0), (1, (2 * steps) // 3)):
        @pl.when(step == at)
        def _():
            for plan in plans:
                plan[stage]()

    def finish():
        @pl.when(step == steps - 1)
        def _():
            for plan in plans:
                plan[2]()

    return finish


def _hosted_specs(slots):
    n = len(slots)
    sems = [pltpu.SemaphoreType.DMA((6 * n,))] * 2 if n else []
    return [HBM] * n, [HBM] * n, [jax.ShapeDtypeStruct(s.shape, s.dtype) for s in slots], sems


def _sb_fwd(qs, ks, vs, tri_later, slots):
    S = qs.shape[0]
    P = SB_PAIRS_FWD
    W = P * LANES
    ns = len(slots)

    def body(q_ref, k_ref, v_ref, u_ref, *rest):
        o_ref, lt_ref, from_ref = rest[ns:ns + 3]
        i = pl.program_id(1)
        finish = _hosted_gathers(rest[:ns], rest[ns + 3:2 * ns + 3], *rest[2 * ns + 3:], i, S // BLOCK) if ns else None
        first = lax.broadcasted_iota(jnp.int32, (BLOCK, LANES), 1) < HEAD_DIM
        q2 = [_scaled(q_ref[:, _lanes(hp)]) for hp in range(P)]

        def walk(tiles, carry):
            runs, accs = list(carry[0]), list(carry[1])
            units = [(t, hp) for t in range(len(tiles)) for hp in range(P)]
            offs = [pl.multiple_of(j * BLOCK, BLOCK) for j, _ in tiles]
            valids = [_sb_valid(i, j) if diag else None for j, diag in tiles]
            z2s, lbs, c2s = {}, {}, {}
            for t, hp in units:
                z2s[t, hp] = _dot_nt(q2[hp], _by_head(k_ref[pl.ds(offs[t], BLOCK), _lanes(hp)], first))
            for t, hp in units:
                for h in range(2):
                    _, lb, lk = _sb_logits(z2s[t, hp][:, h * BLOCK:(h + 1) * BLOCK], valids[t])
                    lbs[t, hp, h] = lb
                    c2s[t, hp, h] = _dot(jnp.concatenate(_split(lk), axis=1), u_ref[...])
            for t, hp in units:
                a2 = []
                for h in range(2):
                    a = jnp.exp(lbs[t, hp, h] + c2s[t, hp, h][:, :BLOCK] + runs[2 * hp + h])
                    if valids[t] is not None:
                        a = jnp.where(valids[t], a, 0.0)
                    a2.append(a.astype(BF16))
                    runs[2 * hp + h] = runs[2 * hp + h] + c2s[t, hp, h][:, BLOCK:]
                vcat = _by_head(v_ref[pl.ds(offs[t], BLOCK), _lanes(hp)], first)
                accs[hp] = accs[hp] + _dot(jnp.concatenate(a2, axis=1), vcat)
            return tuple(runs), tuple(accs)

        def chunk(ci, carry):
            return walk([(ci * SB_TILES + t, False) for t in reversed(range(SB_TILES))], carry)

        def alive(runs):
            top = functools.reduce(jnp.maximum, runs)
            return (jnp.max(top) > SB_DEAD).astype(jnp.int32)

        def step(c):
            t, _, runs, accs = c
            runs, accs = chunk(nfull - 1 - t, (runs, accs))
            return t + 1, alive(runs), runs, accs

        zero = jnp.zeros((BLOCK, LANES), F32)
        nfull = i // SB_TILES
        ragged = [functools.partial(walk, [(i, True)] + [(i - 1 - m, False) for m in range(extra)])
                  for extra in range(SB_TILES)]
        runs, accs = lax.switch(i % SB_TILES, ragged, ((zero,) * (2 * P), (zero,) * P))
        done, _, runs, accs = lax.while_loop(lambda c: jnp.logical_and(c[0] < nfull, c[1] > 0), step,
                                             (jnp.int32(0), alive(runs), runs, accs))
        for hp in range(P):
            o_ref[:, _lanes(hp)] = accs[hp]
            lt_ref[:, _lanes(hp)] = jnp.where(first, runs[2 * hp], runs[2 * hp + 1])
        from_ref[...] = jnp.full(from_ref.shape, nfull - done, jnp.int32)
        if finish is not None:
            finish()

    assert W == GROUP
    blk = pl.BlockSpec((BLOCK, W), lambda hp, i: (i, hp))
    col = pl.BlockSpec((S, W), lambda hp, i: (0, hp))
    h_in, h_out, h_shape, h_sems = _hosted_specs(slots)
    return pl.pallas_call(
        body, name="sb_fwd", grid=(GROUP // W, S // BLOCK),
        in_specs=[blk, col, col, _full((2 * BLOCK, 2 * BLOCK))] + h_in,
        out_specs=[blk, blk, pl.BlockSpec((1, 8, LANES), lambda hp, i: (i, 0, 0))] + h_out,
        out_shape=[jax.ShapeDtypeStruct((S, GROUP), F32)] * 2
                  + [jax.ShapeDtypeStruct((S // BLOCK, 8, LANES), jnp.int32)] + h_shape,
        input_output_aliases={4 + a: 3 + a for a in range(ns)},
        scratch_shapes=h_sems,
    )(qs, ks, vs, tri_later, *slots)


def _sb_bwd(first_chunk, qs, ks, vs, do, ltot, tri_upto, tri_before):
    S = qs.shape[0]
    P = SB_PAIRS_BWD
    W = P * LANES

    def body(from_ref, q_ref, k_ref, v_ref, do_ref, lt_ref, w_ref, x_ref, dq_ref, dk_ref, dv_ref):
        i = pl.program_id(1)

        @pl.when(i == 0)
        def _():
            dk_ref[...] = jnp.zeros_like(dk_ref)
            dv_ref[...] = jnp.zeros_like(dv_ref)

        first = lax.broadcasted_iota(jnp.int32, (BLOCK, LANES), 1) < HEAD_DIM
        q2 = [_scaled(q_ref[:, _lanes(hp)]) for hp in range(P)]
        do2 = [do_ref[:, _lanes(hp)] for hp in range(P)]
        totals = [jnp.broadcast_to(lt_ref[:, n * HEAD_DIM:n * HEAD_DIM + 1], (BLOCK, LANES)) for n in range(2 * P)]

        def walk(tiles, carry):
            keeps, grads, dqs = list(carry[0]), list(carry[1]), list(carry[2])
            units = [(t, hp) for t in range(len(tiles)) for hp in range(P)]
            offs = [pl.multiple_of(j * BLOCK, BLOCK) for j, _ in tiles]
            valids = [_sb_valid(i, j) if diag else None for j, diag in tiles]
            kcat, z2, da2, es, lbs, c2s, as_, des, p2s = {}, {}, {}, {}, {}, {}, {}, {}, {}
            for t, hp in units:
                kcat[t, hp] = _by_head(k_ref[pl.ds(offs[t], BLOCK), _lanes(hp)], first)
                z2[t, hp] = _dot_nt(q2[hp], kcat[t, hp])
                da2[t, hp] = _dot_nt(do2[hp], _by_head(v_ref[pl.ds(offs[t], BLOCK), _lanes(hp)], first))
            for t, hp in units:
                for h in range(2):
                    es[t, hp, h], lbs[t, hp, h], lk = _sb_logits(z2[t, hp][:, h * BLOCK:(h + 1) * BLOCK], valids[t])
                    c2s[t, hp, h] = _dot(jnp.concatenate(_split(lk), axis=1), w_ref[...])
            for t, hp in units:
                for h in range(2):
                    n = 2 * hp + h
                    a = jnp.exp(lbs[t, hp, h] + (totals[n] - (keeps[n] + c2s[t, hp, h][:, :BLOCK])))
                    if valids[t] is not None:
                        a = jnp.where(valids[t], a, 0.0)
                    keeps[n] = keeps[n] + c2s[t, hp, h][:, BLOCK:]
                    de = a * da2[t, hp][:, h * BLOCK:(h + 1) * BLOCK]
                    as_[t, hp, h], des[t, hp, h] = a.astype(BF16), de
                    p2s[t, hp, h] = _dot(jnp.concatenate(_split(de), axis=1), x_ref[...])
            for t, hp in units:
                dz2 = []
                for h in range(2):
                    n = 2 * hp + h
                    e = es[t, hp, h]
                    sig = jnp.where(z2[t, hp][:, h * BLOCK:(h + 1) * BLOCK] >= 0.0, 1.0, e) / (1.0 + e)
                    dz = des[t, hp, h] * (1.0 - sig) - (grads[n] + p2s[t, hp, h][:, :BLOCK]) * sig
                    if valids[t] is not None:
                        dz = jnp.where(valids[t], dz, 0.0)
                    grads[n] = grads[n] + p2s[t, hp, h][:, BLOCK:]
                    dz2.append(dz.astype(BF16))
                dzcat = jnp.concatenate(dz2, axis=1)
                dk2 = _dot_tn(dzcat, q2[hp])
                dv2 = _dot_tn(jnp.concatenate([as_[t, hp, 0], as_[t, hp, 1]], axis=1), do2[hp])
                dk_ref[pl.ds(offs[t], BLOCK), _lanes(hp)] += jnp.where(first, dk2[:BLOCK], dk2[BLOCK:])
                dv_ref[pl.ds(offs[t], BLOCK), _lanes(hp)] += jnp.where(first, dv2[:BLOCK], dv2[BLOCK:])
                dqs[hp] = dqs[hp] + _dot(dzcat, kcat[t, hp])
            return tuple(keeps), tuple(grads), tuple(dqs)

        zero = jnp.zeros((BLOCK, LANES), F32)
        nfull = i // SB_TILES
        carry = lax.fori_loop(
            from_ref[i], nfull, lambda ci, c: walk([(ci * SB_TILES + t, False) for t in range(SB_TILES)], c),
            ((zero,) * (2 * P), (zero,) * (2 * P), (zero,) * P))
        ragged = [functools.partial(walk, [(i - m, False) for m in range(extra, 0, -1)] + [(i, True)])
                  for extra in range(SB_TILES)]
        carry = lax.switch(i % SB_TILES, ragged, carry)
        for hp in range(P):
            dq_ref[:, _lanes(hp)] = carry[2][hp] * (HEAD_DIM ** -0.5)

    blk = pl.BlockSpec((BLOCK, W), lambda hp, i, fr: (i, hp))
    col = pl.BlockSpec((S, W), lambda hp, i, fr: (0, hp))
    tri = pl.BlockSpec((2 * BLOCK, 2 * BLOCK), lambda hp, i, fr: (0, 0))
    return pl.pallas_call(
        body, name="sb_bwd",
        grid_spec=pltpu.PrefetchScalarGridSpec(
            num_scalar_prefetch=1, grid=(GROUP // W, S // BLOCK),
            in_specs=[blk, col, col, blk, blk, tri, tri], out_specs=[blk, col, col]),
        out_shape=[jax.ShapeDtypeStruct((S, GROUP), F32)] * 3,
    )(first_chunk, qs, ks, vs, do, ltot, tri_upto, tri_before)


def _out_proj_fwd(o_br, l_br, o_sb, x, w_dil, w_sbn, w_out_g):
    S, D = x.shape
    tm = 512

    def body(o0, o1, o2, l0, l1, l2, os_ref, x_ref, wd_ref, ws_ref, w_ref, od_ref, s0, s1, s2, x1_ref, scr):
        ls = [_from_strided(scr, l) for l in (l0, l1, l2)]
        os_ = [_from_strided(scr, o) for o in (o0, o1, o2)]
        m = jnp.maximum(jnp.maximum(ls[0], ls[1]), ls[2])
        es = [jnp.exp(l - m) for l in ls]
        den = es[0] + es[1] + es[2]
        od = (es[0] * os_[0] + es[1] * os_[1] + es[2] * os_[2]) / den
        od_ref[...] = od
        _to_strided(scr, m + jnp.log(den), list(zip(DILATIONS, (s0, s1, s2))))
        osb = os_ref[...]
        mixed = jnp.concatenate([(od * _rms(od) * wd_ref[...]).astype(BF16),
                                 (osb * _rms(osb) * ws_ref[...]).astype(BF16)], axis=1)
        x1_ref[...] = x_ref[...] + _dot(mixed, w_ref[...])

    row = lambda i: (i, 0)
    g = pl.BlockSpec((tm, GROUP), row)
    d = pl.BlockSpec((tm, D), row)
    planes = [_strided_spec(tm, r) for r in DILATIONS]
    return pl.pallas_call(
        body, name="out_proj_fwd", grid=(S // tm,),
        in_specs=planes * 2 + [g, d, _full((1, GROUP)), _full((1, GROUP)), _full((2 * GROUP, D))],
        out_specs=[g] + planes + [d],
        out_shape=[jax.ShapeDtypeStruct((S, GROUP), F32)] + [_strided_shape(S, r, F32) for r in DILATIONS]
                  + [jax.ShapeDtypeStruct((S, D), F32)],
        scratch_shapes=[_strided_scratch(tm)],
    )(*o_br, *l_br, o_sb, x, w_dil, w_sbn, w_out_g)


def _ffn_fwd(x1, target, ffn_w, wg_g, wu_g, wd_g):
    S, D = x1.shape
    F = wg_g.shape[1]
    tm = 512
    nt = S // tm

    def body(x_ref, t_ref, nw_ref, wg_ref, wu_ref, wd_ref, h_ref, g_ref, u_ref, dy_ref, loss_ref, h_s, acc):
        j = pl.program_id(1)

        @pl.when(j == 0)
        def _():
            xv = x_ref[...]
            h = (xv * _rms(xv) * nw_ref[...]).astype(BF16)
            h_s[...] = h
            h_ref[...] = h
            acc[...] = xv

        h = h_s[...]
        g = _dot_nt(h, wg_ref[0])
        u = _dot_nt(h, wu_ref[0])
        g_ref[0] = g.astype(BF16)
        u_ref[0] = u.astype(BF16)
        a = (g * _sigmoid(g) * u).astype(BF16)
        acc[...] += _dot(a, wd_ref[0])

        @pl.when(j == N_CHIPS - 1)
        def _():
            err = acc[...] - t_ref[...]
            dy_ref[...] = err * (1.0 / D)
            loss_ref[...] = jnp.full(loss_ref.shape, jnp.sum(err * err), F32)

    row = lambda t, j: (t, 0)
    shard = lambda t, j: (j, 0, 0)
    act = lambda t, j: (j, t, 0)
    return pl.pallas_call(
        body, name="ffn_fwd", grid=(nt, N_CHIPS),
        in_specs=[pl.BlockSpec((tm, D), row), pl.BlockSpec((tm, D), row), pl.BlockSpec((1, D), lambda t, j: (0, 0))]
                 + [pl.BlockSpec((1, F, D), shard)] * 3,
        out_specs=[pl.BlockSpec((tm, D), row), pl.BlockSpec((1, tm, F), act), pl.BlockSpec((1, tm, F), act),
                   pl.BlockSpec((tm, D), row), pl.BlockSpec((1, 8, LANES), lambda t, j: (t, 0, 0))],
        out_shape=[jax.ShapeDtypeStruct((S, D), BF16), jax.ShapeDtypeStruct((N_CHIPS, S, F), BF16),
                   jax.ShapeDtypeStruct((N_CHIPS, S, F), BF16), jax.ShapeDtypeStruct((S, D), F32),
                   jax.ShapeDtypeStruct((nt, 8, LANES), F32)],
        scratch_shapes=[pltpu.VMEM((tm, D), BF16), pltpu.VMEM((tm, D), F32)],
    )(x1, target, ffn_w, wg_g, wu_g, wd_g)


def _ffn_bwd(h2, dy, g, u, wg_g, wu_g, wd_g):
    S, D = dy.shape
    F = wg_g.shape[1]
    tm = 512

    def body(h_ref, dy_ref, g_ref, u_ref, wg_ref, wu_ref, wd_ref, dwg_ref, dwu_ref, dwd_ref, dh_ref):
        t = pl.program_id(1)

        @pl.when(t == 0)
        def _():
            dwg_ref[...] = jnp.zeros_like(dwg_ref)
            dwu_ref[...] = jnp.zeros_like(dwu_ref)
            dwd_ref[...] = jnp.zeros_like(dwd_ref)

        h = h_ref[...]
        dyb = dy_ref[...].astype(BF16)
        gv = g_ref[0].astype(F32)
        uv = u_ref[0].astype(F32)
        da = _dot_nt(dyb, wd_ref[0])
        sg = _sigmoid(gv)
        silu = gv * sg
        du = (da * silu).astype(BF16)
        dg = (da * uv * (sg * (1.0 + gv * (1.0 - sg)))).astype(BF16)
        dwd_ref[0] += _dot_tn((silu * uv).astype(BF16), dyb)
        dwg_ref[0] += _dot_tn(dg, h)
        dwu_ref[0] += _dot_tn(du, h)
        dh_ref[0] = (_dot(dg, wg_ref[0]) + _dot(du, wu_ref[0])).astype(BF16)

    row = lambda j, t: (t, 0)
    shard = lambda j, t: (j, 0, 0)
    act = lambda j, t: (j, t, 0)
    return pl.pallas_call(
        body, name="ffn_bwd", grid=(N_CHIPS, S // tm),
        in_specs=[pl.BlockSpec((tm, D), row), pl.BlockSpec((tm, D), row),
                  pl.BlockSpec((1, tm, F), act), pl.BlockSpec((1, tm, F), act)] + [pl.BlockSpec((1, F, D), shard)] * 3,
        out_specs=[pl.BlockSpec((1, F, D), shard)] * 3 + [pl.BlockSpec((1, tm, D), act)],
        out_shape=[jax.ShapeDtypeStruct((N_CHIPS, F, D), F32)] * 3 + [jax.ShapeDtypeStruct((N_CHIPS, S, D), BF16)],
    )(h2, dy, g, u, wg_g, wu_g, wd_g)


def _out_proj_bwd(dh2p, dy, x1, ffn_w, w_out_g, o_dil, o_sb, w_dil, w_sbn, seg_ones, ffn_grads):
    S, D = dy.shape
    tm = 256
    ng = len(ffn_grads)

    def body(dh_ref, dy_ref, x1_ref, nw_ref, w_ref, od_ref, os_ref, wd_ref, ws_ref, g_ref, *rest):
        gin, rest = rest[:ng], rest[ng:]
        dx1_ref, dod0, dod1, dod2, dos_ref, dl0, dl1, dl2, dw_ref, dnw_ref, dwd_ref, dws_ref = rest[:12]
        gout, (scr, send, recv) = rest[12:12 + ng], rest[12 + ng:]
        i = pl.program_id(0)
        plans = [_pair_send_plan(gin[a], gout[a], send.at[a], recv.at[a]) for a in range(ng)]

        @pl.when(i == 0)
        def _():
            for start, _ in plans:
                start()

        @pl.when(i == 0)
        def _():
            for r_ in (dw_ref, dnw_ref, dwd_ref, dws_ref):
                r_[...] = jnp.zeros_like(r_)

        dh2 = _sum4(dh_ref)
        dxn, dwn = _rms_bwd(dh2, x1_ref[...], nw_ref[...])
        dnw_ref[...] += jnp.sum(dwn, axis=0, keepdims=True)
        dx1 = dy_ref[...] + dxn
        dx1_ref[...] = dx1
        dx1b = dx1.astype(BF16)
        dmix = _dot_nt(dx1b, w_ref[...])
        od = od_ref[...]
        osb = os_ref[...]
        mixed = jnp.concatenate([(od * _rms(od) * wd_ref[...]).astype(BF16),
                                 (osb * _rms(osb) * ws_ref[...]).astype(BF16)], axis=1)
        dw_ref[...] += _dot_tn(mixed, dx1b)
        do, dwo = _rms_bwd(dmix[:, :GROUP], od, wd_ref[...])
        dwd_ref[...] += jnp.sum(dwo, axis=0, keepdims=True)
        _to_strided(scr, do, list(zip(DILATIONS, (dod0, dod1, dod2))))
        _to_strided(scr, _segsum(do * od, g_ref[...]), list(zip(DILATIONS, (dl0, dl1, dl2))))
        do, dwo = _rms_bwd(dmix[:, GROUP:], osb, ws_ref[...])
        dws_ref[...] += jnp.sum(dwo, axis=0, keepdims=True)
        dos_ref[...] = do.astype(BF16)

        @pl.when(i == S // tm - 1)
        def _():
            for _, finish in plans:
                finish()

    row = lambda i: (i, 0)
    gsp = pl.BlockSpec((tm, GROUP), row)
    dsp = pl.BlockSpec((tm, D), row)
    planes = [_strided_spec(tm, r) for r in DILATIONS]
    halves = [jax.ShapeDtypeStruct((g.shape[0], g.shape[1] // 2, g.shape[2]), g.dtype) for g in ffn_grads]
    return pl.pallas_call(
        body, name="out_proj_bwd", grid=(S // tm,),
        in_specs=[pl.BlockSpec((N_CHIPS, tm, D), lambda i: (0, i, 0)), dsp, dsp, _full((1, D)), _full((2 * GROUP, D)),
                  gsp, gsp, _full((1, GROUP)), _full((1, GROUP)), _full((GROUP, GROUP // 2))] + [HBM] * ng,
        out_specs=[dsp] + planes + [gsp] + planes
                  + [_full((2 * GROUP, D)), _full((1, D)), _full((1, GROUP)), _full((1, GROUP))] + [HBM] * ng,
        out_shape=[jax.ShapeDtypeStruct((S, D), F32)] + [_strided_shape(S, r, BF16) for r in DILATIONS]
                  + [jax.ShapeDtypeStruct((S, GROUP), BF16)] + [_strided_shape(S, r, F32) for r in DILATIONS]
                  + [jax.ShapeDtypeStruct((2 * GROUP, D), F32),
                     jax.ShapeDtypeStruct((1, D), F32), jax.ShapeDtypeStruct((1, GROUP), F32),
                     jax.ShapeDtypeStruct((1, GROUP), F32)] + halves,
        scratch_shapes=[_strided_scratch(tm), pltpu.SemaphoreType.DMA((ng,)), pltpu.SemaphoreType.DMA((ng,))],
    )(dh2p, dy, x1, ffn_w, w_out_g, o_dil, o_sb, w_dil, w_sbn, seg_ones, *ffn_grads)


def _attn_in_bwd(dq_br, dk_br, dv_br, dqs, dks, dvs, qa, ka, qw, kw, cos_t, sin_t, seg_ones, h, w_in_g, x, dx1, attn_w):
    S, D = x.shape
    wc = w_in_g.shape[2]
    tm = 256

    def body(q0, q1, q2, k0, k1, k2, v0, v1, v2, dqs_ref, dks_ref, dvs_ref, qa_ref, ka_ref, qw_ref, kw_ref,
             cos_ref, sin_ref, g_ref, h_ref, w_ref, x_ref, dx1_ref, aw_ref,
             gx_ref, dw_ref, daw_ref, dqw_ref, dkw_ref, accq, acck, scr):
        i = pl.program_id(0)

        @pl.when(i == 0)
        def _():
            accq[...] = jnp.zeros_like(accq)
            acck[...] = jnp.zeros_like(acck)
            dw_ref[...] = jnp.zeros_like(dw_ref)
            daw_ref[...] = jnp.zeros_like(daw_ref)

        def branches(refs):
            return (_from_strided(scr, refs[0]) + _from_strided(scr, refs[1])) + _from_strided(scr, refs[2])

        g = g_ref[...]
        cos = _tile4(cos_ref[...])
        sin = _tile4(sin_ref[...])
        pieces = []
        for refs, pre_ref, w_r, acc in (((q0, q1, q2), qa_ref, qw_ref, accq), ((k0, k1, k2), ka_ref, kw_ref, acck)):
            dh = branches(refs)
            dn = dh * cos + _rot_half(dh * sin)
            pre = pre_ref[...]
            rstd = lax.rsqrt(_segsum(pre * pre, g) * (1.0 / HEAD_DIM) + EPS)
            xh = pre * rstd
            acc[...] += jnp.sum(dn * xh, axis=0, keepdims=True)
            dxh = dn * w_r[...]
            pieces.append((rstd * (dxh - xh * (_segsum(dxh * xh, g) * (1.0 / HEAD_DIM)))).astype(BF16))
        pieces += [branches((v0, v1, v2)).astype(BF16), dqs_ref[...].astype(BF16), dks_ref[...].astype(BF16),
                   dvs_ref[...].astype(BF16)]
        dproj = jnp.concatenate(pieces, axis=1)
        hv = h_ref[...]
        dh = jnp.zeros((tm, D), F32)
        for j in range(N_CHIPS):
            dp = dproj[:, j * wc:(j + 1) * wc]
            dw_ref[j] += _dot_tn(hv, dp)
            dh = dh + _dot_nt(dp, w_ref[j])
        dx, dw = _rms_bwd(dh, x_ref[...], aw_ref[...])
        daw_ref[...] += jnp.sum(dw, axis=0, keepdims=True)
        gx_ref[...] = dx1_ref[...] + dx

        @pl.when(i == S // tm - 1)
        def _():
            for acc, o_ref in ((accq, dqw_ref), (acck, dkw_ref)):
                a = acc[...]
                pair = (a[:, 0:LANES] + a[:, LANES:2 * LANES]) + (a[:, 2 * LANES:3 * LANES] + a[:, 3 * LANES:4 * LANES])
                o_ref[...] = pair + pltpu.roll(pair, HEAD_DIM, 1)

    row = lambda i: (i, 0)
    gsp = pl.BlockSpec((tm, GROUP), row)
    dsp = pl.BlockSpec((tm, D), row)
    tab = pl.BlockSpec((tm, LANES), row)
    planes = [_strided_spec(tm, r) for r in DILATIONS]
    return pl.pallas_call(
        body, name="attn_in_bwd", grid=(S // tm,),
        in_specs=planes * 3 + [gsp] * 5 + [_full((1, GROUP)), _full((1, GROUP)), tab, tab, _full((GROUP, GROUP // 2)),
                                          dsp, _full((N_CHIPS, D, wc)), dsp, dsp, _full((1, D))],
        out_specs=[dsp, _full((N_CHIPS, D, wc)), _full((1, D)), _full((1, LANES)), _full((1, LANES))],
        out_shape=[jax.ShapeDtypeStruct((S, D), F32), jax.ShapeDtypeStruct((N_CHIPS, D, wc), F32),
                   jax.ShapeDtypeStruct((1, D), F32), jax.ShapeDtypeStruct((1, LANES), F32),
                   jax.ShapeDtypeStruct((1, LANES), F32)],
        scratch_shapes=[pltpu.VMEM((1, GROUP), F32), pltpu.VMEM((1, GROUP), F32), _strided_scratch(tm)],
    )(*dq_br, *dk_br, *dv_br, dqs, dks, dvs, qa, ka, qw, kw, cos_t, sin_t, seg_ones, h, w_in_g, x, dx1, attn_w)


def _constants(S):
    pos = jnp.arange(S, dtype=F32)
    inv_freq = ROPE_THETA ** (-jnp.arange(0, HEAD_DIM, 2, dtype=F32) / HEAD_DIM)
    ang = pos[:, None] * inv_freq[None, :]
    cos, sin = jnp.cos(ang), jnp.sin(ang)
    cos_t = jnp.concatenate([cos, cos] * 2, axis=1)
    sin_t = jnp.concatenate([-sin, sin] * 2, axis=1)
    idx = jnp.arange(GROUP // 2)
    seg_ones = (idx[:, None] // HEAD_DIM == idx[None, :] // HEAD_DIM).astype(BF16)
    seg_ones = jnp.concatenate([seg_ones, seg_ones], axis=0)
    r = jnp.arange(BLOCK)
    ones = jnp.ones((BLOCK, BLOCK), BF16)
    tris = [jnp.concatenate([jnp.concatenate([m.astype(BF16), ones], axis=1)] * 2, axis=0) for m in
            (r[:, None] > r[None, :],
             r[:, None] <= r[None, :],
             r[:, None] < r[None, :])]
    return cos_t, sin_t, seg_ones, tris


FFN_NAMES = ("w_gate", "w_up", "w_down")


def _device_step(x, target, attn_w, qn_w, kn_w, dil_w, sbn_w, ffn_w, w_in_g, w_out_slots, ffn_slots, core, chip):
    S = x.shape[0]
    cos_t, sin_t, seg_ones, (tri_later, tri_upto, tri_before) = _constants(S)
    reps = GROUP // HEAD_DIM
    qw = jnp.tile(qn_w, (1, reps))
    kw = jnp.tile(kn_w, (1, reps))

    nd = len(DILATIONS)
    h, qa, ka, *rest, wg_g = _in_proj_fwd(x, attn_w, w_in_g, qw, kw, cos_t, sin_t, seg_ones, ffn_slots[:1])
    qh, kh, va, (qs, ks, vs) = rest[:nd], rest[nd:2 * nd], rest[2 * nd:3 * nd], rest[3 * nd:]
    hosted = ([], [w_out_slots], [ffn_slots[2]])
    branches = [_dil_fwd(qh[b], kh[b], va[b], hosted[b]) for b in range(nd)]
    w_out_g, wd_g = branches[1][2].reshape(-1, x.shape[1]), branches[2][2]
    o_sb, ltot, walked, wu_g = _sb_fwd(qs, ks, vs, tri_later, ffn_slots[1:2])
    o_dil, *lse, x1 = _out_proj_fwd([b[0] for b in branches], [b[1] for b in branches], o_sb, x, dil_w, sbn_w, w_out_g)
    h2, g, u, dy, loss_parts = _ffn_fwd(x1, target, ffn_w, wg_g, wu_g, wd_g)

    *ffn_grads, dh2p = _ffn_bwd(h2, dy, g, u, wg_g, wu_g, wd_g)
    dx1, *mid, dw_out, dffn_w, ddil_w, dsbn_w, p0, p1, p2 = _out_proj_bwd(
        dh2p, dy, x1, ffn_w, w_out_g, o_dil, o_sb, dil_w, sbn_w, seg_ones, ffn_grads)
    do_dil, do_sb, delta = mid[:nd], mid[nd], mid[nd + 1:]
    parts = [_pair_sum(gr, fr, core, n) for gr, fr, n in zip(ffn_grads, (p0, p1, p2), FFN_NAMES)]
    dqs, dks, dvs = _sb_bwd(walked[:, 0, 0], qs, ks, vs, do_sb, ltot, tri_upto, tri_before)
    dbr = [_dil_bwd(qh[b], kh[b], va[b], do_dil[b], lse[b], delta[b], parts[b]) for b in range(nd)]
    ffn_halves = [_chip_sum(dbr[b][3], parts[b], chip, FFN_NAMES[b]) for b in range(nd)]
    grad_x, dw_in, dattn_w, dqw, dkw = _attn_in_bwd(
        [b[0] for b in dbr], [b[1] for b in dbr], [b[2] for b in dbr], dqs, dks, dvs,
        qa, ka, qw, kw, cos_t, sin_t, seg_ones, h, w_in_g, x, dx1, attn_w)
    small = dict(attn=dattn_w, q=dqw[:, :HEAD_DIM], k=dkw[:, :HEAD_DIM], dil=ddil_w, sb=dsbn_w, ffn=dffn_w)
    return loss_parts, grad_x, small, dw_in, dw_out, ffn_halves


HBM = pl.BlockSpec(memory_space=pltpu.HBM)
VMEM = pl.BlockSpec(memory_space=pltpu.VMEM)
CHIP_FLIPS = ((1, 0), (0, 1), (1, 1))


def _place():
    return lax.axis_index("x"), lax.axis_index("y"), lax.axis_index("c")


def _flip(v, d):
    return 1 - v if d else v


def _half_rows(c, n):
    return pl.ds(pl.multiple_of(c * (n // 2), 16), n // 2)


def _gather_plan(slot_in, slot_out, send, recv):
    x, y, c = _place()
    p = 2 * x + y
    chips = [(_flip(x, dx), _flip(y, dy)) for dx, dy in CHIP_FLIPS]
    mine, other = _half_rows(c, slot_in.shape[1]), _half_rows(1 - c, slot_in.shape[1])

    def copy(k, src, dst, to):
        return pltpu.make_async_remote_copy(src_ref=src, dst_ref=dst, send_sem=send.at[k], recv_sem=recv.at[k],
                                            device_id=to, device_id_type=MESH)

    def first(k):
        return copy(k, slot_in.at[p, mine], slot_out.at[p, mine], (*chips[k], c))

    def passed(k, rows):
        land = slot_out.at[2 * chips[k][0] + chips[k][1], rows]
        return copy(3 + k, land, land, (x, y, 1 - c))

    def start():
        for k in range(3):
            first(k).start()

    def forward():
        for k in range(3):
            land = slot_out.at[2 * chips[k][0] + chips[k][1], mine]
            copy(k, land, land, (*chips[k], c)).wait_recv()
            passed(k, mine).start()

    def finish():
        for k in range(3):
            passed(k, other).wait_recv()
        for k in range(3):
            first(k).wait_send()
            passed(k, mine).wait_send()

    return start, forward, finish


def _chip_send_plan(part_in, recv_out, send, recv):
    x, y, c = _place()
    p = 2 * x + y
    chips = [(_flip(x, dx), _flip(y, dy)) for dx, dy in CHIP_FLIPS]

    def copy(k):
        q = 2 * chips[k][0] + chips[k][1]
        return pltpu.make_async_remote_copy(src_ref=part_in.at[q], dst_ref=recv_out.at[p], send_sem=send.at[k],
                                            recv_sem=recv.at[k], device_id=(*chips[k], c), device_id_type=MESH)

    def start():
        for k in range(3):
            copy(k).start()

    def finish():
        for k in range(3):
            land = recv_out.at[2 * chips[k][0] + chips[k][1]]
            pltpu.make_async_remote_copy(src_ref=land, dst_ref=land, send_sem=send.at[k], recv_sem=recv.at[k],
                                         device_id=(*chips[k], c), device_id_type=MESH).wait_recv()
        for k in range(3):
            copy(k).wait_send()

    return start, finish


def _pair_send_plan(grad_in, recv_out, send, recv):
    x, y, c = _place()

    def copy():
        theirs = _half_rows(1 - c, grad_in.shape[1])
        return pltpu.make_async_remote_copy(src_ref=grad_in.at[:, theirs, :], dst_ref=recv_out, send_sem=send,
                                            recv_sem=recv, device_id=(x, y, 1 - c), device_id_type=MESH)

    return (lambda: copy().start()), (lambda: copy().wait())


def _own_slots(shard):
    here = 2 * lax.axis_index("x") + lax.axis_index("y")
    return lax.dynamic_update_slice(lax.empty((N_CHIPS,) + shard.shape, shard.dtype), shard[None], (here, 0, 0))


def _gather_weights(shards):
    n = len(shards)

    def body(*refs):
        ins, outs = refs[:n], refs[n:2 * n]
        send, recv = refs[2 * n:]
        plans = [_gather_plan(ins[a], outs[a], send.at[pl.ds(6 * a, 6)], recv.at[pl.ds(6 * a, 6)]) for a in range(n)]
        for stage in range(3):
            for plan in plans:
                plan[stage]()

    slots = [_own_slots(s) for s in shards]
    return pl.pallas_call(
        body, name="gather_weights", in_specs=[HBM] * n, out_specs=[HBM] * n,
        out_shape=[jax.ShapeDtypeStruct(s.shape, s.dtype) for s in slots],
        input_output_aliases={a: a for a in range(n)},
        scratch_shapes=[pltpu.SemaphoreType.DMA((6 * n,)), pltpu.SemaphoreType.DMA((6 * n,))],
    )(*slots)


def _pair_exchange(grads, small):
    n = len(grads)

    def body(*refs):
        gin, sm = refs[:n], refs[n]
        gout, sm_all = refs[n + 1:2 * n + 1], refs[2 * n + 1]
        send, recv = refs[2 * n + 2:]
        x, y, c = _place()
        me = 4 * x + 2 * y + c
        big = [_pair_send_plan(gin[a], gout[a], send.at[a], recv.at[a]) for a in range(n)]
        for start, _ in big:
            start()
        sm_all[pl.ds(me, 1)] = sm[...][None]
        tiny = []
        for k in range(1, N_DEV):
            px, py, pc = _flip(x, k & 4), _flip(y, k & 2), _flip(c, k & 1)
            tiny.append((pltpu.make_async_remote_copy(
                src_ref=sm, dst_ref=sm_all.at[me], send_sem=send.at[n + k - 1], recv_sem=recv.at[n + k - 1],
                device_id=(px, py, pc), device_id_type=MESH), 4 * px + 2 * py + pc))
            tiny[-1][0].start()
        for k, (cp, peer) in enumerate(tiny):
            pltpu.make_async_remote_copy(src_ref=sm, dst_ref=sm_all.at[peer], send_sem=send.at[n + k],
                                         recv_sem=recv.at[n + k], device_id=(x, y, c),
                                         device_id_type=MESH).wait_recv()
            cp.wait_send()
        for _, finish in big:
            finish()

    halves = [jax.ShapeDtypeStruct((g.shape[0], g.shape[1] // 2, g.shape[2]), g.dtype) for g in grads]
    return pl.pallas_call(
        body, name="pair_exchange", in_specs=[HBM] * n + [VMEM], out_specs=[HBM] * n + [VMEM],
        out_shape=halves + [jax.ShapeDtypeStruct((N_DEV,) + small.shape, small.dtype)],
        scratch_shapes=[pltpu.SemaphoreType.DMA((n + N_DEV - 1,)), pltpu.SemaphoreType.DMA((n + N_DEV - 1,))],
    )(*grads, small)


def _chip_exchange(parts):
    n = len(parts)

    def body(*refs):
        pin, pout = refs[:n], refs[n:2 * n]
        send, recv = refs[2 * n:]
        plans = [_chip_send_plan(pin[a], pout[a], send.at[pl.ds(3 * a, 3)], recv.at[pl.ds(3 * a, 3)]) for a in range(n)]
        for stage in range(2):
            for plan in plans:
                plan[stage]()

    return pl.pallas_call(
        body, name="chip_exchange", in_specs=[HBM] * n, out_specs=[HBM] * n,
        out_shape=[jax.ShapeDtypeStruct(s.shape, s.dtype) for s in parts],
        scratch_shapes=[pltpu.SemaphoreType.DMA((3 * n,)), pltpu.SemaphoreType.DMA((3 * n,))],
    )(*parts)


def _pair_swap(halves):
    n = len(halves)

    def body(*refs):
        hin, hout = refs[:n], refs[n:2 * n]
        send, recv = refs[2 * n:]
        x, y, c = _place()
        swaps = [pltpu.make_async_remote_copy(src_ref=hin[a], dst_ref=hout[a], send_sem=send.at[a],
                                              recv_sem=recv.at[a], device_id=(x, y, 1 - c), device_id_type=MESH)
                 for a in range(n)]
        for cp in swaps:
            cp.start()
        for cp in swaps:
            cp.wait()

    return pl.pallas_call(
        body, name="pair_swap", in_specs=[HBM] * n, out_specs=[HBM] * n,
        out_shape=[jax.ShapeDtypeStruct(s.shape, s.dtype) for s in halves],
        scratch_shapes=[pltpu.SemaphoreType.DMA((n,)), pltpu.SemaphoreType.DMA((n,))],
    )(*halves)


def _pair_sum(grad, recv, c, tag):
    _, R, C = grad.shape
    hr = R // 2

    def body(c_ref, a_ref, b_ref, o_ref):
        o_ref[...] = (a_ref[...] + b_ref[...]).astype(BF16)

    return pl.pallas_call(
        body, name="pair_sum_" + tag,
        grid_spec=pltpu.PrefetchScalarGridSpec(
            num_scalar_prefetch=1, grid=(N_CHIPS,),
            in_specs=[pl.BlockSpec((1, hr, C), lambda s, cr: (s, cr[0], 0)),
                      pl.BlockSpec((1, hr, C), lambda s, cr: (s, 0, 0))],
            out_specs=pl.BlockSpec((1, hr, C), lambda s, cr: (s, 0, 0))),
        out_shape=jax.ShapeDtypeStruct((N_CHIPS, hr, C), BF16),
    )(c, grad, recv)


def _chip_sum(received, own, chip, tag):
    _, rows, C = received.shape
    tr = rows // 2

    def body(chip_ref, own_ref, r1_ref, r2_ref, r3_ref, o_ref):
        p = [r[0].astype(F32) for r in (own_ref, r1_ref, r2_ref, r3_ref)]
        o_ref[...] = (p[0] + p[1]) + (p[2] + p[3])

    def slot(k):
        return pl.BlockSpec((1, tr, C), lambda i, cr: (jnp.bitwise_xor(cr[0], k), i, 0))

    return pl.pallas_call(
        body, name="chip_sum_" + tag,
        grid_spec=pltpu.PrefetchScalarGridSpec(
            num_scalar_prefetch=1, grid=(rows // tr,), in_specs=[slot(0), slot(1), slot(2), slot(3)],
            out_specs=pl.BlockSpec((tr, C), lambda i, cr: (i, 0))),
        out_shape=jax.ShapeDtypeStruct((rows, C), F32),
    )(chip, own, received, received, received)


def _adamw_math(w, g, m, v):
    m = ADAM_B1 * m + (1.0 - ADAM_B1) * g
    v = ADAM_B2 * v + (1.0 - ADAM_B2) * (g * g)
    m_hat = m / (1.0 - ADAM_B1 ** ADAM_STEP)
    v_hat = v / (1.0 - ADAM_B2 ** ADAM_STEP)
    delta = -ADAM_LR * (m_hat / (jnp.sqrt(v_hat) + ADAM_EPS) + ADAM_WD * w)
    return delta, m, v


def _adamw(w, g_mine, g_other, m, v, c, tag):
    R, C = w.shape
    tr = R // 4

    def body(c_ref, w_ref, gm_ref, go_ref, m_ref, v_ref, g_ref, d_ref, nm_ref, nv_ref):
        g = jnp.where(pl.program_id(0) == c_ref[0], gm_ref[...], go_ref[...])
        g_ref[...] = g
        d_ref[...], nm_ref[...], nv_ref[...] = _adamw_math(w_ref[...], g, m_ref[...], v_ref[...])

    blk = pl.BlockSpec((tr, C), lambda h, i, cr: (2 * h + i, 0))
    half = pl.BlockSpec((tr, C), lambda h, i, cr: (i, 0))
    return pl.pallas_call(
        body, name="adamw_" + tag,
        grid_spec=pltpu.PrefetchScalarGridSpec(
            num_scalar_prefetch=1, grid=(2, 2), in_specs=[blk, half, half, blk, blk], out_specs=[blk] * 4),
        out_shape=[jax.ShapeDtypeStruct((R, C), F32)] * 4,
    )(c, w, g_mine, g_other, m, v)


def _small_update(all_small, w, m, v):
    def body(a_ref, w_ref, m_ref, v_ref, g_ref, d_ref, nm_ref, nv_ref):
        g = ((a_ref[0] + a_ref[1]) + (a_ref[2] + a_ref[3])) + ((a_ref[4] + a_ref[5]) + (a_ref[6] + a_ref[7]))
        g_ref[...] = g
        d_ref[...], nm_ref[...], nv_ref[...] = _adamw_math(w_ref[...], g, m_ref[...], v_ref[...])

    return pl.pallas_call(
        body, name="small_update", out_shape=[jax.ShapeDtypeStruct(w.shape, F32)] * 4,
    )(all_small, w, m, v)


SMALL_ROWS = (("attn", 0, 0), ("ffn", 1, 0), ("dil", 2, 0), ("sb", 2, GROUP), ("q", 3, 0), ("k", 3, HEAD_DIM),
              ("loss", 4, 0))


def _pack_small(vals, D):
    rows = [jnp.zeros((1, D), F32) for _ in range(8)]
    for name, r, off in SMALL_ROWS:
        if name in vals:
            rows[r] = lax.dynamic_update_slice(rows[r], vals[name].astype(F32), (0, off))
    return jnp.concatenate(rows, axis=0)


def _unpack_small(packed, vals):
    return {name: packed[r:r + 1, off:off + vals[name].shape[1]] for name, r, off in SMALL_ROWS if name in vals}


def kernel(x, attn_norm_w, w_in, q_norm_w, k_norm_w, dil_out_norm_w, sb_out_norm_w, w_out, ffn_norm_w, w_gate, w_up, w_down, loss_target, m_attn_norm_w, m_w_in, m_q_norm_w, m_k_norm_w, m_dil_out_norm_w, m_sb_out_norm_w, m_w_out, m_ffn_norm_w, m_w_gate, m_w_up, m_w_down, v_attn_norm_w, v_w_in, v_q_norm_w, v_k_norm_w, v_dil_out_norm_w, v_sb_out_norm_w, v_w_out, v_ffn_norm_w, v_w_gate, v_w_up, v_w_down):
    D = x.shape[-1]
    big_names = ("w_in", "w_out", "w_gate", "w_up", "w_down")
    flipped = ("w_gate", "w_up")
    tr = lambda a: jnp.swapaxes(a[0], 0, 1)
    big_w = dict(w_in=w_in[0], w_out=w_out[0], w_gate=tr(w_gate), w_up=tr(w_up), w_down=w_down[0])
    big_m = dict(w_in=m_w_in[0], w_out=m_w_out[0], w_gate=tr(m_w_gate), w_up=tr(m_w_up), w_down=m_w_down[0])
    big_v = dict(w_in=v_w_in[0], w_out=v_w_out[0], w_gate=tr(v_w_gate), w_up=tr(v_w_up), w_down=v_w_down[0])
    small_w = dict(attn=attn_norm_w, q=q_norm_w, k=k_norm_w, dil=dil_out_norm_w, sb=sb_out_norm_w, ffn=ffn_norm_w)
    small_m = dict(attn=m_attn_norm_w, q=m_q_norm_w, k=m_k_norm_w, dil=m_dil_out_norm_w, sb=m_sb_out_norm_w,
                   ffn=m_ffn_norm_w)
    small_v = dict(attn=v_attn_norm_w, q=v_q_norm_w, k=v_k_norm_w, dil=v_dil_out_norm_w, sb=v_sb_out_norm_w,
                   ffn=v_ffn_norm_w)

    c = lax.axis_index("c").astype(jnp.int32).reshape(1)
    chip = (2 * lax.axis_index("x") + lax.axis_index("y")).astype(jnp.int32).reshape(1)
    (w_in_g,) = _gather_weights([big_w["w_in"].astype(BF16)])
    w_out_slots = _own_slots(big_w["w_out"].astype(BF16))
    ffn_slots = [_own_slots(big_w[n].astype(BF16)) for n in FFN_NAMES]

    loss_parts, grad_x, small_g, dw_in, dw_out, ffn_halves = _device_step(
        x[0], loss_target[0], attn_norm_w, q_norm_w, k_norm_w, dil_out_norm_w, sb_out_norm_w, ffn_norm_w,
        w_in_g, w_out_slots, ffn_slots, c, chip)
    small_g["loss"] = (jnp.sum(loss_parts[:, 0, 0]) * (0.5 / D)).reshape(1, 1)

    late = [dw_in, dw_out.reshape(N_CHIPS, -1, D)]
    *from_pair, all_small = _pair_exchange(late, _pack_small(small_g, D))
    chip_parts = [_pair_sum(g, r, c, n) for g, r, n in zip(late, from_pair, big_names)]
    from_chips = _chip_exchange(chip_parts)
    halves = [_chip_sum(r, p, chip, n) for r, p, n in zip(from_chips, chip_parts, big_names)] + ffn_halves
    others = _pair_swap(halves)
    big_out = {n: _adamw(big_w[n], mine, other, big_m[n], big_v[n], c, n)
               for n, mine, other in zip(big_names, halves, others)}
    sg, sd, sm, sv = _small_update(all_small, _pack_small(small_w, D), _pack_small(small_m, D),
                                   _pack_small(small_v, D))
    small_out = [_unpack_small(t, small_w) for t in (sg, sd, sm, sv)]

    order = (("attn", None), (None, "w_in"), ("q", None), ("k", None), ("dil", None), ("sb", None),
             (None, "w_out"), ("ffn", None), (None, "w_gate"), (None, "w_up"), (None, "w_down"))
    outs = [sg[4, 0], grad_x[None]]
    for kind in range(4):
        for s_name, b_name in order:
            if s_name is not None:
                outs.append(small_out[kind][s_name])
            else:
                res = big_out[b_name][kind]
                outs.append((jnp.swapaxes(res, 0, 1) if b_name in flipped else res)[None])
    return tuple(outs)
```

```python
import functools

import jax
import jax.numpy as jnp
from jax import lax
from jax.experimental import pallas as pl
from jax.experimental.pallas import tpu as pltpu

F32 = jnp.float32
BF16 = jnp.bfloat16
MESH = pl.DeviceIdType.MESH

HEAD_DIM = 64
GROUP = 512
BLOCK = 128
LANES = 128
N_CHIPS = 4
N_DEV = 8
EPS = 1e-6
ROPE_THETA = 10000.0
DILATIONS = (1, 4, 16)
NEG = -1e30

ADAM_LR = 0.001
ADAM_B1 = 0.9
ADAM_B2 = 0.999
ADAM_EPS = 1e-08
ADAM_WD = 0.01
ADAM_STEP = 10


def _dot(a, b):
    return jnp.dot(a, b, preferred_element_type=F32)


def _dot_nt(a, b):
    return lax.dot_general(a, b, (((1,), (1,)), ((), ())), preferred_element_type=F32)


def _dot_tn(a, b):
    return lax.dot_general(a, b, (((0,), (0,)), ((), ())), preferred_element_type=F32)


def _split(v):
    hi = lax.bitcast_convert_type(lax.bitcast_convert_type(v, jnp.uint32) & jnp.uint32(0xFFFF0000), F32)
    return hi.astype(BF16), (v - hi).astype(BF16)


def _segsum(v, g):
    hi, lo = _split(v)
    w = g.shape[1]
    return jnp.concatenate([_dot(jnp.concatenate([hi[:, c:c + w], lo[:, c:c + w]], axis=1), g)
                            for c in range(0, v.shape[1], w)], axis=1)


def _rot_half(x):
    outs = []
    for c in range(x.shape[1] // LANES):
        xc = x[:, c * LANES:(c + 1) * LANES]
        lane = lax.broadcasted_iota(jnp.int32, xc.shape, 1)
        first = (lane % HEAD_DIM) < (HEAD_DIM // 2)
        outs.append(jnp.where(first, pltpu.roll(xc, LANES - 32, 1), pltpu.roll(xc, 32, 1)))
    return outs[0] if len(outs) == 1 else jnp.concatenate(outs, axis=1)


def _rms(x):
    return lax.rsqrt(jnp.mean(x * x, axis=-1, keepdims=True) + EPS)


def _rms_bwd(dy, x, w):
    rstd = _rms(x)
    xh = x * rstd
    dxh = dy * w
    dx = rstd * (dxh - xh * jnp.mean(dxh * xh, axis=-1, keepdims=True))
    return dx, dy * xh


def _sigmoid(x):
    return 1.0 / (1.0 + jnp.exp(-x))


def _sum4(ref):
    p = [ref[j].astype(F32) for j in range(N_CHIPS)]
    return (p[0] + p[1]) + (p[2] + p[3])


def _full(shape):
    n = len(shape)
    return pl.BlockSpec(shape, lambda *_: (0,) * n)


def _strided_spec(tm, r):
    return pl.BlockSpec((r, tm // r, GROUP), lambda i: (0, i, 0))


def _strided_shape(S, r, dtype):
    return jax.ShapeDtypeStruct((r, S // r, GROUP), dtype)


def _to_strided(scr, val, outs):
    chunks = range(GROUP // LANES)
    for k in chunks:
        scr[k] = val[:, _lanes(k)]
    for r, o_ref in outs:
        if r == 1:
            o_ref[0] = val.astype(o_ref.dtype)
            continue
        n = val.shape[0] // r
        for c in range(r):
            rows = pl.ds(c, n, stride=r)
            o_ref[c] = jnp.concatenate([scr.at[k][rows, :] for k in chunks], axis=1).astype(o_ref.dtype)


def _from_strided(scr, ref):
    r, n, _ = ref.shape
    if r == 1:
        return ref[0].astype(F32)
    chunks = range(GROUP // LANES)
    for c in range(r):
        plane = ref[c].astype(F32)
        for k in chunks:
            scr.at[k][pl.ds(c, n, stride=r), :] = plane[:, _lanes(k)]
    return jnp.concatenate([scr[k] for k in chunks], axis=1)


def _strided_scratch(tm):
    return pltpu.VMEM((GROUP // LANES, tm, LANES), F32)


def _tile4(t):
    return jnp.concatenate([t] * (GROUP // LANES), axis=1)


def _in_proj_fwd(x, attn_w, w_in_g, qw, kw, cos_t, sin_t, seg_ones, slots):
    S, D = x.shape
    tm = 512
    wcols = w_in_g.shape[2]
    nd = len(DILATIONS)
    ns = len(slots)

    def body(x_ref, aw_ref, w_ref, qw_ref, kw_ref, cos_ref, sin_ref, g_ref, *rest):
        slot_in, (h_ref, qa_ref, ka_ref), rest = rest[:ns], rest[ns:ns + 3], rest[ns + 3:]
        q_refs, k_refs, v_refs = rest[:nd], rest[nd:2 * nd], rest[2 * nd:3 * nd]
        qs_ref, ks_ref, vs_ref = rest[3 * nd:3 * nd + 3]
        slot_out, scr, sems = rest[3 * nd + 3:3 * nd + 3 + ns], rest[3 * nd + 3 + ns], rest[3 * nd + 4 + ns:]
        finish = _hosted_gathers(slot_in, slot_out, *sems, pl.program_id(0), S // tm) if ns else None
        xv = x_ref[...]
        h = (xv * _rms(xv) * aw_ref[...]).astype(BF16)
        h_ref[...] = h
        proj = jnp.concatenate([_dot(h, w_ref[j]) for j in range(N_CHIPS)], axis=1)
        qa = proj[:, 0 * GROUP:1 * GROUP]
        ka = proj[:, 1 * GROUP:2 * GROUP]
        qa_ref[...] = qa
        ka_ref[...] = ka
        _to_strided(scr, proj[:, 2 * GROUP:3 * GROUP], list(zip(DILATIONS, v_refs)))
        qs_ref[...] = proj[:, 3 * GROUP:4 * GROUP].astype(BF16)
        ks_ref[...] = proj[:, 4 * GROUP:5 * GROUP].astype(BF16)
        vs_ref[...] = proj[:, 5 * GROUP:6 * GROUP].astype(BF16)
        g = g_ref[...]
        cos = _tile4(cos_ref[...])
        sin = _tile4(sin_ref[...])
        for t, w_r, o_rs in ((qa, qw_ref, q_refs), (ka, kw_ref, k_refs)):
            rstd = lax.rsqrt(_segsum(t * t, g) * (1.0 / HEAD_DIM) + EPS)
            tn = t * rstd * w_r[...]
            _to_strided(scr, tn * cos + _rot_half(tn) * sin, list(zip(DILATIONS, o_rs)))
        if finish is not None:
            finish()

    row = lambda i: (i, 0)
    tile = lambda n, dt: jax.ShapeDtypeStruct((S, n), dt)
    planes = [_strided_spec(tm, r) for r in DILATIONS]
    h_in, h_out, h_shape, h_sems = _hosted_specs(slots)
    n_out = 6 + 3 * nd
    return pl.pallas_call(
        body, name="in_proj_fwd", grid=(S // tm,),
        in_specs=[pl.BlockSpec((tm, D), row), _full((1, D)), _full((N_CHIPS, D, wcols)),
                  _full((1, GROUP)), _full((1, GROUP)),
                  pl.BlockSpec((tm, LANES), row), pl.BlockSpec((tm, LANES), row),
                  _full((GROUP, GROUP // 2))] + h_in,
        out_specs=[pl.BlockSpec((tm, D), row)] + [pl.BlockSpec((tm, GROUP), row)] * 2 + planes * 3
                  + [pl.BlockSpec((tm, GROUP), row)] * 3 + h_out,
        out_shape=[tile(D, BF16), tile(GROUP, F32), tile(GROUP, F32)]
                  + [_strided_shape(S, r, BF16) for r in DILATIONS] * 3 + [tile(GROUP, BF16)] * 3 + h_shape,
        input_output_aliases={8 + a: n_out + a for a in range(ns)},
        scratch_shapes=[_strided_scratch(tm)] + h_sems,
    )(x, attn_w, w_in_g, qw, kw, cos_t, sin_t, seg_ones, *slots)


DIL_PLANES = 1


def _dil_fwd(q, k, v, slots):
    r, L, _ = q.shape
    nb = L // BLOCK
    P = GROUP // LANES
    PL = min(DIL_PLANES, r)
    ns = len(slots)
    units = [(pp, hp) for pp in range(PL) for hp in range(P)]

    def body(q_ref, kc_ref, kp_ref, vc_ref, vp_ref, *rest):
        o_ref, l_ref = rest[ns:ns + 2]
        n = pl.program_id(1)
        finish = (_hosted_gathers(rest[:ns], rest[ns + 2:2 * ns + 2], *rest[2 * ns + 2:],
                                  pl.program_id(0) * nb + n, (r // PL) * nb) if ns else None)
        rowi = lax.broadcasted_iota(jnp.int32, (BLOCK, BLOCK), 0)
        coli = lax.broadcasted_iota(jnp.int32, (BLOCK, BLOCK), 1)
        first = coli < HEAD_DIM
        masks = (coli <= rowi, jnp.logical_and(coli >= rowi, n > 0))
        s2 = {}
        for pp, hp in units:
            q2 = _scaled(q_ref[pp, :, _lanes(hp)])
            for b, k_ref in enumerate((kc_ref, kp_ref)):
                s2[pp, hp, b] = _dot_nt(q2, _by_head(k_ref[pp, :, _lanes(hp)], first))
        ps, inv, lse = {}, {}, {}
        for pp, hp in units:
            for h in range(2):
                s = [jnp.where(masks[b], s2[pp, hp, b][:, h * BLOCK:(h + 1) * BLOCK], NEG) for b in range(2)]
                m = jnp.maximum(jnp.max(s[0], axis=1, keepdims=True), jnp.max(s[1], axis=1, keepdims=True))
                p = [jnp.exp(s[b] - m) for b in range(2)]
                den = jnp.sum(p[0], axis=1, keepdims=True) + jnp.sum(p[1], axis=1, keepdims=True)
                ps[pp, hp, h] = [p[b].astype(BF16) for b in range(2)]
                inv[pp, hp, h] = 1.0 / den
                lse[pp, hp, h] = m + jnp.log(den)
        for pp, hp in units:
            o = jnp.zeros((BLOCK, LANES), F32)
            for b, v_ref in enumerate((vc_ref, vp_ref)):
                o = o + _dot(jnp.concatenate([ps[pp, hp, 0][b], ps[pp, hp, 1][b]], axis=1),
                             _by_head(v_ref[pp, :, _lanes(hp)], first))
            o_ref[pp, :, _lanes(hp)] = o * jnp.where(first, inv[pp, hp, 0], inv[pp, hp, 1])
            l_ref[pp, :, _lanes(hp)] = jnp.where(first, lse[pp, hp, 0], lse[pp, hp, 1])
        if finish is not None:
            finish()

    cur = pl.BlockSpec((PL, BLOCK, GROUP), lambda c, n: (c, n, 0))
    prev = pl.BlockSpec((PL, BLOCK, GROUP), lambda c, n: (c, jnp.maximum(n - 1, 0), 0))
    h_in, h_out, h_shape, h_sems = _hosted_specs(slots)
    return pl.pallas_call(
        body, name="dil_fwd_r%d" % r, grid=(r // PL, nb),
        in_specs=[cur, cur, prev, cur, prev] + h_in, out_specs=[cur, cur] + h_out,
        out_shape=[jax.ShapeDtypeStruct(q.shape, F32)] * 2 + h_shape,
        input_output_aliases={5 + a: 2 + a for a in range(ns)},
        scratch_shapes=h_sems,
    )(q, k, k, v, v, *slots)


def _dil_bwd(q, k, v, do, lse, delta, part):
    r, L, _ = q.shape
    nb = L // BLOCK
    P = GROUP // LANES
    PL = min(DIL_PLANES, r)
    scale = HEAD_DIM ** -0.5
    units = [(pp, hp) for pp in range(PL) for hp in range(P)]

    def body(qc_ref, qn_ref, doc_ref, don_ref, lc_ref, ln_ref, dc_ref, dn_ref, k_ref, v_ref, part_in,
             dq_ref, dk_ref, dv_ref, part_out, carry, send, recv):
        j = pl.program_id(1)
        step = pl.program_id(0) * nb + j
        start, finish = _chip_send_plan(part_in, part_out, send, recv)
        pl.when(step == 0)(start)
        rowi = lax.broadcasted_iota(jnp.int32, (BLOCK, BLOCK), 0)
        coli = lax.broadcasted_iota(jnp.int32, (BLOCK, BLOCK), 1)
        first = coli < HEAD_DIM
        sides = ((qc_ref, doc_ref, lc_ref, dc_ref, coli <= rowi),
                 (qn_ref, don_ref, ln_ref, dn_ref, jnp.logical_and(coli >= rowi, j < nb - 1)))

        @pl.when(j == 0)
        def _():
            carry[...] = jnp.zeros_like(carry)

        kcat, q2, do2, s2, dp2 = {}, {}, {}, {}, {}
        for pp, hp in units:
            kcat[pp, hp] = _by_head(k_ref[pp, :, _lanes(hp)], first)
            vcat = _by_head(v_ref[pp, :, _lanes(hp)], first)
            for x, (q_r, do_r, _, _, _) in enumerate(sides):
                q2[pp, hp, x] = _scaled(q_r[pp, :, _lanes(hp)])
                do2[pp, hp, x] = do_r[pp, :, _lanes(hp)]
                s2[pp, hp, x] = _dot_nt(q2[pp, hp, x], kcat[pp, hp])
                dp2[pp, hp, x] = _dot_nt(do2[pp, hp, x], vcat)
        pcat, dscat = {}, {}
        for pp, hp in units:
            for x, (_, _, l_r, d_r, msk) in enumerate(sides):
                ps, dss = [], []
                for h in range(2):
                    col = hp * LANES + h * HEAD_DIM
                    half = slice(h * BLOCK, (h + 1) * BLOCK)
                    p = jnp.where(msk, jnp.exp(s2[pp, hp, x][:, half] - l_r[pp, :, col:col + 1]), 0.0)
                    ps.append(p.astype(BF16))
                    dss.append((p * (dp2[pp, hp, x][:, half] - d_r[pp, :, col:col + 1])).astype(BF16))
                pcat[pp, hp, x] = jnp.concatenate(ps, axis=1)
                dscat[pp, hp, x] = jnp.concatenate(dss, axis=1)
        for pp, hp in units:
            dv2 = _dot_tn(pcat[pp, hp, 0], do2[pp, hp, 0]) + _dot_tn(pcat[pp, hp, 1], do2[pp, hp, 1])
            dk2 = _dot_tn(dscat[pp, hp, 0], q2[pp, hp, 0]) + _dot_tn(dscat[pp, hp, 1], q2[pp, hp, 1])
            dv_ref[pp, :, _lanes(hp)] = jnp.where(first, dv2[:BLOCK], dv2[BLOCK:]).astype(BF16)
            dk_ref[pp, :, _lanes(hp)] = jnp.where(first, dk2[:BLOCK], dk2[BLOCK:]).astype(BF16)
            dq_ref[pp, :, _lanes(hp)] = (carry[pp, :, _lanes(hp)]
                                         + _dot(dscat[pp, hp, 0], kcat[pp, hp]) * scale).astype(BF16)
            carry[pp, :, _lanes(hp)] = _dot(dscat[pp, hp, 1], kcat[pp, hp]) * scale
        pl.when(step == (r // PL) * nb - 1)(finish)

    cur = pl.BlockSpec((PL, BLOCK, GROUP), lambda c, n: (c, n, 0))
    nxt = pl.BlockSpec((PL, BLOCK, GROUP), lambda c, n: (c, jnp.minimum(n + 1, nb - 1), 0))
    return pl.pallas_call(
        body, name="dil_bwd_r%d" % r, grid=(r // PL, nb),
        in_specs=[cur, nxt, cur, nxt, cur, nxt, cur, nxt, cur, cur, HBM], out_specs=[cur, cur, cur, HBM],
        out_shape=[jax.ShapeDtypeStruct(q.shape, BF16)] * 3 + [jax.ShapeDtypeStruct(part.shape, part.dtype)],
        scratch_shapes=[pltpu.VMEM((PL, BLOCK, GROUP), F32), pltpu.SemaphoreType.DMA((3,)),
                        pltpu.SemaphoreType.DMA((3,))],
    )(q, q, do, do, lse, lse, delta, delta, k, v, part)


SB_TILES = 2
SB_PAIRS_FWD = 4
SB_PAIRS_BWD = 4
SB_DEAD = -110.0


def _lanes(hp):
    return slice(hp * LANES, (hp + 1) * LANES)


def _sb_logits(z, valid):
    e = jnp.exp(-jnp.abs(z))
    lb = jnp.minimum(z, 0.0) - jnp.log(1.0 + e)
    lk = lb - z
    if valid is not None:
        lk = jnp.where(valid, lk, 0.0)
    return e, lb, lk


def _by_head(t, first):
    zero = jnp.zeros_like(t)
    return jnp.concatenate([jnp.where(first, t, zero), jnp.where(first, zero, t)], axis=0)


def _sb_valid(i, j):
    rowi = lax.broadcasted_iota(jnp.int32, (BLOCK, BLOCK), 0)
    coli = lax.broadcasted_iota(jnp.int32, (BLOCK, BLOCK), 1)
    return (coli - rowi) < (i - j) * BLOCK


def _scaled(q):
    return (q.astype(F32) * (HEAD_DIM ** -0.5)).astype(BF16)


def _hosted_gathers(refs_in, refs_out, send, recv, step, steps):
    plans = [_gather_plan(refs_in[a], refs_out[a], send.at[pl.ds(6 * a, 6)], recv.at[pl.ds(6 * a, 6)])
             for a in range(len(refs_in))]
    for stage, at in ((0, 0), (1, (2 * steps) // 3)):
        @pl.when(step == at)
        def _():
            for plan in plans:
                plan[stage]()

    def finish():
        @pl.when(step == steps - 1)
        def _():
            for plan in plans:
                plan[2]()

    return finish


def _hosted_specs(slots):
    n = len(slots)
    sems = [pltpu.SemaphoreType.DMA((6 * n,))] * 2 if n else []
    return [HBM] * n, [HBM] * n, [jax.ShapeDtypeStruct(s.shape, s.dtype) for s in slots], sems


def _sb_fwd(qs, ks, vs, tri_later, slots):
    S = qs.shape[0]
    P = SB_PAIRS_FWD
    W = P * LANES
    ns = len(slots)

    def body(q_ref, k_ref, v_ref, u_ref, *rest):
        o_ref, lt_ref, from_ref = rest[ns:ns + 3]
        i = pl.program_id(1)
        finish = _hosted_gathers(rest[:ns], rest[ns + 3:2 * ns + 3], *rest[2 * ns + 3:], i, S // BLOCK) if ns else None
        first = lax.broadcasted_iota(jnp.int32, (BLOCK, LANES), 1) < HEAD_DIM
        q2 = [_scaled(q_ref[:, _lanes(hp)]) for hp in range(P)]

        def walk(tiles, carry):
            runs, accs = list(carry[0]), list(carry[1])
            units = [(t, hp) for t in range(len(tiles)) for hp in range(P)]
            offs = [pl.multiple_of(j * BLOCK, BLOCK) for j, _ in tiles]
            valids = [_sb_valid(i, j) if diag else None for j, diag in tiles]
            z2s, lbs, c2s = {}, {}, {}
            for t, hp in units:
                z2s[t, hp] = _dot_nt(q2[hp], _by_head(k_ref[pl.ds(offs[t], BLOCK), _lanes(hp)], first))
            for t, hp in units:
                for h in range(2):
                    _, lb, lk = _sb_logits(z2s[t, hp][:, h * BLOCK:(h + 1) * BLOCK], valids[t])
                    lbs[t, hp, h] = lb
                    c2s[t, hp, h] = _dot(jnp.concatenate(_split(lk), axis=1), u_ref[...])
            for t, hp in units:
                a2 = []
                for h in range(2):
                    a = jnp.exp(lbs[t, hp, h] + c2s[t, hp, h][:, :BLOCK] + runs[2 * hp + h])
                    if valids[t] is not None:
                        a = jnp.where(valids[t], a, 0.0)
                    a2.append(a.astype(BF16))
                    runs[2 * hp + h] = runs[2 * hp + h] + c2s[t, hp, h][:, BLOCK:]
                vcat = _by_head(v_ref[pl.ds(offs[t], BLOCK), _lanes(hp)], first)
                accs[hp] = accs[hp] + _dot(jnp.concatenate(a2, axis=1), vcat)
            return tuple(runs), tuple(accs)

        def chunk(ci, carry):
            return walk([(ci * SB_TILES + t, False) for t in reversed(range(SB_TILES))], carry)

        def alive(runs):
            top = functools.reduce(jnp.maximum, runs)
            return (jnp.max(top) > SB_DEAD).astype(jnp.int32)

        def step(c):
            t, _, runs, accs = c
            runs, accs = chunk(nfull - 1 - t, (runs, accs))
            return t + 1, alive(runs), runs, accs

        zero = jnp.zeros((BLOCK, LANES), F32)
        nfull = i // SB_TILES
        ragged = [functools.partial(walk, [(i, True)] + [(i - 1 - m, False) for m in range(extra)])
                  for extra in range(SB_TILES)]
        runs, accs = lax.switch(i % SB_TILES, ragged, ((zero,) * (2 * P), (zero,) * P))
        done, _, runs, accs = lax.while_loop(lambda c: jnp.logical_and(c[0] < nfull, c[1] > 0), step,
                                             (jnp.int32(0), alive(runs), runs, accs))
        for hp in range(P):
            o_ref[:, _lanes(hp)] = accs[hp]
            lt_ref[:, _lanes(hp)] = jnp.where(first, runs[2 * hp], runs[2 * hp + 1])
        from_ref[...] = jnp.full(from_ref.shape, nfull - done, jnp.int32)
        if finish is not None:
            finish()

    assert W == GROUP
    blk = pl.BlockSpec((BLOCK, W), lambda hp, i: (i, hp))
    col = pl.BlockSpec((S, W), lambda hp, i: (0, hp))
    h_in, h_out, h_shape, h_sems = _hosted_specs(slots)
    return pl.pallas_call(
        body, name="sb_fwd", grid=(GROUP // W, S // BLOCK),
        in_specs=[blk, col, col, _full((2 * BLOCK, 2 * BLOCK))] + h_in,
        out_specs=[blk, blk, pl.BlockSpec((1, 8, LANES), lambda hp, i: (i, 0, 0))] + h_out,
        out_shape=[jax.ShapeDtypeStruct((S, GROUP), F32)] * 2
                  + [jax.ShapeDtypeStruct((S // BLOCK, 8, LANES), jnp.int32)] + h_shape,
        input_output_aliases={4 + a: 3 + a for a in range(ns)},
        scratch_shapes=h_sems,
    )(qs, ks, vs, tri_later, *slots)


def _sb_bwd(first_chunk, qs, ks, vs, do, ltot, tri_upto, tri_before):
    S = qs.shape[0]
    P = SB_PAIRS_BWD
    W = P * LANES

    def body(from_ref, q_ref, k_ref, v_ref, do_ref, lt_ref, w_ref, x_ref, dq_ref, dk_ref, dv_ref):
        i = pl.program_id(1)

        @pl.when(i == 0)
        def _():
            dk_ref[...] = jnp.zeros_like(dk_ref)
            dv_ref[...] = jnp.zeros_like(dv_ref)

        first = lax.broadcasted_iota(jnp.int32, (BLOCK, LANES), 1) < HEAD_DIM
        q2 = [_scaled(q_ref[:, _lanes(hp)]) for hp in range(P)]
        do2 = [do_ref[:, _lanes(hp)] for hp in range(P)]
        totals = [jnp.broadcast_to(lt_ref[:, n * HEAD_DIM:n * HEAD_DIM + 1], (BLOCK, LANES)) for n in range(2 * P)]

        def walk(tiles, carry):
            keeps, grads, dqs = list(carry[0]), list(carry[1]), list(carry[2])
            units = [(t, hp) for t in range(len(tiles)) for hp in range(P)]
            offs = [pl.multiple_of(j * BLOCK, BLOCK) for j, _ in tiles]
            valids = [_sb_valid(i, j) if diag else None for j, diag in tiles]
            kcat, z2, da2, es, lbs, c2s, as_, des, p2s = {}, {}, {}, {}, {}, {}, {}, {}, {}
            for t, hp in units:
                kcat[t, hp] = _by_head(k_ref[pl.ds(offs[t], BLOCK), _lanes(hp)], first)
                z2[t, hp] = _dot_nt(q2[hp], kcat[t, hp])
                da2[t, hp] = _dot_nt(do2[hp], _by_head(v_ref[pl.ds(offs[t], BLOCK), _lanes(hp)], first))
            for t, hp in units:
                for h in range(2):
                    es[t, hp, h], lbs[t, hp, h], lk = _sb_logits(z2[t, hp][:, h * BLOCK:(h + 1) * BLOCK], valids[t])
                    c2s[t, hp, h] = _dot(jnp.concatenate(_split(lk), axis=1), w_ref[...])
            for t, hp in units:
                for h in range(2):
                    n = 2 * hp + h
                    a = jnp.exp(lbs[t, hp, h] + (totals[n] - (keeps[n] + c2s[t, hp, h][:, :BLOCK])))
                    if valids[t] is not None:
                        a = jnp.where(valids[t], a, 0.0)
                    keeps[n] = keeps[n] + c2s[t, hp, h][:, BLOCK:]
                    de = a * da2[t, hp][:, h * BLOCK:(h + 1) * BLOCK]
                    as_[t, hp, h], des[t, hp, h] = a.astype(BF16), de
                    p2s[t, hp, h] = _dot(jnp.concatenate(_split(de), axis=1), x_ref[...])
            for t, hp in units:
                dz2 = []
                for h in range(2):
                    n = 2 * hp + h
                    e = es[t, hp, h]
                    sig = jnp.where(z2[t, hp][:, h * BLOCK:(h + 1) * BLOCK] >= 0.0, 1.0, e) / (1.0 + e)
                    dz = des[t, hp, h] * (1.0 - sig) - (grads[n] + p2s[t, hp, h][:, :BLOCK]) * sig
                    if valids[t] is not None:
                        dz = jnp.where(valids[t], dz, 0.0)
                    grads[n] = grads[n] + p2s[t, hp, h][:, BLOCK:]
                    dz2.append(dz.astype(BF16))
                dzcat = jnp.concatenate(dz2, axis=1)
                dk2 = _dot_tn(dzcat, q2[hp])
                dv2 = _dot_tn(jnp.concatenate([as_[t, hp, 0], as_[t, hp, 1]], axis=1), do2[hp])
                dk_ref[pl.ds(offs[t], BLOCK), _lanes(hp)] += jnp.where(first, dk2[:BLOCK], dk2[BLOCK:])
                dv_ref[pl.ds(offs[t], BLOCK), _lanes(hp)] += jnp.where(first, dv2[:BLOCK], dv2[BLOCK:])
                dqs[hp] = dqs[hp] + _dot(dzcat, kcat[t, hp])
            return tuple(keeps), tuple(grads), tuple(dqs)

        zero = jnp.zeros((BLOCK, LANES), F32)
        nfull = i // SB_TILES
        carry = lax.fori_loop(
            from_ref[i], nfull, lambda ci, c: walk([(ci * SB_TILES + t, False) for t in range(SB_TILES)], c),
            ((zero,) * (2 * P), (zero,) * (2 * P), (zero,) * P))
        ragged = [functools.partial(walk, [(i - m, False) for m in range(extra, 0, -1)] + [(i, True)])
                  for extra in range(SB_TILES)]
        carry = lax.switch(i % SB_TILES, ragged, carry)
        for hp in range(P):
            dq_ref[:, _lanes(hp)] = carry[2][hp] * (HEAD_DIM ** -0.5)

    blk = pl.BlockSpec((BLOCK, W), lambda hp, i, fr: (i, hp))
    col = pl.BlockSpec((S, W), lambda hp, i, fr: (0, hp))
    tri = pl.BlockSpec((2 * BLOCK, 2 * BLOCK), lambda hp, i, fr: (0, 0))
    return pl.pallas_call(
        body, name="sb_bwd",
        grid_spec=pltpu.PrefetchScalarGridSpec(
            num_scalar_prefetch=1, grid=(GROUP // W, S // BLOCK),
            in_specs=[blk, col, col, blk, blk, tri, tri], out_specs=[blk, col, col]),
        out_shape=[jax.ShapeDtypeStruct((S, GROUP), F32)] * 3,
    )(first_chunk, qs, ks, vs, do, ltot, tri_upto, tri_before)


def _out_proj_fwd(o_br, l_br, o_sb, x, w_dil, w_sbn, w_out_g):
    S, D = x.shape
    tm = 512

    def body(o0, o1, o2, l0, l1, l2, os_ref, x_ref, wd_ref, ws_ref, w_ref, od_ref, s0, s1, s2, x1_ref, scr):
        ls = [_from_strided(scr, l) for l in (l0, l1, l2)]
        os_ = [_from_strided(scr, o) for o in (o0, o1, o2)]
        m = jnp.maximum(jnp.maximum(ls[0], ls[1]), ls[2])
        es = [jnp.exp(l - m) for l in ls]
        den = es[0] + es[1] + es[2]
        od = (es[0] * os_[0] + es[1] * os_[1] + es[2] * os_[2]) / den
        od_ref[...] = od
        _to_strided(scr, m + jnp.log(den), list(zip(DILATIONS, (s0, s1, s2))))
        osb = os_ref[...]
        mixed = jnp.concatenate([(od * _rms(od) * wd_ref[...]).astype(BF16),
                                 (osb * _rms(osb) * ws_ref[...]).astype(BF16)], axis=1)
        x1_ref[...] = x_ref[...] + _dot(mixed, w_ref[...])

    row = lambda i: (i, 0)
    g = pl.BlockSpec((tm, GROUP), row)
    d = pl.BlockSpec((tm, D), row)
    planes = [_strided_spec(tm, r) for r in DILATIONS]
    return pl.pallas_call(
        body, name="out_proj_fwd", grid=(S // tm,),
        in_specs=planes * 2 + [g, d, _full((1, GROUP)), _full((1, GROUP)), _full((2 * GROUP, D))],
        out_specs=[g] + planes + [d],
        out_shape=[jax.ShapeDtypeStruct((S, GROUP), F32)] + [_strided_shape(S, r, F32) for r in DILATIONS]
                  + [jax.ShapeDtypeStruct((S, D), F32)],
        scratch_shapes=[_strided_scratch(tm)],
    )(*o_br, *l_br, o_sb, x, w_dil, w_sbn, w_out_g)


def _ffn_fwd(x1, target, ffn_w, wg_g, wu_g, wd_g):
    S, D = x1.shape
    F = wg_g.shape[1]
    tm = 512
    nt = S // tm

    def body(x_ref, t_ref, nw_ref, wg_ref, wu_ref, wd_ref, h_ref, g_ref, u_ref, dy_ref, loss_ref, h_s, acc):
        j = pl.program_id(1)

        @pl.when(j == 0)
        def _():
            xv = x_ref[...]
            h = (xv * _rms(xv) * nw_ref[...]).astype(BF16)
            h_s[...] = h
            h_ref[...] = h
            acc[...] = xv

        h = h_s[...]
        g = _dot_nt(h, wg_ref[0])
        u = _dot_nt(h, wu_ref[0])
        g_ref[0] = g.astype(BF16)
        u_ref[0] = u.astype(BF16)
        a = (g * _sigmoid(g) * u).astype(BF16)
        acc[...] += _dot(a, wd_ref[0])

        @pl.when(j == N_CHIPS - 1)
        def _():
            err = acc[...] - t_ref[...]
            dy_ref[...] = err * (1.0 / D)
            loss_ref[...] = jnp.full(loss_ref.shape, jnp.sum(err * err), F32)

    row = lambda t, j: (t, 0)
    shard = lambda t, j: (j, 0, 0)
    act = lambda t, j: (j, t, 0)
    return pl.pallas_call(
        body, name="ffn_fwd", grid=(nt, N_CHIPS),
        in_specs=[pl.BlockSpec((tm, D), row), pl.BlockSpec((tm, D), row), pl.BlockSpec((1, D), lambda t, j: (0, 0))]
                 + [pl.BlockSpec((1, F, D), shard)] * 3,
        out_specs=[pl.BlockSpec((tm, D), row), pl.BlockSpec((1, tm, F), act), pl.BlockSpec((1, tm, F), act),
                   pl.BlockSpec((tm, D), row), pl.BlockSpec((1, 8, LANES), lambda t, j: (t, 0, 0))],
        out_shape=[jax.ShapeDtypeStruct((S, D), BF16), jax.ShapeDtypeStruct((N_CHIPS, S, F), BF16),
                   jax.ShapeDtypeStruct((N_CHIPS, S, F), BF16), jax.ShapeDtypeStruct((S, D), F32),
                   jax.ShapeDtypeStruct((nt, 8, LANES), F32)],
        scratch_shapes=[pltpu.VMEM((tm, D), BF16), pltpu.VMEM((tm, D), F32)],
    )(x1, target, ffn_w, wg_g, wu_g, wd_g)


def _ffn_bwd(h2, dy, g, u, wg_g, wu_g, wd_g):
    S, D = dy.shape
    F = wg_g.shape[1]
    tm = 512

    def body(h_ref, dy_ref, g_ref, u_ref, wg_ref, wu_ref, wd_ref, dwg_ref, dwu_ref, dwd_ref, dh_ref):
        t = pl.program_id(1)

        @pl.when(t == 0)
        def _():
            dwg_ref[...] = jnp.zeros_like(dwg_ref)
            dwu_ref[...] = jnp.zeros_like(dwu_ref)
            dwd_ref[...] = jnp.zeros_like(dwd_ref)

        h = h_ref[...]
        dyb = dy_ref[...].astype(BF16)
        gv = g_ref[0].astype(F32)
        uv = u_ref[0].astype(F32)
        da = _dot_nt(dyb, wd_ref[0])
        sg = _sigmoid(gv)
        silu = gv * sg
        du = (da * silu).astype(BF16)
        dg = (da * uv * (sg * (1.0 + gv * (1.0 - sg)))).astype(BF16)
        dwd_ref[0] += _dot_tn((silu * uv).astype(BF16), dyb)
        dwg_ref[0] += _dot_tn(dg, h)
        dwu_ref[0] += _dot_tn(du, h)
        dh_ref[0] = (_dot(dg, wg_ref[0]) + _dot(du, wu_ref[0])).astype(BF16)

    row = lambda j, t: (t, 0)
    shard = lambda j, t: (j, 0, 0)
    act = lambda j, t: (j, t, 0)
    return pl.pallas_call(
        body, name="ffn_bwd", grid=(N_CHIPS, S // tm),
        in_specs=[pl.BlockSpec((tm, D), row), pl.BlockSpec((tm, D), row),
                  pl.BlockSpec((1, tm, F), act), pl.BlockSpec((1, tm, F), act)] + [pl.BlockSpec((1, F, D), shard)] * 3,
        out_specs=[pl.BlockSpec((1, F, D), shard)] * 3 + [pl.BlockSpec((1, tm, D), act)],
        out_shape=[jax.ShapeDtypeStruct((N_CHIPS, F, D), F32)] * 3 + [jax.ShapeDtypeStruct((N_CHIPS, S, D), BF16)],
    )(h2, dy, g, u, wg_g, wu_g, wd_g)


def _out_proj_bwd(dh2p, dy, x1, ffn_w, w_out_g, o_dil, o_sb, w_dil, w_sbn, seg_ones, ffn_grads):
    S, D = dy.shape
    tm = 256
    ng = len(ffn_grads)

    def body(dh_ref, dy_ref, x1_ref, nw_ref, w_ref, od_ref, os_ref, wd_ref, ws_ref, g_ref, *rest):
        gin, rest = rest[:ng], rest[ng:]
        dx1_ref, dod0, dod1, dod2, dos_ref, dl0, dl1, dl2, dw_ref, dnw_ref, dwd_ref, dws_ref = rest[:12]
        gout, (scr, send, recv) = rest[12:12 + ng], rest[12 + ng:]
        i = pl.program_id(0)
        plans = [_pair_send_plan(gin[a], gout[a], send.at[a], recv.at[a]) for a in range(ng)]

        @pl.when(i == 0)
        def _():
            for start, _ in plans:
                start()

        @pl.when(i == 0)
        def _():
            for r_ in (dw_ref, dnw_ref, dwd_ref, dws_ref):
                r_[...] = jnp.zeros_like(r_)

        dh2 = _sum4(dh_ref)
        dxn, dwn = _rms_bwd(dh2, x1_ref[...], nw_ref[...])
        dnw_ref[...] += jnp.sum(dwn, axis=0, keepdims=True)
        dx1 = dy_ref[...] + dxn
        dx1_ref[...] = dx1
        dx1b = dx1.astype(BF16)
        dmix = _dot_nt(dx1b, w_ref[...])
        od = od_ref[...]
        osb = os_ref[...]
        mixed = jnp.concatenate([(od * _rms(od) * wd_ref[...]).astype(BF16),
                                 (osb * _rms(osb) * ws_ref[...]).astype(BF16)], axis=1)
        dw_ref[...] += _dot_tn(mixed, dx1b)
        do, dwo = _rms_bwd(dmix[:, :GROUP], od, wd_ref[...])
        dwd_ref[...] += jnp.sum(dwo, axis=0, keepdims=True)
        _to_strided(scr, do, list(zip(DILATIONS, (dod0, dod1, dod2))))
        _to_strided(scr, _segsum(do * od, g_ref[...]), list(zip(DILATIONS, (dl0, dl1, dl2))))
        do, dwo = _rms_bwd(dmix[:, GROUP:], osb, ws_ref[...])
        dws_ref[...] += jnp.sum(dwo, axis=0, keepdims=True)
        dos_ref[...] = do.astype(BF16)

        @pl.when(i == S // tm - 1)
        def _():
            for _, finish in plans:
                finish()

    row = lambda i: (i, 0)
    gsp = pl.BlockSpec((tm, GROUP), row)
    dsp = pl.BlockSpec((tm, D), row)
    planes = [_strided_spec(tm, r) for r in DILATIONS]
    halves = [jax.ShapeDtypeStruct((g.shape[0], g.shape[1] // 2, g.shape[2]), g.dtype) for g in ffn_grads]
    return pl.pallas_call(
        body, name="out_proj_bwd", grid=(S // tm,),
        in_specs=[pl.BlockSpec((N_CHIPS, tm, D), lambda i: (0, i, 0)), dsp, dsp, _full((1, D)), _full((2 * GROUP, D)),
                  gsp, gsp, _full((1, GROUP)), _full((1, GROUP)), _full((GROUP, GROUP // 2))] + [HBM] * ng,
        out_specs=[dsp] + planes + [gsp] + planes
                  + [_full((2 * GROUP, D)), _full((1, D)), _full((1, GROUP)), _full((1, GROUP))] + [HBM] * ng,
        out_shape=[jax.ShapeDtypeStruct((S, D), F32)] + [_strided_shape(S, r, BF16) for r in DILATIONS]
                  + [jax.ShapeDtypeStruct((S, GROUP), BF16)] + [_strided_shape(S, r, F32) for r in DILATIONS]
                  + [jax.ShapeDtypeStruct((2 * GROUP, D), F32),
                     jax.ShapeDtypeStruct((1, D), F32), jax.ShapeDtypeStruct((1, GROUP), F32),
                     jax.ShapeDtypeStruct((1, GROUP), F32)] + halves,
        scratch_shapes=[_strided_scratch(tm), pltpu.SemaphoreType.DMA((ng,)), pltpu.SemaphoreType.DMA((ng,))],
    )(dh2p, dy, x1, ffn_w, w_out_g, o_dil, o_sb, w_dil, w_sbn, seg_ones, *ffn_grads)


def _attn_in_bwd(dq_br, dk_br, dv_br, dqs, dks, dvs, qa, ka, qw, kw, cos_t, sin_t, seg_ones, h, w_in_g, x, dx1, attn_w):
    S, D = x.shape
    wc = w_in_g.shape[2]
    tm = 256

    def body(q0, q1, q2, k0, k1, k2, v0, v1, v2, dqs_ref, dks_ref, dvs_ref, qa_ref, ka_ref, qw_ref, kw_ref,
             cos_ref, sin_ref, g_ref, h_ref, w_ref, x_ref, dx1_ref, aw_ref,
             gx_ref, dw_ref, daw_ref, dqw_ref, dkw_ref, accq, acck, scr):
        i = pl.program_id(0)

        @pl.when(i == 0)
        def _():
            accq[...] = jnp.zeros_like(accq)
            acck[...] = jnp.zeros_like(acck)
            dw_ref[...] = jnp.zeros_like(dw_ref)
            daw_ref[...] = jnp.zeros_like(daw_ref)

        def branches(refs):
            return (_from_strided(scr, refs[0]) + _from_strided(scr, refs[1])) + _from_strided(scr, refs[2])

        g = g_ref[...]
        cos = _tile4(cos_ref[...])
        sin = _tile4(sin_ref[...])
        pieces = []
        for refs, pre_ref, w_r, acc in (((q0, q1, q2), qa_ref, qw_ref, accq), ((k0, k1, k2), ka_ref, kw_ref, acck)):
            dh = branches(refs)
            dn = dh * cos + _rot_half(dh * sin)
            pre = pre_ref[...]
            rstd = lax.rsqrt(_segsum(pre * pre, g) * (1.0 / HEAD_DIM) + EPS)
            xh = pre * rstd
            acc[...] += jnp.sum(dn * xh, axis=0, keepdims=True)
            dxh = dn * w_r[...]
            pieces.append((rstd * (dxh - xh * (_segsum(dxh * xh, g) * (1.0 / HEAD_DIM)))).astype(BF16))
        pieces += [branches((v0, v1, v2)).astype(BF16), dqs_ref[...].astype(BF16), dks_ref[...].astype(BF16),
                   dvs_ref[...].astype(BF16)]
        dproj = jnp.concatenate(pieces, axis=1)
        hv = h_ref[...]
        dh = jnp.zeros((tm, D), F32)
        for j in range(N_CHIPS):
            dp = dproj[:, j * wc:(j + 1) * wc]
            dw_ref[j] += _dot_tn(hv, dp)
            dh = dh + _dot_nt(dp, w_ref[j])
        dx, dw = _rms_bwd(dh, x_ref[...], aw_ref[...])
        daw_ref[...] += jnp.sum(dw, axis=0, keepdims=True)
        gx_ref[...] = dx1_ref[...] + dx

        @pl.when(i == S // tm - 1)
        def _():
            for acc, o_ref in ((accq, dqw_ref), (acck, dkw_ref)):
                a = acc[...]
                pair = (a[:, 0:LANES] + a[:, LANES:2 * LANES]) + (a[:, 2 * LANES:3 * LANES] + a[:, 3 * LANES:4 * LANES])
                o_ref[...] = pair + pltpu.roll(pair, HEAD_DIM, 1)

    row = lambda i: (i, 0)
    gsp = pl.BlockSpec((tm, GROUP), row)
    dsp = pl.BlockSpec((tm, D), row)
    tab = pl.BlockSpec((tm, LANES), row)
    planes = [_strided_spec(tm, r) for r in DILATIONS]
    return pl.pallas_call(
        body, name="attn_in_bwd", grid=(S // tm,),
        in_specs=planes * 3 + [gsp] * 5 + [_full((1, GROUP)), _full((1, GROUP)), tab, tab, _full((GROUP, GROUP // 2)),
                                          dsp, _full((N_CHIPS, D, wc)), dsp, dsp, _full((1, D))],
        out_specs=[dsp, _full((N_CHIPS, D, wc)), _full((1, D)), _full((1, LANES)), _full((1, LANES))],
        out_shape=[jax.ShapeDtypeStruct((S, D), F32), jax.ShapeDtypeStruct((N_CHIPS, D, wc), F32),
                   jax.ShapeDtypeStruct((1, D), F32), jax.ShapeDtypeStruct((1, LANES), F32),
                   jax.ShapeDtypeStruct((1, LANES), F32)],
        scratch_shapes=[pltpu.VMEM((1, GROUP), F32), pltpu.VMEM((1, GROUP), F32), _strided_scratch(tm)],
    )(*dq_br, *dk_br, *dv_br, dqs, dks, dvs, qa, ka, qw, kw, cos_t, sin_t, seg_ones, h, w_in_g, x, dx1, attn_w)


def _constants(S):
    pos = jnp.arange(S, dtype=F32)
    inv_freq = ROPE_THETA ** (-jnp.arange(0, HEAD_DIM, 2, dtype=F32) / HEAD_DIM)
    ang = pos[:, None] * inv_freq[None, :]
    cos, sin = jnp.cos(ang), jnp.sin(ang)
    cos_t = jnp.concatenate([cos, cos] * 2, axis=1)
    sin_t = jnp.concatenate([-sin, sin] * 2, axis=1)
    idx = jnp.arange(GROUP // 2)
    seg_ones = (idx[:, None] // HEAD_DIM == idx[None, :] // HEAD_DIM).astype(BF16)
    seg_ones = jnp.concatenate([seg_ones, seg_ones], axis=0)
    r = jnp.arange(BLOCK)
    ones = jnp.ones((BLOCK, BLOCK), BF16)
    tris = [jnp.concatenate([jnp.concatenate([m.astype(BF16), ones], axis=1)] * 2, axis=0) for m in
            (r[:, None] > r[None, :],
             r[:, None] <= r[None, :],
             r[:, None] < r[None, :])]
    return cos_t, sin_t, seg_ones, tris


FFN_NAMES = ("w_gate", "w_up", "w_down")


def _device_step(x, target, attn_w, qn_w, kn_w, dil_w, sbn_w, ffn_w, w_in_g, w_out_slots, ffn_slots, core, chip):
    S = x.shape[0]
    cos_t, sin_t, seg_ones, (tri_later, tri_upto, tri_before) = _constants(S)
    reps = GROUP // HEAD_DIM
    qw = jnp.tile(qn_w, (1, reps))
    kw = jnp.tile(kn_w, (1, reps))

    nd = len(DILATIONS)
    h, qa, ka, *rest, wg_g = _in_proj_fwd(x, attn_w, w_in_g, qw, kw, cos_t, sin_t, seg_ones, ffn_slots[:1])
    qh, kh, va, (qs, ks, vs) = rest[:nd], rest[nd:2 * nd], rest[2 * nd:3 * nd], rest[3 * nd:]
    hosted = ([], [w_out_slots], [ffn_slots[2]])
    branches = [_dil_fwd(qh[b], kh[b], va[b], hosted[b]) for b in range(nd)]
    w_out_g, wd_g = branches[1][2].reshape(-1, x.shape[1]), branches[2][2]
    o_sb, ltot, walked, wu_g = _sb_fwd(qs, ks, vs, tri_later, ffn_slots[1:2])
    o_dil, *lse, x1 = _out_proj_fwd([b[0] for b in branches], [b[1] for b in branches], o_sb, x, dil_w, sbn_w, w_out_g)
    h2, g, u, dy, loss_parts = _ffn_fwd(x1, target, ffn_w, wg_g, wu_g, wd_g)

    *ffn_grads, dh2p = _ffn_bwd(h2, dy, g, u, wg_g, wu_g, wd_g)
    dx1, *mid, dw_out, dffn_w, ddil_w, dsbn_w, p0, p1, p2 = _out_proj_bwd(
        dh2p, dy, x1, ffn_w, w_out_g, o_dil, o_sb, dil_w, sbn_w, seg_ones, ffn_grads)
    do_dil, do_sb, delta = mid[:nd], mid[nd], mid[nd + 1:]
    parts = [_pair_sum(gr, fr, core, n) for gr, fr, n in zip(ffn_grads, (p0, p1, p2), FFN_NAMES)]
    dqs, dks, dvs = _sb_bwd(walked[:, 0, 0], qs, ks, vs, do_sb, ltot, tri_upto, tri_before)
    dbr = [_dil_bwd(qh[b], kh[b], va[b], do_dil[b], lse[b], delta[b], parts[b]) for b in range(nd)]
    ffn_halves = [_chip_sum(dbr[b][3], parts[b], chip, FFN_NAMES[b]) for b in range(nd)]
    grad_x, dw_in, dattn_w, dqw, dkw = _attn_in_bwd(
        [b[0] for b in dbr], [b[1] for b in dbr], [b[2] for b in dbr], dqs, dks, dvs,
        qa, ka, qw, kw, cos_t, sin_t, seg_ones, h, w_in_g, x, dx1, attn_w)
    small = dict(attn=dattn_w, q=dqw[:, :HEAD_DIM], k=dkw[:, :HEAD_DIM], dil=ddil_w, sb=dsbn_w, ffn=dffn_w)
    return loss_parts, grad_x, small, dw_in, dw_out, ffn_halves


HBM = pl.BlockSpec(memory_space=pltpu.HBM)
VMEM = pl.BlockSpec(memory_space=pltpu.VMEM)
CHIP_FLIPS = ((1, 0), (0, 1), (1, 1))


def _place():
    return lax.axis_index("x"), lax.axis_index("y"), lax.axis_index("c")


def _flip(v, d):
    return 1 - v if d else v


def _half_rows(c, n):
    return pl.ds(pl.multiple_of(c * (n // 2), 16), n // 2)


def _gather_plan(slot_in, slot_out, send, recv):
    x, y, c = _place()
    p = 2 * x + y
    chips = [(_flip(x, dx), _flip(y, dy)) for dx, dy in CHIP_FLIPS]
    mine, other = _half_rows(c, slot_in.shape[1]), _half_rows(1 - c, slot_in.shape[1])

    def copy(k, src, dst, to):
        return pltpu.make_async_remote_copy(src_ref=src, dst_ref=dst, send_sem=send.at[k], recv_sem=recv.at[k],
                                            device_id=to, device_id_type=MESH)

    def first(k):
        return copy(k, slot_in.at[p, mine], slot_out.at[p, mine], (*chips[k], c))

    def passed(k, rows):
        land = slot_out.at[2 * chips[k][0] + chips[k][1], rows]
        return copy(3 + k, land, land, (x, y, 1 - c))

    def start():
        for k in range(3):
            first(k).start()

    def forward():
        for k in range(3):
            land = slot_out.at[2 * chips[k][0] + chips[k][1], mine]
            copy(k, land, land, (*chips[k], c)).wait_recv()
            passed(k, mine).start()

    def finish():
        for k in range(3):
            passed(k, other).wait_recv()
        for k in range(3):
            first(k).wait_send()
            passed(k, mine).wait_send()

    return start, forward, finish


def _chip_send_plan(part_in, recv_out, send, recv):
    x, y, c = _place()
    p = 2 * x + y
    chips = [(_flip(x, dx), _flip(y, dy)) for dx, dy in CHIP_FLIPS]

    def copy(k):
        q = 2 * chips[k][0] + chips[k][1]
        return pltpu.make_async_remote_copy(src_ref=part_in.at[q], dst_ref=recv_out.at[p], send_sem=send.at[k],
                                            recv_sem=recv.at[k], device_id=(*chips[k], c), device_id_type=MESH)

    def start():
        for k in range(3):
            copy(k).start()

    def finish():
        for k in range(3):
            land = recv_out.at[2 * chips[k][0] + chips[k][1]]
            pltpu.make_async_remote_copy(src_ref=land, dst_ref=land, send_sem=send.at[k], recv_sem=recv.at[k],
                                         device_id=(*chips[k], c), device_id_type=MESH).wait_recv()
        for k in range(3):
            copy(k).wait_send()

    return start, finish


def _pair_send_plan(grad_in, recv_out, send, recv):
    x, y, c = _place()

    def copy():
        theirs = _half_rows(1 - c, grad_in.shape[1])
        return pltpu.make_async_remote_copy(src_ref=grad_in.at[:, theirs, :], dst_ref=recv_out, send_sem=send,
                                            recv_sem=recv, device_id=(x, y, 1 - c), device_id_type=MESH)

    return (lambda: copy().start()), (lambda: copy().wait())


def _own_slots(shard):
    here = 2 * lax.axis_index("x") + lax.axis_index("y")
    return lax.dynamic_update_slice(lax.empty((N_CHIPS,) + shard.shape, shard.dtype), shard[None], (here, 0, 0))


def _gather_weights(shards):
    n = len(shards)

    def body(*refs):
        ins, outs = refs[:n], refs[n:2 * n]
        send, recv = refs[2 * n:]
        plans = [_gather_plan(ins[a], outs[a], send.at[pl.ds(6 * a, 6)], recv.at[pl.ds(6 * a, 6)]) for a in range(n)]
        for stage in range(3):
            for plan in plans:
                plan[stage]()

    slots = [_own_slots(s) for s in shards]
    return pl.pallas_call(
        body, name="gather_weights", in_specs=[HBM] * n, out_specs=[HBM] * n,
        out_shape=[jax.ShapeDtypeStruct(s.shape, s.dtype) for s in slots],
        input_output_aliases={a: a for a in range(n)},
        scratch_shapes=[pltpu.SemaphoreType.DMA((6 * n,)), pltpu.SemaphoreType.DMA((6 * n,))],
    )(*slots)


def _pair_exchange(grads, small):
    n = len(grads)

    def body(*refs):
        gin, sm = refs[:n], refs[n]
        gout, sm_all = refs[n + 1:2 * n + 1], refs[2 * n + 1]
        send, recv = refs[2 * n + 2:]
        x, y, c = _place()
        me = 4 * x + 2 * y + c
        big = [_pair_send_plan(gin[a], gout[a], send.at[a], recv.at[a]) for a in range(n)]
        for start, _ in big:
            start()
        sm_all[pl.ds(me, 1)] = sm[...][None]
        tiny = []
        for k in range(1, N_DEV):
            px, py, pc = _flip(x, k & 4), _flip(y, k & 2), _flip(c, k & 1)
            tiny.append((pltpu.make_async_remote_copy(
                src_ref=sm, dst_ref=sm_all.at[me], send_sem=send.at[n + k - 1], recv_sem=recv.at[n + k - 1],
                device_id=(px, py, pc), device_id_type=MESH), 4 * px + 2 * py + pc))
            tiny[-1][0].start()
        for k, (cp, peer) in enumerate(tiny):
            pltpu.make_async_remote_copy(src_ref=sm, dst_ref=sm_all.at[peer], send_sem=send.at[n + k],
                                         recv_sem=recv.at[n + k], device_id=(x, y, c),
                                         device_id_type=MESH).wait_recv()
            cp.wait_send()
        for _, finish in big:
            finish()

    halves = [jax.ShapeDtypeStruct((g.shape[0], g.shape[1] // 2, g.shape[2]), g.dtype) for g in grads]
    return pl.pallas_call(
        body, name="pair_exchange", in_specs=[HBM] * n + [VMEM], out_specs=[HBM] * n + [VMEM],
        out_shape=halves + [jax.ShapeDtypeStruct((N_DEV,) + small.shape, small.dtype)],
        scratch_shapes=[pltpu.SemaphoreType.DMA((n + N_DEV - 1,)), pltpu.SemaphoreType.DMA((n + N_DEV - 1,))],
    )(*grads, small)


def _chip_exchange(parts):
    n = len(parts)

    def body(*refs):
        pin, pout = refs[:n], refs[n:2 * n]
        send, recv = refs[2 * n:]
        plans = [_chip_send_plan(pin[a], pout[a], send.at[pl.ds(3 * a, 3)], recv.at[pl.ds(3 * a, 3)]) for a in range(n)]
        for stage in range(2):
            for plan in plans:
                plan[stage]()

    return pl.pallas_call(
        body, name="chip_exchange", in_specs=[HBM] * n, out_specs=[HBM] * n,
        out_shape=[jax.ShapeDtypeStruct(s.shape, s.dtype) for s in parts],
        scratch_shapes=[pltpu.SemaphoreType.DMA((3 * n,)), pltpu.SemaphoreType.DMA((3 * n,))],
    )(*parts)


def _pair_swap(halves):
    n = len(halves)

    def body(*refs):
        hin, hout = refs[:n], refs[n:2 * n]
        send, recv = refs[2 * n:]
        x, y, c = _place()
        swaps = [pltpu.make_async_remote_copy(src_ref=hin[a], dst_ref=hout[a], send_sem=send.at[a],
                                              recv_sem=recv.at[a], device_id=(x, y, 1 - c), device_id_type=MESH)
                 for a in range(n)]
        for cp in swaps:
            cp.start()
        for cp in swaps:
            cp.wait()

    return pl.pallas_call(
        body, name="pair_swap", in_specs=[HBM] * n, out_specs=[HBM] * n,
        out_shape=[jax.ShapeDtypeStruct(s.shape, s.dtype) for s in halves],
        scratch_shapes=[pltpu.SemaphoreType.DMA((n,)), pltpu.SemaphoreType.DMA((n,))],
    )(*halves)


def _pair_sum(grad, recv, c, tag):
    _, R, C = grad.shape
    hr = R // 2

    def body(c_ref, a_ref, b_ref, o_ref):
        o_ref[...] = (a_ref[...] + b_ref[...]).astype(BF16)

    return pl.pallas_call(
        body, name="pair_sum_" + tag,
        grid_spec=pltpu.PrefetchScalarGridSpec(
            num_scalar_prefetch=1, grid=(N_CHIPS,),
            in_specs=[pl.BlockSpec((1, hr, C), lambda s, cr: (s, cr[0], 0)),
                      pl.BlockSpec((1, hr, C), lambda s, cr: (s, 0, 0))],
            out_specs=pl.BlockSpec((1, hr, C), lambda s, cr: (s, 0, 0))),
        out_shape=jax.ShapeDtypeStruct((N_CHIPS, hr, C), BF16),
    )(c, grad, recv)


def _chip_sum(received, own, chip, tag):
    _, rows, C = received.shape
    tr = rows // 2

    def body(chip_ref, own_ref, r1_ref, r2_ref, r3_ref, o_ref):
        p = [r[0].astype(F32) for r in (own_ref, r1_ref, r2_ref, r3_ref)]
        o_ref[...] = (p[0] + p[1]) + (p[2] + p[3])

    def slot(k):
        return pl.BlockSpec((1, tr, C), lambda i, cr: (jnp.bitwise_xor(cr[0], k), i, 0))

    return pl.pallas_call(
        body, name="chip_sum_" + tag,
        grid_spec=pltpu.PrefetchScalarGridSpec(
            num_scalar_prefetch=1, grid=(rows // tr,), in_specs=[slot(0), slot(1), slot(2), slot(3)],
            out_specs=pl.BlockSpec((tr, C), lambda i, cr: (i, 0))),
        out_shape=jax.ShapeDtypeStruct((rows, C), F32),
    )(chip, own, received, received, received)


def _adamw_math(w, g, m, v):
    m = ADAM_B1 * m + (1.0 - ADAM_B1) * g
    v = ADAM_B2 * v + (1.0 - ADAM_B2) * (g * g)
    m_hat = m / (1.0 - ADAM_B1 ** ADAM_STEP)
    v_hat = v / (1.0 - ADAM_B2 ** ADAM_STEP)
    delta = -ADAM_LR * (m_hat / (jnp.sqrt(v_hat) + ADAM_EPS) + ADAM_WD * w)
    return delta, m, v


def _adamw(w, g_mine, g_other, m, v, c, tag):
    R, C = w.shape
    tr = R // 4

    def body(c_ref, w_ref, gm_ref, go_ref, m_ref, v_ref, g_ref, d_ref, nm_ref, nv_ref):
        g = jnp.where(pl.program_id(0) == c_ref[0], gm_ref[...], go_ref[...])
        g_ref[...] = g
        d_ref[...], nm_ref[...], nv_ref[...] = _adamw_math(w_ref[...], g, m_ref[...], v_ref[...])

    blk = pl.BlockSpec((tr, C), lambda h, i, cr: (2 * h + i, 0))
    half = pl.BlockSpec((tr, C), lambda h, i, cr: (i, 0))
    return pl.pallas_call(
        body, name="adamw_" + tag,
        grid_spec=pltpu.PrefetchScalarGridSpec(
            num_scalar_prefetch=1, grid=(2, 2), in_specs=[blk, half, half, blk, blk], out_specs=[blk] * 4),
        out_shape=[jax.ShapeDtypeStruct((R, C), F32)] * 4,
    )(c, w, g_mine, g_other, m, v)


def _small_update(all_small, w, m, v):
    def body(a_ref, w_ref, m_ref, v_ref, g_ref, d_ref, nm_ref, nv_ref):
        g = ((a_ref[0] + a_ref[1]) + (a_ref[2] + a_ref[3])) + ((a_ref[4] + a_ref[5]) + (a_ref[6] + a_ref[7]))
        g_ref[...] = g
        d_ref[...], nm_ref[...], nv_ref[...] = _adamw_math(w_ref[...], g, m_ref[...], v_ref[...])

    return pl.pallas_call(
        body, name="small_update", out_shape=[jax.ShapeDtypeStruct(w.shape, F32)] * 4,
    )(all_small, w, m, v)


SMALL_ROWS = (("attn", 0, 0), ("ffn", 1, 0), ("dil", 2, 0), ("sb", 2, GROUP), ("q", 3, 0), ("k", 3, HEAD_DIM),
              ("loss", 4, 0))


def _pack_small(vals, D):
    rows = [jnp.zeros((1, D), F32) for _ in range(8)]
    for name, r, off in SMALL_ROWS:
        if name in vals:
            rows[r] = lax.dynamic_update_slice(rows[r], vals[name].astype(F32), (0, off))
    return jnp.concatenate(rows, axis=0)


def _unpack_small(packed, vals):
    return {name: packed[r:r + 1, off:off + vals[name].shape[1]] for name, r, off in SMALL_ROWS if name in vals}


def kernel(x, attn_norm_w, w_in, q_norm_w, k_norm_w, dil_out_norm_w, sb_out_norm_w, w_out, ffn_norm_w, w_gate, w_up, w_down, loss_target, m_attn_norm_w, m_w_in, m_q_norm_w, m_k_norm_w, m_dil_out_norm_w, m_sb_out_norm_w, m_w_out, m_ffn_norm_w, m_w_gate, m_w_up, m_w_down, v_attn_norm_w, v_w_in, v_q_norm_w, v_k_norm_w, v_dil_out_norm_w, v_sb_out_norm_w, v_w_out, v_ffn_norm_w, v_w_gate, v_w_up, v_w_down):
    D = x.shape[-1]
    big_names = ("w_in", "w_out", "w_gate", "w_up", "w_down")
    flipped = ("w_gate", "w_up")
    tr = lambda a: jnp.swapaxes(a[0], 0, 1)
    big_w = dict(w_in=w_in[0], w_out=w_out[0], w_gate=tr(w_gate), w_up=tr(w_up), w_down=w_down[0])
    big_m = dict(w_in=m_w_in[0], w_out=m_w_out[0], w_gate=tr(m_w_gate), w_up=tr(m_w_up), w_down=m_w_down[0])
    big_v = dict(w_in=v_w_in[0], w_out=v_w_out[0], w_gate=tr(v_w_gate), w_up=tr(v_w_up), w_down=v_w_down[0])
    small_w = dict(attn=attn_norm_w, q=q_norm_w, k=k_norm_w, dil=dil_out_norm_w, sb=sb_out_norm_w, ffn=ffn_norm_w)
    small_m = dict(attn=m_attn_norm_w, q=m_q_norm_w, k=m_k_norm_w, dil=m_dil_out_norm_w, sb=m_sb_out_norm_w,
                   ffn=m_ffn_norm_w)
    small_v = dict(attn=v_attn_norm_w, q=v_q_norm_w, k=v_k_norm_w, dil=v_dil_out_norm_w, sb=v_sb_out_norm_w,
                   ffn=v_ffn_norm_w)

    c = lax.axis_index("c").astype(jnp.int32).reshape(1)
    chip = (2 * lax.axis_index("x") + lax.axis_index("y")).astype(jnp.int32).reshape(1)
    (w_in_g,) = _gather_weights([big_w["w_in"].astype(BF16)])
    w_out_slots = _own_slots(big_w["w_out"].astype(BF16))
    ffn_slots = [_own_slots(big_w[n].astype(BF16)) for n in FFN_NAMES]

    loss_parts, grad_x, small_g, dw_in, dw_out, ffn_halves = _device_step(
        x[0], loss_target[0], attn_norm_w, q_norm_w, k_norm_w, dil_out_norm_w, sb_out_norm_w, ffn_norm_w,
        w_in_g, w_out_slots, ffn_slots, c, chip)
    small_g["loss"] = (jnp.sum(loss_parts[:, 0, 0]) * (0.5 / D)).reshape(1, 1)

    late = [dw_in, dw_out.reshape(N_CHIPS, -1, D)]
    *from_pair, all_small = _pair_exchange(late, _pack_small(small_g, D))
    chip_parts = [_pair_sum(g, r, c, n) for g, r, n in zip(late, from_pair, big_names)]
    from_chips = _chip_exchange(chip_parts)
    halves = [_chip_sum(r, p, chip, n) for r, p, n in zip(from_chips, chip_parts, big_names)] + ffn_halves
    others = _pair_swap(halves)
    big_out = {n: _adamw(big_w[n], mine, other, big_m[n], big_v[n], c, n)
               for n, mine, other in zip(big_names, halves, others)}
    sg, sd, sm, sv = _small_update(all_small, _pack_small(small_w, D), _pack_small(small_m, D),
                                   _pack_small(small_v, D))
    small_out = [_unpack_small(t, small_w) for t in (sg, sd, sm, sv)]

    order = (("attn", None), (None, "w_in"), ("q", None), ("k", None), ("dil", None), ("sb", None),
             (None, "w_out"), ("ffn", None), (None, "w_gate"), (None, "w_up"), (None, "w_down"))
    outs = [sg[4, 0], grad_x[None]]
    for kind in range(4):
        for s_name, b_name in order:
            if s_name is not None:
                outs.append(small_out[kind][s_name])
            else:
                res = big_out[b_name][kind]
                outs.append((jnp.swapaxes(res, 0, 1) if b_name in flipped else res)[None])
    return tuple(outs)
```

```python
import functools

import jax
import jax.numpy as jnp
from jax import lax
from jax.experimental import pallas as pl
from jax.experimental.pallas import tpu as pltpu

F32 = jnp.float32
BF16 = jnp.bfloat16
MESH = pl.DeviceIdType.MESH

HEAD_DIM = 64
GROUP = 512
BLOCK = 128
LANES = 128
N_CHIPS = 4
N_DEV = 8
EPS = 1e-6
ROPE_THETA = 10000.0
DILATIONS = (1, 4, 16)
NEG = -1e30

ADAM_LR = 0.001
ADAM_B1 = 0.9
ADAM_B2 = 0.999
ADAM_EPS = 1e-08
ADAM_WD = 0.01
ADAM_STEP = 10


def _dot(a, b):
    return jnp.dot(a, b, preferred_element_type=F32)


def _dot_nt(a, b):
    return lax.dot_general(a, b, (((1,), (1,)), ((), ())), preferred_element_type=F32)


def _dot_tn(a, b):
    return lax.dot_general(a, b, (((0,), (0,)), ((), ())), preferred_element_type=F32)


def _split(v):
    hi = lax.bitcast_convert_type(lax.bitcast_convert_type(v, jnp.uint32) & jnp.uint32(0xFFFF0000), F32)
    return hi.astype(BF16), (v - hi).astype(BF16)


def _segsum(v, g):
    hi, lo = _split(v)
    w = g.shape[1]
    return jnp.concatenate([_dot(jnp.concatenate([hi[:, c:c + w], lo[:, c:c + w]], axis=1), g)
                            for c in range(0, v.shape[1], w)], axis=1)


def _rot_half(x):
    outs = []
    for c in range(x.shape[1] // LANES):
        xc = x[:, c * LANES:(c + 1) * LANES]
        lane = lax.broadcasted_iota(jnp.int32, xc.shape, 1)
        first = (lane % HEAD_DIM) < (HEAD_DIM // 2)
        outs.append(jnp.where(first, pltpu.roll(xc, LANES - 32, 1), pltpu.roll(xc, 32, 1)))
    return outs[0] if len(outs) == 1 else jnp.concatenate(outs, axis=1)


def _rms(x):
    return lax.rsqrt(jnp.mean(x * x, axis=-1, keepdims=True) + EPS)


def _rms_bwd(dy, x, w):
    rstd = _rms(x)
    xh = x * rstd
    dxh = dy * w
    dx = rstd * (dxh - xh * jnp.mean(dxh * xh, axis=-1, keepdims=True))
    return dx, dy * xh


def _sigmoid(x):
    return 1.0 / (1.0 + jnp.exp(-x))


def _sum4(ref):
    p = [ref[j].astype(F32) for j in range(N_CHIPS)]
    return (p[0] + p[1]) + (p[2] + p[3])


def _full(shape):
    n = len(shape)
    return pl.BlockSpec(shape, lambda *_: (0,) * n)


def _strided_spec(tm, r):
    return pl.BlockSpec((r, tm // r, GROUP), lambda i: (0, i, 0))


def _strided_shape(S, r, dtype):
    return jax.ShapeDtypeStruct((r, S // r, GROUP), dtype)


def _to_strided(scr, val, outs):
    chunks = range(GROUP // LANES)
    for k in chunks:
        scr[k] = val[:, _lanes(k)]
    for r, o_ref in outs:
        if r == 1:
            o_ref[0] = val.astype(o_ref.dtype)
            continue
        n = val.shape[0] // r
        for c in range(r):
            rows = pl.ds(c, n, stride=r)
            o_ref[c] = jnp.concatenate([scr.at[k][rows, :] for k in chunks], axis=1).astype(o_ref.dtype)


def _from_strided(scr, ref):
    r, n, _ = ref.shape
    if r == 1:
        return ref[0].astype(F32)
    chunks = range(GROUP // LANES)
    for c in range(r):
        plane = ref[c].astype(F32)
        for k in chunks:
            scr.at[k][pl.ds(c, n, stride=r), :] = plane[:, _lanes(k)]
    return jnp.concatenate([scr[k] for k in chunks], axis=1)


def _strided_scratch(tm):
    return pltpu.VMEM((GROUP // LANES, tm, LANES), F32)


def _tile4(t):
    return jnp.concatenate([t] * (GROUP // LANES), axis=1)


def _in_proj_fwd(x, attn_w, w_in_g, qw, kw, cos_t, sin_t, seg_ones, slots):
    S, D = x.shape
    tm = 512
    wcols = w_in_g.shape[2]
    nd = len(DILATIONS)
    ns = len(slots)

    def body(x_ref, aw_ref, w_ref, qw_ref, kw_ref, cos_ref, sin_ref, g_ref, *rest):
        slot_in, (h_ref, qa_ref, ka_ref), rest = rest[:ns], rest[ns:ns + 3], rest[ns + 3:]
        q_refs, k_refs, v_refs = rest[:nd], rest[nd:2 * nd], rest[2 * nd:3 * nd]
        qs_ref, ks_ref, vs_ref = rest[3 * nd:3 * nd + 3]
        slot_out, scr, sems = rest[3 * nd + 3:3 * nd + 3 + ns], rest[3 * nd + 3 + ns], rest[3 * nd + 4 + ns:]
        finish = _hosted_gathers(slot_in, slot_out, *sems, pl.program_id(0), S // tm) if ns else None
        xv = x_ref[...]
        h = (xv * _rms(xv) * aw_ref[...]).astype(BF16)
        h_ref[...] = h
        proj = jnp.concatenate([_dot(h, w_ref[j]) for j in range(N_CHIPS)], axis=1)
        qa = proj[:, 0 * GROUP:1 * GROUP]
        ka = proj[:, 1 * GROUP:2 * GROUP]
        qa_ref[...] = qa
        ka_ref[...] = ka
        _to_strided(scr, proj[:, 2 * GROUP:3 * GROUP], list(zip(DILATIONS, v_refs)))
        qs_ref[...] = proj[:, 3 * GROUP:4 * GROUP].astype(BF16)
        ks_ref[...] = proj[:, 4 * GROUP:5 * GROUP].astype(BF16)
        vs_ref[...] = proj[:, 5 * GROUP:6 * GROUP].astype(BF16)
        g = g_ref[...]
        cos = _tile4(cos_ref[...])
        sin = _tile4(sin_ref[...])
        for t, w_r, o_rs in ((qa, qw_ref, q_refs), (ka, kw_ref, k_refs)):
            rstd = lax.rsqrt(_segsum(t * t, g) * (1.0 / HEAD_DIM) + EPS)
            tn = t * rstd * w_r[...]
            _to_strided(scr, tn * cos + _rot_half(tn) * sin, list(zip(DILATIONS, o_rs)))
        if finish is not None:
            finish()

    row = lambda i: (i, 0)
    tile = lambda n, dt: jax.ShapeDtypeStruct((S, n), dt)
    planes = [_strided_spec(tm, r) for r in DILATIONS]
    h_in, h_out, h_shape, h_sems = _hosted_specs(slots)
    n_out = 6 + 3 * nd
    return pl.pallas_call(
        body, name="in_proj_fwd", grid=(S // tm,),
        in_specs=[pl.BlockSpec((tm, D), row), _full((1, D)), _full((N_CHIPS, D, wcols)),
                  _full((1, GROUP)), _full((1, GROUP)),
                  pl.BlockSpec((tm, LANES), row), pl.BlockSpec((tm, LANES), row),
                  _full((GROUP, GROUP // 2))] + h_in,
        out_specs=[pl.BlockSpec((tm, D), row)] + [pl.BlockSpec((tm, GROUP), row)] * 2 + planes * 3
                  + [pl.BlockSpec((tm, GROUP), row)] * 3 + h_out,
        out_shape=[tile(D, BF16), tile(GROUP, F32), tile(GROUP, F32)]
                  + [_strided_shape(S, r, BF16) for r in DILATIONS] * 3 + [tile(GROUP, BF16)] * 3 + h_shape,
        input_output_aliases={8 + a: n_out + a for a in range(ns)},
        scratch_shapes=[_strided_scratch(tm)] + h_sems,
    )(x, attn_w, w_in_g, qw, kw, cos_t, sin_t, seg_ones, *slots)


DIL_PLANES = 1


def _dil_fwd(q, k, v, slots):
    r, L, _ = q.shape
    nb = L // BLOCK
    P = GROUP // LANES
    PL = min(DIL_PLANES, r)
    ns = len(slots)
    units = [(pp, hp) for pp in range(PL) for hp in range(P)]

    def body(q_ref, kc_ref, kp_ref, vc_ref, vp_ref, *rest):
        o_ref, l_ref = rest[ns:ns + 2]
        n = pl.program_id(1)
        finish = (_hosted_gathers(rest[:ns], rest[ns + 2:2 * ns + 2], *rest[2 * ns + 2:],
                                  pl.program_id(0) * nb + n, (r // PL) * nb) if ns else None)
        rowi = lax.broadcasted_iota(jnp.int32, (BLOCK, BLOCK), 0)
        coli = lax.broadcasted_iota(jnp.int32, (BLOCK, BLOCK), 1)
        first = coli < HEAD_DIM
        masks = (coli <= rowi, jnp.logical_and(coli >= rowi, n > 0))
        s2 = {}
        for pp, hp in units:
            q2 = _scaled(q_ref[pp, :, _lanes(hp)])
            for b, k_ref in enumerate((kc_ref, kp_ref)):
                s2[pp, hp, b] = _dot_nt(q2, _by_head(k_ref[pp, :, _lanes(hp)], first))
        ps, inv, lse = {}, {}, {}
        for pp, hp in units:
            for h in range(2):
                s = [jnp.where(masks[b], s2[pp, hp, b][:, h * BLOCK:(h + 1) * BLOCK], NEG) for b in range(2)]
                m = jnp.maximum(jnp.max(s[0], axis=1, keepdims=True), jnp.max(s[1], axis=1, keepdims=True))
                p = [jnp.exp(s[b] - m) for b in range(2)]
                den = jnp.sum(p[0], axis=1, keepdims=True) + jnp.sum(p[1], axis=1, keepdims=True)
                ps[pp, hp, h] = [p[b].astype(BF16) for b in range(2)]
                inv[pp, hp, h] = 1.0 / den
                lse[pp, hp, h] = m + jnp.log(den)
        for pp, hp in units:
            o = jnp.zeros((BLOCK, LANES), F32)
            for b, v_ref in enumerate((vc_ref, vp_ref)):
                o = o + _dot(jnp.concatenate([ps[pp, hp, 0][b], ps[pp, hp, 1][b]], axis=1),
                             _by_head(v_ref[pp, :, _lanes(hp)], first))
            o_ref[pp, :, _lanes(hp)] = o * jnp.where(first, inv[pp, hp, 0], inv[pp, hp, 1])
            l_ref[pp, :, _lanes(hp)] = jnp.where(first, lse[pp, hp, 0], lse[pp, hp, 1])
        if finish is not None:
            finish()

    cur = pl.BlockSpec((PL, BLOCK, GROUP), lambda c, n: (c, n, 0))
    prev = pl.BlockSpec((PL, BLOCK, GROUP), lambda c, n: (c, jnp.maximum(n - 1, 0), 0))
    h_in, h_out, h_shape, h_sems = _hosted_specs(slots)
    return pl.pallas_call(
        body, name="dil_fwd_r%d" % r, grid=(r // PL, nb),
        in_specs=[cur, cur, prev, cur, prev] + h_in, out_specs=[cur, cur] + h_out,
        out_shape=[jax.ShapeDtypeStruct(q.shape, F32)] * 2 + h_shape,
        input_output_aliases={5 + a: 2 + a for a in range(ns)},
        scratch_shapes=h_sems,
    )(q, k, k, v, v, *slots)


def _dil_bwd(q, k, v, do, lse, delta, part):
    r, L, _ = q.shape
    nb = L // BLOCK
    P = GROUP // LANES
    PL = min(DIL_PLANES, r)
    scale = HEAD_DIM ** -0.5
    units = [(pp, hp) for pp in range(PL) for hp in range(P)]

    def body(qc_ref, qn_ref, doc_ref, don_ref, lc_ref, ln_ref, dc_ref, dn_ref, k_ref, v_ref, part_in,
             dq_ref, dk_ref, dv_ref, part_out, carry, send, recv):
        j = pl.program_id(1)
        step = pl.program_id(0) * nb + j
        start, finish = _chip_send_plan(part_in, part_out, send, recv)
        pl.when(step == 0)(start)
        rowi = lax.broadcasted_iota(jnp.int32, (BLOCK, BLOCK), 0)
        coli = lax.broadcasted_iota(jnp.int32, (BLOCK, BLOCK), 1)
        first = coli < HEAD_DIM
        sides = ((qc_ref, doc_ref, lc_ref, dc_ref, coli <= rowi),
                 (qn_ref, don_ref, ln_ref, dn_ref, jnp.logical_and(coli >= rowi, j < nb - 1)))

        @pl.when(j == 0)
        def _():
            carry[...] = jnp.zeros_like(carry)

        kcat, q2, do2, s2, dp2 = {}, {}, {}, {}, {}
        for pp, hp in units:
            kcat[pp, hp] = _by_head(k_ref[pp, :, _lanes(hp)], first)
            vcat = _by_head(v_ref[pp, :, _lanes(hp)], first)
            for x, (q_r, do_r, _, _, _) in enumerate(sides):
                q2[pp, hp, x] = _scaled(q_r[pp, :, _lanes(hp)])
                do2[pp, hp, x] = do_r[pp, :, _lanes(hp)]
                s2[pp, hp, x] = _dot_nt(q2[pp, hp, x], kcat[pp, hp])
                dp2[pp, hp, x] = _dot_nt(do2[pp, hp, x], vcat)
        pcat, dscat = {}, {}
        for pp, hp in units:
            for x, (_, _, l_r, d_r, msk) in enumerate(sides):
                ps, dss = [], []
                for h in range(2):
                    col = hp * LANES + h * HEAD_DIM
                    half = slice(h * BLOCK, (h + 1) * BLOCK)
                    p = jnp.where(msk, jnp.exp(s2[pp, hp, x][:, half] - l_r[pp, :, col:col + 1]), 0.0)
                    ps.append(p.astype(BF16))
                    dss.append((p * (dp2[pp, hp, x][:, half] - d_r[pp, :, col:col + 1])).astype(BF16))
                pcat[pp, hp, x] = jnp.concatenate(ps, axis=1)
                dscat[pp, hp, x] = jnp.concatenate(dss, axis=1)
        for pp, hp in units:
            dv2 = _dot_tn(pcat[pp, hp, 0], do2[pp, hp, 0]) + _dot_tn(pcat[pp, hp, 1], do2[pp, hp, 1])
            dk2 = _dot_tn(dscat[pp, hp, 0], q2[pp, hp, 0]) + _dot_tn(dscat[pp, hp, 1], q2[pp, hp, 1])
            dv_ref[pp, :, _lanes(hp)] = jnp.where(first, dv2[:BLOCK], dv2[BLOCK:]).astype(BF16)
            dk_ref[pp, :, _lanes(hp)] = jnp.where(first, dk2[:BLOCK], dk2[BLOCK:]).astype(BF16)
            dq_ref[pp, :, _lanes(hp)] = (carry[pp, :, _lanes(hp)]
                                         + _dot(dscat[pp, hp, 0], kcat[pp, hp]) * scale).astype(BF16)
            carry[pp, :, _lanes(hp)] = _dot(dscat[pp, hp, 1], kcat[pp, hp]) * scale
        pl.when(step == (r // PL) * nb - 1)(finish)

    cur = pl.BlockSpec((PL, BLOCK, GROUP), lambda c, n: (c, n, 0))
    nxt = pl.BlockSpec((PL, BLOCK, GROUP), lambda c, n: (c, jnp.minimum(n + 1, nb - 1), 0))
    return pl.pallas_call(
        body, name="dil_bwd_r%d" % r, grid=(r // PL, nb),
        in_specs=[cur, nxt, cur, nxt, cur, nxt, cur, nxt, cur, cur, HBM], out_specs=[cur, cur, cur, HBM],
        out_shape=[jax.ShapeDtypeStruct(q.shape, BF16)] * 3 + [jax.ShapeDtypeStruct(part.shape, part.dtype)],
        scratch_shapes=[pltpu.VMEM((PL, BLOCK, GROUP), F32), pltpu.SemaphoreType.DMA((3,)),
                        pltpu.SemaphoreType.DMA((3,))],
    )(q, q, do, do, lse, lse, delta, delta, k, v, part)


SB_TILES = 2
SB_PAIRS_FWD = 4
SB_PAIRS_BWD = 4
SB_DEAD = -110.0


def _lanes(hp):
    return slice(hp * LANES, (hp + 1) * LANES)


def _sb_logits(z, valid):
    e = jnp.exp(-jnp.abs(z))
    lb = jnp.minimum(z, 0.0) - jnp.log(1.0 + e)
    lk = lb - z
    if valid is not None:
        lk = jnp.where(valid, lk, 0.0)
    return e, lb, lk


def _by_head(t, first):
    zero = jnp.zeros_like(t)
    return jnp.concatenate([jnp.where(first, t, zero), jnp.where(first, zero, t)], axis=0)


def _sb_valid(i, j):
    rowi = lax.broadcasted_iota(jnp.int32, (BLOCK, BLOCK), 0)
    coli = lax.broadcasted_iota(jnp.int32, (BLOCK, BLOCK), 1)
    return (coli - rowi) < (i - j) * BLOCK


def _scaled(q):
    return (q.astype(F32) * (HEAD_DIM ** -0.5)).astype(BF16)


def _hosted_gathers(refs_in, refs_out, send, recv, step, steps):
    plans = [_gather_plan(refs_in[a], refs_out[a], send.at[pl.ds(6 * a, 6)], recv.at[pl.ds(6 * a, 6)])
             for a in range(len(refs_in))]
    for stage, at in ((0, 0), (1, (2 * steps) // 3)):
        @pl.when(step == at)
        def _():
            for plan in plans:
                plan[stage]()

    def finish():
        @pl.when(step == steps - 1)
        def _():
            for plan in plans:
                plan[2]()

    return finish


def _hosted_specs(slots):
    n = len(slots)
    sems = [pltpu.SemaphoreType.DMA((6 * n,))] * 2 if n else []
    return [HBM] * n, [HBM] * n, [jax.ShapeDtypeStruct(s.shape, s.dtype) for s in slots], sems


def _sb_fwd(qs, ks, vs, tri_later, slots):
    S = qs.shape[0]
    P = SB_PAIRS_FWD
    W = P * LANES
    ns = len(slots)

    def body(q_ref, k_ref, v_ref, u_ref, *rest):
        o_ref, lt_ref, from_ref = rest[ns:ns + 3]
        i = pl.program_id(1)
        finish = _hosted_gathers(rest[:ns], rest[ns + 3:2 * ns + 3], *rest[2 * ns + 3:], i, S // BLOCK) if ns else None
        first = lax.broadcasted_iota(jnp.int32, (BLOCK, LANES), 1) < HEAD_DIM
        q2 = [_scaled(q_ref[:, _lanes(hp)]) for hp in range(P)]

        def walk(tiles, carry):
            runs, accs = list(carry[0]), list(carry[1])
            units = [(t, hp) for t in range(len(tiles)) for hp in range(P)]
            offs = [pl.multiple_of(j * BLOCK, BLOCK) for j, _ in tiles]
            valids = [_sb_valid(i, j) if diag else None for j, diag in tiles]
            z2s, lbs, c2s = {}, {}, {}
            for t, hp in units:
                z2s[t, hp] = _dot_nt(q2[hp], _by_head(k_ref[pl.ds(offs[t], BLOCK), _lanes(hp)], first))
            for t, hp in units:
                for h in range(2):
                    _, lb, lk = _sb_logits(z2s[t, hp][:, h * BLOCK:(h + 1) * BLOCK], valids[t])
                    lbs[t, hp, h] = lb
                    c2s[t, hp, h] = _dot(jnp.concatenate(_split(lk), axis=1), u_ref[...])
            for t, hp in units:
                a2 = []
                for h in range(2):
                    a = jnp.exp(lbs[t, hp, h] + c2s[t, hp, h][:, :BLOCK] + runs[2 * hp + h])
                    if valids[t] is not None:
                        a = jnp.where(valids[t], a, 0.0)
                    a2.append(a.astype(BF16))
                    runs[2 * hp + h] = runs[2 * hp + h] + c2s[t, hp, h][:, BLOCK:]
                vcat = _by_head(v_ref[pl.ds(offs[t], BLOCK), _lanes(hp)], first)
                accs[hp] = accs[hp] + _dot(jnp.concatenate(a2, axis=1), vcat)
            return tuple(runs), tuple(accs)

        def chunk(ci, carry):
            return walk([(ci * SB_TILES + t, False) for t in reversed(range(SB_TILES))], carry)

        def alive(runs):
            top = functools.reduce(jnp.maximum, runs)
            return (jnp.max(top) > SB_DEAD).astype(jnp.int32)

        def step(c):
            t, _, runs, accs = c
            runs, accs = chunk(nfull - 1 - t, (runs, accs))
            return t + 1, alive(runs), runs, accs

        zero = jnp.zeros((BLOCK, LANES), F32)
        nfull = i // SB_TILES
        ragged = [functools.partial(walk, [(i, True)] + [(i - 1 - m, False) for m in range(extra)])
                  for extra in range(SB_TILES)]
        runs, accs = lax.switch(i % SB_TILES, ragged, ((zero,) * (2 * P), (zero,) * P))
        done, _, runs, accs = lax.while_loop(lambda c: jnp.logical_and(c[0] < nfull, c[1] > 0), step,
                                             (jnp.int32(0), alive(runs), runs, accs))
        for hp in range(P):
            o_ref[:, _lanes(hp)] = accs[hp]
            lt_ref[:, _lanes(hp)] = jnp.where(first, runs[2 * hp], runs[2 * hp + 1])
        from_ref[...] = jnp.full(from_ref.shape, nfull - done, jnp.int32)
        if finish is not None:
            finish()

    assert W == GROUP
    blk = pl.BlockSpec((BLOCK, W), lambda hp, i: (i, hp))
    col = pl.BlockSpec((S, W), lambda hp, i: (0, hp))
    h_in, h_out, h_shape, h_sems = _hosted_specs(slots)
    return pl.pallas_call(
        body, name="sb_fwd", grid=(GROUP // W, S // BLOCK),
        in_specs=[blk, col, col, _full((2 * BLOCK, 2 * BLOCK))] + h_in,
        out_specs=[blk, blk, pl.BlockSpec((1, 8, LANES), lambda hp, i: (i, 0, 0))] + h_out,
        out_shape=[jax.ShapeDtypeStruct((S, GROUP), F32)] * 2
                  + [jax.ShapeDtypeStruct((S // BLOCK, 8, LANES), jnp.int32)] + h_shape,
        input_output_aliases={4 + a: 3 + a for a in range(ns)},
        scratch_shapes=h_sems,
    )(qs, ks, vs, tri_later, *slots)


def _sb_bwd(first_chunk, qs, ks, vs, do, ltot, tri_upto, tri_before):
    S = qs.shape[0]
    P = SB_PAIRS_BWD
    W = P * LANES

    def body(from_ref, q_ref, k_ref, v_ref, do_ref, lt_ref, w_ref, x_ref, dq_ref, dk_ref, dv_ref):
        i = pl.program_id(1)

        @pl.when(i == 0)
        def _():
            dk_ref[...] = jnp.zeros_like(dk_ref)
            dv_ref[...] = jnp.zeros_like(dv_ref)

        first = lax.broadcasted_iota(jnp.int32, (BLOCK, LANES), 1) < HEAD_DIM
        q2 = [_scaled(q_ref[:, _lanes(hp)]) for hp in range(P)]
        do2 = [do_ref[:, _lanes(hp)] for hp in range(P)]
        totals = [jnp.broadcast_to(lt_ref[:, n * HEAD_DIM:n * HEAD_DIM + 1], (BLOCK, LANES)) for n in range(2 * P)]

        def walk(tiles, carry):
            keeps, grads, dqs = list(carry[0]), list(carry[1]), list(carry[2])
            units = [(t, hp) for t in range(len(tiles)) for hp in range(P)]
            offs = [pl.multiple_of(j * BLOCK, BLOCK) for j, _ in tiles]
            valids = [_sb_valid(i, j) if diag else None for j, diag in tiles]
            kcat, z2, da2, es, lbs, c2s, as_, des, p2s = {}, {}, {}, {}, {}, {}, {}, {}, {}
            for t, hp in units:
                kcat[t, hp] = _by_head(k_ref[pl.ds(offs[t], BLOCK), _lanes(hp)], first)
                z2[t, hp] = _dot_nt(q2[hp], kcat[t, hp])
                da2[t, hp] = _dot_nt(do2[hp], _by_head(v_ref[pl.ds(offs[t], BLOCK), _lanes(hp)], first))
            for t, hp in units:
                for h in range(2):
                    es[t, hp, h], lbs[t, hp, h], lk = _sb_logits(z2[t, hp][:, h * BLOCK:(h + 1) * BLOCK], valids[t])
                    c2s[t, hp, h] = _dot(jnp.concatenate(_split(lk), axis=1), w_ref[...])
            for t, hp in units:
                for h in range(2):
                    n = 2 * hp + h
                    a = jnp.exp(lbs[t, hp, h] + (totals[n] - (keeps[n] + c2s[t, hp, h][:, :BLOCK])))
                    if valids[t] is not None:
                        a = jnp.where(valids[t], a, 0.0)
                    keeps[n] = keeps[n] + c2s[t, hp, h][:, BLOCK:]
                    de = a * da2[t, hp][:, h * BLOCK:(h + 1) * BLOCK]
                    as_[t, hp, h], des[t, hp, h] = a.astype(BF16), de
                    p2s[t, hp, h] = _dot(jnp.concatenate(_split(de), axis=1), x_ref[...])
            for t, hp in units:
                dz2 = []
                for h in range(2):
                    n = 2 * hp + h
                    e = es[t, hp, h]
                    sig = jnp.where(z2[t, hp][:, h * BLOCK:(h + 1) * BLOCK] >= 0.0, 1.0, e) / (1.0 + e)
                    dz = des[t, hp, h] * (1.0 - sig) - (grads[n] + p2s[t, hp, h][:, :BLOCK]) * sig
                    if valids[t] is not None:
                        dz = jnp.where(valids[t], dz, 0.0)
                    grads[n] = grads[n] + p2s[t, hp, h][:, BLOCK:]
                    dz2.append(dz.astype(BF16))
                dzcat = jnp.concatenate(dz2, axis=1)
                dk2 = _dot_tn(dzcat, q2[hp])
                dv2 = _dot_tn(jnp.concatenate([as_[t, hp, 0], as_[t, hp, 1]], axis=1), do2[hp])
                dk_ref[pl.ds(offs[t], BLOCK), _lanes(hp)] += jnp.where(first, dk2[:BLOCK], dk2[BLOCK:])
                dv_ref[pl.ds(offs[t], BLOCK), _lanes(hp)] += jnp.where(first, dv2[:BLOCK], dv2[BLOCK:])
                dqs[hp] = dqs[hp] + _dot(dzcat, kcat[t, hp])
            return tuple(keeps), tuple(grads), tuple(dqs)

        zero = jnp.zeros((BLOCK, LANES), F32)
        nfull = i // SB_TILES
        carry = lax.fori_loop(
            from_ref[i], nfull, lambda ci, c: walk([(ci * SB_TILES + t, False) for t in range(SB_TILES)], c),
            ((zero,) * (2 * P), (zero,) * (2 * P), (zero,) * P))
        ragged = [functools.partial(walk, [(i - m, False) for m in range(extra, 0, -1)] + [(i, True)])
                  for extra in range(SB_TILES)]
        carry = lax.switch(i % SB_TILES, ragged, carry)
        for hp in range(P):
            dq_ref[:, _lanes(hp)] = carry[2][hp] * (HEAD_DIM ** -0.5)

    blk = pl.BlockSpec((BLOCK, W), lambda hp, i, fr: (i, hp))
    col = pl.BlockSpec((S, W), lambda hp, i, fr: (0, hp))
    tri = pl.BlockSpec((2 * BLOCK, 2 * BLOCK), lambda hp, i, fr: (0, 0))
    return pl.pallas_call(
        body, name="sb_bwd",
        grid_spec=pltpu.PrefetchScalarGridSpec(
            num_scalar_prefetch=1, grid=(GROUP // W, S // BLOCK),
            in_specs=[blk, col, col, blk, blk, tri, tri], out_specs=[blk, col, col]),
        out_shape=[jax.ShapeDtypeStruct((S, GROUP), F32)] * 3,
    )(first_chunk, qs, ks, vs, do, ltot, tri_upto, tri_before)


def _out_proj_fwd(o_br, l_br, o_sb, x, w_dil, w_sbn, w_out_g):
    S, D = x.shape
    tm = 512

    def body(o0, o1, o2, l0, l1, l2, os_ref, x_ref, wd_ref, ws_ref, w_ref, od_ref, s0, s1, s2, x1_ref, scr):
        ls = [_from_strided(scr, l) for l in (l0, l1, l2)]
        os_ = [_from_strided(scr, o) for o in (o0, o1, o2)]
        m = jnp.maximum(jnp.maximum(ls[0], ls[1]), ls[2])
        es = [jnp.exp(l - m) for l in ls]
        den = es[0] + es[1] + es[2]
        od = (es[0] * os_[0] + es[1] * os_[1] + es[2] * os_[2]) / den
        od_ref[...] = od
        _to_strided(scr, m + jnp.log(den), list(zip(DILATIONS, (s0, s1, s2))))
        osb = os_ref[...]
        mixed = jnp.concatenate([(od * _rms(od) * wd_ref[...]).astype(BF16),
                                 (osb * _rms(osb) * ws_ref[...]).astype(BF16)], axis=1)
        x1_ref[...] = x_ref[...] + _dot(mixed, w_ref[...])

    row = lambda i: (i, 0)
    g = pl.BlockSpec((tm, GROUP), row)
    d = pl.BlockSpec((tm, D), row)
    planes = [_strided_spec(tm, r) for r in DILATIONS]
    return pl.pallas_call(
        body, name="out_proj_fwd", grid=(S // tm,),
        in_specs=planes * 2 + [g, d, _full((1, GROUP)), _full((1, GROUP)), _full((2 * GROUP, D))],
        out_specs=[g] + planes + [d],
        out_shape=[jax.ShapeDtypeStruct((S, GROUP), F32)] + [_strided_shape(S, r, F32) for r in DILATIONS]
                  + [jax.ShapeDtypeStruct((S, D), F32)],
        scratch_shapes=[_strided_scratch(tm)],
    )(*o_br, *l_br, o_sb, x, w_dil, w_sbn, w_out_g)


def _ffn_fwd(x1, target, ffn_w, wg_g, wu_g, wd_g):
    S, D = x1.shape
    F = wg_g.shape[1]
    tm = 512
    nt = S // tm

    def body(x_ref, t_ref, nw_ref, wg_ref, wu_ref, wd_ref, h_ref, g_ref, u_ref, dy_ref, loss_ref, h_s, acc):
        j = pl.program_id(1)

        @pl.when(j == 0)
        def _():
            xv = x_ref[...]
            h = (xv * _rms(xv) * nw_ref[...]).astype(BF16)
            h_s[...] = h
            h_ref[...] = h
            acc[...] = xv

        h = h_s[...]
        g = _dot_nt(h, wg_ref[0])
        u = _dot_nt(h, wu_ref[0])
        g_ref[0] = g.astype(BF16)
        u_ref[0] = u.astype(BF16)
        a = (g * _sigmoid(g) * u).astype(BF16)
        acc[...] += _dot(a, wd_ref[0])

        @pl.when(j == N_CHIPS - 1)
        def _():
            err = acc[...] - t_ref[...]
            dy_ref[...] = err * (1.0 / D)
            loss_ref[...] = jnp.full(loss_ref.shape, jnp.sum(err * err), F32)

    row = lambda t, j: (t, 0)
    shard = lambda t, j: (j, 0, 0)
    act = lambda t, j: (j, t, 0)
    return pl.pallas_call(
        body, name="ffn_fwd", grid=(nt, N_CHIPS),
        in_specs=[pl.BlockSpec((tm, D), row), pl.BlockSpec((tm, D), row), pl.BlockSpec((1, D), lambda t, j: (0, 0))]
                 + [pl.BlockSpec((1, F, D), shard)] * 3,
        out_specs=[pl.BlockSpec((tm, D), row), pl.BlockSpec((1, tm, F), act), pl.BlockSpec((1, tm, F), act),
                   pl.BlockSpec((tm, D), row), pl.BlockSpec((1, 8, LANES), lambda t, j: (t, 0, 0))],
        out_shape=[jax.ShapeDtypeStruct((S, D), BF16), jax.ShapeDtypeStruct((N_CHIPS, S, F), BF16),
                   jax.ShapeDtypeStruct((N_CHIPS, S, F), BF16), jax.ShapeDtypeStruct((S, D), F32),
                   jax.ShapeDtypeStruct((nt, 8, LANES), F32)],
        scratch_shapes=[pltpu.VMEM((tm, D), BF16), pltpu.VMEM((tm, D), F32)],
    )(x1, target, ffn_w, wg_g, wu_g, wd_g)


def _ffn_bwd(h2, dy, g, u, wg_g, wu_g, wd_g):
    S, D = dy.shape
    F = wg_g.shape[1]
    tm = 512

    def body(h_ref, dy_ref, g_ref, u_ref, wg_ref, wu_ref, wd_ref, dwg_ref, dwu_ref, dwd_ref, dh_ref):
        t = pl.program_id(1)

        @pl.when(t == 0)
        def _():
            dwg_ref[...] = jnp.zeros_like(dwg_ref)
            dwu_ref[...] = jnp.zeros_like(dwu_ref)
            dwd_ref[...] = jnp.zeros_like(dwd_ref)

        h = h_ref[...]
        dyb = dy_ref[...].astype(BF16)
        gv = g_ref[0].astype(F32)
        uv = u_ref[0].astype(F32)
        da = _dot_nt(dyb, wd_ref[0])
        sg = _sigmoid(gv)
        silu = gv * sg
        du = (da * silu).astype(BF16)
        dg = (da * uv * (sg * (1.0 + gv * (1.0 - sg)))).astype(BF16)
        dwd_ref[0] += _dot_tn((silu * uv).astype(BF16), dyb)
        dwg_ref[0] += _dot_tn(dg, h)
        dwu_ref[0] += _dot_tn(du, h)
        dh_ref[0] = (_dot(dg, wg_ref[0]) + _dot(du, wu_ref[0])).astype(BF16)

    row = lambda j, t: (t, 0)
    shard = lambda j, t: (j, 0, 0)
    act = lambda j, t: (j, t, 0)
    return pl.pallas_call(
        body, name="ffn_bwd", grid=(N_CHIPS, S // tm),
        in_specs=[pl.BlockSpec((tm, D), row), pl.BlockSpec((tm, D), row),
                  pl.BlockSpec((1, tm, F), act), pl.BlockSpec((1, tm, F), act)] + [pl.BlockSpec((1, F, D), shard)] * 3,
        out_specs=[pl.BlockSpec((1, F, D), shard)] * 3 + [pl.BlockSpec((1, tm, D), act)],
        out_shape=[jax.ShapeDtypeStruct((N_CHIPS, F, D), F32)] * 3 + [jax.ShapeDtypeStruct((N_CHIPS, S, D), BF16)],
    )(h2, dy, g, u, wg_g, wu_g, wd_g)


def _out_proj_bwd(dh2p, dy, x1, ffn_w, w_out_g, o_dil, o_sb, w_dil, w_sbn, seg_ones, ffn_grads):
    S, D = dy.shape
    tm = 256
    ng = len(ffn_grads)

    def body(dh_ref, dy_ref, x1_ref, nw_ref, w_ref, od_ref, os_ref, wd_ref, ws_ref, g_ref, *rest):
        gin, rest = rest[:ng], rest[ng:]
        dx1_ref, dod0, dod1, dod2, dos_ref, dl0, dl1, dl2, dw_ref, dnw_ref, dwd_ref, dws_ref = rest[:12]
        gout, (scr, send, recv) = rest[12:12 + ng], rest[12 + ng:]
        i = pl.program_id(0)
        plans = [_pair_send_plan(gin[a], gout[a], send.at[a], recv.at[a]) for a in range(ng)]

        @pl.when(i == 0)
        def _():
            for start, _ in plans:
                start()

        @pl.when(i == 0)
        def _():
            for r_ in (dw_ref, dnw_ref, dwd_ref, dws_ref):
                r_[...] = jnp.zeros_like(r_)

        dh2 = _sum4(dh_ref)
        dxn, dwn = _rms_bwd(dh2, x1_ref[...], nw_ref[...])
        dnw_ref[...] += jnp.sum(dwn, axis=0, keepdims=True)
        dx1 = dy_ref[...] + dxn
        dx1_ref[...] = dx1
        dx1b = dx1.astype(BF16)
        dmix = _dot_nt(dx1b, w_ref[...])
        od = od_ref[...]
        osb = os_ref[...]
        mixed = jnp.concatenate([(od * _rms(od) * wd_ref[...]).astype(BF16),
                                 (osb * _rms(osb) * ws_ref[...]).astype(BF16)], axis=1)
        dw_ref[...] += _dot_tn(mixed, dx1b)
        do, dwo = _rms_bwd(dmix[:, :GROUP], od, wd_ref[...])
        dwd_ref[...] += jnp.sum(dwo, axis=0, keepdims=True)
        _to_strided(scr, do, list(zip(DILATIONS, (dod0, dod1, dod2))))
        _to_strided(scr, _segsum(do * od, g_ref[...]), list(zip(DILATIONS, (dl0, dl1, dl2))))
        do, dwo = _rms_bwd(dmix[:, GROUP:], osb, ws_ref[...])
        dws_ref[...] += jnp.sum(dwo, axis=0, keepdims=True)
        dos_ref[...] = do.astype(BF16)

        @pl.when(i == S // tm - 1)
        def _():
            for _, finish in plans:
                finish()

    row = lambda i: (i, 0)
    gsp = pl.BlockSpec((tm, GROUP), row)
    dsp = pl.BlockSpec((tm, D), row)
    planes = [_strided_spec(tm, r) for r in DILATIONS]
    halves = [jax.ShapeDtypeStruct((g.shape[0], g.shape[1] // 2, g.shape[2]), g.dtype) for g in ffn_grads]
    return pl.pallas_call(
        body, name="out_proj_bwd", grid=(S // tm,),
        in_specs=[pl.BlockSpec((N_CHIPS, tm, D), lambda i: (0, i, 0)), dsp, dsp, _full((1, D)), _full((2 * GROUP, D)),
                  gsp, gsp, _full((1, GROUP)), _full((1, GROUP)), _full((GROUP, GROUP // 2))] + [HBM] * ng,
        out_specs=[dsp] + planes + [gsp] + planes
                  + [_full((2 * GROUP, D)), _full((1, D)), _full((1, GROUP)), _full((1, GROUP))] + [HBM] * ng,
        out_shape=[jax.ShapeDtypeStruct((S, D), F32)] + [_strided_shape(S, r, BF16) for r in DILATIONS]
                  + [jax.ShapeDtypeStruct((S, GROUP), BF16)] + [_strided_shape(S, r, F32) for r in DILATIONS]
                  + [jax.ShapeDtypeStruct((2 * GROUP, D), F32),
                     jax.ShapeDtypeStruct((1, D), F32), jax.ShapeDtypeStruct((1, GROUP), F32),
                     jax.ShapeDtypeStruct((1, GROUP), F32)] + halves,
        scratch_shapes=[_strided_scratch(tm), pltpu.SemaphoreType.DMA((ng,)), pltpu.SemaphoreType.DMA((ng,))],
    )(dh2p, dy, x1, ffn_w, w_out_g, o_dil, o_sb, w_dil, w_sbn, seg_ones, *ffn_grads)


def _attn_in_bwd(dq_br, dk_br, dv_br, dqs, dks, dvs, qa, ka, qw, kw, cos_t, sin_t, seg_ones, h, w_in_g, x, dx1, attn_w,
                 swaps):
    S, D = x.shape
    wc = w_in_g.shape[2]
    tm = 256
    nsw = len(swaps)

    def body(q0, q1, q2, k0, k1, k2, v0, v1, v2, dqs_ref, dks_ref, dvs_ref, qa_ref, ka_ref, qw_ref, kw_ref,
             cos_ref, sin_ref, g_ref, h_ref, w_ref, x_ref, dx1_ref, aw_ref, *rest):
        gx_ref, dw_ref, daw_ref, dqw_ref, dkw_ref = rest[nsw:nsw + 5]
        accq, acck, scr = rest[2 * nsw + 5:2 * nsw + 8]
        i = pl.program_id(0)
        swap_start, swap_finish = _pair_swap_plan(rest[:nsw], rest[nsw + 5:2 * nsw + 5], *rest[2 * nsw + 8:])
        pl.when(i == 0)(swap_start)

        @pl.when(i == 0)
        def _():
            accq[...] = jnp.zeros_like(accq)
            acck[...] = jnp.zeros_like(acck)
            dw_ref[...] = jnp.zeros_like(dw_ref)
            daw_ref[...] = jnp.zeros_like(daw_ref)

        def branches(refs):
            return (_from_strided(scr, refs[0]) + _from_strided(scr, refs[1])) + _from_strided(scr, refs[2])

        g = g_ref[...]
        cos = _tile4(cos_ref[...])
        sin = _tile4(sin_ref[...])
        pieces = []
        for refs, pre_ref, w_r, acc in (((q0, q1, q2), qa_ref, qw_ref, accq), ((k0, k1, k2), ka_ref, kw_ref, acck)):
            dh = branches(refs)
            dn = dh * cos + _rot_half(dh * sin)
            pre = pre_ref[...]
            rstd = lax.rsqrt(_segsum(pre * pre, g) * (1.0 / HEAD_DIM) + EPS)
            xh = pre * rstd
            acc[...] += jnp.sum(dn * xh, axis=0, keepdims=True)
            dxh = dn * w_r[...]
            pieces.append((rstd * (dxh - xh * (_segsum(dxh * xh, g) * (1.0 / HEAD_DIM)))).astype(BF16))
        pieces += [branches((v0, v1, v2)).astype(BF16), dqs_ref[...].astype(BF16), dks_ref[...].astype(BF16),
                   dvs_ref[...].astype(BF16)]
        dproj = jnp.concatenate(pieces, axis=1)
        hv = h_ref[...]
        dh = jnp.zeros((tm, D), F32)
        for j in range(N_CHIPS):
            dp = dproj[:, j * wc:(j + 1) * wc]
            dw_ref[j] += _dot_tn(hv, dp)
            dh = dh + _dot_nt(dp, w_ref[j])
        dx, dw = _rms_bwd(dh, x_ref[...], aw_ref[...])
        daw_ref[...] += jnp.sum(dw, axis=0, keepdims=True)
        gx_ref[...] = dx1_ref[...] + dx

        @pl.when(i == S // tm - 1)
        def _():
            for acc, o_ref in ((accq, dqw_ref), (acck, dkw_ref)):
                a = acc[...]
                pair = (a[:, 0:LANES] + a[:, LANES:2 * LANES]) + (a[:, 2 * LANES:3 * LANES] + a[:, 3 * LANES:4 * LANES])
                o_ref[...] = pair + pltpu.roll(pair, HEAD_DIM, 1)
            swap_finish()

    row = lambda i: (i, 0)
    gsp = pl.BlockSpec((tm, GROUP), row)
    dsp = pl.BlockSpec((tm, D), row)
    tab = pl.BlockSpec((tm, LANES), row)
    planes = [_strided_spec(tm, r) for r in DILATIONS]
    return pl.pallas_call(
        body, name="attn_in_bwd", grid=(S // tm,),
        in_specs=planes * 3 + [gsp] * 5 + [_full((1, GROUP)), _full((1, GROUP)), tab, tab, _full((GROUP, GROUP // 2)),
                                          dsp, _full((N_CHIPS, D, wc)), dsp, dsp, _full((1, D))] + [HBM] * nsw,
        out_specs=[dsp, _full((N_CHIPS, D, wc)), _full((1, D)), _full((1, LANES)), _full((1, LANES))] + [HBM] * nsw,
        out_shape=[jax.ShapeDtypeStruct((S, D), F32), jax.ShapeDtypeStruct((N_CHIPS, D, wc), F32),
                   jax.ShapeDtypeStruct((1, D), F32), jax.ShapeDtypeStruct((1, LANES), F32),
                   jax.ShapeDtypeStruct((1, LANES), F32)] + [jax.ShapeDtypeStruct(s.shape, s.dtype) for s in swaps],
        scratch_shapes=[pltpu.VMEM((1, GROUP), F32), pltpu.VMEM((1, GROUP), F32), _strided_scratch(tm),
                        pltpu.SemaphoreType.DMA((nsw,)), pltpu.SemaphoreType.DMA((nsw,))],
    )(*dq_br, *dk_br, *dv_br, dqs, dks, dvs, qa, ka, qw, kw, cos_t, sin_t, seg_ones, h, w_in_g, x, dx1, attn_w, *swaps)


def _constants(S):
    pos = jnp.arange(S, dtype=F32)
    inv_freq = ROPE_THETA ** (-jnp.arange(0, HEAD_DIM, 2, dtype=F32) / HEAD_DIM)
    ang = pos[:, None] * inv_freq[None, :]
    cos, sin = jnp.cos(ang), jnp.sin(ang)
    cos_t = jnp.concatenate([cos, cos] * 2, axis=1)
    sin_t = jnp.concatenate([-sin, sin] * 2, axis=1)
    idx = jnp.arange(GROUP // 2)
    seg_ones = (idx[:, None] // HEAD_DIM == idx[None, :] // HEAD_DIM).astype(BF16)
    seg_ones = jnp.concatenate([seg_ones, seg_ones], axis=0)
    r = jnp.arange(BLOCK)
    ones = jnp.ones((BLOCK, BLOCK), BF16)
    tris = [jnp.concatenate([jnp.concatenate([m.astype(BF16), ones], axis=1)] * 2, axis=0) for m in
            (r[:, None] > r[None, :],
             r[:, None] <= r[None, :],
             r[:, None] < r[None, :])]
    return cos_t, sin_t, seg_ones, tris


FFN_NAMES = ("w_gate", "w_up", "w_down")


def _device_step(x, target, attn_w, qn_w, kn_w, dil_w, sbn_w, ffn_w, w_in_g, w_out_slots, ffn_slots, core, chip):
    S = x.shape[0]
    cos_t, sin_t, seg_ones, (tri_later, tri_upto, tri_before) = _constants(S)
    reps = GROUP // HEAD_DIM
    qw = jnp.tile(qn_w, (1, reps))
    kw = jnp.tile(kn_w, (1, reps))

    nd = len(DILATIONS)
    h, qa, ka, *rest, wg_g = _in_proj_fwd(x, attn_w, w_in_g, qw, kw, cos_t, sin_t, seg_ones, ffn_slots[:1])
    qh, kh, va, (qs, ks, vs) = rest[:nd], rest[nd:2 * nd], rest[2 * nd:3 * nd], rest[3 * nd:]
    hosted = ([], [w_out_slots], [ffn_slots[2]])
    branches = [_dil_fwd(qh[b], kh[b], va[b], hosted[b]) for b in range(nd)]
    w_out_g, wd_g = branches[1][2].reshape(-1, x.shape[1]), branches[2][2]
    o_sb, ltot, walked, wu_g = _sb_fwd(qs, ks, vs, tri_later, ffn_slots[1:2])
    o_dil, *lse, x1 = _out_proj_fwd([b[0] for b in branches], [b[1] for b in branches], o_sb, x, dil_w, sbn_w, w_out_g)
    h2, g, u, dy, loss_parts = _ffn_fwd(x1, target, ffn_w, wg_g, wu_g, wd_g)

    *ffn_grads, dh2p = _ffn_bwd(h2, dy, g, u, wg_g, wu_g, wd_g)
    dx1, *mid, dw_out, dffn_w, ddil_w, dsbn_w, p0, p1, p2 = _out_proj_bwd(
        dh2p, dy, x1, ffn_w, w_out_g, o_dil, o_sb, dil_w, sbn_w, seg_ones, ffn_grads)
    do_dil, do_sb, delta = mid[:nd], mid[nd], mid[nd + 1:]
    parts = [_pair_sum(gr, fr, core, n) for gr, fr, n in zip(ffn_grads, (p0, p1, p2), FFN_NAMES)]
    dqs, dks, dvs = _sb_bwd(walked[:, 0, 0], qs, ks, vs, do_sb, ltot, tri_upto, tri_before)
    dbr = [_dil_bwd(qh[b], kh[b], va[b], do_dil[b], lse[b], delta[b], parts[b]) for b in range(nd)]
    ffn_halves = [_chip_sum(dbr[b][3], parts[b], chip, FFN_NAMES[b]) for b in range(nd)]
    grad_x, dw_in, dattn_w, dqw, dkw, *ffn_others = _attn_in_bwd(
        [b[0] for b in dbr], [b[1] for b in dbr], [b[2] for b in dbr], dqs, dks, dvs,
        qa, ka, qw, kw, cos_t, sin_t, seg_ones, h, w_in_g, x, dx1, attn_w, ffn_halves)
    small = dict(attn=dattn_w, q=dqw[:, :HEAD_DIM], k=dkw[:, :HEAD_DIM], dil=ddil_w, sb=dsbn_w, ffn=dffn_w)
    return loss_parts, grad_x, small, dw_in, dw_out, ffn_halves, ffn_others


HBM = pl.BlockSpec(memory_space=pltpu.HBM)
VMEM = pl.BlockSpec(memory_space=pltpu.VMEM)
CHIP_FLIPS = ((1, 0), (0, 1), (1, 1))


def _place():
    return lax.axis_index("x"), lax.axis_index("y"), lax.axis_index("c")


def _flip(v, d):
    return 1 - v if d else v


def _half_rows(c, n):
    return pl.ds(pl.multiple_of(c * (n // 2), 16), n // 2)


def _gather_plan(slot_in, slot_out, send, recv):
    x, y, c = _place()
    p = 2 * x + y
    chips = [(_flip(x, dx), _flip(y, dy)) for dx, dy in CHIP_FLIPS]
    mine, other = _half_rows(c, slot_in.shape[1]), _half_rows(1 - c, slot_in.shape[1])

    def copy(k, src, dst, to):
        return pltpu.make_async_remote_copy(src_ref=src, dst_ref=dst, send_sem=send.at[k], recv_sem=recv.at[k],
                                            device_id=to, device_id_type=MESH)

    def first(k):
        return copy(k, slot_in.at[p, mine], slot_out.at[p, mine], (*chips[k], c))

    def passed(k, rows):
        land = slot_out.at[2 * chips[k][0] + chips[k][1], rows]
        return copy(3 + k, land, land, (x, y, 1 - c))

    def start():
        for k in range(3):
            first(k).start()

    def forward():
        for k in range(3):
            land = slot_out.at[2 * chips[k][0] + chips[k][1], mine]
            copy(k, land, land, (*chips[k], c)).wait_recv()
            passed(k, mine).start()

    def finish():
        for k in range(3):
            passed(k, other).wait_recv()
        for k in range(3):
            first(k).wait_send()
            passed(k, mine).wait_send()

    return start, forward, finish


def _chip_send_plan(part_in, recv_out, send, recv):
    x, y, c = _place()
    p = 2 * x + y
    chips = [(_flip(x, dx), _flip(y, dy)) for dx, dy in CHIP_FLIPS]

    def copy(k):
        q = 2 * chips[k][0] + chips[k][1]
        return pltpu.make_async_remote_copy(src_ref=part_in.at[q], dst_ref=recv_out.at[p], send_sem=send.at[k],
                                            recv_sem=recv.at[k], device_id=(*chips[k], c), device_id_type=MESH)

    def start():
        for k in range(3):
            copy(k).start()

    def finish():
        for k in range(3):
            land = recv_out.at[2 * chips[k][0] + chips[k][1]]
            pltpu.make_async_remote_copy(src_ref=land, dst_ref=land, send_sem=send.at[k], recv_sem=recv.at[k],
                                         device_id=(*chips[k], c), device_id_type=MESH).wait_recv()
        for k in range(3):
            copy(k).wait_send()

    return start, finish


def _pair_send_plan(grad_in, recv_out, send, recv):
    x, y, c = _place()

    def copy():
        theirs = _half_rows(1 - c, grad_in.shape[1])
        return pltpu.make_async_remote_copy(src_ref=grad_in.at[:, theirs, :], dst_ref=recv_out, send_sem=send,
                                            recv_sem=recv, device_id=(x, y, 1 - c), device_id_type=MESH)

    return (lambda: copy().start()), (lambda: copy().wait())


def _own_slots(shard):
    here = 2 * lax.axis_index("x") + lax.axis_index("y")
    return lax.dynamic_update_slice(lax.empty((N_CHIPS,) + shard.shape, shard.dtype), shard[None], (here, 0, 0))


def _gather_weights(shards):
    n = len(shards)

    def body(*refs):
        ins, outs = refs[:n], refs[n:2 * n]
        send, recv = refs[2 * n:]
        plans = [_gather_plan(ins[a], outs[a], send.at[pl.ds(6 * a, 6)], recv.at[pl.ds(6 * a, 6)]) for a in range(n)]
        for stage in range(3):
            for plan in plans:
                plan[stage]()

    slots = [_own_slots(s) for s in shards]
    return pl.pallas_call(
        body, name="gather_weights", in_specs=[HBM] * n, out_specs=[HBM] * n,
        out_shape=[jax.ShapeDtypeStruct(s.shape, s.dtype) for s in slots],
        input_output_aliases={a: a for a in range(n)},
        scratch_shapes=[pltpu.SemaphoreType.DMA((6 * n,)), pltpu.SemaphoreType.DMA((6 * n,))],
    )(*slots)


def _pair_exchange(grads, small):
    n = len(grads)

    def body(*refs):
        gin, sm = refs[:n], refs[n]
        gout, sm_all = refs[n + 1:2 * n + 1], refs[2 * n + 1]
        send, recv = refs[2 * n + 2:]
        x, y, c = _place()
        me = 4 * x + 2 * y + c
        big = [_pair_send_plan(gin[a], gout[a], send.at[a], recv.at[a]) for a in range(n)]
        for start, _ in big:
            start()
        sm_all[pl.ds(me, 1)] = sm[...][None]
        tiny = []
        for k in range(1, N_DEV):
            px, py, pc = _flip(x, k & 4), _flip(y, k & 2), _flip(c, k & 1)
            tiny.append((pltpu.make_async_remote_copy(
                src_ref=sm, dst_ref=sm_all.at[me], send_sem=send.at[n + k - 1], recv_sem=recv.at[n + k - 1],
                device_id=(px, py, pc), device_id_type=MESH), 4 * px + 2 * py + pc))
            tiny[-1][0].start()
        for k, (cp, peer) in enumerate(tiny):
            pltpu.make_async_remote_copy(src_ref=sm, dst_ref=sm_all.at[peer], send_sem=send.at[n + k],
                                         recv_sem=recv.at[n + k], device_id=(x, y, c),
                                         device_id_type=MESH).wait_recv()
            cp.wait_send()
        for _, finish in big:
            finish()

    halves = [jax.ShapeDtypeStruct((g.shape[0], g.shape[1] // 2, g.shape[2]), g.dtype) for g in grads]
    return pl.pallas_call(
        body, name="pair_exchange", in_specs=[HBM] * n + [VMEM], out_specs=[HBM] * n + [VMEM],
        out_shape=halves + [jax.ShapeDtypeStruct((N_DEV,) + small.shape, small.dtype)],
        scratch_shapes=[pltpu.SemaphoreType.DMA((n + N_DEV - 1,)), pltpu.SemaphoreType.DMA((n + N_DEV - 1,))],
    )(*grads, small)


def _chip_exchange(parts):
    n = len(parts)

    def body(*refs):
        pin, pout = refs[:n], refs[n:2 * n]
        send, recv = refs[2 * n:]
        plans = [_chip_send_plan(pin[a], pout[a], send.at[pl.ds(3 * a, 3)], recv.at[pl.ds(3 * a, 3)]) for a in range(n)]
        for stage in range(2):
            for plan in plans:
                plan[stage]()

    return pl.pallas_call(
        body, name="chip_exchange", in_specs=[HBM] * n, out_specs=[HBM] * n,
        out_shape=[jax.ShapeDtypeStruct(s.shape, s.dtype) for s in parts],
        scratch_shapes=[pltpu.SemaphoreType.DMA((3 * n,)), pltpu.SemaphoreType.DMA((3 * n,))],
    )(*parts)


def _pair_swap_plan(hin, hout, send, recv):
    x, y, c = _place()

    def copies():
        return [pltpu.make_async_remote_copy(src_ref=hin[a], dst_ref=hout[a], send_sem=send.at[a], recv_sem=recv.at[a],
                                             device_id=(x, y, 1 - c), device_id_type=MESH) for a in range(len(hin))]

    def start():
        for cp in copies():
            cp.start()

    def finish():
        for cp in copies():
            cp.wait()

    return start, finish


def _pair_swap(halves):
    n = len(halves)

    def body(*refs):
        start, finish = _pair_swap_plan(refs[:n], refs[n:2 * n], *refs[2 * n:])
        start()
        finish()

    return pl.pallas_call(
        body, name="pair_swap", in_specs=[HBM] * n, out_specs=[HBM] * n,
        out_shape=[jax.ShapeDtypeStruct(s.shape, s.dtype) for s in halves],
        scratch_shapes=[pltpu.SemaphoreType.DMA((n,)), pltpu.SemaphoreType.DMA((n,))],
    )(*halves)


def _pair_sum(grad, recv, c, tag):
    _, R, C = grad.shape
    hr = R // 2

    def body(c_ref, a_ref, b_ref, o_ref):
        o_ref[...] = (a_ref[...] + b_ref[...]).astype(BF16)

    return pl.pallas_call(
        body, name="pair_sum_" + tag,
        grid_spec=pltpu.PrefetchScalarGridSpec(
            num_scalar_prefetch=1, grid=(N_CHIPS,),
            in_specs=[pl.BlockSpec((1, hr, C), lambda s, cr: (s, cr[0], 0)),
                      pl.BlockSpec((1, hr, C), lambda s, cr: (s, 0, 0))],
            out_specs=pl.BlockSpec((1, hr, C), lambda s, cr: (s, 0, 0))),
        out_shape=jax.ShapeDtypeStruct((N_CHIPS, hr, C), BF16),
    )(c, grad, recv)


def _chip_sum(received, own, chip, tag):
    _, rows, C = received.shape
    tr = rows // 2

    def body(chip_ref, own_ref, r1_ref, r2_ref, r3_ref, o_ref):
        p = [r[0].astype(F32) for r in (own_ref, r1_ref, r2_ref, r3_ref)]
        o_ref[...] = (p[0] + p[1]) + (p[2] + p[3])

    def slot(k):
        return pl.BlockSpec((1, tr, C), lambda i, cr: (jnp.bitwise_xor(cr[0], k), i, 0))

    return pl.pallas_call(
        body, name="chip_sum_" + tag,
        grid_spec=pltpu.PrefetchScalarGridSpec(
            num_scalar_prefetch=1, grid=(rows // tr,), in_specs=[slot(0), slot(1), slot(2), slot(3)],
            out_specs=pl.BlockSpec((tr, C), lambda i, cr: (i, 0))),
        out_shape=jax.ShapeDtypeStruct((rows, C), F32),
    )(chip, own, received, received, received)


def _adamw_math(w, g, m, v):
    m = ADAM_B1 * m + (1.0 - ADAM_B1) * g
    v = ADAM_B2 * v + (1.0 - ADAM_B2) * (g * g)
    m_hat = m / (1.0 - ADAM_B1 ** ADAM_STEP)
    v_hat = v / (1.0 - ADAM_B2 ** ADAM_STEP)
    delta = -ADAM_LR * (m_hat / (jnp.sqrt(v_hat) + ADAM_EPS) + ADAM_WD * w)
    return delta, m, v


def _adamw(ws, g_mines, g_others, ms, vs, c, tag, nblk=2, parts=()):
    n, npart = len(ws), len(parts)
    R, C = ws[0].shape
    tr = R // (2 * nblk)

    def body(c_ref, *refs):
        ins, part_in = refs[:5 * n], refs[5 * n:5 * n + npart]
        outs, part_out = refs[5 * n + npart:9 * n + npart], refs[9 * n + npart:9 * n + 2 * npart]
        h, i = pl.program_id(0), pl.program_id(1)
        if npart:
            send, recv = refs[9 * n + 2 * npart:]
            plans = [_chip_send_plan(part_in[a], part_out[a], send.at[pl.ds(3 * a, 3)], recv.at[pl.ds(3 * a, 3)])
                     for a in range(npart)]

            @pl.when(jnp.logical_and(h == 0, i == 0))
            def _():
                for start, _ in plans:
                    start()

        for a in range(n):
            w_ref, gm_ref, go_ref, m_ref, v_ref = ins[5 * a:5 * a + 5]
            g_ref, d_ref, nm_ref, nv_ref = outs[4 * a:4 * a + 4]
            g = jnp.where(h == c_ref[0], gm_ref[...], go_ref[...])
            g_ref[...] = g
            d_ref[...], nm_ref[...], nv_ref[...] = _adamw_math(w_ref[...], g, m_ref[...], v_ref[...])

        if npart:
            @pl.when(jnp.logical_and(h == 1, i == nblk - 1))
            def _():
                for _, finish in plans:
                    finish()

    blk = pl.BlockSpec((tr, C), lambda h, i, cr: (nblk * h + i, 0))
    half = pl.BlockSpec((tr, C), lambda h, i, cr: (i, 0))
    args = [t for a in range(n) for t in (ws[a], g_mines[a], g_others[a], ms[a], vs[a])]
    res = pl.pallas_call(
        body, name="adamw_" + tag,
        grid_spec=pltpu.PrefetchScalarGridSpec(
            num_scalar_prefetch=1, grid=(2, nblk), in_specs=[blk, half, half, blk, blk] * n + [HBM] * npart,
            out_specs=[blk] * (4 * n) + [HBM] * npart,
            scratch_shapes=[pltpu.SemaphoreType.DMA((3 * npart,))] * 2 if npart else []),
        out_shape=[jax.ShapeDtypeStruct((R, C), F32)] * (4 * n)
                  + [jax.ShapeDtypeStruct(p.shape, p.dtype) for p in parts],
    )(c, *args, *parts)
    return [res[4 * a:4 * a + 4] for a in range(n)], list(res[4 * n:])


def _small_update(all_small, w, m, v):
    def body(a_ref, w_ref, m_ref, v_ref, g_ref, d_ref, nm_ref, nv_ref):
        g = ((a_ref[0] + a_ref[1]) + (a_ref[2] + a_ref[3])) + ((a_ref[4] + a_ref[5]) + (a_ref[6] + a_ref[7]))
        g_ref[...] = g
        d_ref[...], nm_ref[...], nv_ref[...] = _adamw_math(w_ref[...], g, m_ref[...], v_ref[...])

    return pl.pallas_call(
        body, name="small_update", out_shape=[jax.ShapeDtypeStruct(w.shape, F32)] * 4,
    )(all_small, w, m, v)


SMALL_ROWS = (("attn", 0, 0), ("ffn", 1, 0), ("dil", 2, 0), ("sb", 2, GROUP), ("q", 3, 0), ("k", 3, HEAD_DIM),
              ("loss", 4, 0))


def _pack_small(vals, D):
    rows = [jnp.zeros((1, D), F32) for _ in range(8)]
    for name, r, off in SMALL_ROWS:
        if name in vals:
            rows[r] = lax.dynamic_update_slice(rows[r], vals[name].astype(F32), (0, off))
    return jnp.concatenate(rows, axis=0)


def _unpack_small(packed, vals):
    return {name: packed[r:r + 1, off:off + vals[name].shape[1]] for name, r, off in SMALL_ROWS if name in vals}


def kernel(x, attn_norm_w, w_in, q_norm_w, k_norm_w, dil_out_norm_w, sb_out_norm_w, w_out, ffn_norm_w, w_gate, w_up, w_down, loss_target, m_attn_norm_w, m_w_in, m_q_norm_w, m_k_norm_w, m_dil_out_norm_w, m_sb_out_norm_w, m_w_out, m_ffn_norm_w, m_w_gate, m_w_up, m_w_down, v_attn_norm_w, v_w_in, v_q_norm_w, v_k_norm_w, v_dil_out_norm_w, v_sb_out_norm_w, v_w_out, v_ffn_norm_w, v_w_gate, v_w_up, v_w_down):
    D = x.shape[-1]
    big_names = ("w_in", "w_out", "w_gate", "w_up", "w_down")
    flipped = ("w_gate", "w_up")
    tr = lambda a: jnp.swapaxes(a[0], 0, 1)
    big_w = dict(w_in=w_in[0], w_out=w_out[0], w_gate=tr(w_gate), w_up=tr(w_up), w_down=w_down[0])
    big_m = dict(w_in=m_w_in[0], w_out=m_w_out[0], w_gate=tr(m_w_gate), w_up=tr(m_w_up), w_down=m_w_down[0])
    big_v = dict(w_in=v_w_in[0], w_out=v_w_out[0], w_gate=tr(v_w_gate), w_up=tr(v_w_up), w_down=v_w_down[0])
    small_w = dict(attn=attn_norm_w, q=q_norm_w, k=k_norm_w, dil=dil_out_norm_w, sb=sb_out_norm_w, ffn=ffn_norm_w)
    small_m = dict(attn=m_attn_norm_w, q=m_q_norm_w, k=m_k_norm_w, dil=m_dil_out_norm_w, sb=m_sb_out_norm_w,
                   ffn=m_ffn_norm_w)
    small_v = dict(attn=v_attn_norm_w, q=v_q_norm_w, k=v_k_norm_w, dil=v_dil_out_norm_w, sb=v_sb_out_norm_w,
                   ffn=v_ffn_norm_w)

    c = lax.axis_index("c").astype(jnp.int32).reshape(1)
    chip = (2 * lax.axis_index("x") + lax.axis_index("y")).astype(jnp.int32).reshape(1)
    (w_in_g,) = _gather_weights([big_w["w_in"].astype(BF16)])
    w_out_slots = _own_slots(big_w["w_out"].astype(BF16))
    ffn_slots = [_own_slots(big_w[n].astype(BF16)) for n in FFN_NAMES]

    loss_parts, grad_x, small_g, dw_in, dw_out, ffn_halves, ffn_others = _device_step(
        x[0], loss_target[0], attn_norm_w, q_norm_w, k_norm_w, dil_out_norm_w, sb_out_norm_w, ffn_norm_w,
        w_in_g, w_out_slots, ffn_slots, c, chip)
    small_g["loss"] = (jnp.sum(loss_parts[:, 0, 0]) * (0.5 / D)).reshape(1, 1)

    late = [dw_in, dw_out.reshape(N_CHIPS, -1, D)]
    *from_pair, all_small = _pair_exchange(late, _pack_small(small_g, D))
    chip_parts = [_pair_sum(g, r, c, n) for g, r, n in zip(late, from_pair, big_names)]
    ffn_out, from_chips = _adamw([big_w[n] for n in FFN_NAMES], ffn_halves, ffn_others,
                                 [big_m[n] for n in FFN_NAMES], [big_v[n] for n in FFN_NAMES], c, "ffn",
                                 nblk=4, parts=chip_parts)
    big_out = dict(zip(FFN_NAMES, ffn_out))
    halves = [_chip_sum(r, p, chip, n) for r, p, n in zip(from_chips, chip_parts, big_names)]
    others = _pair_swap(halves)
    for n, mine, other in zip(big_names, halves, others):
        big_out[n] = _adamw([big_w[n]], [mine], [other], [big_m[n]], [big_v[n]], c, n)[0][0]
    sg, sd, sm, sv = _small_update(all_small, _pack_small(small_w, D), _pack_small(small_m, D),
                                   _pack_small(small_v, D))
    small_out = [_unpack_small(t, small_w) for t in (sg, sd, sm, sv)]

    order = (("attn", None), (None, "w_in"), ("q", None), ("k", None), ("dil", None), ("sb", None),
             (None, "w_out"), ("ffn", None), (None, "w_gate"), (None, "w_up"), (None, "w_down"))
    outs = [sg[4, 0], grad_x[None]]
    for kind in range(4):
        for s_name, b_name in order:
            if s_name is not None:
                outs.append(small_out[kind][s_name])
            else:
                res = big_out[b_name][kind]
                outs.append((jnp.swapaxes(res, 0, 1) if b_name in flipped else res)[None])
    return tuple(outs)
```

```python
import functools

import jax
import jax.numpy as jnp
from jax import lax
from jax.experimental import pallas as pl
from jax.experimental.pallas import tpu as pltpu

F32 = jnp.float32
BF16 = jnp.bfloat16
MESH = pl.DeviceIdType.MESH

HEAD_DIM = 64
GROUP = 512
BLOCK = 128
LANES = 128
N_CHIPS = 4
N_DEV = 8
EPS = 1e-6
ROPE_THETA = 10000.0
DILATIONS = (1, 4, 16)
NEG = -1e30

ADAM_LR = 0.001
ADAM_B1 = 0.9
ADAM_B2 = 0.999
ADAM_EPS = 1e-08
ADAM_WD = 0.01
ADAM_STEP = 10


def _dot(a, b):
    return jnp.dot(a, b, preferred_element_type=F32)


def _dot_nt(a, b):
    return lax.dot_general(a, b, (((1,), (1,)), ((), ())), preferred_element_type=F32)


def _dot_tn(a, b):
    return lax.dot_general(a, b, (((0,), (0,)), ((), ())), preferred_element_type=F32)


def _split(v):
    hi = lax.bitcast_convert_type(lax.bitcast_convert_type(v, jnp.uint32) & jnp.uint32(0xFFFF0000), F32)
    return hi.astype(BF16), (v - hi).astype(BF16)


def _segsum(v, g):
    hi, lo = _split(v)
    w = g.shape[1]
    return jnp.concatenate([_dot(jnp.concatenate([hi[:, c:c + w], lo[:, c:c + w]], axis=1), g)
                            for c in range(0, v.shape[1], w)], axis=1)


def _rot_half(x):
    outs = []
    for c in range(x.shape[1] // LANES):
        xc = x[:, c * LANES:(c + 1) * LANES]
        lane = lax.broadcasted_iota(jnp.int32, xc.shape, 1)
        first = (lane % HEAD_DIM) < (HEAD_DIM // 2)
        outs.append(jnp.where(first, pltpu.roll(xc, LANES - 32, 1), pltpu.roll(xc, 32, 1)))
    return outs[0] if len(outs) == 1 else jnp.concatenate(outs, axis=1)


def _rms(x):
    return lax.rsqrt(jnp.mean(x * x, axis=-1, keepdims=True) + EPS)


def _rms_bwd(dy, x, w):
    rstd = _rms(x)
    xh = x * rstd
    dxh = dy * w
    dx = rstd * (dxh - xh * jnp.mean(dxh * xh, axis=-1, keepdims=True))
    return dx, dy * xh


def _sigmoid(x):
    return 1.0 / (1.0 + jnp.exp(-x))


def _sum4(ref):
    p = [ref[j].astype(F32) for j in range(N_CHIPS)]
    return (p[0] + p[1]) + (p[2] + p[3])


def _full(shape):
    n = len(shape)
    return pl.BlockSpec(shape, lambda *_: (0,) * n)


def _strided_spec(tm, r):
    return pl.BlockSpec((r, tm // r, GROUP), lambda i: (0, i, 0))


def _strided_shape(S, r, dtype):
    return jax.ShapeDtypeStruct((r, S // r, GROUP), dtype)


def _to_strided(scr, val, outs):
    chunks = range(GROUP // LANES)
    for k in chunks:
        scr[k] = val[:, _lanes(k)]
    for r, o_ref in outs:
        if r == 1:
            o_ref[0] = val.astype(o_ref.dtype)
            continue
        n = val.shape[0] // r
        for c in range(r):
            rows = pl.ds(c, n, stride=r)
            o_ref[c] = jnp.concatenate([scr.at[k][rows, :] for k in chunks], axis=1).astype(o_ref.dtype)


def _from_strided(scr, ref):
    r, n, _ = ref.shape
    if r == 1:
        return ref[0].astype(F32)
    chunks = range(GROUP // LANES)
    for c in range(r):
        plane = ref[c].astype(F32)
        for k in chunks:
            scr.at[k][pl.ds(c, n, stride=r), :] = plane[:, _lanes(k)]
    return jnp.concatenate([scr[k] for k in chunks], axis=1)


def _strided_scratch(tm):
    return pltpu.VMEM((GROUP // LANES, tm, LANES), F32)


def _tile4(t):
    return jnp.concatenate([t] * (GROUP // LANES), axis=1)


def _in_proj_fwd(x, attn_w, w_in_g, qw, kw, cos_t, sin_t, seg_ones, slots):
    S, D = x.shape
    tm = 512
    wcols = w_in_g.shape[2]
    nd = len(DILATIONS)
    ns = len(slots)

    def body(x_ref, aw_ref, w_ref, qw_ref, kw_ref, cos_ref, sin_ref, g_ref, *rest):
        slot_in, (h_ref, qa_ref, ka_ref), rest = rest[:ns], rest[ns:ns + 3], rest[ns + 3:]
        q_refs, k_refs, v_refs = rest[:nd], rest[nd:2 * nd], rest[2 * nd:3 * nd]
        qs_ref, ks_ref, vs_ref = rest[3 * nd:3 * nd + 3]
        slot_out, scr, sems = rest[3 * nd + 3:3 * nd + 3 + ns], rest[3 * nd + 3 + ns], rest[3 * nd + 4 + ns:]
        finish = _hosted_gathers(slot_in, slot_out, *sems, pl.program_id(0), S // tm) if ns else None
        xv = x_ref[...]
        h = (xv * _rms(xv) * aw_ref[...]).astype(BF16)
        h_ref[...] = h
        proj = jnp.concatenate([_dot(h, w_ref[j]) for j in range(N_CHIPS)], axis=1)
        qa = proj[:, 0 * GROUP:1 * GROUP]
        ka = proj[:, 1 * GROUP:2 * GROUP]
        qa_ref[...] = qa
        ka_ref[...] = ka
        _to_strided(scr, proj[:, 2 * GROUP:3 * GROUP], list(zip(DILATIONS, v_refs)))
        qs_ref[...] = proj[:, 3 * GROUP:4 * GROUP].astype(BF16)
        ks_ref[...] = proj[:, 4 * GROUP:5 * GROUP].astype(BF16)
        vs_ref[...] = proj[:, 5 * GROUP:6 * GROUP].astype(BF16)
        g = g_ref[...]
        cos = _tile4(cos_ref[...])
        sin = _tile4(sin_ref[...])
        for t, w_r, o_rs in ((qa, qw_ref, q_refs), (ka, kw_ref, k_refs)):
            rstd = lax.rsqrt(_segsum(t * t, g) * (1.0 / HEAD_DIM) + EPS)
            tn = t * rstd * w_r[...]
            _to_strided(scr, tn * cos + _rot_half(tn) * sin, list(zip(DILATIONS, o_rs)))
        if finish is not None:
            finish()

    row = lambda i: (i, 0)
    tile = lambda n, dt: jax.ShapeDtypeStruct((S, n), dt)
    planes = [_strided_spec(tm, r) for r in DILATIONS]
    h_in, h_out, h_shape, h_sems = _hosted_specs(slots)
    n_out = 6 + 3 * nd
    return pl.pallas_call(
        body, name="in_proj_fwd", grid=(S // tm,),
        in_specs=[pl.BlockSpec((tm, D), row), _full((1, D)), _full((N_CHIPS, D, wcols)),
                  _full((1, GROUP)), _full((1, GROUP)),
                  pl.BlockSpec((tm, LANES), row), pl.BlockSpec((tm, LANES), row),
                  _full((GROUP, GROUP // 2))] + h_in,
        out_specs=[pl.BlockSpec((tm, D), row)] + [pl.BlockSpec((tm, GROUP), row)] * 2 + planes * 3
                  + [pl.BlockSpec((tm, GROUP), row)] * 3 + h_out,
        out_shape=[tile(D, BF16), tile(GROUP, F32), tile(GROUP, F32)]
                  + [_strided_shape(S, r, BF16) for r in DILATIONS] * 3 + [tile(GROUP, BF16)] * 3 + h_shape,
        input_output_aliases={8 + a: n_out + a for a in range(ns)},
        scratch_shapes=[_strided_scratch(tm)] + h_sems,
    )(x, attn_w, w_in_g, qw, kw, cos_t, sin_t, seg_ones, *slots)


DIL_PLANES = 1


def _dil_fwd(q, k, v, slots):
    r, L, _ = q.shape
    nb = L // BLOCK
    P = GROUP // LANES
    PL = min(DIL_PLANES, r)
    ns = len(slots)
    units = [(pp, hp) for pp in range(PL) for hp in range(P)]

    def body(q_ref, kc_ref, kp_ref, vc_ref, vp_ref, *rest):
        o_ref, l_ref = rest[ns:ns + 2]
        n = pl.program_id(1)
        finish = (_hosted_gathers(rest[:ns], rest[ns + 2:2 * ns + 2], *rest[2 * ns + 2:],
                                  pl.program_id(0) * nb + n, (r // PL) * nb) if ns else None)
        rowi = lax.broadcasted_iota(jnp.int32, (BLOCK, BLOCK), 0)
        coli = lax.broadcasted_iota(jnp.int32, (BLOCK, BLOCK), 1)
        first = coli < HEAD_DIM
        masks = (coli <= rowi, jnp.logical_and(coli >= rowi, n > 0))
        s2 = {}
        for pp, hp in units:
            q2 = _scaled(q_ref[pp, :, _lanes(hp)])
            for b, k_ref in enumerate((kc_ref, kp_ref)):
                s2[pp, hp, b] = _dot_nt(q2, _by_head(k_ref[pp, :, _lanes(hp)], first))
        ps, inv, lse = {}, {}, {}
        for pp, hp in units:
            for h in range(2):
                s = [jnp.where(masks[b], s2[pp, hp, b][:, h * BLOCK:(h + 1) * BLOCK], NEG) for b in range(2)]
                m = jnp.maximum(jnp.max(s[0], axis=1, keepdims=True), jnp.max(s[1], axis=1, keepdims=True))
                p = [jnp.exp(s[b] - m) for b in range(2)]
                den = jnp.sum(p[0], axis=1, keepdims=True) + jnp.sum(p[1], axis=1, keepdims=True)
                ps[pp, hp, h] = [p[b].astype(BF16) for b in range(2)]
                inv[pp, hp, h] = 1.0 / den
                lse[pp, hp, h] = m + jnp.log(den)
        for pp, hp in units:
            o = jnp.zeros((BLOCK, LANES), F32)
            for b, v_ref in enumerate((vc_ref, vp_ref)):
                o = o + _dot(jnp.concatenate([ps[pp, hp, 0][b], ps[pp, hp, 1][b]], axis=1),
                             _by_head(v_ref[pp, :, _lanes(hp)], first))
            o_ref[pp, :, _lanes(hp)] = o * jnp.where(first, inv[pp, hp, 0], inv[pp, hp, 1])
            l_ref[pp, :, _lanes(hp)] = jnp.where(first, lse[pp, hp, 0], lse[pp, hp, 1])
        if finish is not None:
            finish()

    cur = pl.BlockSpec((PL, BLOCK, GROUP), lambda c, n: (c, n, 0))
    prev = pl.BlockSpec((PL, BLOCK, GROUP), lambda c, n: (c, jnp.maximum(n - 1, 0), 0))
    h_in, h_out, h_shape, h_sems = _hosted_specs(slots)
    return pl.pallas_call(
        body, name="dil_fwd_r%d" % r, grid=(r // PL, nb),
        in_specs=[cur, cur, prev, cur, prev] + h_in, out_specs=[cur, cur] + h_out,
        out_shape=[jax.ShapeDtypeStruct(q.shape, F32)] * 2 + h_shape,
        input_output_aliases={5 + a: 2 + a for a in range(ns)},
        scratch_shapes=h_sems,
    )(q, k, k, v, v, *slots)


def _dil_bwd(q, k, v, do, lse, delta, part):
    r, L, _ = q.shape
    nb = L // BLOCK
    P = GROUP // LANES
    PL = min(DIL_PLANES, r)
    scale = HEAD_DIM ** -0.5
    units = [(pp, hp) for pp in range(PL) for hp in range(P)]

    def body(qc_ref, qn_ref, doc_ref, don_ref, lc_ref, ln_ref, dc_ref, dn_ref, k_ref, v_ref, part_in,
             dq_ref, dk_ref, dv_ref, part_out, carry, send, recv):
        j = pl.program_id(1)
        step = pl.program_id(0) * nb + j
        start, finish = _chip_send_plan(part_in, part_out, send, recv)
        pl.when(step == 0)(start)
        rowi = lax.broadcasted_iota(jnp.int32, (BLOCK, BLOCK), 0)
        coli = lax.broadcasted_iota(jnp.int32, (BLOCK, BLOCK), 1)
        first = coli < HEAD_DIM
        sides = ((qc_ref, doc_ref, lc_ref, dc_ref, coli <= rowi),
                 (qn_ref, don_ref, ln_ref, dn_ref, jnp.logical_and(coli >= rowi, j < nb - 1)))

        @pl.when(j == 0)
        def _():
            carry[...] = jnp.zeros_like(carry)

        kcat, q2, do2, s2, dp2 = {}, {}, {}, {}, {}
        for pp, hp in units:
            kcat[pp, hp] = _by_head(k_ref[pp, :, _lanes(hp)], first)
            vcat = _by_head(v_ref[pp, :, _lanes(hp)], first)
            for x, (q_r, do_r, _, _, _) in enumerate(sides):
                q2[pp, hp, x] = _scaled(q_r[pp, :, _lanes(hp)])
                do2[pp, hp, x] = do_r[pp, :, _lanes(hp)]
                s2[pp, hp, x] = _dot_nt(q2[pp, hp, x], kcat[pp, hp])
                dp2[pp, hp, x] = _dot_nt(do2[pp, hp, x], vcat)
        pcat, dscat = {}, {}
        for pp, hp in units:
            for x, (_, _, l_r, d_r, msk) in enumerate(sides):
                ps, dss = [], []
                for h in range(2):
                    col = hp * LANES + h * HEAD_DIM
                    half = slice(h * BLOCK, (h + 1) * BLOCK)
                    p = jnp.where(msk, jnp.exp(s2[pp, hp, x][:, half] - l_r[pp, :, col:col + 1]), 0.0)
                    ps.append(p.astype(BF16))
                    dss.append((p * (dp2[pp, hp, x][:, half] - d_r[pp, :, col:col + 1])).astype(BF16))
                pcat[pp, hp, x] = jnp.concatenate(ps, axis=1)
                dscat[pp, hp, x] = jnp.concatenate(dss, axis=1)
        for pp, hp in units:
            dv2 = _dot_tn(pcat[pp, hp, 0], do2[pp, hp, 0]) + _dot_tn(pcat[pp, hp, 1], do2[pp, hp, 1])
            dk2 = _dot_tn(dscat[pp, hp, 0], q2[pp, hp, 0]) + _dot_tn(dscat[pp, hp, 1], q2[pp, hp, 1])
            dv_ref[pp, :, _lanes(hp)] = jnp.where(first, dv2[:BLOCK], dv2[BLOCK:]).astype(BF16)
            dk_ref[pp, :, _lanes(hp)] = jnp.where(first, dk2[:BLOCK], dk2[BLOCK:]).astype(BF16)
            dq_ref[pp, :, _lanes(hp)] = (carry[pp, :, _lanes(hp)]
                                         + _dot(dscat[pp, hp, 0], kcat[pp, hp]) * scale).astype(BF16)
            carry[pp, :, _lanes(hp)] = _dot(dscat[pp, hp, 1], kcat[pp, hp]) * scale
        pl.when(step == (r // PL) * nb - 1)(finish)

    cur = pl.BlockSpec((PL, BLOCK, GROUP), lambda c, n: (c, n, 0))
    nxt = pl.BlockSpec((PL, BLOCK, GROUP), lambda c, n: (c, jnp.minimum(n + 1, nb - 1), 0))
    return pl.pallas_call(
        body, name="dil_bwd_r%d" % r, grid=(r // PL, nb),
        in_specs=[cur, nxt, cur, nxt, cur, nxt, cur, nxt, cur, cur, HBM], out_specs=[cur, cur, cur, HBM],
        out_shape=[jax.ShapeDtypeStruct(q.shape, BF16)] * 3 + [jax.ShapeDtypeStruct(part.shape, part.dtype)],
        scratch_shapes=[pltpu.VMEM((PL, BLOCK, GROUP), F32), pltpu.SemaphoreType.DMA((3,)),
                        pltpu.SemaphoreType.DMA((3,))],
    )(q, q, do, do, lse, lse, delta, delta, k, v, part)


SB_TILES = 2
SB_PAIRS_FWD = 4
SB_PAIRS_BWD = 4
SB_DEAD = -110.0


def _lanes(hp):
    return slice(hp * LANES, (hp + 1) * LANES)


def _sb_logits(z, valid):
    e = jnp.exp(-jnp.abs(z))
    lb = jnp.minimum(z, 0.0) - jnp.log(1.0 + e)
    lk = lb - z
    if valid is not None:
        lk = jnp.where(valid, lk, 0.0)
    return e, lb, lk


def _by_head(t, first):
    zero = jnp.zeros_like(t)
    return jnp.concatenate([jnp.where(first, t, zero), jnp.where(first, zero, t)], axis=0)


def _sb_valid(i, j):
    rowi = lax.broadcasted_iota(jnp.int32, (BLOCK, BLOCK), 0)
    coli = lax.broadcasted_iota(jnp.int32, (BLOCK, BLOCK), 1)
    return (coli - rowi) < (i - j) * BLOCK


def _scaled(q):
    return (q.astype(F32) * (HEAD_DIM ** -0.5)).astype(BF16)


def _hosted_gathers(refs_in, refs_out, send, recv, step, steps):
    plans = [_gather_plan(refs_in[a], refs_out[a], send.at[pl.ds(6 * a, 6)], recv.at[pl.ds(6 * a, 6)])
             for a in range(len(refs_in))]
    for stage, at in ((0, 0), (1, (2 * steps) // 3)):
        @pl.when(step == at)
        def _():
            for plan in plans:
                plan[stage]()

    def finish():
        @pl.when(step == steps - 1)
        def _():
            for plan in plans:
                plan[2]()

    return finish


def _hosted_specs(slots):
    n = len(slots)
    sems = [pltpu.SemaphoreType.DMA((6 * n,))] * 2 if n else []
    return [HBM] * n, [HBM] * n, [jax.ShapeDtypeStruct(s.shape, s.dtype) for s in slots], sems


def _sb_fwd(qs, ks, vs, tri_later, slots):
    S = qs.shape[0]
    P = SB_PAIRS_FWD
    W = P * LANES
    ns = len(slots)

    def body(q_ref, k_ref, v_ref, u_ref, *rest):
        o_ref, lt_ref, from_ref = rest[ns:ns + 3]
        i = pl.program_id(1)
        finish = _hosted_gathers(rest[:ns], rest[ns + 3:2 * ns + 3], *rest[2 * ns + 3:], i, S // BLOCK) if ns else None
        first = lax.broadcasted_iota(jnp.int32, (BLOCK, LANES), 1) < HEAD_DIM
        q2 = [_scaled(q_ref[:, _lanes(hp)]) for hp in range(P)]

        def walk(tiles, carry):
            runs, accs = list(carry[0]), list(carry[1])
            units = [(t, hp) for t in range(len(tiles)) for hp in range(P)]
            offs = [pl.multiple_of(j * BLOCK, BLOCK) for j, _ in tiles]
            valids = [_sb_valid(i, j) if diag else None for j, diag in tiles]
            z2s, lbs, c2s = {}, {}, {}
            for t, hp in units:
                z2s[t, hp] = _dot_nt(q2[hp], _by_head(k_ref[pl.ds(offs[t], BLOCK), _lanes(hp)], first))
            for t, hp in units:
                for h in range(2):
                    _, lb, lk = _sb_logits(z2s[t, hp][:, h * BLOCK:(h + 1) * BLOCK], valids[t])
                    lbs[t, hp, h] = lb
                    c2s[t, hp, h] = _dot(jnp.concatenate(_split(lk), axis=1), u_ref[...])
            for t, hp in units:
                a2 = []
                for h in range(2):
                    a = jnp.exp(lbs[t, hp, h] + c2s[t, hp, h][:, :BLOCK] + runs[2 * hp + h])
                    if valids[t] is not None:
                        a = jnp.where(valids[t], a, 0.0)
                    a2.append(a.astype(BF16))
                    runs[2 * hp + h] = runs[2 * hp + h] + c2s[t, hp, h][:, BLOCK:]
                vcat = _by_head(v_ref[pl.ds(offs[t], BLOCK), _lanes(hp)], first)
                accs[hp] = accs[hp] + _dot(jnp.concatenate(a2, axis=1), vcat)
            return tuple(runs), tuple(accs)

        def chunk(ci, carry):
            return walk([(ci * SB_TILES + t, False) for t in reversed(range(SB_TILES))], carry)

        def alive(runs):
            top = functools.reduce(jnp.maximum, runs)
            return (jnp.max(top) > SB_DEAD).astype(jnp.int32)

        def step(c):
            t, _, runs, accs = c
            runs, accs = chunk(nfull - 1 - t, (runs, accs))
            return t + 1, alive(runs), runs, accs

        zero = jnp.zeros((BLOCK, LANES), F32)
        nfull = i // SB_TILES
        ragged = [functools.partial(walk, [(i, True)] + [(i - 1 - m, False) for m in range(extra)])
                  for extra in range(SB_TILES)]
        runs, accs = lax.switch(i % SB_TILES, ragged, ((zero,) * (2 * P), (zero,) * P))
        done, _, runs, accs = lax.while_loop(lambda c: jnp.logical_and(c[0] < nfull, c[1] > 0), step,
                                             (jnp.int32(0), alive(runs), runs, accs))
        for hp in range(P):
            o_ref[:, _lanes(hp)] = accs[hp]
            lt_ref[:, _lanes(hp)] = jnp.where(first, runs[2 * hp], runs[2 * hp + 1])
        from_ref[...] = jnp.full(from_ref.shape, nfull - done, jnp.int32)
        if finish is not None:
            finish()

    assert W == GROUP
    blk = pl.BlockSpec((BLOCK, W), lambda hp, i: (i, hp))
    col = pl.BlockSpec((S, W), lambda hp, i: (0, hp))
    h_in, h_out, h_shape, h_sems = _hosted_specs(slots)
    return pl.pallas_call(
        body, name="sb_fwd", grid=(GROUP // W, S // BLOCK),
        in_specs=[blk, col, col, _full((2 * BLOCK, 2 * BLOCK))] + h_in,
        out_specs=[blk, blk, pl.BlockSpec((1, 8, LANES), lambda hp, i: (i, 0, 0))] + h_out,
        out_shape=[jax.ShapeDtypeStruct((S, GROUP), F32)] * 2
                  + [jax.ShapeDtypeStruct((S // BLOCK, 8, LANES), jnp.int32)] + h_shape,
        input_output_aliases={4 + a: 3 + a for a in range(ns)},
        scratch_shapes=h_sems,
    )(qs, ks, vs, tri_later, *slots)


def _sb_bwd(first_chunk, qs, ks, vs, do, ltot, tri_upto, tri_before):
    S = qs.shape[0]
    P = SB_PAIRS_BWD
    W = P * LANES

    def body(from_ref, q_ref, k_ref, v_ref, do_ref, lt_ref, w_ref, x_ref, dq_ref, dk_ref, dv_ref):
        i = pl.program_id(1)

        @pl.when(i == 0)
        def _():
            dk_ref[...] = jnp.zeros_like(dk_ref)
            dv_ref[...] = jnp.zeros_like(dv_ref)

        first = lax.broadcasted_iota(jnp.int32, (BLOCK, LANES), 1) < HEAD_DIM
        q2 = [_scaled(q_ref[:, _lanes(hp)]) for hp in range(P)]
        do2 = [do_ref[:, _lanes(hp)] for hp in range(P)]
        totals = [jnp.broadcast_to(lt_ref[:, n * HEAD_DIM:n * HEAD_DIM + 1], (BLOCK, LANES)) for n in range(2 * P)]

        def walk(tiles, carry):
            keeps, grads, dqs = list(carry[0]), list(carry[1]), list(carry[2])
            units = [(t, hp) for t in range(len(tiles)) for hp in range(P)]
            offs = [pl.multiple_of(j * BLOCK, BLOCK) for j, _ in tiles]
            valids = [_sb_valid(i, j) if diag else None for j, diag in tiles]
            kcat, z2, da2, es, lbs, c2s, as_, des, p2s = {}, {}, {}, {}, {}, {}, {}, {}, {}
            for t, hp in units:
                kcat[t, hp] = _by_head(k_ref[pl.ds(offs[t], BLOCK), _lanes(hp)], first)
                z2[t, hp] = _dot_nt(q2[hp], kcat[t, hp])
                da2[t, hp] = _dot_nt(do2[hp], _by_head(v_ref[pl.ds(offs[t], BLOCK), _lanes(hp)], first))
            for t, hp in units:
                for h in range(2):
                    es[t, hp, h], lbs[t, hp, h], lk = _sb_logits(z2[t, hp][:, h * BLOCK:(h + 1) * BLOCK], valids[t])
                    c2s[t, hp, h] = _dot(jnp.concatenate(_split(lk), axis=1), w_ref[...])
            for t, hp in units:
                for h in range(2):
                    n = 2 * hp + h
                    a = jnp.exp(lbs[t, hp, h] + (totals[n] - (keeps[n] + c2s[t, hp, h][:, :BLOCK])))
                    if valids[t] is not None:
                        a = jnp.where(valids[t], a, 0.0)
                    keeps[n] = keeps[n] + c2s[t, hp, h][:, BLOCK:]
                    de = a * da2[t, hp][:, h * BLOCK:(h + 1) * BLOCK]
                    as_[t, hp, h], des[t, hp, h] = a.astype(BF16), de
                    p2s[t, hp, h] = _dot(jnp.concatenate(_split(de), axis=1), x_ref[...])
            for t, hp in units:
                dz2 = []
                for h in range(2):
                    n = 2 * hp + h
                    e = es[t, hp, h]
                    sig = jnp.where(z2[t, hp][:, h * BLOCK:(h + 1) * BLOCK] >= 0.0, 1.0, e) / (1.0 + e)
                    dz = des[t, hp, h] * (1.0 - sig) - (grads[n] + p2s[t, hp, h][:, :BLOCK]) * sig
                    if valids[t] is not None:
                        dz = jnp.where(valids[t], dz, 0.0)
                    grads[n] = grads[n] + p2s[t, hp, h][:, BLOCK:]
                    dz2.append(dz.astype(BF16))
                dzcat = jnp.concatenate(dz2, axis=1)
                dk2 = _dot_tn(dzcat, q2[hp])
                dv2 = _dot_tn(jnp.concatenate([as_[t, hp, 0], as_[t, hp, 1]], axis=1), do2[hp])
                dk_ref[pl.ds(offs[t], BLOCK), _lanes(hp)] += jnp.where(first, dk2[:BLOCK], dk2[BLOCK:])
                dv_ref[pl.ds(offs[t], BLOCK), _lanes(hp)] += jnp.where(first, dv2[:BLOCK], dv2[BLOCK:])
                dqs[hp] = dqs[hp] + _dot(dzcat, kcat[t, hp])
            return tuple(keeps), tuple(grads), tuple(dqs)

        zero = jnp.zeros((BLOCK, LANES), F32)
        nfull = i // SB_TILES
        carry = lax.fori_loop(
            from_ref[i], nfull, lambda ci, c: walk([(ci * SB_TILES + t, False) for t in range(SB_TILES)], c),
            ((zero,) * (2 * P), (zero,) * (2 * P), (zero,) * P))
        ragged = [functools.partial(walk, [(i - m, False) for m in range(extra, 0, -1)] + [(i, True)])
                  for extra in range(SB_TILES)]
        carry = lax.switch(i % SB_TILES, ragged, carry)
        for hp in range(P):
            dq_ref[:, _lanes(hp)] = carry[2][hp] * (HEAD_DIM ** -0.5)

    blk = pl.BlockSpec((BLOCK, W), lambda hp, i, fr: (i, hp))
    col = pl.BlockSpec((S, W), lambda hp, i, fr: (0, hp))
    tri = pl.BlockSpec((2 * BLOCK, 2 * BLOCK), lambda hp, i, fr: (0, 0))
    return pl.pallas_call(
        body, name="sb_bwd",
        grid_spec=pltpu.PrefetchScalarGridSpec(
            num_scalar_prefetch=1, grid=(GROUP // W, S // BLOCK),
            in_specs=[blk, col, col, blk, blk, tri, tri], out_specs=[blk, col, col]),
        out_shape=[jax.ShapeDtypeStruct((S, GROUP), F32)] * 3,
    )(first_chunk, qs, ks, vs, do, ltot, tri_upto, tri_before)


def _out_proj_fwd(o_br, l_br, o_sb, x, w_dil, w_sbn, w_out_g):
    S, D = x.shape
    tm = 512

    def body(o0, o1, o2, l0, l1, l2, os_ref, x_ref, wd_ref, ws_ref, w_ref, od_ref, s0, s1, s2, x1_ref, scr):
        ls = [_from_strided(scr, l) for l in (l0, l1, l2)]
        os_ = [_from_strided(scr, o) for o in (o0, o1, o2)]
        m = jnp.maximum(jnp.maximum(ls[0], ls[1]), ls[2])
        es = [jnp.exp(l - m) for l in ls]
        den = es[0] + es[1] + es[2]
        od = (es[0] * os_[0] + es[1] * os_[1] + es[2] * os_[2]) / den
        od_ref[...] = od
        _to_strided(scr, m + jnp.log(den), list(zip(DILATIONS, (s0, s1, s2))))
        osb = os_ref[...]
        mixed = jnp.concatenate([(od * _rms(od) * wd_ref[...]).astype(BF16),
                                 (osb * _rms(osb) * ws_ref[...]).astype(BF16)], axis=1)
        x1_ref[...] = x_ref[...] + _dot(mixed, w_ref[...])

    row = lambda i: (i, 0)
    g = pl.BlockSpec((tm, GROUP), row)
    d = pl.BlockSpec((tm, D), row)
    planes = [_strided_spec(tm, r) for r in DILATIONS]
    return pl.pallas_call(
        body, name="out_proj_fwd", grid=(S // tm,),
        in_specs=planes * 2 + [g, d, _full((1, GROUP)), _full((1, GROUP)), _full((2 * GROUP, D))],
        out_specs=[g] + planes + [d],
        out_shape=[jax.ShapeDtypeStruct((S, GROUP), F32)] + [_strided_shape(S, r, F32) for r in DILATIONS]
                  + [jax.ShapeDtypeStruct((S, D), F32)],
        scratch_shapes=[_strided_scratch(tm)],
    )(*o_br, *l_br, o_sb, x, w_dil, w_sbn, w_out_g)


def _ffn_fwd(x1, target, ffn_w, wg_g, wu_g, wd_g):
    S, D = x1.shape
    F = wg_g.shape[1]
    tm = 512
    nt = S // tm

    def body(x_ref, t_ref, nw_ref, wg_ref, wu_ref, wd_ref, h_ref, g_ref, u_ref, dy_ref, loss_ref, h_s, acc):
        j = pl.program_id(1)

        @pl.when(j == 0)
        def _():
            xv = x_ref[...]
            h = (xv * _rms(xv) * nw_ref[...]).astype(BF16)
            h_s[...] = h
            h_ref[...] = h
            acc[...] = xv

        h = h_s[...]
        g = _dot_nt(h, wg_ref[0])
        u = _dot_nt(h, wu_ref[0])
        g_ref[0] = g.astype(BF16)
        u_ref[0] = u.astype(BF16)
        a = (g * _sigmoid(g) * u).astype(BF16)
        acc[...] += _dot(a, wd_ref[0])

        @pl.when(j == N_CHIPS - 1)
        def _():
            err = acc[...] - t_ref[...]
            dy_ref[...] = err * (1.0 / D)
            loss_ref[...] = jnp.full(loss_ref.shape, jnp.sum(err * err), F32)

    row = lambda t, j: (t, 0)
    shard = lambda t, j: (j, 0, 0)
    act = lambda t, j: (j, t, 0)
    return pl.pallas_call(
        body, name="ffn_fwd", grid=(nt, N_CHIPS),
        in_specs=[pl.BlockSpec((tm, D), row), pl.BlockSpec((tm, D), row), pl.BlockSpec((1, D), lambda t, j: (0, 0))]
                 + [pl.BlockSpec((1, F, D), shard)] * 3,
        out_specs=[pl.BlockSpec((tm, D), row), pl.BlockSpec((1, tm, F), act), pl.BlockSpec((1, tm, F), act),
                   pl.BlockSpec((tm, D), row), pl.BlockSpec((1, 8, LANES), lambda t, j: (t, 0, 0))],
        out_shape=[jax.ShapeDtypeStruct((S, D), BF16), jax.ShapeDtypeStruct((N_CHIPS, S, F), BF16),
                   jax.ShapeDtypeStruct((N_CHIPS, S, F), BF16), jax.ShapeDtypeStruct((S, D), F32),
                   jax.ShapeDtypeStruct((nt, 8, LANES), F32)],
        scratch_shapes=[pltpu.VMEM((tm, D), BF16), pltpu.VMEM((tm, D), F32)],
    )(x1, target, ffn_w, wg_g, wu_g, wd_g)


def _ffn_bwd(h2, dy, g, u, wg_g, wu_g, wd_g):
    S, D = dy.shape
    F = wg_g.shape[1]
    tm = 512

    def body(h_ref, dy_ref, g_ref, u_ref, wg_ref, wu_ref, wd_ref, dwg_ref, dwu_ref, dwd_ref, dh_ref):
        t = pl.program_id(1)

        @pl.when(t == 0)
        def _():
            dwg_ref[...] = jnp.zeros_like(dwg_ref)
            dwu_ref[...] = jnp.zeros_like(dwu_ref)
            dwd_ref[...] = jnp.zeros_like(dwd_ref)

        h = h_ref[...]
        dyb = dy_ref[...].astype(BF16)
        gv = g_ref[0].astype(F32)
        uv = u_ref[0].astype(F32)
        da = _dot_nt(dyb, wd_ref[0])
        sg = _sigmoid(gv)
        silu = gv * sg
        du = (da * silu).astype(BF16)
        dg = (da * uv * (sg * (1.0 + gv * (1.0 - sg)))).astype(BF16)
        dwd_ref[0] += _dot_tn((silu * uv).astype(BF16), dyb)
        dwg_ref[0] += _dot_tn(dg, h)
        dwu_ref[0] += _dot_tn(du, h)
        dh_ref[0] = (_dot(dg, wg_ref[0]) + _dot(du, wu_ref[0])).astype(BF16)

    row = lambda j, t: (t, 0)
    shard = lambda j, t: (j, 0, 0)
    act = lambda j, t: (j, t, 0)
    return pl.pallas_call(
        body, name="ffn_bwd", grid=(N_CHIPS, S // tm),
        in_specs=[pl.BlockSpec((tm, D), row), pl.BlockSpec((tm, D), row),
                  pl.BlockSpec((1, tm, F), act), pl.BlockSpec((1, tm, F), act)] + [pl.BlockSpec((1, F, D), shard)] * 3,
        out_specs=[pl.BlockSpec((1, F, D), shard)] * 3 + [pl.BlockSpec((1, tm, D), act)],
        out_shape=[jax.ShapeDtypeStruct((N_CHIPS, F, D), F32)] * 3 + [jax.ShapeDtypeStruct((N_CHIPS, S, D), BF16)],
    )(h2, dy, g, u, wg_g, wu_g, wd_g)


def _out_proj_bwd(dh2p, dy, x1, ffn_w, w_out_g, o_dil, o_sb, w_dil, w_sbn, seg_ones, ffn_grads):
    S, D = dy.shape
    tm = 512
    ng = len(ffn_grads)

    def body(dh_ref, dy_ref, x1_ref, nw_ref, w_ref, od_ref, os_ref, wd_ref, ws_ref, g_ref, *rest):
        gin, rest = rest[:ng], rest[ng:]
        dx1_ref, dod0, dod1, dod2, dos_ref, dl0, dl1, dl2, dw_ref, dnw_ref, dwd_ref, dws_ref = rest[:12]
        gout, (scr, send, recv) = rest[12:12 + ng], rest[12 + ng:]
        i = pl.program_id(0)
        plans = [_pair_send_plan(gin[a], gout[a], send.at[a], recv.at[a]) for a in range(ng)]

        @pl.when(i == 0)
        def _():
            for start, _ in plans:
                start()

        @pl.when(i == 0)
        def _():
            for r_ in (dw_ref, dnw_ref, dwd_ref, dws_ref):
                r_[...] = jnp.zeros_like(r_)

        dh2 = _sum4(dh_ref)
        dxn, dwn = _rms_bwd(dh2, x1_ref[...], nw_ref[...])
        dnw_ref[...] += jnp.sum(dwn, axis=0, keepdims=True)
        dx1 = dy_ref[...] + dxn
        dx1_ref[...] = dx1
        dx1b = dx1.astype(BF16)
        dmix = _dot_nt(dx1b, w_ref[...])
        od = od_ref[...]
        osb = os_ref[...]
        mixed = jnp.concatenate([(od * _rms(od) * wd_ref[...]).astype(BF16),
                                 (osb * _rms(osb) * ws_ref[...]).astype(BF16)], axis=1)
        dw_ref[...] += _dot_tn(mixed, dx1b)
        do, dwo = _rms_bwd(dmix[:, :GROUP], od, wd_ref[...])
        dwd_ref[...] += jnp.sum(dwo, axis=0, keepdims=True)
        _to_strided(scr, do, list(zip(DILATIONS, (dod0, dod1, dod2))))
        _to_strided(scr, _segsum(do * od, g_ref[...]), list(zip(DILATIONS, (dl0, dl1, dl2))))
        do, dwo = _rms_bwd(dmix[:, GROUP:], osb, ws_ref[...])
        dws_ref[...] += jnp.sum(dwo, axis=0, keepdims=True)
        dos_ref[...] = do.astype(BF16)

        @pl.when(i == S // tm - 1)
        def _():
            for _, finish in plans:
                finish()

    row = lambda i: (i, 0)
    gsp = pl.BlockSpec((tm, GROUP), row)
    dsp = pl.BlockSpec((tm, D), row)
    planes = [_strided_spec(tm, r) for r in DILATIONS]
    halves = [jax.ShapeDtypeStruct((g.shape[0], g.shape[1] // 2, g.shape[2]), g.dtype) for g in ffn_grads]
    return pl.pallas_call(
        body, name="out_proj_bwd", grid=(S // tm,),
        in_specs=[pl.BlockSpec((N_CHIPS, tm, D), lambda i: (0, i, 0)), dsp, dsp, _full((1, D)), _full((2 * GROUP, D)),
                  gsp, gsp, _full((1, GROUP)), _full((1, GROUP)), _full((GROUP, GROUP // 2))] + [HBM] * ng,
        out_specs=[dsp] + planes + [gsp] + planes
                  + [_full((2 * GROUP, D)), _full((1, D)), _full((1, GROUP)), _full((1, GROUP))] + [HBM] * ng,
        out_shape=[jax.ShapeDtypeStruct((S, D), F32)] + [_strided_shape(S, r, BF16) for r in DILATIONS]
                  + [jax.ShapeDtypeStruct((S, GROUP), BF16)] + [_strided_shape(S, r, F32) for r in DILATIONS]
                  + [jax.ShapeDtypeStruct((2 * GROUP, D), F32),
                     jax.ShapeDtypeStruct((1, D), F32), jax.ShapeDtypeStruct((1, GROUP), F32),
                     jax.ShapeDtypeStruct((1, GROUP), F32)] + halves,
        scratch_shapes=[_strided_scratch(tm), pltpu.SemaphoreType.DMA((ng,)), pltpu.SemaphoreType.DMA((ng,))],
    )(dh2p, dy, x1, ffn_w, w_out_g, o_dil, o_sb, w_dil, w_sbn, seg_ones, *ffn_grads)


def _attn_in_bwd(dq_br, dk_br, dv_br, dqs, dks, dvs, qa, ka, qw, kw, cos_t, sin_t, seg_ones, h, w_in_g, x, dx1, attn_w):
    S, D = x.shape
    wc = w_in_g.shape[2]
    tm = 256

    def body(q0, q1, q2, k0, k1, k2, v0, v1, v2, dqs_ref, dks_ref, dvs_ref, qa_ref, ka_ref, qw_ref, kw_ref,
             cos_ref, sin_ref, g_ref, h_ref, w_ref, x_ref, dx1_ref, aw_ref,
             gx_ref, dw_ref, daw_ref, dqw_ref, dkw_ref, accq, acck, scr):
        i = pl.program_id(0)

        @pl.when(i == 0)
        def _():
            accq[...] = jnp.zeros_like(accq)
            acck[...] = jnp.zeros_like(acck)
            dw_ref[...] = jnp.zeros_like(dw_ref)
            daw_ref[...] = jnp.zeros_like(daw_ref)

        def branches(refs):
            return (_from_strided(scr, refs[0]) + _from_strided(scr, refs[1])) + _from_strided(scr, refs[2])

        g = g_ref[...]
        cos = _tile4(cos_ref[...])
        sin = _tile4(sin_ref[...])
        pieces = []
        for refs, pre_ref, w_r, acc in (((q0, q1, q2), qa_ref, qw_ref, accq), ((k0, k1, k2), ka_ref, kw_ref, acck)):
            dh = branches(refs)
            dn = dh * cos + _rot_half(dh * sin)
            pre = pre_ref[...]
            rstd = lax.rsqrt(_segsum(pre * pre, g) * (1.0 / HEAD_DIM) + EPS)
            xh = pre * rstd
            acc[...] += jnp.sum(dn * xh, axis=0, keepdims=True)
            dxh = dn * w_r[...]
            pieces.append((rstd * (dxh - xh * (_segsum(dxh * xh, g) * (1.0 / HEAD_DIM)))).astype(BF16))
        pieces += [branches((v0, v1, v2)).astype(BF16), dqs_ref[...].astype(BF16), dks_ref[...].astype(BF16),
                   dvs_ref[...].astype(BF16)]
        dproj = jnp.concatenate(pieces, axis=1)
        hv = h_ref[...]
        dh = jnp.zeros((tm, D), F32)
        for j in range(N_CHIPS):
            dp = dproj[:, j * wc:(j + 1) * wc]
            dw_ref[j] += _dot_tn(hv, dp)
            dh = dh + _dot_nt(dp, w_ref[j])
        dx, dw = _rms_bwd(dh, x_ref[...], aw_ref[...])
        daw_ref[...] += jnp.sum(dw, axis=0, keepdims=True)
        gx_ref[...] = dx1_ref[...] + dx

        @pl.when(i == S // tm - 1)
        def _():
            for acc, o_ref in ((accq, dqw_ref), (acck, dkw_ref)):
                a = acc[...]
                pair = (a[:, 0:LANES] + a[:, LANES:2 * LANES]) + (a[:, 2 * LANES:3 * LANES] + a[:, 3 * LANES:4 * LANES])
                o_ref[...] = pair + pltpu.roll(pair, HEAD_DIM, 1)

    row = lambda i: (i, 0)
    gsp = pl.BlockSpec((tm, GROUP), row)
    dsp = pl.BlockSpec((tm, D), row)
    tab = pl.BlockSpec((tm, LANES), row)
    planes = [_strided_spec(tm, r) for r in DILATIONS]
    return pl.pallas_call(
        body, name="attn_in_bwd", grid=(S // tm,),
        in_specs=planes * 3 + [gsp] * 5 + [_full((1, GROUP)), _full((1, GROUP)), tab, tab, _full((GROUP, GROUP // 2)),
                                          dsp, _full((N_CHIPS, D, wc)), dsp, dsp, _full((1, D))],
        out_specs=[dsp, _full((N_CHIPS, D, wc)), _full((1, D)), _full((1, LANES)), _full((1, LANES))],
        out_shape=[jax.ShapeDtypeStruct((S, D), F32), jax.ShapeDtypeStruct((N_CHIPS, D, wc), F32),
                   jax.ShapeDtypeStruct((1, D), F32), jax.ShapeDtypeStruct((1, LANES), F32),
                   jax.ShapeDtypeStruct((1, LANES), F32)],
        scratch_shapes=[pltpu.VMEM((1, GROUP), F32), pltpu.VMEM((1, GROUP), F32), _strided_scratch(tm)],
    )(*dq_br, *dk_br, *dv_br, dqs, dks, dvs, qa, ka, qw, kw, cos_t, sin_t, seg_ones, h, w_in_g, x, dx1, attn_w)


def _constants(S):
    pos = jnp.arange(S, dtype=F32)
    inv_freq = ROPE_THETA ** (-jnp.arange(0, HEAD_DIM, 2, dtype=F32) / HEAD_DIM)
    ang_a = pos[::BLOCK, None] * inv_freq[None, :]
    ang_b = pos[:BLOCK, None] * inv_freq[None, :]
    ca, sa, cb, sb = jnp.cos(ang_a)[:, None], jnp.sin(ang_a)[:, None], jnp.cos(ang_b)[None], jnp.sin(ang_b)[None]
    cos = (ca * cb - sa * sb).reshape(S, HEAD_DIM // 2)
    sin = (sa * cb + ca * sb).reshape(S, HEAD_DIM // 2)
    cos_t = jnp.concatenate([cos, cos] * 2, axis=1)
    sin_t = jnp.concatenate([-sin, sin] * 2, axis=1)
    idx = jnp.arange(GROUP // 2)
    seg_ones = (idx[:, None] // HEAD_DIM == idx[None, :] // HEAD_DIM).astype(BF16)
    seg_ones = jnp.concatenate([seg_ones, seg_ones], axis=0)
    r = jnp.arange(BLOCK)
    ones = jnp.ones((BLOCK, BLOCK), BF16)
    tris = [jnp.concatenate([jnp.concatenate([m.astype(BF16), ones], axis=1)] * 2, axis=0) for m in
            (r[:, None] > r[None, :],
             r[:, None] <= r[None, :],
             r[:, None] < r[None, :])]
    return cos_t, sin_t, seg_ones, tris


FFN_NAMES = ("w_gate", "w_up", "w_down")


def _device_step(x, target, attn_w, qn_w, kn_w, dil_w, sbn_w, ffn_w, w_in_g, w_out_slots, ffn_slots, core, chip):
    S = x.shape[0]
    cos_t, sin_t, seg_ones, (tri_later, tri_upto, tri_before) = _constants(S)
    reps = GROUP // HEAD_DIM
    qw = jnp.tile(qn_w, (1, reps))
    kw = jnp.tile(kn_w, (1, reps))

    nd = len(DILATIONS)
    h, qa, ka, *rest, wg_g = _in_proj_fwd(x, attn_w, w_in_g, qw, kw, cos_t, sin_t, seg_ones, ffn_slots[:1])
    qh, kh, va, (qs, ks, vs) = rest[:nd], rest[nd:2 * nd], rest[2 * nd:3 * nd], rest[3 * nd:]
    hosted = ([], [w_out_slots], [ffn_slots[2]])
    branches = [_dil_fwd(qh[b], kh[b], va[b], hosted[b]) for b in range(nd)]
    w_out_g, wd_g = branches[1][2].reshape(-1, x.shape[1]), branches[2][2]
    o_sb, ltot, walked, wu_g = _sb_fwd(qs, ks, vs, tri_later, ffn_slots[1:2])
    o_dil, *lse, x1 = _out_proj_fwd([b[0] for b in branches], [b[1] for b in branches], o_sb, x, dil_w, sbn_w, w_out_g)
    h2, g, u, dy, loss_parts = _ffn_fwd(x1, target, ffn_w, wg_g, wu_g, wd_g)

    *ffn_grads, dh2p = _ffn_bwd(h2, dy, g, u, wg_g, wu_g, wd_g)
    dx1, *mid, dw_out, dffn_w, ddil_w, dsbn_w, p0, p1, p2 = _out_proj_bwd(
        dh2p, dy, x1, ffn_w, w_out_g, o_dil, o_sb, dil_w, sbn_w, seg_ones, ffn_grads)
    do_dil, do_sb, delta = mid[:nd], mid[nd], mid[nd + 1:]
    parts = [_pair_sum(gr, fr, core, n) for gr, fr, n in zip(ffn_grads, (p0, p1, p2), FFN_NAMES)]
    dqs, dks, dvs = _sb_bwd(walked[:, 0, 0], qs, ks, vs, do_sb, ltot, tri_upto, tri_before)
    dbr = [_dil_bwd(qh[b], kh[b], va[b], do_dil[b], lse[b], delta[b], parts[b]) for b in range(nd)]
    ffn_halves = [_chip_sum(dbr[b][3], parts[b], chip, FFN_NAMES[b]) for b in range(nd)]
    grad_x, dw_in, dattn_w, dqw, dkw = _attn_in_bwd(
        [b[0] for b in dbr], [b[1] for b in dbr], [b[2] for b in dbr], dqs, dks, dvs,
        qa, ka, qw, kw, cos_t, sin_t, seg_ones, h, w_in_g, x, dx1, attn_w)
    small = dict(attn=dattn_w, q=dqw[:, :HEAD_DIM], k=dkw[:, :HEAD_DIM], dil=ddil_w, sb=dsbn_w, ffn=dffn_w)
    return loss_parts, grad_x, small, dw_in, dw_out, ffn_halves


HBM = pl.BlockSpec(memory_space=pltpu.HBM)
VMEM = pl.BlockSpec(memory_space=pltpu.VMEM)
CHIP_FLIPS = ((1, 0), (0, 1), (1, 1))


def _place():
    return lax.axis_index("x"), lax.axis_index("y"), lax.axis_index("c")


def _flip(v, d):
    return 1 - v if d else v


def _half_rows(c, n):
    return pl.ds(pl.multiple_of(c * (n // 2), 16), n // 2)


def _gather_plan(slot_in, slot_out, send, recv):
    x, y, c = _place()
    p = 2 * x + y
    chips = [(_flip(x, dx), _flip(y, dy)) for dx, dy in CHIP_FLIPS]
    mine, other = _half_rows(c, slot_in.shape[1]), _half_rows(1 - c, slot_in.shape[1])

    def copy(k, src, dst, to):
        return pltpu.make_async_remote_copy(src_ref=src, dst_ref=dst, send_sem=send.at[k], recv_sem=recv.at[k],
                                            device_id=to, device_id_type=MESH)

    def first(k):
        return copy(k, slot_in.at[p, mine], slot_out.at[p, mine], (*chips[k], c))

    def passed(k, rows):
        land = slot_out.at[2 * chips[k][0] + chips[k][1], rows]
        return copy(3 + k, land, land, (x, y, 1 - c))

    def start():
        for k in range(3):
            first(k).start()

    def forward():
        for k in range(3):
            land = slot_out.at[2 * chips[k][0] + chips[k][1], mine]
            copy(k, land, land, (*chips[k], c)).wait_recv()
            passed(k, mine).start()

    def finish():
        for k in range(3):
            passed(k, other).wait_recv()
        for k in range(3):
            first(k).wait_send()
            passed(k, mine).wait_send()

    return start, forward, finish


def _chip_send_plan(part_in, recv_out, send, recv):
    x, y, c = _place()
    p = 2 * x + y
    chips = [(_flip(x, dx), _flip(y, dy)) for dx, dy in CHIP_FLIPS]

    def copy(k):
        q = 2 * chips[k][0] + chips[k][1]
        return pltpu.make_async_remote_copy(src_ref=part_in.at[q], dst_ref=recv_out.at[p], send_sem=send.at[k],
                                            recv_sem=recv.at[k], device_id=(*chips[k], c), device_id_type=MESH)

    def start():
        for k in range(3):
            copy(k).start()

    def finish():
        for k in range(3):
            land = recv_out.at[2 * chips[k][0] + chips[k][1]]
            pltpu.make_async_remote_copy(src_ref=land, dst_ref=land, send_sem=send.at[k], recv_sem=recv.at[k],
                                         device_id=(*chips[k], c), device_id_type=MESH).wait_recv()
        for k in range(3):
            copy(k).wait_send()

    return start, finish


def _pair_send_plan(grad_in, recv_out, send, recv):
    x, y, c = _place()

    def copy():
        theirs = _half_rows(1 - c, grad_in.shape[1])
        return pltpu.make_async_remote_copy(src_ref=grad_in.at[:, theirs, :], dst_ref=recv_out, send_sem=send,
                                            recv_sem=recv, device_id=(x, y, 1 - c), device_id_type=MESH)

    return (lambda: copy().start()), (lambda: copy().wait())


def _own_slots(shard):
    here = 2 * lax.axis_index("x") + lax.axis_index("y")
    return lax.dynamic_update_slice(lax.empty((N_CHIPS,) + shard.shape, shard.dtype), shard[None], (here, 0, 0))


def _gather_weights(shards):
    n = len(shards)

    def body(*refs):
        ins, outs = refs[:n], refs[n:2 * n]
        send, recv = refs[2 * n:]
        plans = [_gather_plan(ins[a], outs[a], send.at[pl.ds(6 * a, 6)], recv.at[pl.ds(6 * a, 6)]) for a in range(n)]
        for stage in range(3):
            for plan in plans:
                plan[stage]()

    slots = [_own_slots(s) for s in shards]
    return pl.pallas_call(
        body, name="gather_weights", in_specs=[HBM] * n, out_specs=[HBM] * n,
        out_shape=[jax.ShapeDtypeStruct(s.shape, s.dtype) for s in slots],
        input_output_aliases={a: a for a in range(n)},
        scratch_shapes=[pltpu.SemaphoreType.DMA((6 * n,)), pltpu.SemaphoreType.DMA((6 * n,))],
    )(*slots)


def _pair_exchange(grads, small):
    n = len(grads)

    def body(*refs):
        gin, sm = refs[:n], refs[n]
        gout, sm_all = refs[n + 1:2 * n + 1], refs[2 * n + 1]
        send, recv = refs[2 * n + 2:]
        x, y, c = _place()
        me = 4 * x + 2 * y + c
        big = [_pair_send_plan(gin[a], gout[a], send.at[a], recv.at[a]) for a in range(n)]
        for start, _ in big:
            start()
        sm_all[pl.ds(me, 1)] = sm[...][None]
        tiny = []
        for k in range(1, N_DEV):
            px, py, pc = _flip(x, k & 4), _flip(y, k & 2), _flip(c, k & 1)
            tiny.append((pltpu.make_async_remote_copy(
                src_ref=sm, dst_ref=sm_all.at[me], send_sem=send.at[n + k - 1], recv_sem=recv.at[n + k - 1],
                device_id=(px, py, pc), device_id_type=MESH), 4 * px + 2 * py + pc))
            tiny[-1][0].start()
        for k, (cp, peer) in enumerate(tiny):
            pltpu.make_async_remote_copy(src_ref=sm, dst_ref=sm_all.at[peer], send_sem=send.at[n + k],
                                         recv_sem=recv.at[n + k], device_id=(x, y, c),
                                         device_id_type=MESH).wait_recv()
            cp.wait_send()
        for _, finish in big:
            finish()

    halves = [jax.ShapeDtypeStruct((g.shape[0], g.shape[1] // 2, g.shape[2]), g.dtype) for g in grads]
    return pl.pallas_call(
        body, name="pair_exchange", in_specs=[HBM] * n + [VMEM], out_specs=[HBM] * n + [VMEM],
        out_shape=halves + [jax.ShapeDtypeStruct((N_DEV,) + small.shape, small.dtype)],
        scratch_shapes=[pltpu.SemaphoreType.DMA((n + N_DEV - 1,)), pltpu.SemaphoreType.DMA((n + N_DEV - 1,))],
    )(*grads, small)


def _chip_exchange(parts):
    n = len(parts)

    def body(*refs):
        pin, pout = refs[:n], refs[n:2 * n]
        send, recv = refs[2 * n:]
        plans = [_chip_send_plan(pin[a], pout[a], send.at[pl.ds(3 * a, 3)], recv.at[pl.ds(3 * a, 3)]) for a in range(n)]
        for stage in range(2):
            for plan in plans:
                plan[stage]()

    return pl.pallas_call(
        body, name="chip_exchange", in_specs=[HBM] * n, out_specs=[HBM] * n,
        out_shape=[jax.ShapeDtypeStruct(s.shape, s.dtype) for s in parts],
        scratch_shapes=[pltpu.SemaphoreType.DMA((3 * n,)), pltpu.SemaphoreType.DMA((3 * n,))],
    )(*parts)


def _pair_swap_plan(hin, hout, send, recv):
    x, y, c = _place()

    def copies():
        return [pltpu.make_async_remote_copy(src_ref=hin[a], dst_ref=hout[a], send_sem=send.at[a], recv_sem=recv.at[a],
                                             device_id=(x, y, 1 - c), device_id_type=MESH) for a in range(len(hin))]

    def start():
        for cp in copies():
            cp.start()

    def finish():
        for cp in copies():
            cp.wait()

    return start, finish


def _pair_swap(halves):
    n = len(halves)

    def body(*refs):
        start, finish = _pair_swap_plan(refs[:n], refs[n:2 * n], *refs[2 * n:])
        start()
        finish()

    return pl.pallas_call(
        body, name="pair_swap", in_specs=[HBM] * n, out_specs=[HBM] * n,
        out_shape=[jax.ShapeDtypeStruct(s.shape, s.dtype) for s in halves],
        scratch_shapes=[pltpu.SemaphoreType.DMA((n,)), pltpu.SemaphoreType.DMA((n,))],
    )(*halves)


def _pair_sum(grad, recv, c, tag):
    _, R, C = grad.shape
    hr = R // 2

    def body(c_ref, a_ref, b_ref, o_ref):
        o_ref[...] = (a_ref[...] + b_ref[...]).astype(BF16)

    return pl.pallas_call(
        body, name="pair_sum_" + tag,
        grid_spec=pltpu.PrefetchScalarGridSpec(
            num_scalar_prefetch=1, grid=(N_CHIPS,),
            in_specs=[pl.BlockSpec((1, hr, C), lambda s, cr: (s, cr[0], 0)),
                      pl.BlockSpec((1, hr, C), lambda s, cr: (s, 0, 0))],
            out_specs=pl.BlockSpec((1, hr, C), lambda s, cr: (s, 0, 0))),
        out_shape=jax.ShapeDtypeStruct((N_CHIPS, hr, C), BF16),
    )(c, grad, recv)


def _chip_sum(received, own, chip, tag):
    _, rows, C = received.shape
    tr = rows // 2

    def body(chip_ref, own_ref, r1_ref, r2_ref, r3_ref, o_ref):
        p = [r[0].astype(F32) for r in (own_ref, r1_ref, r2_ref, r3_ref)]
        o_ref[...] = (p[0] + p[1]) + (p[2] + p[3])

    def slot(k):
        return pl.BlockSpec((1, tr, C), lambda i, cr: (jnp.bitwise_xor(cr[0], k), i, 0))

    return pl.pallas_call(
        body, name="chip_sum_" + tag,
        grid_spec=pltpu.PrefetchScalarGridSpec(
            num_scalar_prefetch=1, grid=(rows // tr,), in_specs=[slot(0), slot(1), slot(2), slot(3)],
            out_specs=pl.BlockSpec((tr, C), lambda i, cr: (i, 0))),
        out_shape=jax.ShapeDtypeStruct((rows, C), F32),
    )(chip, own, received, received, received)


def _adamw_math(w, g, m, v):
    m = ADAM_B1 * m + (1.0 - ADAM_B1) * g
    v = ADAM_B2 * v + (1.0 - ADAM_B2) * (g * g)
    m_hat = m / (1.0 - ADAM_B1 ** ADAM_STEP)
    v_hat = v / (1.0 - ADAM_B2 ** ADAM_STEP)
    delta = -ADAM_LR * (m_hat / (jnp.sqrt(v_hat) + ADAM_EPS) + ADAM_WD * w)
    return delta, m, v


def _adamw(w, g_mine, g_other, m, v, c, tag):
    R, C = w.shape
    tr = R // 4

    def body(c_ref, w_ref, gm_ref, go_ref, m_ref, v_ref, g_ref, d_ref, nm_ref, nv_ref):
        g = jnp.where(pl.program_id(0) == c_ref[0], gm_ref[...], go_ref[...])
        g_ref[...] = g
        d_ref[...], nm_ref[...], nv_ref[...] = _adamw_math(w_ref[...], g, m_ref[...], v_ref[...])

    blk = pl.BlockSpec((tr, C), lambda h, i, cr: (2 * h + i, 0))
    half = pl.BlockSpec((tr, C), lambda h, i, cr: (i, 0))
    return pl.pallas_call(
        body, name="adamw_" + tag,
        grid_spec=pltpu.PrefetchScalarGridSpec(
            num_scalar_prefetch=1, grid=(2, 2), in_specs=[blk, half, half, blk, blk], out_specs=[blk] * 4),
        out_shape=[jax.ShapeDtypeStruct((R, C), F32)] * 4,
    )(c, w, g_mine, g_other, m, v)


def _small_update(all_small, w, m, v):
    def body(a_ref, w_ref, m_ref, v_ref, g_ref, d_ref, nm_ref, nv_ref):
        g = ((a_ref[0] + a_ref[1]) + (a_ref[2] + a_ref[3])) + ((a_ref[4] + a_ref[5]) + (a_ref[6] + a_ref[7]))
        g_ref[...] = g
        d_ref[...], nm_ref[...], nv_ref[...] = _adamw_math(w_ref[...], g, m_ref[...], v_ref[...])

    return pl.pallas_call(
        body, name="small_update", out_shape=[jax.ShapeDtypeStruct(w.shape, F32)] * 4,
    )(all_small, w, m, v)


SMALL_ROWS = (("attn", 0, 0), ("ffn", 1, 0), ("dil", 2, 0), ("sb", 2, GROUP), ("q", 3, 0), ("k", 3, HEAD_DIM),
              ("loss", 4, 0))


def _pack_small(vals, D):
    rows = [jnp.zeros((1, D), F32) for _ in range(8)]
    for name, r, off in SMALL_ROWS:
        if name in vals:
            rows[r] = lax.dynamic_update_slice(rows[r], vals[name].astype(F32), (0, off))
    return jnp.concatenate(rows, axis=0)


def _unpack_small(packed, vals):
    return {name: packed[r:r + 1, off:off + vals[name].shape[1]] for name, r, off in SMALL_ROWS if name in vals}


def kernel(x, attn_norm_w, w_in, q_norm_w, k_norm_w, dil_out_norm_w, sb_out_norm_w, w_out, ffn_norm_w, w_gate, w_up, w_down, loss_target, m_attn_norm_w, m_w_in, m_q_norm_w, m_k_norm_w, m_dil_out_norm_w, m_sb_out_norm_w, m_w_out, m_ffn_norm_w, m_w_gate, m_w_up, m_w_down, v_attn_norm_w, v_w_in, v_q_norm_w, v_k_norm_w, v_dil_out_norm_w, v_sb_out_norm_w, v_w_out, v_ffn_norm_w, v_w_gate, v_w_up, v_w_down):
    D = x.shape[-1]
    big_names = ("w_in", "w_out", "w_gate", "w_up", "w_down")
    flipped = ("w_gate", "w_up")
    tr = lambda a: jnp.swapaxes(a[0], 0, 1)
    big_w = dict(w_in=w_in[0], w_out=w_out[0], w_gate=tr(w_gate), w_up=tr(w_up), w_down=w_down[0])
    big_m = dict(w_in=m_w_in[0], w_out=m_w_out[0], w_gate=tr(m_w_gate), w_up=tr(m_w_up), w_down=m_w_down[0])
    big_v = dict(w_in=v_w_in[0], w_out=v_w_out[0], w_gate=tr(v_w_gate), w_up=tr(v_w_up), w_down=v_w_down[0])
    small_w = dict(attn=attn_norm_w, q=q_norm_w, k=k_norm_w, dil=dil_out_norm_w, sb=sb_out_norm_w, ffn=ffn_norm_w)
    small_m = dict(attn=m_attn_norm_w, q=m_q_norm_w, k=m_k_norm_w, dil=m_dil_out_norm_w, sb=m_sb_out_norm_w,
                   ffn=m_ffn_norm_w)
    small_v = dict(attn=v_attn_norm_w, q=v_q_norm_w, k=v_k_norm_w, dil=v_dil_out_norm_w, sb=v_sb_out_norm_w,
                   ffn=v_ffn_norm_w)

    c = lax.axis_index("c").astype(jnp.int32).reshape(1)
    chip = (2 * lax.axis_index("x") + lax.axis_index("y")).astype(jnp.int32).reshape(1)
    (w_in_g,) = _gather_weights([big_w["w_in"].astype(BF16)])
    w_out_slots = _own_slots(big_w["w_out"].astype(BF16))
    ffn_slots = [_own_slots(big_w[n].astype(BF16)) for n in FFN_NAMES]

    loss_parts, grad_x, small_g, dw_in, dw_out, ffn_halves = _device_step(
        x[0], loss_target[0], attn_norm_w, q_norm_w, k_norm_w, dil_out_norm_w, sb_out_norm_w, ffn_norm_w,
        w_in_g, w_out_slots, ffn_slots, c, chip)
    small_g["loss"] = (jnp.sum(loss_parts[:, 0, 0]) * (0.5 / D)).reshape(1, 1)

    late = [dw_in, dw_out.reshape(N_CHIPS, -1, D)]
    *from_pair, all_small = _pair_exchange(late, _pack_small(small_g, D))
    chip_parts = [_pair_sum(g, r, c, n) for g, r, n in zip(late, from_pair, big_names)]
    from_chips = _chip_exchange(chip_parts)
    halves = [_chip_sum(r, p, chip, n) for r, p, n in zip(from_chips, chip_parts, big_names)] + ffn_halves
    others = _pair_swap(halves)
    big_out = {n: _adamw(big_w[n], mine, other, big_m[n], big_v[n], c, n)
               for n, mine, other in zip(big_names, halves, others)}
    sg, sd, sm, sv = _small_update(all_small, _pack_small(small_w, D), _pack_small(small_m, D),
                                   _pack_small(small_v, D))
    small_out = [_unpack_small(t, small_w) for t in (sg, sd, sm, sv)]

    order = (("attn", None), (None, "w_in"), ("q", None), ("k", None), ("dil", None), ("sb", None),
             (None, "w_out"), ("ffn", None), (None, "w_gate"), (None, "w_up"), (None, "w_down"))
    outs = [sg[4, 0], grad_x[None]]
    for kind in range(4):
        for s_name, b_name in order:
            if s_name is not None:
                outs.append(small_out[kind][s_name])
            else:
                res = big_out[b_name][kind]
                outs.append((jnp.swapaxes(res, 0, 1) if b_name in flipped else res)[None])
    return tuple(outs)
```

```python
import functools

import jax
import jax.numpy as jnp
from jax import lax
from jax.experimental import pallas as pl
from jax.experimental.pallas import tpu as pltpu

F32 = jnp.float32
BF16 = jnp.bfloat16
MESH = pl.DeviceIdType.MESH

HEAD_DIM = 64
GROUP = 512
BLOCK = 128
LANES = 128
N_CHIPS = 4
N_DEV = 8
EPS = 1e-6
ROPE_THETA = 10000.0
DILATIONS = (1, 4, 16)
NEG = -1e30

ADAM_LR = 0.001
ADAM_B1 = 0.9
ADAM_B2 = 0.999
ADAM_EPS = 1e-08
ADAM_WD = 0.01
ADAM_STEP = 10


def _dot(a, b):
    return jnp.dot(a, b, preferred_element_type=F32)


def _dot_nt(a, b):
    return lax.dot_general(a, b, (((1,), (1,)), ((), ())), preferred_element_type=F32)


def _dot_tn(a, b):
    return lax.dot_general(a, b, (((0,), (0,)), ((), ())), preferred_element_type=F32)


def _split(v):
    hi = lax.bitcast_convert_type(lax.bitcast_convert_type(v, jnp.uint32) & jnp.uint32(0xFFFF0000), F32)
    return hi.astype(BF16), (v - hi).astype(BF16)


def _segsum(v, g):
    hi, lo = _split(v)
    w = g.shape[1]
    return jnp.concatenate([_dot(jnp.concatenate([hi[:, c:c + w], lo[:, c:c + w]], axis=1), g)
                            for c in range(0, v.shape[1], w)], axis=1)


def _rot_half(x):
    outs = []
    for c in range(x.shape[1] // LANES):
        xc = x[:, c * LANES:(c + 1) * LANES]
        lane = lax.broadcasted_iota(jnp.int32, xc.shape, 1)
        first = (lane % HEAD_DIM) < (HEAD_DIM // 2)
        outs.append(jnp.where(first, pltpu.roll(xc, LANES - 32, 1), pltpu.roll(xc, 32, 1)))
    return outs[0] if len(outs) == 1 else jnp.concatenate(outs, axis=1)


def _rms(x):
    return lax.rsqrt(jnp.mean(x * x, axis=-1, keepdims=True) + EPS)


def _rms_bwd(dy, x, w):
    rstd = _rms(x)
    xh = x * rstd
    dxh = dy * w
    dx = rstd * (dxh - xh * jnp.mean(dxh * xh, axis=-1, keepdims=True))
    return dx, dy * xh


def _sigmoid(x):
    return 1.0 / (1.0 + jnp.exp(-x))


def _sum4(ref):
    p = [ref[j].astype(F32) for j in range(N_CHIPS)]
    return (p[0] + p[1]) + (p[2] + p[3])


def _full(shape):
    n = len(shape)
    return pl.BlockSpec(shape, lambda *_: (0,) * n)


def _strided_spec(tm, r):
    return pl.BlockSpec((r, tm // r, GROUP), lambda i: (0, i, 0))


def _strided_shape(S, r, dtype):
    return jax.ShapeDtypeStruct((r, S // r, GROUP), dtype)


def _to_strided(scr, val, outs):
    chunks = range(GROUP // LANES)
    for k in chunks:
        scr[k] = val[:, _lanes(k)]
    for r, o_ref in outs:
        if r == 1:
            o_ref[0] = val.astype(o_ref.dtype)
            continue
        n = val.shape[0] // r
        for c in range(r):
            rows = pl.ds(c, n, stride=r)
            o_ref[c] = jnp.concatenate([scr.at[k][rows, :] for k in chunks], axis=1).astype(o_ref.dtype)


def _from_strided(scr, ref):
    r, n, _ = ref.shape
    if r == 1:
        return ref[0].astype(F32)
    chunks = range(GROUP // LANES)
    for c in range(r):
        plane = ref[c].astype(F32)
        for k in chunks:
            scr.at[k][pl.ds(c, n, stride=r), :] = plane[:, _lanes(k)]
    return jnp.concatenate([scr[k] for k in chunks], axis=1)


def _strided_scratch(tm):
    return pltpu.VMEM((GROUP // LANES, tm, LANES), F32)


def _tile4(t):
    return jnp.concatenate([t] * (GROUP // LANES), axis=1)


def _in_proj_fwd(x, attn_w, w_in_g, qw, kw, cos_t, sin_t, seg_ones, slots):
    S, D = x.shape
    tm = 512
    wcols = w_in_g.shape[2]
    nd = len(DILATIONS)
    ns = len(slots)

    def body(x_ref, aw_ref, w_ref, qw_ref, kw_ref, cos_ref, sin_ref, g_ref, *rest):
        slot_in, (h_ref, qa_ref, ka_ref), rest = rest[:ns], rest[ns:ns + 3], rest[ns + 3:]
        q_refs, k_refs, v_refs = rest[:nd], rest[nd:2 * nd], rest[2 * nd:3 * nd]
        qs_ref, ks_ref, vs_ref = rest[3 * nd:3 * nd + 3]
        slot_out, scr, sems = rest[3 * nd + 3:3 * nd + 3 + ns], rest[3 * nd + 3 + ns], rest[3 * nd + 4 + ns:]
        finish = _hosted_gathers(slot_in, slot_out, *sems, pl.program_id(0), S // tm) if ns else None
        xv = x_ref[...]
        h = (xv * _rms(xv) * aw_ref[...]).astype(BF16)
        h_ref[...] = h
        proj = jnp.concatenate([_dot(h, w_ref[j]) for j in range(N_CHIPS)], axis=1)
        qa = proj[:, 0 * GROUP:1 * GROUP]
        ka = proj[:, 1 * GROUP:2 * GROUP]
        qa_ref[...] = qa
        ka_ref[...] = ka
        _to_strided(scr, proj[:, 2 * GROUP:3 * GROUP], list(zip(DILATIONS, v_refs)))
        qs_ref[...] = proj[:, 3 * GROUP:4 * GROUP].astype(BF16)
        ks_ref[...] = proj[:, 4 * GROUP:5 * GROUP].astype(BF16)
        vs_ref[...] = proj[:, 5 * GROUP:6 * GROUP].astype(BF16)
        g = g_ref[...]
        cos = _tile4(cos_ref[...])
        sin = _tile4(sin_ref[...])
        for t, w_r, o_rs in ((qa, qw_ref, q_refs), (ka, kw_ref, k_refs)):
            rstd = lax.rsqrt(_segsum(t * t, g) * (1.0 / HEAD_DIM) + EPS)
            tn = t * rstd * w_r[...]
            _to_strided(scr, tn * cos + _rot_half(tn) * sin, list(zip(DILATIONS, o_rs)))
        if finish is not None:
            finish()

    row = lambda i: (i, 0)
    tile = lambda n, dt: jax.ShapeDtypeStruct((S, n), dt)
    planes = [_strided_spec(tm, r) for r in DILATIONS]
    h_in, h_out, h_shape, h_sems = _hosted_specs(slots)
    n_out = 6 + 3 * nd
    return pl.pallas_call(
        body, name="in_proj_fwd", grid=(S // tm,),
        in_specs=[pl.BlockSpec((tm, D), row), _full((1, D)), _full((N_CHIPS, D, wcols)),
                  _full((1, GROUP)), _full((1, GROUP)),
                  pl.BlockSpec((tm, LANES), row), pl.BlockSpec((tm, LANES), row),
                  _full((GROUP, GROUP // 2))] + h_in,
        out_specs=[pl.BlockSpec((tm, D), row)] + [pl.BlockSpec((tm, GROUP), row)] * 2 + planes * 3
                  + [pl.BlockSpec((tm, GROUP), row)] * 3 + h_out,
        out_shape=[tile(D, BF16), tile(GROUP, F32), tile(GROUP, F32)]
                  + [_strided_shape(S, r, BF16) for r in DILATIONS] * 3 + [tile(GROUP, BF16)] * 3 + h_shape,
        input_output_aliases={8 + a: n_out + a for a in range(ns)},
        scratch_shapes=[_strided_scratch(tm)] + h_sems,
    )(x, attn_w, w_in_g, qw, kw, cos_t, sin_t, seg_ones, *slots)


DIL_PLANES = 1


def _dil_fwd(q, k, v, slots):
    r, L, _ = q.shape
    nb = L // BLOCK
    P = GROUP // LANES
    PL = min(DIL_PLANES, r)
    ns = len(slots)
    units = [(pp, hp) for pp in range(PL) for hp in range(P)]

    def body(q_ref, kc_ref, kp_ref, vc_ref, vp_ref, *rest):
        o_ref, l_ref = rest[ns:ns + 2]
        n = pl.program_id(1)
        finish = (_hosted_gathers(rest[:ns], rest[ns + 2:2 * ns + 2], *rest[2 * ns + 2:],
                                  pl.program_id(0) * nb + n, (r // PL) * nb) if ns else None)
        rowi = lax.broadcasted_iota(jnp.int32, (BLOCK, BLOCK), 0)
        coli = lax.broadcasted_iota(jnp.int32, (BLOCK, BLOCK), 1)
        first = coli < HEAD_DIM
        masks = (coli <= rowi, jnp.logical_and(coli >= rowi, n > 0))
        s2 = {}
        for pp, hp in units:
            q2 = _scaled(q_ref[pp, :, _lanes(hp)])
            for b, k_ref in enumerate((kc_ref, kp_ref)):
                s2[pp, hp, b] = _dot_nt(q2, _by_head(k_ref[pp, :, _lanes(hp)], first))
        ps, inv, lse = {}, {}, {}
        for pp, hp in units:
            for h in range(2):
                s = [jnp.where(masks[b], s2[pp, hp, b][:, h * BLOCK:(h + 1) * BLOCK], NEG) for b in range(2)]
                m = jnp.maximum(jnp.max(s[0], axis=1, keepdims=True), jnp.max(s[1], axis=1, keepdims=True))
                p = [jnp.exp(s[b] - m) for b in range(2)]
                den = jnp.sum(p[0], axis=1, keepdims=True) + jnp.sum(p[1], axis=1, keepdims=True)
                ps[pp, hp, h] = [p[b].astype(BF16) for b in range(2)]
                inv[pp, hp, h] = 1.0 / den
                lse[pp, hp, h] = m + jnp.log(den)
        for pp, hp in units:
            o = jnp.zeros((BLOCK, LANES), F32)
            for b, v_ref in enumerate((vc_ref, vp_ref)):
                o = o + _dot(jnp.concatenate([ps[pp, hp, 0][b], ps[pp, hp, 1][b]], axis=1),
                             _by_head(v_ref[pp, :, _lanes(hp)], first))
            o_ref[pp, :, _lanes(hp)] = o * jnp.where(first, inv[pp, hp, 0], inv[pp, hp, 1])
            l_ref[pp, :, _lanes(hp)] = jnp.where(first, lse[pp, hp, 0], lse[pp, hp, 1])
        if finish is not None:
            finish()

    cur = pl.BlockSpec((PL, BLOCK, GROUP), lambda c, n: (c, n, 0))
    prev = pl.BlockSpec((PL, BLOCK, GROUP), lambda c, n: (c, jnp.maximum(n - 1, 0), 0))
    h_in, h_out, h_shape, h_sems = _hosted_specs(slots)
    return pl.pallas_call(
        body, name="dil_fwd_r%d" % r, grid=(r // PL, nb),
        in_specs=[cur, cur, prev, cur, prev] + h_in, out_specs=[cur, cur] + h_out,
        out_shape=[jax.ShapeDtypeStruct(q.shape, F32)] * 2 + h_shape,
        input_output_aliases={5 + a: 2 + a for a in range(ns)},
        scratch_shapes=h_sems,
    )(q, k, k, v, v, *slots)


def _dil_bwd(q, k, v, do, lse, delta, part):
    r, L, _ = q.shape
    nb = L // BLOCK
    P = GROUP // LANES
    PL = min(DIL_PLANES, r)
    scale = HEAD_DIM ** -0.5
    units = [(pp, hp) for pp in range(PL) for hp in range(P)]

    def body(qc_ref, qn_ref, doc_ref, don_ref, lc_ref, ln_ref, dc_ref, dn_ref, k_ref, v_ref, part_in,
             dq_ref, dk_ref, dv_ref, part_out, carry, send, recv):
        j = pl.program_id(1)
        step = pl.program_id(0) * nb + j
        start, finish = _chip_send_plan(part_in, part_out, send, recv)
        pl.when(step == 0)(start)
        rowi = lax.broadcasted_iota(jnp.int32, (BLOCK, BLOCK), 0)
        coli = lax.broadcasted_iota(jnp.int32, (BLOCK, BLOCK), 1)
        first = coli < HEAD_DIM
        sides = ((qc_ref, doc_ref, lc_ref, dc_ref, coli <= rowi),
                 (qn_ref, don_ref, ln_ref, dn_ref, jnp.logical_and(coli >= rowi, j < nb - 1)))

        @pl.when(j == 0)
        def _():
            carry[...] = jnp.zeros_like(carry)

        kcat, q2, do2, s2, dp2 = {}, {}, {}, {}, {}
        for pp, hp in units:
            kcat[pp, hp] = _by_head(k_ref[pp, :, _lanes(hp)], first)
            vcat = _by_head(v_ref[pp, :, _lanes(hp)], first)
            for x, (q_r, do_r, _, _, _) in enumerate(sides):
                q2[pp, hp, x] = _scaled(q_r[pp, :, _lanes(hp)])
                do2[pp, hp, x] = do_r[pp, :, _lanes(hp)]
                s2[pp, hp, x] = _dot_nt(q2[pp, hp, x], kcat[pp, hp])
                dp2[pp, hp, x] = _dot_nt(do2[pp, hp, x], vcat)
        pcat, dscat = {}, {}
        for pp, hp in units:
            for x, (_, _, l_r, d_r, msk) in enumerate(sides):
                ps, dss = [], []
                for h in range(2):
                    col = hp * LANES + h * HEAD_DIM
                    half = slice(h * BLOCK, (h + 1) * BLOCK)
                    p = jnp.where(msk, jnp.exp(s2[pp, hp, x][:, half] - l_r[pp, :, col:col + 1]), 0.0)
                    ps.append(p.astype(BF16))
                    dss.append((p * (dp2[pp, hp, x][:, half] - d_r[pp, :, col:col + 1])).astype(BF16))
                pcat[pp, hp, x] = jnp.concatenate(ps, axis=1)
                dscat[pp, hp, x] = jnp.concatenate(dss, axis=1)
        for pp, hp in units:
            dv2 = _dot_tn(pcat[pp, hp, 0], do2[pp, hp, 0]) + _dot_tn(pcat[pp, hp, 1], do2[pp, hp, 1])
            dk2 = _dot_tn(dscat[pp, hp, 0], q2[pp, hp, 0]) + _dot_tn(dscat[pp, hp, 1], q2[pp, hp, 1])
            dv_ref[pp, :, _lanes(hp)] = jnp.where(first, dv2[:BLOCK], dv2[BLOCK:]).astype(BF16)
            dk_ref[pp, :, _lanes(hp)] = jnp.where(first, dk2[:BLOCK], dk2[BLOCK:]).astype(BF16)
            dq_ref[pp, :, _lanes(hp)] = (carry[pp, :, _lanes(hp)]
                                         + _dot(dscat[pp, hp, 0], kcat[pp, hp]) * scale).astype(BF16)
            carry[pp, :, _lanes(hp)] = _dot(dscat[pp, hp, 1], kcat[pp, hp]) * scale
        pl.when(step == (r // PL) * nb - 1)(finish)

    cur = pl.BlockSpec((PL, BLOCK, GROUP), lambda c, n: (c, n, 0))
    nxt = pl.BlockSpec((PL, BLOCK, GROUP), lambda c, n: (c, jnp.minimum(n + 1, nb - 1), 0))
    return pl.pallas_call(
        body, name="dil_bwd_r%d" % r, grid=(r // PL, nb),
        in_specs=[cur, nxt, cur, nxt, cur, nxt, cur, nxt, cur, cur, HBM], out_specs=[cur, cur, cur, HBM],
        out_shape=[jax.ShapeDtypeStruct(q.shape, BF16)] * 3 + [jax.ShapeDtypeStruct(part.shape, part.dtype)],
        scratch_shapes=[pltpu.VMEM((PL, BLOCK, GROUP), F32), pltpu.SemaphoreType.DMA((3,)),
                        pltpu.SemaphoreType.DMA((3,))],
    )(q, q, do, do, lse, lse, delta, delta, k, v, part)


SB_TILES = 2
SB_PAIRS_FWD = 4
SB_PAIRS_BWD = 4
SB_DEAD = -110.0


def _lanes(hp):
    return slice(hp * LANES, (hp + 1) * LANES)


def _sb_logits(z, valid):
    e = jnp.exp(-jnp.abs(z))
    lb = jnp.minimum(z, 0.0) - jnp.log(1.0 + e)
    lk = lb - z
    if valid is not None:
        lk = jnp.where(valid, lk, 0.0)
    return e, lb, lk


def _by_head(t, first):
    zero = jnp.zeros_like(t)
    return jnp.concatenate([jnp.where(first, t, zero), jnp.where(first, zero, t)], axis=0)


def _sb_valid(i, j):
    rowi = lax.broadcasted_iota(jnp.int32, (BLOCK, BLOCK), 0)
    coli = lax.broadcasted_iota(jnp.int32, (BLOCK, BLOCK), 1)
    return (coli - rowi) < (i - j) * BLOCK


def _scaled(q):
    return (q.astype(F32) * (HEAD_DIM ** -0.5)).astype(BF16)


def _hosted_gathers(refs_in, refs_out, send, recv, step, steps):
    plans = [_gather_plan(refs_in[a], refs_out[a], send.at[pl.ds(6 * a, 6)], recv.at[pl.ds(6 * a, 6)])
             for a in range(len(refs_in))]
    for stage, at in ((0, 0), (1, (2 * steps) // 3)):
        @pl.when(step == at)
        def _():
            for plan in plans:
                plan[stage]()

    def finish():
        @pl.when(step == steps - 1)
        def _():
            for plan in plans:
                plan[2]()

    return finish


def _hosted_specs(slots):
    n = len(slots)
    sems = [pltpu.SemaphoreType.DMA((6 * n,))] * 2 if n else []
    return [HBM] * n, [HBM] * n, [jax.ShapeDtypeStruct(s.shape, s.dtype) for s in slots], sems


def _sb_fwd(qs, ks, vs, tri_later, slots):
    S = qs.shape[0]
    P = SB_PAIRS_FWD
    W = P * LANES
    ns = len(slots)

    def body(q_ref, k_ref, v_ref, u_ref, *rest):
        o_ref, lt_ref, from_ref = rest[ns:ns + 3]
        i = pl.program_id(1)
        finish = _hosted_gathers(rest[:ns], rest[ns + 3:2 * ns + 3], *rest[2 * ns + 3:], i, S // BLOCK) if ns else None
        first = lax.broadcasted_iota(jnp.int32, (BLOCK, LANES), 1) < HEAD_DIM
        q2 = [_scaled(q_ref[:, _lanes(hp)]) for hp in range(P)]

        def walk(tiles, carry):
            runs, accs = list(carry[0]), list(carry[1])
            units = [(t, hp) for t in range(len(tiles)) for hp in range(P)]
            offs = [pl.multiple_of(j * BLOCK, BLOCK) for j, _ in tiles]
            valids = [_sb_valid(i, j) if diag else None for j, diag in tiles]
            z2s, lbs, c2s = {}, {}, {}
            for t, hp in units:
                z2s[t, hp] = _dot_nt(q2[hp], _by_head(k_ref[pl.ds(offs[t], BLOCK), _lanes(hp)], first))
            for t, hp in units:
                for h in range(2):
                    _, lb, lk = _sb_logits(z2s[t, hp][:, h * BLOCK:(h + 1) * BLOCK], valids[t])
                    lbs[t, hp, h] = lb
                    c2s[t, hp, h] = _dot(jnp.concatenate(_split(lk), axis=1), u_ref[...])
            for t, hp in units:
                a2 = []
                for h in range(2):
                    a = jnp.exp(lbs[t, hp, h] + c2s[t, hp, h][:, :BLOCK] + runs[2 * hp + h])
                    if valids[t] is not None:
                        a = jnp.where(valids[t], a, 0.0)
                    a2.append(a.astype(BF16))
                    runs[2 * hp + h] = runs[2 * hp + h] + c2s[t, hp, h][:, BLOCK:]
                vcat = _by_head(v_ref[pl.ds(offs[t], BLOCK), _lanes(hp)], first)
                accs[hp] = accs[hp] + _dot(jnp.concatenate(a2, axis=1), vcat)
            return tuple(runs), tuple(accs)

        def chunk(ci, carry):
            return walk([(ci * SB_TILES + t, False) for t in reversed(range(SB_TILES))], carry)

        def alive(runs):
            top = functools.reduce(jnp.maximum, runs)
            return (jnp.max(top) > SB_DEAD).astype(jnp.int32)

        def step(c):
            t, _, runs, accs = c
            runs, accs = chunk(nfull - 1 - t, (runs, accs))
            return t + 1, alive(runs), runs, accs

        zero = jnp.zeros((BLOCK, LANES), F32)
        nfull = i // SB_TILES
        ragged = [functools.partial(walk, [(i, True)] + [(i - 1 - m, False) for m in range(extra)])
                  for extra in range(SB_TILES)]
        runs, accs = lax.switch(i % SB_TILES, ragged, ((zero,) * (2 * P), (zero,) * P))
        done, _, runs, accs = lax.while_loop(lambda c: jnp.logical_and(c[0] < nfull, c[1] > 0), step,
                                             (jnp.int32(0), alive(runs), runs, accs))
        for hp in range(P):
            o_ref[:, _lanes(hp)] = accs[hp]
            lt_ref[:, _lanes(hp)] = jnp.where(first, runs[2 * hp], runs[2 * hp + 1])
        from_ref[...] = jnp.full(from_ref.shape, nfull - done, jnp.int32)
        if finish is not None:
            finish()

    assert W == GROUP
    blk = pl.BlockSpec((BLOCK, W), lambda hp, i: (i, hp))
    col = pl.BlockSpec((S, W), lambda hp, i: (0, hp))
    h_in, h_out, h_shape, h_sems = _hosted_specs(slots)
    return pl.pallas_call(
        body, name="sb_fwd", grid=(GROUP // W, S // BLOCK),
        in_specs=[blk, col, col, _full((2 * BLOCK, 2 * BLOCK))] + h_in,
        out_specs=[blk, blk, pl.BlockSpec((1, 8, LANES), lambda hp, i: (i, 0, 0))] + h_out,
        out_shape=[jax.ShapeDtypeStruct((S, GROUP), F32)] * 2
                  + [jax.ShapeDtypeStruct((S // BLOCK, 8, LANES), jnp.int32)] + h_shape,
        input_output_aliases={4 + a: 3 + a for a in range(ns)},
        scratch_shapes=h_sems,
    )(qs, ks, vs, tri_later, *slots)


def _sb_bwd(first_chunk, qs, ks, vs, do, ltot, tri_upto, tri_before):
    S = qs.shape[0]
    P = SB_PAIRS_BWD
    W = P * LANES

    def body(from_ref, q_ref, k_ref, v_ref, do_ref, lt_ref, w_ref, x_ref, dq_ref, dk_ref, dv_ref):
        i = pl.program_id(1)

        @pl.when(i == 0)
        def _():
            dk_ref[...] = jnp.zeros_like(dk_ref)
            dv_ref[...] = jnp.zeros_like(dv_ref)

        first = lax.broadcasted_iota(jnp.int32, (BLOCK, LANES), 1) < HEAD_DIM
        q2 = [_scaled(q_ref[:, _lanes(hp)]) for hp in range(P)]
        do2 = [do_ref[:, _lanes(hp)] for hp in range(P)]
        totals = [jnp.broadcast_to(lt_ref[:, n * HEAD_DIM:n * HEAD_DIM + 1], (BLOCK, LANES)) for n in range(2 * P)]

        def walk(tiles, carry):
            keeps, grads, dqs = list(carry[0]), list(carry[1]), list(carry[2])
            units = [(t, hp) for t in range(len(tiles)) for hp in range(P)]
            offs = [pl.multiple_of(j * BLOCK, BLOCK) for j, _ in tiles]
            valids = [_sb_valid(i, j) if diag else None for j, diag in tiles]
            kcat, z2, da2, es, lbs, c2s, as_, des, p2s = {}, {}, {}, {}, {}, {}, {}, {}, {}
            for t, hp in units:
                kcat[t, hp] = _by_head(k_ref[pl.ds(offs[t], BLOCK), _lanes(hp)], first)
                z2[t, hp] = _dot_nt(q2[hp], kcat[t, hp])
                da2[t, hp] = _dot_nt(do2[hp], _by_head(v_ref[pl.ds(offs[t], BLOCK), _lanes(hp)], first))
            for t, hp in units:
                for h in range(2):
                    es[t, hp, h], lbs[t, hp, h], lk = _sb_logits(z2[t, hp][:, h * BLOCK:(h + 1) * BLOCK], valids[t])
                    c2s[t, hp, h] = _dot(jnp.concatenate(_split(lk), axis=1), w_ref[...])
            for t, hp in units:
                for h in range(2):
                    n = 2 * hp + h
                    a = jnp.exp(lbs[t, hp, h] + (totals[n] - (keeps[n] + c2s[t, hp, h][:, :BLOCK])))
                    if valids[t] is not None:
                        a = jnp.where(valids[t], a, 0.0)
                    keeps[n] = keeps[n] + c2s[t, hp, h][:, BLOCK:]
                    de = a * da2[t, hp][:, h * BLOCK:(h + 1) * BLOCK]
                    as_[t, hp, h], des[t, hp, h] = a.astype(BF16), de
                    p2s[t, hp, h] = _dot(jnp.concatenate(_split(de), axis=1), x_ref[...])
            for t, hp in units:
                dz2 = []
                for h in range(2):
                    n = 2 * hp + h
                    e = es[t, hp, h]
                    sig = jnp.where(z2[t, hp][:, h * BLOCK:(h + 1) * BLOCK] >= 0.0, 1.0, e) / (1.0 + e)
                    dz = des[t, hp, h] * (1.0 - sig) - (grads[n] + p2s[t, hp, h][:, :BLOCK]) * sig
                    if valids[t] is not None:
                        dz = jnp.where(valids[t], dz, 0.0)
                    grads[n] = grads[n] + p2s[t, hp, h][:, BLOCK:]
                    dz2.append(dz.astype(BF16))
                dzcat = jnp.concatenate(dz2, axis=1)
                dk2 = _dot_tn(dzcat, q2[hp])
                dv2 = _dot_tn(jnp.concatenate([as_[t, hp, 0], as_[t, hp, 1]], axis=1), do2[hp])
                dk_ref[pl.ds(offs[t], BLOCK), _lanes(hp)] += jnp.where(first, dk2[:BLOCK], dk2[BLOCK:])
                dv_ref[pl.ds(offs[t], BLOCK), _lanes(hp)] += jnp.where(first, dv2[:BLOCK], dv2[BLOCK:])
                dqs[hp] = dqs[hp] + _dot(dzcat, kcat[t, hp])
            return tuple(keeps), tuple(grads), tuple(dqs)

        zero = jnp.zeros((BLOCK, LANES), F32)
        nfull = i // SB_TILES
        carry = lax.fori_loop(
            from_ref[i], nfull, lambda ci, c: walk([(ci * SB_TILES + t, False) for t in range(SB_TILES)], c),
            ((zero,) * (2 * P), (zero,) * (2 * P), (zero,) * P))
        ragged = [functools.partial(walk, [(i - m, False) for m in range(extra, 0, -1)] + [(i, True)])
                  for extra in range(SB_TILES)]
        carry = lax.switch(i % SB_TILES, ragged, carry)
        for hp in range(P):
            dq_ref[:, _lanes(hp)] = carry[2][hp] * (HEAD_DIM ** -0.5)

    blk = pl.BlockSpec((BLOCK, W), lambda hp, i, fr: (i, hp))
    col = pl.BlockSpec((S, W), lambda hp, i, fr: (0, hp))
    tri = pl.BlockSpec((2 * BLOCK, 2 * BLOCK), lambda hp, i, fr: (0, 0))
    return pl.pallas_call(
        body, name="sb_bwd",
        grid_spec=pltpu.PrefetchScalarGridSpec(
            num_scalar_prefetch=1, grid=(GROUP // W, S // BLOCK),
            in_specs=[blk, col, col, blk, blk, tri, tri], out_specs=[blk, col, col]),
        out_shape=[jax.ShapeDtypeStruct((S, GROUP), F32)] * 3,
    )(first_chunk, qs, ks, vs, do, ltot, tri_upto, tri_before)


def _out_proj_fwd(o_br, l_br, o_sb, x, w_dil, w_sbn, w_out_g, slots):
    S, D = x.shape
    tm = 512
    ns = len(slots)

    def body(o0, o1, o2, l0, l1, l2, os_ref, x_ref, wd_ref, ws_ref, w_ref, *rest):
        od_ref, s0, s1, s2, x1_ref = rest[ns:ns + 5]
        scr = rest[2 * ns + 5]
        finish = (_hosted_gathers(rest[:ns], rest[ns + 5:2 * ns + 5], *rest[2 * ns + 6:], pl.program_id(0), S // tm)
                  if ns else None)
        ls = [_from_strided(scr, l) for l in (l0, l1, l2)]
        os_ = [_from_strided(scr, o) for o in (o0, o1, o2)]
        m = jnp.maximum(jnp.maximum(ls[0], ls[1]), ls[2])
        es = [jnp.exp(l - m) for l in ls]
        den = es[0] + es[1] + es[2]
        od = (es[0] * os_[0] + es[1] * os_[1] + es[2] * os_[2]) / den
        od_ref[...] = od
        _to_strided(scr, m + jnp.log(den), list(zip(DILATIONS, (s0, s1, s2))))
        osb = os_ref[...]
        mixed = jnp.concatenate([(od * _rms(od) * wd_ref[...]).astype(BF16),
                                 (osb * _rms(osb) * ws_ref[...]).astype(BF16)], axis=1)
        x1_ref[...] = x_ref[...] + _dot(mixed, w_ref[...])
        if finish is not None:
            finish()

    row = lambda i: (i, 0)
    g = pl.BlockSpec((tm, GROUP), row)
    d = pl.BlockSpec((tm, D), row)
    planes = [_strided_spec(tm, r) for r in DILATIONS]
    h_in, h_out, h_shape, h_sems = _hosted_specs(slots)
    return pl.pallas_call(
        body, name="out_proj_fwd", grid=(S // tm,),
        in_specs=planes * 2 + [g, d, _full((1, GROUP)), _full((1, GROUP)), _full((2 * GROUP, D))] + h_in,
        out_specs=[g] + planes + [d] + h_out,
        out_shape=[jax.ShapeDtypeStruct((S, GROUP), F32)] + [_strided_shape(S, r, F32) for r in DILATIONS]
                  + [jax.ShapeDtypeStruct((S, D), F32)] + h_shape,
        input_output_aliases={11 + a: 5 + a for a in range(ns)},
        scratch_shapes=[_strided_scratch(tm)] + h_sems,
    )(*o_br, *l_br, o_sb, x, w_dil, w_sbn, w_out_g, *slots)


def _ffn_fwd(x1, target, ffn_w, wg_g, wu_g, wd_g):
    S, D = x1.shape
    F = wg_g.shape[1]
    tm = 512
    nt = S // tm

    def body(x_ref, t_ref, nw_ref, wg_ref, wu_ref, wd_ref, h_ref, g_ref, u_ref, dy_ref, loss_ref, h_s, acc):
        j = pl.program_id(1)

        @pl.when(j == 0)
        def _():
            xv = x_ref[...]
            h = (xv * _rms(xv) * nw_ref[...]).astype(BF16)
            h_s[...] = h
            h_ref[...] = h
            acc[...] = xv

        h = h_s[...]
        g = _dot_nt(h, wg_ref[0])
        u = _dot_nt(h, wu_ref[0])
        g_ref[0] = g.astype(BF16)
        u_ref[0] = u.astype(BF16)
        a = (g * _sigmoid(g) * u).astype(BF16)
        acc[...] += _dot(a, wd_ref[0])

        @pl.when(j == N_CHIPS - 1)
        def _():
            err = acc[...] - t_ref[...]
            dy_ref[...] = err * (1.0 / D)
            loss_ref[...] = jnp.full(loss_ref.shape, jnp.sum(err * err), F32)

    row = lambda t, j: (t, 0)
    shard = lambda t, j: (j, 0, 0)
    act = lambda t, j: (j, t, 0)
    return pl.pallas_call(
        body, name="ffn_fwd", grid=(nt, N_CHIPS),
        in_specs=[pl.BlockSpec((tm, D), row), pl.BlockSpec((tm, D), row), pl.BlockSpec((1, D), lambda t, j: (0, 0))]
                 + [pl.BlockSpec((1, F, D), shard)] * 3,
        out_specs=[pl.BlockSpec((tm, D), row), pl.BlockSpec((1, tm, F), act), pl.BlockSpec((1, tm, F), act),
                   pl.BlockSpec((tm, D), row), pl.BlockSpec((1, 8, LANES), lambda t, j: (t, 0, 0))],
        out_shape=[jax.ShapeDtypeStruct((S, D), BF16), jax.ShapeDtypeStruct((N_CHIPS, S, F), BF16),
                   jax.ShapeDtypeStruct((N_CHIPS, S, F), BF16), jax.ShapeDtypeStruct((S, D), F32),
                   jax.ShapeDtypeStruct((nt, 8, LANES), F32)],
        scratch_shapes=[pltpu.VMEM((tm, D), BF16), pltpu.VMEM((tm, D), F32)],
    )(x1, target, ffn_w, wg_g, wu_g, wd_g)


def _ffn_bwd(h2, dy, g, u, wg_g, wu_g, wd_g):
    S, D = dy.shape
    F = wg_g.shape[1]
    tm = 512

    def body(h_ref, dy_ref, g_ref, u_ref, wg_ref, wu_ref, wd_ref, dwg_ref, dwu_ref, dwd_ref, dh_ref, *narrow):
        t = pl.program_id(1)

        @pl.when(t == 0)
        def _():
            dwg_ref[...] = jnp.zeros_like(dwg_ref)
            dwu_ref[...] = jnp.zeros_like(dwu_ref)
            dwd_ref[...] = jnp.zeros_like(dwd_ref)

        h = h_ref[...]
        dyb = dy_ref[...].astype(BF16)
        gv = g_ref[0].astype(F32)
        uv = u_ref[0].astype(F32)
        da = _dot_nt(dyb, wd_ref[0])
        sg = _sigmoid(gv)
        silu = gv * sg
        du = (da * silu).astype(BF16)
        dg = (da * uv * (sg * (1.0 + gv * (1.0 - sg)))).astype(BF16)
        dwd_ref[0] += _dot_tn((silu * uv).astype(BF16), dyb)
        dwg_ref[0] += _dot_tn(dg, h)
        dwu_ref[0] += _dot_tn(du, h)
        dh_ref[0] = (_dot(dg, wg_ref[0]) + _dot(du, wu_ref[0])).astype(BF16)

        @pl.when(t == S // tm - 1)
        def _():
            for full, half in zip((dwg_ref, dwu_ref, dwd_ref), narrow):
                half[...] = full[...].astype(BF16)

    row = lambda j, t: (t, 0)
    shard = lambda j, t: (j, 0, 0)
    act = lambda j, t: (j, t, 0)
    return pl.pallas_call(
        body, name="ffn_bwd", grid=(N_CHIPS, S // tm),
        in_specs=[pl.BlockSpec((tm, D), row), pl.BlockSpec((tm, D), row),
                  pl.BlockSpec((1, tm, F), act), pl.BlockSpec((1, tm, F), act)] + [pl.BlockSpec((1, F, D), shard)] * 3,
        out_specs=[pl.BlockSpec((1, F, D), shard)] * 3 + [pl.BlockSpec((1, tm, D), act)]
                  + [pl.BlockSpec((1, F, D), shard)] * 3,
        out_shape=[jax.ShapeDtypeStruct((N_CHIPS, F, D), F32)] * 3 + [jax.ShapeDtypeStruct((N_CHIPS, S, D), BF16)]
                  + [jax.ShapeDtypeStruct((N_CHIPS, F, D), BF16)] * 3,
    )(h2, dy, g, u, wg_g, wu_g, wd_g)


def _out_proj_bwd(dh2p, dy, x1, ffn_w, w_out_g, o_dil, o_sb, w_dil, w_sbn, seg_ones, ffn_grads):
    S, D = dy.shape
    tm = 512
    ng = len(ffn_grads)

    def body(dh_ref, dy_ref, x1_ref, nw_ref, w_ref, od_ref, os_ref, wd_ref, ws_ref, g_ref, *rest):
        gin, rest = rest[:ng], rest[ng:]
        dx1_ref, dod0, dod1, dod2, dos_ref, dl0, dl1, dl2, dw_ref, dnw_ref, dwd_ref, dws_ref = rest[:12]
        gout, (scr, send, recv) = rest[12:12 + ng], rest[12 + ng:]
        i = pl.program_id(0)
        plans = [_pair_send_plan(gin[a], gout[a], send.at[a], recv.at[a]) for a in range(ng)]

        @pl.when(i == 0)
        def _():
            for start, _ in plans:
                start()

        @pl.when(i == 0)
        def _():
            for r_ in (dw_ref, dnw_ref, dwd_ref, dws_ref):
                r_[...] = jnp.zeros_like(r_)

        dh2 = _sum4(dh_ref)
        dxn, dwn = _rms_bwd(dh2, x1_ref[...], nw_ref[...])
        dnw_ref[...] += jnp.sum(dwn, axis=0, keepdims=True)
        dx1 = dy_ref[...] + dxn
        dx1_ref[...] = dx1
        dx1b = dx1.astype(BF16)
        dmix = _dot_nt(dx1b, w_ref[...])
        od = od_ref[...]
        osb = os_ref[...]
        mixed = jnp.concatenate([(od * _rms(od) * wd_ref[...]).astype(BF16),
                                 (osb * _rms(osb) * ws_ref[...]).astype(BF16)], axis=1)
        dw_ref[...] += _dot_tn(mixed, dx1b)
        do, dwo = _rms_bwd(dmix[:, :GROUP], od, wd_ref[...])
        dwd_ref[...] += jnp.sum(dwo, axis=0, keepdims=True)
        _to_strided(scr, do, list(zip(DILATIONS, (dod0, dod1, dod2))))
        _to_strided(scr, _segsum(do * od, g_ref[...]), list(zip(DILATIONS, (dl0, dl1, dl2))))
        do, dwo = _rms_bwd(dmix[:, GROUP:], osb, ws_ref[...])
        dws_ref[...] += jnp.sum(dwo, axis=0, keepdims=True)
        dos_ref[...] = do.astype(BF16)

        @pl.when(i == S // tm - 1)
        def _():
            for _, finish in plans:
                finish()

    row = lambda i: (i, 0)
    gsp = pl.BlockSpec((tm, GROUP), row)
    dsp = pl.BlockSpec((tm, D), row)
    planes = [_strided_spec(tm, r) for r in DILATIONS]
    halves = [jax.ShapeDtypeStruct((g.shape[0], g.shape[1] // 2, g.shape[2]), g.dtype) for g in ffn_grads]
    return pl.pallas_call(
        body, name="out_proj_bwd", grid=(S // tm,),
        in_specs=[pl.BlockSpec((N_CHIPS, tm, D), lambda i: (0, i, 0)), dsp, dsp, _full((1, D)), _full((2 * GROUP, D)),
                  gsp, gsp, _full((1, GROUP)), _full((1, GROUP)), _full((GROUP, GROUP // 2))] + [HBM] * ng,
        out_specs=[dsp] + planes + [gsp] + planes
                  + [_full((2 * GROUP, D)), _full((1, D)), _full((1, GROUP)), _full((1, GROUP))] + [HBM] * ng,
        out_shape=[jax.ShapeDtypeStruct((S, D), F32)] + [_strided_shape(S, r, BF16) for r in DILATIONS]
                  + [jax.ShapeDtypeStruct((S, GROUP), BF16)] + [_strided_shape(S, r, F32) for r in DILATIONS]
                  + [jax.ShapeDtypeStruct((2 * GROUP, D), F32),
                     jax.ShapeDtypeStruct((1, D), F32), jax.ShapeDtypeStruct((1, GROUP), F32),
                     jax.ShapeDtypeStruct((1, GROUP), F32)] + halves,
        scratch_shapes=[_strided_scratch(tm), pltpu.SemaphoreType.DMA((ng,)), pltpu.SemaphoreType.DMA((ng,))],
    )(dh2p, dy, x1, ffn_w, w_out_g, o_dil, o_sb, w_dil, w_sbn, seg_ones, *ffn_grads)


def _attn_in_bwd(dq_br, dk_br, dv_br, dqs, dks, dvs, qa, ka, qw, kw, cos_t, sin_t, seg_ones, h, w_in_g, x, dx1, attn_w):
    S, D = x.shape
    wc = w_in_g.shape[2]
    tm = 256

    def body(q0, q1, q2, k0, k1, k2, v0, v1, v2, dqs_ref, dks_ref, dvs_ref, qa_ref, ka_ref, qw_ref, kw_ref,
             cos_ref, sin_ref, g_ref, h_ref, w_ref, x_ref, dx1_ref, aw_ref,
             gx_ref, dw_ref, daw_ref, dqw_ref, dkw_ref, accq, acck, scr):
        i = pl.program_id(0)

        @pl.when(i == 0)
        def _():
            accq[...] = jnp.zeros_like(accq)
            acck[...] = jnp.zeros_like(acck)
            dw_ref[...] = jnp.zeros_like(dw_ref)
            daw_ref[...] = jnp.zeros_like(daw_ref)

        def branches(refs):
            return (_from_strided(scr, refs[0]) + _from_strided(scr, refs[1])) + _from_strided(scr, refs[2])

        g = g_ref[...]
        cos = _tile4(cos_ref[...])
        sin = _tile4(sin_ref[...])
        pieces = []
        for refs, pre_ref, w_r, acc in (((q0, q1, q2), qa_ref, qw_ref, accq), ((k0, k1, k2), ka_ref, kw_ref, acck)):
            dh = branches(refs)
            dn = dh * cos + _rot_half(dh * sin)
            pre = pre_ref[...]
            rstd = lax.rsqrt(_segsum(pre * pre, g) * (1.0 / HEAD_DIM) + EPS)
            xh = pre * rstd
            acc[...] += jnp.sum(dn * xh, axis=0, keepdims=True)
            dxh = dn * w_r[...]
            pieces.append((rstd * (dxh - xh * (_segsum(dxh * xh, g) * (1.0 / HEAD_DIM)))).astype(BF16))
        pieces += [branches((v0, v1, v2)).astype(BF16), dqs_ref[...].astype(BF16), dks_ref[...].astype(BF16),
                   dvs_ref[...].astype(BF16)]
        dproj = jnp.concatenate(pieces, axis=1)
        hv = h_ref[...]
        dh = jnp.zeros((tm, D), F32)
        for j in range(N_CHIPS):
            dp = dproj[:, j * wc:(j + 1) * wc]
            dw_ref[j] += _dot_tn(hv, dp)
            dh = dh + _dot_nt(dp, w_ref[j])
        dx, dw = _rms_bwd(dh, x_ref[...], aw_ref[...])
        daw_ref[...] += jnp.sum(dw, axis=0, keepdims=True)
        gx_ref[...] = dx1_ref[...] + dx

        @pl.when(i == S // tm - 1)
        def _():
            for acc, o_ref in ((accq, dqw_ref), (acck, dkw_ref)):
                a = acc[...]
                pair = (a[:, 0:LANES] + a[:, LANES:2 * LANES]) + (a[:, 2 * LANES:3 * LANES] + a[:, 3 * LANES:4 * LANES])
                o_ref[...] = pair + pltpu.roll(pair, HEAD_DIM, 1)

    row = lambda i: (i, 0)
    gsp = pl.BlockSpec((tm, GROUP), row)
    dsp = pl.BlockSpec((tm, D), row)
    tab = pl.BlockSpec((tm, LANES), row)
    planes = [_strided_spec(tm, r) for r in DILATIONS]
    return pl.pallas_call(
        body, name="attn_in_bwd", grid=(S // tm,),
        in_specs=planes * 3 + [gsp] * 5 + [_full((1, GROUP)), _full((1, GROUP)), tab, tab, _full((GROUP, GROUP // 2)),
                                          dsp, _full((N_CHIPS, D, wc)), dsp, dsp, _full((1, D))],
        out_specs=[dsp, _full((N_CHIPS, D, wc)), _full((1, D)), _full((1, LANES)), _full((1, LANES))],
        out_shape=[jax.ShapeDtypeStruct((S, D), F32), jax.ShapeDtypeStruct((N_CHIPS, D, wc), F32),
                   jax.ShapeDtypeStruct((1, D), F32), jax.ShapeDtypeStruct((1, LANES), F32),
                   jax.ShapeDtypeStruct((1, LANES), F32)],
        scratch_shapes=[pltpu.VMEM((1, GROUP), F32), pltpu.VMEM((1, GROUP), F32), _strided_scratch(tm)],
    )(*dq_br, *dk_br, *dv_br, dqs, dks, dvs, qa, ka, qw, kw, cos_t, sin_t, seg_ones, h, w_in_g, x, dx1, attn_w)


def _constants(S):
    pos = jnp.arange(S, dtype=F32)
    inv_freq = ROPE_THETA ** (-jnp.arange(0, HEAD_DIM, 2, dtype=F32) / HEAD_DIM)
    ang_a = pos[::BLOCK, None] * inv_freq[None, :]
    ang_b = pos[:BLOCK, None] * inv_freq[None, :]
    ca, sa, cb, sb = jnp.cos(ang_a)[:, None], jnp.sin(ang_a)[:, None], jnp.cos(ang_b)[None], jnp.sin(ang_b)[None]
    cos = (ca * cb - sa * sb).reshape(S, HEAD_DIM // 2)
    sin = (sa * cb + ca * sb).reshape(S, HEAD_DIM // 2)
    cos_t = jnp.concatenate([cos, cos] * 2, axis=1)
    sin_t = jnp.concatenate([-sin, sin] * 2, axis=1)
    idx = jnp.arange(GROUP // 2)
    seg_ones = (idx[:, None] // HEAD_DIM == idx[None, :] // HEAD_DIM).astype(BF16)
    seg_ones = jnp.concatenate([seg_ones, seg_ones], axis=0)
    r = jnp.arange(BLOCK)
    ones = jnp.ones((BLOCK, BLOCK), BF16)
    tris = [jnp.concatenate([jnp.concatenate([m.astype(BF16), ones], axis=1)] * 2, axis=0) for m in
            (r[:, None] > r[None, :],
             r[:, None] <= r[None, :],
             r[:, None] < r[None, :])]
    return cos_t, sin_t, seg_ones, tris


FFN_NAMES = ("w_gate", "w_up", "w_down")


def _device_step(x, target, attn_w, qn_w, kn_w, dil_w, sbn_w, ffn_w, w_in_g, w_out_slots, ffn_slots, core, chip):
    S = x.shape[0]
    cos_t, sin_t, seg_ones, (tri_later, tri_upto, tri_before) = _constants(S)
    reps = GROUP // HEAD_DIM
    qw = jnp.tile(qn_w, (1, reps))
    kw = jnp.tile(kn_w, (1, reps))

    nd = len(DILATIONS)
    h, qa, ka, *rest, wg_g = _in_proj_fwd(x, attn_w, w_in_g, qw, kw, cos_t, sin_t, seg_ones, ffn_slots[:1])
    qh, kh, va, (qs, ks, vs) = rest[:nd], rest[nd:2 * nd], rest[2 * nd:3 * nd], rest[3 * nd:]
    hosted = ([], [w_out_slots], [])
    branches = [_dil_fwd(qh[b], kh[b], va[b], hosted[b]) for b in range(nd)]
    w_out_g = branches[1][2].reshape(-1, x.shape[1])
    o_sb, ltot, walked, wu_g = _sb_fwd(qs, ks, vs, tri_later, ffn_slots[1:2])
    o_dil, *lse, x1, wd_g = _out_proj_fwd([b[0] for b in branches], [b[1] for b in branches], o_sb, x, dil_w, sbn_w,
                                          w_out_g, ffn_slots[2:])
    h2, g, u, dy, loss_parts = _ffn_fwd(x1, target, ffn_w, wg_g, wu_g, wd_g)

    *ffn_grads, dh2p, n0, n1, n2 = _ffn_bwd(h2, dy, g, u, wg_g, wu_g, wd_g)
    dx1, *mid, dw_out, dffn_w, ddil_w, dsbn_w, p0, p1, p2 = _out_proj_bwd(
        dh2p, dy, x1, ffn_w, w_out_g, o_dil, o_sb, dil_w, sbn_w, seg_ones, [n0, n1, n2])
    do_dil, do_sb, delta = mid[:nd], mid[nd], mid[nd + 1:]
    parts = [_pair_sum(gr, fr, core, n) for gr, fr, n in zip(ffn_grads, (p0, p1, p2), FFN_NAMES)]
    dqs, dks, dvs = _sb_bwd(walked[:, 0, 0], qs, ks, vs, do_sb, ltot, tri_upto, tri_before)
    dbr = [_dil_bwd(qh[b], kh[b], va[b], do_dil[b], lse[b], delta[b], parts[b]) for b in range(nd)]
    ffn_halves = [_chip_sum(dbr[b][3], parts[b], chip, FFN_NAMES[b]) for b in range(nd)]
    grad_x, dw_in, dattn_w, dqw, dkw = _attn_in_bwd(
        [b[0] for b in dbr], [b[1] for b in dbr], [b[2] for b in dbr], dqs, dks, dvs,
        qa, ka, qw, kw, cos_t, sin_t, seg_ones, h, w_in_g, x, dx1, attn_w)
    small = dict(attn=dattn_w, q=dqw[:, :HEAD_DIM], k=dkw[:, :HEAD_DIM], dil=ddil_w, sb=dsbn_w, ffn=dffn_w)
    return loss_parts, grad_x, small, dw_in, dw_out, ffn_halves


HBM = pl.BlockSpec(memory_space=pltpu.HBM)
VMEM = pl.BlockSpec(memory_space=pltpu.VMEM)
CHIP_FLIPS = ((1, 0), (0, 1), (1, 1))


def _place():
    return lax.axis_index("x"), lax.axis_index("y"), lax.axis_index("c")


def _flip(v, d):
    return 1 - v if d else v


def _half_rows(c, n):
    return pl.ds(pl.multiple_of(c * (n // 2), 16), n // 2)


def _gather_plan(slot_in, slot_out, send, recv):
    x, y, c = _place()
    p = 2 * x + y
    chips = [(_flip(x, dx), _flip(y, dy)) for dx, dy in CHIP_FLIPS]
    mine, other = _half_rows(c, slot_in.shape[1]), _half_rows(1 - c, slot_in.shape[1])

    def copy(k, src, dst, to):
        return pltpu.make_async_remote_copy(src_ref=src, dst_ref=dst, send_sem=send.at[k], recv_sem=recv.at[k],
                                            device_id=to, device_id_type=MESH)

    def first(k):
        return copy(k, slot_in.at[p, mine], slot_out.at[p, mine], (*chips[k], c))

    def passed(k, rows):
        land = slot_out.at[2 * chips[k][0] + chips[k][1], rows]
        return copy(3 + k, land, land, (x, y, 1 - c))

    def start():
        for k in range(3):
            first(k).start()

    def forward():
        for k in range(3):
            land = slot_out.at[2 * chips[k][0] + chips[k][1], mine]
            copy(k, land, land, (*chips[k], c)).wait_recv()
            passed(k, mine).start()

    def finish():
        for k in range(3):
            passed(k, other).wait_recv()
        for k in range(3):
            first(k).wait_send()
            passed(k, mine).wait_send()

    return start, forward, finish


def _chip_send_plan(part_in, recv_out, send, recv):
    x, y, c = _place()
    p = 2 * x + y
    chips = [(_flip(x, dx), _flip(y, dy)) for dx, dy in CHIP_FLIPS]

    def copy(k):
        q = 2 * chips[k][0] + chips[k][1]
        return pltpu.make_async_remote_copy(src_ref=part_in.at[q], dst_ref=recv_out.at[p], send_sem=send.at[k],
                                            recv_sem=recv.at[k], device_id=(*chips[k], c), device_id_type=MESH)

    def start():
        for k in range(3):
            copy(k).start()

    def finish():
        for k in range(3):
            land = recv_out.at[2 * chips[k][0] + chips[k][1]]
            pltpu.make_async_remote_copy(src_ref=land, dst_ref=land, send_sem=send.at[k], recv_sem=recv.at[k],
                                         device_id=(*chips[k], c), device_id_type=MESH).wait_recv()
        for k in range(3):
            copy(k).wait_send()

    return start, finish


def _pair_send_plan(grad_in, recv_out, send, recv):
    x, y, c = _place()

    def copy():
        theirs = _half_rows(1 - c, grad_in.shape[1])
        return pltpu.make_async_remote_copy(src_ref=grad_in.at[:, theirs, :], dst_ref=recv_out, send_sem=send,
                                            recv_sem=recv, device_id=(x, y, 1 - c), device_id_type=MESH)

    return (lambda: copy().start()), (lambda: copy().wait())


def _own_slots(shard):
    here = 2 * lax.axis_index("x") + lax.axis_index("y")
    return lax.dynamic_update_slice(lax.empty((N_CHIPS,) + shard.shape, shard.dtype), shard[None], (here, 0, 0))


def _gather_weights(shards):
    n = len(shards)

    def body(*refs):
        ins, outs = refs[:n], refs[n:2 * n]
        send, recv = refs[2 * n:]
        plans = [_gather_plan(ins[a], outs[a], send.at[pl.ds(6 * a, 6)], recv.at[pl.ds(6 * a, 6)]) for a in range(n)]
        for stage in range(3):
            for plan in plans:
                plan[stage]()

    slots = [_own_slots(s) for s in shards]
    return pl.pallas_call(
        body, name="gather_weights", in_specs=[HBM] * n, out_specs=[HBM] * n,
        out_shape=[jax.ShapeDtypeStruct(s.shape, s.dtype) for s in slots],
        input_output_aliases={a: a for a in range(n)},
        scratch_shapes=[pltpu.SemaphoreType.DMA((6 * n,)), pltpu.SemaphoreType.DMA((6 * n,))],
    )(*slots)


def _pair_exchange(grads, small):
    n = len(grads)

    def body(*refs):
        gin, sm = refs[:n], refs[n]
        gout, sm_all = refs[n + 1:2 * n + 1], refs[2 * n + 1]
        send, recv = refs[2 * n + 2:]
        x, y, c = _place()
        me = 4 * x + 2 * y + c
        big = [_pair_send_plan(gin[a], gout[a], send.at[a], recv.at[a]) for a in range(n)]
        for start, _ in big:
            start()
        sm_all[pl.ds(me, 1)] = sm[...][None]
        tiny = []
        for k in range(1, N_DEV):
            px, py, pc = _flip(x, k & 4), _flip(y, k & 2), _flip(c, k & 1)
            tiny.append((pltpu.make_async_remote_copy(
                src_ref=sm, dst_ref=sm_all.at[me], send_sem=send.at[n + k - 1], recv_sem=recv.at[n + k - 1],
                device_id=(px, py, pc), device_id_type=MESH), 4 * px + 2 * py + pc))
            tiny[-1][0].start()
        for k, (cp, peer) in enumerate(tiny):
            pltpu.make_async_remote_copy(src_ref=sm, dst_ref=sm_all.at[peer], send_sem=send.at[n + k],
                                         recv_sem=recv.at[n + k], device_id=(x, y, c),
                                         device_id_type=MESH).wait_recv()
            cp.wait_send()
        for _, finish in big:
            finish()

    halves = [jax.ShapeDtypeStruct((g.shape[0], g.shape[1] // 2, g.shape[2]), g.dtype) for g in grads]
    return pl.pallas_call(
        body, name="pair_exchange", in_specs=[HBM] * n + [VMEM], out_specs=[HBM] * n + [VMEM],
        out_shape=halves + [jax.ShapeDtypeStruct((N_DEV,) + small.shape, small.dtype)],
        scratch_shapes=[pltpu.SemaphoreType.DMA((n + N_DEV - 1,)), pltpu.SemaphoreType.DMA((n + N_DEV - 1,))],
    )(*grads, small)


def _chip_exchange(parts):
    n = len(parts)

    def body(*refs):
        pin, pout = refs[:n], refs[n:2 * n]
        send, recv = refs[2 * n:]
        plans = [_chip_send_plan(pin[a], pout[a], send.at[pl.ds(3 * a, 3)], recv.at[pl.ds(3 * a, 3)]) for a in range(n)]
        for stage in range(2):
            for plan in plans:
                plan[stage]()

    return pl.pallas_call(
        body, name="chip_exchange", in_specs=[HBM] * n, out_specs=[HBM] * n,
        out_shape=[jax.ShapeDtypeStruct(s.shape, s.dtype) for s in parts],
        scratch_shapes=[pltpu.SemaphoreType.DMA((3 * n,)), pltpu.SemaphoreType.DMA((3 * n,))],
    )(*parts)


def _pair_swap_plan(hin, hout, send, recv):
    x, y, c = _place()

    def copies():
        return [pltpu.make_async_remote_copy(src_ref=hin[a], dst_ref=hout[a], send_sem=send.at[a], recv_sem=recv.at[a],
                                             device_id=(x, y, 1 - c), device_id_type=MESH) for a in range(len(hin))]

    def start():
        for cp in copies():
            cp.start()

    def finish():
        for cp in copies():
            cp.wait()

    return start, finish


def _pair_swap(halves):
    n = len(halves)

    def body(*refs):
        start, finish = _pair_swap_plan(refs[:n], refs[n:2 * n], *refs[2 * n:])
        start()
        finish()

    return pl.pallas_call(
        body, name="pair_swap", in_specs=[HBM] * n, out_specs=[HBM] * n,
        out_shape=[jax.ShapeDtypeStruct(s.shape, s.dtype) for s in halves],
        scratch_shapes=[pltpu.SemaphoreType.DMA((n,)), pltpu.SemaphoreType.DMA((n,))],
    )(*halves)


def _pair_sum(grad, recv, c, tag):
    _, R, C = grad.shape
    hr = R // 2

    def body(c_ref, a_ref, b_ref, o_ref):
        o_ref[...] = (a_ref[...] + b_ref[...]).astype(BF16)

    return pl.pallas_call(
        body, name="pair_sum_" + tag,
        grid_spec=pltpu.PrefetchScalarGridSpec(
            num_scalar_prefetch=1, grid=(N_CHIPS,),
            in_specs=[pl.BlockSpec((1, hr, C), lambda s, cr: (s, cr[0], 0)),
                      pl.BlockSpec((1, hr, C), lambda s, cr: (s, 0, 0))],
            out_specs=pl.BlockSpec((1, hr, C), lambda s, cr: (s, 0, 0))),
        out_shape=jax.ShapeDtypeStruct((N_CHIPS, hr, C), BF16),
    )(c, grad, recv)


def _chip_sum(received, own, chip, tag):
    _, rows, C = received.shape
    tr = rows // 2

    def body(chip_ref, own_ref, r1_ref, r2_ref, r3_ref, o_ref):
        p = [r[0].astype(F32) for r in (own_ref, r1_ref, r2_ref, r3_ref)]
        o_ref[...] = (p[0] + p[1]) + (p[2] + p[3])

    def slot(k):
        return pl.BlockSpec((1, tr, C), lambda i, cr: (jnp.bitwise_xor(cr[0], k), i, 0))

    return pl.pallas_call(
        body, name="chip_sum_" + tag,
        grid_spec=pltpu.PrefetchScalarGridSpec(
            num_scalar_prefetch=1, grid=(rows // tr,), in_specs=[slot(0), slot(1), slot(2), slot(3)],
            out_specs=pl.BlockSpec((tr, C), lambda i, cr: (i, 0))),
        out_shape=jax.ShapeDtypeStruct((rows, C), F32),
    )(chip, own, received, received, received)


def _adamw_math(w, g, m, v):
    m = ADAM_B1 * m + (1.0 - ADAM_B1) * g
    v = ADAM_B2 * v + (1.0 - ADAM_B2) * (g * g)
    m_hat = m / (1.0 - ADAM_B1 ** ADAM_STEP)
    v_hat = v / (1.0 - ADAM_B2 ** ADAM_STEP)
    delta = -ADAM_LR * (m_hat / (jnp.sqrt(v_hat) + ADAM_EPS) + ADAM_WD * w)
    return delta, m, v


def _adamw(w, g_mine, g_other, m, v, c, tag):
    R, C = w.shape
    tr = R // 4

    def body(c_ref, w_ref, gm_ref, go_ref, m_ref, v_ref, g_ref, d_ref, nm_ref, nv_ref):
        g = jnp.where(pl.program_id(0) == c_ref[0], gm_ref[...], go_ref[...])
        g_ref[...] = g
        d_ref[...], nm_ref[...], nv_ref[...] = _adamw_math(w_ref[...], g, m_ref[...], v_ref[...])

    blk = pl.BlockSpec((tr, C), lambda h, i, cr: (2 * h + i, 0))
    half = pl.BlockSpec((tr, C), lambda h, i, cr: (i, 0))
    return pl.pallas_call(
        body, name="adamw_" + tag,
        grid_spec=pltpu.PrefetchScalarGridSpec(
            num_scalar_prefetch=1, grid=(2, 2), in_specs=[blk, half, half, blk, blk], out_specs=[blk] * 4),
        out_shape=[jax.ShapeDtypeStruct((R, C), F32)] * 4,
    )(c, w, g_mine, g_other, m, v)


def _small_update(all_small, w, m, v):
    def body(a_ref, w_ref, m_ref, v_ref, g_ref, d_ref, nm_ref, nv_ref):
        g = ((a_ref[0] + a_ref[1]) + (a_ref[2] + a_ref[3])) + ((a_ref[4] + a_ref[5]) + (a_ref[6] + a_ref[7]))
        g_ref[...] = g
        d_ref[...], nm_ref[...], nv_ref[...] = _adamw_math(w_ref[...], g, m_ref[...], v_ref[...])

    return pl.pallas_call(
        body, name="small_update", out_shape=[jax.ShapeDtypeStruct(w.shape, F32)] * 4,
    )(all_small, w, m, v)


SMALL_ROWS = (("attn", 0, 0), ("ffn", 1, 0), ("dil", 2, 0), ("sb", 2, GROUP), ("q", 3, 0), ("k", 3, HEAD_DIM),
              ("loss", 4, 0))


def _pack_small(vals, D):
    rows = [jnp.zeros((1, D), F32) for _ in range(8)]
    for name, r, off in SMALL_ROWS:
        if name in vals:
            rows[r] = lax.dynamic_update_slice(rows[r], vals[name].astype(F32), (0, off))
    return jnp.concatenate(rows, axis=0)


def _unpack_small(packed, vals):
    return {name: packed[r:r + 1, off:off + vals[name].shape[1]] for name, r, off in SMALL_ROWS if name in vals}


def kernel(x, attn_norm_w, w_in, q_norm_w, k_norm_w, dil_out_norm_w, sb_out_norm_w, w_out, ffn_norm_w, w_gate, w_up, w_down, loss_target, m_attn_norm_w, m_w_in, m_q_norm_w, m_k_norm_w, m_dil_out_norm_w, m_sb_out_norm_w, m_w_out, m_ffn_norm_w, m_w_gate, m_w_up, m_w_down, v_attn_norm_w, v_w_in, v_q_norm_w, v_k_norm_w, v_dil_out_norm_w, v_sb_out_norm_w, v_w_out, v_ffn_norm_w, v_w_gate, v_w_up, v_w_down):
    D = x.shape[-1]
    big_names = ("w_in", "w_out", "w_gate", "w_up", "w_down")
    flipped = ("w_gate", "w_up")
    tr = lambda a: jnp.swapaxes(a[0], 0, 1)
    big_w = dict(w_in=w_in[0], w_out=w_out[0], w_gate=tr(w_gate), w_up=tr(w_up), w_down=w_down[0])
    big_m = dict(w_in=m_w_in[0], w_out=m_w_out[0], w_gate=tr(m_w_gate), w_up=tr(m_w_up), w_down=m_w_down[0])
    big_v = dict(w_in=v_w_in[0], w_out=v_w_out[0], w_gate=tr(v_w_gate), w_up=tr(v_w_up), w_down=v_w_down[0])
    small_w = dict(attn=attn_norm_w, q=q_norm_w, k=k_norm_w, dil=dil_out_norm_w, sb=sb_out_norm_w, ffn=ffn_norm_w)
    small_m = dict(attn=m_attn_norm_w, q=m_q_norm_w, k=m_k_norm_w, dil=m_dil_out_norm_w, sb=m_sb_out_norm_w,
                   ffn=m_ffn_norm_w)
    small_v = dict(attn=v_attn_norm_w, q=v_q_norm_w, k=v_k_norm_w, dil=v_dil_out_norm_w, sb=v_sb_out_norm_w,
                   ffn=v_ffn_norm_w)

    c = lax.axis_index("c").astype(jnp.int32).reshape(1)
    chip = (2 * lax.axis_index("x") + lax.axis_index("y")).astype(jnp.int32).reshape(1)
    (w_in_g,) = _gather_weights([big_w["w_in"].astype(BF16)])
    w_out_slots = _own_slots(big_w["w_out"].astype(BF16))
    ffn_slots = [_own_slots(big_w[n].astype(BF16)) for n in FFN_NAMES]

    loss_parts, grad_x, small_g, dw_in, dw_out, ffn_halves = _device_step(
        x[0], loss_target[0], attn_norm_w, q_norm_w, k_norm_w, dil_out_norm_w, sb_out_norm_w, ffn_norm_w,
        w_in_g, w_out_slots, ffn_slots, c, chip)
    small_g["loss"] = (jnp.sum(loss_parts[:, 0, 0]) * (0.5 / D)).reshape(1, 1)

    late = [dw_in, dw_out.reshape(N_CHIPS, -1, D)]
    *from_pair, all_small = _pair_exchange(late, _pack_small(small_g, D))
    chip_parts = [_pair_sum(g, r, c, n) for g, r, n in zip(late, from_pair, big_names)]
    from_chips = _chip_exchange(chip_parts)
    halves = [_chip_sum(r, p, chip, n) for r, p, n in zip(from_chips, chip_parts, big_names)] + ffn_halves
    others = _pair_swap(halves)
    big_out = {n: _adamw(big_w[n], mine, other, big_m[n], big_v[n], c, n)
               for n, mine, other in zip(big_names, halves, others)}
    sg, sd, sm, sv = _small_update(all_small, _pack_small(small_w, D), _pack_small(small_m, D),
                                   _pack_small(small_v, D))
    small_out = [_unpack_small(t, small_w) for t in (sg, sd, sm, sv)]

    order = (("attn", None), (None, "w_in"), ("q", None), ("k", None), ("dil", None), ("sb", None),
             (None, "w_out"), ("ffn", None), (None, "w_gate"), (None, "w_up"), (None, "w_down"))
    outs = [sg[4, 0], grad_x[None]]
    for kind in range(4):
        for s_name, b_name in order:
            if s_name is not None:
                outs.append(small_out[kind][s_name])
            else:
                res = big_out[b_name][kind]
                outs.append((jnp.swapaxes(res, 0, 1) if b_name in flipped else res)[None])
    return tuple(outs)
```

```python
import functools

import jax
import jax.numpy as jnp
from jax import lax
from jax.experimental import pallas as pl
from jax.experimental.pallas import tpu as pltpu

F32 = jnp.float32
BF16 = jnp.bfloat16
MESH = pl.DeviceIdType.MESH

HEAD_DIM = 64
GROUP = 512
BLOCK = 128
LANES = 128
N_CHIPS = 4
N_DEV = 8
EPS = 1e-6
ROPE_THETA = 10000.0
DILATIONS = (1, 4, 16)
NEG = -1e30

ADAM_LR = 0.001
ADAM_B1 = 0.9
ADAM_B2 = 0.999
ADAM_EPS = 1e-08
ADAM_WD = 0.01
ADAM_STEP = 10


def _dot(a, b):
    return jnp.dot(a, b, preferred_element_type=F32)


def _dot_nt(a, b):
    return lax.dot_general(a, b, (((1,), (1,)), ((), ())), preferred_element_type=F32)


def _dot_tn(a, b):
    return lax.dot_general(a, b, (((0,), (0,)), ((), ())), preferred_element_type=F32)


def _split(v):
    hi = lax.bitcast_convert_type(lax.bitcast_convert_type(v, jnp.uint32) & jnp.uint32(0xFFFF0000), F32)
    return hi.astype(BF16), (v - hi).astype(BF16)


def _segsum(v, g):
    hi, lo = _split(v)
    w = g.shape[1]
    return jnp.concatenate([_dot(jnp.concatenate([hi[:, c:c + w], lo[:, c:c + w]], axis=1), g)
                            for c in range(0, v.shape[1], w)], axis=1)


def _rot_half(x):
    outs = []
    for c in range(x.shape[1] // LANES):
        xc = x[:, c * LANES:(c + 1) * LANES]
        lane = lax.broadcasted_iota(jnp.int32, xc.shape, 1)
        first = (lane % HEAD_DIM) < (HEAD_DIM // 2)
        outs.append(jnp.where(first, pltpu.roll(xc, LANES - 32, 1), pltpu.roll(xc, 32, 1)))
    return outs[0] if len(outs) == 1 else jnp.concatenate(outs, axis=1)


def _rms(x):
    return lax.rsqrt(jnp.mean(x * x, axis=-1, keepdims=True) + EPS)


def _rms_bwd(dy, x, w):
    rstd = _rms(x)
    xh = x * rstd
    dxh = dy * w
    dx = rstd * (dxh - xh * jnp.mean(dxh * xh, axis=-1, keepdims=True))
    return dx, dy * xh


def _sigmoid(x):
    return 1.0 / (1.0 + jnp.exp(-x))


def _sum4(ref):
    p = [ref[j].astype(F32) for j in range(N_CHIPS)]
    return (p[0] + p[1]) + (p[2] + p[3])


def _full(shape):
    n = len(shape)
    return pl.BlockSpec(shape, lambda *_: (0,) * n)


def _strided_spec(tm, r):
    return pl.BlockSpec((r, tm // r, GROUP), lambda i: (0, i, 0))


def _strided_shape(S, r, dtype):
    return jax.ShapeDtypeStruct((r, S // r, GROUP), dtype)


def _to_strided(scr, val, outs):
    chunks = range(GROUP // LANES)
    for k in chunks:
        scr[k] = val[:, _lanes(k)]
    for r, o_ref in outs:
        if r == 1:
            o_ref[0] = val.astype(o_ref.dtype)
            continue
        n = val.shape[0] // r
        for c in range(r):
            rows = pl.ds(c, n, stride=r)
            o_ref[c] = jnp.concatenate([scr.at[k][rows, :] for k in chunks], axis=1).astype(o_ref.dtype)


def _from_strided(scr, ref):
    r, n, _ = ref.shape
    if r == 1:
        return ref[0].astype(F32)
    chunks = range(GROUP // LANES)
    for c in range(r):
        plane = ref[c].astype(F32)
        for k in chunks:
            scr.at[k][pl.ds(c, n, stride=r), :] = plane[:, _lanes(k)]
    return jnp.concatenate([scr[k] for k in chunks], axis=1)


def _strided_scratch(tm):
    return pltpu.VMEM((GROUP // LANES, tm, LANES), F32)


def _tile4(t):
    return jnp.concatenate([t] * (GROUP // LANES), axis=1)


def _in_proj_fwd(x, attn_w, w_in_g, qw, kw, cos_t, sin_t, seg_ones, slots):
    S, D = x.shape
    tm = 512
    wcols = w_in_g.shape[2]
    nd = len(DILATIONS)
    ns = len(slots)

    def body(x_ref, aw_ref, w_ref, qw_ref, kw_ref, cos_ref, sin_ref, g_ref, *rest):
        slot_in, (h_ref, qa_ref, ka_ref), rest = rest[:ns], rest[ns:ns + 3], rest[ns + 3:]
        q_refs, k_refs, v_refs = rest[:nd], rest[nd:2 * nd], rest[2 * nd:3 * nd]
        qs_ref, ks_ref, vs_ref = rest[3 * nd:3 * nd + 3]
        slot_out, scr, sems = rest[3 * nd + 3:3 * nd + 3 + ns], rest[3 * nd + 3 + ns], rest[3 * nd + 4 + ns:]
        finish = _hosted_gathers(slot_in, slot_out, *sems, pl.program_id(0), S // tm) if ns else None
        xv = x_ref[...]
        h = (xv * _rms(xv) * aw_ref[...]).astype(BF16)
        h_ref[...] = h
        proj = jnp.concatenate([_dot(h, w_ref[j]) for j in range(N_CHIPS)], axis=1)
        qa = proj[:, 0 * GROUP:1 * GROUP]
        ka = proj[:, 1 * GROUP:2 * GROUP]
        qa_ref[...] = qa
        ka_ref[...] = ka
        _to_strided(scr, proj[:, 2 * GROUP:3 * GROUP], list(zip(DILATIONS, v_refs)))
        qs_ref[...] = proj[:, 3 * GROUP:4 * GROUP].astype(BF16)
        ks_ref[...] = proj[:, 4 * GROUP:5 * GROUP].astype(BF16)
        vs_ref[...] = proj[:, 5 * GROUP:6 * GROUP].astype(BF16)
        g = g_ref[...]
        cos = _tile4(cos_ref[...])
        sin = _tile4(sin_ref[...])
        for t, w_r, o_rs in ((qa, qw_ref, q_refs), (ka, kw_ref, k_refs)):
            rstd = lax.rsqrt(_segsum(t * t, g) * (1.0 / HEAD_DIM) + EPS)
            tn = t * rstd * w_r[...]
            _to_strided(scr, tn * cos + _rot_half(tn) * sin, list(zip(DILATIONS, o_rs)))
        if finish is not None:
            finish()

    row = lambda i: (i, 0)
    tile = lambda n, dt: jax.ShapeDtypeStruct((S, n), dt)
    planes = [_strided_spec(tm, r) for r in DILATIONS]
    h_in, h_out, h_shape, h_sems = _hosted_specs(slots)
    n_out = 6 + 3 * nd
    return pl.pallas_call(
        body, name="in_proj_fwd", grid=(S // tm,),
        in_specs=[pl.BlockSpec((tm, D), row), _full((1, D)), _full((N_CHIPS, D, wcols)),
                  _full((1, GROUP)), _full((1, GROUP)),
                  pl.BlockSpec((tm, LANES), row), pl.BlockSpec((tm, LANES), row),
                  _full((GROUP, GROUP // 2))] + h_in,
        out_specs=[pl.BlockSpec((tm, D), row)] + [pl.BlockSpec((tm, GROUP), row)] * 2 + planes * 3
                  + [pl.BlockSpec((tm, GROUP), row)] * 3 + h_out,
        out_shape=[tile(D, BF16), tile(GROUP, F32), tile(GROUP, F32)]
                  + [_strided_shape(S, r, BF16) for r in DILATIONS] * 3 + [tile(GROUP, BF16)] * 3 + h_shape,
        input_output_aliases={8 + a: n_out + a for a in range(ns)},
        scratch_shapes=[_strided_scratch(tm)] + h_sems,
    )(x, attn_w, w_in_g, qw, kw, cos_t, sin_t, seg_ones, *slots)


DIL_PLANES = 1


def _dil_fwd(q, k, v, slots):
    r, L, _ = q.shape
    nb = L // BLOCK
    P = GROUP // LANES
    PL = min(DIL_PLANES, r)
    ns = len(slots)
    units = [(pp, hp) for pp in range(PL) for hp in range(P)]

    def body(q_ref, kc_ref, kp_ref, vc_ref, vp_ref, *rest):
        o_ref, l_ref = rest[ns:ns + 2]
        n = pl.program_id(1)
        finish = (_hosted_gathers(rest[:ns], rest[ns + 2:2 * ns + 2], *rest[2 * ns + 2:],
                                  pl.program_id(0) * nb + n, (r // PL) * nb) if ns else None)
        rowi = lax.broadcasted_iota(jnp.int32, (BLOCK, BLOCK), 0)
        coli = lax.broadcasted_iota(jnp.int32, (BLOCK, BLOCK), 1)
        first = coli < HEAD_DIM
        masks = (coli <= rowi, jnp.logical_and(coli >= rowi, n > 0))
        s2 = {}
        for pp, hp in units:
            q2 = _scaled(q_ref[pp, :, _lanes(hp)])
            for b, k_ref in enumerate((kc_ref, kp_ref)):
                s2[pp, hp, b] = _dot_nt(q2, _by_head(k_ref[pp, :, _lanes(hp)], first))
        ps, inv, lse = {}, {}, {}
        for pp, hp in units:
            for h in range(2):
                s = [jnp.where(masks[b], s2[pp, hp, b][:, h * BLOCK:(h + 1) * BLOCK], NEG) for b in range(2)]
                m = jnp.maximum(jnp.max(s[0], axis=1, keepdims=True), jnp.max(s[1], axis=1, keepdims=True))
                p = [jnp.exp(s[b] - m) for b in range(2)]
                den = jnp.sum(p[0], axis=1, keepdims=True) + jnp.sum(p[1], axis=1, keepdims=True)
                ps[pp, hp, h] = [p[b].astype(BF16) for b in range(2)]
                inv[pp, hp, h] = 1.0 / den
                lse[pp, hp, h] = m + jnp.log(den)
        for pp, hp in units:
            o = jnp.zeros((BLOCK, LANES), F32)
            for b, v_ref in enumerate((vc_ref, vp_ref)):
                o = o + _dot(jnp.concatenate([ps[pp, hp, 0][b], ps[pp, hp, 1][b]], axis=1),
                             _by_head(v_ref[pp, :, _lanes(hp)], first))
            o_ref[pp, :, _lanes(hp)] = o * jnp.where(first, inv[pp, hp, 0], inv[pp, hp, 1])
            l_ref[pp, :, _lanes(hp)] = jnp.where(first, lse[pp, hp, 0], lse[pp, hp, 1])
        if finish is not None:
            finish()

    cur = pl.BlockSpec((PL, BLOCK, GROUP), lambda c, n: (c, n, 0))
    prev = pl.BlockSpec((PL, BLOCK, GROUP), lambda c, n: (c, jnp.maximum(n - 1, 0), 0))
    h_in, h_out, h_shape, h_sems = _hosted_specs(slots)
    return pl.pallas_call(
        body, name="dil_fwd_r%d" % r, grid=(r // PL, nb),
        in_specs=[cur, cur, prev, cur, prev] + h_in, out_specs=[cur, cur] + h_out,
        out_shape=[jax.ShapeDtypeStruct(q.shape, F32)] * 2 + h_shape,
        input_output_aliases={5 + a: 2 + a for a in range(ns)},
        scratch_shapes=h_sems,
    )(q, k, k, v, v, *slots)


def _dil_bwd(q, k, v, do, lse, delta, parts, pairs):
    r, L, _ = q.shape
    nb = L // BLOCK
    P = GROUP // LANES
    PL = min(DIL_PLANES, r)
    scale = HEAD_DIM ** -0.5
    units = [(pp, hp) for pp in range(PL) for hp in range(P)]
    npt, npr = len(parts), len(pairs)
    nx = npt + npr

    def body(qc_ref, qn_ref, doc_ref, don_ref, lc_ref, ln_ref, dc_ref, dn_ref, k_ref, v_ref, *rest):
        x_in, (dq_ref, dk_ref, dv_ref), x_out = rest[:nx], rest[nx:nx + 3], rest[nx + 3:2 * nx + 3]
        carry, send, recv = rest[2 * nx + 3:]
        j = pl.program_id(1)
        step = pl.program_id(0) * nb + j
        plans = [_chip_send_plan(x_in[a], x_out[a], send.at[pl.ds(3 * a, 3)], recv.at[pl.ds(3 * a, 3)])
                 for a in range(npt)]
        plans += [_pair_send_plan(x_in[npt + a], x_out[npt + a], send.at[3 * npt + a], recv.at[3 * npt + a])
                  for a in range(npr)]

        def start():
            for begin, _ in plans:
                begin()

        def finish():
            for _, end in plans:
                end()

        pl.when(step == 0)(start)
        rowi = lax.broadcasted_iota(jnp.int32, (BLOCK, BLOCK), 0)
        coli = lax.broadcasted_iota(jnp.int32, (BLOCK, BLOCK), 1)
        first = coli < HEAD_DIM
        sides = ((qc_ref, doc_ref, lc_ref, dc_ref, coli <= rowi),
                 (qn_ref, don_ref, ln_ref, dn_ref, jnp.logical_and(coli >= rowi, j < nb - 1)))

        @pl.when(j == 0)
        def _():
            carry[...] = jnp.zeros_like(carry)

        kcat, q2, do2, s2, dp2 = {}, {}, {}, {}, {}
        for pp, hp in units:
            kcat[pp, hp] = _by_head(k_ref[pp, :, _lanes(hp)], first)
            vcat = _by_head(v_ref[pp, :, _lanes(hp)], first)
            for x, (q_r, do_r, _, _, _) in enumerate(sides):
                q2[pp, hp, x] = _scaled(q_r[pp, :, _lanes(hp)])
                do2[pp, hp, x] = do_r[pp, :, _lanes(hp)]
                s2[pp, hp, x] = _dot_nt(q2[pp, hp, x], kcat[pp, hp])
                dp2[pp, hp, x] = _dot_nt(do2[pp, hp, x], vcat)
        pcat, dscat = {}, {}
        for pp, hp in units:
            for x, (_, _, l_r, d_r, msk) in enumerate(sides):
                ps, dss = [], []
                for h in range(2):
                    col = hp * LANES + h * HEAD_DIM
                    half = slice(h * BLOCK, (h + 1) * BLOCK)
                    p = jnp.where(msk, jnp.exp(s2[pp, hp, x][:, half] - l_r[pp, :, col:col + 1]), 0.0)
                    ps.append(p.astype(BF16))
                    dss.append((p * (dp2[pp, hp, x][:, half] - d_r[pp, :, col:col + 1])).astype(BF16))
                pcat[pp, hp, x] = jnp.concatenate(ps, axis=1)
                dscat[pp, hp, x] = jnp.concatenate(dss, axis=1)
        for pp, hp in units:
            dv2 = _dot_tn(pcat[pp, hp, 0], do2[pp, hp, 0]) + _dot_tn(pcat[pp, hp, 1], do2[pp, hp, 1])
            dk2 = _dot_tn(dscat[pp, hp, 0], q2[pp, hp, 0]) + _dot_tn(dscat[pp, hp, 1], q2[pp, hp, 1])
            dv_ref[pp, :, _lanes(hp)] = jnp.where(first, dv2[:BLOCK], dv2[BLOCK:]).astype(BF16)
            dk_ref[pp, :, _lanes(hp)] = jnp.where(first, dk2[:BLOCK], dk2[BLOCK:]).astype(BF16)
            dq_ref[pp, :, _lanes(hp)] = (carry[pp, :, _lanes(hp)]
                                         + _dot(dscat[pp, hp, 0], kcat[pp, hp]) * scale).astype(BF16)
            carry[pp, :, _lanes(hp)] = _dot(dscat[pp, hp, 1], kcat[pp, hp]) * scale
        pl.when(step == (r // PL) * nb - 1)(finish)

    cur = pl.BlockSpec((PL, BLOCK, GROUP), lambda c, n: (c, n, 0))
    nxt = pl.BlockSpec((PL, BLOCK, GROUP), lambda c, n: (c, jnp.minimum(n + 1, nb - 1), 0))
    return pl.pallas_call(
        body, name="dil_bwd_r%d" % r, grid=(r // PL, nb),
        in_specs=[cur, nxt, cur, nxt, cur, nxt, cur, nxt, cur, cur] + [HBM] * nx, out_specs=[cur, cur, cur] + [HBM] * nx,
        out_shape=[jax.ShapeDtypeStruct(q.shape, BF16)] * 3 + [jax.ShapeDtypeStruct(p.shape, p.dtype) for p in parts]
                  + [jax.ShapeDtypeStruct((g.shape[0], g.shape[1] // 2, g.shape[2]), g.dtype) for g in pairs],
        scratch_shapes=[pltpu.VMEM((PL, BLOCK, GROUP), F32), pltpu.SemaphoreType.DMA((3 * npt + npr,)),
                        pltpu.SemaphoreType.DMA((3 * npt + npr,))],
    )(q, q, do, do, lse, lse, delta, delta, k, v, *parts, *pairs)


SB_TILES = 2
SB_PAIRS_FWD = 4
SB_PAIRS_BWD = 4
SB_DEAD = -110.0


def _lanes(hp):
    return slice(hp * LANES, (hp + 1) * LANES)


def _sb_logits(z, valid):
    e = jnp.exp(-jnp.abs(z))
    lb = jnp.minimum(z, 0.0) - jnp.log(1.0 + e)
    lk = lb - z
    if valid is not None:
        lk = jnp.where(valid, lk, 0.0)
    return e, lb, lk


def _by_head(t, first):
    zero = jnp.zeros_like(t)
    return jnp.concatenate([jnp.where(first, t, zero), jnp.where(first, zero, t)], axis=0)


def _sb_valid(i, j):
    rowi = lax.broadcasted_iota(jnp.int32, (BLOCK, BLOCK), 0)
    coli = lax.broadcasted_iota(jnp.int32, (BLOCK, BLOCK), 1)
    return (coli - rowi) < (i - j) * BLOCK


def _scaled(q):
    return (q.astype(F32) * (HEAD_DIM ** -0.5)).astype(BF16)


def _hosted_gathers(refs_in, refs_out, send, recv, step, steps):
    plans = [_gather_plan(refs_in[a], refs_out[a], send.at[pl.ds(6 * a, 6)], recv.at[pl.ds(6 * a, 6)])
             for a in range(len(refs_in))]
    for stage, at in ((0, 0), (1, (2 * steps) // 3)):
        @pl.when(step == at)
        def _():
            for plan in plans:
                plan[stage]()

    def finish():
        @pl.when(step == steps - 1)
        def _():
            for plan in plans:
                plan[2]()

    return finish


def _hosted_specs(slots):
    n = len(slots)
    sems = [pltpu.SemaphoreType.DMA((6 * n,))] * 2 if n else []
    return [HBM] * n, [HBM] * n, [jax.ShapeDtypeStruct(s.shape, s.dtype) for s in slots], sems


def _sb_fwd(qs, ks, vs, tri_later, slots):
    S = qs.shape[0]
    P = SB_PAIRS_FWD
    W = P * LANES
    ns = len(slots)

    def body(q_ref, k_ref, v_ref, u_ref, *rest):
        o_ref, lt_ref, from_ref = rest[ns:ns + 3]
        i = pl.program_id(1)
        finish = _hosted_gathers(rest[:ns], rest[ns + 3:2 * ns + 3], *rest[2 * ns + 3:], i, S // BLOCK) if ns else None
        first = lax.broadcasted_iota(jnp.int32, (BLOCK, LANES), 1) < HEAD_DIM
        q2 = [_scaled(q_ref[:, _lanes(hp)]) for hp in range(P)]

        def walk(tiles, carry):
            runs, accs = list(carry[0]), list(carry[1])
            units = [(t, hp) for t in range(len(tiles)) for hp in range(P)]
            offs = [pl.multiple_of(j * BLOCK, BLOCK) for j, _ in tiles]
            valids = [_sb_valid(i, j) if diag else None for j, diag in tiles]
            z2s, lbs, c2s = {}, {}, {}
            for t, hp in units:
                z2s[t, hp] = _dot_nt(q2[hp], _by_head(k_ref[pl.ds(offs[t], BLOCK), _lanes(hp)], first))
            for t, hp in units:
                for h in range(2):
                    _, lb, lk = _sb_logits(z2s[t, hp][:, h * BLOCK:(h + 1) * BLOCK], valids[t])
                    lbs[t, hp, h] = lb
                    c2s[t, hp, h] = _dot(jnp.concatenate(_split(lk), axis=1), u_ref[...])
            for t, hp in units:
                a2 = []
                for h in range(2):
                    a = jnp.exp(lbs[t, hp, h] + c2s[t, hp, h][:, :BLOCK] + runs[2 * hp + h])
                    if valids[t] is not None:
                        a = jnp.where(valids[t], a, 0.0)
                    a2.append(a.astype(BF16))
                    runs[2 * hp + h] = runs[2 * hp + h] + c2s[t, hp, h][:, BLOCK:]
                vcat = _by_head(v_ref[pl.ds(offs[t], BLOCK), _lanes(hp)], first)
                accs[hp] = accs[hp] + _dot(jnp.concatenate(a2, axis=1), vcat)
            return tuple(runs), tuple(accs)

        def chunk(ci, carry):
            return walk([(ci * SB_TILES + t, False) for t in reversed(range(SB_TILES))], carry)

        def alive(runs):
            top = functools.reduce(jnp.maximum, runs)
            return (jnp.max(top) > SB_DEAD).astype(jnp.int32)

        def step(c):
            t, _, runs, accs = c
            runs, accs = chunk(nfull - 1 - t, (runs, accs))
            return t + 1, alive(runs), runs, accs

        zero = jnp.zeros((BLOCK, LANES), F32)
        nfull = i // SB_TILES
        ragged = [functools.partial(walk, [(i, True)] + [(i - 1 - m, False) for m in range(extra)])
                  for extra in range(SB_TILES)]
        runs, accs = lax.switch(i % SB_TILES, ragged, ((zero,) * (2 * P), (zero,) * P))
        done, _, runs, accs = lax.while_loop(lambda c: jnp.logical_and(c[0] < nfull, c[1] > 0), step,
                                             (jnp.int32(0), alive(runs), runs, accs))
        for hp in range(P):
            o_ref[:, _lanes(hp)] = accs[hp]
            lt_ref[:, _lanes(hp)] = jnp.where(first, runs[2 * hp], runs[2 * hp + 1])
        from_ref[...] = jnp.full(from_ref.shape, nfull - done, jnp.int32)
        if finish is not None:
            finish()

    assert W == GROUP
    blk = pl.BlockSpec((BLOCK, W), lambda hp, i: (i, hp))
    col = pl.BlockSpec((S, W), lambda hp, i: (0, hp))
    h_in, h_out, h_shape, h_sems = _hosted_specs(slots)
    return pl.pallas_call(
        body, name="sb_fwd", grid=(GROUP // W, S // BLOCK),
        in_specs=[blk, col, col, _full((2 * BLOCK, 2 * BLOCK))] + h_in,
        out_specs=[blk, blk, pl.BlockSpec((1, 8, LANES), lambda hp, i: (i, 0, 0))] + h_out,
        out_shape=[jax.ShapeDtypeStruct((S, GROUP), F32)] * 2
                  + [jax.ShapeDtypeStruct((S // BLOCK, 8, LANES), jnp.int32)] + h_shape,
        input_output_aliases={4 + a: 3 + a for a in range(ns)},
        scratch_shapes=h_sems,
    )(qs, ks, vs, tri_later, *slots)


def _sb_bwd(first_chunk, qs, ks, vs, do, ltot, tri_upto, tri_before):
    S = qs.shape[0]
    P = SB_PAIRS_BWD
    W = P * LANES

    def body(from_ref, q_ref, k_ref, v_ref, do_ref, lt_ref, w_ref, x_ref, dq_ref, dk_ref, dv_ref):
        i = pl.program_id(1)

        @pl.when(i == 0)
        def _():
            dk_ref[...] = jnp.zeros_like(dk_ref)
            dv_ref[...] = jnp.zeros_like(dv_ref)

        first = lax.broadcasted_iota(jnp.int32, (BLOCK, LANES), 1) < HEAD_DIM
        q2 = [_scaled(q_ref[:, _lanes(hp)]) for hp in range(P)]
        do2 = [do_ref[:, _lanes(hp)] for hp in range(P)]
        totals = [jnp.broadcast_to(lt_ref[:, n * HEAD_DIM:n * HEAD_DIM + 1], (BLOCK, LANES)) for n in range(2 * P)]

        def walk(tiles, carry):
            keeps, grads, dqs = list(carry[0]), list(carry[1]), list(carry[2])
            units = [(t, hp) for t in range(len(tiles)) for hp in range(P)]
            offs = [pl.multiple_of(j * BLOCK, BLOCK) for j, _ in tiles]
            valids = [_sb_valid(i, j) if diag else None for j, diag in tiles]
            kcat, z2, da2, es, lbs, c2s, as_, des, p2s = {}, {}, {}, {}, {}, {}, {}, {}, {}
            for t, hp in units:
                kcat[t, hp] = _by_head(k_ref[pl.ds(offs[t], BLOCK), _lanes(hp)], first)
                z2[t, hp] = _dot_nt(q2[hp], kcat[t, hp])
                da2[t, hp] = _dot_nt(do2[hp], _by_head(v_ref[pl.ds(offs[t], BLOCK), _lanes(hp)], first))
            for t, hp in units:
                for h in range(2):
                    es[t, hp, h], lbs[t, hp, h], lk = _sb_logits(z2[t, hp][:, h * BLOCK:(h + 1) * BLOCK], valids[t])
                    c2s[t, hp, h] = _dot(jnp.concatenate(_split(lk), axis=1), w_ref[...])
            for t, hp in units:
                for h in range(2):
                    n = 2 * hp + h
                    a = jnp.exp(lbs[t, hp, h] + (totals[n] - (keeps[n] + c2s[t, hp, h][:, :BLOCK])))
                    if valids[t] is not None:
                        a = jnp.where(valids[t], a, 0.0)
                    keeps[n] = keeps[n] + c2s[t, hp, h][:, BLOCK:]
                    de = a * da2[t, hp][:, h * BLOCK:(h + 1) * BLOCK]
                    as_[t, hp, h], des[t, hp, h] = a.astype(BF16), de
                    p2s[t, hp, h] = _dot(jnp.concatenate(_split(de), axis=1), x_ref[...])
            for t, hp in units:
                dz2 = []
                for h in range(2):
                    n = 2 * hp + h
                    e = es[t, hp, h]
                    sig = jnp.where(z2[t, hp][:, h * BLOCK:(h + 1) * BLOCK] >= 0.0, 1.0, e) / (1.0 + e)
                    dz = des[t, hp, h] * (1.0 - sig) - (grads[n] + p2s[t, hp, h][:, :BLOCK]) * sig
                    if valids[t] is not None:
                        dz = jnp.where(valids[t], dz, 0.0)
                    grads[n] = grads[n] + p2s[t, hp, h][:, BLOCK:]
                    dz2.append(dz.astype(BF16))
                dzcat = jnp.concatenate(dz2, axis=1)
                dk2 = _dot_tn(dzcat, q2[hp])
                dv2 = _dot_tn(jnp.concatenate([as_[t, hp, 0], as_[t, hp, 1]], axis=1), do2[hp])
                dk_ref[pl.ds(offs[t], BLOCK), _lanes(hp)] += jnp.where(first, dk2[:BLOCK], dk2[BLOCK:])
                dv_ref[pl.ds(offs[t], BLOCK), _lanes(hp)] += jnp.where(first, dv2[:BLOCK], dv2[BLOCK:])
                dqs[hp] = dqs[hp] + _dot(dzcat, kcat[t, hp])
            return tuple(keeps), tuple(grads), tuple(dqs)

        zero = jnp.zeros((BLOCK, LANES), F32)
        nfull = i // SB_TILES
        carry = lax.fori_loop(
            from_ref[i], nfull, lambda ci, c: walk([(ci * SB_TILES + t, False) for t in range(SB_TILES)], c),
            ((zero,) * (2 * P), (zero,) * (2 * P), (zero,) * P))
        ragged = [functools.partial(walk, [(i - m, False) for m in range(extra, 0, -1)] + [(i, True)])
                  for extra in range(SB_TILES)]
        carry = lax.switch(i % SB_TILES, ragged, carry)
        for hp in range(P):
            dq_ref[:, _lanes(hp)] = carry[2][hp] * (HEAD_DIM ** -0.5)

    blk = pl.BlockSpec((BLOCK, W), lambda hp, i, fr: (i, hp))
    col = pl.BlockSpec((S, W), lambda hp, i, fr: (0, hp))
    tri = pl.BlockSpec((2 * BLOCK, 2 * BLOCK), lambda hp, i, fr: (0, 0))
    return pl.pallas_call(
        body, name="sb_bwd",
        grid_spec=pltpu.PrefetchScalarGridSpec(
            num_scalar_prefetch=1, grid=(GROUP // W, S // BLOCK),
            in_specs=[blk, col, col, blk, blk, tri, tri], out_specs=[blk, col, col]),
        out_shape=[jax.ShapeDtypeStruct((S, GROUP), F32)] * 3,
    )(first_chunk, qs, ks, vs, do, ltot, tri_upto, tri_before)


def _out_proj_fwd(o_br, l_br, o_sb, x, w_dil, w_sbn, w_out_g, slots):
    S, D = x.shape
    tm = 512
    ns = len(slots)

    def body(o0, o1, o2, l0, l1, l2, os_ref, x_ref, wd_ref, ws_ref, w_ref, *rest):
        od_ref, s0, s1, s2, x1_ref = rest[ns:ns + 5]
        scr = rest[2 * ns + 5]
        finish = (_hosted_gathers(rest[:ns], rest[ns + 5:2 * ns + 5], *rest[2 * ns + 6:], pl.program_id(0), S // tm)
                  if ns else None)
        ls = [_from_strided(scr, l) for l in (l0, l1, l2)]
        os_ = [_from_strided(scr, o) for o in (o0, o1, o2)]
        m = jnp.maximum(jnp.maximum(ls[0], ls[1]), ls[2])
        es = [jnp.exp(l - m) for l in ls]
        den = es[0] + es[1] + es[2]
        od = (es[0] * os_[0] + es[1] * os_[1] + es[2] * os_[2]) / den
        od_ref[...] = od
        _to_strided(scr, m + jnp.log(den), list(zip(DILATIONS, (s0, s1, s2))))
        osb = os_ref[...]
        mixed = jnp.concatenate([(od * _rms(od) * wd_ref[...]).astype(BF16),
                                 (osb * _rms(osb) * ws_ref[...]).astype(BF16)], axis=1)
        x1_ref[...] = x_ref[...] + _dot(mixed, w_ref[...])
        if finish is not None:
            finish()

    row = lambda i: (i, 0)
    g = pl.BlockSpec((tm, GROUP), row)
    d = pl.BlockSpec((tm, D), row)
    planes = [_strided_spec(tm, r) for r in DILATIONS]
    h_in, h_out, h_shape, h_sems = _hosted_specs(slots)
    return pl.pallas_call(
        body, name="out_proj_fwd", grid=(S // tm,),
        in_specs=planes * 2 + [g, d, _full((1, GROUP)), _full((1, GROUP)), _full((2 * GROUP, D))] + h_in,
        out_specs=[g] + planes + [d] + h_out,
        out_shape=[jax.ShapeDtypeStruct((S, GROUP), F32)] + [_strided_shape(S, r, F32) for r in DILATIONS]
                  + [jax.ShapeDtypeStruct((S, D), F32)] + h_shape,
        input_output_aliases={11 + a: 5 + a for a in range(ns)},
        scratch_shapes=[_strided_scratch(tm)] + h_sems,
    )(*o_br, *l_br, o_sb, x, w_dil, w_sbn, w_out_g, *slots)


def _ffn_fwd(x1, target, ffn_w, wg_g, wu_g, wd_g):
    S, D = x1.shape
    F = wg_g.shape[1]
    tm = 512
    nt = S // tm

    def body(x_ref, t_ref, nw_ref, wg_ref, wu_ref, wd_ref, h_ref, g_ref, u_ref, dy_ref, loss_ref, h_s, acc):
        j = pl.program_id(1)

        @pl.when(j == 0)
        def _():
            xv = x_ref[...]
            h = (xv * _rms(xv) * nw_ref[...]).astype(BF16)
            h_s[...] = h
            h_ref[...] = h
            acc[...] = xv

        h = h_s[...]
        g = _dot_nt(h, wg_ref[0])
        u = _dot_nt(h, wu_ref[0])
        g_ref[0] = g.astype(BF16)
        u_ref[0] = u.astype(BF16)
        a = (g * _sigmoid(g) * u).astype(BF16)
        acc[...] += _dot(a, wd_ref[0])

        @pl.when(j == N_CHIPS - 1)
        def _():
            err = acc[...] - t_ref[...]
            dy_ref[...] = err * (1.0 / D)
            loss_ref[...] = jnp.full(loss_ref.shape, jnp.sum(err * err), F32)

    row = lambda t, j: (t, 0)
    shard = lambda t, j: (j, 0, 0)
    act = lambda t, j: (j, t, 0)
    return pl.pallas_call(
        body, name="ffn_fwd", grid=(nt, N_CHIPS),
        in_specs=[pl.BlockSpec((tm, D), row), pl.BlockSpec((tm, D), row), pl.BlockSpec((1, D), lambda t, j: (0, 0))]
                 + [pl.BlockSpec((1, F, D), shard)] * 3,
        out_specs=[pl.BlockSpec((tm, D), row), pl.BlockSpec((1, tm, F), act), pl.BlockSpec((1, tm, F), act),
                   pl.BlockSpec((tm, D), row), pl.BlockSpec((1, 8, LANES), lambda t, j: (t, 0, 0))],
        out_shape=[jax.ShapeDtypeStruct((S, D), BF16), jax.ShapeDtypeStruct((N_CHIPS, S, F), BF16),
                   jax.ShapeDtypeStruct((N_CHIPS, S, F), BF16), jax.ShapeDtypeStruct((S, D), F32),
                   jax.ShapeDtypeStruct((nt, 8, LANES), F32)],
        scratch_shapes=[pltpu.VMEM((tm, D), BF16), pltpu.VMEM((tm, D), F32)],
    )(x1, target, ffn_w, wg_g, wu_g, wd_g)


def _ffn_bwd(h2, dy, g, u, wg_g, wu_g, wd_g):
    S, D = dy.shape
    F = wg_g.shape[1]
    tm = 512

    def body(h_ref, dy_ref, g_ref, u_ref, wg_ref, wu_ref, wd_ref, dwg_ref, dwu_ref, dwd_ref, dh_ref, *narrow):
        t = pl.program_id(1)

        @pl.when(t == 0)
        def _():
            dwg_ref[...] = jnp.zeros_like(dwg_ref)
            dwu_ref[...] = jnp.zeros_like(dwu_ref)
            dwd_ref[...] = jnp.zeros_like(dwd_ref)

        h = h_ref[...]
        dyb = dy_ref[...].astype(BF16)
        gv = g_ref[0].astype(F32)
        uv = u_ref[0].astype(F32)
        da = _dot_nt(dyb, wd_ref[0])
        sg = _sigmoid(gv)
        silu = gv * sg
        du = (da * silu).astype(BF16)
        dg = (da * uv * (sg * (1.0 + gv * (1.0 - sg)))).astype(BF16)
        dwd_ref[0] += _dot_tn((silu * uv).astype(BF16), dyb)
        dwg_ref[0] += _dot_tn(dg, h)
        dwu_ref[0] += _dot_tn(du, h)
        dh_ref[0] = (_dot(dg, wg_ref[0]) + _dot(du, wu_ref[0])).astype(BF16)

        @pl.when(t == S // tm - 1)
        def _():
            for full, half in zip((dwg_ref, dwu_ref, dwd_ref), narrow):
                half[...] = full[...].astype(BF16)

    row = lambda j, t: (t, 0)
    shard = lambda j, t: (j, 0, 0)
    act = lambda j, t: (j, t, 0)
    return pl.pallas_call(
        body, name="ffn_bwd", grid=(N_CHIPS, S // tm),
        in_specs=[pl.BlockSpec((tm, D), row), pl.BlockSpec((tm, D), row),
                  pl.BlockSpec((1, tm, F), act), pl.BlockSpec((1, tm, F), act)] + [pl.BlockSpec((1, F, D), shard)] * 3,
        out_specs=[pl.BlockSpec((1, F, D), shard)] * 3 + [pl.BlockSpec((1, tm, D), act)]
                  + [pl.BlockSpec((1, F, D), shard)] * 3,
        out_shape=[jax.ShapeDtypeStruct((N_CHIPS, F, D), F32)] * 3 + [jax.ShapeDtypeStruct((N_CHIPS, S, D), BF16)]
                  + [jax.ShapeDtypeStruct((N_CHIPS, F, D), BF16)] * 3,
    )(h2, dy, g, u, wg_g, wu_g, wd_g)


def _out_proj_bwd(dh2p, dy, x1, ffn_w, w_out_g, o_dil, o_sb, w_dil, w_sbn, seg_ones, ffn_grads):
    S, D = dy.shape
    tm = 512
    ng = len(ffn_grads)

    def body(dh_ref, dy_ref, x1_ref, nw_ref, w_ref, od_ref, os_ref, wd_ref, ws_ref, g_ref, *rest):
        gin, rest = rest[:ng], rest[ng:]
        dx1_ref, dod0, dod1, dod2, dos_ref, dl0, dl1, dl2, dw_ref, dnw_ref, dwd_ref, dws_ref = rest[:12]
        gout, (scr, send, recv) = rest[12:12 + ng], rest[12 + ng:]
        i = pl.program_id(0)
        plans = [_pair_send_plan(gin[a], gout[a], send.at[a], recv.at[a]) for a in range(ng)]

        @pl.when(i == 0)
        def _():
            for start, _ in plans:
                start()

        @pl.when(i == 0)
        def _():
            for r_ in (dw_ref, dnw_ref, dwd_ref, dws_ref):
                r_[...] = jnp.zeros_like(r_)

        dh2 = _sum4(dh_ref)
        dxn, dwn = _rms_bwd(dh2, x1_ref[...], nw_ref[...])
        dnw_ref[...] += jnp.sum(dwn, axis=0, keepdims=True)
        dx1 = dy_ref[...] + dxn
        dx1_ref[...] = dx1
        dx1b = dx1.astype(BF16)
        dmix = _dot_nt(dx1b, w_ref[...])
        od = od_ref[...]
        osb = os_ref[...]
        mixed = jnp.concatenate([(od * _rms(od) * wd_ref[...]).astype(BF16),
                                 (osb * _rms(osb) * ws_ref[...]).astype(BF16)], axis=1)
        dw_ref[...] += _dot_tn(mixed, dx1b)
        do, dwo = _rms_bwd(dmix[:, :GROUP], od, wd_ref[...])
        dwd_ref[...] += jnp.sum(dwo, axis=0, keepdims=True)
        _to_strided(scr, do, list(zip(DILATIONS, (dod0, dod1, dod2))))
        _to_strided(scr, _segsum(do * od, g_ref[...]), list(zip(DILATIONS, (dl0, dl1, dl2))))
        do, dwo = _rms_bwd(dmix[:, GROUP:], osb, ws_ref[...])
        dws_ref[...] += jnp.sum(dwo, axis=0, keepdims=True)
        dos_ref[...] = do.astype(BF16)

        @pl.when(i == S // tm - 1)
        def _():
            for _, finish in plans:
                finish()

    row = lambda i: (i, 0)
    gsp = pl.BlockSpec((tm, GROUP), row)
    dsp = pl.BlockSpec((tm, D), row)
    planes = [_strided_spec(tm, r) for r in DILATIONS]
    halves = [jax.ShapeDtypeStruct((g.shape[0], g.shape[1] // 2, g.shape[2]), g.dtype) for g in ffn_grads]
    return pl.pallas_call(
        body, name="out_proj_bwd", grid=(S // tm,),
        in_specs=[pl.BlockSpec((N_CHIPS, tm, D), lambda i: (0, i, 0)), dsp, dsp, _full((1, D)), _full((2 * GROUP, D)),
                  gsp, gsp, _full((1, GROUP)), _full((1, GROUP)), _full((GROUP, GROUP // 2))] + [HBM] * ng,
        out_specs=[dsp] + planes + [gsp] + planes
                  + [_full((2 * GROUP, D)), _full((1, D)), _full((1, GROUP)), _full((1, GROUP))] + [HBM] * ng,
        out_shape=[jax.ShapeDtypeStruct((S, D), F32)] + [_strided_shape(S, r, BF16) for r in DILATIONS]
                  + [jax.ShapeDtypeStruct((S, GROUP), BF16)] + [_strided_shape(S, r, F32) for r in DILATIONS]
                  + [jax.ShapeDtypeStruct((2 * GROUP, D), F32),
                     jax.ShapeDtypeStruct((1, D), F32), jax.ShapeDtypeStruct((1, GROUP), F32),
                     jax.ShapeDtypeStruct((1, GROUP), F32)] + halves,
        scratch_shapes=[_strided_scratch(tm), pltpu.SemaphoreType.DMA((ng,)), pltpu.SemaphoreType.DMA((ng,))],
    )(dh2p, dy, x1, ffn_w, w_out_g, o_dil, o_sb, w_dil, w_sbn, seg_ones, *ffn_grads)


def _attn_in_bwd(dq_br, dk_br, dv_br, dqs, dks, dvs, qa, ka, qw, kw, cos_t, sin_t, seg_ones, h, w_in_g, x, dx1, attn_w):
    S, D = x.shape
    wc = w_in_g.shape[2]
    tm = 256

    def body(q0, q1, q2, k0, k1, k2, v0, v1, v2, dqs_ref, dks_ref, dvs_ref, qa_ref, ka_ref, qw_ref, kw_ref,
             cos_ref, sin_ref, g_ref, h_ref, w_ref, x_ref, dx1_ref, aw_ref,
             gx_ref, dw_ref, daw_ref, dqw_ref, dkw_ref, accq, acck, scr):
        i = pl.program_id(0)

        @pl.when(i == 0)
        def _():
            accq[...] = jnp.zeros_like(accq)
            acck[...] = jnp.zeros_like(acck)
            dw_ref[...] = jnp.zeros_like(dw_ref)
            daw_ref[...] = jnp.zeros_like(daw_ref)

        def branches(refs):
            return (_from_strided(scr, refs[0]) + _from_strided(scr, refs[1])) + _from_strided(scr, refs[2])

        g = g_ref[...]
        cos = _tile4(cos_ref[...])
        sin = _tile4(sin_ref[...])
        pieces = []
        for refs, pre_ref, w_r, acc in (((q0, q1, q2), qa_ref, qw_ref, accq), ((k0, k1, k2), ka_ref, kw_ref, acck)):
            dh = branches(refs)
            dn = dh * cos + _rot_half(dh * sin)
            pre = pre_ref[...]
            rstd = lax.rsqrt(_segsum(pre * pre, g) * (1.0 / HEAD_DIM) + EPS)
            xh = pre * rstd
            acc[...] += jnp.sum(dn * xh, axis=0, keepdims=True)
            dxh = dn * w_r[...]
            pieces.append((rstd * (dxh - xh * (_segsum(dxh * xh, g) * (1.0 / HEAD_DIM)))).astype(BF16))
        pieces += [branches((v0, v1, v2)).astype(BF16), dqs_ref[...].astype(BF16), dks_ref[...].astype(BF16),
                   dvs_ref[...].astype(BF16)]
        dproj = jnp.concatenate(pieces, axis=1)
        hv = h_ref[...]
        dh = jnp.zeros((tm, D), F32)
        for j in range(N_CHIPS):
            dp = dproj[:, j * wc:(j + 1) * wc]
            dw_ref[j] += _dot_tn(hv, dp)
            dh = dh + _dot_nt(dp, w_ref[j])
        dx, dw = _rms_bwd(dh, x_ref[...], aw_ref[...])
        daw_ref[...] += jnp.sum(dw, axis=0, keepdims=True)
        gx_ref[...] = dx1_ref[...] + dx

        @pl.when(i == S // tm - 1)
        def _():
            for acc, o_ref in ((accq, dqw_ref), (acck, dkw_ref)):
                a = acc[...]
                pair = (a[:, 0:LANES] + a[:, LANES:2 * LANES]) + (a[:, 2 * LANES:3 * LANES] + a[:, 3 * LANES:4 * LANES])
                o_ref[...] = pair + pltpu.roll(pair, HEAD_DIM, 1)

    row = lambda i: (i, 0)
    gsp = pl.BlockSpec((tm, GROUP), row)
    dsp = pl.BlockSpec((tm, D), row)
    tab = pl.BlockSpec((tm, LANES), row)
    planes = [_strided_spec(tm, r) for r in DILATIONS]
    return pl.pallas_call(
        body, name="attn_in_bwd", grid=(S // tm,),
        in_specs=planes * 3 + [gsp] * 5 + [_full((1, GROUP)), _full((1, GROUP)), tab, tab, _full((GROUP, GROUP // 2)),
                                          dsp, _full((N_CHIPS, D, wc)), dsp, dsp, _full((1, D))],
        out_specs=[dsp, _full((N_CHIPS, D, wc)), _full((1, D)), _full((1, LANES)), _full((1, LANES))],
        out_shape=[jax.ShapeDtypeStruct((S, D), F32), jax.ShapeDtypeStruct((N_CHIPS, D, wc), F32),
                   jax.ShapeDtypeStruct((1, D), F32), jax.ShapeDtypeStruct((1, LANES), F32),
                   jax.ShapeDtypeStruct((1, LANES), F32)],
        scratch_shapes=[pltpu.VMEM((1, GROUP), F32), pltpu.VMEM((1, GROUP), F32), _strided_scratch(tm)],
    )(*dq_br, *dk_br, *dv_br, dqs, dks, dvs, qa, ka, qw, kw, cos_t, sin_t, seg_ones, h, w_in_g, x, dx1, attn_w)


def _constants(S):
    pos = jnp.arange(S, dtype=F32)
    inv_freq = ROPE_THETA ** (-jnp.arange(0, HEAD_DIM, 2, dtype=F32) / HEAD_DIM)
    ang_a = pos[::BLOCK, None] * inv_freq[None, :]
    ang_b = pos[:BLOCK, None] * inv_freq[None, :]
    ca, sa, cb, sb = jnp.cos(ang_a)[:, None], jnp.sin(ang_a)[:, None], jnp.cos(ang_b)[None], jnp.sin(ang_b)[None]
    cos = (ca * cb - sa * sb).reshape(S, HEAD_DIM // 2)
    sin = (sa * cb + ca * sb).reshape(S, HEAD_DIM // 2)
    cos_t = jnp.concatenate([cos, cos] * 2, axis=1)
    sin_t = jnp.concatenate([-sin, sin] * 2, axis=1)
    idx = jnp.arange(GROUP // 2)
    seg_ones = (idx[:, None] // HEAD_DIM == idx[None, :] // HEAD_DIM).astype(BF16)
    seg_ones = jnp.concatenate([seg_ones, seg_ones], axis=0)
    r = jnp.arange(BLOCK)
    ones = jnp.ones((BLOCK, BLOCK), BF16)
    tris = [jnp.concatenate([jnp.concatenate([m.astype(BF16), ones], axis=1)] * 2, axis=0) for m in
            (r[:, None] > r[None, :],
             r[:, None] <= r[None, :],
             r[:, None] < r[None, :])]
    return cos_t, sin_t, seg_ones, tris


FFN_NAMES = ("w_gate", "w_up", "w_down")


def _device_step(x, target, attn_w, qn_w, kn_w, dil_w, sbn_w, ffn_w, w_in_g, w_out_slots, ffn_slots, core, chip):
    S = x.shape[0]
    cos_t, sin_t, seg_ones, (tri_later, tri_upto, tri_before) = _constants(S)
    reps = GROUP // HEAD_DIM
    qw = jnp.tile(qn_w, (1, reps))
    kw = jnp.tile(kn_w, (1, reps))

    nd = len(DILATIONS)
    h, qa, ka, *rest, wg_g, w_out_g = _in_proj_fwd(x, attn_w, w_in_g, qw, kw, cos_t, sin_t, seg_ones,
                                                   [ffn_slots[0], w_out_slots])
    w_out_g = w_out_g.reshape(-1, x.shape[1])
    qh, kh, va, (qs, ks, vs) = rest[:nd], rest[nd:2 * nd], rest[2 * nd:3 * nd], rest[3 * nd:]
    branches = [_dil_fwd(qh[b], kh[b], va[b], []) for b in range(nd)]
    o_sb, ltot, walked, wu_g, wd_g = _sb_fwd(qs, ks, vs, tri_later, ffn_slots[1:])
    o_dil, *lse, x1 = _out_proj_fwd([b[0] for b in branches], [b[1] for b in branches], o_sb, x, dil_w, sbn_w,
                                    w_out_g, [])
    h2, g, u, dy, loss_parts = _ffn_fwd(x1, target, ffn_w, wg_g, wu_g, wd_g)

    *ffn_grads, dh2p, n0, n1, n2 = _ffn_bwd(h2, dy, g, u, wg_g, wu_g, wd_g)
    dx1, *mid, dw_out, dffn_w, ddil_w, dsbn_w, p0, p1, p2 = _out_proj_bwd(
        dh2p, dy, x1, ffn_w, w_out_g, o_dil, o_sb, dil_w, sbn_w, seg_ones, [n0, n1, n2])
    do_dil, do_sb, delta = mid[:nd], mid[nd], mid[nd + 1:]
    parts = [_pair_sum(gr, fr, core, n) for gr, fr, n in zip(ffn_grads, (p0, p1, p2), FFN_NAMES)]
    dqs, dks, dvs = _sb_bwd(walked[:, 0, 0], qs, ks, vs, do_sb, ltot, tri_upto, tri_before)
    dw_out = dw_out.reshape(N_CHIPS, -1, x.shape[1])
    dbr = [None] * nd
    dbr[0] = _dil_bwd(qh[0], kh[0], va[0], do_dil[0], lse[0], delta[0], [parts[0]], [dw_out])
    out_part = _pair_sum(dw_out, dbr[0][4], core, "w_out")
    dbr[1] = _dil_bwd(qh[1], kh[1], va[1], do_dil[1], lse[1], delta[1], [parts[1], out_part], [])
    dbr[2] = _dil_bwd(qh[2], kh[2], va[2], do_dil[2], lse[2], delta[2], [parts[2]], [])
    ffn_halves = [_chip_sum(dbr[b][3], parts[b], chip, FFN_NAMES[b]) for b in range(nd)]
    w_out_half = _chip_sum(dbr[1][4], out_part, chip, "w_out")
    grad_x, dw_in, dattn_w, dqw, dkw = _attn_in_bwd(
        [b[0] for b in dbr], [b[1] for b in dbr], [b[2] for b in dbr], dqs, dks, dvs,
        qa, ka, qw, kw, cos_t, sin_t, seg_ones, h, w_in_g, x, dx1, attn_w)
    small = dict(attn=dattn_w, q=dqw[:, :HEAD_DIM], k=dkw[:, :HEAD_DIM], dil=ddil_w, sb=dsbn_w, ffn=dffn_w)
    return loss_parts, grad_x, small, dw_in, w_out_half, ffn_halves


HBM = pl.BlockSpec(memory_space=pltpu.HBM)
VMEM = pl.BlockSpec(memory_space=pltpu.VMEM)
CHIP_FLIPS = ((1, 0), (0, 1), (1, 1))


def _place():
    return lax.axis_index("x"), lax.axis_index("y"), lax.axis_index("c")


def _flip(v, d):
    return 1 - v if d else v


def _half_rows(c, n):
    return pl.ds(pl.multiple_of(c * (n // 2), 16), n // 2)


def _gather_plan(slot_in, slot_out, send, recv):
    x, y, c = _place()
    p = 2 * x + y
    chips = [(_flip(x, dx), _flip(y, dy)) for dx, dy in CHIP_FLIPS]
    mine, other = _half_rows(c, slot_in.shape[1]), _half_rows(1 - c, slot_in.shape[1])

    def copy(k, src, dst, to):
        return pltpu.make_async_remote_copy(src_ref=src, dst_ref=dst, send_sem=send.at[k], recv_sem=recv.at[k],
                                            device_id=to, device_id_type=MESH)

    def first(k):
        return copy(k, slot_in.at[p, mine], slot_out.at[p, mine], (*chips[k], c))

    def passed(k, rows):
        land = slot_out.at[2 * chips[k][0] + chips[k][1], rows]
        return copy(3 + k, land, land, (x, y, 1 - c))

    def start():
        for k in range(3):
            first(k).start()

    def forward():
        for k in range(3):
            land = slot_out.at[2 * chips[k][0] + chips[k][1], mine]
            copy(k, land, land, (*chips[k], c)).wait_recv()
            passed(k, mine).start()

    def finish():
        for k in range(3):
            passed(k, other).wait_recv()
        for k in range(3):
            first(k).wait_send()
            passed(k, mine).wait_send()

    return start, forward, finish


def _chip_send_plan(part_in, recv_out, send, recv):
    x, y, c = _place()
    p = 2 * x + y
    chips = [(_flip(x, dx), _flip(y, dy)) for dx, dy in CHIP_FLIPS]

    def copy(k):
        q = 2 * chips[k][0] + chips[k][1]
        return pltpu.make_async_remote_copy(src_ref=part_in.at[q], dst_ref=recv_out.at[p], send_sem=send.at[k],
                                            recv_sem=recv.at[k], device_id=(*chips[k], c), device_id_type=MESH)

    def start():
        for k in range(3):
            copy(k).start()

    def finish():
        for k in range(3):
            land = recv_out.at[2 * chips[k][0] + chips[k][1]]
            pltpu.make_async_remote_copy(src_ref=land, dst_ref=land, send_sem=send.at[k], recv_sem=recv.at[k],
                                         device_id=(*chips[k], c), device_id_type=MESH).wait_recv()
        for k in range(3):
            copy(k).wait_send()

    return start, finish


def _pair_send_plan(grad_in, recv_out, send, recv):
    x, y, c = _place()

    def copy():
        theirs = _half_rows(1 - c, grad_in.shape[1])
        return pltpu.make_async_remote_copy(src_ref=grad_in.at[:, theirs, :], dst_ref=recv_out, send_sem=send,
                                            recv_sem=recv, device_id=(x, y, 1 - c), device_id_type=MESH)

    return (lambda: copy().start()), (lambda: copy().wait())


def _own_slots(shard):
    here = 2 * lax.axis_index("x") + lax.axis_index("y")
    return lax.dynamic_update_slice(lax.empty((N_CHIPS,) + shard.shape, shard.dtype), shard[None], (here, 0, 0))


def _gather_weights(shards):
    n = len(shards)

    def body(*refs):
        ins, outs = refs[:n], refs[n:2 * n]
        send, recv = refs[2 * n:]
        plans = [_gather_plan(ins[a], outs[a], send.at[pl.ds(6 * a, 6)], recv.at[pl.ds(6 * a, 6)]) for a in range(n)]
        for stage in range(3):
            for plan in plans:
                plan[stage]()

    slots = [_own_slots(s) for s in shards]
    return pl.pallas_call(
        body, name="gather_weights", in_specs=[HBM] * n, out_specs=[HBM] * n,
        out_shape=[jax.ShapeDtypeStruct(s.shape, s.dtype) for s in slots],
        input_output_aliases={a: a for a in range(n)},
        scratch_shapes=[pltpu.SemaphoreType.DMA((6 * n,)), pltpu.SemaphoreType.DMA((6 * n,))],
    )(*slots)


def _pair_exchange(grads, small):
    n = len(grads)

    def body(*refs):
        gin, sm = refs[:n], refs[n]
        gout, sm_all = refs[n + 1:2 * n + 1], refs[2 * n + 1]
        send, recv = refs[2 * n + 2:]
        x, y, c = _place()
        me = 4 * x + 2 * y + c
        big = [_pair_send_plan(gin[a], gout[a], send.at[a], recv.at[a]) for a in range(n)]
        for start, _ in big:
            start()
        sm_all[pl.ds(me, 1)] = sm[...][None]
        tiny = []
        for k in range(1, N_DEV):
            px, py, pc = _flip(x, k & 4), _flip(y, k & 2), _flip(c, k & 1)
            tiny.append((pltpu.make_async_remote_copy(
                src_ref=sm, dst_ref=sm_all.at[me], send_sem=send.at[n + k - 1], recv_sem=recv.at[n + k - 1],
                device_id=(px, py, pc), device_id_type=MESH), 4 * px + 2 * py + pc))
            tiny[-1][0].start()
        for k, (cp, peer) in enumerate(tiny):
            pltpu.make_async_remote_copy(src_ref=sm, dst_ref=sm_all.at[peer], send_sem=send.at[n + k],
                                         recv_sem=recv.at[n + k], device_id=(x, y, c),
                                         device_id_type=MESH).wait_recv()
            cp.wait_send()
        for _, finish in big:
            finish()

    halves = [jax.ShapeDtypeStruct((g.shape[0], g.shape[1] // 2, g.shape[2]), g.dtype) for g in grads]
    return pl.pallas_call(
        body, name="pair_exchange", in_specs=[HBM] * n + [VMEM], out_specs=[HBM] * n + [VMEM],
        out_shape=halves + [jax.ShapeDtypeStruct((N_DEV,) + small.shape, small.dtype)],
        scratch_shapes=[pltpu.SemaphoreType.DMA((n + N_DEV - 1,)), pltpu.SemaphoreType.DMA((n + N_DEV - 1,))],
    )(*grads, small)


def _chip_exchange(parts):
    n = len(parts)

    def body(*refs):
        pin, pout = refs[:n], refs[n:2 * n]
        send, recv = refs[2 * n:]
        plans = [_chip_send_plan(pin[a], pout[a], send.at[pl.ds(3 * a, 3)], recv.at[pl.ds(3 * a, 3)]) for a in range(n)]
        for stage in range(2):
            for plan in plans:
                plan[stage]()

    return pl.pallas_call(
        body, name="chip_exchange", in_specs=[HBM] * n, out_specs=[HBM] * n,
        out_shape=[jax.ShapeDtypeStruct(s.shape, s.dtype) for s in parts],
        scratch_shapes=[pltpu.SemaphoreType.DMA((3 * n,)), pltpu.SemaphoreType.DMA((3 * n,))],
    )(*parts)


def _pair_swap_plan(hin, hout, send, recv):
    x, y, c = _place()

    def copies():
        return [pltpu.make_async_remote_copy(src_ref=hin[a], dst_ref=hout[a], send_sem=send.at[a], recv_sem=recv.at[a],
                                             device_id=(x, y, 1 - c), device_id_type=MESH) for a in range(len(hin))]

    def start():
        for cp in copies():
            cp.start()

    def finish():
        for cp in copies():
            cp.wait()

    return start, finish


def _pair_swap(halves):
    n = len(halves)

    def body(*refs):
        start, finish = _pair_swap_plan(refs[:n], refs[n:2 * n], *refs[2 * n:])
        start()
        finish()

    return pl.pallas_call(
        body, name="pair_swap", in_specs=[HBM] * n, out_specs=[HBM] * n,
        out_shape=[jax.ShapeDtypeStruct(s.shape, s.dtype) for s in halves],
        scratch_shapes=[pltpu.SemaphoreType.DMA((n,)), pltpu.SemaphoreType.DMA((n,))],
    )(*halves)


def _pair_sum(grad, recv, c, tag):
    _, R, C = grad.shape
    hr = R // 2

    def body(c_ref, a_ref, b_ref, o_ref):
        o_ref[...] = (a_ref[...] + b_ref[...]).astype(BF16)

    return pl.pallas_call(
        body, name="pair_sum_" + tag,
        grid_spec=pltpu.PrefetchScalarGridSpec(
            num_scalar_prefetch=1, grid=(N_CHIPS,),
            in_specs=[pl.BlockSpec((1, hr, C), lambda s, cr: (s, cr[0], 0)),
                      pl.BlockSpec((1, hr, C), lambda s, cr: (s, 0, 0))],
            out_specs=pl.BlockSpec((1, hr, C), lambda s, cr: (s, 0, 0))),
        out_shape=jax.ShapeDtypeStruct((N_CHIPS, hr, C), BF16),
    )(c, grad, recv)


def _chip_sum(received, own, chip, tag):
    _, rows, C = received.shape
    tr = rows // 2

    def body(chip_ref, own_ref, r1_ref, r2_ref, r3_ref, o_ref):
        p = [r[0].astype(F32) for r in (own_ref, r1_ref, r2_ref, r3_ref)]
        o_ref[...] = (p[0] + p[1]) + (p[2] + p[3])

    def slot(k):
        return pl.BlockSpec((1, tr, C), lambda i, cr: (jnp.bitwise_xor(cr[0], k), i, 0))

    return pl.pallas_call(
        body, name="chip_sum_" + tag,
        grid_spec=pltpu.PrefetchScalarGridSpec(
            num_scalar_prefetch=1, grid=(rows // tr,), in_specs=[slot(0), slot(1), slot(2), slot(3)],
            out_specs=pl.BlockSpec((tr, C), lambda i, cr: (i, 0))),
        out_shape=jax.ShapeDtypeStruct((rows, C), F32),
    )(chip, own, received, received, received)


def _adamw_math(w, g, m, v):
    m = ADAM_B1 * m + (1.0 - ADAM_B1) * g
    v = ADAM_B2 * v + (1.0 - ADAM_B2) * (g * g)
    m_hat = m / (1.0 - ADAM_B1 ** ADAM_STEP)
    v_hat = v / (1.0 - ADAM_B2 ** ADAM_STEP)
    delta = -ADAM_LR * (m_hat / (jnp.sqrt(v_hat) + ADAM_EPS) + ADAM_WD * w)
    return delta, m, v


def _adamw(w, g_mine, g_other, m, v, c, tag):
    R, C = w.shape
    tr = R // 4

    def body(c_ref, w_ref, gm_ref, go_ref, m_ref, v_ref, g_ref, d_ref, nm_ref, nv_ref):
        g = jnp.where(pl.program_id(0) == c_ref[0], gm_ref[...], go_ref[...])
        g_ref[...] = g
        d_ref[...], nm_ref[...], nv_ref[...] = _adamw_math(w_ref[...], g, m_ref[...], v_ref[...])

    blk = pl.BlockSpec((tr, C), lambda h, i, cr: (2 * h + i, 0))
    half = pl.BlockSpec((tr, C), lambda h, i, cr: (i, 0))
    return pl.pallas_call(
        body, name="adamw_" + tag,
        grid_spec=pltpu.PrefetchScalarGridSpec(
            num_scalar_prefetch=1, grid=(2, 2), in_specs=[blk, half, half, blk, blk], out_specs=[blk] * 4),
        out_shape=[jax.ShapeDtypeStruct((R, C), F32)] * 4,
    )(c, w, g_mine, g_other, m, v)


def _small_update(all_small, w, m, v):
    def body(a_ref, w_ref, m_ref, v_ref, g_ref, d_ref, nm_ref, nv_ref):
        g = ((a_ref[0] + a_ref[1]) + (a_ref[2] + a_ref[3])) + ((a_ref[4] + a_ref[5]) + (a_ref[6] + a_ref[7]))
        g_ref[...] = g
        d_ref[...], nm_ref[...], nv_ref[...] = _adamw_math(w_ref[...], g, m_ref[...], v_ref[...])

    return pl.pallas_call(
        body, name="small_update", out_shape=[jax.ShapeDtypeStruct(w.shape, F32)] * 4,
    )(all_small, w, m, v)


SMALL_ROWS = (("attn", 0, 0), ("ffn", 1, 0), ("dil", 2, 0), ("sb", 2, GROUP), ("q", 3, 0), ("k", 3, HEAD_DIM),
              ("loss", 4, 0))


def _pack_small(vals, D):
    rows = [jnp.zeros((1, D), F32) for _ in range(8)]
    for name, r, off in SMALL_ROWS:
        if name in vals:
            rows[r] = lax.dynamic_update_slice(rows[r], vals[name].astype(F32), (0, off))
    return jnp.concatenate(rows, axis=0)


def _unpack_small(packed, vals):
    return {name: packed[r:r + 1, off:off + vals[name].shape[1]] for name, r, off in SMALL_ROWS if name in vals}


def kernel(x, attn_norm_w, w_in, q_norm_w, k_norm_w, dil_out_norm_w, sb_out_norm_w, w_out, ffn_norm_w, w_gate, w_up, w_down, loss_target, m_attn_norm_w, m_w_in, m_q_norm_w, m_k_norm_w, m_dil_out_norm_w, m_sb_out_norm_w, m_w_out, m_ffn_norm_w, m_w_gate, m_w_up, m_w_down, v_attn_norm_w, v_w_in, v_q_norm_w, v_k_norm_w, v_dil_out_norm_w, v_sb_out_norm_w, v_w_out, v_ffn_norm_w, v_w_gate, v_w_up, v_w_down):
    D = x.shape[-1]
    big_names = ("w_in", "w_out", "w_gate", "w_up", "w_down")
    flipped = ("w_gate", "w_up")
    tr = lambda a: jnp.swapaxes(a[0], 0, 1)
    big_w = dict(w_in=w_in[0], w_out=w_out[0], w_gate=tr(w_gate), w_up=tr(w_up), w_down=w_down[0])
    big_m = dict(w_in=m_w_in[0], w_out=m_w_out[0], w_gate=tr(m_w_gate), w_up=tr(m_w_up), w_down=m_w_down[0])
    big_v = dict(w_in=v_w_in[0], w_out=v_w_out[0], w_gate=tr(v_w_gate), w_up=tr(v_w_up), w_down=v_w_down[0])
    small_w = dict(attn=attn_norm_w, q=q_norm_w, k=k_norm_w, dil=dil_out_norm_w, sb=sb_out_norm_w, ffn=ffn_norm_w)
    small_m = dict(attn=m_attn_norm_w, q=m_q_norm_w, k=m_k_norm_w, dil=m_dil_out_norm_w, sb=m_sb_out_norm_w,
                   ffn=m_ffn_norm_w)
    small_v = dict(attn=v_attn_norm_w, q=v_q_norm_w, k=v_k_norm_w, dil=v_dil_out_norm_w, sb=v_sb_out_norm_w,
                   ffn=v_ffn_norm_w)

    c = lax.axis_index("c").astype(jnp.int32).reshape(1)
    chip = (2 * lax.axis_index("x") + lax.axis_index("y")).astype(jnp.int32).reshape(1)
    (w_in_g,) = _gather_weights([big_w["w_in"].astype(BF16)])
    w_out_slots = _own_slots(big_w["w_out"].astype(BF16))
    ffn_slots = [_own_slots(big_w[n].astype(BF16)) for n in FFN_NAMES]

    loss_parts, grad_x, small_g, dw_in, w_out_half, ffn_halves = _device_step(
        x[0], loss_target[0], attn_norm_w, q_norm_w, k_norm_w, dil_out_norm_w, sb_out_norm_w, ffn_norm_w,
        w_in_g, w_out_slots, ffn_slots, c, chip)
    small_g["loss"] = (jnp.sum(loss_parts[:, 0, 0]) * (0.5 / D)).reshape(1, 1)

    from_pair, all_small = _pair_exchange([dw_in], _pack_small(small_g, D))
    in_part = _pair_sum(dw_in, from_pair, c, "w_in")
    (from_chips,) = _chip_exchange([in_part])
    halves = [_chip_sum(from_chips, in_part, chip, "w_in"), w_out_half] + ffn_halves
    others = _pair_swap(halves)
    big_out = {n: _adamw(big_w[n], mine, other, big_m[n], big_v[n], c, n)
               for n, mine, other in zip(big_names, halves, others)}
    sg, sd, sm, sv = _small_update(all_small, _pack_small(small_w, D), _pack_small(small_m, D),
                                   _pack_small(small_v, D))
    small_out = [_unpack_small(t, small_w) for t in (sg, sd, sm, sv)]

    order = (("attn", None), (None, "w_in"), ("q", None), ("k", None), ("dil", None), ("sb", None),
             (None, "w_out"), ("ffn", None), (None, "w_gate"), (None, "w_up"), (None, "w_down"))
    outs = [sg[4, 0], grad_x[None]]
    for kind in range(4):
        for s_name, b_name in order:
            if s_name is not None:
                outs.append(small_out[kind][s_name])
            else:
                res = big_out[b_name][kind]
                outs.append((jnp.swapaxes(res, 0, 1) if b_name in flipped else res)[None])
    return tuple(outs)
```

```python
import functools

import jax
import jax.numpy as jnp
from jax import lax
from jax.experimental import pallas as pl
from jax.experimental.pallas import tpu as pltpu

F32 = jnp.float32
BF16 = jnp.bfloat16
MESH = pl.DeviceIdType.MESH

HEAD_DIM = 64
GROUP = 512
BLOCK = 128
LANES = 128
N_CHIPS = 4
N_DEV = 8
EPS = 1e-6
ROPE_THETA = 10000.0
DILATIONS = (1, 4, 16)
NEG = -1e30

ADAM_LR = 0.001
ADAM_B1 = 0.9
ADAM_B2 = 0.999
ADAM_EPS = 1e-08
ADAM_WD = 0.01
ADAM_STEP = 10


def _dot(a, b):
    return jnp.dot(a, b, preferred_element_type=F32)


def _dot_nt(a, b):
    return lax.dot_general(a, b, (((1,), (1,)), ((), ())), preferred_element_type=F32)


def _dot_tn(a, b):
    return lax.dot_general(a, b, (((0,), (0,)), ((), ())), preferred_element_type=F32)


def _split(v):
    hi = lax.bitcast_convert_type(lax.bitcast_convert_type(v, jnp.uint32) & jnp.uint32(0xFFFF0000), F32)
    return hi.astype(BF16), (v - hi).astype(BF16)


def _segsum(v, g):
    hi, lo = _split(v)
    w = g.shape[1]
    return jnp.concatenate([_dot(jnp.concatenate([hi[:, c:c + w], lo[:, c:c + w]], axis=1), g)
                            for c in range(0, v.shape[1], w)], axis=1)


def _rot_half(x):
    outs = []
    for c in range(x.shape[1] // LANES):
        xc = x[:, c * LANES:(c + 1) * LANES]
        lane = lax.broadcasted_iota(jnp.int32, xc.shape, 1)
        first = (lane % HEAD_DIM) < (HEAD_DIM // 2)
        outs.append(jnp.where(first, pltpu.roll(xc, LANES - 32, 1), pltpu.roll(xc, 32, 1)))
    return outs[0] if len(outs) == 1 else jnp.concatenate(outs, axis=1)


def _rms(x):
    return lax.rsqrt(jnp.mean(x * x, axis=-1, keepdims=True) + EPS)


def _rms_bwd(dy, x, w):
    rstd = _rms(x)
    xh = x * rstd
    dxh = dy * w
    dx = rstd * (dxh - xh * jnp.mean(dxh * xh, axis=-1, keepdims=True))
    return dx, dy * xh


def _sigmoid(x):
    return 1.0 / (1.0 + jnp.exp(-x))


def _sum4(ref):
    p = [ref[j].astype(F32) for j in range(N_CHIPS)]
    return (p[0] + p[1]) + (p[2] + p[3])


def _full(shape):
    n = len(shape)
    return pl.BlockSpec(shape, lambda *_: (0,) * n)


def _strided_spec(tm, r):
    return pl.BlockSpec((r, tm // r, GROUP), lambda i: (0, i, 0))


def _strided_shape(S, r, dtype):
    return jax.ShapeDtypeStruct((r, S // r, GROUP), dtype)


def _to_strided(scr, val, outs):
    chunks = range(GROUP // LANES)
    for k in chunks:
        scr[k] = val[:, _lanes(k)]
    for r, o_ref in outs:
        if r == 1:
            o_ref[0] = val.astype(o_ref.dtype)
            continue
        n = val.shape[0] // r
        for c in range(r):
            rows = pl.ds(c, n, stride=r)
            o_ref[c] = jnp.concatenate([scr.at[k][rows, :] for k in chunks], axis=1).astype(o_ref.dtype)


def _from_strided(scr, ref):
    r, n, _ = ref.shape
    if r == 1:
        return ref[0].astype(F32)
    chunks = range(GROUP // LANES)
    for c in range(r):
        plane = ref[c].astype(F32)
        for k in chunks:
            scr.at[k][pl.ds(c, n, stride=r), :] = plane[:, _lanes(k)]
    return jnp.concatenate([scr[k] for k in chunks], axis=1)


def _strided_scratch(tm):
    return pltpu.VMEM((GROUP // LANES, tm, LANES), F32)


def _tile4(t):
    return jnp.concatenate([t] * (GROUP // LANES), axis=1)


def _in_proj_fwd(x, attn_w, w_in_g, qw, kw, cos_t, sin_t, seg_ones, slots):
    S, D = x.shape
    tm = 512
    wcols = w_in_g.shape[2]
    nd = len(DILATIONS)
    ns = len(slots)

    def body(x_ref, aw_ref, w_ref, qw_ref, kw_ref, cos_ref, sin_ref, g_ref, *rest):
        slot_in, (h_ref, qa_ref, ka_ref), rest = rest[:ns], rest[ns:ns + 3], rest[ns + 3:]
        q_refs, k_refs, v_refs = rest[:nd], rest[nd:2 * nd], rest[2 * nd:3 * nd]
        qs_ref, ks_ref, vs_ref = rest[3 * nd:3 * nd + 3]
        slot_out, scr, sems = rest[3 * nd + 3:3 * nd + 3 + ns], rest[3 * nd + 3 + ns], rest[3 * nd + 4 + ns:]
        finish = _hosted_gathers(slot_in, slot_out, *sems, pl.program_id(0), S // tm) if ns else None
        xv = x_ref[...]
        h = (xv * _rms(xv) * aw_ref[...]).astype(BF16)
        h_ref[...] = h
        proj = jnp.concatenate([_dot(h, w_ref[j]) for j in range(N_CHIPS)], axis=1)
        qa = proj[:, 0 * GROUP:1 * GROUP]
        ka = proj[:, 1 * GROUP:2 * GROUP]
        qa_ref[...] = qa
        ka_ref[...] = ka
        _to_strided(scr, proj[:, 2 * GROUP:3 * GROUP], list(zip(DILATIONS, v_refs)))
        qs_ref[...] = proj[:, 3 * GROUP:4 * GROUP].astype(BF16)
        ks_ref[...] = proj[:, 4 * GROUP:5 * GROUP].astype(BF16)
        vs_ref[...] = proj[:, 5 * GROUP:6 * GROUP].astype(BF16)
        g = g_ref[...]
        cos = _tile4(cos_ref[...])
        sin = _tile4(sin_ref[...])
        for t, w_r, o_rs in ((qa, qw_ref, q_refs), (ka, kw_ref, k_refs)):
            rstd = lax.rsqrt(_segsum(t * t, g) * (1.0 / HEAD_DIM) + EPS)
            tn = t * rstd * w_r[...]
            _to_strided(scr, tn * cos + _rot_half(tn) * sin, list(zip(DILATIONS, o_rs)))
        if finish is not None:
            finish()

    row = lambda i: (i, 0)
    tile = lambda n, dt: jax.ShapeDtypeStruct((S, n), dt)
    planes = [_strided_spec(tm, r) for r in DILATIONS]
    h_in, h_out, h_shape, h_sems = _hosted_specs(slots)
    n_out = 6 + 3 * nd
    return pl.pallas_call(
        body, name="in_proj_fwd", grid=(S // tm,),
        in_specs=[pl.BlockSpec((tm, D), row), _full((1, D)), _full((N_CHIPS, D, wcols)),
                  _full((1, GROUP)), _full((1, GROUP)),
                  pl.BlockSpec((tm, LANES), row), pl.BlockSpec((tm, LANES), row),
                  _full((GROUP, GROUP // 2))] + h_in,
        out_specs=[pl.BlockSpec((tm, D), row)] + [pl.BlockSpec((tm, GROUP), row)] * 2 + planes * 3
                  + [pl.BlockSpec((tm, GROUP), row)] * 3 + h_out,
        out_shape=[tile(D, BF16), tile(GROUP, F32), tile(GROUP, F32)]
                  + [_strided_shape(S, r, BF16) for r in DILATIONS] * 3 + [tile(GROUP, BF16)] * 3 + h_shape,
        input_output_aliases={8 + a: n_out + a for a in range(ns)},
        scratch_shapes=[_strided_scratch(tm)] + h_sems,
    )(x, attn_w, w_in_g, qw, kw, cos_t, sin_t, seg_ones, *slots)


DIL_PLANES = 1


def _dil_fwd(q, k, v, slots):
    r, L, _ = q.shape
    nb = L // BLOCK
    P = GROUP // LANES
    PL = min(DIL_PLANES, r)
    ns = len(slots)
    units = [(pp, hp) for pp in range(PL) for hp in range(P)]

    def body(q_ref, kc_ref, kp_ref, vc_ref, vp_ref, *rest):
        o_ref, l_ref = rest[ns:ns + 2]
        n = pl.program_id(1)
        finish = (_hosted_gathers(rest[:ns], rest[ns + 2:2 * ns + 2], *rest[2 * ns + 2:],
                                  pl.program_id(0) * nb + n, (r // PL) * nb) if ns else None)
        rowi = lax.broadcasted_iota(jnp.int32, (BLOCK, BLOCK), 0)
        coli = lax.broadcasted_iota(jnp.int32, (BLOCK, BLOCK), 1)
        first = coli < HEAD_DIM
        masks = (coli <= rowi, jnp.logical_and(coli >= rowi, n > 0))
        s2 = {}
        for pp, hp in units:
            q2 = _scaled(q_ref[pp, :, _lanes(hp)])
            for b, k_ref in enumerate((kc_ref, kp_ref)):
                s2[pp, hp, b] = _dot_nt(q2, _by_head(k_ref[pp, :, _lanes(hp)], first))
        ps, inv, lse = {}, {}, {}
        for pp, hp in units:
            for h in range(2):
                s = [jnp.where(masks[b], s2[pp, hp, b][:, h * BLOCK:(h + 1) * BLOCK], NEG) for b in range(2)]
                m = jnp.maximum(jnp.max(s[0], axis=1, keepdims=True), jnp.max(s[1], axis=1, keepdims=True))
                p = [jnp.exp(s[b] - m) for b in range(2)]
                den = jnp.sum(p[0], axis=1, keepdims=True) + jnp.sum(p[1], axis=1, keepdims=True)
                ps[pp, hp, h] = [p[b].astype(BF16) for b in range(2)]
                inv[pp, hp, h] = 1.0 / den
                lse[pp, hp, h] = m + jnp.log(den)
        for pp, hp in units:
            o = jnp.zeros((BLOCK, LANES), F32)
            for b, v_ref in enumerate((vc_ref, vp_ref)):
                o = o + _dot(jnp.concatenate([ps[pp, hp, 0][b], ps[pp, hp, 1][b]], axis=1),
                             _by_head(v_ref[pp, :, _lanes(hp)], first))
            o_ref[pp, :, _lanes(hp)] = o * jnp.where(first, inv[pp, hp, 0], inv[pp, hp, 1])
            l_ref[pp, :, _lanes(hp)] = jnp.where(first, lse[pp, hp, 0], lse[pp, hp, 1])
        if finish is not None:
            finish()

    cur = pl.BlockSpec((PL, BLOCK, GROUP), lambda c, n: (c, n, 0))
    prev = pl.BlockSpec((PL, BLOCK, GROUP), lambda c, n: (c, jnp.maximum(n - 1, 0), 0))
    h_in, h_out, h_shape, h_sems = _hosted_specs(slots)
    return pl.pallas_call(
        body, name="dil_fwd_r%d" % r, grid=(r // PL, nb),
        in_specs=[cur, cur, prev, cur, prev] + h_in, out_specs=[cur, cur] + h_out,
        out_shape=[jax.ShapeDtypeStruct(q.shape, F32)] * 2 + h_shape,
        input_output_aliases={5 + a: 2 + a for a in range(ns)},
        scratch_shapes=h_sems,
    )(q, k, k, v, v, *slots)


def _dil_bwd(q, k, v, do, lse, delta, parts, pairs):
    r, L, _ = q.shape
    nb = L // BLOCK
    P = GROUP // LANES
    PL = min(DIL_PLANES, r)
    scale = HEAD_DIM ** -0.5
    units = [(pp, hp) for pp in range(PL) for hp in range(P)]
    npt, npr = len(parts), len(pairs)
    nx = npt + npr

    def body(qc_ref, qn_ref, doc_ref, don_ref, lc_ref, ln_ref, dc_ref, dn_ref, k_ref, v_ref, *rest):
        x_in, (dq_ref, dk_ref, dv_ref), x_out = rest[:nx], rest[nx:nx + 3], rest[nx + 3:2 * nx + 3]
        carry, send, recv = rest[2 * nx + 3:]
        j = pl.program_id(1)
        step = pl.program_id(0) * nb + j
        plans = [_chip_send_plan(x_in[a], x_out[a], send.at[pl.ds(3 * a, 3)], recv.at[pl.ds(3 * a, 3)])
                 for a in range(npt)]
        plans += [_pair_send_plan(x_in[npt + a], x_out[npt + a], send.at[3 * npt + a], recv.at[3 * npt + a])
                  for a in range(npr)]

        def start():
            for begin, _ in plans:
                begin()

        def finish():
            for _, end in plans:
                end()

        pl.when(step == 0)(start)
        rowi = lax.broadcasted_iota(jnp.int32, (BLOCK, BLOCK), 0)
        coli = lax.broadcasted_iota(jnp.int32, (BLOCK, BLOCK), 1)
        first = coli < HEAD_DIM
        sides = ((qc_ref, doc_ref, lc_ref, dc_ref, coli <= rowi),
                 (qn_ref, don_ref, ln_ref, dn_ref, jnp.logical_and(coli >= rowi, j < nb - 1)))

        @pl.when(j == 0)
        def _():
            carry[...] = jnp.zeros_like(carry)

        kcat, q2, do2, s2, dp2 = {}, {}, {}, {}, {}
        for pp, hp in units:
            kcat[pp, hp] = _by_head(k_ref[pp, :, _lanes(hp)], first)
            vcat = _by_head(v_ref[pp, :, _lanes(hp)], first)
            for x, (q_r, do_r, _, _, _) in enumerate(sides):
                q2[pp, hp, x] = _scaled(q_r[pp, :, _lanes(hp)])
                do2[pp, hp, x] = do_r[pp, :, _lanes(hp)]
                s2[pp, hp, x] = _dot_nt(q2[pp, hp, x], kcat[pp, hp])
                dp2[pp, hp, x] = _dot_nt(do2[pp, hp, x], vcat)
        pcat, dscat = {}, {}
        for pp, hp in units:
            for x, (_, _, l_r, d_r, msk) in enumerate(sides):
                ps, dss = [], []
                for h in range(2):
                    col = hp * LANES + h * HEAD_DIM
                    half = slice(h * BLOCK, (h + 1) * BLOCK)
                    p = jnp.where(msk, jnp.exp(s2[pp, hp, x][:, half] - l_r[pp, :, col:col + 1]), 0.0)
                    ps.append(p.astype(BF16))
                    dss.append((p * (dp2[pp, hp, x][:, half] - d_r[pp, :, col:col + 1])).astype(BF16))
                pcat[pp, hp, x] = jnp.concatenate(ps, axis=1)
                dscat[pp, hp, x] = jnp.concatenate(dss, axis=1)
        for pp, hp in units:
            dv2 = _dot_tn(pcat[pp, hp, 0], do2[pp, hp, 0]) + _dot_tn(pcat[pp, hp, 1], do2[pp, hp, 1])
            dk2 = _dot_tn(dscat[pp, hp, 0], q2[pp, hp, 0]) + _dot_tn(dscat[pp, hp, 1], q2[pp, hp, 1])
            dv_ref[pp, :, _lanes(hp)] = jnp.where(first, dv2[:BLOCK], dv2[BLOCK:]).astype(BF16)
            dk_ref[pp, :, _lanes(hp)] = jnp.where(first, dk2[:BLOCK], dk2[BLOCK:]).astype(BF16)
            dq_ref[pp, :, _lanes(hp)] = (carry[pp, :, _lanes(hp)]
                                         + _dot(dscat[pp, hp, 0], kcat[pp, hp]) * scale).astype(BF16)
            carry[pp, :, _lanes(hp)] = _dot(dscat[pp, hp, 1], kcat[pp, hp]) * scale
        pl.when(step == (r // PL) * nb - 1)(finish)

    cur = pl.BlockSpec((PL, BLOCK, GROUP), lambda c, n: (c, n, 0))
    nxt = pl.BlockSpec((PL, BLOCK, GROUP), lambda c, n: (c, jnp.minimum(n + 1, nb - 1), 0))
    return pl.pallas_call(
        body, name="dil_bwd_r%d" % r, grid=(r // PL, nb),
        in_specs=[cur, nxt, cur, nxt, cur, nxt, cur, nxt, cur, cur] + [HBM] * nx, out_specs=[cur, cur, cur] + [HBM] * nx,
        out_shape=[jax.ShapeDtypeStruct(q.shape, BF16)] * 3 + [jax.ShapeDtypeStruct(p.shape, p.dtype) for p in parts]
                  + [jax.ShapeDtypeStruct((g.shape[0], g.shape[1] // 2, g.shape[2]), g.dtype) for g in pairs],
        scratch_shapes=[pltpu.VMEM((PL, BLOCK, GROUP), F32), pltpu.SemaphoreType.DMA((3 * npt + npr,)),
                        pltpu.SemaphoreType.DMA((3 * npt + npr,))],
    )(q, q, do, do, lse, lse, delta, delta, k, v, *parts, *pairs)


SB_TILES = 2
SB_PAIRS_FWD = 4
SB_PAIRS_BWD = 4
SB_DEAD = -110.0


def _lanes(hp):
    return slice(hp * LANES, (hp + 1) * LANES)


def _sb_logits(z, valid):
    e = jnp.exp(-jnp.abs(z))
    lb = jnp.minimum(z, 0.0) - jnp.log(1.0 + e)
    lk = lb - z
    if valid is not None:
        lk = jnp.where(valid, lk, 0.0)
    return e, lb, lk


def _by_head(t, first):
    zero = jnp.zeros_like(t)
    return jnp.concatenate([jnp.where(first, t, zero), jnp.where(first, zero, t)], axis=0)


def _sb_valid(i, j):
    rowi = lax.broadcasted_iota(jnp.int32, (BLOCK, BLOCK), 0)
    coli = lax.broadcasted_iota(jnp.int32, (BLOCK, BLOCK), 1)
    return (coli - rowi) < (i - j) * BLOCK


def _scaled(q):
    return (q.astype(F32) * (HEAD_DIM ** -0.5)).astype(BF16)


def _hosted_gathers(refs_in, refs_out, send, recv, step, steps):
    plans = [_gather_plan(refs_in[a], refs_out[a], send.at[pl.ds(6 * a, 6)], recv.at[pl.ds(6 * a, 6)])
             for a in range(len(refs_in))]
    for stage, at in ((0, 0), (1, (2 * steps) // 3)):
        @pl.when(step == at)
        def _():
            for plan in plans:
                plan[stage]()

    def finish():
        @pl.when(step == steps - 1)
        def _():
            for plan in plans:
                plan[2]()

    return finish


def _hosted_specs(slots):
    n = len(slots)
    sems = [pltpu.SemaphoreType.DMA((6 * n,))] * 2 if n else []
    return [HBM] * n, [HBM] * n, [jax.ShapeDtypeStruct(s.shape, s.dtype) for s in slots], sems


def _sb_fwd(qs, ks, vs, tri_later, slots):
    S = qs.shape[0]
    P = SB_PAIRS_FWD
    W = P * LANES
    ns = len(slots)

    def body(q_ref, k_ref, v_ref, u_ref, *rest):
        o_ref, lt_ref, from_ref = rest[ns:ns + 3]
        i = pl.program_id(1)
        finish = _hosted_gathers(rest[:ns], rest[ns + 3:2 * ns + 3], *rest[2 * ns + 3:], i, S // BLOCK) if ns else None
        first = lax.broadcasted_iota(jnp.int32, (BLOCK, LANES), 1) < HEAD_DIM
        q2 = [_scaled(q_ref[:, _lanes(hp)]) for hp in range(P)]

        def walk(tiles, carry):
            runs, accs = list(carry[0]), list(carry[1])
            units = [(t, hp) for t in range(len(tiles)) for hp in range(P)]
            offs = [pl.multiple_of(j * BLOCK, BLOCK) for j, _ in tiles]
            valids = [_sb_valid(i, j) if diag else None for j, diag in tiles]
            z2s, lbs, c2s = {}, {}, {}
            for t, hp in units:
                z2s[t, hp] = _dot_nt(q2[hp], _by_head(k_ref[pl.ds(offs[t], BLOCK), _lanes(hp)], first))
            for t, hp in units:
                for h in range(2):
                    _, lb, lk = _sb_logits(z2s[t, hp][:, h * BLOCK:(h + 1) * BLOCK], valids[t])
                    lbs[t, hp, h] = lb
                    c2s[t, hp, h] = _dot(jnp.concatenate(_split(lk), axis=1), u_ref[...])
            for t, hp in units:
                a2 = []
                for h in range(2):
                    a = jnp.exp(lbs[t, hp, h] + c2s[t, hp, h][:, :BLOCK] + runs[2 * hp + h])
                    if valids[t] is not None:
                        a = jnp.where(valids[t], a, 0.0)
                    a2.append(a.astype(BF16))
                    runs[2 * hp + h] = runs[2 * hp + h] + c2s[t, hp, h][:, BLOCK:]
                vcat = _by_head(v_ref[pl.ds(offs[t], BLOCK), _lanes(hp)], first)
                accs[hp] = accs[hp] + _dot(jnp.concatenate(a2, axis=1), vcat)
            return tuple(runs), tuple(accs)

        def chunk(ci, carry):
            return walk([(ci * SB_TILES + t, False) for t in reversed(range(SB_TILES))], carry)

        def alive(runs):
            top = functools.reduce(jnp.maximum, runs)
            return (jnp.max(top) > SB_DEAD).astype(jnp.int32)

        def step(c):
            t, _, runs, accs = c
            runs, accs = chunk(nfull - 1 - t, (runs, accs))
            return t + 1, alive(runs), runs, accs

        zero = jnp.zeros((BLOCK, LANES), F32)
        nfull = i // SB_TILES
        ragged = [functools.partial(walk, [(i, True)] + [(i - 1 - m, False) for m in range(extra)])
                  for extra in range(SB_TILES)]
        runs, accs = lax.switch(i % SB_TILES, ragged, ((zero,) * (2 * P), (zero,) * P))
        done, _, runs, accs = lax.while_loop(lambda c: jnp.logical_and(c[0] < nfull, c[1] > 0), step,
                                             (jnp.int32(0), alive(runs), runs, accs))
        for hp in range(P):
            o_ref[:, _lanes(hp)] = accs[hp]
            lt_ref[:, _lanes(hp)] = jnp.where(first, runs[2 * hp], runs[2 * hp + 1])
        from_ref[...] = jnp.full(from_ref.shape, nfull - done, jnp.int32)
        if finish is not None:
            finish()

    assert W == GROUP
    blk = pl.BlockSpec((BLOCK, W), lambda hp, i: (i, hp))
    col = pl.BlockSpec((S, W), lambda hp, i: (0, hp))
    h_in, h_out, h_shape, h_sems = _hosted_specs(slots)
    return pl.pallas_call(
        body, name="sb_fwd", grid=(GROUP // W, S // BLOCK),
        in_specs=[blk, col, col, _full((2 * BLOCK, 2 * BLOCK))] + h_in,
        out_specs=[blk, blk, pl.BlockSpec((1, 8, LANES), lambda hp, i: (i, 0, 0))] + h_out,
        out_shape=[jax.ShapeDtypeStruct((S, GROUP), F32)] * 2
                  + [jax.ShapeDtypeStruct((S // BLOCK, 8, LANES), jnp.int32)] + h_shape,
        input_output_aliases={4 + a: 3 + a for a in range(ns)},
        scratch_shapes=h_sems,
    )(qs, ks, vs, tri_later, *slots)


def _sb_bwd(first_chunk, qs, ks, vs, do, ltot, tri_upto, tri_before):
    S = qs.shape[0]
    P = SB_PAIRS_BWD
    W = P * LANES

    def body(from_ref, q_ref, k_ref, v_ref, do_ref, lt_ref, w_ref, x_ref, dq_ref, dk_ref, dv_ref):
        i = pl.program_id(1)

        @pl.when(i == 0)
        def _():
            dk_ref[...] = jnp.zeros_like(dk_ref)
            dv_ref[...] = jnp.zeros_like(dv_ref)

        first = lax.broadcasted_iota(jnp.int32, (BLOCK, LANES), 1) < HEAD_DIM
        q2 = [_scaled(q_ref[:, _lanes(hp)]) for hp in range(P)]
        do2 = [do_ref[:, _lanes(hp)] for hp in range(P)]
        totals = [jnp.broadcast_to(lt_ref[:, n * HEAD_DIM:n * HEAD_DIM + 1], (BLOCK, LANES)) for n in range(2 * P)]

        def walk(tiles, carry):
            keeps, grads, dqs = list(carry[0]), list(carry[1]), list(carry[2])
            units = [(t, hp) for t in range(len(tiles)) for hp in range(P)]
            offs = [pl.multiple_of(j * BLOCK, BLOCK) for j, _ in tiles]
            valids = [_sb_valid(i, j) if diag else None for j, diag in tiles]
            kcat, z2, da2, es, lbs, c2s, as_, des, p2s = {}, {}, {}, {}, {}, {}, {}, {}, {}
            for t, hp in units:
                kcat[t, hp] = _by_head(k_ref[pl.ds(offs[t], BLOCK), _lanes(hp)], first)
                z2[t, hp] = _dot_nt(q2[hp], kcat[t, hp])
                da2[t, hp] = _dot_nt(do2[hp], _by_head(v_ref[pl.ds(offs[t], BLOCK), _lanes(hp)], first))
            for t, hp in units:
                for h in range(2):
                    es[t, hp, h], lbs[t, hp, h], lk = _sb_logits(z2[t, hp][:, h * BLOCK:(h + 1) * BLOCK], valids[t])
                    c2s[t, hp, h] = _dot(jnp.concatenate(_split(lk), axis=1), w_ref[...])
            for t, hp in units:
                for h in range(2):
                    n = 2 * hp + h
                    a = jnp.exp(lbs[t, hp, h] + (totals[n] - (keeps[n] + c2s[t, hp, h][:, :BLOCK])))
                    if valids[t] is not None:
                        a = jnp.where(valids[t], a, 0.0)
                    keeps[n] = keeps[n] + c2s[t, hp, h][:, BLOCK:]
                    de = a * da2[t, hp][:, h * BLOCK:(h + 1) * BLOCK]
                    as_[t, hp, h], des[t, hp, h] = a.astype(BF16), de
                    p2s[t, hp, h] = _dot(jnp.concatenate(_split(de), axis=1), x_ref[...])
            for t, hp in units:
                dz2 = []
                for h in range(2):
                    n = 2 * hp + h
                    e = es[t, hp, h]
                    sig = jnp.where(z2[t, hp][:, h * BLOCK:(h + 1) * BLOCK] >= 0.0, 1.0, e) / (1.0 + e)
                    dz = des[t, hp, h] * (1.0 - sig) - (grads[n] + p2s[t, hp, h][:, :BLOCK]) * sig
                    if valids[t] is not None:
                        dz = jnp.where(valids[t], dz, 0.0)
                    grads[n] = grads[n] + p2s[t, hp, h][:, BLOCK:]
                    dz2.append(dz.astype(BF16))
                dzcat = jnp.concatenate(dz2, axis=1)
                dk2 = _dot_tn(dzcat, q2[hp])
                dv2 = _dot_tn(jnp.concatenate([as_[t, hp, 0], as_[t, hp, 1]], axis=1), do2[hp])
                dk_ref[pl.ds(offs[t], BLOCK), _lanes(hp)] += jnp.where(first, dk2[:BLOCK], dk2[BLOCK:])
                dv_ref[pl.ds(offs[t], BLOCK), _lanes(hp)] += jnp.where(first, dv2[:BLOCK], dv2[BLOCK:])
                dqs[hp] = dqs[hp] + _dot(dzcat, kcat[t, hp])
            return tuple(keeps), tuple(grads), tuple(dqs)

        zero = jnp.zeros((BLOCK, LANES), F32)
        nfull = i // SB_TILES
        carry = lax.fori_loop(
            from_ref[i], nfull, lambda ci, c: walk([(ci * SB_TILES + t, False) for t in range(SB_TILES)], c),
            ((zero,) * (2 * P), (zero,) * (2 * P), (zero,) * P))
        ragged = [functools.partial(walk, [(i - m, False) for m in range(extra, 0, -1)] + [(i, True)])
                  for extra in range(SB_TILES)]
        carry = lax.switch(i % SB_TILES, ragged, carry)
        for hp in range(P):
            dq_ref[:, _lanes(hp)] = carry[2][hp] * (HEAD_DIM ** -0.5)

    blk = pl.BlockSpec((BLOCK, W), lambda hp, i, fr: (i, hp))
    col = pl.BlockSpec((S, W), lambda hp, i, fr: (0, hp))
    tri = pl.BlockSpec((2 * BLOCK, 2 * BLOCK), lambda hp, i, fr: (0, 0))
    return pl.pallas_call(
        body, name="sb_bwd",
        grid_spec=pltpu.PrefetchScalarGridSpec(
            num_scalar_prefetch=1, grid=(GROUP // W, S // BLOCK),
            in_specs=[blk, col, col, blk, blk, tri, tri], out_specs=[blk, col, col]),
        out_shape=[jax.ShapeDtypeStruct((S, GROUP), F32)] * 3,
    )(first_chunk, qs, ks, vs, do, ltot, tri_upto, tri_before)


def _out_proj_fwd(o_br, l_br, o_sb, x, w_dil, w_sbn, w_out_g, slots):
    S, D = x.shape
    tm = 512
    ns = len(slots)

    def body(o0, o1, o2, l0, l1, l2, os_ref, x_ref, wd_ref, ws_ref, w_ref, *rest):
        od_ref, s0, s1, s2, x1_ref = rest[ns:ns + 5]
        scr = rest[2 * ns + 5]
        finish = (_hosted_gathers(rest[:ns], rest[ns + 5:2 * ns + 5], *rest[2 * ns + 6:], pl.program_id(0), S // tm)
                  if ns else None)
        ls = [_from_strided(scr, l) for l in (l0, l1, l2)]
        os_ = [_from_strided(scr, o) for o in (o0, o1, o2)]
        m = jnp.maximum(jnp.maximum(ls[0], ls[1]), ls[2])
        es = [jnp.exp(l - m) for l in ls]
        den = es[0] + es[1] + es[2]
        od = (es[0] * os_[0] + es[1] * os_[1] + es[2] * os_[2]) / den
        od_ref[...] = od
        _to_strided(scr, m + jnp.log(den), list(zip(DILATIONS, (s0, s1, s2))))
        osb = os_ref[...]
        mixed = jnp.concatenate([(od * _rms(od) * wd_ref[...]).astype(BF16),
                                 (osb * _rms(osb) * ws_ref[...]).astype(BF16)], axis=1)
        x1_ref[...] = x_ref[...] + _dot(mixed, w_ref[...])
        if finish is not None:
            finish()

    row = lambda i: (i, 0)
    g = pl.BlockSpec((tm, GROUP), row)
    d = pl.BlockSpec((tm, D), row)
    planes = [_strided_spec(tm, r) for r in DILATIONS]
    h_in, h_out, h_shape, h_sems = _hosted_specs(slots)
    return pl.pallas_call(
        body, name="out_proj_fwd", grid=(S // tm,),
        in_specs=planes * 2 + [g, d, _full((1, GROUP)), _full((1, GROUP)), _full((2 * GROUP, D))] + h_in,
        out_specs=[g] + planes + [d] + h_out,
        out_shape=[jax.ShapeDtypeStruct((S, GROUP), F32)] + [_strided_shape(S, r, F32) for r in DILATIONS]
                  + [jax.ShapeDtypeStruct((S, D), F32)] + h_shape,
        input_output_aliases={11 + a: 5 + a for a in range(ns)},
        scratch_shapes=[_strided_scratch(tm)] + h_sems,
    )(*o_br, *l_br, o_sb, x, w_dil, w_sbn, w_out_g, *slots)


def _ffn_fwd(x1, target, ffn_w, wg_g, wu_g, wd_g):
    S, D = x1.shape
    F = wg_g.shape[1]
    tm = 512
    nt = S // tm

    def body(x_ref, t_ref, nw_ref, wg_ref, wu_ref, wd_ref, h_ref, g_ref, u_ref, dy_ref, loss_ref, h_s, acc):
        j = pl.program_id(1)

        @pl.when(j == 0)
        def _():
            xv = x_ref[...]
            h = (xv * _rms(xv) * nw_ref[...]).astype(BF16)
            h_s[...] = h
            h_ref[...] = h
            acc[...] = xv

        h = h_s[...]
        g = _dot_nt(h, wg_ref[0])
        u = _dot_nt(h, wu_ref[0])
        g_ref[0] = g.astype(BF16)
        u_ref[0] = u.astype(BF16)
        a = (g * _sigmoid(g) * u).astype(BF16)
        acc[...] += _dot(a, wd_ref[0])

        @pl.when(j == N_CHIPS - 1)
        def _():
            err = acc[...] - t_ref[...]
            dy_ref[...] = err * (1.0 / D)
            loss_ref[...] = jnp.full(loss_ref.shape, jnp.sum(err * err), F32)

    row = lambda t, j: (t, 0)
    shard = lambda t, j: (j, 0, 0)
    act = lambda t, j: (j, t, 0)
    return pl.pallas_call(
        body, name="ffn_fwd", grid=(nt, N_CHIPS),
        in_specs=[pl.BlockSpec((tm, D), row), pl.BlockSpec((tm, D), row), pl.BlockSpec((1, D), lambda t, j: (0, 0))]
                 + [pl.BlockSpec((1, F, D), shard)] * 3,
        out_specs=[pl.BlockSpec((tm, D), row), pl.BlockSpec((1, tm, F), act), pl.BlockSpec((1, tm, F), act),
                   pl.BlockSpec((tm, D), row), pl.BlockSpec((1, 8, LANES), lambda t, j: (t, 0, 0))],
        out_shape=[jax.ShapeDtypeStruct((S, D), BF16), jax.ShapeDtypeStruct((N_CHIPS, S, F), BF16),
                   jax.ShapeDtypeStruct((N_CHIPS, S, F), BF16), jax.ShapeDtypeStruct((S, D), F32),
                   jax.ShapeDtypeStruct((nt, 8, LANES), F32)],
        scratch_shapes=[pltpu.VMEM((tm, D), BF16), pltpu.VMEM((tm, D), F32)],
    )(x1, target, ffn_w, wg_g, wu_g, wd_g)


def _ffn_bwd(h2, dy, g, u, wg_g, wu_g, wd_g):
    S, D = dy.shape
    F = wg_g.shape[1]
    tm = 512

    def body(h_ref, dy_ref, g_ref, u_ref, wg_ref, wu_ref, wd_ref, dwg_ref, dwu_ref, dwd_ref, dh_ref, *narrow):
        t = pl.program_id(1)

        @pl.when(t == 0)
        def _():
            dwg_ref[...] = jnp.zeros_like(dwg_ref)
            dwu_ref[...] = jnp.zeros_like(dwu_ref)
            dwd_ref[...] = jnp.zeros_like(dwd_ref)

        h = h_ref[...]
        dyb = dy_ref[...].astype(BF16)
        gv = g_ref[0].astype(F32)
        uv = u_ref[0].astype(F32)
        da = _dot_nt(dyb, wd_ref[0])
        sg = _sigmoid(gv)
        silu = gv * sg
        du = (da * silu).astype(BF16)
        dg = (da * uv * (sg * (1.0 + gv * (1.0 - sg)))).astype(BF16)
        dwd_ref[0] += _dot_tn((silu * uv).astype(BF16), dyb)
        dwg_ref[0] += _dot_tn(dg, h)
        dwu_ref[0] += _dot_tn(du, h)
        dh_ref[0] = (_dot(dg, wg_ref[0]) + _dot(du, wu_ref[0])).astype(BF16)

        @pl.when(t == S // tm - 1)
        def _():
            for full, half in zip((dwg_ref, dwu_ref, dwd_ref), narrow):
                half[...] = full[...].astype(BF16)

    row = lambda j, t: (t, 0)
    shard = lambda j, t: (j, 0, 0)
    act = lambda j, t: (j, t, 0)
    return pl.pallas_call(
        body, name="ffn_bwd", grid=(N_CHIPS, S // tm),
        in_specs=[pl.BlockSpec((tm, D), row), pl.BlockSpec((tm, D), row),
                  pl.BlockSpec((1, tm, F), act), pl.BlockSpec((1, tm, F), act)] + [pl.BlockSpec((1, F, D), shard)] * 3,
        out_specs=[pl.BlockSpec((1, F, D), shard)] * 3 + [pl.BlockSpec((1, tm, D), act)]
                  + [pl.BlockSpec((1, F, D), shard)] * 3,
        out_shape=[jax.ShapeDtypeStruct((N_CHIPS, F, D), F32)] * 3 + [jax.ShapeDtypeStruct((N_CHIPS, S, D), BF16)]
                  + [jax.ShapeDtypeStruct((N_CHIPS, F, D), BF16)] * 3,
    )(h2, dy, g, u, wg_g, wu_g, wd_g)


def _out_proj_bwd(dh2p, dy, x1, ffn_w, w_out_g, o_dil, o_sb, w_dil, w_sbn, seg_ones, ffn_grads):
    S, D = dy.shape
    tm = 512
    ng = len(ffn_grads)

    def body(dh_ref, dy_ref, x1_ref, nw_ref, w_ref, od_ref, os_ref, wd_ref, ws_ref, g_ref, *rest):
        gin, rest = rest[:ng], rest[ng:]
        dx1_ref, dod0, dod1, dod2, dos_ref, dl0, dl1, dl2, dw_ref, dnw_ref, dwd_ref, dws_ref = rest[:12]
        gout, (scr, send, recv) = rest[12:12 + ng], rest[12 + ng:]
        i = pl.program_id(0)
        plans = [_pair_send_plan(gin[a], gout[a], send.at[a], recv.at[a]) for a in range(ng)]

        @pl.when(i == 0)
        def _():
            for start, _ in plans:
                start()

        @pl.when(i == 0)
        def _():
            for r_ in (dw_ref, dnw_ref, dwd_ref, dws_ref):
                r_[...] = jnp.zeros_like(r_)

        dh2 = _sum4(dh_ref)
        dxn, dwn = _rms_bwd(dh2, x1_ref[...], nw_ref[...])
        dnw_ref[...] += jnp.sum(dwn, axis=0, keepdims=True)
        dx1 = dy_ref[...] + dxn
        dx1_ref[...] = dx1
        dx1b = dx1.astype(BF16)
        dmix = _dot_nt(dx1b, w_ref[...])
        od = od_ref[...]
        osb = os_ref[...]
        mixed = jnp.concatenate([(od * _rms(od) * wd_ref[...]).astype(BF16),
                                 (osb * _rms(osb) * ws_ref[...]).astype(BF16)], axis=1)
        dw_ref[...] += _dot_tn(mixed, dx1b)
        do, dwo = _rms_bwd(dmix[:, :GROUP], od, wd_ref[...])
        dwd_ref[...] += jnp.sum(dwo, axis=0, keepdims=True)
        _to_strided(scr, do, list(zip(DILATIONS, (dod0, dod1, dod2))))
        _to_strided(scr, _segsum(do * od, g_ref[...]), list(zip(DILATIONS, (dl0, dl1, dl2))))
        do, dwo = _rms_bwd(dmix[:, GROUP:], osb, ws_ref[...])
        dws_ref[...] += jnp.sum(dwo, axis=0, keepdims=True)
        dos_ref[...] = do.astype(BF16)

        @pl.when(i == S // tm - 1)
        def _():
            for _, finish in plans:
                finish()

    row = lambda i: (i, 0)
    gsp = pl.BlockSpec((tm, GROUP), row)
    dsp = pl.BlockSpec((tm, D), row)
    planes = [_strided_spec(tm, r) for r in DILATIONS]
    halves = [jax.ShapeDtypeStruct((g.shape[0], g.shape[1] // 2, g.shape[2]), g.dtype) for g in ffn_grads]
    return pl.pallas_call(
        body, name="out_proj_bwd", grid=(S // tm,),
        in_specs=[pl.BlockSpec((N_CHIPS, tm, D), lambda i: (0, i, 0)), dsp, dsp, _full((1, D)), _full((2 * GROUP, D)),
                  gsp, gsp, _full((1, GROUP)), _full((1, GROUP)), _full((GROUP, GROUP // 2))] + [HBM] * ng,
        out_specs=[dsp] + planes + [gsp] + planes
                  + [_full((2 * GROUP, D)), _full((1, D)), _full((1, GROUP)), _full((1, GROUP))] + [HBM] * ng,
        out_shape=[jax.ShapeDtypeStruct((S, D), F32)] + [_strided_shape(S, r, BF16) for r in DILATIONS]
                  + [jax.ShapeDtypeStruct((S, GROUP), BF16)] + [_strided_shape(S, r, F32) for r in DILATIONS]
                  + [jax.ShapeDtypeStruct((2 * GROUP, D), F32),
                     jax.ShapeDtypeStruct((1, D), F32), jax.ShapeDtypeStruct((1, GROUP), F32),
                     jax.ShapeDtypeStruct((1, GROUP), F32)] + halves,
        scratch_shapes=[_strided_scratch(tm), pltpu.SemaphoreType.DMA((ng,)), pltpu.SemaphoreType.DMA((ng,))],
    )(dh2p, dy, x1, ffn_w, w_out_g, o_dil, o_sb, w_dil, w_sbn, seg_ones, *ffn_grads)


def _attn_in_bwd(dq_br, dk_br, dv_br, dqs, dks, dvs, qa, ka, qw, kw, cos_t, sin_t, seg_ones, h, w_in_g, x, dx1, attn_w):
    S, D = x.shape
    wc = w_in_g.shape[2]
    tm = 256

    def body(q0, q1, q2, k0, k1, k2, v0, v1, v2, dqs_ref, dks_ref, dvs_ref, qa_ref, ka_ref, qw_ref, kw_ref,
             cos_ref, sin_ref, g_ref, h_ref, w_ref, x_ref, dx1_ref, aw_ref,
             gx_ref, dw_ref, daw_ref, dqw_ref, dkw_ref, accq, acck, scr):
        i = pl.program_id(0)

        @pl.when(i == 0)
        def _():
            accq[...] = jnp.zeros_like(accq)
            acck[...] = jnp.zeros_like(acck)
            dw_ref[...] = jnp.zeros_like(dw_ref)
            daw_ref[...] = jnp.zeros_like(daw_ref)

        def branches(refs):
            return (_from_strided(scr, refs[0]) + _from_strided(scr, refs[1])) + _from_strided(scr, refs[2])

        g = g_ref[...]
        cos = _tile4(cos_ref[...])
        sin = _tile4(sin_ref[...])
        pieces = []
        for refs, pre_ref, w_r, acc in (((q0, q1, q2), qa_ref, qw_ref, accq), ((k0, k1, k2), ka_ref, kw_ref, acck)):
            dh = branches(refs)
            dn = dh * cos + _rot_half(dh * sin)
            pre = pre_ref[...]
            rstd = lax.rsqrt(_segsum(pre * pre, g) * (1.0 / HEAD_DIM) + EPS)
            xh = pre * rstd
            acc[...] += jnp.sum(dn * xh, axis=0, keepdims=True)
            dxh = dn * w_r[...]
            pieces.append((rstd * (dxh - xh * (_segsum(dxh * xh, g) * (1.0 / HEAD_DIM)))).astype(BF16))
        pieces += [branches((v0, v1, v2)).astype(BF16), dqs_ref[...].astype(BF16), dks_ref[...].astype(BF16),
                   dvs_ref[...].astype(BF16)]
        dproj = jnp.concatenate(pieces, axis=1)
        hv = h_ref[...]
        dh = jnp.zeros((tm, D), F32)
        for j in range(N_CHIPS):
            dp = dproj[:, j * wc:(j + 1) * wc]
            dw_ref[j] += _dot_tn(hv, dp)
            dh = dh + _dot_nt(dp, w_ref[j])
        dx, dw = _rms_bwd(dh, x_ref[...], aw_ref[...])
        daw_ref[...] += jnp.sum(dw, axis=0, keepdims=True)
        gx_ref[...] = dx1_ref[...] + dx

        @pl.when(i == S // tm - 1)
        def _():
            for acc, o_ref in ((accq, dqw_ref), (acck, dkw_ref)):
                a = acc[...]
                pair = (a[:, 0:LANES] + a[:, LANES:2 * LANES]) + (a[:, 2 * LANES:3 * LANES] + a[:, 3 * LANES:4 * LANES])
                o_ref[...] = pair + pltpu.roll(pair, HEAD_DIM, 1)

    row = lambda i: (i, 0)
    gsp = pl.BlockSpec((tm, GROUP), row)
    dsp = pl.BlockSpec((tm, D), row)
    tab = pl.BlockSpec((tm, LANES), row)
    planes = [_strided_spec(tm, r) for r in DILATIONS]
    return pl.pallas_call(
        body, name="attn_in_bwd", grid=(S // tm,),
        in_specs=planes * 3 + [gsp] * 5 + [_full((1, GROUP)), _full((1, GROUP)), tab, tab, _full((GROUP, GROUP // 2)),
                                          dsp, _full((N_CHIPS, D, wc)), dsp, dsp, _full((1, D))],
        out_specs=[dsp, _full((N_CHIPS, D, wc)), _full((1, D)), _full((1, LANES)), _full((1, LANES))],
        out_shape=[jax.ShapeDtypeStruct((S, D), F32), jax.ShapeDtypeStruct((N_CHIPS, D, wc), F32),
                   jax.ShapeDtypeStruct((1, D), F32), jax.ShapeDtypeStruct((1, LANES), F32),
                   jax.ShapeDtypeStruct((1, LANES), F32)],
        scratch_shapes=[pltpu.VMEM((1, GROUP), F32), pltpu.VMEM((1, GROUP), F32), _strided_scratch(tm)],
    )(*dq_br, *dk_br, *dv_br, dqs, dks, dvs, qa, ka, qw, kw, cos_t, sin_t, seg_ones, h, w_in_g, x, dx1, attn_w)


def _constants(S):
    pos = jnp.arange(S, dtype=F32)
    inv_freq = ROPE_THETA ** (-jnp.arange(0, HEAD_DIM, 2, dtype=F32) / HEAD_DIM)
    ang_a = pos[::BLOCK, None] * inv_freq[None, :]
    ang_b = pos[:BLOCK, None] * inv_freq[None, :]
    ca, sa, cb, sb = jnp.cos(ang_a)[:, None], jnp.sin(ang_a)[:, None], jnp.cos(ang_b)[None], jnp.sin(ang_b)[None]
    cos = (ca * cb - sa * sb).reshape(S, HEAD_DIM // 2)
    sin = (sa * cb + ca * sb).reshape(S, HEAD_DIM // 2)
    cos_t = jnp.concatenate([cos, cos] * 2, axis=1)
    sin_t = jnp.concatenate([-sin, sin] * 2, axis=1)
    idx = jnp.arange(GROUP // 2)
    seg_ones = (idx[:, None] // HEAD_DIM == idx[None, :] // HEAD_DIM).astype(BF16)
    seg_ones = jnp.concatenate([seg_ones, seg_ones], axis=0)
    r = jnp.arange(BLOCK)
    ones = jnp.ones((BLOCK, BLOCK), BF16)
    tris = [jnp.concatenate([jnp.concatenate([m.astype(BF16), ones], axis=1)] * 2, axis=0) for m in
            (r[:, None] > r[None, :],
             r[:, None] <= r[None, :],
             r[:, None] < r[None, :])]
    return cos_t, sin_t, seg_ones, tris


FFN_NAMES = ("w_gate", "w_up", "w_down")


def _device_step(x, target, attn_w, qn_w, kn_w, dil_w, sbn_w, ffn_w, w_in_g, w_out_slots, ffn_slots, core, chip):
    S = x.shape[0]
    cos_t, sin_t, seg_ones, (tri_later, tri_upto, tri_before) = _constants(S)
    reps = GROUP // HEAD_DIM
    qw = jnp.tile(qn_w, (1, reps))
    kw = jnp.tile(kn_w, (1, reps))

    nd = len(DILATIONS)
    h, qa, ka, *rest, wg_g, w_out_g = _in_proj_fwd(x, attn_w, w_in_g, qw, kw, cos_t, sin_t, seg_ones,
                                                   [ffn_slots[0], w_out_slots])
    w_out_g = w_out_g.reshape(-1, x.shape[1])
    qh, kh, va, (qs, ks, vs) = rest[:nd], rest[nd:2 * nd], rest[2 * nd:3 * nd], rest[3 * nd:]
    branches = [_dil_fwd(qh[b], kh[b], va[b], []) for b in range(nd)]
    o_sb, ltot, walked, wu_g, wd_g = _sb_fwd(qs, ks, vs, tri_later, ffn_slots[1:])
    o_dil, *lse, x1 = _out_proj_fwd([b[0] for b in branches], [b[1] for b in branches], o_sb, x, dil_w, sbn_w,
                                    w_out_g, [])
    h2, g, u, dy, loss_parts = _ffn_fwd(x1, target, ffn_w, wg_g, wu_g, wd_g)

    *ffn_grads, dh2p, n0, n1, n2 = _ffn_bwd(h2, dy, g, u, wg_g, wu_g, wd_g)
    dx1, *mid, dw_out, dffn_w, ddil_w, dsbn_w, p0, p1, p2 = _out_proj_bwd(
        dh2p, dy, x1, ffn_w, w_out_g, o_dil, o_sb, dil_w, sbn_w, seg_ones, [n0, n1, n2])
    do_dil, do_sb, delta = mid[:nd], mid[nd], mid[nd + 1:]
    parts = [_pair_sum(gr, fr, core, n) for gr, fr, n in zip(ffn_grads, (p0, p1, p2), FFN_NAMES)]
    dqs, dks, dvs = _sb_bwd(walked[:, 0, 0], qs, ks, vs, do_sb, ltot, tri_upto, tri_before)
    dw_out = dw_out.reshape(N_CHIPS, -1, x.shape[1])
    dbr = [None] * nd
    dbr[0] = _dil_bwd(qh[0], kh[0], va[0], do_dil[0], lse[0], delta[0], [parts[0]], [dw_out])
    out_part = _pair_sum(dw_out, dbr[0][4], core, "w_out")
    dbr[1] = _dil_bwd(qh[1], kh[1], va[1], do_dil[1], lse[1], delta[1], [parts[1], out_part], [])
    dbr[2] = _dil_bwd(qh[2], kh[2], va[2], do_dil[2], lse[2], delta[2], [parts[2]], [])
    ffn_halves = [_chip_sum(dbr[b][3], parts[b], chip, FFN_NAMES[b]) for b in range(nd)]
    w_out_half = _chip_sum(dbr[1][4], out_part, chip, "w_out")
    grad_x, dw_in, dattn_w, dqw, dkw = _attn_in_bwd(
        [b[0] for b in dbr], [b[1] for b in dbr], [b[2] for b in dbr], dqs, dks, dvs,
        qa, ka, qw, kw, cos_t, sin_t, seg_ones, h, w_in_g, x, dx1, attn_w)
    small = dict(attn=dattn_w, q=dqw[:, :HEAD_DIM], k=dkw[:, :HEAD_DIM], dil=ddil_w, sb=dsbn_w, ffn=dffn_w)
    return loss_parts, grad_x, small, dw_in, w_out_half, ffn_halves


HBM = pl.BlockSpec(memory_space=pltpu.HBM)
VMEM = pl.BlockSpec(memory_space=pltpu.VMEM)
CHIP_FLIPS = ((1, 0), (0, 1), (1, 1))


def _place():
    return lax.axis_index("x"), lax.axis_index("y"), lax.axis_index("c")


def _flip(v, d):
    return 1 - v if d else v


def _half_rows(c, n):
    return pl.ds(pl.multiple_of(c * (n // 2), 16), n // 2)


def _gather_plan(slot_in, slot_out, send, recv):
    x, y, c = _place()
    p = 2 * x + y
    chips = [(_flip(x, dx), _flip(y, dy)) for dx, dy in CHIP_FLIPS]
    mine, other = _half_rows(c, slot_in.shape[1]), _half_rows(1 - c, slot_in.shape[1])

    def copy(k, src, dst, to):
        return pltpu.make_async_remote_copy(src_ref=src, dst_ref=dst, send_sem=send.at[k], recv_sem=recv.at[k],
                                            device_id=to, device_id_type=MESH)

    def first(k):
        return copy(k, slot_in.at[p, mine], slot_out.at[p, mine], (*chips[k], c))

    def passed(k, rows):
        land = slot_out.at[2 * chips[k][0] + chips[k][1], rows]
        return copy(3 + k, land, land, (x, y, 1 - c))

    def start():
        for k in range(3):
            first(k).start()

    def forward():
        for k in range(3):
            land = slot_out.at[2 * chips[k][0] + chips[k][1], mine]
            copy(k, land, land, (*chips[k], c)).wait_recv()
            passed(k, mine).start()

    def finish():
        for k in range(3):
            passed(k, other).wait_recv()
        for k in range(3):
            first(k).wait_send()
            passed(k, mine).wait_send()

    return start, forward, finish


def _chip_send_plan(part_in, recv_out, send, recv):
    x, y, c = _place()
    p = 2 * x + y
    chips = [(_flip(x, dx), _flip(y, dy)) for dx, dy in CHIP_FLIPS]

    def copy(k):
        q = 2 * chips[k][0] + chips[k][1]
        return pltpu.make_async_remote_copy(src_ref=part_in.at[q], dst_ref=recv_out.at[p], send_sem=send.at[k],
                                            recv_sem=recv.at[k], device_id=(*chips[k], c), device_id_type=MESH)

    def start():
        for k in range(3):
            copy(k).start()

    def finish():
        for k in range(3):
            land = recv_out.at[2 * chips[k][0] + chips[k][1]]
            pltpu.make_async_remote_copy(src_ref=land, dst_ref=land, send_sem=send.at[k], recv_sem=recv.at[k],
                                         device_id=(*chips[k], c), device_id_type=MESH).wait_recv()
        for k in range(3):
            copy(k).wait_send()

    return start, finish


def _pair_send_plan(grad_in, recv_out, send, recv):
    x, y, c = _place()

    def copy():
        theirs = _half_rows(1 - c, grad_in.shape[1])
        return pltpu.make_async_remote_copy(src_ref=grad_in.at[:, theirs, :], dst_ref=recv_out, send_sem=send,
                                            recv_sem=recv, device_id=(x, y, 1 - c), device_id_type=MESH)

    return (lambda: copy().start()), (lambda: copy().wait())


def _own_slots(shard):
    here = 2 * lax.axis_index("x") + lax.axis_index("y")
    return lax.dynamic_update_slice(lax.empty((N_CHIPS,) + shard.shape, shard.dtype), shard[None], (here, 0, 0))


def _gather_weights(shards):
    n = len(shards)

    def body(*refs):
        ins, outs = refs[:n], refs[n:2 * n]
        send, recv = refs[2 * n:]
        plans = [_gather_plan(ins[a], outs[a], send.at[pl.ds(6 * a, 6)], recv.at[pl.ds(6 * a, 6)]) for a in range(n)]
        for stage in range(3):
            for plan in plans:
                plan[stage]()

    slots = [_own_slots(s) for s in shards]
    return pl.pallas_call(
        body, name="gather_weights", in_specs=[HBM] * n, out_specs=[HBM] * n,
        out_shape=[jax.ShapeDtypeStruct(s.shape, s.dtype) for s in slots],
        input_output_aliases={a: a for a in range(n)},
        scratch_shapes=[pltpu.SemaphoreType.DMA((6 * n,)), pltpu.SemaphoreType.DMA((6 * n,))],
    )(*slots)


def _pair_exchange(grads, small):
    n = len(grads)

    def body(*refs):
        gin, sm = refs[:n], refs[n]
        gout, sm_all = refs[n + 1:2 * n + 1], refs[2 * n + 1]
        send, recv = refs[2 * n + 2:]
        x, y, c = _place()
        me = 4 * x + 2 * y + c
        big = [_pair_send_plan(gin[a], gout[a], send.at[a], recv.at[a]) for a in range(n)]
        for start, _ in big:
            start()
        sm_all[pl.ds(me, 1)] = sm[...][None]
        tiny = []
        for k in range(1, N_DEV):
            px, py, pc = _flip(x, k & 4), _flip(y, k & 2), _flip(c, k & 1)
            tiny.append((pltpu.make_async_remote_copy(
                src_ref=sm, dst_ref=sm_all.at[me], send_sem=send.at[n + k - 1], recv_sem=recv.at[n + k - 1],
                device_id=(px, py, pc), device_id_type=MESH), 4 * px + 2 * py + pc))
            tiny[-1][0].start()
        for k, (cp, peer) in enumerate(tiny):
            pltpu.make_async_remote_copy(src_ref=sm, dst_ref=sm_all.at[peer], send_sem=send.at[n + k],
                                         recv_sem=recv.at[n + k], device_id=(x, y, c),
                                         device_id_type=MESH).wait_recv()
            cp.wait_send()
        for _, finish in big:
            finish()

    halves = [jax.ShapeDtypeStruct((g.shape[0], g.shape[1] // 2, g.shape[2]), g.dtype) for g in grads]
    return pl.pallas_call(
        body, name="pair_exchange", in_specs=[HBM] * n + [VMEM], out_specs=[HBM] * n + [VMEM],
        out_shape=halves + [jax.ShapeDtypeStruct((N_DEV,) + small.shape, small.dtype)],
        scratch_shapes=[pltpu.SemaphoreType.DMA((n + N_DEV - 1,)), pltpu.SemaphoreType.DMA((n + N_DEV - 1,))],
    )(*grads, small)


def _chip_exchange(parts):
    n = len(parts)

    def body(*refs):
        pin, pout = refs[:n], refs[n:2 * n]
        send, recv = refs[2 * n:]
        plans = [_chip_send_plan(pin[a], pout[a], send.at[pl.ds(3 * a, 3)], recv.at[pl.ds(3 * a, 3)]) for a in range(n)]
        for stage in range(2):
            for plan in plans:
                plan[stage]()

    return pl.pallas_call(
        body, name="chip_exchange", in_specs=[HBM] * n, out_specs=[HBM] * n,
        out_shape=[jax.ShapeDtypeStruct(s.shape, s.dtype) for s in parts],
        scratch_shapes=[pltpu.SemaphoreType.DMA((3 * n,)), pltpu.SemaphoreType.DMA((3 * n,))],
    )(*parts)


def _pair_swap_plan(hin, hout, send, recv):
    x, y, c = _place()

    def copies():
        return [pltpu.make_async_remote_copy(src_ref=hin[a], dst_ref=hout[a], send_sem=send.at[a], recv_sem=recv.at[a],
                                             device_id=(x, y, 1 - c), device_id_type=MESH) for a in range(len(hin))]

    def start():
        for cp in copies():
            cp.start()

    def finish():
        for cp in copies():
            cp.wait()

    return start, finish


def _pair_swap(halves):
    n = len(halves)

    def body(*refs):
        start, finish = _pair_swap_plan(refs[:n], refs[n:2 * n], *refs[2 * n:])
        start()
        finish()

    return pl.pallas_call(
        body, name="pair_swap", in_specs=[HBM] * n, out_specs=[HBM] * n,
        out_shape=[jax.ShapeDtypeStruct(s.shape, s.dtype) for s in halves],
        scratch_shapes=[pltpu.SemaphoreType.DMA((n,)), pltpu.SemaphoreType.DMA((n,))],
    )(*halves)


def _pair_sum(grad, recv, c, tag):
    _, R, C = grad.shape
    hr = R // 2

    def body(c_ref, a_ref, b_ref, o_ref):
        o_ref[...] = (a_ref[...] + b_ref[...]).astype(BF16)

    return pl.pallas_call(
        body, name="pair_sum_" + tag,
        grid_spec=pltpu.PrefetchScalarGridSpec(
            num_scalar_prefetch=1, grid=(N_CHIPS,),
            in_specs=[pl.BlockSpec((1, hr, C), lambda s, cr: (s, cr[0], 0)),
                      pl.BlockSpec((1, hr, C), lambda s, cr: (s, 0, 0))],
            out_specs=pl.BlockSpec((1, hr, C), lambda s, cr: (s, 0, 0))),
        out_shape=jax.ShapeDtypeStruct((N_CHIPS, hr, C), BF16),
    )(c, grad, recv)


def _chip_sum(received, own, chip, tag):
    _, rows, C = received.shape
    tr = rows // 2

    def body(chip_ref, own_ref, r1_ref, r2_ref, r3_ref, o_ref):
        p = [r[0].astype(F32) for r in (own_ref, r1_ref, r2_ref, r3_ref)]
        o_ref[...] = (p[0] + p[1]) + (p[2] + p[3])

    def slot(k):
        return pl.BlockSpec((1, tr, C), lambda i, cr: (jnp.bitwise_xor(cr[0], k), i, 0))

    return pl.pallas_call(
        body, name="chip_sum_" + tag,
        grid_spec=pltpu.PrefetchScalarGridSpec(
            num_scalar_prefetch=1, grid=(rows // tr,), in_specs=[slot(0), slot(1), slot(2), slot(3)],
            out_specs=pl.BlockSpec((tr, C), lambda i, cr: (i, 0))),
        out_shape=jax.ShapeDtypeStruct((rows, C), F32),
    )(chip, own, received, received, received)


def _adamw_math(w, g, m, v):
    m = ADAM_B1 * m + (1.0 - ADAM_B1) * g
    v = ADAM_B2 * v + (1.0 - ADAM_B2) * (g * g)
    m_hat = m / (1.0 - ADAM_B1 ** ADAM_STEP)
    v_hat = v / (1.0 - ADAM_B2 ** ADAM_STEP)
    delta = -ADAM_LR * (m_hat / (jnp.sqrt(v_hat) + ADAM_EPS) + ADAM_WD * w)
    return delta, m, v


def _adamw(w, g_mine, g_other, m, v, c, tag):
    R, C = w.shape

    def body(c_ref, w_ref, gm_ref, go_ref, m_ref, v_ref, g_ref, d_ref, nm_ref, nv_ref):
        g = jnp.where(pl.program_id(0) == c_ref[0], gm_ref[...], go_ref[...])
        g_ref[...] = g
        d_ref[...], nm_ref[...], nv_ref[...] = _adamw_math(w_ref[...], g, m_ref[...], v_ref[...])

    blk = pl.BlockSpec((R // 2, C), lambda h, cr: (h, 0))
    half = pl.BlockSpec((R // 2, C), lambda h, cr: (0, 0))
    return pl.pallas_call(
        body, name="adamw_" + tag,
        grid_spec=pltpu.PrefetchScalarGridSpec(
            num_scalar_prefetch=1, grid=(2,), in_specs=[blk, half, half, blk, blk], out_specs=[blk] * 4),
        out_shape=[jax.ShapeDtypeStruct((R, C), F32)] * 4,
    )(c, w, g_mine, g_other, m, v)


def _small_update(all_small, w, m, v):
    def body(a_ref, w_ref, m_ref, v_ref, g_ref, d_ref, nm_ref, nv_ref):
        g = ((a_ref[0] + a_ref[1]) + (a_ref[2] + a_ref[3])) + ((a_ref[4] + a_ref[5]) + (a_ref[6] + a_ref[7]))
        g_ref[...] = g
        d_ref[...], nm_ref[...], nv_ref[...] = _adamw_math(w_ref[...], g, m_ref[...], v_ref[...])

    return pl.pallas_call(
        body, name="small_update", out_shape=[jax.ShapeDtypeStruct(w.shape, F32)] * 4,
    )(all_small, w, m, v)


SMALL_ROWS = (("attn", 0, 0), ("ffn", 1, 0), ("dil", 2, 0), ("sb", 2, GROUP), ("q", 3, 0), ("k", 3, HEAD_DIM),
              ("loss", 4, 0))


def _pack_small(vals, D):
    rows = [jnp.zeros((1, D), F32) for _ in range(8)]
    for name, r, off in SMALL_ROWS:
        if name in vals:
            rows[r] = lax.dynamic_update_slice(rows[r], vals[name].astype(F32), (0, off))
    return jnp.concatenate(rows, axis=0)


def _unpack_small(packed, vals):
    return {name: packed[r:r + 1, off:off + vals[name].shape[1]] for name, r, off in SMALL_ROWS if name in vals}


def kernel(x, attn_norm_w, w_in, q_norm_w, k_norm_w, dil_out_norm_w, sb_out_norm_w, w_out, ffn_norm_w, w_gate, w_up, w_down, loss_target, m_attn_norm_w, m_w_in, m_q_norm_w, m_k_norm_w, m_dil_out_norm_w, m_sb_out_norm_w, m_w_out, m_ffn_norm_w, m_w_gate, m_w_up, m_w_down, v_attn_norm_w, v_w_in, v_q_norm_w, v_k_norm_w, v_dil_out_norm_w, v_sb_out_norm_w, v_w_out, v_ffn_norm_w, v_w_gate, v_w_up, v_w_down):
    D = x.shape[-1]
    big_names = ("w_in", "w_out", "w_gate", "w_up", "w_down")
    flipped = ("w_gate", "w_up")
    tr = lambda a: jnp.swapaxes(a[0], 0, 1)
    big_w = dict(w_in=w_in[0], w_out=w_out[0], w_gate=tr(w_gate), w_up=tr(w_up), w_down=w_down[0])
    big_m = dict(w_in=m_w_in[0], w_out=m_w_out[0], w_gate=tr(m_w_gate), w_up=tr(m_w_up), w_down=m_w_down[0])
    big_v = dict(w_in=v_w_in[0], w_out=v_w_out[0], w_gate=tr(v_w_gate), w_up=tr(v_w_up), w_down=v_w_down[0])
    small_w = dict(attn=attn_norm_w, q=q_norm_w, k=k_norm_w, dil=dil_out_norm_w, sb=sb_out_norm_w, ffn=ffn_norm_w)
    small_m = dict(attn=m_attn_norm_w, q=m_q_norm_w, k=m_k_norm_w, dil=m_dil_out_norm_w, sb=m_sb_out_norm_w,
                   ffn=m_ffn_norm_w)
    small_v = dict(attn=v_attn_norm_w, q=v_q_norm_w, k=v_k_norm_w, dil=v_dil_out_norm_w, sb=v_sb_out_norm_w,
                   ffn=v_ffn_norm_w)

    c = lax.axis_index("c").astype(jnp.int32).reshape(1)
    chip = (2 * lax.axis_index("x") + lax.axis_index("y")).astype(jnp.int32).reshape(1)
    (w_in_g,) = _gather_weights([big_w["w_in"].astype(BF16)])
    w_out_slots = _own_slots(big_w["w_out"].astype(BF16))
    ffn_slots = [_own_slots(big_w[n].astype(BF16)) for n in FFN_NAMES]

    loss_parts, grad_x, small_g, dw_in, w_out_half, ffn_halves = _device_step(
        x[0], loss_target[0], attn_norm_w, q_norm_w, k_norm_w, dil_out_norm_w, sb_out_norm_w, ffn_norm_w,
        w_in_g, w_out_slots, ffn_slots, c, chip)
    small_g["loss"] = (jnp.sum(loss_parts[:, 0, 0]) * (0.5 / D)).reshape(1, 1)

    from_pair, all_small = _pair_exchange([dw_in], _pack_small(small_g, D))
    in_part = _pair_sum(dw_in, from_pair, c, "w_in")
    (from_chips,) = _chip_exchange([in_part])
    halves = [_chip_sum(from_chips, in_part, chip, "w_in"), w_out_half] + ffn_halves
    others = _pair_swap(halves)
    big_out = {n: _adamw(big_w[n], mine, other, big_m[n], big_v[n], c, n)
               for n, mine, other in zip(big_names, halves, others)}
    sg, sd, sm, sv = _small_update(all_small, _pack_small(small_w, D), _pack_small(small_m, D),
                                   _pack_small(small_v, D))
    small_out = [_unpack_small(t, small_w) for t in (sg, sd, sm, sv)]

    order = (("attn", None), (None, "w_in"), ("q", None), ("k", None), ("dil", None), ("sb", None),
             (None, "w_out"), ("ffn", None), (None, "w_gate"), (None, "w_up"), (None, "w_down"))
    outs = [sg[4, 0], grad_x[None]]
    for kind in range(4):
        for s_name, b_name in order:
            if s_name is not None:
                outs.append(small_out[kind][s_name])
            else:
                res = big_out[b_name][kind]
                outs.append((jnp.swapaxes(res, 0, 1) if b_name in flipped else res)[None])
    return tuple(outs)
```

```python
import functools

import jax
import jax.numpy as jnp
from jax import lax
from jax.experimental import pallas as pl
from jax.experimental.pallas import tpu as pltpu

F32 = jnp.float32
BF16 = jnp.bfloat16
MESH = pl.DeviceIdType.MESH

HEAD_DIM = 64
GROUP = 512
BLOCK = 128
LANES = 128
N_CHIPS = 4
N_DEV = 8
EPS = 1e-6
ROPE_THETA = 10000.0
DILATIONS = (1, 4, 16)
NEG = -1e30

ADAM_LR = 0.001
ADAM_B1 = 0.9
ADAM_B2 = 0.999
ADAM_EPS = 1e-08
ADAM_WD = 0.01
ADAM_STEP = 10


def _dot(a, b):
    return jnp.dot(a, b, preferred_element_type=F32)


def _dot_nt(a, b):
    return lax.dot_general(a, b, (((1,), (1,)), ((), ())), preferred_element_type=F32)


def _dot_tn(a, b):
    return lax.dot_general(a, b, (((0,), (0,)), ((), ())), preferred_element_type=F32)


def _split(v):
    hi = lax.bitcast_convert_type(lax.bitcast_convert_type(v, jnp.uint32) & jnp.uint32(0xFFFF0000), F32)
    return hi.astype(BF16), (v - hi).astype(BF16)


def _segsum(v, g):
    hi, lo = _split(v)
    w = g.shape[1]
    return jnp.concatenate([_dot(jnp.concatenate([hi[:, c:c + w], lo[:, c:c + w]], axis=1), g)
                            for c in range(0, v.shape[1], w)], axis=1)


def _rot_half(x):
    outs = []
    for c in range(x.shape[1] // LANES):
        xc = x[:, c * LANES:(c + 1) * LANES]
        lane = lax.broadcasted_iota(jnp.int32, xc.shape, 1)
        first = (lane % HEAD_DIM) < (HEAD_DIM // 2)
        outs.append(jnp.where(first, pltpu.roll(xc, LANES - 32, 1), pltpu.roll(xc, 32, 1)))
    return outs[0] if len(outs) == 1 else jnp.concatenate(outs, axis=1)


def _rms(x):
    return lax.rsqrt(jnp.mean(x * x, axis=-1, keepdims=True) + EPS)


def _rms_bwd(dy, x, w):
    rstd = _rms(x)
    xh = x * rstd
    dxh = dy * w
    dx = rstd * (dxh - xh * jnp.mean(dxh * xh, axis=-1, keepdims=True))
    return dx, dy * xh


def _sigmoid(x):
    return 1.0 / (1.0 + jnp.exp(-x))


def _sum4(ref):
    p = [ref[j].astype(F32) for j in range(N_CHIPS)]
    return (p[0] + p[1]) + (p[2] + p[3])


def _full(shape):
    n = len(shape)
    return pl.BlockSpec(shape, lambda *_: (0,) * n)


def _strided_spec(tm, r):
    return pl.BlockSpec((r, tm // r, GROUP), lambda i: (0, i, 0))


def _strided_shape(S, r, dtype):
    return jax.ShapeDtypeStruct((r, S // r, GROUP), dtype)


def _to_strided(scr, val, outs):
    chunks = range(GROUP // LANES)
    for k in chunks:
        scr[k] = val[:, _lanes(k)]
    for r, o_ref in outs:
        if r == 1:
            o_ref[0] = val.astype(o_ref.dtype)
            continue
        n = val.shape[0] // r
        for c in range(r):
            rows = pl.ds(c, n, stride=r)
            o_ref[c] = jnp.concatenate([scr.at[k][rows, :] for k in chunks], axis=1).astype(o_ref.dtype)


def _from_strided(scr, ref):
    r, n, _ = ref.shape
    if r == 1:
        return ref[0].astype(F32)
    chunks = range(GROUP // LANES)
    for c in range(r):
        plane = ref[c].astype(F32)
        for k in chunks:
            scr.at[k][pl.ds(c, n, stride=r), :] = plane[:, _lanes(k)]
    return jnp.concatenate([scr[k] for k in chunks], axis=1)


def _strided_scratch(tm):
    return pltpu.VMEM((GROUP // LANES, tm, LANES), F32)


def _tile4(t):
    return jnp.concatenate([t] * (GROUP // LANES), axis=1)


def _in_proj_fwd(x, attn_w, w_in_g, qw, kw, cos_t, sin_t, seg_ones, slots):
    S, D = x.shape
    tm = 512
    wcols = w_in_g.shape[2]
    nd = len(DILATIONS)
    ns = len(slots)

    def body(x_ref, aw_ref, w_ref, qw_ref, kw_ref, cos_ref, sin_ref, g_ref, *rest):
        slot_in, (h_ref, qa_ref, ka_ref), rest = rest[:ns], rest[ns:ns + 3], rest[ns + 3:]
        q_refs, k_refs, v_refs = rest[:nd], rest[nd:2 * nd], rest[2 * nd:3 * nd]
        qs_ref, ks_ref, vs_ref = rest[3 * nd:3 * nd + 3]
        slot_out, scr, sems = rest[3 * nd + 3:3 * nd + 3 + ns], rest[3 * nd + 3 + ns], rest[3 * nd + 4 + ns:]
        finish = _hosted_gathers(slot_in, slot_out, *sems, pl.program_id(0), S // tm) if ns else None
        xv = x_ref[...]
        h = (xv * _rms(xv) * aw_ref[...]).astype(BF16)
        h_ref[...] = h
        proj = jnp.concatenate([_dot(h, w_ref[j]) for j in range(N_CHIPS)], axis=1)
        qa = proj[:, 0 * GROUP:1 * GROUP]
        ka = proj[:, 1 * GROUP:2 * GROUP]
        qa_ref[...] = qa
        ka_ref[...] = ka
        _to_strided(scr, proj[:, 2 * GROUP:3 * GROUP], list(zip(DILATIONS, v_refs)))
        qs_ref[...] = proj[:, 3 * GROUP:4 * GROUP].astype(BF16)
        ks_ref[...] = proj[:, 4 * GROUP:5 * GROUP].astype(BF16)
        vs_ref[...] = proj[:, 5 * GROUP:6 * GROUP].astype(BF16)
        g = g_ref[...]
        cos = _tile4(cos_ref[...])
        sin = _tile4(sin_ref[...])
        for t, w_r, o_rs in ((qa, qw_ref, q_refs), (ka, kw_ref, k_refs)):
            rstd = lax.rsqrt(_segsum(t * t, g) * (1.0 / HEAD_DIM) + EPS)
            tn = t * rstd * w_r[...]
            _to_strided(scr, tn * cos + _rot_half(tn) * sin, list(zip(DILATIONS, o_rs)))
        if finish is not None:
            finish()

    row = lambda i: (i, 0)
    tile = lambda n, dt: jax.ShapeDtypeStruct((S, n), dt)
    planes = [_strided_spec(tm, r) for r in DILATIONS]
    h_in, h_out, h_shape, h_sems = _hosted_specs(slots)
    n_out = 6 + 3 * nd
    return pl.pallas_call(
        body, name="in_proj_fwd", grid=(S // tm,),
        in_specs=[pl.BlockSpec((tm, D), row), _full((1, D)), _full((N_CHIPS, D, wcols)),
                  _full((1, GROUP)), _full((1, GROUP)),
                  pl.BlockSpec((tm, LANES), row), pl.BlockSpec((tm, LANES), row),
                  _full((GROUP, GROUP // 2))] + h_in,
        out_specs=[pl.BlockSpec((tm, D), row)] + [pl.BlockSpec((tm, GROUP), row)] * 2 + planes * 3
                  + [pl.BlockSpec((tm, GROUP), row)] * 3 + h_out,
        out_shape=[tile(D, BF16), tile(GROUP, F32), tile(GROUP, F32)]
                  + [_strided_shape(S, r, BF16) for r in DILATIONS] * 3 + [tile(GROUP, BF16)] * 3 + h_shape,
        input_output_aliases={8 + a: n_out + a for a in range(ns)},
        scratch_shapes=[_strided_scratch(tm)] + h_sems,
    )(x, attn_w, w_in_g, qw, kw, cos_t, sin_t, seg_ones, *slots)


DIL_PLANES = 1


def _dil_fwd(q, k, v, slots):
    r, L, _ = q.shape
    nb = L // BLOCK
    P = GROUP // LANES
    PL = min(DIL_PLANES, r)
    ns = len(slots)
    units = [(pp, hp) for pp in range(PL) for hp in range(P)]

    def body(q_ref, kc_ref, kp_ref, vc_ref, vp_ref, *rest):
        o_ref, l_ref = rest[ns:ns + 2]
        n = pl.program_id(1)
        finish = (_hosted_gathers(rest[:ns], rest[ns + 2:2 * ns + 2], *rest[2 * ns + 2:],
                                  pl.program_id(0) * nb + n, (r // PL) * nb) if ns else None)
        rowi = lax.broadcasted_iota(jnp.int32, (BLOCK, BLOCK), 0)
        coli = lax.broadcasted_iota(jnp.int32, (BLOCK, BLOCK), 1)
        first = coli < HEAD_DIM
        masks = (coli <= rowi, jnp.logical_and(coli >= rowi, n > 0))
        s2 = {}
        for pp, hp in units:
            q2 = _scaled(q_ref[pp, :, _lanes(hp)])
            for b, k_ref in enumerate((kc_ref, kp_ref)):
                s2[pp, hp, b] = _dot_nt(q2, _by_head(k_ref[pp, :, _lanes(hp)], first))
        ps, inv, lse = {}, {}, {}
        for pp, hp in units:
            for h in range(2):
                s = [jnp.where(masks[b], s2[pp, hp, b][:, h * BLOCK:(h + 1) * BLOCK], NEG) for b in range(2)]
                m = jnp.maximum(jnp.max(s[0], axis=1, keepdims=True), jnp.max(s[1], axis=1, keepdims=True))
                p = [jnp.exp(s[b] - m) for b in range(2)]
                den = jnp.sum(p[0], axis=1, keepdims=True) + jnp.sum(p[1], axis=1, keepdims=True)
                ps[pp, hp, h] = [p[b].astype(BF16) for b in range(2)]
                inv[pp, hp, h] = 1.0 / den
                lse[pp, hp, h] = m + jnp.log(den)
        for pp, hp in units:
            o = jnp.zeros((BLOCK, LANES), F32)
            for b, v_ref in enumerate((vc_ref, vp_ref)):
                o = o + _dot(jnp.concatenate([ps[pp, hp, 0][b], ps[pp, hp, 1][b]], axis=1),
                             _by_head(v_ref[pp, :, _lanes(hp)], first))
            o_ref[pp, :, _lanes(hp)] = (o * jnp.where(first, inv[pp, hp, 0], inv[pp, hp, 1])).astype(BF16)
            l_ref[pp, :, _lanes(hp)] = jnp.where(first, lse[pp, hp, 0], lse[pp, hp, 1])
        if finish is not None:
            finish()

    cur = pl.BlockSpec((PL, BLOCK, GROUP), lambda c, n: (c, n, 0))
    prev = pl.BlockSpec((PL, BLOCK, GROUP), lambda c, n: (c, jnp.maximum(n - 1, 0), 0))
    h_in, h_out, h_shape, h_sems = _hosted_specs(slots)
    return pl.pallas_call(
        body, name="dil_fwd_r%d" % r, grid=(r // PL, nb),
        in_specs=[cur, cur, prev, cur, prev] + h_in, out_specs=[cur, cur] + h_out,
        out_shape=[jax.ShapeDtypeStruct(q.shape, BF16), jax.ShapeDtypeStruct(q.shape, F32)] + h_shape,
        input_output_aliases={5 + a: 2 + a for a in range(ns)},
        scratch_shapes=h_sems,
    )(q, k, k, v, v, *slots)


def _dil_bwd(q, k, v, do, lse, delta, parts, pairs):
    r, L, _ = q.shape
    nb = L // BLOCK
    P = GROUP // LANES
    PL = min(DIL_PLANES, r)
    scale = HEAD_DIM ** -0.5
    units = [(pp, hp) for pp in range(PL) for hp in range(P)]
    npt, npr = len(parts), len(pairs)
    nx = npt + npr

    def body(qc_ref, qn_ref, doc_ref, don_ref, lc_ref, ln_ref, dc_ref, dn_ref, k_ref, v_ref, *rest):
        x_in, (dq_ref, dk_ref, dv_ref), x_out = rest[:nx], rest[nx:nx + 3], rest[nx + 3:2 * nx + 3]
        carry, send, recv = rest[2 * nx + 3:]
        j = pl.program_id(1)
        step = pl.program_id(0) * nb + j
        plans = [_chip_send_plan(x_in[a], x_out[a], send.at[pl.ds(3 * a, 3)], recv.at[pl.ds(3 * a, 3)])
                 for a in range(npt)]
        plans += [_pair_send_plan(x_in[npt + a], x_out[npt + a], send.at[3 * npt + a], recv.at[3 * npt + a])
                  for a in range(npr)]

        def start():
            for begin, _ in plans:
                begin()

        def finish():
            for _, end in plans:
                end()

        pl.when(step == 0)(start)
        rowi = lax.broadcasted_iota(jnp.int32, (BLOCK, BLOCK), 0)
        coli = lax.broadcasted_iota(jnp.int32, (BLOCK, BLOCK), 1)
        first = coli < HEAD_DIM
        sides = ((qc_ref, doc_ref, lc_ref, dc_ref, coli <= rowi),
                 (qn_ref, don_ref, ln_ref, dn_ref, jnp.logical_and(coli >= rowi, j < nb - 1)))

        @pl.when(j == 0)
        def _():
            carry[...] = jnp.zeros_like(carry)

        kcat, q2, do2, s2, dp2 = {}, {}, {}, {}, {}
        for pp, hp in units:
            kcat[pp, hp] = _by_head(k_ref[pp, :, _lanes(hp)], first)
            vcat = _by_head(v_ref[pp, :, _lanes(hp)], first)
            for x, (q_r, do_r, _, _, _) in enumerate(sides):
                q2[pp, hp, x] = _scaled(q_r[pp, :, _lanes(hp)])
                do2[pp, hp, x] = do_r[pp, :, _lanes(hp)]
                s2[pp, hp, x] = _dot_nt(q2[pp, hp, x], kcat[pp, hp])
                dp2[pp, hp, x] = _dot_nt(do2[pp, hp, x], vcat)
        pcat, dscat = {}, {}
        for pp, hp in units:
            for x, (_, _, l_r, d_r, msk) in enumerate(sides):
                ps, dss = [], []
                for h in range(2):
                    col = hp * LANES + h * HEAD_DIM
                    half = slice(h * BLOCK, (h + 1) * BLOCK)
                    p = jnp.where(msk, jnp.exp(s2[pp, hp, x][:, half] - l_r[pp, :, col:col + 1]), 0.0)
                    ps.append(p.astype(BF16))
                    dss.append((p * (dp2[pp, hp, x][:, half] - d_r[pp, :, col:col + 1])).astype(BF16))
                pcat[pp, hp, x] = jnp.concatenate(ps, axis=1)
                dscat[pp, hp, x] = jnp.concatenate(dss, axis=1)
        for pp, hp in units:
            dv2 = _dot_tn(pcat[pp, hp, 0], do2[pp, hp, 0]) + _dot_tn(pcat[pp, hp, 1], do2[pp, hp, 1])
            dk2 = _dot_tn(dscat[pp, hp, 0], q2[pp, hp, 0]) + _dot_tn(dscat[pp, hp, 1], q2[pp, hp, 1])
            dv_ref[pp, :, _lanes(hp)] = jnp.where(first, dv2[:BLOCK], dv2[BLOCK:]).astype(BF16)
            dk_ref[pp, :, _lanes(hp)] = jnp.where(first, dk2[:BLOCK], dk2[BLOCK:]).astype(BF16)
            dq_ref[pp, :, _lanes(hp)] = (carry[pp, :, _lanes(hp)]
                                         + _dot(dscat[pp, hp, 0], kcat[pp, hp]) * scale).astype(BF16)
            carry[pp, :, _lanes(hp)] = _dot(dscat[pp, hp, 1], kcat[pp, hp]) * scale
        pl.when(step == (r // PL) * nb - 1)(finish)

    cur = pl.BlockSpec((PL, BLOCK, GROUP), lambda c, n: (c, n, 0))
    nxt = pl.BlockSpec((PL, BLOCK, GROUP), lambda c, n: (c, jnp.minimum(n + 1, nb - 1), 0))
    return pl.pallas_call(
        body, name="dil_bwd_r%d" % r, grid=(r // PL, nb),
        in_specs=[cur, nxt, cur, nxt, cur, nxt, cur, nxt, cur, cur] + [HBM] * nx, out_specs=[cur, cur, cur] + [HBM] * nx,
        out_shape=[jax.ShapeDtypeStruct(q.shape, BF16)] * 3 + [jax.ShapeDtypeStruct(p.shape, p.dtype) for p in parts]
                  + [jax.ShapeDtypeStruct((g.shape[0], g.shape[1] // 2, g.shape[2]), g.dtype) for g in pairs],
        scratch_shapes=[pltpu.VMEM((PL, BLOCK, GROUP), F32), pltpu.SemaphoreType.DMA((3 * npt + npr,)),
                        pltpu.SemaphoreType.DMA((3 * npt + npr,))],
    )(q, q, do, do, lse, lse, delta, delta, k, v, *parts, *pairs)


SB_TILES = 2
SB_PAIRS_FWD = 4
SB_PAIRS_BWD = 4
SB_DEAD = -110.0


def _lanes(hp):
    return slice(hp * LANES, (hp + 1) * LANES)


def _sb_logits(z, valid):
    e = jnp.exp(-jnp.abs(z))
    lb = jnp.minimum(z, 0.0) - jnp.log(1.0 + e)
    lk = lb - z
    if valid is not None:
        lk = jnp.where(valid, lk, 0.0)
    return e, lb, lk


def _by_head(t, first):
    zero = jnp.zeros_like(t)
    return jnp.concatenate([jnp.where(first, t, zero), jnp.where(first, zero, t)], axis=0)


def _sb_valid(i, j):
    rowi = lax.broadcasted_iota(jnp.int32, (BLOCK, BLOCK), 0)
    coli = lax.broadcasted_iota(jnp.int32, (BLOCK, BLOCK), 1)
    return (coli - rowi) < (i - j) * BLOCK


def _scaled(q):
    return (q.astype(F32) * (HEAD_DIM ** -0.5)).astype(BF16)


def _hosted_gathers(refs_in, refs_out, send, recv, step, steps):
    plans = [_gather_plan(refs_in[a], refs_out[a], send.at[pl.ds(6 * a, 6)], recv.at[pl.ds(6 * a, 6)])
             for a in range(len(refs_in))]
    for stage, at in ((0, 0), (1, (2 * steps) // 3)):
        @pl.when(step == at)
        def _():
            for plan in plans:
                plan[stage]()

    def finish():
        @pl.when(step == steps - 1)
        def _():
            for plan in plans:
                plan[2]()

    return finish


def _hosted_specs(slots):
    n = len(slots)
    sems = [pltpu.SemaphoreType.DMA((6 * n,))] * 2 if n else []
    return [HBM] * n, [HBM] * n, [jax.ShapeDtypeStruct(s.shape, s.dtype) for s in slots], sems


def _sb_fwd(qs, ks, vs, tri_later, slots):
    S = qs.shape[0]
    P = SB_PAIRS_FWD
    W = P * LANES
    ns = len(slots)

    def body(q_ref, k_ref, v_ref, u_ref, *rest):
        o_ref, lt_ref, from_ref = rest[ns:ns + 3]
        i = pl.program_id(1)
        finish = _hosted_gathers(rest[:ns], rest[ns + 3:2 * ns + 3], *rest[2 * ns + 3:], i, S // BLOCK) if ns else None
        first = lax.broadcasted_iota(jnp.int32, (BLOCK, LANES), 1) < HEAD_DIM
        q2 = [_scaled(q_ref[:, _lanes(hp)]) for hp in range(P)]

        def walk(tiles, carry):
            runs, accs = list(carry[0]), list(carry[1])
            units = [(t, hp) for t in range(len(tiles)) for hp in range(P)]
            offs = [pl.multiple_of(j * BLOCK, BLOCK) for j, _ in tiles]
            valids = [_sb_valid(i, j) if diag else None for j, diag in tiles]
            z2s, lbs, c2s = {}, {}, {}
            for t, hp in units:
                z2s[t, hp] = _dot_nt(q2[hp], _by_head(k_ref[pl.ds(offs[t], BLOCK), _lanes(hp)], first))
            for t, hp in units:
                for h in range(2):
                    _, lb, lk = _sb_logits(z2s[t, hp][:, h * BLOCK:(h + 1) * BLOCK], valids[t])
                    lbs[t, hp, h] = lb
                    c2s[t, hp, h] = _dot(jnp.concatenate(_split(lk), axis=1), u_ref[...])
            for t, hp in units:
                a2 = []
                for h in range(2):
                    a = jnp.exp(lbs[t, hp, h] + c2s[t, hp, h][:, :BLOCK] + runs[2 * hp + h])
                    if valids[t] is not None:
                        a = jnp.where(valids[t], a, 0.0)
                    a2.append(a.astype(BF16))
                    runs[2 * hp + h] = runs[2 * hp + h] + c2s[t, hp, h][:, BLOCK:]
                vcat = _by_head(v_ref[pl.ds(offs[t], BLOCK), _lanes(hp)], first)
                accs[hp] = accs[hp] + _dot(jnp.concatenate(a2, axis=1), vcat)
            return tuple(runs), tuple(accs)

        def chunk(ci, carry):
            return walk([(ci * SB_TILES + t, False) for t in reversed(range(SB_TILES))], carry)

        def alive(runs):
            top = functools.reduce(jnp.maximum, runs)
            return (jnp.max(top) > SB_DEAD).astype(jnp.int32)

        def step(c):
            t, _, runs, accs = c
            runs, accs = chunk(nfull - 1 - t, (runs, accs))
            return t + 1, alive(runs), runs, accs

        zero = jnp.zeros((BLOCK, LANES), F32)
        nfull = i // SB_TILES
        ragged = [functools.partial(walk, [(i, True)] + [(i - 1 - m, False) for m in range(extra)])
                  for extra in range(SB_TILES)]
        runs, accs = lax.switch(i % SB_TILES, ragged, ((zero,) * (2 * P), (zero,) * P))
        done, _, runs, accs = lax.while_loop(lambda c: jnp.logical_and(c[0] < nfull, c[1] > 0), step,
                                             (jnp.int32(0), alive(runs), runs, accs))
        for hp in range(P):
            o_ref[:, _lanes(hp)] = accs[hp]
            lt_ref[:, _lanes(hp)] = jnp.where(first, runs[2 * hp], runs[2 * hp + 1])
        from_ref[...] = jnp.full(from_ref.shape, nfull - done, jnp.int32)
        if finish is not None:
            finish()

    assert W == GROUP
    blk = pl.BlockSpec((BLOCK, W), lambda hp, i: (i, hp))
    col = pl.BlockSpec((S, W), lambda hp, i: (0, hp))
    h_in, h_out, h_shape, h_sems = _hosted_specs(slots)
    return pl.pallas_call(
        body, name="sb_fwd", grid=(GROUP // W, S // BLOCK),
        in_specs=[blk, col, col, _full((2 * BLOCK, 2 * BLOCK))] + h_in,
        out_specs=[blk, blk, pl.BlockSpec((1, 8, LANES), lambda hp, i: (i, 0, 0))] + h_out,
        out_shape=[jax.ShapeDtypeStruct((S, GROUP), F32)] * 2
                  + [jax.ShapeDtypeStruct((S // BLOCK, 8, LANES), jnp.int32)] + h_shape,
        input_output_aliases={4 + a: 3 + a for a in range(ns)},
        scratch_shapes=h_sems,
    )(qs, ks, vs, tri_later, *slots)


def _sb_bwd(first_chunk, qs, ks, vs, do, ltot, tri_upto, tri_before):
    S = qs.shape[0]
    P = SB_PAIRS_BWD
    W = P * LANES

    def body(from_ref, q_ref, k_ref, v_ref, do_ref, lt_ref, w_ref, x_ref, dq_ref, dk_ref, dv_ref):
        i = pl.program_id(1)

        @pl.when(i == 0)
        def _():
            dk_ref[...] = jnp.zeros_like(dk_ref)
            dv_ref[...] = jnp.zeros_like(dv_ref)

        first = lax.broadcasted_iota(jnp.int32, (BLOCK, LANES), 1) < HEAD_DIM
        q2 = [_scaled(q_ref[:, _lanes(hp)]) for hp in range(P)]
        do2 = [do_ref[:, _lanes(hp)] for hp in range(P)]
        totals = [jnp.broadcast_to(lt_ref[:, n * HEAD_DIM:n * HEAD_DIM + 1], (BLOCK, LANES)) for n in range(2 * P)]

        def walk(tiles, carry):
            keeps, grads, dqs = list(carry[0]), list(carry[1]), list(carry[2])
            units = [(t, hp) for t in range(len(tiles)) for hp in range(P)]
            offs = [pl.multiple_of(j * BLOCK, BLOCK) for j, _ in tiles]
            valids = [_sb_valid(i, j) if diag else None for j, diag in tiles]
            kcat, z2, da2, es, lbs, c2s, as_, des, p2s = {}, {}, {}, {}, {}, {}, {}, {}, {}
            for t, hp in units:
                kcat[t, hp] = _by_head(k_ref[pl.ds(offs[t], BLOCK), _lanes(hp)], first)
                z2[t, hp] = _dot_nt(q2[hp], kcat[t, hp])
                da2[t, hp] = _dot_nt(do2[hp], _by_head(v_ref[pl.ds(offs[t], BLOCK), _lanes(hp)], first))
            for t, hp in units:
                for h in range(2):
                    es[t, hp, h], lbs[t, hp, h], lk = _sb_logits(z2[t, hp][:, h * BLOCK:(h + 1) * BLOCK], valids[t])
                    c2s[t, hp, h] = _dot(jnp.concatenate(_split(lk), axis=1), w_ref[...])
            for t, hp in units:
                for h in range(2):
                    n = 2 * hp + h
                    a = jnp.exp(lbs[t, hp, h] + (totals[n] - (keeps[n] + c2s[t, hp, h][:, :BLOCK])))
                    if valids[t] is not None:
                        a = jnp.where(valids[t], a, 0.0)
                    keeps[n] = keeps[n] + c2s[t, hp, h][:, BLOCK:]
                    de = a * da2[t, hp][:, h * BLOCK:(h + 1) * BLOCK]
                    as_[t, hp, h], des[t, hp, h] = a.astype(BF16), de
                    p2s[t, hp, h] = _dot(jnp.concatenate(_split(de), axis=1), x_ref[...])
            for t, hp in units:
                dz2 = []
                for h in range(2):
                    n = 2 * hp + h
                    e = es[t, hp, h]
                    sig = jnp.where(z2[t, hp][:, h * BLOCK:(h + 1) * BLOCK] >= 0.0, 1.0, e) / (1.0 + e)
                    dz = des[t, hp, h] * (1.0 - sig) - (grads[n] + p2s[t, hp, h][:, :BLOCK]) * sig
                    if valids[t] is not None:
                        dz = jnp.where(valids[t], dz, 0.0)
                    grads[n] = grads[n] + p2s[t, hp, h][:, BLOCK:]
                    dz2.append(dz.astype(BF16))
                dzcat = jnp.concatenate(dz2, axis=1)
                dk2 = _dot_tn(dzcat, q2[hp])
                dv2 = _dot_tn(jnp.concatenate([as_[t, hp, 0], as_[t, hp, 1]], axis=1), do2[hp])
                dk_ref[pl.ds(offs[t], BLOCK), _lanes(hp)] += jnp.where(first, dk2[:BLOCK], dk2[BLOCK:])
                dv_ref[pl.ds(offs[t], BLOCK), _lanes(hp)] += jnp.where(first, dv2[:BLOCK], dv2[BLOCK:])
                dqs[hp] = dqs[hp] + _dot(dzcat, kcat[t, hp])
            return tuple(keeps), tuple(grads), tuple(dqs)

        zero = jnp.zeros((BLOCK, LANES), F32)
        nfull = i // SB_TILES
        carry = lax.fori_loop(
            from_ref[i], nfull, lambda ci, c: walk([(ci * SB_TILES + t, False) for t in range(SB_TILES)], c),
            ((zero,) * (2 * P), (zero,) * (2 * P), (zero,) * P))
        ragged = [functools.partial(walk, [(i - m, False) for m in range(extra, 0, -1)] + [(i, True)])
                  for extra in range(SB_TILES)]
        carry = lax.switch(i % SB_TILES, ragged, carry)
        for hp in range(P):
            dq_ref[:, _lanes(hp)] = carry[2][hp] * (HEAD_DIM ** -0.5)

    blk = pl.BlockSpec((BLOCK, W), lambda hp, i, fr: (i, hp))
    col = pl.BlockSpec((S, W), lambda hp, i, fr: (0, hp))
    tri = pl.BlockSpec((2 * BLOCK, 2 * BLOCK), lambda hp, i, fr: (0, 0))
    return pl.pallas_call(
        body, name="sb_bwd",
        grid_spec=pltpu.PrefetchScalarGridSpec(
            num_scalar_prefetch=1, grid=(GROUP // W, S // BLOCK),
            in_specs=[blk, col, col, blk, blk, tri, tri], out_specs=[blk, col, col]),
        out_shape=[jax.ShapeDtypeStruct((S, GROUP), F32)] * 3,
    )(first_chunk, qs, ks, vs, do, ltot, tri_upto, tri_before)


def _out_proj_fwd(o_br, l_br, o_sb, x, w_dil, w_sbn, w_out_g, slots):
    S, D = x.shape
    tm = 512
    ns = len(slots)

    def body(o0, o1, o2, l0, l1, l2, os_ref, x_ref, wd_ref, ws_ref, w_ref, *rest):
        od_ref, s0, s1, s2, x1_ref = rest[ns:ns + 5]
        scr = rest[2 * ns + 5]
        finish = (_hosted_gathers(rest[:ns], rest[ns + 5:2 * ns + 5], *rest[2 * ns + 6:], pl.program_id(0), S // tm)
                  if ns else None)
        ls = [_from_strided(scr, l) for l in (l0, l1, l2)]
        os_ = [_from_strided(scr, o) for o in (o0, o1, o2)]
        m = jnp.maximum(jnp.maximum(ls[0], ls[1]), ls[2])
        es = [jnp.exp(l - m) for l in ls]
        den = es[0] + es[1] + es[2]
        od = (es[0] * os_[0] + es[1] * os_[1] + es[2] * os_[2]) / den
        od_ref[...] = od
        _to_strided(scr, m + jnp.log(den), list(zip(DILATIONS, (s0, s1, s2))))
        osb = os_ref[...]
        mixed = jnp.concatenate([(od * _rms(od) * wd_ref[...]).astype(BF16),
                                 (osb * _rms(osb) * ws_ref[...]).astype(BF16)], axis=1)
        x1_ref[...] = x_ref[...] + _dot(mixed, w_ref[...])
        if finish is not None:
            finish()

    row = lambda i: (i, 0)
    g = pl.BlockSpec((tm, GROUP), row)
    d = pl.BlockSpec((tm, D), row)
    planes = [_strided_spec(tm, r) for r in DILATIONS]
    h_in, h_out, h_shape, h_sems = _hosted_specs(slots)
    return pl.pallas_call(
        body, name="out_proj_fwd", grid=(S // tm,),
        in_specs=planes * 2 + [g, d, _full((1, GROUP)), _full((1, GROUP)), _full((2 * GROUP, D))] + h_in,
        out_specs=[g] + planes + [d] + h_out,
        out_shape=[jax.ShapeDtypeStruct((S, GROUP), F32)] + [_strided_shape(S, r, F32) for r in DILATIONS]
                  + [jax.ShapeDtypeStruct((S, D), F32)] + h_shape,
        input_output_aliases={11 + a: 5 + a for a in range(ns)},
        scratch_shapes=[_strided_scratch(tm)] + h_sems,
    )(*o_br, *l_br, o_sb, x, w_dil, w_sbn, w_out_g, *slots)


def _ffn_fwd(x1, target, ffn_w, wg_g, wu_g, wd_g):
    S, D = x1.shape
    F = wg_g.shape[1]
    tm = 512
    nt = S // tm

    def body(x_ref, t_ref, nw_ref, wg_ref, wu_ref, wd_ref, h_ref, g_ref, u_ref, dy_ref, loss_ref, h_s, acc):
        j = pl.program_id(1)

        @pl.when(j == 0)
        def _():
            xv = x_ref[...]
            h = (xv * _rms(xv) * nw_ref[...]).astype(BF16)
            h_s[...] = h
            h_ref[...] = h
            acc[...] = xv

        h = h_s[...]
        g = _dot_nt(h, wg_ref[0])
        u = _dot_nt(h, wu_ref[0])
        g_ref[0] = g.astype(BF16)
        u_ref[0] = u.astype(BF16)
        a = (g * _sigmoid(g) * u).astype(BF16)
        acc[...] += _dot(a, wd_ref[0])

        @pl.when(j == N_CHIPS - 1)
        def _():
            err = acc[...] - t_ref[...]
            dy_ref[...] = err * (1.0 / D)
            loss_ref[...] = jnp.full(loss_ref.shape, jnp.sum(err * err), F32)

    row = lambda t, j: (t, 0)
    shard = lambda t, j: (j, 0, 0)
    act = lambda t, j: (j, t, 0)
    return pl.pallas_call(
        body, name="ffn_fwd", grid=(nt, N_CHIPS),
        in_specs=[pl.BlockSpec((tm, D), row), pl.BlockSpec((tm, D), row), pl.BlockSpec((1, D), lambda t, j: (0, 0))]
                 + [pl.BlockSpec((1, F, D), shard)] * 3,
        out_specs=[pl.BlockSpec((tm, D), row), pl.BlockSpec((1, tm, F), act), pl.BlockSpec((1, tm, F), act),
                   pl.BlockSpec((tm, D), row), pl.BlockSpec((1, 8, LANES), lambda t, j: (t, 0, 0))],
        out_shape=[jax.ShapeDtypeStruct((S, D), BF16), jax.ShapeDtypeStruct((N_CHIPS, S, F), BF16),
                   jax.ShapeDtypeStruct((N_CHIPS, S, F), BF16), jax.ShapeDtypeStruct((S, D), F32),
                   jax.ShapeDtypeStruct((nt, 8, LANES), F32)],
        scratch_shapes=[pltpu.VMEM((tm, D), BF16), pltpu.VMEM((tm, D), F32)],
    )(x1, target, ffn_w, wg_g, wu_g, wd_g)


def _ffn_bwd(h2, dy, g, u, wg_g, wu_g, wd_g):
    S, D = dy.shape
    F = wg_g.shape[1]
    tm = 512

    def body(h_ref, dy_ref, g_ref, u_ref, wg_ref, wu_ref, wd_ref, dwg_ref, dwu_ref, dwd_ref, dh_ref, *narrow):
        t = pl.program_id(1)

        @pl.when(t == 0)
        def _():
            dwg_ref[...] = jnp.zeros_like(dwg_ref)
            dwu_ref[...] = jnp.zeros_like(dwu_ref)
            dwd_ref[...] = jnp.zeros_like(dwd_ref)

        h = h_ref[...]
        dyb = dy_ref[...].astype(BF16)
        gv = g_ref[0].astype(F32)
        uv = u_ref[0].astype(F32)
        da = _dot_nt(dyb, wd_ref[0])
        sg = _sigmoid(gv)
        silu = gv * sg
        du = (da * silu).astype(BF16)
        dg = (da * uv * (sg * (1.0 + gv * (1.0 - sg)))).astype(BF16)
        dwd_ref[0] += _dot_tn((silu * uv).astype(BF16), dyb)
        dwg_ref[0] += _dot_tn(dg, h)
        dwu_ref[0] += _dot_tn(du, h)
        dh_ref[0] = (_dot(dg, wg_ref[0]) + _dot(du, wu_ref[0])).astype(BF16)

        @pl.when(t == S // tm - 1)
        def _():
            for full, half in zip((dwg_ref, dwu_ref, dwd_ref), narrow):
                half[...] = full[...].astype(BF16)

    row = lambda j, t: (t, 0)
    shard = lambda j, t: (j, 0, 0)
    act = lambda j, t: (j, t, 0)
    return pl.pallas_call(
        body, name="ffn_bwd", grid=(N_CHIPS, S // tm),
        in_specs=[pl.BlockSpec((tm, D), row), pl.BlockSpec((tm, D), row),
                  pl.BlockSpec((1, tm, F), act), pl.BlockSpec((1, tm, F), act)] + [pl.BlockSpec((1, F, D), shard)] * 3,
        out_specs=[pl.BlockSpec((1, F, D), shard)] * 3 + [pl.BlockSpec((1, tm, D), act)]
                  + [pl.BlockSpec((1, F, D), shard)] * 3,
        out_shape=[jax.ShapeDtypeStruct((N_CHIPS, F, D), F32)] * 3 + [jax.ShapeDtypeStruct((N_CHIPS, S, D), BF16)]
                  + [jax.ShapeDtypeStruct((N_CHIPS, F, D), BF16)] * 3,
    )(h2, dy, g, u, wg_g, wu_g, wd_g)


def _out_proj_bwd(dh2p, dy, x1, ffn_w, w_out_g, o_dil, o_sb, w_dil, w_sbn, seg_ones, ffn_grads):
    S, D = dy.shape
    tm = 512
    ng = len(ffn_grads)

    def body(dh_ref, dy_ref, x1_ref, nw_ref, w_ref, od_ref, os_ref, wd_ref, ws_ref, g_ref, *rest):
        gin, rest = rest[:ng], rest[ng:]
        dx1_ref, dod0, dod1, dod2, dos_ref, dl0, dl1, dl2, dw_ref, dnw_ref, dwd_ref, dws_ref = rest[:12]
        gout, (scr, send, recv) = rest[12:12 + ng], rest[12 + ng:]
        i = pl.program_id(0)
        plans = [_pair_send_plan(gin[a], gout[a], send.at[a], recv.at[a]) for a in range(ng)]

        @pl.when(i == 0)
        def _():
            for start, _ in plans:
                start()

        @pl.when(i == 0)
        def _():
            for r_ in (dw_ref, dnw_ref, dwd_ref, dws_ref):
                r_[...] = jnp.zeros_like(r_)

        dh2 = _sum4(dh_ref)
        dxn, dwn = _rms_bwd(dh2, x1_ref[...], nw_ref[...])
        dnw_ref[...] += jnp.sum(dwn, axis=0, keepdims=True)
        dx1 = dy_ref[...] + dxn
        dx1_ref[...] = dx1
        dx1b = dx1.astype(BF16)
        dmix = _dot_nt(dx1b, w_ref[...])
        od = od_ref[...]
        osb = os_ref[...]
        mixed = jnp.concatenate([(od * _rms(od) * wd_ref[...]).astype(BF16),
                                 (osb * _rms(osb) * ws_ref[...]).astype(BF16)], axis=1)
        dw_ref[...] += _dot_tn(mixed, dx1b)
        do, dwo = _rms_bwd(dmix[:, :GROUP], od, wd_ref[...])
        dwd_ref[...] += jnp.sum(dwo, axis=0, keepdims=True)
        _to_strided(scr, do, list(zip(DILATIONS, (dod0, dod1, dod2))))
        _to_strided(scr, _segsum(do * od, g_ref[...]), list(zip(DILATIONS, (dl0, dl1, dl2))))
        do, dwo = _rms_bwd(dmix[:, GROUP:], osb, ws_ref[...])
        dws_ref[...] += jnp.sum(dwo, axis=0, keepdims=True)
        dos_ref[...] = do.astype(BF16)

        @pl.when(i == S // tm - 1)
        def _():
            for _, finish in plans:
                finish()

    row = lambda i: (i, 0)
    gsp = pl.BlockSpec((tm, GROUP), row)
    dsp = pl.BlockSpec((tm, D), row)
    planes = [_strided_spec(tm, r) for r in DILATIONS]
    halves = [jax.ShapeDtypeStruct((g.shape[0], g.shape[1] // 2, g.shape[2]), g.dtype) for g in ffn_grads]
    return pl.pallas_call(
        body, name="out_proj_bwd", grid=(S // tm,),
        in_specs=[pl.BlockSpec((N_CHIPS, tm, D), lambda i: (0, i, 0)), dsp, dsp, _full((1, D)), _full((2 * GROUP, D)),
                  gsp, gsp, _full((1, GROUP)), _full((1, GROUP)), _full((GROUP, GROUP // 2))] + [HBM] * ng,
        out_specs=[dsp] + planes + [gsp] + planes
                  + [_full((2 * GROUP, D)), _full((1, D)), _full((1, GROUP)), _full((1, GROUP))] + [HBM] * ng,
        out_shape=[jax.ShapeDtypeStruct((S, D), F32)] + [_strided_shape(S, r, BF16) for r in DILATIONS]
                  + [jax.ShapeDtypeStruct((S, GROUP), BF16)] + [_strided_shape(S, r, F32) for r in DILATIONS]
                  + [jax.ShapeDtypeStruct((2 * GROUP, D), F32),
                     jax.ShapeDtypeStruct((1, D), F32), jax.ShapeDtypeStruct((1, GROUP), F32),
                     jax.ShapeDtypeStruct((1, GROUP), F32)] + halves,
        scratch_shapes=[_strided_scratch(tm), pltpu.SemaphoreType.DMA((ng,)), pltpu.SemaphoreType.DMA((ng,))],
    )(dh2p, dy, x1, ffn_w, w_out_g, o_dil, o_sb, w_dil, w_sbn, seg_ones, *ffn_grads)


def _attn_in_bwd(dq_br, dk_br, dv_br, dqs, dks, dvs, qa, ka, qw, kw, cos_t, sin_t, seg_ones, h, w_in_g, x, dx1, attn_w):
    S, D = x.shape
    wc = w_in_g.shape[2]
    tm = 256

    def body(q0, q1, q2, k0, k1, k2, v0, v1, v2, dqs_ref, dks_ref, dvs_ref, qa_ref, ka_ref, qw_ref, kw_ref,
             cos_ref, sin_ref, g_ref, h_ref, w_ref, x_ref, dx1_ref, aw_ref,
             gx_ref, dw_ref, daw_ref, dqw_ref, dkw_ref, accq, acck, scr):
        i = pl.program_id(0)

        @pl.when(i == 0)
        def _():
            accq[...] = jnp.zeros_like(accq)
            acck[...] = jnp.zeros_like(acck)
            dw_ref[...] = jnp.zeros_like(dw_ref)
            daw_ref[...] = jnp.zeros_like(daw_ref)

        def branches(refs):
            return (_from_strided(scr, refs[0]) + _from_strided(scr, refs[1])) + _from_strided(scr, refs[2])

        g = g_ref[...]
        cos = _tile4(cos_ref[...])
        sin = _tile4(sin_ref[...])
        pieces = []
        for refs, pre_ref, w_r, acc in (((q0, q1, q2), qa_ref, qw_ref, accq), ((k0, k1, k2), ka_ref, kw_ref, acck)):
            dh = branches(refs)
            dn = dh * cos + _rot_half(dh * sin)
            pre = pre_ref[...]
            rstd = lax.rsqrt(_segsum(pre * pre, g) * (1.0 / HEAD_DIM) + EPS)
            xh = pre * rstd
            acc[...] += jnp.sum(dn * xh, axis=0, keepdims=True)
            dxh = dn * w_r[...]
            pieces.append((rstd * (dxh - xh * (_segsum(dxh * xh, g) * (1.0 / HEAD_DIM)))).astype(BF16))
        pieces += [branches((v0, v1, v2)).astype(BF16), dqs_ref[...].astype(BF16), dks_ref[...].astype(BF16),
                   dvs_ref[...].astype(BF16)]
        dproj = jnp.concatenate(pieces, axis=1)
        hv = h_ref[...]
        dh = jnp.zeros((tm, D), F32)
        for j in range(N_CHIPS):
            dp = dproj[:, j * wc:(j + 1) * wc]
            dw_ref[j] += _dot_tn(hv, dp)
            dh = dh + _dot_nt(dp, w_ref[j])
        dx, dw = _rms_bwd(dh, x_ref[...], aw_ref[...])
        daw_ref[...] += jnp.sum(dw, axis=0, keepdims=True)
        gx_ref[...] = dx1_ref[...] + dx

        @pl.when(i == S // tm - 1)
        def _():
            for acc, o_ref in ((accq, dqw_ref), (acck, dkw_ref)):
                a = acc[...]
                pair = (a[:, 0:LANES] + a[:, LANES:2 * LANES]) + (a[:, 2 * LANES:3 * LANES] + a[:, 3 * LANES:4 * LANES])
                o_ref[...] = pair + pltpu.roll(pair, HEAD_DIM, 1)

    row = lambda i: (i, 0)
    gsp = pl.BlockSpec((tm, GROUP), row)
    dsp = pl.BlockSpec((tm, D), row)
    tab = pl.BlockSpec((tm, LANES), row)
    planes = [_strided_spec(tm, r) for r in DILATIONS]
    return pl.pallas_call(
        body, name="attn_in_bwd", grid=(S // tm,),
        in_specs=planes * 3 + [gsp] * 5 + [_full((1, GROUP)), _full((1, GROUP)), tab, tab, _full((GROUP, GROUP // 2)),
                                          dsp, _full((N_CHIPS, D, wc)), dsp, dsp, _full((1, D))],
        out_specs=[dsp, _full((N_CHIPS, D, wc)), _full((1, D)), _full((1, LANES)), _full((1, LANES))],
        out_shape=[jax.ShapeDtypeStruct((S, D), F32), jax.ShapeDtypeStruct((N_CHIPS, D, wc), F32),
                   jax.ShapeDtypeStruct((1, D), F32), jax.ShapeDtypeStruct((1, LANES), F32),
                   jax.ShapeDtypeStruct((1, LANES), F32)],
        scratch_shapes=[pltpu.VMEM((1, GROUP), F32), pltpu.VMEM((1, GROUP), F32), _strided_scratch(tm)],
    )(*dq_br, *dk_br, *dv_br, dqs, dks, dvs, qa, ka, qw, kw, cos_t, sin_t, seg_ones, h, w_in_g, x, dx1, attn_w)


def _constants(S):
    pos = jnp.arange(S, dtype=F32)
    inv_freq = ROPE_THETA ** (-jnp.arange(0, HEAD_DIM, 2, dtype=F32) / HEAD_DIM)
    ang_a = pos[::BLOCK, None] * inv_freq[None, :]
    ang_b = pos[:BLOCK, None] * inv_freq[None, :]
    ca, sa, cb, sb = jnp.cos(ang_a)[:, None], jnp.sin(ang_a)[:, None], jnp.cos(ang_b)[None], jnp.sin(ang_b)[None]
    cos = (ca * cb - sa * sb).reshape(S, HEAD_DIM // 2)
    sin = (sa * cb + ca * sb).reshape(S, HEAD_DIM // 2)
    cos_t = jnp.concatenate([cos, cos] * 2, axis=1)
    sin_t = jnp.concatenate([-sin, sin] * 2, axis=1)
    idx = jnp.arange(GROUP // 2)
    seg_ones = (idx[:, None] // HEAD_DIM == idx[None, :] // HEAD_DIM).astype(BF16)
    seg_ones = jnp.concatenate([seg_ones, seg_ones], axis=0)
    r = jnp.arange(BLOCK)
    ones = jnp.ones((BLOCK, BLOCK), BF16)
    tris = [jnp.concatenate([jnp.concatenate([m.astype(BF16), ones], axis=1)] * 2, axis=0) for m in
            (r[:, None] > r[None, :],
             r[:, None] <= r[None, :],
             r[:, None] < r[None, :])]
    return cos_t, sin_t, seg_ones, tris


FFN_NAMES = ("w_gate", "w_up", "w_down")


def _device_step(x, target, attn_w, qn_w, kn_w, dil_w, sbn_w, ffn_w, w_in_g, w_out_slots, ffn_slots, core, chip):
    S = x.shape[0]
    cos_t, sin_t, seg_ones, (tri_later, tri_upto, tri_before) = _constants(S)
    reps = GROUP // HEAD_DIM
    qw = jnp.tile(qn_w, (1, reps))
    kw = jnp.tile(kn_w, (1, reps))

    nd = len(DILATIONS)
    h, qa, ka, *rest, wg_g, w_out_g = _in_proj_fwd(x, attn_w, w_in_g, qw, kw, cos_t, sin_t, seg_ones,
                                                   [ffn_slots[0], w_out_slots])
    w_out_g = w_out_g.reshape(-1, x.shape[1])
    qh, kh, va, (qs, ks, vs) = rest[:nd], rest[nd:2 * nd], rest[2 * nd:3 * nd], rest[3 * nd:]
    branches = [_dil_fwd(qh[b], kh[b], va[b], []) for b in range(nd)]
    o_sb, ltot, walked, wu_g, wd_g = _sb_fwd(qs, ks, vs, tri_later, ffn_slots[1:])
    o_dil, *lse, x1 = _out_proj_fwd([b[0] for b in branches], [b[1] for b in branches], o_sb, x, dil_w, sbn_w,
                                    w_out_g, [])
    h2, g, u, dy, loss_parts = _ffn_fwd(x1, target, ffn_w, wg_g, wu_g, wd_g)

    *ffn_grads, dh2p, n0, n1, n2 = _ffn_bwd(h2, dy, g, u, wg_g, wu_g, wd_g)
    dx1, *mid, dw_out, dffn_w, ddil_w, dsbn_w, p0, p1, p2 = _out_proj_bwd(
        dh2p, dy, x1, ffn_w, w_out_g, o_dil, o_sb, dil_w, sbn_w, seg_ones, [n0, n1, n2])
    do_dil, do_sb, delta = mid[:nd], mid[nd], mid[nd + 1:]
    parts = [_pair_sum(gr, fr, core, n) for gr, fr, n in zip(ffn_grads, (p0, p1, p2), FFN_NAMES)]
    dqs, dks, dvs = _sb_bwd(walked[:, 0, 0], qs, ks, vs, do_sb, ltot, tri_upto, tri_before)
    dw_out = dw_out.reshape(N_CHIPS, -1, x.shape[1])
    dbr = [None] * nd
    dbr[0] = _dil_bwd(qh[0], kh[0], va[0], do_dil[0], lse[0], delta[0], [parts[0]], [dw_out])
    out_part = _pair_sum(dw_out, dbr[0][4], core, "w_out")
    dbr[1] = _dil_bwd(qh[1], kh[1], va[1], do_dil[1], lse[1], delta[1], [parts[1], out_part], [])
    dbr[2] = _dil_bwd(qh[2], kh[2], va[2], do_dil[2], lse[2], delta[2], [parts[2]], [])
    ffn_halves = [_chip_sum(dbr[b][3], parts[b], chip, FFN_NAMES[b]) for b in range(nd)]
    w_out_half = _chip_sum(dbr[1][4], out_part, chip, "w_out")
    grad_x, dw_in, dattn_w, dqw, dkw = _attn_in_bwd(
        [b[0] for b in dbr], [b[1] for b in dbr], [b[2] for b in dbr], dqs, dks, dvs,
        qa, ka, qw, kw, cos_t, sin_t, seg_ones, h, w_in_g, x, dx1, attn_w)
    small = dict(attn=dattn_w, q=dqw[:, :HEAD_DIM], k=dkw[:, :HEAD_DIM], dil=ddil_w, sb=dsbn_w, ffn=dffn_w)
    return loss_parts, grad_x, small, dw_in, w_out_half, ffn_halves


HBM = pl.BlockSpec(memory_space=pltpu.HBM)
VMEM = pl.BlockSpec(memory_space=pltpu.VMEM)
CHIP_FLIPS = ((1, 0), (0, 1), (1, 1))


def _place():
    return lax.axis_index("x"), lax.axis_index("y"), lax.axis_index("c")


def _flip(v, d):
    return 1 - v if d else v


def _half_rows(c, n):
    return pl.ds(pl.multiple_of(c * (n // 2), 16), n // 2)


def _gather_plan(slot_in, slot_out, send, recv):
    x, y, c = _place()
    p = 2 * x + y
    chips = [(_flip(x, dx), _flip(y, dy)) for dx, dy in CHIP_FLIPS]
    mine, other = _half_rows(c, slot_in.shape[1]), _half_rows(1 - c, slot_in.shape[1])

    def copy(k, src, dst, to):
        return pltpu.make_async_remote_copy(src_ref=src, dst_ref=dst, send_sem=send.at[k], recv_sem=recv.at[k],
                                            device_id=to, device_id_type=MESH)

    def first(k):
        return copy(k, slot_in.at[p, mine], slot_out.at[p, mine], (*chips[k], c))

    def passed(k, rows):
        land = slot_out.at[2 * chips[k][0] + chips[k][1], rows]
        return copy(3 + k, land, land, (x, y, 1 - c))

    def start():
        for k in range(3):
            first(k).start()

    def forward():
        for k in range(3):
            land = slot_out.at[2 * chips[k][0] + chips[k][1], mine]
            copy(k, land, land, (*chips[k], c)).wait_recv()
            passed(k, mine).start()

    def finish():
        for k in range(3):
            passed(k, other).wait_recv()
        for k in range(3):
            first(k).wait_send()
            passed(k, mine).wait_send()

    return start, forward, finish


def _chip_send_plan(part_in, recv_out, send, recv):
    x, y, c = _place()
    p = 2 * x + y
    chips = [(_flip(x, dx), _flip(y, dy)) for dx, dy in CHIP_FLIPS]

    def copy(k):
        q = 2 * chips[k][0] + chips[k][1]
        return pltpu.make_async_remote_copy(src_ref=part_in.at[q], dst_ref=recv_out.at[p], send_sem=send.at[k],
                                            recv_sem=recv.at[k], device_id=(*chips[k], c), device_id_type=MESH)

    def start():
        for k in range(3):
            copy(k).start()

    def finish():
        for k in range(3):
            land = recv_out.at[2 * chips[k][0] + chips[k][1]]
            pltpu.make_async_remote_copy(src_ref=land, dst_ref=land, send_sem=send.at[k], recv_sem=recv.at[k],
                                         device_id=(*chips[k], c), device_id_type=MESH).wait_recv()
        for k in range(3):
            copy(k).wait_send()

    return start, finish


def _pair_send_plan(grad_in, recv_out, send, recv):
    x, y, c = _place()

    def copy():
        theirs = _half_rows(1 - c, grad_in.shape[1])
        return pltpu.make_async_remote_copy(src_ref=grad_in.at[:, theirs, :], dst_ref=recv_out, send_sem=send,
                                            recv_sem=recv, device_id=(x, y, 1 - c), device_id_type=MESH)

    return (lambda: copy().start()), (lambda: copy().wait())


def _own_slots(shard):
    here = 2 * lax.axis_index("x") + lax.axis_index("y")
    return lax.dynamic_update_slice(lax.empty((N_CHIPS,) + shard.shape, shard.dtype), shard[None], (here, 0, 0))


def _gather_weights(shards):
    n = len(shards)

    def body(*refs):
        ins, outs = refs[:n], refs[n:2 * n]
        send, recv = refs[2 * n:]
        plans = [_gather_plan(ins[a], outs[a], send.at[pl.ds(6 * a, 6)], recv.at[pl.ds(6 * a, 6)]) for a in range(n)]
        for stage in range(3):
            for plan in plans:
                plan[stage]()

    slots = [_own_slots(s) for s in shards]
    return pl.pallas_call(
        body, name="gather_weights", in_specs=[HBM] * n, out_specs=[HBM] * n,
        out_shape=[jax.ShapeDtypeStruct(s.shape, s.dtype) for s in slots],
        input_output_aliases={a: a for a in range(n)},
        scratch_shapes=[pltpu.SemaphoreType.DMA((6 * n,)), pltpu.SemaphoreType.DMA((6 * n,))],
    )(*slots)


def _pair_exchange(grads, small):
    n = len(grads)

    def body(*refs):
        gin, sm = refs[:n], refs[n]
        gout, sm_all = refs[n + 1:2 * n + 1], refs[2 * n + 1]
        send, recv = refs[2 * n + 2:]
        x, y, c = _place()
        me = 4 * x + 2 * y + c
        big = [_pair_send_plan(gin[a], gout[a], send.at[a], recv.at[a]) for a in range(n)]
        for start, _ in big:
            start()
        sm_all[pl.ds(me, 1)] = sm[...][None]
        tiny = []
        for k in range(1, N_DEV):
            px, py, pc = _flip(x, k & 4), _flip(y, k & 2), _flip(c, k & 1)
            tiny.append((pltpu.make_async_remote_copy(
                src_ref=sm, dst_ref=sm_all.at[me], send_sem=send.at[n + k - 1], recv_sem=recv.at[n + k - 1],
                device_id=(px, py, pc), device_id_type=MESH), 4 * px + 2 * py + pc))
            tiny[-1][0].start()
        for k, (cp, peer) in enumerate(tiny):
            pltpu.make_async_remote_copy(src_ref=sm, dst_ref=sm_all.at[peer], send_sem=send.at[n + k],
                                         recv_sem=recv.at[n + k], device_id=(x, y, c),
                                         device_id_type=MESH).wait_recv()
            cp.wait_send()
        for _, finish in big:
            finish()

    halves = [jax.ShapeDtypeStruct((g.shape[0], g.shape[1] // 2, g.shape[2]), g.dtype) for g in grads]
    return pl.pallas_call(
        body, name="pair_exchange", in_specs=[HBM] * n + [VMEM], out_specs=[HBM] * n + [VMEM],
        out_shape=halves + [jax.ShapeDtypeStruct((N_DEV,) + small.shape, small.dtype)],
        scratch_shapes=[pltpu.SemaphoreType.DMA((n + N_DEV - 1,)), pltpu.SemaphoreType.DMA((n + N_DEV - 1,))],
    )(*grads, small)


def _chip_exchange(parts):
    n = len(parts)

    def body(*refs):
        pin, pout = refs[:n], refs[n:2 * n]
        send, recv = refs[2 * n:]
        plans = [_chip_send_plan(pin[a], pout[a], send.at[pl.ds(3 * a, 3)], recv.at[pl.ds(3 * a, 3)]) for a in range(n)]
        for stage in range(2):
            for plan in plans:
                plan[stage]()

    return pl.pallas_call(
        body, name="chip_exchange", in_specs=[HBM] * n, out_specs=[HBM] * n,
        out_shape=[jax.ShapeDtypeStruct(s.shape, s.dtype) for s in parts],
        scratch_shapes=[pltpu.SemaphoreType.DMA((3 * n,)), pltpu.SemaphoreType.DMA((3 * n,))],
    )(*parts)


def _pair_swap_plan(hin, hout, send, recv):
    x, y, c = _place()

    def copies():
        return [pltpu.make_async_remote_copy(src_ref=hin[a], dst_ref=hout[a], send_sem=send.at[a], recv_sem=recv.at[a],
                                             device_id=(x, y, 1 - c), device_id_type=MESH) for a in range(len(hin))]

    def start():
        for cp in copies():
            cp.start()

    def finish():
        for cp in copies():
            cp.wait()

    return start, finish


def _pair_swap(halves):
    n = len(halves)

    def body(*refs):
        start, finish = _pair_swap_plan(refs[:n], refs[n:2 * n], *refs[2 * n:])
        start()
        finish()

    return pl.pallas_call(
        body, name="pair_swap", in_specs=[HBM] * n, out_specs=[HBM] * n,
        out_shape=[jax.ShapeDtypeStruct(s.shape, s.dtype) for s in halves],
        scratch_shapes=[pltpu.SemaphoreType.DMA((n,)), pltpu.SemaphoreType.DMA((n,))],
    )(*halves)


def _pair_sum(grad, recv, c, tag):
    _, R, C = grad.shape
    hr = R // 2

    def body(c_ref, a_ref, b_ref, o_ref):
        o_ref[...] = (a_ref[...] + b_ref[...]).astype(BF16)

    return pl.pallas_call(
        body, name="pair_sum_" + tag,
        grid_spec=pltpu.PrefetchScalarGridSpec(
            num_scalar_prefetch=1, grid=(N_CHIPS,),
            in_specs=[pl.BlockSpec((1, hr, C), lambda s, cr: (s, cr[0], 0)),
                      pl.BlockSpec((1, hr, C), lambda s, cr: (s, 0, 0))],
            out_specs=pl.BlockSpec((1, hr, C), lambda s, cr: (s, 0, 0))),
        out_shape=jax.ShapeDtypeStruct((N_CHIPS, hr, C), BF16),
    )(c, grad, recv)


def _chip_sum(received, own, chip, tag):
    _, rows, C = received.shape
    tr = rows // 2

    def body(chip_ref, own_ref, r1_ref, r2_ref, r3_ref, o_ref):
        p = [r[0].astype(F32) for r in (own_ref, r1_ref, r2_ref, r3_ref)]
        o_ref[...] = (p[0] + p[1]) + (p[2] + p[3])

    def slot(k):
        return pl.BlockSpec((1, tr, C), lambda i, cr: (jnp.bitwise_xor(cr[0], k), i, 0))

    return pl.pallas_call(
        body, name="chip_sum_" + tag,
        grid_spec=pltpu.PrefetchScalarGridSpec(
            num_scalar_prefetch=1, grid=(rows // tr,), in_specs=[slot(0), slot(1), slot(2), slot(3)],
            out_specs=pl.BlockSpec((tr, C), lambda i, cr: (i, 0))),
        out_shape=jax.ShapeDtypeStruct((rows, C), F32),
    )(chip, own, received, received, received)


def _adamw_math(w, g, m, v):
    m = ADAM_B1 * m + (1.0 - ADAM_B1) * g
    v = ADAM_B2 * v + (1.0 - ADAM_B2) * (g * g)
    m_hat = m / (1.0 - ADAM_B1 ** ADAM_STEP)
    v_hat = v / (1.0 - ADAM_B2 ** ADAM_STEP)
    delta = -ADAM_LR * (m_hat / (jnp.sqrt(v_hat) + ADAM_EPS) + ADAM_WD * w)
    return delta, m, v


def _adamw(w, g_mine, g_other, m, v, c, tag):
    R, C = w.shape

    def body(c_ref, w_ref, gm_ref, go_ref, m_ref, v_ref, g_ref, d_ref, nm_ref, nv_ref):
        g = jnp.where(pl.program_id(0) == c_ref[0], gm_ref[...], go_ref[...])
        g_ref[...] = g
        d_ref[...], nm_ref[...], nv_ref[...] = _adamw_math(w_ref[...], g, m_ref[...], v_ref[...])

    blk = pl.BlockSpec((R // 2, C), lambda h, cr: (h, 0))
    half = pl.BlockSpec((R // 2, C), lambda h, cr: (0, 0))
    return pl.pallas_call(
        body, name="adamw_" + tag,
        grid_spec=pltpu.PrefetchScalarGridSpec(
            num_scalar_prefetch=1, grid=(2,), in_specs=[blk, half, half, blk, blk], out_specs=[blk] * 4),
        out_shape=[jax.ShapeDtypeStruct((R, C), F32)] * 4,
    )(c, w, g_mine, g_other, m, v)


def _small_update(all_small, w, m, v):
    def body(a_ref, w_ref, m_ref, v_ref, g_ref, d_ref, nm_ref, nv_ref):
        g = ((a_ref[0] + a_ref[1]) + (a_ref[2] + a_ref[3])) + ((a_ref[4] + a_ref[5]) + (a_ref[6] + a_ref[7]))
        g_ref[...] = g
        d_ref[...], nm_ref[...], nv_ref[...] = _adamw_math(w_ref[...], g, m_ref[...], v_ref[...])

    return pl.pallas_call(
        body, name="small_update", out_shape=[jax.ShapeDtypeStruct(w.shape, F32)] * 4,
    )(all_small, w, m, v)


SMALL_ROWS = (("attn", 0, 0), ("ffn", 1, 0), ("dil", 2, 0), ("sb", 2, GROUP), ("q", 3, 0), ("k", 3, HEAD_DIM),
              ("loss", 4, 0))


def _pack_small(vals, D):
    rows = [jnp.zeros((1, D), F32) for _ in range(8)]
    for name, r, off in SMALL_ROWS:
        if name in vals:
            rows[r] = lax.dynamic_update_slice(rows[r], vals[name].astype(F32), (0, off))
    return jnp.concatenate(rows, axis=0)


def _unpack_small(packed, vals):
    return {name: packed[r:r + 1, off:off + vals[name].shape[1]] for name, r, off in SMALL_ROWS if name in vals}


def kernel(x, attn_norm_w, w_in, q_norm_w, k_norm_w, dil_out_norm_w, sb_out_norm_w, w_out, ffn_norm_w, w_gate, w_up, w_down, loss_target, m_attn_norm_w, m_w_in, m_q_norm_w, m_k_norm_w, m_dil_out_norm_w, m_sb_out_norm_w, m_w_out, m_ffn_norm_w, m_w_gate, m_w_up, m_w_down, v_attn_norm_w, v_w_in, v_q_norm_w, v_k_norm_w, v_dil_out_norm_w, v_sb_out_norm_w, v_w_out, v_ffn_norm_w, v_w_gate, v_w_up, v_w_down):
    D = x.shape[-1]
    big_names = ("w_in", "w_out", "w_gate", "w_up", "w_down")
    flipped = ("w_gate", "w_up")
    tr = lambda a: jnp.swapaxes(a[0], 0, 1)
    big_w = dict(w_in=w_in[0], w_out=w_out[0], w_gate=tr(w_gate), w_up=tr(w_up), w_down=w_down[0])
    big_m = dict(w_in=m_w_in[0], w_out=m_w_out[0], w_gate=tr(m_w_gate), w_up=tr(m_w_up), w_down=m_w_down[0])
    big_v = dict(w_in=v_w_in[0], w_out=v_w_out[0], w_gate=tr(v_w_gate), w_up=tr(v_w_up), w_down=v_w_down[0])
    small_w = dict(attn=attn_norm_w, q=q_norm_w, k=k_norm_w, dil=dil_out_norm_w, sb=sb_out_norm_w, ffn=ffn_norm_w)
    small_m = dict(attn=m_attn_norm_w, q=m_q_norm_w, k=m_k_norm_w, dil=m_dil_out_norm_w, sb=m_sb_out_norm_w,
                   ffn=m_ffn_norm_w)
    small_v = dict(attn=v_attn_norm_w, q=v_q_norm_w, k=v_k_norm_w, dil=v_dil_out_norm_w, sb=v_sb_out_norm_w,
                   ffn=v_ffn_norm_w)

    c = lax.axis_index("c").astype(jnp.int32).reshape(1)
    chip = (2 * lax.axis_index("x") + lax.axis_index("y")).astype(jnp.int32).reshape(1)
    (w_in_g,) = _gather_weights([big_w["w_in"].astype(BF16)])
    w_out_slots = _own_slots(big_w["w_out"].astype(BF16))
    ffn_slots = [_own_slots(big_w[n].astype(BF16)) for n in FFN_NAMES]

    loss_parts, grad_x, small_g, dw_in, w_out_half, ffn_halves = _device_step(
        x[0], loss_target[0], attn_norm_w, q_norm_w, k_norm_w, dil_out_norm_w, sb_out_norm_w, ffn_norm_w,
        w_in_g, w_out_slots, ffn_slots, c, chip)
    small_g["loss"] = (jnp.sum(loss_parts[:, 0, 0]) * (0.5 / D)).reshape(1, 1)

    from_pair, all_small = _pair_exchange([dw_in], _pack_small(small_g, D))
    in_part = _pair_sum(dw_in, from_pair, c, "w_in")
    (from_chips,) = _chip_exchange([in_part])
    halves = [_chip_sum(from_chips, in_part, chip, "w_in"), w_out_half] + ffn_halves
    others = _pair_swap(halves)
    big_out = {n: _adamw(big_w[n], mine, other, big_m[n], big_v[n], c, n)
               for n, mine, other in zip(big_names, halves, others)}
    sg, sd, sm, sv = _small_update(all_small, _pack_small(small_w, D), _pack_small(small_m, D),
                                   _pack_small(small_v, D))
    small_out = [_unpack_small(t, small_w) for t in (sg, sd, sm, sv)]

    order = (("attn", None), (None, "w_in"), ("q", None), ("k", None), ("dil", None), ("sb", None),
             (None, "w_out"), ("ffn", None), (None, "w_gate"), (None, "w_up"), (None, "w_down"))
    outs = [sg[4, 0], grad_x[None]]
    for kind in range(4):
        for s_name, b_name in order:
            if s_name is not None:
                outs.append(small_out[kind][s_name])
            else:
                res = big_out[b_name][kind]
                outs.append((jnp.swapaxes(res, 0, 1) if b_name in flipped else res)[None])
    return tuple(outs)
```

```python
import functools

import jax
import jax.numpy as jnp
from jax import lax
from jax.experimental import pallas as pl
from jax.experimental.pallas import tpu as pltpu

F32 = jnp.float32
BF16 = jnp.bfloat16
MESH = pl.DeviceIdType.MESH

HEAD_DIM = 64
GROUP = 512
BLOCK = 128
LANES = 128
N_CHIPS = 4
N_DEV = 8
EPS = 1e-6
ROPE_THETA = 10000.0
DILATIONS = (1, 4, 16)
NEG = -1e30

ADAM_LR = 0.001
ADAM_B1 = 0.9
ADAM_B2 = 0.999
ADAM_EPS = 1e-08
ADAM_WD = 0.01
ADAM_STEP = 10


def _dot(a, b):
    return jnp.dot(a, b, preferred_element_type=F32)


def _dot_nt(a, b):
    return lax.dot_general(a, b, (((1,), (1,)), ((), ())), preferred_element_type=F32)


def _dot_tn(a, b):
    return lax.dot_general(a, b, (((0,), (0,)), ((), ())), preferred_element_type=F32)


def _split(v):
    hi = lax.bitcast_convert_type(lax.bitcast_convert_type(v, jnp.uint32) & jnp.uint32(0xFFFF0000), F32)
    return hi.astype(BF16), (v - hi).astype(BF16)


def _segsum(v, g):
    hi, lo = _split(v)
    w = g.shape[1]
    return jnp.concatenate([_dot(jnp.concatenate([hi[:, c:c + w], lo[:, c:c + w]], axis=1), g)
                            for c in range(0, v.shape[1], w)], axis=1)


def _rot_half(x):
    outs = []
    for c in range(x.shape[1] // LANES):
        xc = x[:, c * LANES:(c + 1) * LANES]
        lane = lax.broadcasted_iota(jnp.int32, xc.shape, 1)
        first = (lane % HEAD_DIM) < (HEAD_DIM // 2)
        outs.append(jnp.where(first, pltpu.roll(xc, LANES - 32, 1), pltpu.roll(xc, 32, 1)))
    return outs[0] if len(outs) == 1 else jnp.concatenate(outs, axis=1)


def _rms(x):
    return lax.rsqrt(jnp.mean(x * x, axis=-1, keepdims=True) + EPS)


def _rms_bwd(dy, x, w):
    rstd = _rms(x)
    xh = x * rstd
    dxh = dy * w
    dx = rstd * (dxh - xh * jnp.mean(dxh * xh, axis=-1, keepdims=True))
    return dx, dy * xh


def _sigmoid(x):
    return 1.0 / (1.0 + jnp.exp(-x))


def _sum4(ref):
    p = [ref[j].astype(F32) for j in range(N_CHIPS)]
    return (p[0] + p[1]) + (p[2] + p[3])


def _full(shape):
    n = len(shape)
    return pl.BlockSpec(shape, lambda *_: (0,) * n)


def _strided_spec(tm, r):
    return pl.BlockSpec((r, tm // r, GROUP), lambda i: (0, i, 0))


def _strided_shape(S, r, dtype):
    return jax.ShapeDtypeStruct((r, S // r, GROUP), dtype)


def _to_strided(scr, val, outs):
    chunks = range(GROUP // LANES)
    for k in chunks:
        scr[k] = val[:, _lanes(k)]
    for r, o_ref in outs:
        if r == 1:
            o_ref[0] = val.astype(o_ref.dtype)
            continue
        n = val.shape[0] // r
        for c in range(r):
            rows = pl.ds(c, n, stride=r)
            o_ref[c] = jnp.concatenate([scr.at[k][rows, :] for k in chunks], axis=1).astype(o_ref.dtype)


def _from_strided(scr, ref):
    r, n, _ = ref.shape
    if r == 1:
        return ref[0].astype(F32)
    chunks = range(GROUP // LANES)
    for c in range(r):
        plane = ref[c].astype(F32)
        for k in chunks:
            scr.at[k][pl.ds(c, n, stride=r), :] = plane[:, _lanes(k)]
    return jnp.concatenate([scr[k] for k in chunks], axis=1)


def _strided_scratch(tm):
    return pltpu.VMEM((GROUP // LANES, tm, LANES), F32)


def _tile4(t):
    return jnp.concatenate([t] * (GROUP // LANES), axis=1)


def _in_proj_fwd(x, attn_w, w_in_g, qw, kw, cos_t, sin_t, seg_ones, slots):
    S, D = x.shape
    tm = 512
    wcols = w_in_g.shape[2]
    nd = len(DILATIONS)
    ns = len(slots)

    def body(x_ref, aw_ref, w_ref, qw_ref, kw_ref, cos_ref, sin_ref, g_ref, *rest):
        slot_in, (h_ref, qa_ref, ka_ref), rest = rest[:ns], rest[ns:ns + 3], rest[ns + 3:]
        q_refs, k_refs, v_refs = rest[:nd], rest[nd:2 * nd], rest[2 * nd:3 * nd]
        qs_ref, ks_ref, vs_ref = rest[3 * nd:3 * nd + 3]
        slot_out, scr, sems = rest[3 * nd + 3:3 * nd + 3 + ns], rest[3 * nd + 3 + ns], rest[3 * nd + 4 + ns:]
        finish = _hosted_gathers(slot_in, slot_out, *sems, pl.program_id(0), S // tm) if ns else None
        xv = x_ref[...]
        h = (xv * _rms(xv) * aw_ref[...]).astype(BF16)
        h_ref[...] = h
        proj = jnp.concatenate([_dot(h, w_ref[j]) for j in range(N_CHIPS)], axis=1)
        qa = proj[:, 0 * GROUP:1 * GROUP]
        ka = proj[:, 1 * GROUP:2 * GROUP]
        qa_ref[...] = qa
        ka_ref[...] = ka
        _to_strided(scr, proj[:, 2 * GROUP:3 * GROUP], list(zip(DILATIONS, v_refs)))
        qs_ref[...] = proj[:, 3 * GROUP:4 * GROUP].astype(BF16)
        ks_ref[...] = proj[:, 4 * GROUP:5 * GROUP].astype(BF16)
        vs_ref[...] = proj[:, 5 * GROUP:6 * GROUP].astype(BF16)
        g = g_ref[...]
        cos = _tile4(cos_ref[...])
        sin = _tile4(sin_ref[...])
        for t, w_r, o_rs in ((qa, qw_ref, q_refs), (ka, kw_ref, k_refs)):
            rstd = lax.rsqrt(_segsum(t * t, g) * (1.0 / HEAD_DIM) + EPS)
            tn = t * rstd * w_r[...]
            _to_strided(scr, tn * cos + _rot_half(tn) * sin, list(zip(DILATIONS, o_rs)))
        if finish is not None:
            finish()

    row = lambda i: (i, 0)
    tile = lambda n, dt: jax.ShapeDtypeStruct((S, n), dt)
    planes = [_strided_spec(tm, r) for r in DILATIONS]
    h_in, h_out, h_shape, h_sems = _hosted_specs(slots)
    n_out = 6 + 3 * nd
    return pl.pallas_call(
        body, name="in_proj_fwd", grid=(S // tm,),
        in_specs=[pl.BlockSpec((tm, D), row), _full((1, D)), _full((N_CHIPS, D, wcols)),
                  _full((1, GROUP)), _full((1, GROUP)),
                  pl.BlockSpec((tm, LANES), row), pl.BlockSpec((tm, LANES), row),
                  _full((GROUP, GROUP // 2))] + h_in,
        out_specs=[pl.BlockSpec((tm, D), row)] + [pl.BlockSpec((tm, GROUP), row)] * 2 + planes * 3
                  + [pl.BlockSpec((tm, GROUP), row)] * 3 + h_out,
        out_shape=[tile(D, BF16), tile(GROUP, F32), tile(GROUP, F32)]
                  + [_strided_shape(S, r, BF16) for r in DILATIONS] * 3 + [tile(GROUP, BF16)] * 3 + h_shape,
        input_output_aliases={8 + a: n_out + a for a in range(ns)},
        scratch_shapes=[_strided_scratch(tm)] + h_sems,
    )(x, attn_w, w_in_g, qw, kw, cos_t, sin_t, seg_ones, *slots)


DIL_PLANES = 1


def _dil_fwd(q, k, v, slots):
    r, L, _ = q.shape
    nb = L // BLOCK
    P = GROUP // LANES
    PL = min(DIL_PLANES, r)
    ns = len(slots)
    units = [(pp, hp) for pp in range(PL) for hp in range(P)]

    def body(q_ref, kc_ref, kp_ref, vc_ref, vp_ref, *rest):
        o_ref, l_ref = rest[ns:ns + 2]
        n = pl.program_id(1)
        finish = (_hosted_gathers(rest[:ns], rest[ns + 2:2 * ns + 2], *rest[2 * ns + 2:],
                                  pl.program_id(0) * nb + n, (r // PL) * nb) if ns else None)
        rowi = lax.broadcasted_iota(jnp.int32, (BLOCK, BLOCK), 0)
        coli = lax.broadcasted_iota(jnp.int32, (BLOCK, BLOCK), 1)
        first = coli < HEAD_DIM
        masks = (coli <= rowi, jnp.logical_and(coli >= rowi, n > 0))
        s2 = {}
        for pp, hp in units:
            q2 = _scaled(q_ref[pp, :, _lanes(hp)])
            for b, k_ref in enumerate((kc_ref, kp_ref)):
                s2[pp, hp, b] = _dot_nt(q2, _by_head(k_ref[pp, :, _lanes(hp)], first))
        ps, inv, lse = {}, {}, {}
        for pp, hp in units:
            for h in range(2):
                s = [jnp.where(masks[b], s2[pp, hp, b][:, h * BLOCK:(h + 1) * BLOCK], NEG) for b in range(2)]
                m = jnp.maximum(jnp.max(s[0], axis=1, keepdims=True), jnp.max(s[1], axis=1, keepdims=True))
                p = [jnp.exp(s[b] - m) for b in range(2)]
                den = jnp.sum(p[0], axis=1, keepdims=True) + jnp.sum(p[1], axis=1, keepdims=True)
                ps[pp, hp, h] = [p[b].astype(BF16) for b in range(2)]
                inv[pp, hp, h] = 1.0 / den
                lse[pp, hp, h] = m + jnp.log(den)
        for pp, hp in units:
            o = jnp.zeros((BLOCK, LANES), F32)
            for b, v_ref in enumerate((vc_ref, vp_ref)):
                o = o + _dot(jnp.concatenate([ps[pp, hp, 0][b], ps[pp, hp, 1][b]], axis=1),
                             _by_head(v_ref[pp, :, _lanes(hp)], first))
            o_ref[pp, :, _lanes(hp)] = (o * jnp.where(first, inv[pp, hp, 0], inv[pp, hp, 1])).astype(BF16)
            l_ref[pp, :, _lanes(hp)] = jnp.where(first, lse[pp, hp, 0], lse[pp, hp, 1])
        if finish is not None:
            finish()

    cur = pl.BlockSpec((PL, BLOCK, GROUP), lambda c, n: (c, n, 0))
    prev = pl.BlockSpec((PL, BLOCK, GROUP), lambda c, n: (c, jnp.maximum(n - 1, 0), 0))
    h_in, h_out, h_shape, h_sems = _hosted_specs(slots)
    return pl.pallas_call(
        body, name="dil_fwd_r%d" % r, grid=(r // PL, nb),
        in_specs=[cur, cur, prev, cur, prev] + h_in, out_specs=[cur, cur] + h_out,
        out_shape=[jax.ShapeDtypeStruct(q.shape, BF16), jax.ShapeDtypeStruct(q.shape, F32)] + h_shape,
        input_output_aliases={5 + a: 2 + a for a in range(ns)},
        scratch_shapes=h_sems,
    )(q, k, k, v, v, *slots)


def _dil_bwd(q, k, v, do, lse, delta, parts, pairs):
    r, L, _ = q.shape
    nb = L // BLOCK
    P = GROUP // LANES
    PL = min(DIL_PLANES, r)
    scale = HEAD_DIM ** -0.5
    units = [(pp, hp) for pp in range(PL) for hp in range(P)]
    npt, npr = len(parts), len(pairs)
    nx = npt + npr

    def body(qc_ref, qn_ref, doc_ref, don_ref, lc_ref, ln_ref, dc_ref, dn_ref, k_ref, v_ref, *rest):
        x_in, (dq_ref, dk_ref, dv_ref), x_out = rest[:nx], rest[nx:nx + 3], rest[nx + 3:2 * nx + 3]
        carry, send, recv = rest[2 * nx + 3:]
        j = pl.program_id(1)
        step = pl.program_id(0) * nb + j
        plans = [_chip_send_plan(x_in[a], x_out[a], send.at[pl.ds(3 * a, 3)], recv.at[pl.ds(3 * a, 3)])
                 for a in range(npt)]
        plans += [_pair_send_plan(x_in[npt + a], x_out[npt + a], send.at[3 * npt + a], recv.at[3 * npt + a])
                  for a in range(npr)]

        def start():
            for begin, _ in plans:
                begin()

        def finish():
            for _, end in plans:
                end()

        pl.when(step == 0)(start)
        rowi = lax.broadcasted_iota(jnp.int32, (BLOCK, BLOCK), 0)
        coli = lax.broadcasted_iota(jnp.int32, (BLOCK, BLOCK), 1)
        first = coli < HEAD_DIM
        sides = ((qc_ref, doc_ref, lc_ref, dc_ref, coli <= rowi),
                 (qn_ref, don_ref, ln_ref, dn_ref, jnp.logical_and(coli >= rowi, j < nb - 1)))

        @pl.when(j == 0)
        def _():
            carry[...] = jnp.zeros_like(carry)

        kcat, q2, do2, s2, dp2 = {}, {}, {}, {}, {}
        for pp, hp in units:
            kcat[pp, hp] = _by_head(k_ref[pp, :, _lanes(hp)], first)
            vcat = _by_head(v_ref[pp, :, _lanes(hp)], first)
            for x, (q_r, do_r, _, _, _) in enumerate(sides):
                q2[pp, hp, x] = _scaled(q_r[pp, :, _lanes(hp)])
                do2[pp, hp, x] = do_r[pp, :, _lanes(hp)]
                s2[pp, hp, x] = _dot_nt(q2[pp, hp, x], kcat[pp, hp])
                dp2[pp, hp, x] = _dot_nt(do2[pp, hp, x], vcat)
        pcat, dscat = {}, {}
        for pp, hp in units:
            for x, (_, _, l_r, d_r, msk) in enumerate(sides):
                ps, dss = [], []
                for h in range(2):
                    col = hp * LANES + h * HEAD_DIM
                    half = slice(h * BLOCK, (h + 1) * BLOCK)
                    p = jnp.where(msk, jnp.exp(s2[pp, hp, x][:, half] - l_r[pp, :, col:col + 1]), 0.0)
                    ps.append(p.astype(BF16))
                    dss.append((p * (dp2[pp, hp, x][:, half] - d_r[pp, :, col:col + 1])).astype(BF16))
                pcat[pp, hp, x] = jnp.concatenate(ps, axis=1)
                dscat[pp, hp, x] = jnp.concatenate(dss, axis=1)
        for pp, hp in units:
            dv2 = _dot_tn(pcat[pp, hp, 0], do2[pp, hp, 0]) + _dot_tn(pcat[pp, hp, 1], do2[pp, hp, 1])
            dk2 = _dot_tn(dscat[pp, hp, 0], q2[pp, hp, 0]) + _dot_tn(dscat[pp, hp, 1], q2[pp, hp, 1])
            dv_ref[pp, :, _lanes(hp)] = jnp.where(first, dv2[:BLOCK], dv2[BLOCK:]).astype(BF16)
            dk_ref[pp, :, _lanes(hp)] = jnp.where(first, dk2[:BLOCK], dk2[BLOCK:]).astype(BF16)
            dq_ref[pp, :, _lanes(hp)] = (carry[pp, :, _lanes(hp)]
                                         + _dot(dscat[pp, hp, 0], kcat[pp, hp]) * scale).astype(BF16)
            carry[pp, :, _lanes(hp)] = _dot(dscat[pp, hp, 1], kcat[pp, hp]) * scale
        pl.when(step == (r // PL) * nb - 1)(finish)

    cur = pl.BlockSpec((PL, BLOCK, GROUP), lambda c, n: (c, n, 0))
    nxt = pl.BlockSpec((PL, BLOCK, GROUP), lambda c, n: (c, jnp.minimum(n + 1, nb - 1), 0))
    return pl.pallas_call(
        body, name="dil_bwd_r%d" % r, grid=(r // PL, nb),
        in_specs=[cur, nxt, cur, nxt, cur, nxt, cur, nxt, cur, cur] + [HBM] * nx, out_specs=[cur, cur, cur] + [HBM] * nx,
        out_shape=[jax.ShapeDtypeStruct(q.shape, BF16)] * 3 + [jax.ShapeDtypeStruct(p.shape, p.dtype) for p in parts]
                  + [jax.ShapeDtypeStruct((g.shape[0], g.shape[1] // 2, g.shape[2]), g.dtype) for g in pairs],
        scratch_shapes=[pltpu.VMEM((PL, BLOCK, GROUP), F32), pltpu.SemaphoreType.DMA((3 * npt + npr,)),
                        pltpu.SemaphoreType.DMA((3 * npt + npr,))],
    )(q, q, do, do, lse, lse, delta, delta, k, v, *parts, *pairs)


SB_TILES = 2
SB_PAIRS_FWD = 4
SB_PAIRS_BWD = 4
SB_DEAD = -110.0


def _lanes(hp):
    return slice(hp * LANES, (hp + 1) * LANES)


def _sb_logits(z, valid):
    e = jnp.exp(-jnp.abs(z))
    lb = jnp.minimum(z, 0.0) - jnp.log(1.0 + e)
    lk = lb - z
    if valid is not None:
        lk = jnp.where(valid, lk, 0.0)
    return e, lb, lk


def _by_head(t, first):
    zero = jnp.zeros_like(t)
    return jnp.concatenate([jnp.where(first, t, zero), jnp.where(first, zero, t)], axis=0)


def _sb_valid(i, j):
    rowi = lax.broadcasted_iota(jnp.int32, (BLOCK, BLOCK), 0)
    coli = lax.broadcasted_iota(jnp.int32, (BLOCK, BLOCK), 1)
    return (coli - rowi) < (i - j) * BLOCK


def _scaled(q):
    return (q.astype(F32) * (HEAD_DIM ** -0.5)).astype(BF16)


def _hosted_gathers(refs_in, refs_out, send, recv, step, steps):
    plans = [_gather_plan(refs_in[a], refs_out[a], send.at[pl.ds(6 * a, 6)], recv.at[pl.ds(6 * a, 6)])
             for a in range(len(refs_in))]
    for stage, at in ((0, 0), (1, (2 * steps) // 3)):
        @pl.when(step == at)
        def _():
            for plan in plans:
                plan[stage]()

    def finish():
        @pl.when(step == steps - 1)
        def _():
            for plan in plans:
                plan[2]()

    return finish


def _hosted_specs(slots):
    n = len(slots)
    sems = [pltpu.SemaphoreType.DMA((6 * n,))] * 2 if n else []
    return [HBM] * n, [HBM] * n, [jax.ShapeDtypeStruct(s.shape, s.dtype) for s in slots], sems


def _sb_fwd(qs, ks, vs, tri_later, slots):
    S = qs.shape[0]
    P = SB_PAIRS_FWD
    W = P * LANES
    ns = len(slots)

    def body(q_ref, k_ref, v_ref, u_ref, *rest):
        o_ref, lt_ref, from_ref = rest[ns:ns + 3]
        i = pl.program_id(1)
        finish = _hosted_gathers(rest[:ns], rest[ns + 3:2 * ns + 3], *rest[2 * ns + 3:], i, S // BLOCK) if ns else None
        first = lax.broadcasted_iota(jnp.int32, (BLOCK, LANES), 1) < HEAD_DIM
        q2 = [_scaled(q_ref[:, _lanes(hp)]) for hp in range(P)]

        def walk(tiles, carry):
            runs, accs = list(carry[0]), list(carry[1])
            units = [(t, hp) for t in range(len(tiles)) for hp in range(P)]
            offs = [pl.multiple_of(j * BLOCK, BLOCK) for j, _ in tiles]
            valids = [_sb_valid(i, j) if diag else None for j, diag in tiles]
            z2s, lbs, c2s = {}, {}, {}
            for t, hp in units:
                z2s[t, hp] = _dot_nt(q2[hp], _by_head(k_ref[pl.ds(offs[t], BLOCK), _lanes(hp)], first))
            for t, hp in units:
                for h in range(2):
                    _, lb, lk = _sb_logits(z2s[t, hp][:, h * BLOCK:(h + 1) * BLOCK], valids[t])
                    lbs[t, hp, h] = lb
                    c2s[t, hp, h] = _dot(jnp.concatenate(_split(lk), axis=1), u_ref[...])
            for t, hp in units:
                a2 = []
                for h in range(2):
                    a = jnp.exp(lbs[t, hp, h] + c2s[t, hp, h][:, :BLOCK] + runs[2 * hp + h])
                    if valids[t] is not None:
                        a = jnp.where(valids[t], a, 0.0)
                    a2.append(a.astype(BF16))
                    runs[2 * hp + h] = runs[2 * hp + h] + c2s[t, hp, h][:, BLOCK:]
                vcat = _by_head(v_ref[pl.ds(offs[t], BLOCK), _lanes(hp)], first)
                accs[hp] = accs[hp] + _dot(jnp.concatenate(a2, axis=1), vcat)
            return tuple(runs), tuple(accs)

        def chunk(ci, carry):
            return walk([(ci * SB_TILES + t, False) for t in reversed(range(SB_TILES))], carry)

        def alive(runs):
            top = functools.reduce(jnp.maximum, runs)
            return (jnp.max(top) > SB_DEAD).astype(jnp.int32)

        def step(c):
            t, _, runs, accs = c
            runs, accs = chunk(nfull - 1 - t, (runs, accs))
            return t + 1, alive(runs), runs, accs

        zero = jnp.zeros((BLOCK, LANES), F32)
        nfull = i // SB_TILES
        ragged = [functools.partial(walk, [(i, True)] + [(i - 1 - m, False) for m in range(extra)])
                  for extra in range(SB_TILES)]
        runs, accs = lax.switch(i % SB_TILES, ragged, ((zero,) * (2 * P), (zero,) * P))
        done, _, runs, accs = lax.while_loop(lambda c: jnp.logical_and(c[0] < nfull, c[1] > 0), step,
                                             (jnp.int32(0), alive(runs), runs, accs))
        for hp in range(P):
            o_ref[:, _lanes(hp)] = accs[hp]
            lt_ref[:, _lanes(hp)] = jnp.where(first, runs[2 * hp], runs[2 * hp + 1])
        from_ref[...] = jnp.full(from_ref.shape, nfull - done, jnp.int32)
        if finish is not None:
            finish()

    assert W == GROUP
    blk = pl.BlockSpec((BLOCK, W), lambda hp, i: (i, hp))
    col = pl.BlockSpec((S, W), lambda hp, i: (0, hp))
    h_in, h_out, h_shape, h_sems = _hosted_specs(slots)
    return pl.pallas_call(
        body, name="sb_fwd", grid=(GROUP // W, S // BLOCK),
        in_specs=[blk, col, col, _full((2 * BLOCK, 2 * BLOCK))] + h_in,
        out_specs=[blk, blk, pl.BlockSpec((1, 8, LANES), lambda hp, i: (i, 0, 0))] + h_out,
        out_shape=[jax.ShapeDtypeStruct((S, GROUP), F32)] * 2
                  + [jax.ShapeDtypeStruct((S // BLOCK, 8, LANES), jnp.int32)] + h_shape,
        input_output_aliases={4 + a: 3 + a for a in range(ns)},
        scratch_shapes=h_sems,
    )(qs, ks, vs, tri_later, *slots)


def _sb_bwd(first_chunk, qs, ks, vs, do, ltot, tri_upto, tri_before):
    S = qs.shape[0]
    P = SB_PAIRS_BWD
    W = P * LANES

    def body(from_ref, q_ref, k_ref, v_ref, do_ref, lt_ref, w_ref, x_ref, dq_ref, dk_ref, dv_ref):
        i = pl.program_id(1)

        @pl.when(i == 0)
        def _():
            dk_ref[...] = jnp.zeros_like(dk_ref)
            dv_ref[...] = jnp.zeros_like(dv_ref)

        first = lax.broadcasted_iota(jnp.int32, (BLOCK, LANES), 1) < HEAD_DIM
        q2 = [_scaled(q_ref[:, _lanes(hp)]) for hp in range(P)]
        do2 = [do_ref[:, _lanes(hp)] for hp in range(P)]
        totals = [jnp.broadcast_to(lt_ref[:, n * HEAD_DIM:n * HEAD_DIM + 1], (BLOCK, LANES)) for n in range(2 * P)]

        def walk(tiles, carry):
            keeps, grads, dqs = list(carry[0]), list(carry[1]), list(carry[2])
            units = [(t, hp) for t in range(len(tiles)) for hp in range(P)]
            offs = [pl.multiple_of(j * BLOCK, BLOCK) for j, _ in tiles]
            valids = [_sb_valid(i, j) if diag else None for j, diag in tiles]
            kcat, z2, da2, es, lbs, c2s, as_, des, p2s = {}, {}, {}, {}, {}, {}, {}, {}, {}
            for t, hp in units:
                kcat[t, hp] = _by_head(k_ref[pl.ds(offs[t], BLOCK), _lanes(hp)], first)
                z2[t, hp] = _dot_nt(q2[hp], kcat[t, hp])
                da2[t, hp] = _dot_nt(do2[hp], _by_head(v_ref[pl.ds(offs[t], BLOCK), _lanes(hp)], first))
            for t, hp in units:
                for h in range(2):
                    es[t, hp, h], lbs[t, hp, h], lk = _sb_logits(z2[t, hp][:, h * BLOCK:(h + 1) * BLOCK], valids[t])
                    c2s[t, hp, h] = _dot(jnp.concatenate(_split(lk), axis=1), w_ref[...])
            for t, hp in units:
                for h in range(2):
                    n = 2 * hp + h
                    a = jnp.exp(lbs[t, hp, h] + (totals[n] - (keeps[n] + c2s[t, hp, h][:, :BLOCK])))
                    if valids[t] is not None:
                        a = jnp.where(valids[t], a, 0.0)
                    keeps[n] = keeps[n] + c2s[t, hp, h][:, BLOCK:]
                    de = a * da2[t, hp][:, h * BLOCK:(h + 1) * BLOCK]
                    as_[t, hp, h], des[t, hp, h] = a.astype(BF16), de
                    p2s[t, hp, h] = _dot(jnp.concatenate(_split(de), axis=1), x_ref[...])
            for t, hp in units:
                dz2 = []
                for h in range(2):
                    n = 2 * hp + h
                    e = es[t, hp, h]
                    sig = jnp.where(z2[t, hp][:, h * BLOCK:(h + 1) * BLOCK] >= 0.0, 1.0, e) / (1.0 + e)
                    dz = des[t, hp, h] * (1.0 - sig) - (grads[n] + p2s[t, hp, h][:, :BLOCK]) * sig
                    if valids[t] is not None:
                        dz = jnp.where(valids[t], dz, 0.0)
                    grads[n] = grads[n] + p2s[t, hp, h][:, BLOCK:]
                    dz2.append(dz.astype(BF16))
                dzcat = jnp.concatenate(dz2, axis=1)
                dk2 = _dot_tn(dzcat, q2[hp])
                dv2 = _dot_tn(jnp.concatenate([as_[t, hp, 0], as_[t, hp, 1]], axis=1), do2[hp])
                dk_ref[pl.ds(offs[t], BLOCK), _lanes(hp)] += jnp.where(first, dk2[:BLOCK], dk2[BLOCK:])
                dv_ref[pl.ds(offs[t], BLOCK), _lanes(hp)] += jnp.where(first, dv2[:BLOCK], dv2[BLOCK:])
                dqs[hp] = dqs[hp] + _dot(dzcat, kcat[t, hp])
            return tuple(keeps), tuple(grads), tuple(dqs)

        zero = jnp.zeros((BLOCK, LANES), F32)
        nfull = i // SB_TILES
        carry = lax.fori_loop(
            from_ref[i], nfull, lambda ci, c: walk([(ci * SB_TILES + t, False) for t in range(SB_TILES)], c),
            ((zero,) * (2 * P), (zero,) * (2 * P), (zero,) * P))
        ragged = [functools.partial(walk, [(i - m, False) for m in range(extra, 0, -1)] + [(i, True)])
                  for extra in range(SB_TILES)]
        carry = lax.switch(i % SB_TILES, ragged, carry)
        for hp in range(P):
            dq_ref[:, _lanes(hp)] = carry[2][hp] * (HEAD_DIM ** -0.5)

    blk = pl.BlockSpec((BLOCK, W), lambda hp, i, fr: (i, hp))
    col = pl.BlockSpec((S, W), lambda hp, i, fr: (0, hp))
    tri = pl.BlockSpec((2 * BLOCK, 2 * BLOCK), lambda hp, i, fr: (0, 0))
    return pl.pallas_call(
        body, name="sb_bwd",
        grid_spec=pltpu.PrefetchScalarGridSpec(
            num_scalar_prefetch=1, grid=(GROUP // W, S // BLOCK),
            in_specs=[blk, col, col, blk, blk, tri, tri], out_specs=[blk, col, col]),
        out_shape=[jax.ShapeDtypeStruct((S, GROUP), F32)] * 3,
    )(first_chunk, qs, ks, vs, do, ltot, tri_upto, tri_before)


def _out_proj_fwd(o_br, l_br, o_sb, x, w_dil, w_sbn, w_out_g, slots):
    S, D = x.shape
    tm = 512
    ns = len(slots)

    def body(o0, o1, o2, l0, l1, l2, os_ref, x_ref, wd_ref, ws_ref, w_ref, *rest):
        od_ref, s0, s1, s2, x1_ref = rest[ns:ns + 5]
        scr = rest[2 * ns + 5]
        finish = (_hosted_gathers(rest[:ns], rest[ns + 5:2 * ns + 5], *rest[2 * ns + 6:], pl.program_id(0), S // tm)
                  if ns else None)
        ls = [_from_strided(scr, l) for l in (l0, l1, l2)]
        os_ = [_from_strided(scr, o) for o in (o0, o1, o2)]
        m = jnp.maximum(jnp.maximum(ls[0], ls[1]), ls[2])
        es = [jnp.exp(l - m) for l in ls]
        den = es[0] + es[1] + es[2]
        od = (es[0] * os_[0] + es[1] * os_[1] + es[2] * os_[2]) / den
        od_ref[...] = od
        _to_strided(scr, m + jnp.log(den), list(zip(DILATIONS, (s0, s1, s2))))
        osb = os_ref[...]
        mixed = jnp.concatenate([(od * _rms(od) * wd_ref[...]).astype(BF16),
                                 (osb * _rms(osb) * ws_ref[...]).astype(BF16)], axis=1)
        x1_ref[...] = x_ref[...] + _dot(mixed, w_ref[...])
        if finish is not None:
            finish()

    row = lambda i: (i, 0)
    g = pl.BlockSpec((tm, GROUP), row)
    d = pl.BlockSpec((tm, D), row)
    planes = [_strided_spec(tm, r) for r in DILATIONS]
    h_in, h_out, h_shape, h_sems = _hosted_specs(slots)
    return pl.pallas_call(
        body, name="out_proj_fwd", grid=(S // tm,),
        in_specs=planes * 2 + [g, d, _full((1, GROUP)), _full((1, GROUP)), _full((2 * GROUP, D))] + h_in,
        out_specs=[g] + planes + [d] + h_out,
        out_shape=[jax.ShapeDtypeStruct((S, GROUP), F32)] + [_strided_shape(S, r, F32) for r in DILATIONS]
                  + [jax.ShapeDtypeStruct((S, D), F32)] + h_shape,
        input_output_aliases={11 + a: 5 + a for a in range(ns)},
        scratch_shapes=[_strided_scratch(tm)] + h_sems,
    )(*o_br, *l_br, o_sb, x, w_dil, w_sbn, w_out_g, *slots)


def _ffn_fwd(x1, target, ffn_w, wg_g, wu_g, wd_g):
    S, D = x1.shape
    F = wg_g.shape[1]
    tm = 512
    nt = S // tm

    def body(x_ref, t_ref, nw_ref, wg_ref, wu_ref, wd_ref, h_ref, g_ref, u_ref, dy_ref, loss_ref, h_s, acc):
        j = pl.program_id(1)

        @pl.when(j == 0)
        def _():
            xv = x_ref[...]
            h = (xv * _rms(xv) * nw_ref[...]).astype(BF16)
            h_s[...] = h
            h_ref[...] = h
            acc[...] = xv

        h = h_s[...]
        g = _dot_nt(h, wg_ref[0])
        u = _dot_nt(h, wu_ref[0])
        g_ref[0] = g.astype(BF16)
        u_ref[0] = u.astype(BF16)
        a = (g * _sigmoid(g) * u).astype(BF16)
        acc[...] += _dot(a, wd_ref[0])

        @pl.when(j == N_CHIPS - 1)
        def _():
            err = acc[...] - t_ref[...]
            dy_ref[...] = err * (1.0 / D)
            loss_ref[...] = jnp.full(loss_ref.shape, jnp.sum(err * err), F32)

    row = lambda t, j: (t, 0)
    shard = lambda t, j: (j, 0, 0)
    act = lambda t, j: (j, t, 0)
    return pl.pallas_call(
        body, name="ffn_fwd", grid=(nt, N_CHIPS),
        in_specs=[pl.BlockSpec((tm, D), row), pl.BlockSpec((tm, D), row), pl.BlockSpec((1, D), lambda t, j: (0, 0))]
                 + [pl.BlockSpec((1, F, D), shard)] * 3,
        out_specs=[pl.BlockSpec((tm, D), row), pl.BlockSpec((1, tm, F), act), pl.BlockSpec((1, tm, F), act),
                   pl.BlockSpec((tm, D), row), pl.BlockSpec((1, 8, LANES), lambda t, j: (t, 0, 0))],
        out_shape=[jax.ShapeDtypeStruct((S, D), BF16), jax.ShapeDtypeStruct((N_CHIPS, S, F), BF16),
                   jax.ShapeDtypeStruct((N_CHIPS, S, F), BF16), jax.ShapeDtypeStruct((S, D), F32),
                   jax.ShapeDtypeStruct((nt, 8, LANES), F32)],
        scratch_shapes=[pltpu.VMEM((tm, D), BF16), pltpu.VMEM((tm, D), F32)],
    )(x1, target, ffn_w, wg_g, wu_g, wd_g)


def _ffn_bwd(h2, dy, g, u, wg_g, wu_g, wd_g):
    S, D = dy.shape
    F = wg_g.shape[1]
    tm = 512

    def body(h_ref, dy_ref, g_ref, u_ref, wg_ref, wu_ref, wd_ref, dwg_ref, dwu_ref, dwd_ref, dh_ref, *narrow):
        t = pl.program_id(1)

        @pl.when(t == 0)
        def _():
            dwg_ref[...] = jnp.zeros_like(dwg_ref)
            dwu_ref[...] = jnp.zeros_like(dwu_ref)
            dwd_ref[...] = jnp.zeros_like(dwd_ref)

        h = h_ref[...]
        dyb = dy_ref[...].astype(BF16)
        gv = g_ref[0].astype(F32)
        uv = u_ref[0].astype(F32)
        da = _dot_nt(dyb, wd_ref[0])
        sg = _sigmoid(gv)
        silu = gv * sg
        du = (da * silu).astype(BF16)
        dg = (da * uv * (sg * (1.0 + gv * (1.0 - sg)))).astype(BF16)
        dwd_ref[0] += _dot_tn((silu * uv).astype(BF16), dyb)
        dwg_ref[0] += _dot_tn(dg, h)
        dwu_ref[0] += _dot_tn(du, h)
        dh_ref[0] = (_dot(dg, wg_ref[0]) + _dot(du, wu_ref[0])).astype(BF16)

        @pl.when(t == S // tm - 1)
        def _():
            for full, half in zip((dwg_ref, dwu_ref, dwd_ref), narrow):
                half[...] = full[...].astype(BF16)

    row = lambda j, t: (t, 0)
    shard = lambda j, t: (j, 0, 0)
    act = lambda j, t: (j, t, 0)
    return pl.pallas_call(
        body, name="ffn_bwd", grid=(N_CHIPS, S // tm),
        in_specs=[pl.BlockSpec((tm, D), row), pl.BlockSpec((tm, D), row),
                  pl.BlockSpec((1, tm, F), act), pl.BlockSpec((1, tm, F), act)] + [pl.BlockSpec((1, F, D), shard)] * 3,
        out_specs=[pl.BlockSpec((1, F, D), shard)] * 3 + [pl.BlockSpec((1, tm, D), act)]
                  + [pl.BlockSpec((1, F, D), shard)] * 3,
        out_shape=[jax.ShapeDtypeStruct((N_CHIPS, F, D), F32)] * 3 + [jax.ShapeDtypeStruct((N_CHIPS, S, D), BF16)]
                  + [jax.ShapeDtypeStruct((N_CHIPS, F, D), BF16)] * 3,
    )(h2, dy, g, u, wg_g, wu_g, wd_g)


def _out_proj_bwd(dh2p, dy, x1, ffn_w, w_out_g, o_dil, o_sb, w_dil, w_sbn, seg_ones, ffn_grads):
    S, D = dy.shape
    tm = 512
    ng = len(ffn_grads)

    def body(dh_ref, dy_ref, x1_ref, nw_ref, w_ref, od_ref, os_ref, wd_ref, ws_ref, g_ref, *rest):
        gin, rest = rest[:ng], rest[ng:]
        dx1_ref, dod0, dod1, dod2, dos_ref, dl0, dl1, dl2, dw_ref, dnw_ref, dwd_ref, dws_ref = rest[:12]
        gout, (scr, send, recv) = rest[12:12 + ng], rest[12 + ng:]
        i = pl.program_id(0)
        plans = [_pair_send_plan(gin[a], gout[a], send.at[a], recv.at[a]) for a in range(ng)]

        @pl.when(i == 0)
        def _():
            for start, _ in plans:
                start()

        @pl.when(i == 0)
        def _():
            for r_ in (dw_ref, dnw_ref, dwd_ref, dws_ref):
                r_[...] = jnp.zeros_like(r_)

        dh2 = _sum4(dh_ref)
        dxn, dwn = _rms_bwd(dh2, x1_ref[...], nw_ref[...])
        dnw_ref[...] += jnp.sum(dwn, axis=0, keepdims=True)
        dx1 = dy_ref[...] + dxn
        dx1_ref[...] = dx1
        dx1b = dx1.astype(BF16)
        dmix = _dot_nt(dx1b, w_ref[...])
        od = od_ref[...]
        osb = os_ref[...]
        mixed = jnp.concatenate([(od * _rms(od) * wd_ref[...]).astype(BF16),
                                 (osb * _rms(osb) * ws_ref[...]).astype(BF16)], axis=1)
        dw_ref[...] += _dot_tn(mixed, dx1b)
        do, dwo = _rms_bwd(dmix[:, :GROUP], od, wd_ref[...])
        dwd_ref[...] += jnp.sum(dwo, axis=0, keepdims=True)
        _to_strided(scr, do, list(zip(DILATIONS, (dod0, dod1, dod2))))
        _to_strided(scr, _segsum(do * od, g_ref[...]), list(zip(DILATIONS, (dl0, dl1, dl2))))
        do, dwo = _rms_bwd(dmix[:, GROUP:], osb, ws_ref[...])
        dws_ref[...] += jnp.sum(dwo, axis=0, keepdims=True)
        dos_ref[...] = do.astype(BF16)

        @pl.when(i == S // tm - 1)
        def _():
            for _, finish in plans:
                finish()

    row = lambda i: (i, 0)
    gsp = pl.BlockSpec((tm, GROUP), row)
    dsp = pl.BlockSpec((tm, D), row)
    planes = [_strided_spec(tm, r) for r in DILATIONS]
    halves = [jax.ShapeDtypeStruct((g.shape[0], g.shape[1] // 2, g.shape[2]), g.dtype) for g in ffn_grads]
    return pl.pallas_call(
        body, name="out_proj_bwd", grid=(S // tm,),
        in_specs=[pl.BlockSpec((N_CHIPS, tm, D), lambda i: (0, i, 0)), dsp, dsp, _full((1, D)), _full((2 * GROUP, D)),
                  gsp, gsp, _full((1, GROUP)), _full((1, GROUP)), _full((GROUP, GROUP // 2))] + [HBM] * ng,
        out_specs=[dsp] + planes + [gsp] + planes
                  + [_full((2 * GROUP, D)), _full((1, D)), _full((1, GROUP)), _full((1, GROUP))] + [HBM] * ng,
        out_shape=[jax.ShapeDtypeStruct((S, D), F32)] + [_strided_shape(S, r, BF16) for r in DILATIONS]
                  + [jax.ShapeDtypeStruct((S, GROUP), BF16)] + [_strided_shape(S, r, F32) for r in DILATIONS]
                  + [jax.ShapeDtypeStruct((2 * GROUP, D), F32),
                     jax.ShapeDtypeStruct((1, D), F32), jax.ShapeDtypeStruct((1, GROUP), F32),
                     jax.ShapeDtypeStruct((1, GROUP), F32)] + halves,
        scratch_shapes=[_strided_scratch(tm), pltpu.SemaphoreType.DMA((ng,)), pltpu.SemaphoreType.DMA((ng,))],
    )(dh2p, dy, x1, ffn_w, w_out_g, o_dil, o_sb, w_dil, w_sbn, seg_ones, *ffn_grads)


def _attn_in_bwd(dq_br, dk_br, dv_br, dqs, dks, dvs, qa, ka, qw, kw, cos_t, sin_t, seg_ones, h, w_in_g, x, dx1, attn_w):
    S, D = x.shape
    wc = w_in_g.shape[2]
    tm = 256

    def body(q0, q1, q2, k0, k1, k2, v0, v1, v2, dqs_ref, dks_ref, dvs_ref, qa_ref, ka_ref, qw_ref, kw_ref,
             cos_ref, sin_ref, g_ref, h_ref, w_ref, x_ref, dx1_ref, aw_ref,
             gx_ref, dw_ref, daw_ref, dqw_ref, dkw_ref, accq, acck, scr):
        i = pl.program_id(0)

        @pl.when(i == 0)
        def _():
            accq[...] = jnp.zeros_like(accq)
            acck[...] = jnp.zeros_like(acck)
            dw_ref[...] = jnp.zeros_like(dw_ref)
            daw_ref[...] = jnp.zeros_like(daw_ref)

        def branches(refs):
            return (_from_strided(scr, refs[0]) + _from_strided(scr, refs[1])) + _from_strided(scr, refs[2])

        g = g_ref[...]
        cos = _tile4(cos_ref[...])
        sin = _tile4(sin_ref[...])
        pieces = []
        for refs, pre_ref, w_r, acc in (((q0, q1, q2), qa_ref, qw_ref, accq), ((k0, k1, k2), ka_ref, kw_ref, acck)):
            dh = branches(refs)
            dn = dh * cos + _rot_half(dh * sin)
            pre = pre_ref[...]
            rstd = lax.rsqrt(_segsum(pre * pre, g) * (1.0 / HEAD_DIM) + EPS)
            xh = pre * rstd
            acc[...] += jnp.sum(dn * xh, axis=0, keepdims=True)
            dxh = dn * w_r[...]
            pieces.append((rstd * (dxh - xh * (_segsum(dxh * xh, g) * (1.0 / HEAD_DIM)))).astype(BF16))
        pieces += [branches((v0, v1, v2)).astype(BF16), dqs_ref[...].astype(BF16), dks_ref[...].astype(BF16),
                   dvs_ref[...].astype(BF16)]
        dproj = jnp.concatenate(pieces, axis=1)
        hv = h_ref[...]
        dh = jnp.zeros((tm, D), F32)
        for j in range(N_CHIPS):
            dp = dproj[:, j * wc:(j + 1) * wc]
            dw_ref[j] += _dot_tn(hv, dp)
            dh = dh + _dot_nt(dp, w_ref[j])
        dx, dw = _rms_bwd(dh, x_ref[...], aw_ref[...])
        daw_ref[...] += jnp.sum(dw, axis=0, keepdims=True)
        gx_ref[...] = dx1_ref[...] + dx

        @pl.when(i == S // tm - 1)
        def _():
            for acc, o_ref in ((accq, dqw_ref), (acck, dkw_ref)):
                a = acc[...]
                pair = (a[:, 0:LANES] + a[:, LANES:2 * LANES]) + (a[:, 2 * LANES:3 * LANES] + a[:, 3 * LANES:4 * LANES])
                o_ref[...] = pair + pltpu.roll(pair, HEAD_DIM, 1)

    row = lambda i: (i, 0)
    gsp = pl.BlockSpec((tm, GROUP), row)
    dsp = pl.BlockSpec((tm, D), row)
    tab = pl.BlockSpec((tm, LANES), row)
    planes = [_strided_spec(tm, r) for r in DILATIONS]
    return pl.pallas_call(
        body, name="attn_in_bwd", grid=(S // tm,),
        in_specs=planes * 3 + [gsp] * 5 + [_full((1, GROUP)), _full((1, GROUP)), tab, tab, _full((GROUP, GROUP // 2)),
                                          dsp, _full((N_CHIPS, D, wc)), dsp, dsp, _full((1, D))],
        out_specs=[dsp, _full((N_CHIPS, D, wc)), _full((1, D)), _full((1, LANES)), _full((1, LANES))],
        out_shape=[jax.ShapeDtypeStruct((S, D), F32), jax.ShapeDtypeStruct((N_CHIPS, D, wc), F32),
                   jax.ShapeDtypeStruct((1, D), F32), jax.ShapeDtypeStruct((1, LANES), F32),
                   jax.ShapeDtypeStruct((1, LANES), F32)],
        scratch_shapes=[pltpu.VMEM((1, GROUP), F32), pltpu.VMEM((1, GROUP), F32), _strided_scratch(tm)],
    )(*dq_br, *dk_br, *dv_br, dqs, dks, dvs, qa, ka, qw, kw, cos_t, sin_t, seg_ones, h, w_in_g, x, dx1, attn_w)


def _constants(S):
    pos = jnp.arange(S, dtype=F32)
    inv_freq = ROPE_THETA ** (-jnp.arange(0, HEAD_DIM, 2, dtype=F32) / HEAD_DIM)
    ang_a = pos[::BLOCK, None] * inv_freq[None, :]
    ang_b = pos[:BLOCK, None] * inv_freq[None, :]
    ca, sa, cb, sb = jnp.cos(ang_a)[:, None], jnp.sin(ang_a)[:, None], jnp.cos(ang_b)[None], jnp.sin(ang_b)[None]
    cos = (ca * cb - sa * sb).reshape(S, HEAD_DIM // 2)
    sin = (sa * cb + ca * sb).reshape(S, HEAD_DIM // 2)
    cos_t = jnp.concatenate([cos, cos] * 2, axis=1)
    sin_t = jnp.concatenate([-sin, sin] * 2, axis=1)
    idx = jnp.arange(GROUP // 2)
    seg_ones = (idx[:, None] // HEAD_DIM == idx[None, :] // HEAD_DIM).astype(BF16)
    seg_ones = jnp.concatenate([seg_ones, seg_ones], axis=0)
    r = jnp.arange(BLOCK)
    ones = jnp.ones((BLOCK, BLOCK), BF16)
    tris = [jnp.concatenate([jnp.concatenate([m.astype(BF16), ones], axis=1)] * 2, axis=0) for m in
            (r[:, None] > r[None, :],
             r[:, None] <= r[None, :],
             r[:, None] < r[None, :])]
    return cos_t, sin_t, seg_ones, tris


FFN_NAMES = ("w_gate", "w_up", "w_down")


def _device_step(x, target, attn_w, qn_w, kn_w, dil_w, sbn_w, ffn_w, w_in_g, w_out_slots, ffn_slots, core, chip):
    S = x.shape[0]
    cos_t, sin_t, seg_ones, (tri_later, tri_upto, tri_before) = _constants(S)
    reps = GROUP // HEAD_DIM
    qw = jnp.tile(qn_w, (1, reps))
    kw = jnp.tile(kn_w, (1, reps))

    nd = len(DILATIONS)
    h, qa, ka, *rest, wg_g, w_out_g = _in_proj_fwd(x, attn_w, w_in_g, qw, kw, cos_t, sin_t, seg_ones,
                                                   [ffn_slots[0], w_out_slots])
    w_out_g = w_out_g.reshape(-1, x.shape[1])
    qh, kh, va, (qs, ks, vs) = rest[:nd], rest[nd:2 * nd], rest[2 * nd:3 * nd], rest[3 * nd:]
    branches = [_dil_fwd(qh[b], kh[b], va[b], []) for b in range(nd)]
    o_sb, ltot, walked, wu_g, wd_g = _sb_fwd(qs, ks, vs, tri_later, ffn_slots[1:])
    o_dil, *lse, x1 = _out_proj_fwd([b[0] for b in branches], [b[1] for b in branches], o_sb, x, dil_w, sbn_w,
                                    w_out_g, [])
    h2, g, u, dy, loss_parts = _ffn_fwd(x1, target, ffn_w, wg_g, wu_g, wd_g)

    *ffn_grads, dh2p, n0, n1, n2 = _ffn_bwd(h2, dy, g, u, wg_g, wu_g, wd_g)
    dx1, *mid, dw_out, dffn_w, ddil_w, dsbn_w, p0, p1, p2 = _out_proj_bwd(
        dh2p, dy, x1, ffn_w, w_out_g, o_dil, o_sb, dil_w, sbn_w, seg_ones, [n0, n1, n2])
    do_dil, do_sb, delta = mid[:nd], mid[nd], mid[nd + 1:]
    parts = [_pair_sum(gr, fr, core, n) for gr, fr, n in zip(ffn_grads, (p0, p1, p2), FFN_NAMES)]
    dqs, dks, dvs = _sb_bwd(walked[:, 0, 0], qs, ks, vs, do_sb, ltot, tri_upto, tri_before)
    dw_out = dw_out.reshape(N_CHIPS, -1, x.shape[1])
    dbr = [None] * nd
    dbr[0] = _dil_bwd(qh[0], kh[0], va[0], do_dil[0], lse[0], delta[0], [parts[0]], [dw_out])
    out_part = _pair_sum(dw_out, dbr[0][4], core, "w_out")
    dbr[1] = _dil_bwd(qh[1], kh[1], va[1], do_dil[1], lse[1], delta[1], [parts[1], out_part], [])
    dbr[2] = _dil_bwd(qh[2], kh[2], va[2], do_dil[2], lse[2], delta[2], [parts[2]], [])
    ffn_halves = [_chip_sum(dbr[b][3], parts[b], chip, FFN_NAMES[b]) for b in range(nd)]
    w_out_half = _chip_sum(dbr[1][4], out_part, chip, "w_out")
    grad_x, dw_in, dattn_w, dqw, dkw = _attn_in_bwd(
        [b[0] for b in dbr], [b[1] for b in dbr], [b[2] for b in dbr], dqs, dks, dvs,
        qa, ka, qw, kw, cos_t, sin_t, seg_ones, h, w_in_g, x, dx1, attn_w)
    small = dict(attn=dattn_w, q=dqw[:, :HEAD_DIM], k=dkw[:, :HEAD_DIM], dil=ddil_w, sb=dsbn_w, ffn=dffn_w)
    return loss_parts, grad_x, small, dw_in, w_out_half, ffn_halves


HBM = pl.BlockSpec(memory_space=pltpu.HBM)
VMEM = pl.BlockSpec(memory_space=pltpu.VMEM)
CHIP_FLIPS = ((1, 0), (0, 1), (1, 1))


def _place():
    return lax.axis_index("x"), lax.axis_index("y"), lax.axis_index("c")


def _flip(v, d):
    return 1 - v if d else v


def _half_rows(c, n):
    return pl.ds(pl.multiple_of(c * (n // 2), 16), n // 2)


def _gather_plan(slot_in, slot_out, send, recv):
    x, y, c = _place()
    p = 2 * x + y
    chips = [(_flip(x, dx), _flip(y, dy)) for dx, dy in CHIP_FLIPS]
    mine, other = _half_rows(c, slot_in.shape[1]), _half_rows(1 - c, slot_in.shape[1])

    def copy(k, src, dst, to):
        return pltpu.make_async_remote_copy(src_ref=src, dst_ref=dst, send_sem=send.at[k], recv_sem=recv.at[k],
                                            device_id=to, device_id_type=MESH)

    def first(k):
        return copy(k, slot_in.at[p, mine], slot_out.at[p, mine], (*chips[k], c))

    def passed(k, rows):
        land = slot_out.at[2 * chips[k][0] + chips[k][1], rows]
        return copy(3 + k, land, land, (x, y, 1 - c))

    def start():
        for k in range(3):
            first(k).start()

    def forward():
        for k in range(3):
            land = slot_out.at[2 * chips[k][0] + chips[k][1], mine]
            copy(k, land, land, (*chips[k], c)).wait_recv()
            passed(k, mine).start()

    def finish():
        for k in range(3):
            passed(k, other).wait_recv()
        for k in range(3):
            first(k).wait_send()
            passed(k, mine).wait_send()

    return start, forward, finish


def _chip_send_plan(part_in, recv_out, send, recv):
    x, y, c = _place()
    p = 2 * x + y
    chips = [(_flip(x, dx), _flip(y, dy)) for dx, dy in CHIP_FLIPS]

    def copy(k):
        q = 2 * chips[k][0] + chips[k][1]
        return pltpu.make_async_remote_copy(src_ref=part_in.at[q], dst_ref=recv_out.at[p], send_sem=send.at[k],
                                            recv_sem=recv.at[k], device_id=(*chips[k], c), device_id_type=MESH)

    def start():
        for k in range(3):
            copy(k).start()

    def finish():
        for k in range(3):
            land = recv_out.at[2 * chips[k][0] + chips[k][1]]
            pltpu.make_async_remote_copy(src_ref=land, dst_ref=land, send_sem=send.at[k], recv_sem=recv.at[k],
                                         device_id=(*chips[k], c), device_id_type=MESH).wait_recv()
        for k in range(3):
            copy(k).wait_send()

    return start, finish


def _pair_send_plan(grad_in, recv_out, send, recv):
    x, y, c = _place()

    def copy():
        theirs = _half_rows(1 - c, grad_in.shape[1])
        return pltpu.make_async_remote_copy(src_ref=grad_in.at[:, theirs, :], dst_ref=recv_out, send_sem=send,
                                            recv_sem=recv, device_id=(x, y, 1 - c), device_id_type=MESH)

    return (lambda: copy().start()), (lambda: copy().wait())


def _own_slots(shard):
    here = 2 * lax.axis_index("x") + lax.axis_index("y")
    return lax.dynamic_update_slice(lax.empty((N_CHIPS,) + shard.shape, shard.dtype), shard[None], (here, 0, 0))


def _gather_weights(shards):
    n = len(shards)

    def body(*refs):
        ins, outs = refs[:n], refs[n:2 * n]
        send, recv = refs[2 * n:]
        plans = [_gather_plan(ins[a], outs[a], send.at[pl.ds(6 * a, 6)], recv.at[pl.ds(6 * a, 6)]) for a in range(n)]
        for stage in range(3):
            for plan in plans:
                plan[stage]()

    slots = [_own_slots(s) for s in shards]
    return pl.pallas_call(
        body, name="gather_weights", in_specs=[HBM] * n, out_specs=[HBM] * n,
        out_shape=[jax.ShapeDtypeStruct(s.shape, s.dtype) for s in slots],
        input_output_aliases={a: a for a in range(n)},
        scratch_shapes=[pltpu.SemaphoreType.DMA((6 * n,)), pltpu.SemaphoreType.DMA((6 * n,))],
    )(*slots)


def _pair_exchange(grads, small):
    n = len(grads)

    def body(*refs):
        gin, sm = refs[:n], refs[n]
        gout, sm_all = refs[n + 1:2 * n + 1], refs[2 * n + 1]
        send, recv = refs[2 * n + 2:]
        x, y, c = _place()
        me = 4 * x + 2 * y + c
        big = [_pair_send_plan(gin[a], gout[a], send.at[a], recv.at[a]) for a in range(n)]
        for start, _ in big:
            start()
        sm_all[pl.ds(me, 1)] = sm[...][None]
        tiny = []
        for k in range(1, N_DEV):
            px, py, pc = _flip(x, k & 4), _flip(y, k & 2), _flip(c, k & 1)
            tiny.append((pltpu.make_async_remote_copy(
                src_ref=sm, dst_ref=sm_all.at[me], send_sem=send.at[n + k - 1], recv_sem=recv.at[n + k - 1],
                device_id=(px, py, pc), device_id_type=MESH), 4 * px + 2 * py + pc))
            tiny[-1][0].start()
        for k, (cp, peer) in enumerate(tiny):
            pltpu.make_async_remote_copy(src_ref=sm, dst_ref=sm_all.at[peer], send_sem=send.at[n + k],
                                         recv_sem=recv.at[n + k], device_id=(x, y, c),
                                         device_id_type=MESH).wait_recv()
            cp.wait_send()
        for _, finish in big:
            finish()

    halves = [jax.ShapeDtypeStruct((g.shape[0], g.shape[1] // 2, g.shape[2]), g.dtype) for g in grads]
    return pl.pallas_call(
        body, name="pair_exchange", in_specs=[HBM] * n + [VMEM], out_specs=[HBM] * n + [VMEM],
        out_shape=halves + [jax.ShapeDtypeStruct((N_DEV,) + small.shape, small.dtype)],
        scratch_shapes=[pltpu.SemaphoreType.DMA((n + N_DEV - 1,)), pltpu.SemaphoreType.DMA((n + N_DEV - 1,))],
    )(*grads, small)


def _chip_exchange(parts):
    n = len(parts)

    def body(*refs):
        pin, pout = refs[:n], refs[n:2 * n]
        send, recv = refs[2 * n:]
        plans = [_chip_send_plan(pin[a], pout[a], send.at[pl.ds(3 * a, 3)], recv.at[pl.ds(3 * a, 3)]) for a in range(n)]
        for stage in range(2):
            for plan in plans:
                plan[stage]()

    return pl.pallas_call(
        body, name="chip_exchange", in_specs=[HBM] * n, out_specs=[HBM] * n,
        out_shape=[jax.ShapeDtypeStruct(s.shape, s.dtype) for s in parts],
        scratch_shapes=[pltpu.SemaphoreType.DMA((3 * n,)), pltpu.SemaphoreType.DMA((3 * n,))],
    )(*parts)


def _pair_swap_plan(hin, hout, send, recv):
    x, y, c = _place()

    def copies():
        return [pltpu.make_async_remote_copy(src_ref=hin[a], dst_ref=hout[a], send_sem=send.at[a], recv_sem=recv.at[a],
                                             device_id=(x, y, 1 - c), device_id_type=MESH) for a in range(len(hin))]

    def start():
        for cp in copies():
            cp.start()

    def finish():
        for cp in copies():
            cp.wait()

    return start, finish


def _pair_swap(halves):
    n = len(halves)

    def body(*refs):
        start, finish = _pair_swap_plan(refs[:n], refs[n:2 * n], *refs[2 * n:])
        start()
        finish()

    return pl.pallas_call(
        body, name="pair_swap", in_specs=[HBM] * n, out_specs=[HBM] * n,
        out_shape=[jax.ShapeDtypeStruct(s.shape, s.dtype) for s in halves],
        scratch_shapes=[pltpu.SemaphoreType.DMA((n,)), pltpu.SemaphoreType.DMA((n,))],
    )(*halves)


def _pair_sum(grad, recv, c, tag):
    _, R, C = grad.shape
    hr = R // 2

    def body(c_ref, a_ref, b_ref, o_ref):
        o_ref[...] = (a_ref[...] + b_ref[...]).astype(BF16)

    return pl.pallas_call(
        body, name="pair_sum_" + tag,
        grid_spec=pltpu.PrefetchScalarGridSpec(
            num_scalar_prefetch=1, grid=(N_CHIPS // 2,),
            in_specs=[pl.BlockSpec((2, hr, C), lambda s, cr: (s, cr[0], 0)),
                      pl.BlockSpec((2, hr, C), lambda s, cr: (s, 0, 0))],
            out_specs=pl.BlockSpec((2, hr, C), lambda s, cr: (s, 0, 0))),
        out_shape=jax.ShapeDtypeStruct((N_CHIPS, hr, C), BF16),
    )(c, grad, recv)


def _chip_sum(received, own, chip, tag):
    _, rows, C = received.shape
    tr = rows // 2

    def body(chip_ref, own_ref, r1_ref, r2_ref, r3_ref, o_ref):
        p = [r[0].astype(F32) for r in (own_ref, r1_ref, r2_ref, r3_ref)]
        o_ref[...] = (p[0] + p[1]) + (p[2] + p[3])

    def slot(k):
        return pl.BlockSpec((1, tr, C), lambda i, cr: (jnp.bitwise_xor(cr[0], k), i, 0))

    return pl.pallas_call(
        body, name="chip_sum_" + tag,
        grid_spec=pltpu.PrefetchScalarGridSpec(
            num_scalar_prefetch=1, grid=(rows // tr,), in_specs=[slot(0), slot(1), slot(2), slot(3)],
            out_specs=pl.BlockSpec((tr, C), lambda i, cr: (i, 0))),
        out_shape=jax.ShapeDtypeStruct((rows, C), F32),
    )(chip, own, received, received, received)


def _adamw_math(w, g, m, v):
    m = ADAM_B1 * m + (1.0 - ADAM_B1) * g
    v = ADAM_B2 * v + (1.0 - ADAM_B2) * (g * g)
    m_hat = m / (1.0 - ADAM_B1 ** ADAM_STEP)
    v_hat = v / (1.0 - ADAM_B2 ** ADAM_STEP)
    delta = -ADAM_LR * (m_hat / (jnp.sqrt(v_hat) + ADAM_EPS) + ADAM_WD * w)
    return delta, m, v


def _adamw(w, g_mine, g_other, m, v, c, tag):
    R, C = w.shape

    def body(c_ref, w_ref, gm_ref, go_ref, m_ref, v_ref, g_ref, d_ref, nm_ref, nv_ref):
        g = jnp.where(pl.program_id(0) == c_ref[0], gm_ref[...], go_ref[...])
        g_ref[...] = g
        d_ref[...], nm_ref[...], nv_ref[...] = _adamw_math(w_ref[...], g, m_ref[...], v_ref[...])

    blk = pl.BlockSpec((R // 2, C), lambda h, cr: (h, 0))
    half = pl.BlockSpec((R // 2, C), lambda h, cr: (0, 0))
    return pl.pallas_call(
        body, name="adamw_" + tag,
        grid_spec=pltpu.PrefetchScalarGridSpec(
            num_scalar_prefetch=1, grid=(2,), in_specs=[blk, half, half, blk, blk], out_specs=[blk] * 4),
        out_shape=[jax.ShapeDtypeStruct((R, C), F32)] * 4,
    )(c, w, g_mine, g_other, m, v)


def _small_update(all_small, w, m, v):
    def body(a_ref, w_ref, m_ref, v_ref, g_ref, d_ref, nm_ref, nv_ref):
        g = ((a_ref[0] + a_ref[1]) + (a_ref[2] + a_ref[3])) + ((a_ref[4] + a_ref[5]) + (a_ref[6] + a_ref[7]))
        g_ref[...] = g
        d_ref[...], nm_ref[...], nv_ref[...] = _adamw_math(w_ref[...], g, m_ref[...], v_ref[...])

    return pl.pallas_call(
        body, name="small_update", out_shape=[jax.ShapeDtypeStruct(w.shape, F32)] * 4,
    )(all_small, w, m, v)


SMALL_ROWS = (("attn", 0, 0), ("ffn", 1, 0), ("dil", 2, 0), ("sb", 2, GROUP), ("q", 3, 0), ("k", 3, HEAD_DIM),
              ("loss", 4, 0))


def _pack_small(vals, D):
    rows = [jnp.zeros((1, D), F32) for _ in range(8)]
    for name, r, off in SMALL_ROWS:
        if name in vals:
            rows[r] = lax.dynamic_update_slice(rows[r], vals[name].astype(F32), (0, off))
    return jnp.concatenate(rows, axis=0)


def _unpack_small(packed, vals):
    return {name: packed[r:r + 1, off:off + vals[name].shape[1]] for name, r, off in SMALL_ROWS if name in vals}


def kernel(x, attn_norm_w, w_in, q_norm_w, k_norm_w, dil_out_norm_w, sb_out_norm_w, w_out, ffn_norm_w, w_gate, w_up, w_down, loss_target, m_attn_norm_w, m_w_in, m_q_norm_w, m_k_norm_w, m_dil_out_norm_w, m_sb_out_norm_w, m_w_out, m_ffn_norm_w, m_w_gate, m_w_up, m_w_down, v_attn_norm_w, v_w_in, v_q_norm_w, v_k_norm_w, v_dil_out_norm_w, v_sb_out_norm_w, v_w_out, v_ffn_norm_w, v_w_gate, v_w_up, v_w_down):
    D = x.shape[-1]
    big_names = ("w_in", "w_out", "w_gate", "w_up", "w_down")
    flipped = ("w_gate", "w_up")
    tr = lambda a: jnp.swapaxes(a[0], 0, 1)
    big_w = dict(w_in=w_in[0], w_out=w_out[0], w_gate=tr(w_gate), w_up=tr(w_up), w_down=w_down[0])
    big_m = dict(w_in=m_w_in[0], w_out=m_w_out[0], w_gate=tr(m_w_gate), w_up=tr(m_w_up), w_down=m_w_down[0])
    big_v = dict(w_in=v_w_in[0], w_out=v_w_out[0], w_gate=tr(v_w_gate), w_up=tr(v_w_up), w_down=v_w_down[0])
    small_w = dict(attn=attn_norm_w, q=q_norm_w, k=k_norm_w, dil=dil_out_norm_w, sb=sb_out_norm_w, ffn=ffn_norm_w)
    small_m = dict(attn=m_attn_norm_w, q=m_q_norm_w, k=m_k_norm_w, dil=m_dil_out_norm_w, sb=m_sb_out_norm_w,
                   ffn=m_ffn_norm_w)
    small_v = dict(attn=v_attn_norm_w, q=v_q_norm_w, k=v_k_norm_w, dil=v_dil_out_norm_w, sb=v_sb_out_norm_w,
                   ffn=v_ffn_norm_w)

    c = lax.axis_index("c").astype(jnp.int32).reshape(1)
    chip = (2 * lax.axis_index("x") + lax.axis_index("y")).astype(jnp.int32).reshape(1)
    (w_in_g,) = _gather_weights([big_w["w_in"].astype(BF16)])
    w_out_slots = _own_slots(big_w["w_out"].astype(BF16))
    ffn_slots = [_own_slots(big_w[n].astype(BF16)) for n in FFN_NAMES]

    loss_parts, grad_x, small_g, dw_in, w_out_half, ffn_halves = _device_step(
        x[0], loss_target[0], attn_norm_w, q_norm_w, k_norm_w, dil_out_norm_w, sb_out_norm_w, ffn_norm_w,
        w_in_g, w_out_slots, ffn_slots, c, chip)
    small_g["loss"] = (jnp.sum(loss_parts[:, 0, 0]) * (0.5 / D)).reshape(1, 1)

    from_pair, all_small = _pair_exchange([dw_in], _pack_small(small_g, D))
    in_part = _pair_sum(dw_in, from_pair, c, "w_in")
    (from_chips,) = _chip_exchange([in_part])
    halves = [_chip_sum(from_chips, in_part, chip, "w_in"), w_out_half] + ffn_halves
    others = _pair_swap(halves)
    big_out = {n: _adamw(big_w[n], mine, other, big_m[n], big_v[n], c, n)
               for n, mine, other in zip(big_names, halves, others)}
    sg, sd, sm, sv = _small_update(all_small, _pack_small(small_w, D), _pack_small(small_m, D),
                                   _pack_small(small_v, D))
    small_out = [_unpack_small(t, small_w) for t in (sg, sd, sm, sv)]

    order = (("attn", None), (None, "w_in"), ("q", None), ("k", None), ("dil", None), ("sb", None),
             (None, "w_out"), ("ffn", None), (None, "w_gate"), (None, "w_up"), (None, "w_down"))
    outs = [sg[4, 0], grad_x[None]]
    for kind in range(4):
        for s_name, b_name in order:
            if s_name is not None:
                outs.append(small_out[kind][s_name])
            else:
                res = big_out[b_name][kind]
                outs.append((jnp.swapaxes(res, 0, 1) if b_name in flipped else res)[None])
    return tuple(outs)
```

```python
import functools

import jax
import jax.numpy as jnp
from jax import lax
from jax.experimental import pallas as pl
from jax.experimental.pallas import tpu as pltpu

F32 = jnp.float32
BF16 = jnp.bfloat16
MESH = pl.DeviceIdType.MESH

HEAD_DIM = 64
GROUP = 512
BLOCK = 128
LANES = 128
N_CHIPS = 4
N_DEV = 8
EPS = 1e-6
ROPE_THETA = 10000.0
DILATIONS = (1, 4, 16)
NEG = -1e30

ADAM_LR = 0.001
ADAM_B1 = 0.9
ADAM_B2 = 0.999
ADAM_EPS = 1e-08
ADAM_WD = 0.01
ADAM_STEP = 10


def _dot(a, b):
    return jnp.dot(a, b, preferred_element_type=F32)


def _dot_nt(a, b):
    return lax.dot_general(a, b, (((1,), (1,)), ((), ())), preferred_element_type=F32)


def _dot_tn(a, b):
    return lax.dot_general(a, b, (((0,), (0,)), ((), ())), preferred_element_type=F32)


def _split(v):
    hi = lax.bitcast_convert_type(lax.bitcast_convert_type(v, jnp.uint32) & jnp.uint32(0xFFFF0000), F32)
    return hi.astype(BF16), (v - hi).astype(BF16)


def _segsum(v, g):
    hi, lo = _split(v)
    w = g.shape[1]
    return jnp.concatenate([_dot(jnp.concatenate([hi[:, c:c + w], lo[:, c:c + w]], axis=1), g)
                            for c in range(0, v.shape[1], w)], axis=1)


def _rot_half(x):
    outs = []
    for c in range(x.shape[1] // LANES):
        xc = x[:, c * LANES:(c + 1) * LANES]
        lane = lax.broadcasted_iota(jnp.int32, xc.shape, 1)
        first = (lane % HEAD_DIM) < (HEAD_DIM // 2)
        outs.append(jnp.where(first, pltpu.roll(xc, LANES - 32, 1), pltpu.roll(xc, 32, 1)))
    return outs[0] if len(outs) == 1 else jnp.concatenate(outs, axis=1)


def _rms(x):
    return lax.rsqrt(jnp.mean(x * x, axis=-1, keepdims=True) + EPS)


def _rms_bwd(dy, x, w):
    rstd = _rms(x)
    xh = x * rstd
    dxh = dy * w
    dx = rstd * (dxh - xh * jnp.mean(dxh * xh, axis=-1, keepdims=True))
    return dx, dy * xh


def _sigmoid(x):
    return 1.0 / (1.0 + jnp.exp(-x))


def _sum4(ref):
    p = [ref[j].astype(F32) for j in range(N_CHIPS)]
    return (p[0] + p[1]) + (p[2] + p[3])


def _full(shape):
    n = len(shape)
    return pl.BlockSpec(shape, lambda *_: (0,) * n)


def _strided_spec(tm, r):
    return pl.BlockSpec((r, tm // r, GROUP), lambda i: (0, i, 0))


def _strided_shape(S, r, dtype):
    return jax.ShapeDtypeStruct((r, S // r, GROUP), dtype)


def _to_strided(scr, val, outs):
    chunks = range(GROUP // LANES)
    for k in chunks:
        scr[k] = val[:, _lanes(k)]
    for r, o_ref in outs:
        if r == 1:
            o_ref[0] = val.astype(o_ref.dtype)
            continue
        n = val.shape[0] // r
        for c in range(r):
            rows = pl.ds(c, n, stride=r)
            o_ref[c] = jnp.concatenate([scr.at[k][rows, :] for k in chunks], axis=1).astype(o_ref.dtype)


def _from_strided(scr, ref):
    r, n, _ = ref.shape
    if r == 1:
        return ref[0].astype(F32)
    chunks = range(GROUP // LANES)
    for c in range(r):
        plane = ref[c].astype(F32)
        for k in chunks:
            scr.at[k][pl.ds(c, n, stride=r), :] = plane[:, _lanes(k)]
    return jnp.concatenate([scr[k] for k in chunks], axis=1)


def _strided_scratch(tm):
    return pltpu.VMEM((GROUP // LANES, tm, LANES), F32)


def _tile4(t):
    return jnp.concatenate([t] * (GROUP // LANES), axis=1)


def _in_proj_fwd(x, attn_w, w_in_g, qw, kw, cos_t, sin_t, seg_ones, slots):
    S, D = x.shape
    tm = 512
    wcols = w_in_g.shape[2]
    nd = len(DILATIONS)
    ns = len(slots)

    def body(x_ref, aw_ref, w_ref, qw_ref, kw_ref, cos_ref, sin_ref, g_ref, *rest):
        slot_in, (h_ref, qa_ref, ka_ref), rest = rest[:ns], rest[ns:ns + 3], rest[ns + 3:]
        q_refs, k_refs, v_refs = rest[:nd], rest[nd:2 * nd], rest[2 * nd:3 * nd]
        qs_ref, ks_ref, vs_ref = rest[3 * nd:3 * nd + 3]
        slot_out, scr, sems = rest[3 * nd + 3:3 * nd + 3 + ns], rest[3 * nd + 3 + ns], rest[3 * nd + 4 + ns:]
        finish = _hosted_gathers(slot_in, slot_out, *sems, pl.program_id(0), S // tm) if ns else None
        xv = x_ref[...]
        h = (xv * _rms(xv) * aw_ref[...]).astype(BF16)
        h_ref[...] = h
        proj = jnp.concatenate([_dot(h, w_ref[j]) for j in range(N_CHIPS)], axis=1)
        qa = proj[:, 0 * GROUP:1 * GROUP]
        ka = proj[:, 1 * GROUP:2 * GROUP]
        qa_ref[...] = qa
        ka_ref[...] = ka
        _to_strided(scr, proj[:, 2 * GROUP:3 * GROUP], list(zip(DILATIONS, v_refs)))
        qs_ref[...] = proj[:, 3 * GROUP:4 * GROUP].astype(BF16)
        ks_ref[...] = proj[:, 4 * GROUP:5 * GROUP].astype(BF16)
        vs_ref[...] = proj[:, 5 * GROUP:6 * GROUP].astype(BF16)
        g = g_ref[...]
        cos = _tile4(cos_ref[...])
        sin = _tile4(sin_ref[...])
        for t, w_r, o_rs in ((qa, qw_ref, q_refs), (ka, kw_ref, k_refs)):
            rstd = lax.rsqrt(_segsum(t * t, g) * (1.0 / HEAD_DIM) + EPS)
            tn = t * rstd * w_r[...]
            _to_strided(scr, tn * cos + _rot_half(tn) * sin, list(zip(DILATIONS, o_rs)))
        if finish is not None:
            finish()

    row = lambda i: (i, 0)
    tile = lambda n, dt: jax.ShapeDtypeStruct((S, n), dt)
    planes = [_strided_spec(tm, r) for r in DILATIONS]
    h_in, h_out, h_shape, h_sems = _hosted_specs(slots)
    n_out = 6 + 3 * nd
    return pl.pallas_call(
        body, name="in_proj_fwd", grid=(S // tm,),
        in_specs=[pl.BlockSpec((tm, D), row), _full((1, D)), _full((N_CHIPS, D, wcols)),
                  _full((1, GROUP)), _full((1, GROUP)),
                  pl.BlockSpec((tm, LANES), row), pl.BlockSpec((tm, LANES), row),
                  _full((GROUP, GROUP // 2))] + h_in,
        out_specs=[pl.BlockSpec((tm, D), row)] + [pl.BlockSpec((tm, GROUP), row)] * 2 + planes * 3
                  + [pl.BlockSpec((tm, GROUP), row)] * 3 + h_out,
        out_shape=[tile(D, BF16), tile(GROUP, F32), tile(GROUP, F32)]
                  + [_strided_shape(S, r, BF16) for r in DILATIONS] * 3 + [tile(GROUP, BF16)] * 3 + h_shape,
        input_output_aliases={8 + a: n_out + a for a in range(ns)},
        scratch_shapes=[_strided_scratch(tm)] + h_sems,
    )(x, attn_w, w_in_g, qw, kw, cos_t, sin_t, seg_ones, *slots)


DIL_PLANES = 1


def _dil_fwd(q, k, v, slots):
    r, L, _ = q.shape
    nb = L // BLOCK
    P = GROUP // LANES
    PL = min(DIL_PLANES, r)
    ns = len(slots)
    units = [(pp, hp) for pp in range(PL) for hp in range(P)]

    def body(q_ref, kc_ref, kp_ref, vc_ref, vp_ref, *rest):
        o_ref, l_ref = rest[ns:ns + 2]
        n = pl.program_id(1)
        finish = (_hosted_gathers(rest[:ns], rest[ns + 2:2 * ns + 2], *rest[2 * ns + 2:],
                                  pl.program_id(0) * nb + n, (r // PL) * nb) if ns else None)
        rowi = lax.broadcasted_iota(jnp.int32, (BLOCK, BLOCK), 0)
        coli = lax.broadcasted_iota(jnp.int32, (BLOCK, BLOCK), 1)
        first = coli < HEAD_DIM
        masks = (coli <= rowi, jnp.logical_and(coli >= rowi, n > 0))
        s2 = {}
        for pp, hp in units:
            q2 = _scaled(q_ref[pp, :, _lanes(hp)])
            for b, k_ref in enumerate((kc_ref, kp_ref)):
                s2[pp, hp, b] = _dot_nt(q2, _by_head(k_ref[pp, :, _lanes(hp)], first))
        ps, inv, lse = {}, {}, {}
        for pp, hp in units:
            for h in range(2):
                s = [jnp.where(masks[b], s2[pp, hp, b][:, h * BLOCK:(h + 1) * BLOCK], NEG) for b in range(2)]
                m = jnp.maximum(jnp.max(s[0], axis=1, keepdims=True), jnp.max(s[1], axis=1, keepdims=True))
                p = [jnp.exp(s[b] - m) for b in range(2)]
                den = jnp.sum(p[0], axis=1, keepdims=True) + jnp.sum(p[1], axis=1, keepdims=True)
                ps[pp, hp, h] = [p[b].astype(BF16) for b in range(2)]
                inv[pp, hp, h] = 1.0 / den
                lse[pp, hp, h] = m + jnp.log(den)
        for pp, hp in units:
            o = jnp.zeros((BLOCK, LANES), F32)
            for b, v_ref in enumerate((vc_ref, vp_ref)):
                o = o + _dot(jnp.concatenate([ps[pp, hp, 0][b], ps[pp, hp, 1][b]], axis=1),
                             _by_head(v_ref[pp, :, _lanes(hp)], first))
            o_ref[pp, :, _lanes(hp)] = (o * jnp.where(first, inv[pp, hp, 0], inv[pp, hp, 1])).astype(BF16)
            l_ref[pp, :, _lanes(hp)] = jnp.where(first, lse[pp, hp, 0], lse[pp, hp, 1])
        if finish is not None:
            finish()

    cur = pl.BlockSpec((PL, BLOCK, GROUP), lambda c, n: (c, n, 0))
    prev = pl.BlockSpec((PL, BLOCK, GROUP), lambda c, n: (c, jnp.maximum(n - 1, 0), 0))
    h_in, h_out, h_shape, h_sems = _hosted_specs(slots)
    return pl.pallas_call(
        body, name="dil_fwd_r%d" % r, grid=(r // PL, nb),
        in_specs=[cur, cur, prev, cur, prev] + h_in, out_specs=[cur, cur] + h_out,
        out_shape=[jax.ShapeDtypeStruct(q.shape, BF16), jax.ShapeDtypeStruct(q.shape, F32)] + h_shape,
        input_output_aliases={5 + a: 2 + a for a in range(ns)},
        scratch_shapes=h_sems,
    )(q, k, k, v, v, *slots)


def _dil_bwd(q, k, v, do, lse, delta, parts, pairs):
    r, L, _ = q.shape
    nb = L // BLOCK
    P = GROUP // LANES
    PL = min(DIL_PLANES, r)
    scale = HEAD_DIM ** -0.5
    units = [(pp, hp) for pp in range(PL) for hp in range(P)]
    npt, npr = len(parts), len(pairs)
    nx = npt + npr

    def body(qc_ref, qn_ref, doc_ref, don_ref, lc_ref, ln_ref, dc_ref, dn_ref, k_ref, v_ref, *rest):
        x_in, (dq_ref, dk_ref, dv_ref), x_out = rest[:nx], rest[nx:nx + 3], rest[nx + 3:2 * nx + 3]
        carry, send, recv = rest[2 * nx + 3:]
        j = pl.program_id(1)
        step = pl.program_id(0) * nb + j
        plans = [_chip_send_plan(x_in[a], x_out[a], send.at[pl.ds(3 * a, 3)], recv.at[pl.ds(3 * a, 3)])
                 for a in range(npt)]
        plans += [_pair_send_plan(x_in[npt + a], x_out[npt + a], send.at[3 * npt + a], recv.at[3 * npt + a])
                  for a in range(npr)]

        def start():
            for begin, _ in plans:
                begin()

        def finish():
            for _, end in plans:
                end()

        pl.when(step == 0)(start)
        rowi = lax.broadcasted_iota(jnp.int32, (BLOCK, BLOCK), 0)
        coli = lax.broadcasted_iota(jnp.int32, (BLOCK, BLOCK), 1)
        first = coli < HEAD_DIM
        sides = ((qc_ref, doc_ref, lc_ref, dc_ref, coli <= rowi),
                 (qn_ref, don_ref, ln_ref, dn_ref, jnp.logical_and(coli >= rowi, j < nb - 1)))

        @pl.when(j == 0)
        def _():
            carry[...] = jnp.zeros_like(carry)

        kcat, q2, do2, s2, dp2 = {}, {}, {}, {}, {}
        for pp, hp in units:
            kcat[pp, hp] = _by_head(k_ref[pp, :, _lanes(hp)], first)
            vcat = _by_head(v_ref[pp, :, _lanes(hp)], first)
            for x, (q_r, do_r, _, _, _) in enumerate(sides):
                q2[pp, hp, x] = _scaled(q_r[pp, :, _lanes(hp)])
                do2[pp, hp, x] = do_r[pp, :, _lanes(hp)]
                s2[pp, hp, x] = _dot_nt(q2[pp, hp, x], kcat[pp, hp])
                dp2[pp, hp, x] = _dot_nt(do2[pp, hp, x], vcat)
        pcat, dscat = {}, {}
        for pp, hp in units:
            for x, (_, _, l_r, d_r, msk) in enumerate(sides):
                ps, dss = [], []
                for h in range(2):
                    col = hp * LANES + h * HEAD_DIM
                    half = slice(h * BLOCK, (h + 1) * BLOCK)
                    p = jnp.where(msk, jnp.exp(s2[pp, hp, x][:, half] - l_r[pp, :, col:col + 1]), 0.0)
                    ps.append(p.astype(BF16))
                    dss.append((p * (dp2[pp, hp, x][:, half] - d_r[pp, :, col:col + 1])).astype(BF16))
                pcat[pp, hp, x] = jnp.concatenate(ps, axis=1)
                dscat[pp, hp, x] = jnp.concatenate(dss, axis=1)
        for pp, hp in units:
            dv2 = _dot_tn(pcat[pp, hp, 0], do2[pp, hp, 0]) + _dot_tn(pcat[pp, hp, 1], do2[pp, hp, 1])
            dk2 = _dot_tn(dscat[pp, hp, 0], q2[pp, hp, 0]) + _dot_tn(dscat[pp, hp, 1], q2[pp, hp, 1])
            dv_ref[pp, :, _lanes(hp)] = jnp.where(first, dv2[:BLOCK], dv2[BLOCK:]).astype(BF16)
            dk_ref[pp, :, _lanes(hp)] = jnp.where(first, dk2[:BLOCK], dk2[BLOCK:]).astype(BF16)
            dq_ref[pp, :, _lanes(hp)] = (carry[pp, :, _lanes(hp)]
                                         + _dot(dscat[pp, hp, 0], kcat[pp, hp]) * scale).astype(BF16)
            carry[pp, :, _lanes(hp)] = _dot(dscat[pp, hp, 1], kcat[pp, hp]) * scale
        pl.when(step == (r // PL) * nb - 1)(finish)

    cur = pl.BlockSpec((PL, BLOCK, GROUP), lambda c, n: (c, n, 0))
    nxt = pl.BlockSpec((PL, BLOCK, GROUP), lambda c, n: (c, jnp.minimum(n + 1, nb - 1), 0))
    return pl.pallas_call(
        body, name="dil_bwd_r%d" % r, grid=(r // PL, nb),
        in_specs=[cur, nxt, cur, nxt, cur, nxt, cur, nxt, cur, cur] + [HBM] * nx, out_specs=[cur, cur, cur] + [HBM] * nx,
        out_shape=[jax.ShapeDtypeStruct(q.shape, BF16)] * 3 + [jax.ShapeDtypeStruct(p.shape, p.dtype) for p in parts]
                  + [jax.ShapeDtypeStruct((g.shape[0], g.shape[1] // 2, g.shape[2]), g.dtype) for g in pairs],
        scratch_shapes=[pltpu.VMEM((PL, BLOCK, GROUP), F32), pltpu.SemaphoreType.DMA((3 * npt + npr,)),
                        pltpu.SemaphoreType.DMA((3 * npt + npr,))],
    )(q, q, do, do, lse, lse, delta, delta, k, v, *parts, *pairs)


SB_TILES = 2
SB_PAIRS_FWD = 4
SB_PAIRS_BWD = 4
SB_DEAD = -110.0


def _lanes(hp):
    return slice(hp * LANES, (hp + 1) * LANES)


def _sb_logits(z, valid):
    e = jnp.exp(-jnp.abs(z))
    lb = jnp.minimum(z, 0.0) - jnp.log(1.0 + e)
    lk = lb - z
    if valid is not None:
        lk = jnp.where(valid, lk, 0.0)
    return e, lb, lk


def _by_head(t, first):
    zero = jnp.zeros_like(t)
    return jnp.concatenate([jnp.where(first, t, zero), jnp.where(first, zero, t)], axis=0)


def _sb_valid(i, j):
    rowi = lax.broadcasted_iota(jnp.int32, (BLOCK, BLOCK), 0)
    coli = lax.broadcasted_iota(jnp.int32, (BLOCK, BLOCK), 1)
    return (coli - rowi) < (i - j) * BLOCK


def _scaled(q):
    return (q.astype(F32) * (HEAD_DIM ** -0.5)).astype(BF16)


def _hosted_gathers(refs_in, refs_out, send, recv, step, steps):
    plans = [_gather_plan(refs_in[a], refs_out[a], send.at[pl.ds(6 * a, 6)], recv.at[pl.ds(6 * a, 6)])
             for a in range(len(refs_in))]
    for stage, at in ((0, 0), (1, (2 * steps) // 3)):
        @pl.when(step == at)
        def _():
            for plan in plans:
                plan[stage]()

    def finish():
        @pl.when(step == steps - 1)
        def _():
            for plan in plans:
                plan[2]()

    return finish


def _hosted_specs(slots):
    n = len(slots)
    sems = [pltpu.SemaphoreType.DMA((6 * n,))] * 2 if n else []
    return [HBM] * n, [HBM] * n, [jax.ShapeDtypeStruct(s.shape, s.dtype) for s in slots], sems


def _sb_fwd(qs, ks, vs, tri_later, slots):
    S = qs.shape[0]
    P = SB_PAIRS_FWD
    W = P * LANES
    ns = len(slots)

    def body(q_ref, k_ref, v_ref, u_ref, *rest):
        o_ref, lt_ref, from_ref = rest[ns:ns + 3]
        i = pl.program_id(1)
        finish = _hosted_gathers(rest[:ns], rest[ns + 3:2 * ns + 3], *rest[2 * ns + 3:], i, S // BLOCK) if ns else None
        first = lax.broadcasted_iota(jnp.int32, (BLOCK, LANES), 1) < HEAD_DIM
        q2 = [_scaled(q_ref[:, _lanes(hp)]) for hp in range(P)]

        def walk(tiles, carry):
            runs, accs = list(carry[0]), list(carry[1])
            units = [(t, hp) for t in range(len(tiles)) for hp in range(P)]
            offs = [pl.multiple_of(j * BLOCK, BLOCK) for j, _ in tiles]
            valids = [_sb_valid(i, j) if diag else None for j, diag in tiles]
            z2s, lbs, c2s = {}, {}, {}
            for t, hp in units:
                z2s[t, hp] = _dot_nt(q2[hp], _by_head(k_ref[pl.ds(offs[t], BLOCK), _lanes(hp)], first))
            for t, hp in units:
                for h in range(2):
                    _, lb, lk = _sb_logits(z2s[t, hp][:, h * BLOCK:(h + 1) * BLOCK], valids[t])
                    lbs[t, hp, h] = lb
                    c2s[t, hp, h] = _dot(jnp.concatenate(_split(lk), axis=1), u_ref[...])
            for t, hp in units:
                a2 = []
                for h in range(2):
                    a = jnp.exp(lbs[t, hp, h] + c2s[t, hp, h][:, :BLOCK] + runs[2 * hp + h])
                    if valids[t] is not None:
                        a = jnp.where(valids[t], a, 0.0)
                    a2.append(a.astype(BF16))
                    runs[2 * hp + h] = runs[2 * hp + h] + c2s[t, hp, h][:, BLOCK:]
                vcat = _by_head(v_ref[pl.ds(offs[t], BLOCK), _lanes(hp)], first)
                accs[hp] = accs[hp] + _dot(jnp.concatenate(a2, axis=1), vcat)
            return tuple(runs), tuple(accs)

        def chunk(ci, carry):
            return walk([(ci * SB_TILES + t, False) for t in reversed(range(SB_TILES))], carry)

        def alive(runs):
            top = functools.reduce(jnp.maximum, runs)
            return (jnp.max(top) > SB_DEAD).astype(jnp.int32)

        def step(c):
            t, _, runs, accs = c
            runs, accs = chunk(nfull - 1 - t, (runs, accs))
            return t + 1, alive(runs), runs, accs

        zero = jnp.zeros((BLOCK, LANES), F32)
        nfull = i // SB_TILES
        ragged = [functools.partial(walk, [(i, True)] + [(i - 1 - m, False) for m in range(extra)])
                  for extra in range(SB_TILES)]
        runs, accs = lax.switch(i % SB_TILES, ragged, ((zero,) * (2 * P), (zero,) * P))
        done, _, runs, accs = lax.while_loop(lambda c: jnp.logical_and(c[0] < nfull, c[1] > 0), step,
                                             (jnp.int32(0), alive(runs), runs, accs))
        for hp in range(P):
            o_ref[:, _lanes(hp)] = accs[hp]
            lt_ref[:, _lanes(hp)] = jnp.where(first, runs[2 * hp], runs[2 * hp + 1])
        from_ref[...] = jnp.full(from_ref.shape, nfull - done, jnp.int32)
        if finish is not None:
            finish()

    assert W == GROUP
    blk = pl.BlockSpec((BLOCK, W), lambda hp, i: (i, hp))
    col = pl.BlockSpec((S, W), lambda hp, i: (0, hp))
    h_in, h_out, h_shape, h_sems = _hosted_specs(slots)
    return pl.pallas_call(
        body, name="sb_fwd", grid=(GROUP // W, S // BLOCK),
        in_specs=[blk, col, col, _full((2 * BLOCK, 2 * BLOCK))] + h_in,
        out_specs=[blk, blk, pl.BlockSpec((1, 8, LANES), lambda hp, i: (i, 0, 0))] + h_out,
        out_shape=[jax.ShapeDtypeStruct((S, GROUP), F32)] * 2
                  + [jax.ShapeDtypeStruct((S // BLOCK, 8, LANES), jnp.int32)] + h_shape,
        input_output_aliases={4 + a: 3 + a for a in range(ns)},
        scratch_shapes=h_sems,
    )(qs, ks, vs, tri_later, *slots)


def _sb_bwd(first_chunk, qs, ks, vs, do, ltot, tri_upto, tri_before):
    S = qs.shape[0]
    P = SB_PAIRS_BWD
    W = P * LANES

    def body(from_ref, q_ref, k_ref, v_ref, do_ref, lt_ref, w_ref, x_ref, dq_ref, dk_ref, dv_ref):
        i = pl.program_id(1)

        @pl.when(i == 0)
        def _():
            dk_ref[...] = jnp.zeros_like(dk_ref)
            dv_ref[...] = jnp.zeros_like(dv_ref)

        first = lax.broadcasted_iota(jnp.int32, (BLOCK, LANES), 1) < HEAD_DIM
        q2 = [_scaled(q_ref[:, _lanes(hp)]) for hp in range(P)]
        do2 = [do_ref[:, _lanes(hp)] for hp in range(P)]
        totals = [jnp.broadcast_to(lt_ref[:, n * HEAD_DIM:n * HEAD_DIM + 1], (BLOCK, LANES)) for n in range(2 * P)]

        def walk(tiles, carry):
            keeps, grads, dqs = list(carry[0]), list(carry[1]), list(carry[2])
            units = [(t, hp) for t in range(len(tiles)) for hp in range(P)]
            offs = [pl.multiple_of(j * BLOCK, BLOCK) for j, _ in tiles]
            valids = [_sb_valid(i, j) if diag else None for j, diag in tiles]
            kcat, z2, da2, es, lbs, c2s, as_, des, p2s = {}, {}, {}, {}, {}, {}, {}, {}, {}
            for t, hp in units:
                kcat[t, hp] = _by_head(k_ref[pl.ds(offs[t], BLOCK), _lanes(hp)], first)
                z2[t, hp] = _dot_nt(q2[hp], kcat[t, hp])
                da2[t, hp] = _dot_nt(do2[hp], _by_head(v_ref[pl.ds(offs[t], BLOCK), _lanes(hp)], first))
            for t, hp in units:
                for h in range(2):
                    es[t, hp, h], lbs[t, hp, h], lk = _sb_logits(z2[t, hp][:, h * BLOCK:(h + 1) * BLOCK], valids[t])
                    c2s[t, hp, h] = _dot(jnp.concatenate(_split(lk), axis=1), w_ref[...])
            for t, hp in units:
                for h in range(2):
                    n = 2 * hp + h
                    a = jnp.exp(lbs[t, hp, h] + (totals[n] - (keeps[n] + c2s[t, hp, h][:, :BLOCK])))
                    if valids[t] is not None:
                        a = jnp.where(valids[t], a, 0.0)
                    keeps[n] = keeps[n] + c2s[t, hp, h][:, BLOCK:]
                    de = a * da2[t, hp][:, h * BLOCK:(h + 1) * BLOCK]
                    as_[t, hp, h], des[t, hp, h] = a.astype(BF16), de
                    p2s[t, hp, h] = _dot(jnp.concatenate(_split(de), axis=1), x_ref[...])
            for t, hp in units:
                dz2 = []
                for h in range(2):
                    n = 2 * hp + h
                    e = es[t, hp, h]
                    sig = jnp.where(z2[t, hp][:, h * BLOCK:(h + 1) * BLOCK] >= 0.0, 1.0, e) / (1.0 + e)
                    dz = des[t, hp, h] * (1.0 - sig) - (grads[n] + p2s[t, hp, h][:, :BLOCK]) * sig
                    if valids[t] is not None:
                        dz = jnp.where(valids[t], dz, 0.0)
                    grads[n] = grads[n] + p2s[t, hp, h][:, BLOCK:]
                    dz2.append(dz.astype(BF16))
                dzcat = jnp.concatenate(dz2, axis=1)
                dk2 = _dot_tn(dzcat, q2[hp])
                dv2 = _dot_tn(jnp.concatenate([as_[t, hp, 0], as_[t, hp, 1]], axis=1), do2[hp])
                dk_ref[pl.ds(offs[t], BLOCK), _lanes(hp)] += jnp.where(first, dk2[:BLOCK], dk2[BLOCK:])
                dv_ref[pl.ds(offs[t], BLOCK), _lanes(hp)] += jnp.where(first, dv2[:BLOCK], dv2[BLOCK:])
                dqs[hp] = dqs[hp] + _dot(dzcat, kcat[t, hp])
            return tuple(keeps), tuple(grads), tuple(dqs)

        zero = jnp.zeros((BLOCK, LANES), F32)
        nfull = i // SB_TILES
        carry = lax.fori_loop(
            from_ref[i], nfull, lambda ci, c: walk([(ci * SB_TILES + t, False) for t in range(SB_TILES)], c),
            ((zero,) * (2 * P), (zero,) * (2 * P), (zero,) * P))
        ragged = [functools.partial(walk, [(i - m, False) for m in range(extra, 0, -1)] + [(i, True)])
                  for extra in range(SB_TILES)]
        carry = lax.switch(i % SB_TILES, ragged, carry)
        for hp in range(P):
            dq_ref[:, _lanes(hp)] = carry[2][hp] * (HEAD_DIM ** -0.5)

    blk = pl.BlockSpec((BLOCK, W), lambda hp, i, fr: (i, hp))
    col = pl.BlockSpec((S, W), lambda hp, i, fr: (0, hp))
    tri = pl.BlockSpec((2 * BLOCK, 2 * BLOCK), lambda hp, i, fr: (0, 0))
    return pl.pallas_call(
        body, name="sb_bwd",
        grid_spec=pltpu.PrefetchScalarGridSpec(
            num_scalar_prefetch=1, grid=(GROUP // W, S // BLOCK),
            in_specs=[blk, col, col, blk, blk, tri, tri], out_specs=[blk, col, col]),
        out_shape=[jax.ShapeDtypeStruct((S, GROUP), F32)] * 3,
    )(first_chunk, qs, ks, vs, do, ltot, tri_upto, tri_before)


def _out_proj_fwd(o_br, l_br, o_sb, x, w_dil, w_sbn, w_out_g, slots):
    S, D = x.shape
    tm = 512
    ns = len(slots)

    def body(o0, o1, o2, l0, l1, l2, os_ref, x_ref, wd_ref, ws_ref, w_ref, *rest):
        od_ref, s0, s1, s2, x1_ref = rest[ns:ns + 5]
        scr = rest[2 * ns + 5]
        finish = (_hosted_gathers(rest[:ns], rest[ns + 5:2 * ns + 5], *rest[2 * ns + 6:], pl.program_id(0), S // tm)
                  if ns else None)
        ls = [_from_strided(scr, l) for l in (l0, l1, l2)]
        os_ = [_from_strided(scr, o) for o in (o0, o1, o2)]
        m = jnp.maximum(jnp.maximum(ls[0], ls[1]), ls[2])
        es = [jnp.exp(l - m) for l in ls]
        den = es[0] + es[1] + es[2]
        od = (es[0] * os_[0] + es[1] * os_[1] + es[2] * os_[2]) / den
        od_ref[...] = od
        _to_strided(scr, m + jnp.log(den), list(zip(DILATIONS, (s0, s1, s2))))
        osb = os_ref[...]
        mixed = jnp.concatenate([(od * _rms(od) * wd_ref[...]).astype(BF16),
                                 (osb * _rms(osb) * ws_ref[...]).astype(BF16)], axis=1)
        x1_ref[...] = x_ref[...] + _dot(mixed, w_ref[...])
        if finish is not None:
            finish()

    row = lambda i: (i, 0)
    g = pl.BlockSpec((tm, GROUP), row)
    d = pl.BlockSpec((tm, D), row)
    planes = [_strided_spec(tm, r) for r in DILATIONS]
    h_in, h_out, h_shape, h_sems = _hosted_specs(slots)
    return pl.pallas_call(
        body, name="out_proj_fwd", grid=(S // tm,),
        in_specs=planes * 2 + [g, d, _full((1, GROUP)), _full((1, GROUP)), _full((2 * GROUP, D))] + h_in,
        out_specs=[g] + planes + [d] + h_out,
        out_shape=[jax.ShapeDtypeStruct((S, GROUP), F32)] + [_strided_shape(S, r, F32) for r in DILATIONS]
                  + [jax.ShapeDtypeStruct((S, D), F32)] + h_shape,
        input_output_aliases={11 + a: 5 + a for a in range(ns)},
        scratch_shapes=[_strided_scratch(tm)] + h_sems,
    )(*o_br, *l_br, o_sb, x, w_dil, w_sbn, w_out_g, *slots)


def _ffn_fwd(x1, target, ffn_w, wg_g, wu_g, wd_g):
    S, D = x1.shape
    F = wg_g.shape[1]
    tm = 512
    nt = S // tm

    def body(x_ref, t_ref, nw_ref, wg_ref, wu_ref, wd_ref, h_ref, g_ref, u_ref, dy_ref, loss_ref, h_s, acc):
        j = pl.program_id(1)

        @pl.when(j == 0)
        def _():
            xv = x_ref[...]
            h = (xv * _rms(xv) * nw_ref[...]).astype(BF16)
            h_s[...] = h
            h_ref[...] = h
            acc[...] = xv

        h = h_s[...]
        g = _dot_nt(h, wg_ref[0])
        u = _dot_nt(h, wu_ref[0])
        g_ref[0] = g.astype(BF16)
        u_ref[0] = u.astype(BF16)
        a = (g * _sigmoid(g) * u).astype(BF16)
        acc[...] += _dot(a, wd_ref[0])

        @pl.when(j == N_CHIPS - 1)
        def _():
            err = acc[...] - t_ref[...]
            dy_ref[...] = err * (1.0 / D)
            loss_ref[...] = jnp.full(loss_ref.shape, jnp.sum(err * err), F32)

    row = lambda t, j: (t, 0)
    shard = lambda t, j: (j, 0, 0)
    act = lambda t, j: (j, t, 0)
    return pl.pallas_call(
        body, name="ffn_fwd", grid=(nt, N_CHIPS),
        in_specs=[pl.BlockSpec((tm, D), row), pl.BlockSpec((tm, D), row), pl.BlockSpec((1, D), lambda t, j: (0, 0))]
                 + [pl.BlockSpec((1, F, D), shard)] * 3,
        out_specs=[pl.BlockSpec((tm, D), row), pl.BlockSpec((1, tm, F), act), pl.BlockSpec((1, tm, F), act),
                   pl.BlockSpec((tm, D), row), pl.BlockSpec((1, 8, LANES), lambda t, j: (t, 0, 0))],
        out_shape=[jax.ShapeDtypeStruct((S, D), BF16), jax.ShapeDtypeStruct((N_CHIPS, S, F), BF16),
                   jax.ShapeDtypeStruct((N_CHIPS, S, F), BF16), jax.ShapeDtypeStruct((S, D), F32),
                   jax.ShapeDtypeStruct((nt, 8, LANES), F32)],
        scratch_shapes=[pltpu.VMEM((tm, D), BF16), pltpu.VMEM((tm, D), F32)],
    )(x1, target, ffn_w, wg_g, wu_g, wd_g)


def _ffn_bwd(h2, dy, g, u, wg_g, wu_g, wd_g):
    S, D = dy.shape
    F = wg_g.shape[1]
    tm = 512

    def body(h_ref, dy_ref, g_ref, u_ref, wg_ref, wu_ref, wd_ref, dwg_ref, dwu_ref, dwd_ref, dh_ref, *narrow):
        t = pl.program_id(1)

        @pl.when(t == 0)
        def _():
            dwg_ref[...] = jnp.zeros_like(dwg_ref)
            dwu_ref[...] = jnp.zeros_like(dwu_ref)
            dwd_ref[...] = jnp.zeros_like(dwd_ref)

        h = h_ref[...]
        dyb = dy_ref[...].astype(BF16)
        gv = g_ref[0].astype(F32)
        uv = u_ref[0].astype(F32)
        da = _dot_nt(dyb, wd_ref[0])
        sg = _sigmoid(gv)
        silu = gv * sg
        du = (da * silu).astype(BF16)
        dg = (da * uv * (sg * (1.0 + gv * (1.0 - sg)))).astype(BF16)
        dwd_ref[0] += _dot_tn((silu * uv).astype(BF16), dyb)
        dwg_ref[0] += _dot_tn(dg, h)
        dwu_ref[0] += _dot_tn(du, h)
        dh_ref[0] = (_dot(dg, wg_ref[0]) + _dot(du, wu_ref[0])).astype(BF16)

        @pl.when(t == S // tm - 1)
        def _():
            for full, half in zip((dwg_ref, dwu_ref, dwd_ref), narrow):
                half[...] = full[...].astype(BF16)

    row = lambda j, t: (t, 0)
    shard = lambda j, t: (j, 0, 0)
    act = lambda j, t: (j, t, 0)
    return pl.pallas_call(
        body, name="ffn_bwd", grid=(N_CHIPS, S // tm),
        in_specs=[pl.BlockSpec((tm, D), row), pl.BlockSpec((tm, D), row),
                  pl.BlockSpec((1, tm, F), act), pl.BlockSpec((1, tm, F), act)] + [pl.BlockSpec((1, F, D), shard)] * 3,
        out_specs=[pl.BlockSpec((1, F, D), shard)] * 3 + [pl.BlockSpec((1, tm, D), act)]
                  + [pl.BlockSpec((1, F, D), shard)] * 3,
        out_shape=[jax.ShapeDtypeStruct((N_CHIPS, F, D), F32)] * 3 + [jax.ShapeDtypeStruct((N_CHIPS, S, D), BF16)]
                  + [jax.ShapeDtypeStruct((N_CHIPS, F, D), BF16)] * 3,
    )(h2, dy, g, u, wg_g, wu_g, wd_g)


def _out_proj_bwd(dh2p, dy, x1, ffn_w, w_out_g, o_dil, o_sb, w_dil, w_sbn, seg_ones, ffn_grads):
    S, D = dy.shape
    tm = 512
    ng = len(ffn_grads)

    def body(dh_ref, dy_ref, x1_ref, nw_ref, w_ref, od_ref, os_ref, wd_ref, ws_ref, g_ref, *rest):
        gin, rest = rest[:ng], rest[ng:]
        dx1_ref, dod0, dod1, dod2, dos_ref, dl0, dl1, dl2, dw_ref, dnw_ref, dwd_ref, dws_ref = rest[:12]
        gout, (scr, send, recv) = rest[12:12 + ng], rest[12 + ng:]
        i = pl.program_id(0)
        plans = [_pair_send_plan(gin[a], gout[a], send.at[a], recv.at[a]) for a in range(ng)]

        @pl.when(i == 0)
        def _():
            for start, _ in plans:
                start()

        @pl.when(i == 0)
        def _():
            for r_ in (dw_ref, dnw_ref, dwd_ref, dws_ref):
                r_[...] = jnp.zeros_like(r_)

        dh2 = _sum4(dh_ref)
        dxn, dwn = _rms_bwd(dh2, x1_ref[...], nw_ref[...])
        dnw_ref[...] += jnp.sum(dwn, axis=0, keepdims=True)
        dx1 = dy_ref[...] + dxn
        dx1_ref[...] = dx1
        dx1b = dx1.astype(BF16)
        dmix = _dot_nt(dx1b, w_ref[...])
        od = od_ref[...]
        osb = os_ref[...]
        mixed = jnp.concatenate([(od * _rms(od) * wd_ref[...]).astype(BF16),
                                 (osb * _rms(osb) * ws_ref[...]).astype(BF16)], axis=1)
        dw_ref[...] += _dot_tn(mixed, dx1b)
        do, dwo = _rms_bwd(dmix[:, :GROUP], od, wd_ref[...])
        dwd_ref[...] += jnp.sum(dwo, axis=0, keepdims=True)
        _to_strided(scr, do, list(zip(DILATIONS, (dod0, dod1, dod2))))
        _to_strided(scr, _segsum(do * od, g_ref[...]), list(zip(DILATIONS, (dl0, dl1, dl2))))
        do, dwo = _rms_bwd(dmix[:, GROUP:], osb, ws_ref[...])
        dws_ref[...] += jnp.sum(dwo, axis=0, keepdims=True)
        dos_ref[...] = do.astype(BF16)

        @pl.when(i == S // tm - 1)
        def _():
            for _, finish in plans:
                finish()

    row = lambda i: (i, 0)
    gsp = pl.BlockSpec((tm, GROUP), row)
    dsp = pl.BlockSpec((tm, D), row)
    planes = [_strided_spec(tm, r) for r in DILATIONS]
    halves = [jax.ShapeDtypeStruct((g.shape[0], g.shape[1] // 2, g.shape[2]), g.dtype) for g in ffn_grads]
    return pl.pallas_call(
        body, name="out_proj_bwd", grid=(S // tm,),
        in_specs=[pl.BlockSpec((N_CHIPS, tm, D), lambda i: (0, i, 0)), dsp, dsp, _full((1, D)), _full((2 * GROUP, D)),
                  gsp, gsp, _full((1, GROUP)), _full((1, GROUP)), _full((GROUP, GROUP // 2))] + [HBM] * ng,
        out_specs=[dsp] + planes + [gsp] + planes
                  + [_full((2 * GROUP, D)), _full((1, D)), _full((1, GROUP)), _full((1, GROUP))] + [HBM] * ng,
        out_shape=[jax.ShapeDtypeStruct((S, D), F32)] + [_strided_shape(S, r, BF16) for r in DILATIONS]
                  + [jax.ShapeDtypeStruct((S, GROUP), BF16)] + [_strided_shape(S, r, F32) for r in DILATIONS]
                  + [jax.ShapeDtypeStruct((2 * GROUP, D), F32),
                     jax.ShapeDtypeStruct((1, D), F32), jax.ShapeDtypeStruct((1, GROUP), F32),
                     jax.ShapeDtypeStruct((1, GROUP), F32)] + halves,
        scratch_shapes=[_strided_scratch(tm), pltpu.SemaphoreType.DMA((ng,)), pltpu.SemaphoreType.DMA((ng,))],
    )(dh2p, dy, x1, ffn_w, w_out_g, o_dil, o_sb, w_dil, w_sbn, seg_ones, *ffn_grads)


def _attn_in_bwd(dq_br, dk_br, dv_br, dqs, dks, dvs, qa, ka, qw, kw, cos_t, sin_t, seg_ones, h, w_in_g, x, dx1, attn_w):
    S, D = x.shape
    wc = w_in_g.shape[2]
    tm = 256

    def body(q0, q1, q2, k0, k1, k2, v0, v1, v2, dqs_ref, dks_ref, dvs_ref, qa_ref, ka_ref, qw_ref, kw_ref,
             cos_ref, sin_ref, g_ref, h_ref, w_ref, x_ref, dx1_ref, aw_ref,
             gx_ref, dw_ref, daw_ref, dqw_ref, dkw_ref, accq, acck, scr):
        i = pl.program_id(0)

        @pl.when(i == 0)
        def _():
            accq[...] = jnp.zeros_like(accq)
            acck[...] = jnp.zeros_like(acck)
            dw_ref[...] = jnp.zeros_like(dw_ref)
            daw_ref[...] = jnp.zeros_like(daw_ref)

        def branches(refs):
            return (_from_strided(scr, refs[0]) + _from_strided(scr, refs[1])) + _from_strided(scr, refs[2])

        g = g_ref[...]
        cos = _tile4(cos_ref[...])
        sin = _tile4(sin_ref[...])
        pieces = []
        for refs, pre_ref, w_r, acc in (((q0, q1, q2), qa_ref, qw_ref, accq), ((k0, k1, k2), ka_ref, kw_ref, acck)):
            dh = branches(refs)
            dn = dh * cos + _rot_half(dh * sin)
            pre = pre_ref[...]
            rstd = lax.rsqrt(_segsum(pre * pre, g) * (1.0 / HEAD_DIM) + EPS)
            xh = pre * rstd
            acc[...] += jnp.sum(dn * xh, axis=0, keepdims=True)
            dxh = dn * w_r[...]
            pieces.append((rstd * (dxh - xh * (_segsum(dxh * xh, g) * (1.0 / HEAD_DIM)))).astype(BF16))
        pieces += [branches((v0, v1, v2)).astype(BF16), dqs_ref[...].astype(BF16), dks_ref[...].astype(BF16),
                   dvs_ref[...].astype(BF16)]
        dproj = jnp.concatenate(pieces, axis=1)
        hv = h_ref[...]
        dh = jnp.zeros((tm, D), F32)
        for j in range(N_CHIPS):
            dp = dproj[:, j * wc:(j + 1) * wc]
            dw_ref[j] += _dot_tn(hv, dp)
            dh = dh + _dot_nt(dp, w_ref[j])
        dx, dw = _rms_bwd(dh, x_ref[...], aw_ref[...])
        daw_ref[...] += jnp.sum(dw, axis=0, keepdims=True)
        gx_ref[...] = dx1_ref[...] + dx

        @pl.when(i == S // tm - 1)
        def _():
            for acc, o_ref in ((accq, dqw_ref), (acck, dkw_ref)):
                a = acc[...]
                pair = (a[:, 0:LANES] + a[:, LANES:2 * LANES]) + (a[:, 2 * LANES:3 * LANES] + a[:, 3 * LANES:4 * LANES])
                o_ref[...] = pair + pltpu.roll(pair, HEAD_DIM, 1)

    row = lambda i: (i, 0)
    gsp = pl.BlockSpec((tm, GROUP), row)
    dsp = pl.BlockSpec((tm, D), row)
    tab = pl.BlockSpec((tm, LANES), row)
    planes = [_strided_spec(tm, r) for r in DILATIONS]
    return pl.pallas_call(
        body, name="attn_in_bwd", grid=(S // tm,),
        in_specs=planes * 3 + [gsp] * 5 + [_full((1, GROUP)), _full((1, GROUP)), tab, tab, _full((GROUP, GROUP // 2)),
                                          dsp, _full((N_CHIPS, D, wc)), dsp, dsp, _full((1, D))],
        out_specs=[dsp, _full((N_CHIPS, D, wc)), _full((1, D)), _full((1, LANES)), _full((1, LANES))],
        out_shape=[jax.ShapeDtypeStruct((S, D), F32), jax.ShapeDtypeStruct((N_CHIPS, D, wc), F32),
                   jax.ShapeDtypeStruct((1, D), F32), jax.ShapeDtypeStruct((1, LANES), F32),
                   jax.ShapeDtypeStruct((1, LANES), F32)],
        scratch_shapes=[pltpu.VMEM((1, GROUP), F32), pltpu.VMEM((1, GROUP), F32), _strided_scratch(tm)],
    )(*dq_br, *dk_br, *dv_br, dqs, dks, dvs, qa, ka, qw, kw, cos_t, sin_t, seg_ones, h, w_in_g, x, dx1, attn_w)


def _constants(S):
    pos = jnp.arange(S, dtype=F32)
    inv_freq = ROPE_THETA ** (-jnp.arange(0, HEAD_DIM, 2, dtype=F32) / HEAD_DIM)
    ang_a = pos[::BLOCK, None] * inv_freq[None, :]
    ang_b = pos[:BLOCK, None] * inv_freq[None, :]
    ca, sa, cb, sb = jnp.cos(ang_a)[:, None], jnp.sin(ang_a)[:, None], jnp.cos(ang_b)[None], jnp.sin(ang_b)[None]
    cos = (ca * cb - sa * sb).reshape(S, HEAD_DIM // 2)
    sin = (sa * cb + ca * sb).reshape(S, HEAD_DIM // 2)
    cos_t = jnp.concatenate([cos, cos] * 2, axis=1)
    sin_t = jnp.concatenate([-sin, sin] * 2, axis=1)
    idx = jnp.arange(GROUP // 2)
    seg_ones = (idx[:, None] // HEAD_DIM == idx[None, :] // HEAD_DIM).astype(BF16)
    seg_ones = jnp.concatenate([seg_ones, seg_ones], axis=0)
    r = jnp.arange(BLOCK)
    ones = jnp.ones((BLOCK, BLOCK), BF16)
    tris = [jnp.concatenate([jnp.concatenate([m.astype(BF16), ones], axis=1)] * 2, axis=0) for m in
            (r[:, None] > r[None, :],
             r[:, None] <= r[None, :],
             r[:, None] < r[None, :])]
    return cos_t, sin_t, seg_ones, tris


FFN_NAMES = ("w_gate", "w_up", "w_down")


def _device_step(x, target, attn_w, qn_w, kn_w, dil_w, sbn_w, ffn_w, w_in_g, w_out_slots, ffn_slots, core, chip):
    S = x.shape[0]
    cos_t, sin_t, seg_ones, (tri_later, tri_upto, tri_before) = _constants(S)
    reps = GROUP // HEAD_DIM
    qw = jnp.tile(qn_w, (1, reps))
    kw = jnp.tile(kn_w, (1, reps))

    nd = len(DILATIONS)
    h, qa, ka, *rest, wg_g = _in_proj_fwd(x, attn_w, w_in_g, qw, kw, cos_t, sin_t, seg_ones, ffn_slots[:1])
    qh, kh, va, (qs, ks, vs) = rest[:nd], rest[nd:2 * nd], rest[2 * nd:3 * nd], rest[3 * nd:]
    hosted = ([], [w_out_slots], [])
    branches = [_dil_fwd(qh[b], kh[b], va[b], hosted[b]) for b in range(nd)]
    w_out_g = branches[1][2].reshape(-1, x.shape[1])
    o_sb, ltot, walked, wu_g, wd_g = _sb_fwd(qs, ks, vs, tri_later, ffn_slots[1:])
    o_dil, *lse, x1 = _out_proj_fwd([b[0] for b in branches], [b[1] for b in branches], o_sb, x, dil_w, sbn_w,
                                    w_out_g, [])
    h2, g, u, dy, loss_parts = _ffn_fwd(x1, target, ffn_w, wg_g, wu_g, wd_g)

    *ffn_grads, dh2p, n0, n1, n2 = _ffn_bwd(h2, dy, g, u, wg_g, wu_g, wd_g)
    dx1, *mid, dw_out, dffn_w, ddil_w, dsbn_w, p0, p1, p2 = _out_proj_bwd(
        dh2p, dy, x1, ffn_w, w_out_g, o_dil, o_sb, dil_w, sbn_w, seg_ones, [n0, n1, n2])
    do_dil, do_sb, delta = mid[:nd], mid[nd], mid[nd + 1:]
    parts = [_pair_sum(gr, fr, core, n) for gr, fr, n in zip(ffn_grads, (p0, p1, p2), FFN_NAMES)]
    dqs, dks, dvs = _sb_bwd(walked[:, 0, 0], qs, ks, vs, do_sb, ltot, tri_upto, tri_before)
    dw_out = dw_out.reshape(N_CHIPS, -1, x.shape[1])
    dbr = [None] * nd
    dbr[0] = _dil_bwd(qh[0], kh[0], va[0], do_dil[0], lse[0], delta[0], [parts[0]], [dw_out])
    out_part = _pair_sum(dw_out, dbr[0][4], core, "w_out")
    dbr[1] = _dil_bwd(qh[1], kh[1], va[1], do_dil[1], lse[1], delta[1], [parts[1], out_part], [])
    dbr[2] = _dil_bwd(qh[2], kh[2], va[2], do_dil[2], lse[2], delta[2], [parts[2]], [])
    ffn_halves = [_chip_sum(dbr[b][3], parts[b], chip, FFN_NAMES[b]) for b in range(nd)]
    w_out_half = _chip_sum(dbr[1][4], out_part, chip, "w_out")
    grad_x, dw_in, dattn_w, dqw, dkw = _attn_in_bwd(
        [b[0] for b in dbr], [b[1] for b in dbr], [b[2] for b in dbr], dqs, dks, dvs,
        qa, ka, qw, kw, cos_t, sin_t, seg_ones, h, w_in_g, x, dx1, attn_w)
    small = dict(attn=dattn_w, q=dqw[:, :HEAD_DIM], k=dkw[:, :HEAD_DIM], dil=ddil_w, sb=dsbn_w, ffn=dffn_w)
    return loss_parts, grad_x, small, dw_in, w_out_half, ffn_halves


HBM = pl.BlockSpec(memory_space=pltpu.HBM)
VMEM = pl.BlockSpec(memory_space=pltpu.VMEM)
CHIP_FLIPS = ((1, 0), (0, 1), (1, 1))


def _place():
    return lax.axis_index("x"), lax.axis_index("y"), lax.axis_index("c")


def _flip(v, d):
    return 1 - v if d else v


def _half_rows(c, n):
    return pl.ds(pl.multiple_of(c * (n // 2), 16), n // 2)


def _gather_plan(slot_in, slot_out, send, recv):
    x, y, c = _place()
    p = 2 * x + y
    chips = [(_flip(x, dx), _flip(y, dy)) for dx, dy in CHIP_FLIPS]
    mine, other = _half_rows(c, slot_in.shape[1]), _half_rows(1 - c, slot_in.shape[1])

    def copy(k, src, dst, to):
        return pltpu.make_async_remote_copy(src_ref=src, dst_ref=dst, send_sem=send.at[k], recv_sem=recv.at[k],
                                            device_id=to, device_id_type=MESH)

    def first(k):
        return copy(k, slot_in.at[p, mine], slot_out.at[p, mine], (*chips[k], c))

    def passed(k, rows):
        land = slot_out.at[2 * chips[k][0] + chips[k][1], rows]
        return copy(3 + k, land, land, (x, y, 1 - c))

    def start():
        for k in range(3):
            first(k).start()

    def forward():
        for k in range(3):
            land = slot_out.at[2 * chips[k][0] + chips[k][1], mine]
            copy(k, land, land, (*chips[k], c)).wait_recv()
            passed(k, mine).start()

    def finish():
        for k in range(3):
            passed(k, other).wait_recv()
        for k in range(3):
            first(k).wait_send()
            passed(k, mine).wait_send()

    return start, forward, finish


def _chip_send_plan(part_in, recv_out, send, recv):
    x, y, c = _place()
    p = 2 * x + y
    chips = [(_flip(x, dx), _flip(y, dy)) for dx, dy in CHIP_FLIPS]

    def copy(k):
        q = 2 * chips[k][0] + chips[k][1]
        return pltpu.make_async_remote_copy(src_ref=part_in.at[q], dst_ref=recv_out.at[p], send_sem=send.at[k],
                                            recv_sem=recv.at[k], device_id=(*chips[k], c), device_id_type=MESH)

    def start():
        for k in range(3):
            copy(k).start()

    def finish():
        for k in range(3):
            land = recv_out.at[2 * chips[k][0] + chips[k][1]]
            pltpu.make_async_remote_copy(src_ref=land, dst_ref=land, send_sem=send.at[k], recv_sem=recv.at[k],
                                         device_id=(*chips[k], c), device_id_type=MESH).wait_recv()
        for k in range(3):
            copy(k).wait_send()

    return start, finish


def _pair_send_plan(grad_in, recv_out, send, recv):
    x, y, c = _place()

    def copy():
        theirs = _half_rows(1 - c, grad_in.shape[1])
        return pltpu.make_async_remote_copy(src_ref=grad_in.at[:, theirs, :], dst_ref=recv_out, send_sem=send,
                                            recv_sem=recv, device_id=(x, y, 1 - c), device_id_type=MESH)

    return (lambda: copy().start()), (lambda: copy().wait())


def _own_slots(shard):
    here = 2 * lax.axis_index("x") + lax.axis_index("y")
    return lax.dynamic_update_slice(lax.empty((N_CHIPS,) + shard.shape, shard.dtype), shard[None], (here, 0, 0))


def _gather_weights(shards):
    n = len(shards)

    def body(*refs):
        ins, outs = refs[:n], refs[n:2 * n]
        send, recv = refs[2 * n:]
        plans = [_gather_plan(ins[a], outs[a], send.at[pl.ds(6 * a, 6)], recv.at[pl.ds(6 * a, 6)]) for a in range(n)]
        for stage in range(3):
            for plan in plans:
                plan[stage]()

    slots = [_own_slots(s) for s in shards]
    return pl.pallas_call(
        body, name="gather_weights", in_specs=[HBM] * n, out_specs=[HBM] * n,
        out_shape=[jax.ShapeDtypeStruct(s.shape, s.dtype) for s in slots],
        input_output_aliases={a: a for a in range(n)},
        scratch_shapes=[pltpu.SemaphoreType.DMA((6 * n,)), pltpu.SemaphoreType.DMA((6 * n,))],
    )(*slots)


def _pair_exchange(grads, small):
    n = len(grads)

    def body(*refs):
        gin, sm = refs[:n], refs[n]
        gout, sm_all = refs[n + 1:2 * n + 1], refs[2 * n + 1]
        send, recv = refs[2 * n + 2:]
        x, y, c = _place()
        me = 4 * x + 2 * y + c
        big = [_pair_send_plan(gin[a], gout[a], send.at[a], recv.at[a]) for a in range(n)]
        for start, _ in big:
            start()
        sm_all[pl.ds(me, 1)] = sm[...][None]
        tiny = []
        for k in range(1, N_DEV):
            px, py, pc = _flip(x, k & 4), _flip(y, k & 2), _flip(c, k & 1)
            tiny.append((pltpu.make_async_remote_copy(
                src_ref=sm, dst_ref=sm_all.at[me], send_sem=send.at[n + k - 1], recv_sem=recv.at[n + k - 1],
                device_id=(px, py, pc), device_id_type=MESH), 4 * px + 2 * py + pc))
            tiny[-1][0].start()
        for k, (cp, peer) in enumerate(tiny):
            pltpu.make_async_remote_copy(src_ref=sm, dst_ref=sm_all.at[peer], send_sem=send.at[n + k],
                                         recv_sem=recv.at[n + k], device_id=(x, y, c),
                                         device_id_type=MESH).wait_recv()
            cp.wait_send()
        for _, finish in big:
            finish()

    halves = [jax.ShapeDtypeStruct((g.shape[0], g.shape[1] // 2, g.shape[2]), g.dtype) for g in grads]
    return pl.pallas_call(
        body, name="pair_exchange", in_specs=[HBM] * n + [VMEM], out_specs=[HBM] * n + [VMEM],
        out_shape=halves + [jax.ShapeDtypeStruct((N_DEV,) + small.shape, small.dtype)],
        scratch_shapes=[pltpu.SemaphoreType.DMA((n + N_DEV - 1,)), pltpu.SemaphoreType.DMA((n + N_DEV - 1,))],
    )(*grads, small)


def _chip_exchange(parts):
    n = len(parts)

    def body(*refs):
        pin, pout = refs[:n], refs[n:2 * n]
        send, recv = refs[2 * n:]
        plans = [_chip_send_plan(pin[a], pout[a], send.at[pl.ds(3 * a, 3)], recv.at[pl.ds(3 * a, 3)]) for a in range(n)]
        for stage in range(2):
            for plan in plans:
                plan[stage]()

    return pl.pallas_call(
        body, name="chip_exchange", in_specs=[HBM] * n, out_specs=[HBM] * n,
        out_shape=[jax.ShapeDtypeStruct(s.shape, s.dtype) for s in parts],
        scratch_shapes=[pltpu.SemaphoreType.DMA((3 * n,)), pltpu.SemaphoreType.DMA((3 * n,))],
    )(*parts)


def _pair_swap_plan(hin, hout, send, recv):
    x, y, c = _place()

    def copies():
        return [pltpu.make_async_remote_copy(src_ref=hin[a], dst_ref=hout[a], send_sem=send.at[a], recv_sem=recv.at[a],
                                             device_id=(x, y, 1 - c), device_id_type=MESH) for a in range(len(hin))]

    def start():
        for cp in copies():
            cp.start()

    def finish():
        for cp in copies():
            cp.wait()

    return start, finish


def _pair_swap(halves):
    n = len(halves)

    def body(*refs):
        start, finish = _pair_swap_plan(refs[:n], refs[n:2 * n], *refs[2 * n:])
        start()
        finish()

    return pl.pallas_call(
        body, name="pair_swap", in_specs=[HBM] * n, out_specs=[HBM] * n,
        out_shape=[jax.ShapeDtypeStruct(s.shape, s.dtype) for s in halves],
        scratch_shapes=[pltpu.SemaphoreType.DMA((n,)), pltpu.SemaphoreType.DMA((n,))],
    )(*halves)


def _pair_sum(grad, recv, c, tag):
    _, R, C = grad.shape
    hr = R // 2

    def body(c_ref, a_ref, b_ref, o_ref):
        o_ref[...] = (a_ref[...] + b_ref[...]).astype(BF16)

    return pl.pallas_call(
        body, name="pair_sum_" + tag,
        grid_spec=pltpu.PrefetchScalarGridSpec(
            num_scalar_prefetch=1, grid=(N_CHIPS,),
            in_specs=[pl.BlockSpec((1, hr, C), lambda s, cr: (s, cr[0], 0)),
                      pl.BlockSpec((1, hr, C), lambda s, cr: (s, 0, 0))],
            out_specs=pl.BlockSpec((1, hr, C), lambda s, cr: (s, 0, 0))),
        out_shape=jax.ShapeDtypeStruct((N_CHIPS, hr, C), BF16),
    )(c, grad, recv)


def _chip_sum(received, own, chip, tag):
    _, rows, C = received.shape
    tr = rows // 2

    def body(chip_ref, own_ref, r1_ref, r2_ref, r3_ref, o_ref):
        p = [r[0].astype(F32) for r in (own_ref, r1_ref, r2_ref, r3_ref)]
        o_ref[...] = (p[0] + p[1]) + (p[2] + p[3])

    def slot(k):
        return pl.BlockSpec((1, tr, C), lambda i, cr: (jnp.bitwise_xor(cr[0], k), i, 0))

    return pl.pallas_call(
        body, name="chip_sum_" + tag,
        grid_spec=pltpu.PrefetchScalarGridSpec(
            num_scalar_prefetch=1, grid=(rows // tr,), in_specs=[slot(0), slot(1), slot(2), slot(3)],
            out_specs=pl.BlockSpec((tr, C), lambda i, cr: (i, 0))),
        out_shape=jax.ShapeDtypeStruct((rows, C), F32),
    )(chip, own, received, received, received)


def _adamw_math(w, g, m, v):
    m = ADAM_B1 * m + (1.0 - ADAM_B1) * g
    v = ADAM_B2 * v + (1.0 - ADAM_B2) * (g * g)
    m_hat = m / (1.0 - ADAM_B1 ** ADAM_STEP)
    v_hat = v / (1.0 - ADAM_B2 ** ADAM_STEP)
    delta = -ADAM_LR * (m_hat / (jnp.sqrt(v_hat) + ADAM_EPS) + ADAM_WD * w)
    return delta, m, v


def _adamw(w, g_mine, g_other, m, v, c, tag):
    R, C = w.shape

    def body(c_ref, w_ref, gm_ref, go_ref, m_ref, v_ref, g_ref, d_ref, nm_ref, nv_ref):
        g = jnp.where(pl.program_id(0) == c_ref[0], gm_ref[...], go_ref[...])
        g_ref[...] = g
        d_ref[...], nm_ref[...], nv_ref[...] = _adamw_math(w_ref[...], g, m_ref[...], v_ref[...])

    blk = pl.BlockSpec((R // 2, C), lambda h, cr: (h, 0))
    half = pl.BlockSpec((R // 2, C), lambda h, cr: (0, 0))
    return pl.pallas_call(
        body, name="adamw_" + tag,
        grid_spec=pltpu.PrefetchScalarGridSpec(
            num_scalar_prefetch=1, grid=(2,), in_specs=[blk, half, half, blk, blk], out_specs=[blk] * 4),
        out_shape=[jax.ShapeDtypeStruct((R, C), F32)] * 4,
    )(c, w, g_mine, g_other, m, v)


def _small_update(all_small, w, m, v):
    def body(a_ref, w_ref, m_ref, v_ref, g_ref, d_ref, nm_ref, nv_ref):
        g = ((a_ref[0] + a_ref[1]) + (a_ref[2] + a_ref[3])) + ((a_ref[4] + a_ref[5]) + (a_ref[6] + a_ref[7]))
        g_ref[...] = g
        d_ref[...], nm_ref[...], nv_ref[...] = _adamw_math(w_ref[...], g, m_ref[...], v_ref[...])

    return pl.pallas_call(
        body, name="small_update", out_shape=[jax.ShapeDtypeStruct(w.shape, F32)] * 4,
    )(all_small, w, m, v)


SMALL_ROWS = (("attn", 0, 0), ("ffn", 1, 0), ("dil", 2, 0), ("sb", 2, GROUP), ("q", 3, 0), ("k", 3, HEAD_DIM),
              ("loss", 4, 0))


def _pack_small(vals, D):
    rows = [jnp.zeros((1, D), F32) for _ in range(8)]
    for name, r, off in SMALL_ROWS:
        if name in vals:
            rows[r] = lax.dynamic_update_slice(rows[r], vals[name].astype(F32), (0, off))
    return jnp.concatenate(rows, axis=0)


def _unpack_small(packed, vals):
    return {name: packed[r:r + 1, off:off + vals[name].shape[1]] for name, r, off in SMALL_ROWS if name in vals}


def kernel(x, attn_norm_w, w_in, q_norm_w, k_norm_w, dil_out_norm_w, sb_out_norm_w, w_out, ffn_norm_w, w_gate, w_up, w_down, loss_target, m_attn_norm_w, m_w_in, m_q_norm_w, m_k_norm_w, m_dil_out_norm_w, m_sb_out_norm_w, m_w_out, m_ffn_norm_w, m_w_gate, m_w_up, m_w_down, v_attn_norm_w, v_w_in, v_q_norm_w, v_k_norm_w, v_dil_out_norm_w, v_sb_out_norm_w, v_w_out, v_ffn_norm_w, v_w_gate, v_w_up, v_w_down):
    D = x.shape[-1]
    big_names = ("w_in", "w_out", "w_gate", "w_up", "w_down")
    flipped = ("w_gate", "w_up")
    tr = lambda a: jnp.swapaxes(a[0], 0, 1)
    big_w = dict(w_in=w_in[0], w_out=w_out[0], w_gate=tr(w_gate), w_up=tr(w_up), w_down=w_down[0])
    big_m = dict(w_in=m_w_in[0], w_out=m_w_out[0], w_gate=tr(m_w_gate), w_up=tr(m_w_up), w_down=m_w_down[0])
    big_v = dict(w_in=v_w_in[0], w_out=v_w_out[0], w_gate=tr(v_w_gate), w_up=tr(v_w_up), w_down=v_w_down[0])
    small_w = dict(attn=attn_norm_w, q=q_norm_w, k=k_norm_w, dil=dil_out_norm_w, sb=sb_out_norm_w, ffn=ffn_norm_w)
    small_m = dict(attn=m_attn_norm_w, q=m_q_norm_w, k=m_k_norm_w, dil=m_dil_out_norm_w, sb=m_sb_out_norm_w,
                   ffn=m_ffn_norm_w)
    small_v = dict(attn=v_attn_norm_w, q=v_q_norm_w, k=v_k_norm_w, dil=v_dil_out_norm_w, sb=v_sb_out_norm_w,
                   ffn=v_ffn_norm_w)

    c = lax.axis_index("c").astype(jnp.int32).reshape(1)
    chip = (2 * lax.axis_index("x") + lax.axis_index("y")).astype(jnp.int32).reshape(1)
    (w_in_g,) = _gather_weights([big_w["w_in"].astype(BF16)])
    w_out_slots = _own_slots(big_w["w_out"].astype(BF16))
    ffn_slots = [_own_slots(big_w[n].astype(BF16)) for n in FFN_NAMES]

    loss_parts, grad_x, small_g, dw_in, w_out_half, ffn_halves = _device_step(
        x[0], loss_target[0], attn_norm_w, q_norm_w, k_norm_w, dil_out_norm_w, sb_out_norm_w, ffn_norm_w,
        w_in_g, w_out_slots, ffn_slots, c, chip)
    small_g["loss"] = (jnp.sum(loss_parts[:, 0, 0]) * (0.5 / D)).reshape(1, 1)

    from_pair, all_small = _pair_exchange([dw_in], _pack_small(small_g, D))
    in_part = _pair_sum(dw_in, from_pair, c, "w_in")
    (from_chips,) = _chip_exchange([in_part])
    halves = [_chip_sum(from_chips, in_part, chip, "w_in"), w_out_half] + ffn_halves
    others = _pair_swap(halves)
    big_out = {n: _adamw(big_w[n], mine, other, big_m[n], big_v[n], c, n)
               for n, mine, other in zip(big_names, halves, others)}
    sg, sd, sm, sv = _small_update(all_small, _pack_small(small_w, D), _pack_small(small_m, D),
                                   _pack_small(small_v, D))
    small_out = [_unpack_small(t, small_w) for t in (sg, sd, sm, sv)]

    order = (("attn", None), (None, "w_in"), ("q", None), ("k", None), ("dil", None), ("sb", None),
             (None, "w_out"), ("ffn", None), (None, "w_gate"), (None, "w_up"), (None, "w_down"))
    outs = [sg[4, 0], grad_x[None]]
    for kind in range(4):
        for s_name, b_name in order:
            if s_name is not None:
                outs.append(small_out[kind][s_name])
            else:
                res = big_out[b_name][kind]
                outs.append((jnp.swapaxes(res, 0, 1) if b_name in flipped else res)[None])
    return tuple(outs)
```

```python
import functools

import jax
import jax.numpy as jnp
from jax import lax
from jax.experimental import pallas as pl
from jax.experimental.pallas import tpu as pltpu

F32 = jnp.float32
BF16 = jnp.bfloat16
MESH = pl.DeviceIdType.MESH

HEAD_DIM = 64
GROUP = 512
BLOCK = 128
LANES = 128
N_CHIPS = 4
N_DEV = 8
EPS = 1e-6
ROPE_THETA = 10000.0
DILATIONS = (1, 4, 16)
NEG = -1e30

ADAM_LR = 0.001
ADAM_B1 = 0.9
ADAM_B2 = 0.999
ADAM_EPS = 1e-08
ADAM_WD = 0.01
ADAM_STEP = 10


def _dot(a, b):
    return jnp.dot(a, b, preferred_element_type=F32)


def _dot_nt(a, b):
    return lax.dot_general(a, b, (((1,), (1,)), ((), ())), preferred_element_type=F32)


def _dot_tn(a, b):
    return lax.dot_general(a, b, (((0,), (0,)), ((), ())), preferred_element_type=F32)


def _split(v):
    hi = lax.bitcast_convert_type(lax.bitcast_convert_type(v, jnp.uint32) & jnp.uint32(0xFFFF0000), F32)
    return hi.astype(BF16), (v - hi).astype(BF16)


def _segsum(v, g):
    hi, lo = _split(v)
    w = g.shape[1]
    return jnp.concatenate([_dot(jnp.concatenate([hi[:, c:c + w], lo[:, c:c + w]], axis=1), g)
                            for c in range(0, v.shape[1], w)], axis=1)


def _rot_half(x):
    outs = []
    for c in range(x.shape[1] // LANES):
        xc = x[:, c * LANES:(c + 1) * LANES]
        lane = lax.broadcasted_iota(jnp.int32, xc.shape, 1)
        first = (lane % HEAD_DIM) < (HEAD_DIM // 2)
        outs.append(jnp.where(first, pltpu.roll(xc, LANES - 32, 1), pltpu.roll(xc, 32, 1)))
    return outs[0] if len(outs) == 1 else jnp.concatenate(outs, axis=1)


def _rms(x):
    return lax.rsqrt(jnp.mean(x * x, axis=-1, keepdims=True) + EPS)


def _rms_bwd(dy, x, w):
    rstd = _rms(x)
    xh = x * rstd
    dxh = dy * w
    dx = rstd * (dxh - xh * jnp.mean(dxh * xh, axis=-1, keepdims=True))
    return dx, dy * xh


def _sigmoid(x):
    return 1.0 / (1.0 + jnp.exp(-x))


def _sum4(ref):
    p = [ref[j].astype(F32) for j in range(N_CHIPS)]
    return (p[0] + p[1]) + (p[2] + p[3])


def _full(shape):
    n = len(shape)
    return pl.BlockSpec(shape, lambda *_: (0,) * n)


def _strided_spec(tm, r):
    return pl.BlockSpec((r, tm // r, GROUP), lambda i: (0, i, 0))


def _strided_shape(S, r, dtype):
    return jax.ShapeDtypeStruct((r, S // r, GROUP), dtype)


def _to_strided(scr, val, outs):
    chunks = range(GROUP // LANES)
    for k in chunks:
        scr[k] = val[:, _lanes(k)]
    for r, o_ref in outs:
        if r == 1:
            o_ref[0] = val.astype(o_ref.dtype)
            continue
        n = val.shape[0] // r
        for c in range(r):
            rows = pl.ds(c, n, stride=r)
            o_ref[c] = jnp.concatenate([scr.at[k][rows, :] for k in chunks], axis=1).astype(o_ref.dtype)


def _from_strided(scr, ref):
    r, n, _ = ref.shape
    if r == 1:
        return ref[0].astype(F32)
    chunks = range(GROUP // LANES)
    for c in range(r):
        plane = ref[c].astype(F32)
        for k in chunks:
            scr.at[k][pl.ds(c, n, stride=r), :] = plane[:, _lanes(k)]
    return jnp.concatenate([scr[k] for k in chunks], axis=1)


def _strided_scratch(tm):
    return pltpu.VMEM((GROUP // LANES, tm, LANES), F32)


def _tile4(t):
    return jnp.concatenate([t] * (GROUP // LANES), axis=1)


def _in_proj_fwd(x, attn_w, w_in_g, qw, kw, cos_t, sin_t, seg_ones, slots):
    S, D = x.shape
    tm = 512
    wcols = w_in_g.shape[2]
    nd = len(DILATIONS)
    ns = len(slots)

    def body(x_ref, aw_ref, w_ref, qw_ref, kw_ref, cos_ref, sin_ref, g_ref, *rest):
        slot_in, (h_ref, qa_ref, ka_ref), rest = rest[:ns], rest[ns:ns + 3], rest[ns + 3:]
        q_refs, k_refs, v_refs = rest[:nd], rest[nd:2 * nd], rest[2 * nd:3 * nd]
        qs_ref, ks_ref, vs_ref = rest[3 * nd:3 * nd + 3]
        slot_out, scr, sems = rest[3 * nd + 3:3 * nd + 3 + ns], rest[3 * nd + 3 + ns], rest[3 * nd + 4 + ns:]
        finish = _hosted_gathers(slot_in, slot_out, *sems, pl.program_id(0), S // tm) if ns else None
        xv = x_ref[...]
        h = (xv * _rms(xv) * aw_ref[...]).astype(BF16)
        h_ref[...] = h
        proj = jnp.concatenate([_dot(h, w_ref[j]) for j in range(N_CHIPS)], axis=1)
        qa = proj[:, 0 * GROUP:1 * GROUP]
        ka = proj[:, 1 * GROUP:2 * GROUP]
        qa_ref[...] = qa
        ka_ref[...] = ka
        _to_strided(scr, proj[:, 2 * GROUP:3 * GROUP], list(zip(DILATIONS, v_refs)))
        qs_ref[...] = proj[:, 3 * GROUP:4 * GROUP].astype(BF16)
        ks_ref[...] = proj[:, 4 * GROUP:5 * GROUP].astype(BF16)
        vs_ref[...] = proj[:, 5 * GROUP:6 * GROUP].astype(BF16)
        g = g_ref[...]
        cos = _tile4(cos_ref[...])
        sin = _tile4(sin_ref[...])
        for t, w_r, o_rs in ((qa, qw_ref, q_refs), (ka, kw_ref, k_refs)):
            rstd = lax.rsqrt(_segsum(t * t, g) * (1.0 / HEAD_DIM) + EPS)
            tn = t * rstd * w_r[...]
            _to_strided(scr, tn * cos + _rot_half(tn) * sin, list(zip(DILATIONS, o_rs)))
        if finish is not None:
            finish()

    row = lambda i: (i, 0)
    tile = lambda n, dt: jax.ShapeDtypeStruct((S, n), dt)
    planes = [_strided_spec(tm, r) for r in DILATIONS]
    h_in, h_out, h_shape, h_sems = _hosted_specs(slots)
    n_out = 6 + 3 * nd
    return pl.pallas_call(
        body, name="in_proj_fwd", grid=(S // tm,),
        in_specs=[pl.BlockSpec((tm, D), row), _full((1, D)), _full((N_CHIPS, D, wcols)),
                  _full((1, GROUP)), _full((1, GROUP)),
                  pl.BlockSpec((tm, LANES), row), pl.BlockSpec((tm, LANES), row),
                  _full((GROUP, GROUP // 2))] + h_in,
        out_specs=[pl.BlockSpec((tm, D), row)] + [pl.BlockSpec((tm, GROUP), row)] * 2 + planes * 3
                  + [pl.BlockSpec((tm, GROUP), row)] * 3 + h_out,
        out_shape=[tile(D, BF16), tile(GROUP, F32), tile(GROUP, F32)]
                  + [_strided_shape(S, r, BF16) for r in DILATIONS] * 3 + [tile(GROUP, BF16)] * 3 + h_shape,
        input_output_aliases={8 + a: n_out + a for a in range(ns)},
        scratch_shapes=[_strided_scratch(tm)] + h_sems,
    )(x, attn_w, w_in_g, qw, kw, cos_t, sin_t, seg_ones, *slots)


DIL_PLANES = 1


def _dil_fwd(q, k, v, slots):
    r, L, _ = q.shape
    nb = L // BLOCK
    P = GROUP // LANES
    PL = min(DIL_PLANES, r)
    ns = len(slots)
    units = [(pp, hp) for pp in range(PL) for hp in range(P)]

    def body(q_ref, kc_ref, kp_ref, vc_ref, vp_ref, *rest):
        o_ref, l_ref = rest[ns:ns + 2]
        n = pl.program_id(1)
        finish = (_hosted_gathers(rest[:ns], rest[ns + 2:2 * ns + 2], *rest[2 * ns + 2:],
                                  pl.program_id(0) * nb + n, (r // PL) * nb) if ns else None)
        rowi = lax.broadcasted_iota(jnp.int32, (BLOCK, BLOCK), 0)
        coli = lax.broadcasted_iota(jnp.int32, (BLOCK, BLOCK), 1)
        first = coli < HEAD_DIM
        masks = (coli <= rowi, jnp.logical_and(coli >= rowi, n > 0))
        s2 = {}
        for pp, hp in units:
            q2 = _scaled(q_ref[pp, :, _lanes(hp)])
            for b, k_ref in enumerate((kc_ref, kp_ref)):
                s2[pp, hp, b] = _dot_nt(q2, _by_head(k_ref[pp, :, _lanes(hp)], first))
        ps, inv, lse = {}, {}, {}
        for pp, hp in units:
            for h in range(2):
                s = [jnp.where(masks[b], s2[pp, hp, b][:, h * BLOCK:(h + 1) * BLOCK], NEG) for b in range(2)]
                m = jnp.maximum(jnp.max(s[0], axis=1, keepdims=True), jnp.max(s[1], axis=1, keepdims=True))
                p = [jnp.exp(s[b] - m) for b in range(2)]
                den = jnp.sum(p[0], axis=1, keepdims=True) + jnp.sum(p[1], axis=1, keepdims=True)
                ps[pp, hp, h] = [p[b].astype(BF16) for b in range(2)]
                inv[pp, hp, h] = 1.0 / den
                lse[pp, hp, h] = m + jnp.log(den)
        for pp, hp in units:
            o = jnp.zeros((BLOCK, LANES), F32)
            for b, v_ref in enumerate((vc_ref, vp_ref)):
                o = o + _dot(jnp.concatenate([ps[pp, hp, 0][b], ps[pp, hp, 1][b]], axis=1),
                             _by_head(v_ref[pp, :, _lanes(hp)], first))
            o_ref[pp, :, _lanes(hp)] = (o * jnp.where(first, inv[pp, hp, 0], inv[pp, hp, 1])).astype(BF16)
            l_ref[pp, :, _lanes(hp)] = jnp.where(first, lse[pp, hp, 0], lse[pp, hp, 1])
        if finish is not None:
            finish()

    cur = pl.BlockSpec((PL, BLOCK, GROUP), lambda c, n: (c, n, 0))
    prev = pl.BlockSpec((PL, BLOCK, GROUP), lambda c, n: (c, jnp.maximum(n - 1, 0), 0))
    h_in, h_out, h_shape, h_sems = _hosted_specs(slots)
    return pl.pallas_call(
        body, name="dil_fwd_r%d" % r, grid=(r // PL, nb),
        in_specs=[cur, cur, prev, cur, prev] + h_in, out_specs=[cur, cur] + h_out,
        out_shape=[jax.ShapeDtypeStruct(q.shape, BF16), jax.ShapeDtypeStruct(q.shape, F32)] + h_shape,
        input_output_aliases={5 + a: 2 + a for a in range(ns)},
        scratch_shapes=h_sems,
    )(q, k, k, v, v, *slots)


def _dil_bwd(q, k, v, do, lse, delta, parts, pairs):
    r, L, _ = q.shape
    nb = L // BLOCK
    P = GROUP // LANES
    PL = min(DIL_PLANES, r)
    scale = HEAD_DIM ** -0.5
    units = [(pp, hp) for pp in range(PL) for hp in range(P)]
    npt, npr = len(parts), len(pairs)
    nx = npt + npr

    def body(qc_ref, qn_ref, doc_ref, don_ref, lc_ref, ln_ref, dc_ref, dn_ref, k_ref, v_ref, *rest):
        x_in, (dq_ref, dk_ref, dv_ref), x_out = rest[:nx], rest[nx:nx + 3], rest[nx + 3:2 * nx + 3]
        carry, send, recv = rest[2 * nx + 3:]
        j = pl.program_id(1)
        step = pl.program_id(0) * nb + j
        plans = [_chip_send_plan(x_in[a], x_out[a], send.at[pl.ds(3 * a, 3)], recv.at[pl.ds(3 * a, 3)])
                 for a in range(npt)]
        plans += [_pair_send_plan(x_in[npt + a], x_out[npt + a], send.at[3 * npt + a], recv.at[3 * npt + a])
                  for a in range(npr)]

        def start():
            for begin, _ in plans:
                begin()

        def finish():
            for _, end in plans:
                end()

        rowi = lax.broadcasted_iota(jnp.int32, (BLOCK, BLOCK), 0)
        coli = lax.broadcasted_iota(jnp.int32, (BLOCK, BLOCK), 1)
        first = coli < HEAD_DIM
        sides = ((qc_ref, doc_ref, lc_ref, dc_ref, coli <= rowi),
                 (qn_ref, don_ref, ln_ref, dn_ref, jnp.logical_and(coli >= rowi, j < nb - 1)))

        @pl.when(j == 0)
        def _():
            carry[...] = jnp.zeros_like(carry)

        kcat, q2, do2, s2, dp2 = {}, {}, {}, {}, {}
        for pp, hp in units:
            kcat[pp, hp] = _by_head(k_ref[pp, :, _lanes(hp)], first)
            vcat = _by_head(v_ref[pp, :, _lanes(hp)], first)
            for x, (q_r, do_r, _, _, _) in enumerate(sides):
                q2[pp, hp, x] = _scaled(q_r[pp, :, _lanes(hp)])
                do2[pp, hp, x] = do_r[pp, :, _lanes(hp)]
                s2[pp, hp, x] = _dot_nt(q2[pp, hp, x], kcat[pp, hp])
                dp2[pp, hp, x] = _dot_nt(do2[pp, hp, x], vcat)
        pcat, dscat = {}, {}
        for pp, hp in units:
            for x, (_, _, l_r, d_r, msk) in enumerate(sides):
                ps, dss = [], []
                for h in range(2):
                    col = hp * LANES + h * HEAD_DIM
                    half = slice(h * BLOCK, (h + 1) * BLOCK)
                    p = jnp.where(msk, jnp.exp(s2[pp, hp, x][:, half] - l_r[pp, :, col:col + 1]), 0.0)
                    ps.append(p.astype(BF16))
                    dss.append((p * (dp2[pp, hp, x][:, half] - d_r[pp, :, col:col + 1])).astype(BF16))
                pcat[pp, hp, x] = jnp.concatenate(ps, axis=1)
                dscat[pp, hp, x] = jnp.concatenate(dss, axis=1)
        for pp, hp in units:
            dv2 = _dot_tn(pcat[pp, hp, 0], do2[pp, hp, 0]) + _dot_tn(pcat[pp, hp, 1], do2[pp, hp, 1])
            dk2 = _dot_tn(dscat[pp, hp, 0], q2[pp, hp, 0]) + _dot_tn(dscat[pp, hp, 1], q2[pp, hp, 1])
            dv_ref[pp, :, _lanes(hp)] = jnp.where(first, dv2[:BLOCK], dv2[BLOCK:]).astype(BF16)
            dk_ref[pp, :, _lanes(hp)] = jnp.where(first, dk2[:BLOCK], dk2[BLOCK:]).astype(BF16)
            dq_ref[pp, :, _lanes(hp)] = (carry[pp, :, _lanes(hp)]
                                         + _dot(dscat[pp, hp, 0], kcat[pp, hp]) * scale).astype(BF16)
            carry[pp, :, _lanes(hp)] = _dot(dscat[pp, hp, 1], kcat[pp, hp]) * scale
        pl.when(step == 0)(start)
        pl.when(step == (r // PL) * nb - 1)(finish)

    cur = pl.BlockSpec((PL, BLOCK, GROUP), lambda c, n: (c, n, 0))
    nxt = pl.BlockSpec((PL, BLOCK, GROUP), lambda c, n: (c, jnp.minimum(n + 1, nb - 1), 0))
    return pl.pallas_call(
        body, name="dil_bwd_r%d" % r, grid=(r // PL, nb),
        in_specs=[cur, nxt, cur, nxt, cur, nxt, cur, nxt, cur, cur] + [HBM] * nx, out_specs=[cur, cur, cur] + [HBM] * nx,
        out_shape=[jax.ShapeDtypeStruct(q.shape, BF16)] * 3 + [jax.ShapeDtypeStruct(p.shape, p.dtype) for p in parts]
                  + [jax.ShapeDtypeStruct((g.shape[0], g.shape[1] // 2, g.shape[2]), g.dtype) for g in pairs],
        scratch_shapes=[pltpu.VMEM((PL, BLOCK, GROUP), F32), pltpu.SemaphoreType.DMA((3 * npt + npr,)),
                        pltpu.SemaphoreType.DMA((3 * npt + npr,))],
    )(q, q, do, do, lse, lse, delta, delta, k, v, *parts, *pairs)


SB_TILES = 2
SB_PAIRS_FWD = 4
SB_PAIRS_BWD = 4
SB_DEAD = -110.0


def _lanes(hp):
    return slice(hp * LANES, (hp + 1) * LANES)


def _sb_logits(z, valid):
    e = jnp.exp(-jnp.abs(z))
    lb = jnp.minimum(z, 0.0) - jnp.log(1.0 + e)
    lk = lb - z
    if valid is not None:
        lk = jnp.where(valid, lk, 0.0)
    return e, lb, lk


def _by_head(t, first):
    zero = jnp.zeros_like(t)
    return jnp.concatenate([jnp.where(first, t, zero), jnp.where(first, zero, t)], axis=0)


def _sb_valid(i, j):
    rowi = lax.broadcasted_iota(jnp.int32, (BLOCK, BLOCK), 0)
    coli = lax.broadcasted_iota(jnp.int32, (BLOCK, BLOCK), 1)
    return (coli - rowi) < (i - j) * BLOCK


def _scaled(q):
    return (q.astype(F32) * (HEAD_DIM ** -0.5)).astype(BF16)


def _hosted_gathers(refs_in, refs_out, send, recv, step, steps):
    plans = [_gather_plan(refs_in[a], refs_out[a], send.at[pl.ds(6 * a, 6)], recv.at[pl.ds(6 * a, 6)])
             for a in range(len(refs_in))]

    def stages():
        for stage, at in ((0, 0), (1, (2 * steps) // 3), (2, steps - 1)):
            @pl.when(step == at)
            def _():
                for plan in plans:
                    plan[stage]()

    return stages


def _hosted_specs(slots):
    n = len(slots)
    sems = [pltpu.SemaphoreType.DMA((6 * n,))] * 2 if n else []
    return [HBM] * n, [HBM] * n, [jax.ShapeDtypeStruct(s.shape, s.dtype) for s in slots], sems


def _sb_fwd(qs, ks, vs, tri_later, slots):
    S = qs.shape[0]
    P = SB_PAIRS_FWD
    W = P * LANES
    ns = len(slots)

    def body(q_ref, k_ref, v_ref, u_ref, *rest):
        o_ref, lt_ref, from_ref = rest[ns:ns + 3]
        i = pl.program_id(1)
        finish = _hosted_gathers(rest[:ns], rest[ns + 3:2 * ns + 3], *rest[2 * ns + 3:], i, S // BLOCK) if ns else None
        first = lax.broadcasted_iota(jnp.int32, (BLOCK, LANES), 1) < HEAD_DIM
        q2 = [_scaled(q_ref[:, _lanes(hp)]) for hp in range(P)]

        def walk(tiles, carry):
            runs, accs = list(carry[0]), list(carry[1])
            units = [(t, hp) for t in range(len(tiles)) for hp in range(P)]
            offs = [pl.multiple_of(j * BLOCK, BLOCK) for j, _ in tiles]
            valids = [_sb_valid(i, j) if diag else None for j, diag in tiles]
            z2s, lbs, c2s = {}, {}, {}
            for t, hp in units:
                z2s[t, hp] = _dot_nt(q2[hp], _by_head(k_ref[pl.ds(offs[t], BLOCK), _lanes(hp)], first))
            for t, hp in units:
                for h in range(2):
                    _, lb, lk = _sb_logits(z2s[t, hp][:, h * BLOCK:(h + 1) * BLOCK], valids[t])
                    lbs[t, hp, h] = lb
                    c2s[t, hp, h] = _dot(jnp.concatenate(_split(lk), axis=1), u_ref[...])
            for t, hp in units:
                a2 = []
                for h in range(2):
                    a = jnp.exp(lbs[t, hp, h] + c2s[t, hp, h][:, :BLOCK] + runs[2 * hp + h])
                    if valids[t] is not None:
                        a = jnp.where(valids[t], a, 0.0)
                    a2.append(a.astype(BF16))
                    runs[2 * hp + h] = runs[2 * hp + h] + c2s[t, hp, h][:, BLOCK:]
                vcat = _by_head(v_ref[pl.ds(offs[t], BLOCK), _lanes(hp)], first)
                accs[hp] = accs[hp] + _dot(jnp.concatenate(a2, axis=1), vcat)
            return tuple(runs), tuple(accs)

        def chunk(ci, carry):
            return walk([(ci * SB_TILES + t, False) for t in reversed(range(SB_TILES))], carry)

        def alive(runs):
            top = functools.reduce(jnp.maximum, runs)
            return (jnp.max(top) > SB_DEAD).astype(jnp.int32)

        def step(c):
            t, _, runs, accs = c
            runs, accs = chunk(nfull - 1 - t, (runs, accs))
            return t + 1, alive(runs), runs, accs

        zero = jnp.zeros((BLOCK, LANES), F32)
        nfull = i // SB_TILES
        ragged = [functools.partial(walk, [(i, True)] + [(i - 1 - m, False) for m in range(extra)])
                  for extra in range(SB_TILES)]
        runs, accs = lax.switch(i % SB_TILES, ragged, ((zero,) * (2 * P), (zero,) * P))
        done, _, runs, accs = lax.while_loop(lambda c: jnp.logical_and(c[0] < nfull, c[1] > 0), step,
                                             (jnp.int32(0), alive(runs), runs, accs))
        for hp in range(P):
            o_ref[:, _lanes(hp)] = accs[hp]
            lt_ref[:, _lanes(hp)] = jnp.where(first, runs[2 * hp], runs[2 * hp + 1])
        from_ref[...] = jnp.full(from_ref.shape, nfull - done, jnp.int32)
        if finish is not None:
            finish()

    assert W == GROUP
    blk = pl.BlockSpec((BLOCK, W), lambda hp, i: (i, hp))
    col = pl.BlockSpec((S, W), lambda hp, i: (0, hp))
    h_in, h_out, h_shape, h_sems = _hosted_specs(slots)
    return pl.pallas_call(
        body, name="sb_fwd", grid=(GROUP // W, S // BLOCK),
        in_specs=[blk, col, col, _full((2 * BLOCK, 2 * BLOCK))] + h_in,
        out_specs=[blk, blk, pl.BlockSpec((1, 8, LANES), lambda hp, i: (i, 0, 0))] + h_out,
        out_shape=[jax.ShapeDtypeStruct((S, GROUP), F32)] * 2
                  + [jax.ShapeDtypeStruct((S // BLOCK, 8, LANES), jnp.int32)] + h_shape,
        input_output_aliases={4 + a: 3 + a for a in range(ns)},
        scratch_shapes=h_sems,
    )(qs, ks, vs, tri_later, *slots)


def _sb_bwd(first_chunk, qs, ks, vs, do, ltot, tri_upto, tri_before):
    S = qs.shape[0]
    P = SB_PAIRS_BWD
    W = P * LANES

    def body(from_ref, q_ref, k_ref, v_ref, do_ref, lt_ref, w_ref, x_ref, dq_ref, dk_ref, dv_ref):
        i = pl.program_id(1)

        @pl.when(i == 0)
        def _():
            dk_ref[...] = jnp.zeros_like(dk_ref)
            dv_ref[...] = jnp.zeros_like(dv_ref)

        first = lax.broadcasted_iota(jnp.int32, (BLOCK, LANES), 1) < HEAD_DIM
        q2 = [_scaled(q_ref[:, _lanes(hp)]) for hp in range(P)]
        do2 = [do_ref[:, _lanes(hp)] for hp in range(P)]
        totals = [jnp.broadcast_to(lt_ref[:, n * HEAD_DIM:n * HEAD_DIM + 1], (BLOCK, LANES)) for n in range(2 * P)]

        def walk(tiles, carry):
            keeps, grads, dqs = list(carry[0]), list(carry[1]), list(carry[2])
            units = [(t, hp) for t in range(len(tiles)) for hp in range(P)]
            offs = [pl.multiple_of(j * BLOCK, BLOCK) for j, _ in tiles]
            valids = [_sb_valid(i, j) if diag else None for j, diag in tiles]
            kcat, z2, da2, es, lbs, c2s, as_, des, p2s = {}, {}, {}, {}, {}, {}, {}, {}, {}
            for t, hp in units:
                kcat[t, hp] = _by_head(k_ref[pl.ds(offs[t], BLOCK), _lanes(hp)], first)
                z2[t, hp] = _dot_nt(q2[hp], kcat[t, hp])
                da2[t, hp] = _dot_nt(do2[hp], _by_head(v_ref[pl.ds(offs[t], BLOCK), _lanes(hp)], first))
            for t, hp in units:
                for h in range(2):
                    es[t, hp, h], lbs[t, hp, h], lk = _sb_logits(z2[t, hp][:, h * BLOCK:(h + 1) * BLOCK], valids[t])
                    c2s[t, hp, h] = _dot(jnp.concatenate(_split(lk), axis=1), w_ref[...])
            for t, hp in units:
                for h in range(2):
                    n = 2 * hp + h
                    a = jnp.exp(lbs[t, hp, h] + (totals[n] - (keeps[n] + c2s[t, hp, h][:, :BLOCK])))
                    if valids[t] is not None:
                        a = jnp.where(valids[t], a, 0.0)
                    keeps[n] = keeps[n] + c2s[t, hp, h][:, BLOCK:]
                    de = a * da2[t, hp][:, h * BLOCK:(h + 1) * BLOCK]
                    as_[t, hp, h], des[t, hp, h] = a.astype(BF16), de
                    p2s[t, hp, h] = _dot(jnp.concatenate(_split(de), axis=1), x_ref[...])
            for t, hp in units:
                dz2 = []
                for h in range(2):
                    n = 2 * hp + h
                    e = es[t, hp, h]
                    sig = jnp.where(z2[t, hp][:, h * BLOCK:(h + 1) * BLOCK] >= 0.0, 1.0, e) / (1.0 + e)
                    dz = des[t, hp, h] * (1.0 - sig) - (grads[n] + p2s[t, hp, h][:, :BLOCK]) * sig
                    if valids[t] is not None:
                        dz = jnp.where(valids[t], dz, 0.0)
                    grads[n] = grads[n] + p2s[t, hp, h][:, BLOCK:]
                    dz2.append(dz.astype(BF16))
                dzcat = jnp.concatenate(dz2, axis=1)
                dk2 = _dot_tn(dzcat, q2[hp])
                dv2 = _dot_tn(jnp.concatenate([as_[t, hp, 0], as_[t, hp, 1]], axis=1), do2[hp])
                dk_ref[pl.ds(offs[t], BLOCK), _lanes(hp)] += jnp.where(first, dk2[:BLOCK], dk2[BLOCK:])
                dv_ref[pl.ds(offs[t], BLOCK), _lanes(hp)] += jnp.where(first, dv2[:BLOCK], dv2[BLOCK:])
                dqs[hp] = dqs[hp] + _dot(dzcat, kcat[t, hp])
            return tuple(keeps), tuple(grads), tuple(dqs)

        zero = jnp.zeros((BLOCK, LANES), F32)
        nfull = i // SB_TILES
        carry = lax.fori_loop(
            from_ref[i], nfull, lambda ci, c: walk([(ci * SB_TILES + t, False) for t in range(SB_TILES)], c),
            ((zero,) * (2 * P), (zero,) * (2 * P), (zero,) * P))
        ragged = [functools.partial(walk, [(i - m, False) for m in range(extra, 0, -1)] + [(i, True)])
                  for extra in range(SB_TILES)]
        carry = lax.switch(i % SB_TILES, ragged, carry)
        for hp in range(P):
            dq_ref[:, _lanes(hp)] = carry[2][hp] * (HEAD_DIM ** -0.5)

    blk = pl.BlockSpec((BLOCK, W), lambda hp, i, fr: (i, hp))
    col = pl.BlockSpec((S, W), lambda hp, i, fr: (0, hp))
    tri = pl.BlockSpec((2 * BLOCK, 2 * BLOCK), lambda hp, i, fr: (0, 0))
    return pl.pallas_call(
        body, name="sb_bwd",
        grid_spec=pltpu.PrefetchScalarGridSpec(
            num_scalar_prefetch=1, grid=(GROUP // W, S // BLOCK),
            in_specs=[blk, col, col, blk, blk, tri, tri], out_specs=[blk, col, col]),
        out_shape=[jax.ShapeDtypeStruct((S, GROUP), F32)] * 3,
    )(first_chunk, qs, ks, vs, do, ltot, tri_upto, tri_before)


def _out_proj_fwd(o_br, l_br, o_sb, x, w_dil, w_sbn, w_out_g, slots):
    S, D = x.shape
    tm = 512
    ns = len(slots)

    def body(o0, o1, o2, l0, l1, l2, os_ref, x_ref, wd_ref, ws_ref, w_ref, *rest):
        od_ref, s0, s1, s2, x1_ref = rest[ns:ns + 5]
        scr = rest[2 * ns + 5]
        finish = (_hosted_gathers(rest[:ns], rest[ns + 5:2 * ns + 5], *rest[2 * ns + 6:], pl.program_id(0), S // tm)
                  if ns else None)
        ls = [_from_strided(scr, l) for l in (l0, l1, l2)]
        os_ = [_from_strided(scr, o) for o in (o0, o1, o2)]
        m = jnp.maximum(jnp.maximum(ls[0], ls[1]), ls[2])
        es = [jnp.exp(l - m) for l in ls]
        den = es[0] + es[1] + es[2]
        od = (es[0] * os_[0] + es[1] * os_[1] + es[2] * os_[2]) / den
        od_ref[...] = od
        _to_strided(scr, m + jnp.log(den), list(zip(DILATIONS, (s0, s1, s2))))
        osb = os_ref[...]
        mixed = jnp.concatenate([(od * _rms(od) * wd_ref[...]).astype(BF16),
                                 (osb * _rms(osb) * ws_ref[...]).astype(BF16)], axis=1)
        x1_ref[...] = x_ref[...] + _dot(mixed, w_ref[...])
        if finish is not None:
            finish()

    row = lambda i: (i, 0)
    g = pl.BlockSpec((tm, GROUP), row)
    d = pl.BlockSpec((tm, D), row)
    planes = [_strided_spec(tm, r) for r in DILATIONS]
    h_in, h_out, h_shape, h_sems = _hosted_specs(slots)
    return pl.pallas_call(
        body, name="out_proj_fwd", grid=(S // tm,),
        in_specs=planes * 2 + [g, d, _full((1, GROUP)), _full((1, GROUP)), _full((2 * GROUP, D))] + h_in,
        out_specs=[g] + planes + [d] + h_out,
        out_shape=[jax.ShapeDtypeStruct((S, GROUP), F32)] + [_strided_shape(S, r, F32) for r in DILATIONS]
                  + [jax.ShapeDtypeStruct((S, D), F32)] + h_shape,
        input_output_aliases={11 + a: 5 + a for a in range(ns)},
        scratch_shapes=[_strided_scratch(tm)] + h_sems,
    )(*o_br, *l_br, o_sb, x, w_dil, w_sbn, w_out_g, *slots)


def _ffn_fwd(x1, target, ffn_w, wg_g, wu_g, wd_g):
    S, D = x1.shape
    F = wg_g.shape[1]
    tm = 512
    nt = S // tm

    def body(x_ref, t_ref, nw_ref, wg_ref, wu_ref, wd_ref, h_ref, g_ref, u_ref, dy_ref, loss_ref, h_s, acc):
        j = pl.program_id(1)

        @pl.when(j == 0)
        def _():
            xv = x_ref[...]
            h = (xv * _rms(xv) * nw_ref[...]).astype(BF16)
            h_s[...] = h
            h_ref[...] = h
            acc[...] = xv

        h = h_s[...]
        g = _dot_nt(h, wg_ref[0])
        u = _dot_nt(h, wu_ref[0])
        g_ref[0] = g.astype(BF16)
        u_ref[0] = u.astype(BF16)
        a = (g * _sigmoid(g) * u).astype(BF16)
        acc[...] += _dot(a, wd_ref[0])

        @pl.when(j == N_CHIPS - 1)
        def _():
            err = acc[...] - t_ref[...]
            dy_ref[...] = err * (1.0 / D)
            loss_ref[...] = jnp.full(loss_ref.shape, jnp.sum(err * err), F32)

    row = lambda t, j: (t, 0)
    shard = lambda t, j: (j, 0, 0)
    act = lambda t, j: (j, t, 0)
    return pl.pallas_call(
        body, name="ffn_fwd", grid=(nt, N_CHIPS),
        in_specs=[pl.BlockSpec((tm, D), row), pl.BlockSpec((tm, D), row), pl.BlockSpec((1, D), lambda t, j: (0, 0))]
                 + [pl.BlockSpec((1, F, D), shard)] * 3,
        out_specs=[pl.BlockSpec((tm, D), row), pl.BlockSpec((1, tm, F), act), pl.BlockSpec((1, tm, F), act),
                   pl.BlockSpec((tm, D), row), pl.BlockSpec((1, 8, LANES), lambda t, j: (t, 0, 0))],
        out_shape=[jax.ShapeDtypeStruct((S, D), BF16), jax.ShapeDtypeStruct((N_CHIPS, S, F), BF16),
                   jax.ShapeDtypeStruct((N_CHIPS, S, F), BF16), jax.ShapeDtypeStruct((S, D), F32),
                   jax.ShapeDtypeStruct((nt, 8, LANES), F32)],
        scratch_shapes=[pltpu.VMEM((tm, D), BF16), pltpu.VMEM((tm, D), F32)],
    )(x1, target, ffn_w, wg_g, wu_g, wd_g)


def _ffn_bwd(h2, dy, g, u, wg_g, wu_g, wd_g):
    S, D = dy.shape
    F = wg_g.shape[1]
    tm = 512

    def body(h_ref, dy_ref, g_ref, u_ref, wg_ref, wu_ref, wd_ref, dwg_ref, dwu_ref, dwd_ref, dh_ref, *narrow):
        t = pl.program_id(1)

        @pl.when(t == 0)
        def _():
            dwg_ref[...] = jnp.zeros_like(dwg_ref)
            dwu_ref[...] = jnp.zeros_like(dwu_ref)
            dwd_ref[...] = jnp.zeros_like(dwd_ref)

        h = h_ref[...]
        dyb = dy_ref[...].astype(BF16)
        gv = g_ref[0].astype(F32)
        uv = u_ref[0].astype(F32)
        da = _dot_nt(dyb, wd_ref[0])
        sg = _sigmoid(gv)
        silu = gv * sg
        du = (da * silu).astype(BF16)
        dg = (da * uv * (sg * (1.0 + gv * (1.0 - sg)))).astype(BF16)
        dwd_ref[0] += _dot_tn((silu * uv).astype(BF16), dyb)
        dwg_ref[0] += _dot_tn(dg, h)
        dwu_ref[0] += _dot_tn(du, h)
        dh_ref[0] = (_dot(dg, wg_ref[0]) + _dot(du, wu_ref[0])).astype(BF16)

        @pl.when(t == S // tm - 1)
        def _():
            for full, half in zip((dwg_ref, dwu_ref, dwd_ref), narrow):
                half[...] = full[...].astype(BF16)

    row = lambda j, t: (t, 0)
    shard = lambda j, t: (j, 0, 0)
    act = lambda j, t: (j, t, 0)
    return pl.pallas_call(
        body, name="ffn_bwd", grid=(N_CHIPS, S // tm),
        in_specs=[pl.BlockSpec((tm, D), row), pl.BlockSpec((tm, D), row),
                  pl.BlockSpec((1, tm, F), act), pl.BlockSpec((1, tm, F), act)] + [pl.BlockSpec((1, F, D), shard)] * 3,
        out_specs=[pl.BlockSpec((1, F, D), shard)] * 3 + [pl.BlockSpec((1, tm, D), act)]
                  + [pl.BlockSpec((1, F, D), shard)] * 3,
        out_shape=[jax.ShapeDtypeStruct((N_CHIPS, F, D), F32)] * 3 + [jax.ShapeDtypeStruct((N_CHIPS, S, D), BF16)]
                  + [jax.ShapeDtypeStruct((N_CHIPS, F, D), BF16)] * 3,
    )(h2, dy, g, u, wg_g, wu_g, wd_g)


def _out_proj_bwd(dh2p, dy, x1, ffn_w, w_out_g, o_dil, o_sb, w_dil, w_sbn, seg_ones, ffn_grads):
    S, D = dy.shape
    tm = 512
    ng = len(ffn_grads)

    def body(dh_ref, dy_ref, x1_ref, nw_ref, w_ref, od_ref, os_ref, wd_ref, ws_ref, g_ref, *rest):
        gin, rest = rest[:ng], rest[ng:]
        dx1_ref, dod0, dod1, dod2, dos_ref, dl0, dl1, dl2, dw_ref, dnw_ref, dwd_ref, dws_ref = rest[:12]
        gout, (scr, send, recv) = rest[12:12 + ng], rest[12 + ng:]
        i = pl.program_id(0)
        plans = [_pair_send_plan(gin[a], gout[a], send.at[a], recv.at[a]) for a in range(ng)]

        @pl.when(i == 0)
        def _():
            for r_ in (dw_ref, dnw_ref, dwd_ref, dws_ref):
                r_[...] = jnp.zeros_like(r_)

        dh2 = _sum4(dh_ref)
        dxn, dwn = _rms_bwd(dh2, x1_ref[...], nw_ref[...])
        dnw_ref[...] += jnp.sum(dwn, axis=0, keepdims=True)
        dx1 = dy_ref[...] + dxn
        dx1_ref[...] = dx1
        dx1b = dx1.astype(BF16)
        dmix = _dot_nt(dx1b, w_ref[...])
        od = od_ref[...]
        osb = os_ref[...]
        mixed = jnp.concatenate([(od * _rms(od) * wd_ref[...]).astype(BF16),
                                 (osb * _rms(osb) * ws_ref[...]).astype(BF16)], axis=1)
        dw_ref[...] += _dot_tn(mixed, dx1b)
        do, dwo = _rms_bwd(dmix[:, :GROUP], od, wd_ref[...])
        dwd_ref[...] += jnp.sum(dwo, axis=0, keepdims=True)
        _to_strided(scr, do, list(zip(DILATIONS, (dod0, dod1, dod2))))
        _to_strided(scr, _segsum(do * od, g_ref[...]), list(zip(DILATIONS, (dl0, dl1, dl2))))
        do, dwo = _rms_bwd(dmix[:, GROUP:], osb, ws_ref[...])
        dws_ref[...] += jnp.sum(dwo, axis=0, keepdims=True)
        dos_ref[...] = do.astype(BF16)

        @pl.when(i == 0)
        def _():
            for start, _ in plans:
                start()

        @pl.when(i == S // tm - 1)
        def _():
            for _, finish in plans:
                finish()

    row = lambda i: (i, 0)
    gsp = pl.BlockSpec((tm, GROUP), row)
    dsp = pl.BlockSpec((tm, D), row)
    planes = [_strided_spec(tm, r) for r in DILATIONS]
    halves = [jax.ShapeDtypeStruct((g.shape[0], g.shape[1] // 2, g.shape[2]), g.dtype) for g in ffn_grads]
    return pl.pallas_call(
        body, name="out_proj_bwd", grid=(S // tm,),
        in_specs=[pl.BlockSpec((N_CHIPS, tm, D), lambda i: (0, i, 0)), dsp, dsp, _full((1, D)), _full((2 * GROUP, D)),
                  gsp, gsp, _full((1, GROUP)), _full((1, GROUP)), _full((GROUP, GROUP // 2))] + [HBM] * ng,
        out_specs=[dsp] + planes + [gsp] + planes
                  + [_full((2 * GROUP, D)), _full((1, D)), _full((1, GROUP)), _full((1, GROUP))] + [HBM] * ng,
        out_shape=[jax.ShapeDtypeStruct((S, D), F32)] + [_strided_shape(S, r, BF16) for r in DILATIONS]
                  + [jax.ShapeDtypeStruct((S, GROUP), BF16)] + [_strided_shape(S, r, F32) for r in DILATIONS]
                  + [jax.ShapeDtypeStruct((2 * GROUP, D), F32),
                     jax.ShapeDtypeStruct((1, D), F32), jax.ShapeDtypeStruct((1, GROUP), F32),
                     jax.ShapeDtypeStruct((1, GROUP), F32)] + halves,
        scratch_shapes=[_strided_scratch(tm), pltpu.SemaphoreType.DMA((ng,)), pltpu.SemaphoreType.DMA((ng,))],
    )(dh2p, dy, x1, ffn_w, w_out_g, o_dil, o_sb, w_dil, w_sbn, seg_ones, *ffn_grads)


def _attn_in_bwd(dq_br, dk_br, dv_br, dqs, dks, dvs, qa, ka, qw, kw, cos_t, sin_t, seg_ones, h, w_in_g, x, dx1, attn_w):
    S, D = x.shape
    wc = w_in_g.shape[2]
    tm = 256

    def body(q0, q1, q2, k0, k1, k2, v0, v1, v2, dqs_ref, dks_ref, dvs_ref, qa_ref, ka_ref, qw_ref, kw_ref,
             cos_ref, sin_ref, g_ref, h_ref, w_ref, x_ref, dx1_ref, aw_ref,
             gx_ref, dw_ref, daw_ref, dqw_ref, dkw_ref, accq, acck, scr):
        i = pl.program_id(0)

        @pl.when(i == 0)
        def _():
            accq[...] = jnp.zeros_like(accq)
            acck[...] = jnp.zeros_like(acck)
            dw_ref[...] = jnp.zeros_like(dw_ref)
            daw_ref[...] = jnp.zeros_like(daw_ref)

        def branches(refs):
            return (_from_strided(scr, refs[0]) + _from_strided(scr, refs[1])) + _from_strided(scr, refs[2])

        g = g_ref[...]
        cos = _tile4(cos_ref[...])
        sin = _tile4(sin_ref[...])
        pieces = []
        for refs, pre_ref, w_r, acc in (((q0, q1, q2), qa_ref, qw_ref, accq), ((k0, k1, k2), ka_ref, kw_ref, acck)):
            dh = branches(refs)
            dn = dh * cos + _rot_half(dh * sin)
            pre = pre_ref[...]
            rstd = lax.rsqrt(_segsum(pre * pre, g) * (1.0 / HEAD_DIM) + EPS)
            xh = pre * rstd
            acc[...] += jnp.sum(dn * xh, axis=0, keepdims=True)
            dxh = dn * w_r[...]
            pieces.append((rstd * (dxh - xh * (_segsum(dxh * xh, g) * (1.0 / HEAD_DIM)))).astype(BF16))
        pieces += [branches((v0, v1, v2)).astype(BF16), dqs_ref[...].astype(BF16), dks_ref[...].astype(BF16),
                   dvs_ref[...].astype(BF16)]
        dproj = jnp.concatenate(pieces, axis=1)
        hv = h_ref[...]
        dh = jnp.zeros((tm, D), F32)
        for j in range(N_CHIPS):
            dp = dproj[:, j * wc:(j + 1) * wc]
            dw_ref[j] += _dot_tn(hv, dp)
            dh = dh + _dot_nt(dp, w_ref[j])
        dx, dw = _rms_bwd(dh, x_ref[...], aw_ref[...])
        daw_ref[...] += jnp.sum(dw, axis=0, keepdims=True)
        gx_ref[...] = dx1_ref[...] + dx

        @pl.when(i == S // tm - 1)
        def _():
            for acc, o_ref in ((accq, dqw_ref), (acck, dkw_ref)):
                a = acc[...]
                pair = (a[:, 0:LANES] + a[:, LANES:2 * LANES]) + (a[:, 2 * LANES:3 * LANES] + a[:, 3 * LANES:4 * LANES])
                o_ref[...] = pair + pltpu.roll(pair, HEAD_DIM, 1)

    row = lambda i: (i, 0)
    gsp = pl.BlockSpec((tm, GROUP), row)
    dsp = pl.BlockSpec((tm, D), row)
    tab = pl.BlockSpec((tm, LANES), row)
    planes = [_strided_spec(tm, r) for r in DILATIONS]
    return pl.pallas_call(
        body, name="attn_in_bwd", grid=(S // tm,),
        in_specs=planes * 3 + [gsp] * 5 + [_full((1, GROUP)), _full((1, GROUP)), tab, tab, _full((GROUP, GROUP // 2)),
                                          dsp, _full((N_CHIPS, D, wc)), dsp, dsp, _full((1, D))],
        out_specs=[dsp, _full((N_CHIPS, D, wc)), _full((1, D)), _full((1, LANES)), _full((1, LANES))],
        out_shape=[jax.ShapeDtypeStruct((S, D), F32), jax.ShapeDtypeStruct((N_CHIPS, D, wc), F32),
                   jax.ShapeDtypeStruct((1, D), F32), jax.ShapeDtypeStruct((1, LANES), F32),
                   jax.ShapeDtypeStruct((1, LANES), F32)],
        scratch_shapes=[pltpu.VMEM((1, GROUP), F32), pltpu.VMEM((1, GROUP), F32), _strided_scratch(tm)],
    )(*dq_br, *dk_br, *dv_br, dqs, dks, dvs, qa, ka, qw, kw, cos_t, sin_t, seg_ones, h, w_in_g, x, dx1, attn_w)


def _constants(S):
    pos = jnp.arange(S, dtype=F32)
    inv_freq = ROPE_THETA ** (-jnp.arange(0, HEAD_DIM, 2, dtype=F32) / HEAD_DIM)
    ang_a = pos[::BLOCK, None] * inv_freq[None, :]
    ang_b = pos[:BLOCK, None] * inv_freq[None, :]
    ca, sa, cb, sb = jnp.cos(ang_a)[:, None], jnp.sin(ang_a)[:, None], jnp.cos(ang_b)[None], jnp.sin(ang_b)[None]
    cos = (ca * cb - sa * sb).reshape(S, HEAD_DIM // 2)
    sin = (sa * cb + ca * sb).reshape(S, HEAD_DIM // 2)
    cos_t = jnp.concatenate([cos, cos] * 2, axis=1)
    sin_t = jnp.concatenate([-sin, sin] * 2, axis=1)
    idx = jnp.arange(GROUP // 2)
    seg_ones = (idx[:, None] // HEAD_DIM == idx[None, :] // HEAD_DIM).astype(BF16)
    seg_ones = jnp.concatenate([seg_ones, seg_ones], axis=0)
    r = jnp.arange(BLOCK)
    ones = jnp.ones((BLOCK, BLOCK), BF16)
    tris = [jnp.concatenate([jnp.concatenate([m.astype(BF16), ones], axis=1)] * 2, axis=0) for m in
            (r[:, None] > r[None, :],
             r[:, None] <= r[None, :],
             r[:, None] < r[None, :])]
    return cos_t, sin_t, seg_ones, tris


FFN_NAMES = ("w_gate", "w_up", "w_down")


def _device_step(x, target, attn_w, qn_w, kn_w, dil_w, sbn_w, ffn_w, w_in_g, w_out_slots, ffn_slots, core, chip):
    S = x.shape[0]
    cos_t, sin_t, seg_ones, (tri_later, tri_upto, tri_before) = _constants(S)
    reps = GROUP // HEAD_DIM
    qw = jnp.tile(qn_w, (1, reps))
    kw = jnp.tile(kn_w, (1, reps))

    nd = len(DILATIONS)
    h, qa, ka, *rest, wg_g = _in_proj_fwd(x, attn_w, w_in_g, qw, kw, cos_t, sin_t, seg_ones, ffn_slots[:1])
    qh, kh, va, (qs, ks, vs) = rest[:nd], rest[nd:2 * nd], rest[2 * nd:3 * nd], rest[3 * nd:]
    hosted = ([], [w_out_slots], [])
    branches = [_dil_fwd(qh[b], kh[b], va[b], hosted[b]) for b in range(nd)]
    w_out_g = branches[1][2].reshape(-1, x.shape[1])
    o_sb, ltot, walked, wu_g, wd_g = _sb_fwd(qs, ks, vs, tri_later, ffn_slots[1:])
    o_dil, *lse, x1 = _out_proj_fwd([b[0] for b in branches], [b[1] for b in branches], o_sb, x, dil_w, sbn_w,
                                    w_out_g, [])
    h2, g, u, dy, loss_parts = _ffn_fwd(x1, target, ffn_w, wg_g, wu_g, wd_g)

    *ffn_grads, dh2p, n0, n1, n2 = _ffn_bwd(h2, dy, g, u, wg_g, wu_g, wd_g)
    dx1, *mid, dw_out, dffn_w, ddil_w, dsbn_w, p0, p1, p2 = _out_proj_bwd(
        dh2p, dy, x1, ffn_w, w_out_g, o_dil, o_sb, dil_w, sbn_w, seg_ones, [n0, n1, n2])
    do_dil, do_sb, delta = mid[:nd], mid[nd], mid[nd + 1:]
    parts = [_pair_sum(gr, fr, core, n) for gr, fr, n in zip(ffn_grads, (p0, p1, p2), FFN_NAMES)]
    dqs, dks, dvs = _sb_bwd(walked[:, 0, 0], qs, ks, vs, do_sb, ltot, tri_upto, tri_before)
    dw_out = dw_out.reshape(N_CHIPS, -1, x.shape[1])
    dbr = [None] * nd
    dbr[0] = _dil_bwd(qh[0], kh[0], va[0], do_dil[0], lse[0], delta[0], [parts[0]], [dw_out])
    out_part = _pair_sum(dw_out, dbr[0][4], core, "w_out")
    dbr[1] = _dil_bwd(qh[1], kh[1], va[1], do_dil[1], lse[1], delta[1], [parts[1], out_part], [])
    dbr[2] = _dil_bwd(qh[2], kh[2], va[2], do_dil[2], lse[2], delta[2], [parts[2]], [])
    ffn_halves = [_chip_sum(dbr[b][3], parts[b], chip, FFN_NAMES[b]) for b in range(nd)]
    w_out_half = _chip_sum(dbr[1][4], out_part, chip, "w_out")
    grad_x, dw_in, dattn_w, dqw, dkw = _attn_in_bwd(
        [b[0] for b in dbr], [b[1] for b in dbr], [b[2] for b in dbr], dqs, dks, dvs,
        qa, ka, qw, kw, cos_t, sin_t, seg_ones, h, w_in_g, x, dx1, attn_w)
    small = dict(attn=dattn_w, q=dqw[:, :HEAD_DIM], k=dkw[:, :HEAD_DIM], dil=ddil_w, sb=dsbn_w, ffn=dffn_w)
    return loss_parts, grad_x, small, dw_in, w_out_half, ffn_halves


HBM = pl.BlockSpec(memory_space=pltpu.HBM)
VMEM = pl.BlockSpec(memory_space=pltpu.VMEM)
CHIP_FLIPS = ((1, 0), (0, 1), (1, 1))


def _place():
    return lax.axis_index("x"), lax.axis_index("y"), lax.axis_index("c")


def _flip(v, d):
    return 1 - v if d else v


def _half_rows(c, n):
    return pl.ds(pl.multiple_of(c * (n // 2), 16), n // 2)


def _gather_plan(slot_in, slot_out, send, recv):
    x, y, c = _place()
    p = 2 * x + y
    chips = [(_flip(x, dx), _flip(y, dy)) for dx, dy in CHIP_FLIPS]
    mine, other = _half_rows(c, slot_in.shape[1]), _half_rows(1 - c, slot_in.shape[1])

    def copy(k, src, dst, to):
        return pltpu.make_async_remote_copy(src_ref=src, dst_ref=dst, send_sem=send.at[k], recv_sem=recv.at[k],
                                            device_id=to, device_id_type=MESH)

    def first(k):
        return copy(k, slot_in.at[p, mine], slot_out.at[p, mine], (*chips[k], c))

    def passed(k, rows):
        land = slot_out.at[2 * chips[k][0] + chips[k][1], rows]
        return copy(3 + k, land, land, (x, y, 1 - c))

    def start():
        for k in range(3):
            first(k).start()

    def forward():
        for k in range(3):
            land = slot_out.at[2 * chips[k][0] + chips[k][1], mine]
            copy(k, land, land, (*chips[k], c)).wait_recv()
            passed(k, mine).start()

    def finish():
        for k in range(3):
            passed(k, other).wait_recv()
        for k in range(3):
            first(k).wait_send()
            passed(k, mine).wait_send()

    return start, forward, finish


def _chip_send_plan(part_in, recv_out, send, recv):
    x, y, c = _place()
    p = 2 * x + y
    chips = [(_flip(x, dx), _flip(y, dy)) for dx, dy in CHIP_FLIPS]

    def copy(k):
        q = 2 * chips[k][0] + chips[k][1]
        return pltpu.make_async_remote_copy(src_ref=part_in.at[q], dst_ref=recv_out.at[p], send_sem=send.at[k],
                                            recv_sem=recv.at[k], device_id=(*chips[k], c), device_id_type=MESH)

    def start():
        for k in range(3):
            copy(k).start()

    def finish():
        for k in range(3):
            land = recv_out.at[2 * chips[k][0] + chips[k][1]]
            pltpu.make_async_remote_copy(src_ref=land, dst_ref=land, send_sem=send.at[k], recv_sem=recv.at[k],
                                         device_id=(*chips[k], c), device_id_type=MESH).wait_recv()
        for k in range(3):
            copy(k).wait_send()

    return start, finish


def _pair_send_plan(grad_in, recv_out, send, recv):
    x, y, c = _place()

    def copy():
        theirs = _half_rows(1 - c, grad_in.shape[1])
        return pltpu.make_async_remote_copy(src_ref=grad_in.at[:, theirs, :], dst_ref=recv_out, send_sem=send,
                                            recv_sem=recv, device_id=(x, y, 1 - c), device_id_type=MESH)

    return (lambda: copy().start()), (lambda: copy().wait())


def _own_slots(shard):
    here = 2 * lax.axis_index("x") + lax.axis_index("y")
    return lax.dynamic_update_slice(lax.empty((N_CHIPS,) + shard.shape, shard.dtype), shard[None], (here, 0, 0))


def _gather_weights(shards):
    n = len(shards)

    def body(*refs):
        ins, outs = refs[:n], refs[n:2 * n]
        send, recv = refs[2 * n:]
        plans = [_gather_plan(ins[a], outs[a], send.at[pl.ds(6 * a, 6)], recv.at[pl.ds(6 * a, 6)]) for a in range(n)]
        for stage in range(3):
            for plan in plans:
                plan[stage]()

    slots = [_own_slots(s) for s in shards]
    return pl.pallas_call(
        body, name="gather_weights", in_specs=[HBM] * n, out_specs=[HBM] * n,
        out_shape=[jax.ShapeDtypeStruct(s.shape, s.dtype) for s in slots],
        input_output_aliases={a: a for a in range(n)},
        scratch_shapes=[pltpu.SemaphoreType.DMA((6 * n,)), pltpu.SemaphoreType.DMA((6 * n,))],
    )(*slots)


def _pair_exchange(grads, small):
    n = len(grads)

    def body(*refs):
        gin, sm = refs[:n], refs[n]
        gout, sm_all = refs[n + 1:2 * n + 1], refs[2 * n + 1]
        send, recv = refs[2 * n + 2:]
        x, y, c = _place()
        me = 4 * x + 2 * y + c
        big = [_pair_send_plan(gin[a], gout[a], send.at[a], recv.at[a]) for a in range(n)]
        for start, _ in big:
            start()
        sm_all[pl.ds(me, 1)] = sm[...][None]
        tiny = []
        for k in range(1, N_DEV):
            px, py, pc = _flip(x, k & 4), _flip(y, k & 2), _flip(c, k & 1)
            tiny.append((pltpu.make_async_remote_copy(
                src_ref=sm, dst_ref=sm_all.at[me], send_sem=send.at[n + k - 1], recv_sem=recv.at[n + k - 1],
                device_id=(px, py, pc), device_id_type=MESH), 4 * px + 2 * py + pc))
            tiny[-1][0].start()
        for k, (cp, peer) in enumerate(tiny):
            pltpu.make_async_remote_copy(src_ref=sm, dst_ref=sm_all.at[peer], send_sem=send.at[n + k],
                                         recv_sem=recv.at[n + k], device_id=(x, y, c),
                                         device_id_type=MESH).wait_recv()
            cp.wait_send()
        for _, finish in big:
            finish()

    halves = [jax.ShapeDtypeStruct((g.shape[0], g.shape[1] // 2, g.shape[2]), g.dtype) for g in grads]
    return pl.pallas_call(
        body, name="pair_exchange", in_specs=[HBM] * n + [VMEM], out_specs=[HBM] * n + [VMEM],
        out_shape=halves + [jax.ShapeDtypeStruct((N_DEV,) + small.shape, small.dtype)],
        scratch_shapes=[pltpu.SemaphoreType.DMA((n + N_DEV - 1,)), pltpu.SemaphoreType.DMA((n + N_DEV - 1,))],
    )(*grads, small)


def _chip_exchange(parts):
    n = len(parts)

    def body(*refs):
        pin, pout = refs[:n], refs[n:2 * n]
        send, recv = refs[2 * n:]
        plans = [_chip_send_plan(pin[a], pout[a], send.at[pl.ds(3 * a, 3)], recv.at[pl.ds(3 * a, 3)]) for a in range(n)]
        for stage in range(2):
            for plan in plans:
                plan[stage]()

    return pl.pallas_call(
        body, name="chip_exchange", in_specs=[HBM] * n, out_specs=[HBM] * n,
        out_shape=[jax.ShapeDtypeStruct(s.shape, s.dtype) for s in parts],
        scratch_shapes=[pltpu.SemaphoreType.DMA((3 * n,)), pltpu.SemaphoreType.DMA((3 * n,))],
    )(*parts)


def _pair_swap_plan(hin, hout, send, recv):
    x, y, c = _place()

    def copies():
        return [pltpu.make_async_remote_copy(src_ref=hin[a], dst_ref=hout[a], send_sem=send.at[a], recv_sem=recv.at[a],
                                             device_id=(x, y, 1 - c), device_id_type=MESH) for a in range(len(hin))]

    def start():
        for cp in copies():
            cp.start()

    def finish():
        for cp in copies():
            cp.wait()

    return start, finish


def _pair_swap(halves):
    n = len(halves)

    def body(*refs):
        start, finish = _pair_swap_plan(refs[:n], refs[n:2 * n], *refs[2 * n:])
        start()
        finish()

    return pl.pallas_call(
        body, name="pair_swap", in_specs=[HBM] * n, out_specs=[HBM] * n,
        out_shape=[jax.ShapeDtypeStruct(s.shape, s.dtype) for s in halves],
        scratch_shapes=[pltpu.SemaphoreType.DMA((n,)), pltpu.SemaphoreType.DMA((n,))],
    )(*halves)


def _pair_sum(grad, recv, c, tag):
    _, R, C = grad.shape
    hr = R // 2

    def body(c_ref, a_ref, b_ref, o_ref):
        o_ref[...] = (a_ref[...] + b_ref[...]).astype(BF16)

    return pl.pallas_call(
        body, name="pair_sum_" + tag,
        grid_spec=pltpu.PrefetchScalarGridSpec(
            num_scalar_prefetch=1, grid=(N_CHIPS,),
            in_specs=[pl.BlockSpec((1, hr, C), lambda s, cr: (s, cr[0], 0)),
                      pl.BlockSpec((1, hr, C), lambda s, cr: (s, 0, 0))],
            out_specs=pl.BlockSpec((1, hr, C), lambda s, cr: (s, 0, 0))),
        out_shape=jax.ShapeDtypeStruct((N_CHIPS, hr, C), BF16),
    )(c, grad, recv)


def _chip_sum(received, own, chip, tag):
    _, rows, C = received.shape
    tr = rows // 2

    def body(chip_ref, own_ref, r1_ref, r2_ref, r3_ref, o_ref):
        p = [r[0].astype(F32) for r in (own_ref, r1_ref, r2_ref, r3_ref)]
        o_ref[...] = (p[0] + p[1]) + (p[2] + p[3])

    def slot(k):
        return pl.BlockSpec((1, tr, C), lambda i, cr: (jnp.bitwise_xor(cr[0], k), i, 0))

    return pl.pallas_call(
        body, name="chip_sum_" + tag,
        grid_spec=pltpu.PrefetchScalarGridSpec(
            num_scalar_prefetch=1, grid=(rows // tr,), in_specs=[slot(0), slot(1), slot(2), slot(3)],
            out_specs=pl.BlockSpec((tr, C), lambda i, cr: (i, 0))),
        out_shape=jax.ShapeDtypeStruct((rows, C), F32),
    )(chip, own, received, received, received)


def _adamw_math(w, g, m, v):
    m = ADAM_B1 * m + (1.0 - ADAM_B1) * g
    v = ADAM_B2 * v + (1.0 - ADAM_B2) * (g * g)
    m_hat = m / (1.0 - ADAM_B1 ** ADAM_STEP)
    v_hat = v / (1.0 - ADAM_B2 ** ADAM_STEP)
    delta = -ADAM_LR * (m_hat / (jnp.sqrt(v_hat) + ADAM_EPS) + ADAM_WD * w)
    return delta, m, v


def _adamw(w, g_mine, g_other, m, v, c, tag):
    R, C = w.shape

    def body(c_ref, w_ref, gm_ref, go_ref, m_ref, v_ref, g_ref, d_ref, nm_ref, nv_ref):
        g = jnp.where(pl.program_id(0) == c_ref[0], gm_ref[...], go_ref[...])
        g_ref[...] = g
        d_ref[...], nm_ref[...], nv_ref[...] = _adamw_math(w_ref[...], g, m_ref[...], v_ref[...])

    blk = pl.BlockSpec((R // 2, C), lambda h, cr: (h, 0))
    half = pl.BlockSpec((R // 2, C), lambda h, cr: (0, 0))
    return pl.pallas_call(
        body, name="adamw_" + tag,
        grid_spec=pltpu.PrefetchScalarGridSpec(
            num_scalar_prefetch=1, grid=(2,), in_specs=[blk, half, half, blk, blk], out_specs=[blk] * 4),
        out_shape=[jax.ShapeDtypeStruct((R, C), F32)] * 4,
    )(c, w, g_mine, g_other, m, v)


def _small_update(all_small, w, m, v):
    def body(a_ref, w_ref, m_ref, v_ref, g_ref, d_ref, nm_ref, nv_ref):
        g = ((a_ref[0] + a_ref[1]) + (a_ref[2] + a_ref[3])) + ((a_ref[4] + a_ref[5]) + (a_ref[6] + a_ref[7]))
        g_ref[...] = g
        d_ref[...], nm_ref[...], nv_ref[...] = _adamw_math(w_ref[...], g, m_ref[...], v_ref[...])

    return pl.pallas_call(
        body, name="small_update", out_shape=[jax.ShapeDtypeStruct(w.shape, F32)] * 4,
    )(all_small, w, m, v)


SMALL_ROWS = (("attn", 0, 0), ("ffn", 1, 0), ("dil", 2, 0), ("sb", 2, GROUP), ("q", 3, 0), ("k", 3, HEAD_DIM),
              ("loss", 4, 0))


def _pack_small(vals, D):
    rows = [jnp.zeros((1, D), F32) for _ in range(8)]
    for name, r, off in SMALL_ROWS:
        if name in vals:
            rows[r] = lax.dynamic_update_slice(rows[r], vals[name].astype(F32), (0, off))
    return jnp.concatenate(rows, axis=0)


def _unpack_small(packed, vals):
    return {name: packed[r:r + 1, off:off + vals[name].shape[1]] for name, r, off in SMALL_ROWS if name in vals}


def kernel(x, attn_norm_w, w_in, q_norm_w, k_norm_w, dil_out_norm_w, sb_out_norm_w, w_out, ffn_norm_w, w_gate, w_up, w_down, loss_target, m_attn_norm_w, m_w_in, m_q_norm_w, m_k_norm_w, m_dil_out_norm_w, m_sb_out_norm_w, m_w_out, m_ffn_norm_w, m_w_gate, m_w_up, m_w_down, v_attn_norm_w, v_w_in, v_q_norm_w, v_k_norm_w, v_dil_out_norm_w, v_sb_out_norm_w, v_w_out, v_ffn_norm_w, v_w_gate, v_w_up, v_w_down):
    D = x.shape[-1]
    big_names = ("w_in", "w_out", "w_gate", "w_up", "w_down")
    flipped = ("w_gate", "w_up")
    tr = lambda a: jnp.swapaxes(a[0], 0, 1)
    big_w = dict(w_in=w_in[0], w_out=w_out[0], w_gate=tr(w_gate), w_up=tr(w_up), w_down=w_down[0])
    big_m = dict(w_in=m_w_in[0], w_out=m_w_out[0], w_gate=tr(m_w_gate), w_up=tr(m_w_up), w_down=m_w_down[0])
    big_v = dict(w_in=v_w_in[0], w_out=v_w_out[0], w_gate=tr(v_w_gate), w_up=tr(v_w_up), w_down=v_w_down[0])
    small_w = dict(attn=attn_norm_w, q=q_norm_w, k=k_norm_w, dil=dil_out_norm_w, sb=sb_out_norm_w, ffn=ffn_norm_w)
    small_m = dict(attn=m_attn_norm_w, q=m_q_norm_w, k=m_k_norm_w, dil=m_dil_out_norm_w, sb=m_sb_out_norm_w,
                   ffn=m_ffn_norm_w)
    small_v = dict(attn=v_attn_norm_w, q=v_q_norm_w, k=v_k_norm_w, dil=v_dil_out_norm_w, sb=v_sb_out_norm_w,
                   ffn=v_ffn_norm_w)

    c = lax.axis_index("c").astype(jnp.int32).reshape(1)
    chip = (2 * lax.axis_index("x") + lax.axis_index("y")).astype(jnp.int32).reshape(1)
    (w_in_g,) = _gather_weights([big_w["w_in"].astype(BF16)])
    w_out_slots = _own_slots(big_w["w_out"].astype(BF16))
    ffn_slots = [_own_slots(big_w[n].astype(BF16)) for n in FFN_NAMES]

    loss_parts, grad_x, small_g, dw_in, w_out_half, ffn_halves = _device_step(
        x[0], loss_target[0], attn_norm_w, q_norm_w, k_norm_w, dil_out_norm_w, sb_out_norm_w, ffn_norm_w,
        w_in_g, w_out_slots, ffn_slots, c, chip)
    small_g["loss"] = (jnp.sum(loss_parts[:, 0, 0]) * (0.5 / D)).reshape(1, 1)

    from_pair, all_small = _pair_exchange([dw_in], _pack_small(small_g, D))
    in_part = _pair_sum(dw_in, from_pair, c, "w_in")
    (from_chips,) = _chip_exchange([in_part])
    halves = [_chip_sum(from_chips, in_part, chip, "w_in"), w_out_half] + ffn_halves
    others = _pair_swap(halves)
    big_out = {n: _adamw(big_w[n], mine, other, big_m[n], big_v[n], c, n)
               for n, mine, other in zip(big_names, halves, others)}
    sg, sd, sm, sv = _small_update(all_small, _pack_small(small_w, D), _pack_small(small_m, D),
                                   _pack_small(small_v, D))
    small_out = [_unpack_small(t, small_w) for t in (sg, sd, sm, sv)]

    order = (("attn", None), (None, "w_in"), ("q", None), ("k", None), ("dil", None), ("sb", None),
             (None, "w_out"), ("ffn", None), (None, "w_gate"), (None, "w_up"), (None, "w_down"))
    outs = [sg[4, 0], grad_x[None]]
    for kind in range(4):
        for s_name, b_name in order:
            if s_name is not None:
                outs.append(small_out[kind][s_name])
            else:
                res = big_out[b_name][kind]
                outs.append((jnp.swapaxes(res, 0, 1) if b_name in flipped else res)[None])
    return tuple(outs)
```

```python
import functools

import jax
import jax.numpy as jnp
from jax import lax
from jax.experimental import pallas as pl
from jax.experimental.pallas import tpu as pltpu

F32 = jnp.float32
BF16 = jnp.bfloat16
MESH = pl.DeviceIdType.MESH

HEAD_DIM = 64
GROUP = 512
BLOCK = 128
LANES = 128
N_CHIPS = 4
N_DEV = 8
EPS = 1e-6
ROPE_THETA = 10000.0
DILATIONS = (1, 4, 16)
NEG = -1e30

ADAM_LR = 0.001
ADAM_B1 = 0.9
ADAM_B2 = 0.999
ADAM_EPS = 1e-08
ADAM_WD = 0.01
ADAM_STEP = 10


def _dot(a, b):
    return jnp.dot(a, b, preferred_element_type=F32)


def _dot_nt(a, b):
    return lax.dot_general(a, b, (((1,), (1,)), ((), ())), preferred_element_type=F32)


def _dot_tn(a, b):
    return lax.dot_general(a, b, (((0,), (0,)), ((), ())), preferred_element_type=F32)


def _split(v):
    hi = lax.bitcast_convert_type(lax.bitcast_convert_type(v, jnp.uint32) & jnp.uint32(0xFFFF0000), F32)
    return hi.astype(BF16), (v - hi).astype(BF16)


def _segsum(v, g):
    hi, lo = _split(v)
    w = g.shape[1]
    return jnp.concatenate([_dot(jnp.concatenate([hi[:, c:c + w], lo[:, c:c + w]], axis=1), g)
                            for c in range(0, v.shape[1], w)], axis=1)


def _rot_half(x):
    outs = []
    for c in range(x.shape[1] // LANES):
        xc = x[:, c * LANES:(c + 1) * LANES]
        lane = lax.broadcasted_iota(jnp.int32, xc.shape, 1)
        first = (lane % HEAD_DIM) < (HEAD_DIM // 2)
        outs.append(jnp.where(first, pltpu.roll(xc, LANES - 32, 1), pltpu.roll(xc, 32, 1)))
    return outs[0] if len(outs) == 1 else jnp.concatenate(outs, axis=1)


def _rms(x):
    return lax.rsqrt(jnp.mean(x * x, axis=-1, keepdims=True) + EPS)


def _rms_bwd(dy, x, w):
    rstd = _rms(x)
    xh = x * rstd
    dxh = dy * w
    dx = rstd * (dxh - xh * jnp.mean(dxh * xh, axis=-1, keepdims=True))
    return dx, dy * xh


def _sigmoid(x):
    return 1.0 / (1.0 + jnp.exp(-x))


def _sum4(ref):
    p = [ref[j].astype(F32) for j in range(N_CHIPS)]
    return (p[0] + p[1]) + (p[2] + p[3])


def _full(shape):
    n = len(shape)
    return pl.BlockSpec(shape, lambda *_: (0,) * n)


def _strided_spec(tm, r):
    return pl.BlockSpec((r, tm // r, GROUP), lambda i: (0, i, 0))


def _strided_shape(S, r, dtype):
    return jax.ShapeDtypeStruct((r, S // r, GROUP), dtype)


def _to_strided(scr, val, outs):
    chunks = range(GROUP // LANES)
    for k in chunks:
        scr[k] = val[:, _lanes(k)]
    for r, o_ref in outs:
        if r == 1:
            o_ref[0] = val.astype(o_ref.dtype)
            continue
        n = val.shape[0] // r
        for c in range(r):
            rows = pl.ds(c, n, stride=r)
            o_ref[c] = jnp.concatenate([scr.at[k][rows, :] for k in chunks], axis=1).astype(o_ref.dtype)


def _from_strided(scr, ref):
    r, n, _ = ref.shape
    if r == 1:
        return ref[0].astype(F32)
    chunks = range(GROUP // LANES)
    for c in range(r):
        plane = ref[c].astype(F32)
        for k in chunks:
            scr.at[k][pl.ds(c, n, stride=r), :] = plane[:, _lanes(k)]
    return jnp.concatenate([scr[k] for k in chunks], axis=1)


def _strided_scratch(tm):
    return pltpu.VMEM((GROUP // LANES, tm, LANES), F32)


def _tile4(t):
    return jnp.concatenate([t] * (GROUP // LANES), axis=1)


def _in_proj_fwd(x, attn_w, w_in_g, qw, kw, cos_t, sin_t, seg_ones, slots):
    S, D = x.shape
    tm = 512
    wcols = w_in_g.shape[2]
    nd = len(DILATIONS)
    ns = len(slots)

    def body(x_ref, aw_ref, w_ref, qw_ref, kw_ref, cos_ref, sin_ref, g_ref, *rest):
        slot_in, (h_ref, qa_ref, ka_ref), rest = rest[:ns], rest[ns:ns + 3], rest[ns + 3:]
        q_refs, k_refs, v_refs = rest[:nd], rest[nd:2 * nd], rest[2 * nd:3 * nd]
        qs_ref, ks_ref, vs_ref = rest[3 * nd:3 * nd + 3]
        slot_out, scr, sems = rest[3 * nd + 3:3 * nd + 3 + ns], rest[3 * nd + 3 + ns], rest[3 * nd + 4 + ns:]
        finish = _hosted_gathers(slot_in, slot_out, *sems, pl.program_id(0), S // tm) if ns else None
        xv = x_ref[...]
        h = (xv * _rms(xv) * aw_ref[...]).astype(BF16)
        h_ref[...] = h
        proj = jnp.concatenate([_dot(h, w_ref[j]) for j in range(N_CHIPS)], axis=1)
        qa = proj[:, 0 * GROUP:1 * GROUP]
        ka = proj[:, 1 * GROUP:2 * GROUP]
        qa_ref[...] = qa
        ka_ref[...] = ka
        _to_strided(scr, proj[:, 2 * GROUP:3 * GROUP], list(zip(DILATIONS, v_refs)))
        qs_ref[...] = proj[:, 3 * GROUP:4 * GROUP].astype(BF16)
        ks_ref[...] = proj[:, 4 * GROUP:5 * GROUP].astype(BF16)
        vs_ref[...] = proj[:, 5 * GROUP:6 * GROUP].astype(BF16)
        g = g_ref[...]
        cos = _tile4(cos_ref[...])
        sin = _tile4(sin_ref[...])
        for t, w_r, o_rs in ((qa, qw_ref, q_refs), (ka, kw_ref, k_refs)):
            rstd = lax.rsqrt(_segsum(t * t, g) * (1.0 / HEAD_DIM) + EPS)
            tn = t * rstd * w_r[...]
            _to_strided(scr, tn * cos + _rot_half(tn) * sin, list(zip(DILATIONS, o_rs)))
        if finish is not None:
            finish()

    row = lambda i: (i, 0)
    tile = lambda n, dt: jax.ShapeDtypeStruct((S, n), dt)
    planes = [_strided_spec(tm, r) for r in DILATIONS]
    h_in, h_out, h_shape, h_sems = _hosted_specs(slots)
    n_out = 6 + 3 * nd
    return pl.pallas_call(
        body, name="in_proj_fwd", grid=(S // tm,),
        in_specs=[pl.BlockSpec((tm, D), row), _full((1, D)), _full((N_CHIPS, D, wcols)),
                  _full((1, GROUP)), _full((1, GROUP)),
                  pl.BlockSpec((tm, LANES), row), pl.BlockSpec((tm, LANES), row),
                  _full((GROUP, GROUP // 2))] + h_in,
        out_specs=[pl.BlockSpec((tm, D), row)] + [pl.BlockSpec((tm, GROUP), row)] * 2 + planes * 3
                  + [pl.BlockSpec((tm, GROUP), row)] * 3 + h_out,
        out_shape=[tile(D, BF16), tile(GROUP, F32), tile(GROUP, F32)]
                  + [_strided_shape(S, r, BF16) for r in DILATIONS] * 3 + [tile(GROUP, BF16)] * 3 + h_shape,
        input_output_aliases={8 + a: n_out + a for a in range(ns)},
        scratch_shapes=[_strided_scratch(tm)] + h_sems,
    )(x, attn_w, w_in_g, qw, kw, cos_t, sin_t, seg_ones, *slots)


DIL_PLANES = 1


def _dil_fwd(q, k, v, slots):
    r, L, _ = q.shape
    nb = L // BLOCK
    P = GROUP // LANES
    PL = min(DIL_PLANES, r)
    ns = len(slots)
    units = [(pp, hp) for pp in range(PL) for hp in range(P)]

    def body(q_ref, kc_ref, kp_ref, vc_ref, vp_ref, *rest):
        o_ref, l_ref = rest[ns:ns + 2]
        n = pl.program_id(1)
        finish = (_hosted_gathers(rest[:ns], rest[ns + 2:2 * ns + 2], *rest[2 * ns + 2:],
                                  pl.program_id(0) * nb + n, (r // PL) * nb) if ns else None)
        rowi = lax.broadcasted_iota(jnp.int32, (BLOCK, BLOCK), 0)
        coli = lax.broadcasted_iota(jnp.int32, (BLOCK, BLOCK), 1)
        first = coli < HEAD_DIM
        masks = (coli <= rowi, jnp.logical_and(coli >= rowi, n > 0))
        s2 = {}
        for pp, hp in units:
            q2 = _scaled(q_ref[pp, :, _lanes(hp)])
            for b, k_ref in enumerate((kc_ref, kp_ref)):
                s2[pp, hp, b] = _dot_nt(q2, _by_head(k_ref[pp, :, _lanes(hp)], first))
        ps, inv, lse = {}, {}, {}
        for pp, hp in units:
            for h in range(2):
                s = [jnp.where(masks[b], s2[pp, hp, b][:, h * BLOCK:(h + 1) * BLOCK], NEG) for b in range(2)]
                m = jnp.maximum(jnp.max(s[0], axis=1, keepdims=True), jnp.max(s[1], axis=1, keepdims=True))
                p = [jnp.exp(s[b] - m) for b in range(2)]
                den = jnp.sum(p[0], axis=1, keepdims=True) + jnp.sum(p[1], axis=1, keepdims=True)
                ps[pp, hp, h] = [p[b].astype(BF16) for b in range(2)]
                inv[pp, hp, h] = 1.0 / den
                lse[pp, hp, h] = m + jnp.log(den)
        for pp, hp in units:
            o = jnp.zeros((BLOCK, LANES), F32)
            for b, v_ref in enumerate((vc_ref, vp_ref)):
                o = o + _dot(jnp.concatenate([ps[pp, hp, 0][b], ps[pp, hp, 1][b]], axis=1),
                             _by_head(v_ref[pp, :, _lanes(hp)], first))
            o_ref[pp, :, _lanes(hp)] = (o * jnp.where(first, inv[pp, hp, 0], inv[pp, hp, 1])).astype(BF16)
            l_ref[pp, :, _lanes(hp)] = jnp.where(first, lse[pp, hp, 0], lse[pp, hp, 1])
        if finish is not None:
            finish()

    cur = pl.BlockSpec((PL, BLOCK, GROUP), lambda c, n: (c, n, 0))
    prev = pl.BlockSpec((PL, BLOCK, GROUP), lambda c, n: (c, jnp.maximum(n - 1, 0), 0))
    h_in, h_out, h_shape, h_sems = _hosted_specs(slots)
    return pl.pallas_call(
        body, name="dil_fwd_r%d" % r, grid=(r // PL, nb),
        in_specs=[cur, cur, prev, cur, prev] + h_in, out_specs=[cur, cur] + h_out,
        out_shape=[jax.ShapeDtypeStruct(q.shape, BF16), jax.ShapeDtypeStruct(q.shape, F32)] + h_shape,
        input_output_aliases={5 + a: 2 + a for a in range(ns)},
        scratch_shapes=h_sems,
    )(q, k, k, v, v, *slots)


def _dil_bwd(q, k, v, do, lse, delta, parts, pairs):
    r, L, _ = q.shape
    nb = L // BLOCK
    P = GROUP // LANES
    PL = min(DIL_PLANES, r)
    scale = HEAD_DIM ** -0.5
    units = [(pp, hp) for pp in range(PL) for hp in range(P)]
    npt, npr = len(parts), len(pairs)
    nx = npt + npr

    def body(qc_ref, qn_ref, doc_ref, don_ref, lc_ref, ln_ref, dc_ref, dn_ref, k_ref, v_ref, *rest):
        x_in, (dq_ref, dk_ref, dv_ref), x_out = rest[:nx], rest[nx:nx + 3], rest[nx + 3:2 * nx + 3]
        carry, send, recv = rest[2 * nx + 3:]
        j = pl.program_id(1)
        step = pl.program_id(0) * nb + j
        plans = [_chip_send_plan(x_in[a], x_out[a], send.at[pl.ds(3 * a, 3)], recv.at[pl.ds(3 * a, 3)])
                 for a in range(npt)]
        plans += [_pair_send_plan(x_in[npt + a], x_out[npt + a], send.at[3 * npt + a], recv.at[3 * npt + a])
                  for a in range(npr)]

        def start():
            for begin, _ in plans:
                begin()

        def finish():
            for _, end in plans:
                end()

        rowi = lax.broadcasted_iota(jnp.int32, (BLOCK, BLOCK), 0)
        coli = lax.broadcasted_iota(jnp.int32, (BLOCK, BLOCK), 1)
        first = coli < HEAD_DIM
        sides = ((qc_ref, doc_ref, lc_ref, dc_ref, coli <= rowi),
                 (qn_ref, don_ref, ln_ref, dn_ref, jnp.logical_and(coli >= rowi, j < nb - 1)))

        @pl.when(j == 0)
        def _():
            carry[...] = jnp.zeros_like(carry)

        kcat, q2, do2, s2, dp2 = {}, {}, {}, {}, {}
        for pp, hp in units:
            kcat[pp, hp] = _by_head(k_ref[pp, :, _lanes(hp)], first)
            vcat = _by_head(v_ref[pp, :, _lanes(hp)], first)
            for x, (q_r, do_r, _, _, _) in enumerate(sides):
                q2[pp, hp, x] = _scaled(q_r[pp, :, _lanes(hp)])
                do2[pp, hp, x] = do_r[pp, :, _lanes(hp)]
                s2[pp, hp, x] = _dot_nt(q2[pp, hp, x], kcat[pp, hp])
                dp2[pp, hp, x] = _dot_nt(do2[pp, hp, x], vcat)
        pcat, dscat = {}, {}
        for pp, hp in units:
            for x, (_, _, l_r, d_r, msk) in enumerate(sides):
                ps, dss = [], []
                for h in range(2):
                    col = hp * LANES + h * HEAD_DIM
                    half = slice(h * BLOCK, (h + 1) * BLOCK)
                    p = jnp.where(msk, jnp.exp(s2[pp, hp, x][:, half] - l_r[pp, :, col:col + 1]), 0.0)
                    ps.append(p.astype(BF16))
                    dss.append((p * (dp2[pp, hp, x][:, half] - d_r[pp, :, col:col + 1])).astype(BF16))
                pcat[pp, hp, x] = jnp.concatenate(ps, axis=1)
                dscat[pp, hp, x] = jnp.concatenate(dss, axis=1)
        for pp, hp in units:
            dv2 = _dot_tn(pcat[pp, hp, 0], do2[pp, hp, 0]) + _dot_tn(pcat[pp, hp, 1], do2[pp, hp, 1])
            dk2 = _dot_tn(dscat[pp, hp, 0], q2[pp, hp, 0]) + _dot_tn(dscat[pp, hp, 1], q2[pp, hp, 1])
            dv_ref[pp, :, _lanes(hp)] = jnp.where(first, dv2[:BLOCK], dv2[BLOCK:]).astype(BF16)
            dk_ref[pp, :, _lanes(hp)] = jnp.where(first, dk2[:BLOCK], dk2[BLOCK:]).astype(BF16)
            dq_ref[pp, :, _lanes(hp)] = (carry[pp, :, _lanes(hp)]
                                         + _dot(dscat[pp, hp, 0], kcat[pp, hp]) * scale).astype(BF16)
            carry[pp, :, _lanes(hp)] = _dot(dscat[pp, hp, 1], kcat[pp, hp]) * scale
        pl.when(step == 0)(start)
        pl.when(step == (r // PL) * nb - 1)(finish)

    cur = pl.BlockSpec((PL, BLOCK, GROUP), lambda c, n: (c, n, 0))
    nxt = pl.BlockSpec((PL, BLOCK, GROUP), lambda c, n: (c, jnp.minimum(n + 1, nb - 1), 0))
    return pl.pallas_call(
        body, name="dil_bwd_r%d" % r, grid=(r // PL, nb),
        in_specs=[cur, nxt, cur, nxt, cur, nxt, cur, nxt, cur, cur] + [HBM] * nx, out_specs=[cur, cur, cur] + [HBM] * nx,
        out_shape=[jax.ShapeDtypeStruct(q.shape, BF16)] * 3 + [jax.ShapeDtypeStruct(p.shape, p.dtype) for p in parts]
                  + [jax.ShapeDtypeStruct((g.shape[0], g.shape[1] // 2, g.shape[2]), g.dtype) for g in pairs],
        scratch_shapes=[pltpu.VMEM((PL, BLOCK, GROUP), F32), pltpu.SemaphoreType.DMA((3 * npt + npr,)),
                        pltpu.SemaphoreType.DMA((3 * npt + npr,))],
    )(q, q, do, do, lse, lse, delta, delta, k, v, *parts, *pairs)


SB_TILES = 2
SB_PAIRS_FWD = 4
SB_PAIRS_BWD = 4
SB_DEAD = -110.0


def _lanes(hp):
    return slice(hp * LANES, (hp + 1) * LANES)


def _sb_logits(z, valid):
    e = jnp.exp(-jnp.abs(z))
    lb = jnp.minimum(z, 0.0) - jnp.log(1.0 + e)
    lk = lb - z
    if valid is not None:
        lk = jnp.where(valid, lk, 0.0)
    return e, lb, lk


def _by_head(t, first):
    zero = jnp.zeros_like(t)
    return jnp.concatenate([jnp.where(first, t, zero), jnp.where(first, zero, t)], axis=0)


def _sb_valid(i, j):
    rowi = lax.broadcasted_iota(jnp.int32, (BLOCK, BLOCK), 0)
    coli = lax.broadcasted_iota(jnp.int32, (BLOCK, BLOCK), 1)
    return (coli - rowi) < (i - j) * BLOCK


def _scaled(q):
    return (q.astype(F32) * (HEAD_DIM ** -0.5)).astype(BF16)


def _hosted_gathers(refs_in, refs_out, send, recv, step, steps):
    plans = [_gather_plan(refs_in[a], refs_out[a], send.at[pl.ds(6 * a, 6)], recv.at[pl.ds(6 * a, 6)])
             for a in range(len(refs_in))]

    def stages():
        for stage, at in ((0, 0), (1, (5 * steps) // 6), (2, steps - 1)):
            @pl.when(step == at)
            def _():
                for plan in plans:
                    plan[stage]()

    return stages


def _hosted_specs(slots):
    n = len(slots)
    sems = [pltpu.SemaphoreType.DMA((6 * n,))] * 2 if n else []
    return [HBM] * n, [HBM] * n, [jax.ShapeDtypeStruct(s.shape, s.dtype) for s in slots], sems


def _sb_fwd(qs, ks, vs, tri_later, slots):
    S = qs.shape[0]
    P = SB_PAIRS_FWD
    W = P * LANES
    ns = len(slots)

    def body(q_ref, k_ref, v_ref, u_ref, *rest):
        o_ref, lt_ref, from_ref = rest[ns:ns + 3]
        i = pl.program_id(1)
        finish = _hosted_gathers(rest[:ns], rest[ns + 3:2 * ns + 3], *rest[2 * ns + 3:], i, S // BLOCK) if ns else None
        first = lax.broadcasted_iota(jnp.int32, (BLOCK, LANES), 1) < HEAD_DIM
        q2 = [_scaled(q_ref[:, _lanes(hp)]) for hp in range(P)]

        def walk(tiles, carry):
            runs, accs = list(carry[0]), list(carry[1])
            units = [(t, hp) for t in range(len(tiles)) for hp in range(P)]
            offs = [pl.multiple_of(j * BLOCK, BLOCK) for j, _ in tiles]
            valids = [_sb_valid(i, j) if diag else None for j, diag in tiles]
            z2s, lbs, c2s = {}, {}, {}
            for t, hp in units:
                z2s[t, hp] = _dot_nt(q2[hp], _by_head(k_ref[pl.ds(offs[t], BLOCK), _lanes(hp)], first))
            for t, hp in units:
                for h in range(2):
                    _, lb, lk = _sb_logits(z2s[t, hp][:, h * BLOCK:(h + 1) * BLOCK], valids[t])
                    lbs[t, hp, h] = lb
                    c2s[t, hp, h] = _dot(jnp.concatenate(_split(lk), axis=1), u_ref[...])
            for t, hp in units:
                a2 = []
                for h in range(2):
                    a = jnp.exp(lbs[t, hp, h] + c2s[t, hp, h][:, :BLOCK] + runs[2 * hp + h])
                    if valids[t] is not None:
                        a = jnp.where(valids[t], a, 0.0)
                    a2.append(a.astype(BF16))
                    runs[2 * hp + h] = runs[2 * hp + h] + c2s[t, hp, h][:, BLOCK:]
                vcat = _by_head(v_ref[pl.ds(offs[t], BLOCK), _lanes(hp)], first)
                accs[hp] = accs[hp] + _dot(jnp.concatenate(a2, axis=1), vcat)
            return tuple(runs), tuple(accs)

        def chunk(ci, carry):
            return walk([(ci * SB_TILES + t, False) for t in reversed(range(SB_TILES))], carry)

        def alive(runs):
            top = functools.reduce(jnp.maximum, runs)
            return (jnp.max(top) > SB_DEAD).astype(jnp.int32)

        def step(c):
            t, _, runs, accs = c
            runs, accs = chunk(nfull - 1 - t, (runs, accs))
            return t + 1, alive(runs), runs, accs

        zero = jnp.zeros((BLOCK, LANES), F32)
        nfull = i // SB_TILES
        ragged = [functools.partial(walk, [(i, True)] + [(i - 1 - m, False) for m in range(extra)])
                  for extra in range(SB_TILES)]
        runs, accs = lax.switch(i % SB_TILES, ragged, ((zero,) * (2 * P), (zero,) * P))
        done, _, runs, accs = lax.while_loop(lambda c: jnp.logical_and(c[0] < nfull, c[1] > 0), step,
                                             (jnp.int32(0), alive(runs), runs, accs))
        for hp in range(P):
            o_ref[:, _lanes(hp)] = accs[hp]
            lt_ref[:, _lanes(hp)] = jnp.where(first, runs[2 * hp], runs[2 * hp + 1])
        from_ref[...] = jnp.full(from_ref.shape, nfull - done, jnp.int32)
        if finish is not None:
            finish()

    assert W == GROUP
    blk = pl.BlockSpec((BLOCK, W), lambda hp, i: (i, hp))
    col = pl.BlockSpec((S, W), lambda hp, i: (0, hp))
    h_in, h_out, h_shape, h_sems = _hosted_specs(slots)
    return pl.pallas_call(
        body, name="sb_fwd", grid=(GROUP // W, S // BLOCK),
        in_specs=[blk, col, col, _full((2 * BLOCK, 2 * BLOCK))] + h_in,
        out_specs=[blk, blk, pl.BlockSpec((1, 8, LANES), lambda hp, i: (i, 0, 0))] + h_out,
        out_shape=[jax.ShapeDtypeStruct((S, GROUP), F32)] * 2
                  + [jax.ShapeDtypeStruct((S // BLOCK, 8, LANES), jnp.int32)] + h_shape,
        input_output_aliases={4 + a: 3 + a for a in range(ns)},
        scratch_shapes=h_sems,
    )(qs, ks, vs, tri_later, *slots)


def _sb_bwd(first_chunk, qs, ks, vs, do, ltot, tri_upto, tri_before):
    S = qs.shape[0]
    P = SB_PAIRS_BWD
    W = P * LANES

    def body(from_ref, q_ref, k_ref, v_ref, do_ref, lt_ref, w_ref, x_ref, dq_ref, dk_ref, dv_ref):
        i = pl.program_id(1)

        @pl.when(i == 0)
        def _():
            dk_ref[...] = jnp.zeros_like(dk_ref)
            dv_ref[...] = jnp.zeros_like(dv_ref)

        first = lax.broadcasted_iota(jnp.int32, (BLOCK, LANES), 1) < HEAD_DIM
        q2 = [_scaled(q_ref[:, _lanes(hp)]) for hp in range(P)]
        do2 = [do_ref[:, _lanes(hp)] for hp in range(P)]
        totals = [jnp.broadcast_to(lt_ref[:, n * HEAD_DIM:n * HEAD_DIM + 1], (BLOCK, LANES)) for n in range(2 * P)]

        def walk(tiles, carry):
            keeps, grads, dqs = list(carry[0]), list(carry[1]), list(carry[2])
            units = [(t, hp) for t in range(len(tiles)) for hp in range(P)]
            offs = [pl.multiple_of(j * BLOCK, BLOCK) for j, _ in tiles]
            valids = [_sb_valid(i, j) if diag else None for j, diag in tiles]
            kcat, z2, da2, es, lbs, c2s, as_, des, p2s = {}, {}, {}, {}, {}, {}, {}, {}, {}
            for t, hp in units:
                kcat[t, hp] = _by_head(k_ref[pl.ds(offs[t], BLOCK), _lanes(hp)], first)
                z2[t, hp] = _dot_nt(q2[hp], kcat[t, hp])
                da2[t, hp] = _dot_nt(do2[hp], _by_head(v_ref[pl.ds(offs[t], BLOCK), _lanes(hp)], first))
            for t, hp in units:
                for h in range(2):
                    es[t, hp, h], lbs[t, hp, h], lk = _sb_logits(z2[t, hp][:, h * BLOCK:(h + 1) * BLOCK], valids[t])
                    c2s[t, hp, h] = _dot(jnp.concatenate(_split(lk), axis=1), w_ref[...])
            for t, hp in units:
                for h in range(2):
                    n = 2 * hp + h
                    a = jnp.exp(lbs[t, hp, h] + (totals[n] - (keeps[n] + c2s[t, hp, h][:, :BLOCK])))
                    if valids[t] is not None:
                        a = jnp.where(valids[t], a, 0.0)
                    keeps[n] = keeps[n] + c2s[t, hp, h][:, BLOCK:]
                    de = a * da2[t, hp][:, h * BLOCK:(h + 1) * BLOCK]
                    as_[t, hp, h], des[t, hp, h] = a.astype(BF16), de
                    p2s[t, hp, h] = _dot(jnp.concatenate(_split(de), axis=1), x_ref[...])
            for t, hp in units:
                dz2 = []
                for h in range(2):
                    n = 2 * hp + h
                    e = es[t, hp, h]
                    sig = jnp.where(z2[t, hp][:, h * BLOCK:(h + 1) * BLOCK] >= 0.0, 1.0, e) / (1.0 + e)
                    dz = des[t, hp, h] * (1.0 - sig) - (grads[n] + p2s[t, hp, h][:, :BLOCK]) * sig
                    if valids[t] is not None:
                        dz = jnp.where(valids[t], dz, 0.0)
                    grads[n] = grads[n] + p2s[t, hp, h][:, BLOCK:]
                    dz2.append(dz.astype(BF16))
                dzcat = jnp.concatenate(dz2, axis=1)
                dk2 = _dot_tn(dzcat, q2[hp])
                dv2 = _dot_tn(jnp.concatenate([as_[t, hp, 0], as_[t, hp, 1]], axis=1), do2[hp])
                dk_ref[pl.ds(offs[t], BLOCK), _lanes(hp)] += jnp.where(first, dk2[:BLOCK], dk2[BLOCK:])
                dv_ref[pl.ds(offs[t], BLOCK), _lanes(hp)] += jnp.where(first, dv2[:BLOCK], dv2[BLOCK:])
                dqs[hp] = dqs[hp] + _dot(dzcat, kcat[t, hp])
            return tuple(keeps), tuple(grads), tuple(dqs)

        zero = jnp.zeros((BLOCK, LANES), F32)
        nfull = i // SB_TILES
        carry = lax.fori_loop(
            from_ref[i], nfull, lambda ci, c: walk([(ci * SB_TILES + t, False) for t in range(SB_TILES)], c),
            ((zero,) * (2 * P), (zero,) * (2 * P), (zero,) * P))
        ragged = [functools.partial(walk, [(i - m, False) for m in range(extra, 0, -1)] + [(i, True)])
                  for extra in range(SB_TILES)]
        carry = lax.switch(i % SB_TILES, ragged, carry)
        for hp in range(P):
            dq_ref[:, _lanes(hp)] = carry[2][hp] * (HEAD_DIM ** -0.5)

    blk = pl.BlockSpec((BLOCK, W), lambda hp, i, fr: (i, hp))
    col = pl.BlockSpec((S, W), lambda hp, i, fr: (0, hp))
    tri = pl.BlockSpec((2 * BLOCK, 2 * BLOCK), lambda hp, i, fr: (0, 0))
    return pl.pallas_call(
        body, name="sb_bwd",
        grid_spec=pltpu.PrefetchScalarGridSpec(
            num_scalar_prefetch=1, grid=(GROUP // W, S // BLOCK),
            in_specs=[blk, col, col, blk, blk, tri, tri], out_specs=[blk, col, col]),
        out_shape=[jax.ShapeDtypeStruct((S, GROUP), F32)] * 3,
    )(first_chunk, qs, ks, vs, do, ltot, tri_upto, tri_before)


def _out_proj_fwd(o_br, l_br, o_sb, x, w_dil, w_sbn, w_out_g, slots):
    S, D = x.shape
    tm = 512
    ns = len(slots)

    def body(o0, o1, o2, l0, l1, l2, os_ref, x_ref, wd_ref, ws_ref, w_ref, *rest):
        od_ref, s0, s1, s2, x1_ref = rest[ns:ns + 5]
        scr = rest[2 * ns + 5]
        finish = (_hosted_gathers(rest[:ns], rest[ns + 5:2 * ns + 5], *rest[2 * ns + 6:], pl.program_id(0), S // tm)
                  if ns else None)
        ls = [_from_strided(scr, l) for l in (l0, l1, l2)]
        os_ = [_from_strided(scr, o) for o in (o0, o1, o2)]
        m = jnp.maximum(jnp.maximum(ls[0], ls[1]), ls[2])
        es = [jnp.exp(l - m) for l in ls]
        den = es[0] + es[1] + es[2]
        od = (es[0] * os_[0] + es[1] * os_[1] + es[2] * os_[2]) / den
        od_ref[...] = od
        _to_strided(scr, m + jnp.log(den), list(zip(DILATIONS, (s0, s1, s2))))
        osb = os_ref[...]
        mixed = jnp.concatenate([(od * _rms(od) * wd_ref[...]).astype(BF16),
                                 (osb * _rms(osb) * ws_ref[...]).astype(BF16)], axis=1)
        x1_ref[...] = x_ref[...] + _dot(mixed, w_ref[...])
        if finish is not None:
            finish()

    row = lambda i: (i, 0)
    g = pl.BlockSpec((tm, GROUP), row)
    d = pl.BlockSpec((tm, D), row)
    planes = [_strided_spec(tm, r) for r in DILATIONS]
    h_in, h_out, h_shape, h_sems = _hosted_specs(slots)
    return pl.pallas_call(
        body, name="out_proj_fwd", grid=(S // tm,),
        in_specs=planes * 2 + [g, d, _full((1, GROUP)), _full((1, GROUP)), _full((2 * GROUP, D))] + h_in,
        out_specs=[g] + planes + [d] + h_out,
        out_shape=[jax.ShapeDtypeStruct((S, GROUP), F32)] + [_strided_shape(S, r, F32) for r in DILATIONS]
                  + [jax.ShapeDtypeStruct((S, D), F32)] + h_shape,
        input_output_aliases={11 + a: 5 + a for a in range(ns)},
        scratch_shapes=[_strided_scratch(tm)] + h_sems,
    )(*o_br, *l_br, o_sb, x, w_dil, w_sbn, w_out_g, *slots)


def _ffn_fwd(x1, target, ffn_w, wg_g, wu_g, wd_g):
    S, D = x1.shape
    F = wg_g.shape[1]
    tm = 512
    nt = S // tm

    def body(x_ref, t_ref, nw_ref, wg_ref, wu_ref, wd_ref, h_ref, g_ref, u_ref, dy_ref, loss_ref, h_s, acc):
        j = pl.program_id(1)

        @pl.when(j == 0)
        def _():
            xv = x_ref[...]
            h = (xv * _rms(xv) * nw_ref[...]).astype(BF16)
            h_s[...] = h
            h_ref[...] = h
            acc[...] = xv

        h = h_s[...]
        g = _dot_nt(h, wg_ref[0])
        u = _dot_nt(h, wu_ref[0])
        g_ref[0] = g.astype(BF16)
        u_ref[0] = u.astype(BF16)
        a = (g * _sigmoid(g) * u).astype(BF16)
        acc[...] += _dot(a, wd_ref[0])

        @pl.when(j == N_CHIPS - 1)
        def _():
            err = acc[...] - t_ref[...]
            dy_ref[...] = err * (1.0 / D)
            loss_ref[...] = jnp.full(loss_ref.shape, jnp.sum(err * err), F32)

    row = lambda t, j: (t, 0)
    shard = lambda t, j: (j, 0, 0)
    act = lambda t, j: (j, t, 0)
    return pl.pallas_call(
        body, name="ffn_fwd", grid=(nt, N_CHIPS),
        in_specs=[pl.BlockSpec((tm, D), row), pl.BlockSpec((tm, D), row), pl.BlockSpec((1, D), lambda t, j: (0, 0))]
                 + [pl.BlockSpec((1, F, D), shard)] * 3,
        out_specs=[pl.BlockSpec((tm, D), row), pl.BlockSpec((1, tm, F), act), pl.BlockSpec((1, tm, F), act),
                   pl.BlockSpec((tm, D), row), pl.BlockSpec((1, 8, LANES), lambda t, j: (t, 0, 0))],
        out_shape=[jax.ShapeDtypeStruct((S, D), BF16), jax.ShapeDtypeStruct((N_CHIPS, S, F), BF16),
                   jax.ShapeDtypeStruct((N_CHIPS, S, F), BF16), jax.ShapeDtypeStruct((S, D), F32),
                   jax.ShapeDtypeStruct((nt, 8, LANES), F32)],
        scratch_shapes=[pltpu.VMEM((tm, D), BF16), pltpu.VMEM((tm, D), F32)],
    )(x1, target, ffn_w, wg_g, wu_g, wd_g)


def _ffn_bwd(h2, dy, g, u, wg_g, wu_g, wd_g):
    S, D = dy.shape
    F = wg_g.shape[1]
    tm = 512

    def body(h_ref, dy_ref, g_ref, u_ref, wg_ref, wu_ref, wd_ref, dwg_ref, dwu_ref, dwd_ref, dh_ref, *narrow):
        t = pl.program_id(1)

        @pl.when(t == 0)
        def _():
            dwg_ref[...] = jnp.zeros_like(dwg_ref)
            dwu_ref[...] = jnp.zeros_like(dwu_ref)
            dwd_ref[...] = jnp.zeros_like(dwd_ref)

        h = h_ref[...]
        dyb = dy_ref[...].astype(BF16)
        gv = g_ref[0].astype(F32)
        uv = u_ref[0].astype(F32)
        da = _dot_nt(dyb, wd_ref[0])
        sg = _sigmoid(gv)
        silu = gv * sg
        du = (da * silu).astype(BF16)
        dg = (da * uv * (sg * (1.0 + gv * (1.0 - sg)))).astype(BF16)
        dwd_ref[0] += _dot_tn((silu * uv).astype(BF16), dyb)
        dwg_ref[0] += _dot_tn(dg, h)
        dwu_ref[0] += _dot_tn(du, h)
        dh_ref[0] = (_dot(dg, wg_ref[0]) + _dot(du, wu_ref[0])).astype(BF16)

        @pl.when(t == S // tm - 1)
        def _():
            for full, half in zip((dwg_ref, dwu_ref, dwd_ref), narrow):
                half[...] = full[...].astype(BF16)

    row = lambda j, t: (t, 0)
    shard = lambda j, t: (j, 0, 0)
    act = lambda j, t: (j, t, 0)
    return pl.pallas_call(
        body, name="ffn_bwd", grid=(N_CHIPS, S // tm),
        in_specs=[pl.BlockSpec((tm, D), row), pl.BlockSpec((tm, D), row),
                  pl.BlockSpec((1, tm, F), act), pl.BlockSpec((1, tm, F), act)] + [pl.BlockSpec((1, F, D), shard)] * 3,
        out_specs=[pl.BlockSpec((1, F, D), shard)] * 3 + [pl.BlockSpec((1, tm, D), act)]
                  + [pl.BlockSpec((1, F, D), shard)] * 3,
        out_shape=[jax.ShapeDtypeStruct((N_CHIPS, F, D), F32)] * 3 + [jax.ShapeDtypeStruct((N_CHIPS, S, D), BF16)]
                  + [jax.ShapeDtypeStruct((N_CHIPS, F, D), BF16)] * 3,
    )(h2, dy, g, u, wg_g, wu_g, wd_g)


def _out_proj_bwd(dh2p, dy, x1, ffn_w, w_out_g, o_dil, o_sb, w_dil, w_sbn, seg_ones, ffn_grads):
    S, D = dy.shape
    tm = 512
    ng = len(ffn_grads)

    def body(dh_ref, dy_ref, x1_ref, nw_ref, w_ref, od_ref, os_ref, wd_ref, ws_ref, g_ref, *rest):
        gin, rest = rest[:ng], rest[ng:]
        dx1_ref, dod0, dod1, dod2, dos_ref, dl0, dl1, dl2, dw_ref, dnw_ref, dwd_ref, dws_ref = rest[:12]
        gout, (scr, send, recv) = rest[12:12 + ng], rest[12 + ng:]
        i = pl.program_id(0)
        plans = [_pair_send_plan(gin[a], gout[a], send.at[a], recv.at[a]) for a in range(ng)]

        @pl.when(i == 0)
        def _():
            for r_ in (dw_ref, dnw_ref, dwd_ref, dws_ref):
                r_[...] = jnp.zeros_like(r_)

        dh2 = _sum4(dh_ref)
        dxn, dwn = _rms_bwd(dh2, x1_ref[...], nw_ref[...])
        dnw_ref[...] += jnp.sum(dwn, axis=0, keepdims=True)
        dx1 = dy_ref[...] + dxn
        dx1_ref[...] = dx1
        dx1b = dx1.astype(BF16)
        dmix = _dot_nt(dx1b, w_ref[...])
        od = od_ref[...]
        osb = os_ref[...]
        mixed = jnp.concatenate([(od * _rms(od) * wd_ref[...]).astype(BF16),
                                 (osb * _rms(osb) * ws_ref[...]).astype(BF16)], axis=1)
        dw_ref[...] += _dot_tn(mixed, dx1b)
        do, dwo = _rms_bwd(dmix[:, :GROUP], od, wd_ref[...])
        dwd_ref[...] += jnp.sum(dwo, axis=0, keepdims=True)
        _to_strided(scr, do, list(zip(DILATIONS, (dod0, dod1, dod2))))
        _to_strided(scr, _segsum(do * od, g_ref[...]), list(zip(DILATIONS, (dl0, dl1, dl2))))
        do, dwo = _rms_bwd(dmix[:, GROUP:], osb, ws_ref[...])
        dws_ref[...] += jnp.sum(dwo, axis=0, keepdims=True)
        dos_ref[...] = do.astype(BF16)

        @pl.when(i == 0)
        def _():
            for start, _ in plans:
                start()

        @pl.when(i == S // tm - 1)
        def _():
            for _, finish in plans:
                finish()

    row = lambda i: (i, 0)
    gsp = pl.BlockSpec((tm, GROUP), row)
    dsp = pl.BlockSpec((tm, D), row)
    planes = [_strided_spec(tm, r) for r in DILATIONS]
    halves = [jax.ShapeDtypeStruct((g.shape[0], g.shape[1] // 2, g.shape[2]), g.dtype) for g in ffn_grads]
    return pl.pallas_call(
        body, name="out_proj_bwd", grid=(S // tm,),
        in_specs=[pl.BlockSpec((N_CHIPS, tm, D), lambda i: (0, i, 0)), dsp, dsp, _full((1, D)), _full((2 * GROUP, D)),
                  gsp, gsp, _full((1, GROUP)), _full((1, GROUP)), _full((GROUP, GROUP // 2))] + [HBM] * ng,
        out_specs=[dsp] + planes + [gsp] + planes
                  + [_full((2 * GROUP, D)), _full((1, D)), _full((1, GROUP)), _full((1, GROUP))] + [HBM] * ng,
        out_shape=[jax.ShapeDtypeStruct((S, D), F32)] + [_strided_shape(S, r, BF16) for r in DILATIONS]
                  + [jax.ShapeDtypeStruct((S, GROUP), BF16)] + [_strided_shape(S, r, F32) for r in DILATIONS]
                  + [jax.ShapeDtypeStruct((2 * GROUP, D), F32),
                     jax.ShapeDtypeStruct((1, D), F32), jax.ShapeDtypeStruct((1, GROUP), F32),
                     jax.ShapeDtypeStruct((1, GROUP), F32)] + halves,
        scratch_shapes=[_strided_scratch(tm), pltpu.SemaphoreType.DMA((ng,)), pltpu.SemaphoreType.DMA((ng,))],
    )(dh2p, dy, x1, ffn_w, w_out_g, o_dil, o_sb, w_dil, w_sbn, seg_ones, *ffn_grads)


def _attn_in_bwd(dq_br, dk_br, dv_br, dqs, dks, dvs, qa, ka, qw, kw, cos_t, sin_t, seg_ones, h, w_in_g, x, dx1, attn_w):
    S, D = x.shape
    wc = w_in_g.shape[2]
    tm = 256

    def body(q0, q1, q2, k0, k1, k2, v0, v1, v2, dqs_ref, dks_ref, dvs_ref, qa_ref, ka_ref, qw_ref, kw_ref,
             cos_ref, sin_ref, g_ref, h_ref, w_ref, x_ref, dx1_ref, aw_ref,
             gx_ref, dw_ref, daw_ref, dqw_ref, dkw_ref, accq, acck, scr):
        i = pl.program_id(0)

        @pl.when(i == 0)
        def _():
            accq[...] = jnp.zeros_like(accq)
            acck[...] = jnp.zeros_like(acck)
            dw_ref[...] = jnp.zeros_like(dw_ref)
            daw_ref[...] = jnp.zeros_like(daw_ref)

        def branches(refs):
            return (_from_strided(scr, refs[0]) + _from_strided(scr, refs[1])) + _from_strided(scr, refs[2])

        g = g_ref[...]
        cos = _tile4(cos_ref[...])
        sin = _tile4(sin_ref[...])
        pieces = []
        for refs, pre_ref, w_r, acc in (((q0, q1, q2), qa_ref, qw_ref, accq), ((k0, k1, k2), ka_ref, kw_ref, acck)):
            dh = branches(refs)
            dn = dh * cos + _rot_half(dh * sin)
            pre = pre_ref[...]
            rstd = lax.rsqrt(_segsum(pre * pre, g) * (1.0 / HEAD_DIM) + EPS)
            xh = pre * rstd
            acc[...] += jnp.sum(dn * xh, axis=0, keepdims=True)
            dxh = dn * w_r[...]
            pieces.append((rstd * (dxh - xh * (_segsum(dxh * xh, g) * (1.0 / HEAD_DIM)))).astype(BF16))
        pieces += [branches((v0, v1, v2)).astype(BF16), dqs_ref[...].astype(BF16), dks_ref[...].astype(BF16),
                   dvs_ref[...].astype(BF16)]
        dproj = jnp.concatenate(pieces, axis=1)
        hv = h_ref[...]
        dh = jnp.zeros((tm, D), F32)
        for j in range(N_CHIPS):
            dp = dproj[:, j * wc:(j + 1) * wc]
            dw_ref[j] += _dot_tn(hv, dp)
            dh = dh + _dot_nt(dp, w_ref[j])
        dx, dw = _rms_bwd(dh, x_ref[...], aw_ref[...])
        daw_ref[...] += jnp.sum(dw, axis=0, keepdims=True)
        gx_ref[...] = dx1_ref[...] + dx

        @pl.when(i == S // tm - 1)
        def _():
            for acc, o_ref in ((accq, dqw_ref), (acck, dkw_ref)):
                a = acc[...]
                pair = (a[:, 0:LANES] + a[:, LANES:2 * LANES]) + (a[:, 2 * LANES:3 * LANES] + a[:, 3 * LANES:4 * LANES])
                o_ref[...] = pair + pltpu.roll(pair, HEAD_DIM, 1)

    row = lambda i: (i, 0)
    gsp = pl.BlockSpec((tm, GROUP), row)
    dsp = pl.BlockSpec((tm, D), row)
    tab = pl.BlockSpec((tm, LANES), row)
    planes = [_strided_spec(tm, r) for r in DILATIONS]
    return pl.pallas_call(
        body, name="attn_in_bwd", grid=(S // tm,),
        in_specs=planes * 3 + [gsp] * 5 + [_full((1, GROUP)), _full((1, GROUP)), tab, tab, _full((GROUP, GROUP // 2)),
                                          dsp, _full((N_CHIPS, D, wc)), dsp, dsp, _full((1, D))],
        out_specs=[dsp, _full((N_CHIPS, D, wc)), _full((1, D)), _full((1, LANES)), _full((1, LANES))],
        out_shape=[jax.ShapeDtypeStruct((S, D), F32), jax.ShapeDtypeStruct((N_CHIPS, D, wc), F32),
                   jax.ShapeDtypeStruct((1, D), F32), jax.ShapeDtypeStruct((1, LANES), F32),
                   jax.ShapeDtypeStruct((1, LANES), F32)],
        scratch_shapes=[pltpu.VMEM((1, GROUP), F32), pltpu.VMEM((1, GROUP), F32), _strided_scratch(tm)],
    )(*dq_br, *dk_br, *dv_br, dqs, dks, dvs, qa, ka, qw, kw, cos_t, sin_t, seg_ones, h, w_in_g, x, dx1, attn_w)


def _constants(S):
    pos = jnp.arange(S, dtype=F32)
    inv_freq = ROPE_THETA ** (-jnp.arange(0, HEAD_DIM, 2, dtype=F32) / HEAD_DIM)
    ang_a = pos[::BLOCK, None] * inv_freq[None, :]
    ang_b = pos[:BLOCK, None] * inv_freq[None, :]
    ca, sa, cb, sb = jnp.cos(ang_a)[:, None], jnp.sin(ang_a)[:, None], jnp.cos(ang_b)[None], jnp.sin(ang_b)[None]
    cos = (ca * cb - sa * sb).reshape(S, HEAD_DIM // 2)
    sin = (sa * cb + ca * sb).reshape(S, HEAD_DIM // 2)
    cos_t = jnp.concatenate([cos, cos] * 2, axis=1)
    sin_t = jnp.concatenate([-sin, sin] * 2, axis=1)
    idx = jnp.arange(GROUP // 2)
    seg_ones = (idx[:, None] // HEAD_DIM == idx[None, :] // HEAD_DIM).astype(BF16)
    seg_ones = jnp.concatenate([seg_ones, seg_ones], axis=0)
    r = jnp.arange(BLOCK)
    ones = jnp.ones((BLOCK, BLOCK), BF16)
    tris = [jnp.concatenate([jnp.concatenate([m.astype(BF16), ones], axis=1)] * 2, axis=0) for m in
            (r[:, None] > r[None, :],
             r[:, None] <= r[None, :],
             r[:, None] < r[None, :])]
    return cos_t, sin_t, seg_ones, tris


FFN_NAMES = ("w_gate", "w_up", "w_down")


def _device_step(x, target, attn_w, qn_w, kn_w, dil_w, sbn_w, ffn_w, w_in_g, w_out_slots, ffn_slots, core, chip):
    S = x.shape[0]
    cos_t, sin_t, seg_ones, (tri_later, tri_upto, tri_before) = _constants(S)
    reps = GROUP // HEAD_DIM
    qw = jnp.tile(qn_w, (1, reps))
    kw = jnp.tile(kn_w, (1, reps))

    nd = len(DILATIONS)
    h, qa, ka, *rest, wg_g = _in_proj_fwd(x, attn_w, w_in_g, qw, kw, cos_t, sin_t, seg_ones, ffn_slots[:1])
    qh, kh, va, (qs, ks, vs) = rest[:nd], rest[nd:2 * nd], rest[2 * nd:3 * nd], rest[3 * nd:]
    hosted = ([], [w_out_slots], [])
    branches = [_dil_fwd(qh[b], kh[b], va[b], hosted[b]) for b in range(nd)]
    w_out_g = branches[1][2].reshape(-1, x.shape[1])
    o_sb, ltot, walked, wu_g, wd_g = _sb_fwd(qs, ks, vs, tri_later, ffn_slots[1:])
    o_dil, *lse, x1 = _out_proj_fwd([b[0] for b in branches], [b[1] for b in branches], o_sb, x, dil_w, sbn_w,
                                    w_out_g, [])
    h2, g, u, dy, loss_parts = _ffn_fwd(x1, target, ffn_w, wg_g, wu_g, wd_g)

    *ffn_grads, dh2p, n0, n1, n2 = _ffn_bwd(h2, dy, g, u, wg_g, wu_g, wd_g)
    dx1, *mid, dw_out, dffn_w, ddil_w, dsbn_w, p0, p1, p2 = _out_proj_bwd(
        dh2p, dy, x1, ffn_w, w_out_g, o_dil, o_sb, dil_w, sbn_w, seg_ones, [n0, n1, n2])
    do_dil, do_sb, delta = mid[:nd], mid[nd], mid[nd + 1:]
    parts = [_pair_sum(gr, fr, core, n) for gr, fr, n in zip(ffn_grads, (p0, p1, p2), FFN_NAMES)]
    dqs, dks, dvs = _sb_bwd(walked[:, 0, 0], qs, ks, vs, do_sb, ltot, tri_upto, tri_before)
    dw_out = dw_out.reshape(N_CHIPS, -1, x.shape[1])
    dbr = [None] * nd
    dbr[0] = _dil_bwd(qh[0], kh[0], va[0], do_dil[0], lse[0], delta[0], [parts[0]], [dw_out])
    out_part = _pair_sum(dw_out, dbr[0][4], core, "w_out")
    dbr[1] = _dil_bwd(qh[1], kh[1], va[1], do_dil[1], lse[1], delta[1], [parts[1], out_part], [])
    dbr[2] = _dil_bwd(qh[2], kh[2], va[2], do_dil[2], lse[2], delta[2], [parts[2]], [])
    ffn_halves = [_chip_sum(dbr[b][3], parts[b], chip, FFN_NAMES[b]) for b in range(nd)]
    w_out_half = _chip_sum(dbr[1][4], out_part, chip, "w_out")
    grad_x, dw_in, dattn_w, dqw, dkw = _attn_in_bwd(
        [b[0] for b in dbr], [b[1] for b in dbr], [b[2] for b in dbr], dqs, dks, dvs,
        qa, ka, qw, kw, cos_t, sin_t, seg_ones, h, w_in_g, x, dx1, attn_w)
    small = dict(attn=dattn_w, q=dqw[:, :HEAD_DIM], k=dkw[:, :HEAD_DIM], dil=ddil_w, sb=dsbn_w, ffn=dffn_w)
    return loss_parts, grad_x, small, dw_in, w_out_half, ffn_halves


HBM = pl.BlockSpec(memory_space=pltpu.HBM)
VMEM = pl.BlockSpec(memory_space=pltpu.VMEM)
CHIP_FLIPS = ((1, 0), (0, 1), (1, 1))


def _place():
    return lax.axis_index("x"), lax.axis_index("y"), lax.axis_index("c")


def _flip(v, d):
    return 1 - v if d else v


def _half_rows(c, n):
    return pl.ds(pl.multiple_of(c * (n // 2), 16), n // 2)


def _gather_plan(slot_in, slot_out, send, recv):
    x, y, c = _place()
    p = 2 * x + y
    chips = [(_flip(x, dx), _flip(y, dy)) for dx, dy in CHIP_FLIPS]
    mine, other = _half_rows(c, slot_in.shape[1]), _half_rows(1 - c, slot_in.shape[1])

    def copy(k, src, dst, to):
        return pltpu.make_async_remote_copy(src_ref=src, dst_ref=dst, send_sem=send.at[k], recv_sem=recv.at[k],
                                            device_id=to, device_id_type=MESH)

    def first(k):
        return copy(k, slot_in.at[p, mine], slot_out.at[p, mine], (*chips[k], c))

    def passed(k, rows):
        land = slot_out.at[2 * chips[k][0] + chips[k][1], rows]
        return copy(3 + k, land, land, (x, y, 1 - c))

    def start():
        for k in range(3):
            first(k).start()

    def forward():
        for k in range(3):
            land = slot_out.at[2 * chips[k][0] + chips[k][1], mine]
            copy(k, land, land, (*chips[k], c)).wait_recv()
            passed(k, mine).start()

    def finish():
        for k in range(3):
            passed(k, other).wait_recv()
        for k in range(3):
            first(k).wait_send()
            passed(k, mine).wait_send()

    return start, forward, finish


def _chip_send_plan(part_in, recv_out, send, recv):
    x, y, c = _place()
    p = 2 * x + y
    chips = [(_flip(x, dx), _flip(y, dy)) for dx, dy in CHIP_FLIPS]

    def copy(k):
        q = 2 * chips[k][0] + chips[k][1]
        return pltpu.make_async_remote_copy(src_ref=part_in.at[q], dst_ref=recv_out.at[p], send_sem=send.at[k],
                                            recv_sem=recv.at[k], device_id=(*chips[k], c), device_id_type=MESH)

    def start():
        for k in range(3):
            copy(k).start()

    def finish():
        for k in range(3):
            land = recv_out.at[2 * chips[k][0] + chips[k][1]]
            pltpu.make_async_remote_copy(src_ref=land, dst_ref=land, send_sem=send.at[k], recv_sem=recv.at[k],
                                         device_id=(*chips[k], c), device_id_type=MESH).wait_recv()
        for k in range(3):
            copy(k).wait_send()

    return start, finish


def _pair_send_plan(grad_in, recv_out, send, recv):
    x, y, c = _place()

    def copy():
        theirs = _half_rows(1 - c, grad_in.shape[1])
        return pltpu.make_async_remote_copy(src_ref=grad_in.at[:, theirs, :], dst_ref=recv_out, send_sem=send,
                                            recv_sem=recv, device_id=(x, y, 1 - c), device_id_type=MESH)

    return (lambda: copy().start()), (lambda: copy().wait())


def _own_slots(shard):
    here = 2 * lax.axis_index("x") + lax.axis_index("y")
    return lax.dynamic_update_slice(lax.empty((N_CHIPS,) + shard.shape, shard.dtype), shard[None], (here, 0, 0))


def _gather_weights(shards):
    n = len(shards)

    def body(*refs):
        ins, outs = refs[:n], refs[n:2 * n]
        send, recv = refs[2 * n:]
        plans = [_gather_plan(ins[a], outs[a], send.at[pl.ds(6 * a, 6)], recv.at[pl.ds(6 * a, 6)]) for a in range(n)]
        for stage in range(3):
            for plan in plans:
                plan[stage]()

    slots = [_own_slots(s) for s in shards]
    return pl.pallas_call(
        body, name="gather_weights", in_specs=[HBM] * n, out_specs=[HBM] * n,
        out_shape=[jax.ShapeDtypeStruct(s.shape, s.dtype) for s in slots],
        input_output_aliases={a: a for a in range(n)},
        scratch_shapes=[pltpu.SemaphoreType.DMA((6 * n,)), pltpu.SemaphoreType.DMA((6 * n,))],
    )(*slots)


def _pair_exchange(grads, small):
    n = len(grads)

    def body(*refs):
        gin, sm = refs[:n], refs[n]
        gout, sm_all = refs[n + 1:2 * n + 1], refs[2 * n + 1]
        send, recv = refs[2 * n + 2:]
        x, y, c = _place()
        me = 4 * x + 2 * y + c
        big = [_pair_send_plan(gin[a], gout[a], send.at[a], recv.at[a]) for a in range(n)]
        for start, _ in big:
            start()
        sm_all[pl.ds(me, 1)] = sm[...][None]
        tiny = []
        for k in range(1, N_DEV):
            px, py, pc = _flip(x, k & 4), _flip(y, k & 2), _flip(c, k & 1)
            tiny.append((pltpu.make_async_remote_copy(
                src_ref=sm, dst_ref=sm_all.at[me], send_sem=send.at[n + k - 1], recv_sem=recv.at[n + k - 1],
                device_id=(px, py, pc), device_id_type=MESH), 4 * px + 2 * py + pc))
            tiny[-1][0].start()
        for k, (cp, peer) in enumerate(tiny):
            pltpu.make_async_remote_copy(src_ref=sm, dst_ref=sm_all.at[peer], send_sem=send.at[n + k],
                                         recv_sem=recv.at[n + k], device_id=(x, y, c),
                                         device_id_type=MESH).wait_recv()
            cp.wait_send()
        for _, finish in big:
            finish()

    halves = [jax.ShapeDtypeStruct((g.shape[0], g.shape[1] // 2, g.shape[2]), g.dtype) for g in grads]
    return pl.pallas_call(
        body, name="pair_exchange", in_specs=[HBM] * n + [VMEM], out_specs=[HBM] * n + [VMEM],
        out_shape=halves + [jax.ShapeDtypeStruct((N_DEV,) + small.shape, small.dtype)],
        scratch_shapes=[pltpu.SemaphoreType.DMA((n + N_DEV - 1,)), pltpu.SemaphoreType.DMA((n + N_DEV - 1,))],
    )(*grads, small)


def _chip_exchange(parts):
    n = len(parts)

    def body(*refs):
        pin, pout = refs[:n], refs[n:2 * n]
        send, recv = refs[2 * n:]
        plans = [_chip_send_plan(pin[a], pout[a], send.at[pl.ds(3 * a, 3)], recv.at[pl.ds(3 * a, 3)]) for a in range(n)]
        for stage in range(2):
            for plan in plans:
                plan[stage]()

    return pl.pallas_call(
        body, name="chip_exchange", in_specs=[HBM] * n, out_specs=[HBM] * n,
        out_shape=[jax.ShapeDtypeStruct(s.shape, s.dtype) for s in parts],
        scratch_shapes=[pltpu.SemaphoreType.DMA((3 * n,)), pltpu.SemaphoreType.DMA((3 * n,))],
    )(*parts)


def _pair_swap_plan(hin, hout, send, recv):
    x, y, c = _place()

    def copies():
        return [pltpu.make_async_remote_copy(src_ref=hin[a], dst_ref=hout[a], send_sem=send.at[a], recv_sem=recv.at[a],
                                             device_id=(x, y, 1 - c), device_id_type=MESH) for a in range(len(hin))]

    def start():
        for cp in copies():
            cp.start()

    def finish():
        for cp in copies():
            cp.wait()

    return start, finish


def _pair_swap(halves):
    n = len(halves)

    def body(*refs):
        start, finish = _pair_swap_plan(refs[:n], refs[n:2 * n], *refs[2 * n:])
        start()
        finish()

    return pl.pallas_call(
        body, name="pair_swap", in_specs=[HBM] * n, out_specs=[HBM] * n,
        out_shape=[jax.ShapeDtypeStruct(s.shape, s.dtype) for s in halves],
        scratch_shapes=[pltpu.SemaphoreType.DMA((n,)), pltpu.SemaphoreType.DMA((n,))],
    )(*halves)


def _pair_sum(grad, recv, c, tag):
    _, R, C = grad.shape
    hr = R // 2

    def body(c_ref, a_ref, b_ref, o_ref):
        o_ref[...] = (a_ref[...] + b_ref[...]).astype(BF16)

    return pl.pallas_call(
        body, name="pair_sum_" + tag,
        grid_spec=pltpu.PrefetchScalarGridSpec(
            num_scalar_prefetch=1, grid=(N_CHIPS,),
            in_specs=[pl.BlockSpec((1, hr, C), lambda s, cr: (s, cr[0], 0)),
                      pl.BlockSpec((1, hr, C), lambda s, cr: (s, 0, 0))],
            out_specs=pl.BlockSpec((1, hr, C), lambda s, cr: (s, 0, 0))),
        out_shape=jax.ShapeDtypeStruct((N_CHIPS, hr, C), BF16),
    )(c, grad, recv)


def _chip_sum(received, own, chip, tag):
    _, rows, C = received.shape
    tr = rows // 2

    def body(chip_ref, own_ref, r1_ref, r2_ref, r3_ref, o_ref):
        p = [r[0].astype(F32) for r in (own_ref, r1_ref, r2_ref, r3_ref)]
        o_ref[...] = (p[0] + p[1]) + (p[2] + p[3])

    def slot(k):
        return pl.BlockSpec((1, tr, C), lambda i, cr: (jnp.bitwise_xor(cr[0], k), i, 0))

    return pl.pallas_call(
        body, name="chip_sum_" + tag,
        grid_spec=pltpu.PrefetchScalarGridSpec(
            num_scalar_prefetch=1, grid=(rows // tr,), in_specs=[slot(0), slot(1), slot(2), slot(3)],
            out_specs=pl.BlockSpec((tr, C), lambda i, cr: (i, 0))),
        out_shape=jax.ShapeDtypeStruct((rows, C), F32),
    )(chip, own, received, received, received)


def _adamw_math(w, g, m, v):
    m = ADAM_B1 * m + (1.0 - ADAM_B1) * g
    v = ADAM_B2 * v + (1.0 - ADAM_B2) * (g * g)
    m_hat = m / (1.0 - ADAM_B1 ** ADAM_STEP)
    v_hat = v / (1.0 - ADAM_B2 ** ADAM_STEP)
    delta = -ADAM_LR * (m_hat / (jnp.sqrt(v_hat) + ADAM_EPS) + ADAM_WD * w)
    return delta, m, v


def _adamw(w, g_mine, g_other, m, v, c, tag):
    R, C = w.shape

    def body(c_ref, w_ref, gm_ref, go_ref, m_ref, v_ref, g_ref, d_ref, nm_ref, nv_ref):
        g = jnp.where(pl.program_id(0) == c_ref[0], gm_ref[...], go_ref[...])
        g_ref[...] = g
        d_ref[...], nm_ref[...], nv_ref[...] = _adamw_math(w_ref[...], g, m_ref[...], v_ref[...])

    blk = pl.BlockSpec((R // 2, C), lambda h, cr: (h, 0))
    half = pl.BlockSpec((R // 2, C), lambda h, cr: (0, 0))
    return pl.pallas_call(
        body, name="adamw_" + tag,
        grid_spec=pltpu.PrefetchScalarGridSpec(
            num_scalar_prefetch=1, grid=(2,), in_specs=[blk, half, half, blk, blk], out_specs=[blk] * 4),
        out_shape=[jax.ShapeDtypeStruct((R, C), F32)] * 4,
    )(c, w, g_mine, g_other, m, v)


def _small_update(all_small, w, m, v):
    def body(a_ref, w_ref, m_ref, v_ref, g_ref, d_ref, nm_ref, nv_ref):
        g = ((a_ref[0] + a_ref[1]) + (a_ref[2] + a_ref[3])) + ((a_ref[4] + a_ref[5]) + (a_ref[6] + a_ref[7]))
        g_ref[...] = g
        d_ref[...], nm_ref[...], nv_ref[...] = _adamw_math(w_ref[...], g, m_ref[...], v_ref[...])

    return pl.pallas_call(
        body, name="small_update", out_shape=[jax.ShapeDtypeStruct(w.shape, F32)] * 4,
    )(all_small, w, m, v)


SMALL_ROWS = (("attn", 0, 0), ("ffn", 1, 0), ("dil", 2, 0), ("sb", 2, GROUP), ("q", 3, 0), ("k", 3, HEAD_DIM),
              ("loss", 4, 0))


def _pack_small(vals, D):
    rows = [jnp.zeros((1, D), F32) for _ in range(8)]
    for name, r, off in SMALL_ROWS:
        if name in vals:
            rows[r] = lax.dynamic_update_slice(rows[r], vals[name].astype(F32), (0, off))
    return jnp.concatenate(rows, axis=0)


def _unpack_small(packed, vals):
    return {name: packed[r:r + 1, off:off + vals[name].shape[1]] for name, r, off in SMALL_ROWS if name in vals}


def kernel(x, attn_norm_w, w_in, q_norm_w, k_norm_w, dil_out_norm_w, sb_out_norm_w, w_out, ffn_norm_w, w_gate, w_up, w_down, loss_target, m_attn_norm_w, m_w_in, m_q_norm_w, m_k_norm_w, m_dil_out_norm_w, m_sb_out_norm_w, m_w_out, m_ffn_norm_w, m_w_gate, m_w_up, m_w_down, v_attn_norm_w, v_w_in, v_q_norm_w, v_k_norm_w, v_dil_out_norm_w, v_sb_out_norm_w, v_w_out, v_ffn_norm_w, v_w_gate, v_w_up, v_w_down):
    D = x.shape[-1]
    big_names = ("w_in", "w_out", "w_gate", "w_up", "w_down")
    flipped = ("w_gate", "w_up")
    tr = lambda a: jnp.swapaxes(a[0], 0, 1)
    big_w = dict(w_in=w_in[0], w_out=w_out[0], w_gate=tr(w_gate), w_up=tr(w_up), w_down=w_down[0])
    big_m = dict(w_in=m_w_in[0], w_out=m_w_out[0], w_gate=tr(m_w_gate), w_up=tr(m_w_up), w_down=m_w_down[0])
    big_v = dict(w_in=v_w_in[0], w_out=v_w_out[0], w_gate=tr(v_w_gate), w_up=tr(v_w_up), w_down=v_w_down[0])
    small_w = dict(attn=attn_norm_w, q=q_norm_w, k=k_norm_w, dil=dil_out_norm_w, sb=sb_out_norm_w, ffn=ffn_norm_w)
    small_m = dict(attn=m_attn_norm_w, q=m_q_norm_w, k=m_k_norm_w, dil=m_dil_out_norm_w, sb=m_sb_out_norm_w,
                   ffn=m_ffn_norm_w)
    small_v = dict(attn=v_attn_norm_w, q=v_q_norm_w, k=v_k_norm_w, dil=v_dil_out_norm_w, sb=v_sb_out_norm_w,
                   ffn=v_ffn_norm_w)

    c = lax.axis_index("c").astype(jnp.int32).reshape(1)
    chip = (2 * lax.axis_index("x") + lax.axis_index("y")).astype(jnp.int32).reshape(1)
    (w_in_g,) = _gather_weights([big_w["w_in"].astype(BF16)])
    w_out_slots = _own_slots(big_w["w_out"].astype(BF16))
    ffn_slots = [_own_slots(big_w[n].astype(BF16)) for n in FFN_NAMES]

    loss_parts, grad_x, small_g, dw_in, w_out_half, ffn_halves = _device_step(
        x[0], loss_target[0], attn_norm_w, q_norm_w, k_norm_w, dil_out_norm_w, sb_out_norm_w, ffn_norm_w,
        w_in_g, w_out_slots, ffn_slots, c, chip)
    small_g["loss"] = (jnp.sum(loss_parts[:, 0, 0]) * (0.5 / D)).reshape(1, 1)

    from_pair, all_small = _pair_exchange([dw_in], _pack_small(small_g, D))
    in_part = _pair_sum(dw_in, from_pair, c, "w_in")
    (from_chips,) = _chip_exchange([in_part])
    halves = [_chip_sum(from_chips, in_part, chip, "w_in"), w_out_half] + ffn_halves
    others = _pair_swap(halves)
    big_out = {n: _adamw(big_w[n], mine, other, big_m[n], big_v[n], c, n)
               for n, mine, other in zip(big_names, halves, others)}
    sg, sd, sm, sv = _small_update(all_small, _pack_small(small_w, D), _pack_small(small_m, D),
                                   _pack_small(small_v, D))
    small_out = [_unpack_small(t, small_w) for t in (sg, sd, sm, sv)]

    order = (("attn", None), (None, "w_in"), ("q", None), ("k", None), ("dil", None), ("sb", None),
             (None, "w_out"), ("ffn", None), (None, "w_gate"), (None, "w_up"), (None, "w_down"))
    outs = [sg[4, 0], grad_x[None]]
    for kind in range(4):
        for s_name, b_name in order:
            if s_name is not None:
                outs.append(small_out[kind][s_name])
            else:
                res = big_out[b_name][kind]
                outs.append((jnp.swapaxes(res, 0, 1) if b_name in flipped else res)[None])
    return tuple(outs)
```

```python
import functools

import jax
import jax.numpy as jnp
from jax import lax
from jax.experimental import pallas as pl
from jax.experimental.pallas import tpu as pltpu

F32 = jnp.float32
BF16 = jnp.bfloat16
MESH = pl.DeviceIdType.MESH

HEAD_DIM = 64
GROUP = 512
BLOCK = 128
LANES = 128
N_CHIPS = 4
N_DEV = 8
EPS = 1e-6
ROPE_THETA = 10000.0
DILATIONS = (1, 4, 16)
NEG = -1e30

ADAM_LR = 0.001
ADAM_B1 = 0.9
ADAM_B2 = 0.999
ADAM_EPS = 1e-08
ADAM_WD = 0.01
ADAM_STEP = 10


def _dot(a, b):
    return jnp.dot(a, b, preferred_element_type=F32)


def _dot_nt(a, b):
    return lax.dot_general(a, b, (((1,), (1,)), ((), ())), preferred_element_type=F32)


def _dot_tn(a, b):
    return lax.dot_general(a, b, (((0,), (0,)), ((), ())), preferred_element_type=F32)


def _split(v):
    hi = lax.bitcast_convert_type(lax.bitcast_convert_type(v, jnp.uint32) & jnp.uint32(0xFFFF0000), F32)
    return hi.astype(BF16), (v - hi).astype(BF16)


def _segsum(v, g):
    hi, lo = _split(v)
    w = g.shape[1]
    return jnp.concatenate([_dot(jnp.concatenate([hi[:, c:c + w], lo[:, c:c + w]], axis=1), g)
                            for c in range(0, v.shape[1], w)], axis=1)


def _rot_half(x):
    outs = []
    for c in range(x.shape[1] // LANES):
        xc = x[:, c * LANES:(c + 1) * LANES]
        lane = lax.broadcasted_iota(jnp.int32, xc.shape, 1)
        first = (lane % HEAD_DIM) < (HEAD_DIM // 2)
        outs.append(jnp.where(first, pltpu.roll(xc, LANES - 32, 1), pltpu.roll(xc, 32, 1)))
    return outs[0] if len(outs) == 1 else jnp.concatenate(outs, axis=1)


def _rms(x):
    return lax.rsqrt(jnp.mean(x * x, axis=-1, keepdims=True) + EPS)


def _rms_bwd(dy, x, w):
    rstd = _rms(x)
    xh = x * rstd
    dxh = dy * w
    dx = rstd * (dxh - xh * jnp.mean(dxh * xh, axis=-1, keepdims=True))
    return dx, dy * xh


def _sigmoid(x):
    return 1.0 / (1.0 + jnp.exp(-x))


def _sum4(ref):
    p = [ref[j].astype(F32) for j in range(N_CHIPS)]
    return (p[0] + p[1]) + (p[2] + p[3])


def _full(shape):
    n = len(shape)
    return pl.BlockSpec(shape, lambda *_: (0,) * n)


def _strided_spec(tm, r):
    return pl.BlockSpec((r, tm // r, GROUP), lambda i: (0, i, 0))


def _strided_shape(S, r, dtype):
    return jax.ShapeDtypeStruct((r, S // r, GROUP), dtype)


def _to_strided(scr, val, outs):
    chunks = range(GROUP // LANES)
    for k in chunks:
        scr[k] = val[:, _lanes(k)]
    for r, o_ref in outs:
        if r == 1:
            o_ref[0] = val.astype(o_ref.dtype)
            continue
        n = val.shape[0] // r
        for c in range(r):
            rows = pl.ds(c, n, stride=r)
            o_ref[c] = jnp.concatenate([scr.at[k][rows, :] for k in chunks], axis=1).astype(o_ref.dtype)


def _from_strided(scr, ref):
    r, n, _ = ref.shape
    if r == 1:
        return ref[0].astype(F32)
    chunks = range(GROUP // LANES)
    for c in range(r):
        plane = ref[c].astype(F32)
        for k in chunks:
            scr.at[k][pl.ds(c, n, stride=r), :] = plane[:, _lanes(k)]
    return jnp.concatenate([scr[k] for k in chunks], axis=1)


def _strided_scratch(tm):
    return pltpu.VMEM((GROUP // LANES, tm, LANES), F32)


def _tile4(t):
    return jnp.concatenate([t] * (GROUP // LANES), axis=1)


def _in_proj_fwd(x, attn_w, w_in_g, qw, kw, cos_t, sin_t, seg_ones, slots):
    S, D = x.shape
    tm = 512
    wcols = w_in_g.shape[2]
    nd = len(DILATIONS)
    ns = len(slots)

    def body(x_ref, aw_ref, w_ref, qw_ref, kw_ref, cos_ref, sin_ref, g_ref, *rest):
        slot_in, (h_ref, qa_ref, ka_ref), rest = rest[:ns], rest[ns:ns + 3], rest[ns + 3:]
        q_refs, k_refs, v_refs = rest[:nd], rest[nd:2 * nd], rest[2 * nd:3 * nd]
        qs_ref, ks_ref, vs_ref = rest[3 * nd:3 * nd + 3]
        slot_out, scr, sems = rest[3 * nd + 3:3 * nd + 3 + ns], rest[3 * nd + 3 + ns], rest[3 * nd + 4 + ns:]
        finish = _hosted_gathers(slot_in, slot_out, *sems, pl.program_id(0), S // tm) if ns else None
        xv = x_ref[...]
        h = (xv * _rms(xv) * aw_ref[...]).astype(BF16)
        h_ref[...] = h
        proj = jnp.concatenate([_dot(h, w_ref[j]) for j in range(N_CHIPS)], axis=1)
        qa = proj[:, 0 * GROUP:1 * GROUP]
        ka = proj[:, 1 * GROUP:2 * GROUP]
        qa_ref[...] = qa
        ka_ref[...] = ka
        _to_strided(scr, proj[:, 2 * GROUP:3 * GROUP], list(zip(DILATIONS, v_refs)))
        qs_ref[...] = proj[:, 3 * GROUP:4 * GROUP].astype(BF16)
        ks_ref[...] = proj[:, 4 * GROUP:5 * GROUP].astype(BF16)
        vs_ref[...] = proj[:, 5 * GROUP:6 * GROUP].astype(BF16)
        g = g_ref[...]
        cos = _tile4(cos_ref[...])
        sin = _tile4(sin_ref[...])
        for t, w_r, o_rs in ((qa, qw_ref, q_refs), (ka, kw_ref, k_refs)):
            rstd = lax.rsqrt(_segsum(t * t, g) * (1.0 / HEAD_DIM) + EPS)
            tn = t * rstd * w_r[...]
            _to_strided(scr, tn * cos + _rot_half(tn) * sin, list(zip(DILATIONS, o_rs)))
        if finish is not None:
            finish()

    row = lambda i: (i, 0)
    tile = lambda n, dt: jax.ShapeDtypeStruct((S, n), dt)
    planes = [_strided_spec(tm, r) for r in DILATIONS]
    h_in, h_out, h_shape, h_sems = _hosted_specs(slots)
    n_out = 6 + 3 * nd
    return pl.pallas_call(
        body, name="in_proj_fwd", grid=(S // tm,),
        in_specs=[pl.BlockSpec((tm, D), row), _full((1, D)), _full((N_CHIPS, D, wcols)),
                  _full((1, GROUP)), _full((1, GROUP)),
                  pl.BlockSpec((tm, LANES), row), pl.BlockSpec((tm, LANES), row),
                  _full((GROUP, GROUP // 2))] + h_in,
        out_specs=[pl.BlockSpec((tm, D), row)] + [pl.BlockSpec((tm, GROUP), row)] * 2 + planes * 3
                  + [pl.BlockSpec((tm, GROUP), row)] * 3 + h_out,
        out_shape=[tile(D, BF16), tile(GROUP, F32), tile(GROUP, F32)]
                  + [_strided_shape(S, r, BF16) for r in DILATIONS] * 3 + [tile(GROUP, BF16)] * 3 + h_shape,
        input_output_aliases={8 + a: n_out + a for a in range(ns)},
        scratch_shapes=[_strided_scratch(tm)] + h_sems,
    )(x, attn_w, w_in_g, qw, kw, cos_t, sin_t, seg_ones, *slots)


DIL_PLANES = 1


def _dil_fwd(q, k, v, slots):
    r, L, _ = q.shape
    nb = L // BLOCK
    P = GROUP // LANES
    PL = min(DIL_PLANES, r)
    ns = len(slots)
    units = [(pp, hp) for pp in range(PL) for hp in range(P)]

    def body(q_ref, kc_ref, kp_ref, vc_ref, vp_ref, *rest):
        o_ref, l_ref = rest[ns:ns + 2]
        n = pl.program_id(1)
        finish = (_hosted_gathers(rest[:ns], rest[ns + 2:2 * ns + 2], *rest[2 * ns + 2:],
                                  pl.program_id(0) * nb + n, (r // PL) * nb) if ns else None)
        rowi = lax.broadcasted_iota(jnp.int32, (BLOCK, BLOCK), 0)
        coli = lax.broadcasted_iota(jnp.int32, (BLOCK, BLOCK), 1)
        first = coli < HEAD_DIM
        masks = (coli <= rowi, jnp.logical_and(coli >= rowi, n > 0))
        s2 = {}
        for pp, hp in units:
            q2 = _scaled(q_ref[pp, :, _lanes(hp)])
            for b, k_ref in enumerate((kc_ref, kp_ref)):
                s2[pp, hp, b] = _dot_nt(q2, _by_head(k_ref[pp, :, _lanes(hp)], first))
        ps, inv, lse = {}, {}, {}
        for pp, hp in units:
            for h in range(2):
                s = [jnp.where(masks[b], s2[pp, hp, b][:, h * BLOCK:(h + 1) * BLOCK], NEG) for b in range(2)]
                m = jnp.maximum(jnp.max(s[0], axis=1, keepdims=True), jnp.max(s[1], axis=1, keepdims=True))
                p = [jnp.exp(s[b] - m) for b in range(2)]
                den = jnp.sum(p[0], axis=1, keepdims=True) + jnp.sum(p[1], axis=1, keepdims=True)
                ps[pp, hp, h] = [p[b].astype(BF16) for b in range(2)]
                inv[pp, hp, h] = 1.0 / den
                lse[pp, hp, h] = m + jnp.log(den)
        for pp, hp in units:
            o = jnp.zeros((BLOCK, LANES), F32)
            for b, v_ref in enumerate((vc_ref, vp_ref)):
                o = o + _dot(jnp.concatenate([ps[pp, hp, 0][b], ps[pp, hp, 1][b]], axis=1),
                             _by_head(v_ref[pp, :, _lanes(hp)], first))
            o_ref[pp, :, _lanes(hp)] = (o * jnp.where(first, inv[pp, hp, 0], inv[pp, hp, 1])).astype(BF16)
            l_ref[pp, :, _lanes(hp)] = jnp.where(first, lse[pp, hp, 0], lse[pp, hp, 1])
        if finish is not None:
            finish()

    cur = pl.BlockSpec((PL, BLOCK, GROUP), lambda c, n: (c, n, 0))
    prev = pl.BlockSpec((PL, BLOCK, GROUP), lambda c, n: (c, jnp.maximum(n - 1, 0), 0))
    h_in, h_out, h_shape, h_sems = _hosted_specs(slots)
    return pl.pallas_call(
        body, name="dil_fwd_r%d" % r, grid=(r // PL, nb),
        in_specs=[cur, cur, prev, cur, prev] + h_in, out_specs=[cur, cur] + h_out,
        out_shape=[jax.ShapeDtypeStruct(q.shape, BF16), jax.ShapeDtypeStruct(q.shape, F32)] + h_shape,
        input_output_aliases={5 + a: 2 + a for a in range(ns)},
        scratch_shapes=h_sems,
    )(q, k, k, v, v, *slots)


def _dil_bwd(q, k, v, do, lse, delta, parts, pairs):
    r, L, _ = q.shape
    nb = L // BLOCK
    P = GROUP // LANES
    PL = min(DIL_PLANES, r)
    scale = HEAD_DIM ** -0.5
    units = [(pp, hp) for pp in range(PL) for hp in range(P)]
    npt, npr = len(parts), len(pairs)
    nx = npt + npr

    def body(qc_ref, qn_ref, doc_ref, don_ref, lc_ref, ln_ref, dc_ref, dn_ref, k_ref, v_ref, *rest):
        x_in, (dq_ref, dk_ref, dv_ref), x_out = rest[:nx], rest[nx:nx + 3], rest[nx + 3:2 * nx + 3]
        carry, send, recv = rest[2 * nx + 3:]
        j = pl.program_id(1)
        step = pl.program_id(0) * nb + j
        plans = [_chip_send_plan(x_in[a], x_out[a], send.at[pl.ds(3 * a, 3)], recv.at[pl.ds(3 * a, 3)])
                 for a in range(npt)]
        plans += [_pair_send_plan(x_in[npt + a], x_out[npt + a], send.at[3 * npt + a], recv.at[3 * npt + a])
                  for a in range(npr)]

        def start():
            for begin, _ in plans:
                begin()

        def finish():
            for _, end in plans:
                end()

        rowi = lax.broadcasted_iota(jnp.int32, (BLOCK, BLOCK), 0)
        coli = lax.broadcasted_iota(jnp.int32, (BLOCK, BLOCK), 1)
        first = coli < HEAD_DIM
        sides = ((qc_ref, doc_ref, lc_ref, dc_ref, coli <= rowi),
                 (qn_ref, don_ref, ln_ref, dn_ref, jnp.logical_and(coli >= rowi, j < nb - 1)))

        @pl.when(j == 0)
        def _():
            carry[...] = jnp.zeros_like(carry)

        kcat, q2, do2, s2, dp2 = {}, {}, {}, {}, {}
        for pp, hp in units:
            kcat[pp, hp] = _by_head(k_ref[pp, :, _lanes(hp)], first)
            vcat = _by_head(v_ref[pp, :, _lanes(hp)], first)
            for x, (q_r, do_r, _, _, _) in enumerate(sides):
                q2[pp, hp, x] = _scaled(q_r[pp, :, _lanes(hp)])
                do2[pp, hp, x] = do_r[pp, :, _lanes(hp)]
                s2[pp, hp, x] = _dot_nt(q2[pp, hp, x], kcat[pp, hp])
                dp2[pp, hp, x] = _dot_nt(do2[pp, hp, x], vcat)
        pcat, dscat = {}, {}
        for pp, hp in units:
            for x, (_, _, l_r, d_r, msk) in enumerate(sides):
                ps, dss = [], []
                for h in range(2):
                    col = hp * LANES + h * HEAD_DIM
                    half = slice(h * BLOCK, (h + 1) * BLOCK)
                    p = jnp.where(msk, jnp.exp(s2[pp, hp, x][:, half] - l_r[pp, :, col:col + 1]), 0.0)
                    ps.append(p.astype(BF16))
                    dss.append((p * (dp2[pp, hp, x][:, half] - d_r[pp, :, col:col + 1])).astype(BF16))
                pcat[pp, hp, x] = jnp.concatenate(ps, axis=1)
                dscat[pp, hp, x] = jnp.concatenate(dss, axis=1)
        for pp, hp in units:
            dv2 = _dot_tn(pcat[pp, hp, 0], do2[pp, hp, 0]) + _dot_tn(pcat[pp, hp, 1], do2[pp, hp, 1])
            dk2 = _dot_tn(dscat[pp, hp, 0], q2[pp, hp, 0]) + _dot_tn(dscat[pp, hp, 1], q2[pp, hp, 1])
            dv_ref[pp, :, _lanes(hp)] = jnp.where(first, dv2[:BLOCK], dv2[BLOCK:]).astype(BF16)
            dk_ref[pp, :, _lanes(hp)] = jnp.where(first, dk2[:BLOCK], dk2[BLOCK:]).astype(BF16)
            dq_ref[pp, :, _lanes(hp)] = (carry[pp, :, _lanes(hp)]
                                         + _dot(dscat[pp, hp, 0], kcat[pp, hp]) * scale).astype(BF16)
            carry[pp, :, _lanes(hp)] = _dot(dscat[pp, hp, 1], kcat[pp, hp]) * scale
        pl.when(step == 0)(start)
        pl.when(step == (r // PL) * nb - 1)(finish)

    cur = pl.BlockSpec((PL, BLOCK, GROUP), lambda c, n: (c, n, 0))
    nxt = pl.BlockSpec((PL, BLOCK, GROUP), lambda c, n: (c, jnp.minimum(n + 1, nb - 1), 0))
    return pl.pallas_call(
        body, name="dil_bwd_r%d" % r, grid=(r // PL, nb),
        in_specs=[cur, nxt, cur, nxt, cur, nxt, cur, nxt, cur, cur] + [HBM] * nx, out_specs=[cur, cur, cur] + [HBM] * nx,
        out_shape=[jax.ShapeDtypeStruct(q.shape, BF16)] * 3 + [jax.ShapeDtypeStruct(p.shape, p.dtype) for p in parts]
                  + [jax.ShapeDtypeStruct((g.shape[0], g.shape[1] // 2, g.shape[2]), g.dtype) for g in pairs],
        scratch_shapes=[pltpu.VMEM((PL, BLOCK, GROUP), F32), pltpu.SemaphoreType.DMA((3 * npt + npr,)),
                        pltpu.SemaphoreType.DMA((3 * npt + npr,))],
    )(q, q, do, do, lse, lse, delta, delta, k, v, *parts, *pairs)


SB_TILES = 2
SB_PAIRS_FWD = 4
SB_PAIRS_BWD = 4
SB_DEAD = -110.0


def _lanes(hp):
    return slice(hp * LANES, (hp + 1) * LANES)


def _sb_logits(z, valid):
    e = jnp.exp(-jnp.abs(z))
    lb = jnp.minimum(z, 0.0) - jnp.log(1.0 + e)
    lk = lb - z
    if valid is not None:
        lk = jnp.where(valid, lk, 0.0)
    return e, lb, lk


def _by_head(t, first):
    zero = jnp.zeros_like(t)
    return jnp.concatenate([jnp.where(first, t, zero), jnp.where(first, zero, t)], axis=0)


def _sb_valid(i, j):
    rowi = lax.broadcasted_iota(jnp.int32, (BLOCK, BLOCK), 0)
    coli = lax.broadcasted_iota(jnp.int32, (BLOCK, BLOCK), 1)
    return (coli - rowi) < (i - j) * BLOCK


def _scaled(q):
    return (q.astype(F32) * (HEAD_DIM ** -0.5)).astype(BF16)


def _hosted_gathers(refs_in, refs_out, send, recv, step, steps):
    plans = [_gather_plan(refs_in[a], refs_out[a], send.at[pl.ds(6 * a, 6)], recv.at[pl.ds(6 * a, 6)])
             for a in range(len(refs_in))]

    def stages():
        for stage, at in ((0, 0), (1, min((7 * steps) // 8, steps - 2)), (2, steps - 1)):
            @pl.when(step == at)
            def _():
                for plan in plans:
                    plan[stage]()

    return stages


def _hosted_specs(slots):
    n = len(slots)
    sems = [pltpu.SemaphoreType.DMA((6 * n,))] * 2 if n else []
    return [HBM] * n, [HBM] * n, [jax.ShapeDtypeStruct(s.shape, s.dtype) for s in slots], sems


def _sb_fwd(qs, ks, vs, tri_later, slots):
    S = qs.shape[0]
    P = SB_PAIRS_FWD
    W = P * LANES
    ns = len(slots)

    def body(q_ref, k_ref, v_ref, u_ref, *rest):
        o_ref, lt_ref, from_ref = rest[ns:ns + 3]
        i = pl.program_id(1)
        finish = _hosted_gathers(rest[:ns], rest[ns + 3:2 * ns + 3], *rest[2 * ns + 3:], i, S // BLOCK) if ns else None
        first = lax.broadcasted_iota(jnp.int32, (BLOCK, LANES), 1) < HEAD_DIM
        q2 = [_scaled(q_ref[:, _lanes(hp)]) for hp in range(P)]

        def walk(tiles, carry):
            runs, accs = list(carry[0]), list(carry[1])
            units = [(t, hp) for t in range(len(tiles)) for hp in range(P)]
            offs = [pl.multiple_of(j * BLOCK, BLOCK) for j, _ in tiles]
            valids = [_sb_valid(i, j) if diag else None for j, diag in tiles]
            z2s, lbs, c2s = {}, {}, {}
            for t, hp in units:
                z2s[t, hp] = _dot_nt(q2[hp], _by_head(k_ref[pl.ds(offs[t], BLOCK), _lanes(hp)], first))
            for t, hp in units:
                for h in range(2):
                    _, lb, lk = _sb_logits(z2s[t, hp][:, h * BLOCK:(h + 1) * BLOCK], valids[t])
                    lbs[t, hp, h] = lb
                    c2s[t, hp, h] = _dot(jnp.concatenate(_split(lk), axis=1), u_ref[...])
            for t, hp in units:
                a2 = []
                for h in range(2):
                    a = jnp.exp(lbs[t, hp, h] + c2s[t, hp, h][:, :BLOCK] + runs[2 * hp + h])
                    if valids[t] is not None:
                        a = jnp.where(valids[t], a, 0.0)
                    a2.append(a.astype(BF16))
                    runs[2 * hp + h] = runs[2 * hp + h] + c2s[t, hp, h][:, BLOCK:]
                vcat = _by_head(v_ref[pl.ds(offs[t], BLOCK), _lanes(hp)], first)
                accs[hp] = accs[hp] + _dot(jnp.concatenate(a2, axis=1), vcat)
            return tuple(runs), tuple(accs)

        def chunk(ci, carry):
            return walk([(ci * SB_TILES + t, False) for t in reversed(range(SB_TILES))], carry)

        def alive(runs):
            top = functools.reduce(jnp.maximum, runs)
            return (jnp.max(top) > SB_DEAD).astype(jnp.int32)

        def step(c):
            t, _, runs, accs = c
            runs, accs = chunk(nfull - 1 - t, (runs, accs))
            return t + 1, alive(runs), runs, accs

        zero = jnp.zeros((BLOCK, LANES), F32)
        nfull = i // SB_TILES
        ragged = [functools.partial(walk, [(i, True)] + [(i - 1 - m, False) for m in range(extra)])
                  for extra in range(SB_TILES)]
        runs, accs = lax.switch(i % SB_TILES, ragged, ((zero,) * (2 * P), (zero,) * P))
        done, _, runs, accs = lax.while_loop(lambda c: jnp.logical_and(c[0] < nfull, c[1] > 0), step,
                                             (jnp.int32(0), alive(runs), runs, accs))
        for hp in range(P):
            o_ref[:, _lanes(hp)] = accs[hp]
            lt_ref[:, _lanes(hp)] = jnp.where(first, runs[2 * hp], runs[2 * hp + 1])
        from_ref[...] = jnp.full(from_ref.shape, nfull - done, jnp.int32)
        if finish is not None:
            finish()

    assert W == GROUP
    blk = pl.BlockSpec((BLOCK, W), lambda hp, i: (i, hp))
    col = pl.BlockSpec((S, W), lambda hp, i: (0, hp))
    h_in, h_out, h_shape, h_sems = _hosted_specs(slots)
    return pl.pallas_call(
        body, name="sb_fwd", grid=(GROUP // W, S // BLOCK),
        in_specs=[blk, col, col, _full((2 * BLOCK, 2 * BLOCK))] + h_in,
        out_specs=[blk, blk, pl.BlockSpec((1, 8, LANES), lambda hp, i: (i, 0, 0))] + h_out,
        out_shape=[jax.ShapeDtypeStruct((S, GROUP), F32)] * 2
                  + [jax.ShapeDtypeStruct((S // BLOCK, 8, LANES), jnp.int32)] + h_shape,
        input_output_aliases={4 + a: 3 + a for a in range(ns)},
        scratch_shapes=h_sems,
    )(qs, ks, vs, tri_later, *slots)


def _sb_bwd(first_chunk, qs, ks, vs, do, ltot, tri_upto, tri_before):
    S = qs.shape[0]
    P = SB_PAIRS_BWD
    W = P * LANES

    def body(from_ref, q_ref, k_ref, v_ref, do_ref, lt_ref, w_ref, x_ref, dq_ref, dk_ref, dv_ref):
        i = pl.program_id(1)

        @pl.when(i == 0)
        def _():
            dk_ref[...] = jnp.zeros_like(dk_ref)
            dv_ref[...] = jnp.zeros_like(dv_ref)

        first = lax.broadcasted_iota(jnp.int32, (BLOCK, LANES), 1) < HEAD_DIM
        q2 = [_scaled(q_ref[:, _lanes(hp)]) for hp in range(P)]
        do2 = [do_ref[:, _lanes(hp)] for hp in range(P)]
        totals = [jnp.broadcast_to(lt_ref[:, n * HEAD_DIM:n * HEAD_DIM + 1], (BLOCK, LANES)) for n in range(2 * P)]

        def walk(tiles, carry):
            keeps, grads, dqs = list(carry[0]), list(carry[1]), list(carry[2])
            units = [(t, hp) for t in range(len(tiles)) for hp in range(P)]
            offs = [pl.multiple_of(j * BLOCK, BLOCK) for j, _ in tiles]
            valids = [_sb_valid(i, j) if diag else None for j, diag in tiles]
            kcat, z2, da2, es, lbs, c2s, as_, des, p2s = {}, {}, {}, {}, {}, {}, {}, {}, {}
            for t, hp in units:
                kcat[t, hp] = _by_head(k_ref[pl.ds(offs[t], BLOCK), _lanes(hp)], first)
                z2[t, hp] = _dot_nt(q2[hp], kcat[t, hp])
                da2[t, hp] = _dot_nt(do2[hp], _by_head(v_ref[pl.ds(offs[t], BLOCK), _lanes(hp)], first))
            for t, hp in units:
                for h in range(2):
                    es[t, hp, h], lbs[t, hp, h], lk = _sb_logits(z2[t, hp][:, h * BLOCK:(h + 1) * BLOCK], valids[t])
                    c2s[t, hp, h] = _dot(jnp.concatenate(_split(lk), axis=1), w_ref[...])
            for t, hp in units:
                for h in range(2):
                    n = 2 * hp + h
                    a = jnp.exp(lbs[t, hp, h] + (totals[n] - (keeps[n] + c2s[t, hp, h][:, :BLOCK])))
                    if valids[t] is not None:
                        a = jnp.where(valids[t], a, 0.0)
                    keeps[n] = keeps[n] + c2s[t, hp, h][:, BLOCK:]
                    de = a * da2[t, hp][:, h * BLOCK:(h + 1) * BLOCK]
                    as_[t, hp, h], des[t, hp, h] = a.astype(BF16), de
                    p2s[t, hp, h] = _dot(jnp.concatenate(_split(de), axis=1), x_ref[...])
            for t, hp in units:
                dz2 = []
                for h in range(2):
                    n = 2 * hp + h
                    e = es[t, hp, h]
                    sig = jnp.where(z2[t, hp][:, h * BLOCK:(h + 1) * BLOCK] >= 0.0, 1.0, e) / (1.0 + e)
                    dz = des[t, hp, h] * (1.0 - sig) - (grads[n] + p2s[t, hp, h][:, :BLOCK]) * sig
                    if valids[t] is not None:
                        dz = jnp.where(valids[t], dz, 0.0)
                    grads[n] = grads[n] + p2s[t, hp, h][:, BLOCK:]
                    dz2.append(dz.astype(BF16))
                dzcat = jnp.concatenate(dz2, axis=1)
                dk2 = _dot_tn(dzcat, q2[hp])
                dv2 = _dot_tn(jnp.concatenate([as_[t, hp, 0], as_[t, hp, 1]], axis=1), do2[hp])
                dk_ref[pl.ds(offs[t], BLOCK), _lanes(hp)] += jnp.where(first, dk2[:BLOCK], dk2[BLOCK:])
                dv_ref[pl.ds(offs[t], BLOCK), _lanes(hp)] += jnp.where(first, dv2[:BLOCK], dv2[BLOCK:])
                dqs[hp] = dqs[hp] + _dot(dzcat, kcat[t, hp])
            return tuple(keeps), tuple(grads), tuple(dqs)

        zero = jnp.zeros((BLOCK, LANES), F32)
        nfull = i // SB_TILES
        carry = lax.fori_loop(
            from_ref[i], nfull, lambda ci, c: walk([(ci * SB_TILES + t, False) for t in range(SB_TILES)], c),
            ((zero,) * (2 * P), (zero,) * (2 * P), (zero,) * P))
        ragged = [functools.partial(walk, [(i - m, False) for m in range(extra, 0, -1)] + [(i, True)])
                  for extra in range(SB_TILES)]
        carry = lax.switch(i % SB_TILES, ragged, carry)
        for hp in range(P):
            dq_ref[:, _lanes(hp)] = carry[2][hp] * (HEAD_DIM ** -0.5)

    blk = pl.BlockSpec((BLOCK, W), lambda hp, i, fr: (i, hp))
    col = pl.BlockSpec((S, W), lambda hp, i, fr: (0, hp))
    tri = pl.BlockSpec((2 * BLOCK, 2 * BLOCK), lambda hp, i, fr: (0, 0))
    return pl.pallas_call(
        body, name="sb_bwd",
        grid_spec=pltpu.PrefetchScalarGridSpec(
            num_scalar_prefetch=1, grid=(GROUP // W, S // BLOCK),
            in_specs=[blk, col, col, blk, blk, tri, tri], out_specs=[blk, col, col]),
        out_shape=[jax.ShapeDtypeStruct((S, GROUP), F32)] * 3,
    )(first_chunk, qs, ks, vs, do, ltot, tri_upto, tri_before)


def _out_proj_fwd(o_br, l_br, o_sb, x, w_dil, w_sbn, w_out_g, slots):
    S, D = x.shape
    tm = 512
    ns = len(slots)

    def body(o0, o1, o2, l0, l1, l2, os_ref, x_ref, wd_ref, ws_ref, w_ref, *rest):
        od_ref, s0, s1, s2, x1_ref = rest[ns:ns + 5]
        scr = rest[2 * ns + 5]
        finish = (_hosted_gathers(rest[:ns], rest[ns + 5:2 * ns + 5], *rest[2 * ns + 6:], pl.program_id(0), S // tm)
                  if ns else None)
        ls = [_from_strided(scr, l) for l in (l0, l1, l2)]
        os_ = [_from_strided(scr, o) for o in (o0, o1, o2)]
        m = jnp.maximum(jnp.maximum(ls[0], ls[1]), ls[2])
        es = [jnp.exp(l - m) for l in ls]
        den = es[0] + es[1] + es[2]
        od = (es[0] * os_[0] + es[1] * os_[1] + es[2] * os_[2]) / den
        od_ref[...] = od
        _to_strided(scr, m + jnp.log(den), list(zip(DILATIONS, (s0, s1, s2))))
        osb = os_ref[...]
        mixed = jnp.concatenate([(od * _rms(od) * wd_ref[...]).astype(BF16),
                                 (osb * _rms(osb) * ws_ref[...]).astype(BF16)], axis=1)
        x1_ref[...] = x_ref[...] + _dot(mixed, w_ref[...])
        if finish is not None:
            finish()

    row = lambda i: (i, 0)
    g = pl.BlockSpec((tm, GROUP), row)
    d = pl.BlockSpec((tm, D), row)
    planes = [_strided_spec(tm, r) for r in DILATIONS]
    h_in, h_out, h_shape, h_sems = _hosted_specs(slots)
    return pl.pallas_call(
        body, name="out_proj_fwd", grid=(S // tm,),
        in_specs=planes * 2 + [g, d, _full((1, GROUP)), _full((1, GROUP)), _full((2 * GROUP, D))] + h_in,
        out_specs=[g] + planes + [d] + h_out,
        out_shape=[jax.ShapeDtypeStruct((S, GROUP), F32)] + [_strided_shape(S, r, F32) for r in DILATIONS]
                  + [jax.ShapeDtypeStruct((S, D), F32)] + h_shape,
        input_output_aliases={11 + a: 5 + a for a in range(ns)},
        scratch_shapes=[_strided_scratch(tm)] + h_sems,
    )(*o_br, *l_br, o_sb, x, w_dil, w_sbn, w_out_g, *slots)


def _ffn_fwd(x1, target, ffn_w, wg_g, wu_g, wd_g):
    S, D = x1.shape
    F = wg_g.shape[1]
    tm = 512
    nt = S // tm

    def body(x_ref, t_ref, nw_ref, wg_ref, wu_ref, wd_ref, h_ref, g_ref, u_ref, dy_ref, loss_ref, h_s, acc):
        j = pl.program_id(1)

        @pl.when(j == 0)
        def _():
            xv = x_ref[...]
            h = (xv * _rms(xv) * nw_ref[...]).astype(BF16)
            h_s[...] = h
            h_ref[...] = h
            acc[...] = xv

        h = h_s[...]
        g = _dot_nt(h, wg_ref[0])
        u = _dot_nt(h, wu_ref[0])
        g_ref[0] = g.astype(BF16)
        u_ref[0] = u.astype(BF16)
        a = (g * _sigmoid(g) * u).astype(BF16)
        acc[...] += _dot(a, wd_ref[0])

        @pl.when(j == N_CHIPS - 1)
        def _():
            err = acc[...] - t_ref[...]
            dy_ref[...] = err * (1.0 / D)
            loss_ref[...] = jnp.full(loss_ref.shape, jnp.sum(err * err), F32)

    row = lambda t, j: (t, 0)
    shard = lambda t, j: (j, 0, 0)
    act = lambda t, j: (j, t, 0)
    return pl.pallas_call(
        body, name="ffn_fwd", grid=(nt, N_CHIPS),
        in_specs=[pl.BlockSpec((tm, D), row), pl.BlockSpec((tm, D), row), pl.BlockSpec((1, D), lambda t, j: (0, 0))]
                 + [pl.BlockSpec((1, F, D), shard)] * 3,
        out_specs=[pl.BlockSpec((tm, D), row), pl.BlockSpec((1, tm, F), act), pl.BlockSpec((1, tm, F), act),
                   pl.BlockSpec((tm, D), row), pl.BlockSpec((1, 8, LANES), lambda t, j: (t, 0, 0))],
        out_shape=[jax.ShapeDtypeStruct((S, D), BF16), jax.ShapeDtypeStruct((N_CHIPS, S, F), BF16),
                   jax.ShapeDtypeStruct((N_CHIPS, S, F), BF16), jax.ShapeDtypeStruct((S, D), F32),
                   jax.ShapeDtypeStruct((nt, 8, LANES), F32)],
        scratch_shapes=[pltpu.VMEM((tm, D), BF16), pltpu.VMEM((tm, D), F32)],
    )(x1, target, ffn_w, wg_g, wu_g, wd_g)


def _ffn_bwd(h2, dy, g, u, wg_g, wu_g, wd_g):
    S, D = dy.shape
    F = wg_g.shape[1]
    tm = 512

    def body(h_ref, dy_ref, g_ref, u_ref, wg_ref, wu_ref, wd_ref, dwg_ref, dwu_ref, dwd_ref, dh_ref, *narrow):
        t = pl.program_id(1)

        @pl.when(t == 0)
        def _():
            dwg_ref[...] = jnp.zeros_like(dwg_ref)
            dwu_ref[...] = jnp.zeros_like(dwu_ref)
            dwd_ref[...] = jnp.zeros_like(dwd_ref)

        h = h_ref[...]
        dyb = dy_ref[...].astype(BF16)
        gv = g_ref[0].astype(F32)
        uv = u_ref[0].astype(F32)
        da = _dot_nt(dyb, wd_ref[0])
        sg = _sigmoid(gv)
        silu = gv * sg
        du = (da * silu).astype(BF16)
        dg = (da * uv * (sg * (1.0 + gv * (1.0 - sg)))).astype(BF16)
        dwd_ref[0] += _dot_tn((silu * uv).astype(BF16), dyb)
        dwg_ref[0] += _dot_tn(dg, h)
        dwu_ref[0] += _dot_tn(du, h)
        dh_ref[0] = (_dot(dg, wg_ref[0]) + _dot(du, wu_ref[0])).astype(BF16)

        @pl.when(t == S // tm - 1)
        def _():
            for full, half in zip((dwg_ref, dwu_ref, dwd_ref), narrow):
                half[...] = full[...].astype(BF16)

    row = lambda j, t: (t, 0)
    shard = lambda j, t: (j, 0, 0)
    act = lambda j, t: (j, t, 0)
    return pl.pallas_call(
        body, name="ffn_bwd", grid=(N_CHIPS, S // tm),
        in_specs=[pl.BlockSpec((tm, D), row), pl.BlockSpec((tm, D), row),
                  pl.BlockSpec((1, tm, F), act), pl.BlockSpec((1, tm, F), act)] + [pl.BlockSpec((1, F, D), shard)] * 3,
        out_specs=[pl.BlockSpec((1, F, D), shard)] * 3 + [pl.BlockSpec((1, tm, D), act)]
                  + [pl.BlockSpec((1, F, D), shard)] * 3,
        out_shape=[jax.ShapeDtypeStruct((N_CHIPS, F, D), F32)] * 3 + [jax.ShapeDtypeStruct((N_CHIPS, S, D), BF16)]
                  + [jax.ShapeDtypeStruct((N_CHIPS, F, D), BF16)] * 3,
    )(h2, dy, g, u, wg_g, wu_g, wd_g)


def _out_proj_bwd(dh2p, dy, x1, ffn_w, w_out_g, o_dil, o_sb, w_dil, w_sbn, seg_ones, ffn_grads):
    S, D = dy.shape
    tm = 512
    ng = len(ffn_grads)

    def body(dh_ref, dy_ref, x1_ref, nw_ref, w_ref, od_ref, os_ref, wd_ref, ws_ref, g_ref, *rest):
        gin, rest = rest[:ng], rest[ng:]
        dx1_ref, dod0, dod1, dod2, dos_ref, dl0, dl1, dl2, dw_ref, dnw_ref, dwd_ref, dws_ref = rest[:12]
        gout, (scr, send, recv) = rest[12:12 + ng], rest[12 + ng:]
        i = pl.program_id(0)
        plans = [_pair_send_plan(gin[a], gout[a], send.at[a], recv.at[a]) for a in range(ng)]

        @pl.when(i == 0)
        def _():
            for r_ in (dw_ref, dnw_ref, dwd_ref, dws_ref):
                r_[...] = jnp.zeros_like(r_)

        dh2 = _sum4(dh_ref)
        dxn, dwn = _rms_bwd(dh2, x1_ref[...], nw_ref[...])
        dnw_ref[...] += jnp.sum(dwn, axis=0, keepdims=True)
        dx1 = dy_ref[...] + dxn
        dx1_ref[...] = dx1
        dx1b = dx1.astype(BF16)
        dmix = _dot_nt(dx1b, w_ref[...])
        od = od_ref[...]
        osb = os_ref[...]
        mixed = jnp.concatenate([(od * _rms(od) * wd_ref[...]).astype(BF16),
                                 (osb * _rms(osb) * ws_ref[...]).astype(BF16)], axis=1)
        dw_ref[...] += _dot_tn(mixed, dx1b)
        do, dwo = _rms_bwd(dmix[:, :GROUP], od, wd_ref[...])
        dwd_ref[...] += jnp.sum(dwo, axis=0, keepdims=True)
        _to_strided(scr, do, list(zip(DILATIONS, (dod0, dod1, dod2))))
        _to_strided(scr, _segsum(do * od, g_ref[...]), list(zip(DILATIONS, (dl0, dl1, dl2))))
        do, dwo = _rms_bwd(dmix[:, GROUP:], osb, ws_ref[...])
        dws_ref[...] += jnp.sum(dwo, axis=0, keepdims=True)
        dos_ref[...] = do.astype(BF16)

        @pl.when(i == 0)
        def _():
            for start, _ in plans:
                start()

        @pl.when(i == S // tm - 1)
        def _():
            for _, finish in plans:
                finish()

    row = lambda i: (i, 0)
    gsp = pl.BlockSpec((tm, GROUP), row)
    dsp = pl.BlockSpec((tm, D), row)
    planes = [_strided_spec(tm, r) for r in DILATIONS]
    halves = [jax.ShapeDtypeStruct((g.shape[0], g.shape[1] // 2, g.shape[2]), g.dtype) for g in ffn_grads]
    return pl.pallas_call(
        body, name="out_proj_bwd", grid=(S // tm,),
        in_specs=[pl.BlockSpec((N_CHIPS, tm, D), lambda i: (0, i, 0)), dsp, dsp, _full((1, D)), _full((2 * GROUP, D)),
                  gsp, gsp, _full((1, GROUP)), _full((1, GROUP)), _full((GROUP, GROUP // 2))] + [HBM] * ng,
        out_specs=[dsp] + planes + [gsp] + planes
                  + [_full((2 * GROUP, D)), _full((1, D)), _full((1, GROUP)), _full((1, GROUP))] + [HBM] * ng,
        out_shape=[jax.ShapeDtypeStruct((S, D), F32)] + [_strided_shape(S, r, BF16) for r in DILATIONS]
                  + [jax.ShapeDtypeStruct((S, GROUP), BF16)] + [_strided_shape(S, r, F32) for r in DILATIONS]
                  + [jax.ShapeDtypeStruct((2 * GROUP, D), F32),
                     jax.ShapeDtypeStruct((1, D), F32), jax.ShapeDtypeStruct((1, GROUP), F32),
                     jax.ShapeDtypeStruct((1, GROUP), F32)] + halves,
        scratch_shapes=[_strided_scratch(tm), pltpu.SemaphoreType.DMA((ng,)), pltpu.SemaphoreType.DMA((ng,))],
    )(dh2p, dy, x1, ffn_w, w_out_g, o_dil, o_sb, w_dil, w_sbn, seg_ones, *ffn_grads)


def _attn_in_bwd(dq_br, dk_br, dv_br, dqs, dks, dvs, qa, ka, qw, kw, cos_t, sin_t, seg_ones, h, w_in_g, x, dx1, attn_w):
    S, D = x.shape
    wc = w_in_g.shape[2]
    tm = 256

    def body(q0, q1, q2, k0, k1, k2, v0, v1, v2, dqs_ref, dks_ref, dvs_ref, qa_ref, ka_ref, qw_ref, kw_ref,
             cos_ref, sin_ref, g_ref, h_ref, w_ref, x_ref, dx1_ref, aw_ref,
             gx_ref, dw_ref, daw_ref, dqw_ref, dkw_ref, accq, acck, scr):
        i = pl.program_id(0)

        @pl.when(i == 0)
        def _():
            accq[...] = jnp.zeros_like(accq)
            acck[...] = jnp.zeros_like(acck)
            dw_ref[...] = jnp.zeros_like(dw_ref)
            daw_ref[...] = jnp.zeros_like(daw_ref)

        def branches(refs):
            return (_from_strided(scr, refs[0]) + _from_strided(scr, refs[1])) + _from_strided(scr, refs[2])

        g = g_ref[...]
        cos = _tile4(cos_ref[...])
        sin = _tile4(sin_ref[...])
        pieces = []
        for refs, pre_ref, w_r, acc in (((q0, q1, q2), qa_ref, qw_ref, accq), ((k0, k1, k2), ka_ref, kw_ref, acck)):
            dh = branches(refs)
            dn = dh * cos + _rot_half(dh * sin)
            pre = pre_ref[...]
            rstd = lax.rsqrt(_segsum(pre * pre, g) * (1.0 / HEAD_DIM) + EPS)
            xh = pre * rstd
            acc[...] += jnp.sum(dn * xh, axis=0, keepdims=True)
            dxh = dn * w_r[...]
            pieces.append((rstd * (dxh - xh * (_segsum(dxh * xh, g) * (1.0 / HEAD_DIM)))).astype(BF16))
        pieces += [branches((v0, v1, v2)).astype(BF16), dqs_ref[...].astype(BF16), dks_ref[...].astype(BF16),
                   dvs_ref[...].astype(BF16)]
        dproj = jnp.concatenate(pieces, axis=1)
        hv = h_ref[...]
        dh = jnp.zeros((tm, D), F32)
        for j in range(N_CHIPS):
            dp = dproj[:, j * wc:(j + 1) * wc]
            dw_ref[j] += _dot_tn(hv, dp)
            dh = dh + _dot_nt(dp, w_ref[j])
        dx, dw = _rms_bwd(dh, x_ref[...], aw_ref[...])
        daw_ref[...] += jnp.sum(dw, axis=0, keepdims=True)
        gx_ref[...] = dx1_ref[...] + dx

        @pl.when(i == S // tm - 1)
        def _():
            for acc, o_ref in ((accq, dqw_ref), (acck, dkw_ref)):
                a = acc[...]
                pair = (a[:, 0:LANES] + a[:, LANES:2 * LANES]) + (a[:, 2 * LANES:3 * LANES] + a[:, 3 * LANES:4 * LANES])
                o_ref[...] = pair + pltpu.roll(pair, HEAD_DIM, 1)

    row = lambda i: (i, 0)
    gsp = pl.BlockSpec((tm, GROUP), row)
    dsp = pl.BlockSpec((tm, D), row)
    tab = pl.BlockSpec((tm, LANES), row)
    planes = [_strided_spec(tm, r) for r in DILATIONS]
    return pl.pallas_call(
        body, name="attn_in_bwd", grid=(S // tm,),
        in_specs=planes * 3 + [gsp] * 5 + [_full((1, GROUP)), _full((1, GROUP)), tab, tab, _full((GROUP, GROUP // 2)),
                                          dsp, _full((N_CHIPS, D, wc)), dsp, dsp, _full((1, D))],
        out_specs=[dsp, _full((N_CHIPS, D, wc)), _full((1, D)), _full((1, LANES)), _full((1, LANES))],
        out_shape=[jax.ShapeDtypeStruct((S, D), F32), jax.ShapeDtypeStruct((N_CHIPS, D, wc), F32),
                   jax.ShapeDtypeStruct((1, D), F32), jax.ShapeDtypeStruct((1, LANES), F32),
                   jax.ShapeDtypeStruct((1, LANES), F32)],
        scratch_shapes=[pltpu.VMEM((1, GROUP), F32), pltpu.VMEM((1, GROUP), F32), _strided_scratch(tm)],
    )(*dq_br, *dk_br, *dv_br, dqs, dks, dvs, qa, ka, qw, kw, cos_t, sin_t, seg_ones, h, w_in_g, x, dx1, attn_w)


def _constants(S):
    pos = jnp.arange(S, dtype=F32)
    inv_freq = ROPE_THETA ** (-jnp.arange(0, HEAD_DIM, 2, dtype=F32) / HEAD_DIM)
    ang_a = pos[::BLOCK, None] * inv_freq[None, :]
    ang_b = pos[:BLOCK, None] * inv_freq[None, :]
    ca, sa, cb, sb = jnp.cos(ang_a)[:, None], jnp.sin(ang_a)[:, None], jnp.cos(ang_b)[None], jnp.sin(ang_b)[None]
    cos = (ca * cb - sa * sb).reshape(S, HEAD_DIM // 2)
    sin = (sa * cb + ca * sb).reshape(S, HEAD_DIM // 2)
    cos_t = jnp.concatenate([cos, cos] * 2, axis=1)
    sin_t = jnp.concatenate([-sin, sin] * 2, axis=1)
    idx = jnp.arange(GROUP // 2)
    seg_ones = (idx[:, None] // HEAD_DIM == idx[None, :] // HEAD_DIM).astype(BF16)
    seg_ones = jnp.concatenate([seg_ones, seg_ones], axis=0)
    r = jnp.arange(BLOCK)
    ones = jnp.ones((BLOCK, BLOCK), BF16)
    tris = [jnp.concatenate([jnp.concatenate([m.astype(BF16), ones], axis=1)] * 2, axis=0) for m in
            (r[:, None] > r[None, :],
             r[:, None] <= r[None, :],
             r[:, None] < r[None, :])]
    return cos_t, sin_t, seg_ones, tris


FFN_NAMES = ("w_gate", "w_up", "w_down")


def _device_step(x, target, attn_w, qn_w, kn_w, dil_w, sbn_w, ffn_w, w_in_g, w_out_slots, ffn_slots, core, chip):
    S = x.shape[0]
    cos_t, sin_t, seg_ones, (tri_later, tri_upto, tri_before) = _constants(S)
    reps = GROUP // HEAD_DIM
    qw = jnp.tile(qn_w, (1, reps))
    kw = jnp.tile(kn_w, (1, reps))

    nd = len(DILATIONS)
    h, qa, ka, *rest, wg_g = _in_proj_fwd(x, attn_w, w_in_g, qw, kw, cos_t, sin_t, seg_ones, ffn_slots[:1])
    qh, kh, va, (qs, ks, vs) = rest[:nd], rest[nd:2 * nd], rest[2 * nd:3 * nd], rest[3 * nd:]
    hosted = ([], [w_out_slots], [])
    branches = [_dil_fwd(qh[b], kh[b], va[b], hosted[b]) for b in range(nd)]
    w_out_g = branches[1][2].reshape(-1, x.shape[1])
    o_sb, ltot, walked, wu_g, wd_g = _sb_fwd(qs, ks, vs, tri_later, ffn_slots[1:])
    o_dil, *lse, x1 = _out_proj_fwd([b[0] for b in branches], [b[1] for b in branches], o_sb, x, dil_w, sbn_w,
                                    w_out_g, [])
    h2, g, u, dy, loss_parts = _ffn_fwd(x1, target, ffn_w, wg_g, wu_g, wd_g)

    *ffn_grads, dh2p, n0, n1, n2 = _ffn_bwd(h2, dy, g, u, wg_g, wu_g, wd_g)
    dx1, *mid, dw_out, dffn_w, ddil_w, dsbn_w, p0, p1, p2 = _out_proj_bwd(
        dh2p, dy, x1, ffn_w, w_out_g, o_dil, o_sb, dil_w, sbn_w, seg_ones, [n0, n1, n2])
    do_dil, do_sb, delta = mid[:nd], mid[nd], mid[nd + 1:]
    parts = [_pair_sum(gr, fr, core, n) for gr, fr, n in zip(ffn_grads, (p0, p1, p2), FFN_NAMES)]
    dqs, dks, dvs = _sb_bwd(walked[:, 0, 0], qs, ks, vs, do_sb, ltot, tri_upto, tri_before)
    dw_out = dw_out.reshape(N_CHIPS, -1, x.shape[1])
    dbr = [None] * nd
    dbr[0] = _dil_bwd(qh[0], kh[0], va[0], do_dil[0], lse[0], delta[0], [parts[0]], [dw_out])
    out_part = _pair_sum(dw_out, dbr[0][4], core, "w_out")
    dbr[1] = _dil_bwd(qh[1], kh[1], va[1], do_dil[1], lse[1], delta[1], [parts[1], out_part], [])
    dbr[2] = _dil_bwd(qh[2], kh[2], va[2], do_dil[2], lse[2], delta[2], [parts[2]], [])
    ffn_halves = [_chip_sum(dbr[b][3], parts[b], chip, FFN_NAMES[b]) for b in range(nd)]
    w_out_half = _chip_sum(dbr[1][4], out_part, chip, "w_out")
    grad_x, dw_in, dattn_w, dqw, dkw = _attn_in_bwd(
        [b[0] for b in dbr], [b[1] for b in dbr], [b[2] for b in dbr], dqs, dks, dvs,
        qa, ka, qw, kw, cos_t, sin_t, seg_ones, h, w_in_g, x, dx1, attn_w)
    small = dict(attn=dattn_w, q=dqw[:, :HEAD_DIM], k=dkw[:, :HEAD_DIM], dil=ddil_w, sb=dsbn_w, ffn=dffn_w)
    return loss_parts, grad_x, small, dw_in, w_out_half, ffn_halves


HBM = pl.BlockSpec(memory_space=pltpu.HBM)
VMEM = pl.BlockSpec(memory_space=pltpu.VMEM)
CHIP_FLIPS = ((1, 0), (0, 1), (1, 1))


def _place():
    return lax.axis_index("x"), lax.axis_index("y"), lax.axis_index("c")


def _flip(v, d):
    return 1 - v if d else v


def _half_rows(c, n):
    return pl.ds(pl.multiple_of(c * (n // 2), 16), n // 2)


def _gather_plan(slot_in, slot_out, send, recv):
    x, y, c = _place()
    p = 2 * x + y
    chips = [(_flip(x, dx), _flip(y, dy)) for dx, dy in CHIP_FLIPS]
    mine, other = _half_rows(c, slot_in.shape[1]), _half_rows(1 - c, slot_in.shape[1])

    def copy(k, src, dst, to):
        return pltpu.make_async_remote_copy(src_ref=src, dst_ref=dst, send_sem=send.at[k], recv_sem=recv.at[k],
                                            device_id=to, device_id_type=MESH)

    def first(k):
        return copy(k, slot_in.at[p, mine], slot_out.at[p, mine], (*chips[k], c))

    def passed(k, rows):
        land = slot_out.at[2 * chips[k][0] + chips[k][1], rows]
        return copy(3 + k, land, land, (x, y, 1 - c))

    def start():
        for k in range(3):
            first(k).start()

    def forward():
        for k in range(3):
            land = slot_out.at[2 * chips[k][0] + chips[k][1], mine]
            copy(k, land, land, (*chips[k], c)).wait_recv()
            passed(k, mine).start()

    def finish():
        for k in range(3):
            passed(k, other).wait_recv()
        for k in range(3):
            first(k).wait_send()
            passed(k, mine).wait_send()

    return start, forward, finish


def _chip_send_plan(part_in, recv_out, send, recv):
    x, y, c = _place()
    p = 2 * x + y
    chips = [(_flip(x, dx), _flip(y, dy)) for dx, dy in CHIP_FLIPS]

    def copy(k):
        q = 2 * chips[k][0] + chips[k][1]
        return pltpu.make_async_remote_copy(src_ref=part_in.at[q], dst_ref=recv_out.at[p], send_sem=send.at[k],
                                            recv_sem=recv.at[k], device_id=(*chips[k], c), device_id_type=MESH)

    def start():
        for k in range(3):
            copy(k).start()

    def finish():
        for k in range(3):
            land = recv_out.at[2 * chips[k][0] + chips[k][1]]
            pltpu.make_async_remote_copy(src_ref=land, dst_ref=land, send_sem=send.at[k], recv_sem=recv.at[k],
                                         device_id=(*chips[k], c), device_id_type=MESH).wait_recv()
        for k in range(3):
            copy(k).wait_send()

    return start, finish


def _pair_send_plan(grad_in, recv_out, send, recv):
    x, y, c = _place()

    def copy():
        theirs = _half_rows(1 - c, grad_in.shape[1])
        return pltpu.make_async_remote_copy(src_ref=grad_in.at[:, theirs, :], dst_ref=recv_out, send_sem=send,
                                            recv_sem=recv, device_id=(x, y, 1 - c), device_id_type=MESH)

    return (lambda: copy().start()), (lambda: copy().wait())


def _own_slots(shard):
    here = 2 * lax.axis_index("x") + lax.axis_index("y")
    return lax.dynamic_update_slice(lax.empty((N_CHIPS,) + shard.shape, shard.dtype), shard[None], (here, 0, 0))


def _gather_weights(shards):
    n = len(shards)

    def body(*refs):
        ins, outs = refs[:n], refs[n:2 * n]
        send, recv = refs[2 * n:]
        plans = [_gather_plan(ins[a], outs[a], send.at[pl.ds(6 * a, 6)], recv.at[pl.ds(6 * a, 6)]) for a in range(n)]
        for stage in range(3):
            for plan in plans:
                plan[stage]()

    slots = [_own_slots(s) for s in shards]
    return pl.pallas_call(
        body, name="gather_weights", in_specs=[HBM] * n, out_specs=[HBM] * n,
        out_shape=[jax.ShapeDtypeStruct(s.shape, s.dtype) for s in slots],
        input_output_aliases={a: a for a in range(n)},
        scratch_shapes=[pltpu.SemaphoreType.DMA((6 * n,)), pltpu.SemaphoreType.DMA((6 * n,))],
    )(*slots)


def _pair_exchange(grads, small):
    n = len(grads)

    def body(*refs):
        gin, sm = refs[:n], refs[n]
        gout, sm_all = refs[n + 1:2 * n + 1], refs[2 * n + 1]
        send, recv = refs[2 * n + 2:]
        x, y, c = _place()
        me = 4 * x + 2 * y + c
        big = [_pair_send_plan(gin[a], gout[a], send.at[a], recv.at[a]) for a in range(n)]
        for start, _ in big:
            start()
        sm_all[pl.ds(me, 1)] = sm[...][None]
        tiny = []
        for k in range(1, N_DEV):
            px, py, pc = _flip(x, k & 4), _flip(y, k & 2), _flip(c, k & 1)
            tiny.append((pltpu.make_async_remote_copy(
                src_ref=sm, dst_ref=sm_all.at[me], send_sem=send.at[n + k - 1], recv_sem=recv.at[n + k - 1],
                device_id=(px, py, pc), device_id_type=MESH), 4 * px + 2 * py + pc))
            tiny[-1][0].start()
        for k, (cp, peer) in enumerate(tiny):
            pltpu.make_async_remote_copy(src_ref=sm, dst_ref=sm_all.at[peer], send_sem=send.at[n + k],
                                         recv_sem=recv.at[n + k], device_id=(x, y, c),
                                         device_id_type=MESH).wait_recv()
            cp.wait_send()
        for _, finish in big:
            finish()

    halves = [jax.ShapeDtypeStruct((g.shape[0], g.shape[1] // 2, g.shape[2]), g.dtype) for g in grads]
    return pl.pallas_call(
        body, name="pair_exchange", in_specs=[HBM] * n + [VMEM], out_specs=[HBM] * n + [VMEM],
        out_shape=halves + [jax.ShapeDtypeStruct((N_DEV,) + small.shape, small.dtype)],
        scratch_shapes=[pltpu.SemaphoreType.DMA((n + N_DEV - 1,)), pltpu.SemaphoreType.DMA((n + N_DEV - 1,))],
    )(*grads, small)


def _chip_exchange(parts):
    n = len(parts)

    def body(*refs):
        pin, pout = refs[:n], refs[n:2 * n]
        send, recv = refs[2 * n:]
        plans = [_chip_send_plan(pin[a], pout[a], send.at[pl.ds(3 * a, 3)], recv.at[pl.ds(3 * a, 3)]) for a in range(n)]
        for stage in range(2):
            for plan in plans:
                plan[stage]()

    return pl.pallas_call(
        body, name="chip_exchange", in_specs=[HBM] * n, out_specs=[HBM] * n,
        out_shape=[jax.ShapeDtypeStruct(s.shape, s.dtype) for s in parts],
        scratch_shapes=[pltpu.SemaphoreType.DMA((3 * n,)), pltpu.SemaphoreType.DMA((3 * n,))],
    )(*parts)


def _pair_swap_plan(hin, hout, send, recv):
    x, y, c = _place()

    def copies():
        return [pltpu.make_async_remote_copy(src_ref=hin[a], dst_ref=hout[a], send_sem=send.at[a], recv_sem=recv.at[a],
                                             device_id=(x, y, 1 - c), device_id_type=MESH) for a in range(len(hin))]

    def start():
        for cp in copies():
            cp.start()

    def finish():
        for cp in copies():
            cp.wait()

    return start, finish


def _pair_swap(halves):
    n = len(halves)

    def body(*refs):
        start, finish = _pair_swap_plan(refs[:n], refs[n:2 * n], *refs[2 * n:])
        start()
        finish()

    return pl.pallas_call(
        body, name="pair_swap", in_specs=[HBM] * n, out_specs=[HBM] * n,
        out_shape=[jax.ShapeDtypeStruct(s.shape, s.dtype) for s in halves],
        scratch_shapes=[pltpu.SemaphoreType.DMA((n,)), pltpu.SemaphoreType.DMA((n,))],
    )(*halves)


def _pair_sum(grad, recv, c, tag):
    _, R, C = grad.shape
    hr = R // 2

    def body(c_ref, a_ref, b_ref, o_ref):
        o_ref[...] = (a_ref[...] + b_ref[...]).astype(BF16)

    return pl.pallas_call(
        body, name="pair_sum_" + tag,
        grid_spec=pltpu.PrefetchScalarGridSpec(
            num_scalar_prefetch=1, grid=(N_CHIPS,),
            in_specs=[pl.BlockSpec((1, hr, C), lambda s, cr: (s, cr[0], 0)),
                      pl.BlockSpec((1, hr, C), lambda s, cr: (s, 0, 0))],
            out_specs=pl.BlockSpec((1, hr, C), lambda s, cr: (s, 0, 0))),
        out_shape=jax.ShapeDtypeStruct((N_CHIPS, hr, C), BF16),
    )(c, grad, recv)


def _chip_sum(received, own, chip, tag):
    _, rows, C = received.shape
    tr = rows // 2

    def body(chip_ref, own_ref, r1_ref, r2_ref, r3_ref, o_ref):
        p = [r[0].astype(F32) for r in (own_ref, r1_ref, r2_ref, r3_ref)]
        o_ref[...] = (p[0] + p[1]) + (p[2] + p[3])

    def slot(k):
        return pl.BlockSpec((1, tr, C), lambda i, cr: (jnp.bitwise_xor(cr[0], k), i, 0))

    return pl.pallas_call(
        body, name="chip_sum_" + tag,
        grid_spec=pltpu.PrefetchScalarGridSpec(
            num_scalar_prefetch=1, grid=(rows // tr,), in_specs=[slot(0), slot(1), slot(2), slot(3)],
            out_specs=pl.BlockSpec((tr, C), lambda i, cr: (i, 0))),
        out_shape=jax.ShapeDtypeStruct((rows, C), F32),
    )(chip, own, received, received, received)


def _adamw_math(w, g, m, v):
    m = ADAM_B1 * m + (1.0 - ADAM_B1) * g
    v = ADAM_B2 * v + (1.0 - ADAM_B2) * (g * g)
    m_hat = m / (1.0 - ADAM_B1 ** ADAM_STEP)
    v_hat = v / (1.0 - ADAM_B2 ** ADAM_STEP)
    delta = -ADAM_LR * (m_hat / (jnp.sqrt(v_hat) + ADAM_EPS) + ADAM_WD * w)
    return delta, m, v


def _adamw(w, g_mine, g_other, m, v, c, tag):
    R, C = w.shape

    def body(c_ref, w_ref, gm_ref, go_ref, m_ref, v_ref, g_ref, d_ref, nm_ref, nv_ref):
        g = jnp.where(pl.program_id(0) == c_ref[0], gm_ref[...], go_ref[...])
        g_ref[...] = g
        d_ref[...], nm_ref[...], nv_ref[...] = _adamw_math(w_ref[...], g, m_ref[...], v_ref[...])

    blk = pl.BlockSpec((R // 2, C), lambda h, cr: (h, 0))
    half = pl.BlockSpec((R // 2, C), lambda h, cr: (0, 0))
    return pl.pallas_call(
        body, name="adamw_" + tag,
        grid_spec=pltpu.PrefetchScalarGridSpec(
            num_scalar_prefetch=1, grid=(2,), in_specs=[blk, half, half, blk, blk], out_specs=[blk] * 4),
        out_shape=[jax.ShapeDtypeStruct((R, C), F32)] * 4,
    )(c, w, g_mine, g_other, m, v)


def _small_update(all_small, w, m, v):
    def body(a_ref, w_ref, m_ref, v_ref, g_ref, d_ref, nm_ref, nv_ref):
        g = ((a_ref[0] + a_ref[1]) + (a_ref[2] + a_ref[3])) + ((a_ref[4] + a_ref[5]) + (a_ref[6] + a_ref[7]))
        g_ref[...] = g
        d_ref[...], nm_ref[...], nv_ref[...] = _adamw_math(w_ref[...], g, m_ref[...], v_ref[...])

    return pl.pallas_call(
        body, name="small_update", out_shape=[jax.ShapeDtypeStruct(w.shape, F32)] * 4,
    )(all_small, w, m, v)


SMALL_ROWS = (("attn", 0, 0), ("ffn", 1, 0), ("dil", 2, 0), ("sb", 2, GROUP), ("q", 3, 0), ("k", 3, HEAD_DIM),
              ("loss", 4, 0))


def _pack_small(vals, D):
    rows = [jnp.zeros((1, D), F32) for _ in range(8)]
    for name, r, off in SMALL_ROWS:
        if name in vals:
            rows[r] = lax.dynamic_update_slice(rows[r], vals[name].astype(F32), (0, off))
    return jnp.concatenate(rows, axis=0)


def _unpack_small(packed, vals):
    return {name: packed[r:r + 1, off:off + vals[name].shape[1]] for name, r, off in SMALL_ROWS if name in vals}


def kernel(x, attn_norm_w, w_in, q_norm_w, k_norm_w, dil_out_norm_w, sb_out_norm_w, w_out, ffn_norm_w, w_gate, w_up, w_down, loss_target, m_attn_norm_w, m_w_in, m_q_norm_w, m_k_norm_w, m_dil_out_norm_w, m_sb_out_norm_w, m_w_out, m_ffn_norm_w, m_w_gate, m_w_up, m_w_down, v_attn_norm_w, v_w_in, v_q_norm_w, v_k_norm_w, v_dil_out_norm_w, v_sb_out_norm_w, v_w_out, v_ffn_norm_w, v_w_gate, v_w_up, v_w_down):
    D = x.shape[-1]
    big_names = ("w_in", "w_out", "w_gate", "w_up", "w_down")
    flipped = ("w_gate", "w_up")
    tr = lambda a: jnp.swapaxes(a[0], 0, 1)
    big_w = dict(w_in=w_in[0], w_out=w_out[0], w_gate=tr(w_gate), w_up=tr(w_up), w_down=w_down[0])
    big_m = dict(w_in=m_w_in[0], w_out=m_w_out[0], w_gate=tr(m_w_gate), w_up=tr(m_w_up), w_down=m_w_down[0])
    big_v = dict(w_in=v_w_in[0], w_out=v_w_out[0], w_gate=tr(v_w_gate), w_up=tr(v_w_up), w_down=v_w_down[0])
    small_w = dict(attn=attn_norm_w, q=q_norm_w, k=k_norm_w, dil=dil_out_norm_w, sb=sb_out_norm_w, ffn=ffn_norm_w)
    small_m = dict(attn=m_attn_norm_w, q=m_q_norm_w, k=m_k_norm_w, dil=m_dil_out_norm_w, sb=m_sb_out_norm_w,
                   ffn=m_ffn_norm_w)
    small_v = dict(attn=v_attn_norm_w, q=v_q_norm_w, k=v_k_norm_w, dil=v_dil_out_norm_w, sb=v_sb_out_norm_w,
                   ffn=v_ffn_norm_w)

    c = lax.axis_index("c").astype(jnp.int32).reshape(1)
    chip = (2 * lax.axis_index("x") + lax.axis_index("y")).astype(jnp.int32).reshape(1)
    (w_in_g,) = _gather_weights([big_w["w_in"].astype(BF16)])
    w_out_slots = _own_slots(big_w["w_out"].astype(BF16))
    ffn_slots = [_own_slots(big_w[n].astype(BF16)) for n in FFN_NAMES]

    loss_parts, grad_x, small_g, dw_in, w_out_half, ffn_halves = _device_step(
        x[0], loss_target[0], attn_norm_w, q_norm_w, k_norm_w, dil_out_norm_w, sb_out_norm_w, ffn_norm_w,
        w_in_g, w_out_slots, ffn_slots, c, chip)
    small_g["loss"] = (jnp.sum(loss_parts[:, 0, 0]) * (0.5 / D)).reshape(1, 1)

    from_pair, all_small = _pair_exchange([dw_in], _pack_small(small_g, D))
    in_part = _pair_sum(dw_in, from_pair, c, "w_in")
    (from_chips,) = _chip_exchange([in_part])
    halves = [_chip_sum(from_chips, in_part, chip, "w_in"), w_out_half] + ffn_halves
    others = _pair_swap(halves)
    big_out = {n: _adamw(big_w[n], mine, other, big_m[n], big_v[n], c, n)
               for n, mine, other in zip(big_names, halves, others)}
    sg, sd, sm, sv = _small_update(all_small, _pack_small(small_w, D), _pack_small(small_m, D),
                                   _pack_small(small_v, D))
    small_out = [_unpack_small(t, small_w) for t in (sg, sd, sm, sv)]

    order = (("attn", None), (None, "w_in"), ("q", None), ("k", None), ("dil", None), ("sb", None),
             (None, "w_out"), ("ffn", None), (None, "w_gate"), (None, "w_up"), (None, "w_down"))
    outs = [sg[4, 0], grad_x[None]]
    for kind in range(4):
        for s_name, b_name in order:
            if s_name is not None:
                outs.append(small_out[kind][s_name])
            else:
                res = big_out[b_name][kind]
                outs.append((jnp.swapaxes(res, 0, 1) if b_name in flipped else res)[None])
    return tuple(outs)
```
